```python
import jax, jax.numpy as jnp
from jax import lax
import numpy as np

D_MODEL = 2048
BATCH = 8
SEQ = 4096
DEPTH = 4

MIX_W = D_MODEL
ATT_W = MIX_W // 2
HEAD_DIM = 128
N_ATT_HEADS = ATT_W // HEAD_DIM
CONV_W = MIX_W // 4
CONV_TAPS = 3
POOL_W = MIX_W - ATT_W - CONV_W
POOL_WINDOWS = (2, 4, 8, 16)
POOL_GROUP = POOL_W // len(POOL_WINDOWS)
Q_BLOCK = 128
LN_EPS = 1e-5
DEEPNORM_ALPHA = (2 * DEPTH) ** 0.25
DEEPNORM_BETA = (8 * DEPTH) ** -0.25

IN_WIDTHS = (ATT_W, ATT_W, ATT_W, ATT_W, N_ATT_HEADS,
             CONV_W, CONV_W, CONV_W, CONV_W, POOL_W, POOL_W)
IN_W = sum(IN_WIDTHS)
IN_SPLITS = tuple(sum(IN_WIDTHS[:i + 1]) for i in range(len(IN_WIDTHS) - 1))

kernel_name = "hybrid_fox_shortconv_pool_deepnorm"


def layer_norm(x, g, b):
    x32 = x.astype(jnp.float32)
    mu = jnp.mean(x32, axis=-1, keepdims=True)
    var = jnp.mean(jnp.square(x32 - mu), axis=-1, keepdims=True)
    return ((x32 - mu) * lax.rsqrt(var + LN_EPS) * g + b).astype(x.dtype)


def forgetting_attention(q, k, v, fg_logit, b_f):
    b, s, h, dh = q.shape
    nb = s // Q_BLOCK
    log_f = jax.nn.log_sigmoid(fg_logit.astype(jnp.float32) + b_f.astype(jnp.float32))
    cum = jnp.cumsum(log_f, axis=1)
    cum_k = jnp.transpose(cum, (0, 2, 1))
    q_blocks = q.reshape(b, nb, Q_BLOCK, h, dh).transpose(1, 0, 2, 3, 4)
    c_blocks = cum.reshape(b, nb, Q_BLOCK, h).transpose(1, 0, 3, 2)
    k_pos = jnp.arange(s)
    scale = HEAD_DIM ** -0.5

    def block(args):
        qb, cb, i = args
        logits = jnp.einsum('bqhd,bkhd->bhqk', qb, k).astype(jnp.float32) * scale
        logits = logits + cb[..., None] - cum_k[:, :, None, :]
        q_pos = i * Q_BLOCK + jnp.arange(Q_BLOCK)
        causal = k_pos[None, :] <= q_pos[:, None]
        logits = jnp.where(causal, logits, -jnp.inf)
        p = jax.nn.softmax(logits, axis=-1).astype(v.dtype)
        return jnp.einsum('bhqk,bkhd->bqhd', p, v)

    out = lax.map(block, (q_blocks, c_blocks, jnp.arange(nb)))
    return out.transpose(1, 0, 2, 3, 4).reshape(b, s, h * dh)


def short_conv_mixer(gate_b, gate_c, h, conv_w):
    s = h.shape[1]
    u = gate_c * h
    up = jnp.pad(u, ((0, 0), (CONV_TAPS - 1, 0), (0, 0)))
    y = conv_w[0] * up[:, 0:s]
    for j in range(1, CONV_TAPS):
        y = y + conv_w[j] * up[:, j:j + s]
    return gate_b * y


def multiscale_pool_mixer(u, pool_w, pool_scale):
    s = u.shape[1]
    u32 = u.astype(jnp.float32)
    cs = jnp.pad(jnp.cumsum(u32, axis=1), ((0, 0), (1, 0), (0, 0)))
    t1 = jnp.arange(1, s + 1, dtype=jnp.float32)
    outs = []
    for g, w in enumerate(POOL_WINDOWS):
        sl = slice(g * POOL_GROUP, (g + 1) * POOL_GROUP)
        csg = cs[:, :, sl]
        lagged = jnp.pad(csg[:, :s - w + 1], ((0, 0), (w - 1, 0), (0, 0)))
        count = jnp.minimum(t1, float(w))
        mean = (csg[:, 1:] - lagged) / count[None, :, None]
        z = (mean - u32[:, :, sl]).astype(u.dtype)
        outs.append(jnp.einsum('bsc,cd->bsd', z, pool_w[g]))
    return jnp.concatenate(outs, axis=-1) * pool_scale


def _fwd_setup_inputs(seed: int = 0) -> dict:
    key = jax.random.key(seed)
    ks = jax.random.split(key, 10)
    x = jax.random.normal(ks[0], (BATCH, SEQ, D_MODEL), jnp.float32)
    w_in = jax.random.normal(ks[1], (DEPTH, D_MODEL, IN_W), jnp.float32) * D_MODEL ** -0.5
    b_f = jax.random.uniform(ks[2], (DEPTH, N_ATT_HEADS), jnp.float32, 1.0, 4.0)
    conv_w = jax.random.normal(ks[3], (DEPTH, CONV_TAPS, CONV_W), jnp.float32) * CONV_TAPS ** -0.5
    pool_w = jax.random.normal(ks[4], (DEPTH, len(POOL_WINDOWS), POOL_GROUP, POOL_GROUP),
                               jnp.float32) * POOL_GROUP ** -0.5
    pool_scale = 1.0 + 0.1 * jax.random.normal(ks[5], (DEPTH, POOL_W), jnp.float32)
    w_out = jax.random.normal(ks[6], (DEPTH, MIX_W, D_MODEL), jnp.float32) * (
        MIX_W ** -0.5 * DEEPNORM_BETA)
    ln_g = 1.0 + 0.1 * jax.random.normal(ks[7], (DEPTH, D_MODEL), jnp.float32)
    ln_b = 0.02 * jax.random.normal(ks[8], (DEPTH, D_MODEL), jnp.float32)
    return {"x": x, "w_in": w_in, "b_f": b_f, "conv_w": conv_w, "pool_w": pool_w,
            "pool_scale": pool_scale, "w_out": w_out, "ln_g": ln_g, "ln_b": ln_b}


def _fwd_reference(x, w_in, b_f, conv_w, pool_w, pool_scale, w_out, ln_g, ln_b):
    b, s, _ = x.shape
    for l in range(DEPTH):
        proj = jnp.einsum('bsd,de->bse', x, w_in[l])
        (q, k, v, g_att, fg, c_b, c_c, c_h, g_conv, p_u, g_pool) = jnp.split(
            proj, IN_SPLITS, axis=-1)
        heads = (b, s, N_ATT_HEADS, HEAD_DIM)
        y_att = forgetting_attention(q.reshape(heads), k.reshape(heads), v.reshape(heads),
                                     fg, b_f[l]) * jax.nn.silu(g_att)
        y_conv = short_conv_mixer(c_b, c_c, c_h, conv_w[l]) * jax.nn.silu(g_conv)
        y_pool = multiscale_pool_mixer(p_u, pool_w[l], pool_scale[l]) * jax.nn.silu(g_pool)
        y = jnp.concatenate([y_att, y_conv, y_pool], axis=-1)
        y = jnp.einsum('bse,ed->bsd', y, w_out[l])
        x = layer_norm(DEEPNORM_ALPHA * x + y, ln_g[l], ln_b[l])
    return x


import jax as _jax
import jax.numpy as _jnp

TWIN_FORMAT = 'train_step'
FWD_PARAMS = ['x', 'w_in', 'b_f', 'conv_w', 'pool_w', 'pool_scale', 'w_out', 'ln_g', 'ln_b']
TWIN_WEIGHTS = ['w_in', 'b_f', 'conv_w', 'pool_w', 'pool_scale', 'w_out', 'ln_g', 'ln_b']
TWIN_DIFF_INPUT = 'x'
TWIN_INPUTS = ['x', 'w_in', 'b_f', 'conv_w', 'pool_w', 'pool_scale', 'w_out', 'ln_g', 'ln_b', 'loss_target', 'm_w_in', 'm_b_f', 'm_conv_w', 'm_pool_w', 'm_pool_scale', 'm_w_out', 'm_ln_g', 'm_ln_b', 'v_w_in', 'v_b_f', 'v_conv_w', 'v_pool_w', 'v_pool_scale', 'v_w_out', 'v_ln_g', 'v_ln_b']
TWIN_OUTPUTS = ['loss', 'grad_x', 'grad_w_in', 'grad_b_f', 'grad_conv_w', 'grad_pool_w', 'grad_pool_scale', 'grad_w_out', 'grad_ln_g', 'grad_ln_b', 'delta_w_in', 'delta_b_f', 'delta_conv_w', 'delta_pool_w', 'delta_pool_scale', 'delta_w_out', 'delta_ln_g', 'delta_ln_b', 'new_m_w_in', 'new_m_b_f', 'new_m_conv_w', 'new_m_pool_w', 'new_m_pool_scale', 'new_m_w_out', 'new_m_ln_g', 'new_m_ln_b', 'new_v_w_in', 'new_v_b_f', 'new_v_conv_w', 'new_v_pool_w', 'new_v_pool_scale', 'new_v_w_out', 'new_v_ln_g', 'new_v_ln_b']
TWIN_LEAF_KINDS = {'loss': 'loss', 'grad_x': 'grad_x', 'grad_w_in': 'grad_w', 'grad_b_f': 'grad_w', 'grad_conv_w': 'grad_w', 'grad_pool_w': 'grad_w', 'grad_pool_scale': 'grad_w', 'grad_w_out': 'grad_w', 'grad_ln_g': 'grad_w', 'grad_ln_b': 'grad_w', 'delta_w_in': 'delta_w', 'delta_b_f': 'delta_w', 'delta_conv_w': 'delta_w', 'delta_pool_w': 'delta_w', 'delta_pool_scale': 'delta_w', 'delta_w_out': 'delta_w', 'delta_ln_g': 'delta_w', 'delta_ln_b': 'delta_w', 'new_m_w_in': 'new_m', 'new_m_b_f': 'new_m', 'new_m_conv_w': 'new_m', 'new_m_pool_w': 'new_m', 'new_m_pool_scale': 'new_m', 'new_m_w_out': 'new_m', 'new_m_ln_g': 'new_m', 'new_m_ln_b': 'new_m', 'new_v_w_in': 'new_v', 'new_v_b_f': 'new_v', 'new_v_conv_w': 'new_v', 'new_v_pool_w': 'new_v', 'new_v_pool_scale': 'new_v', 'new_v_w_out': 'new_v', 'new_v_ln_g': 'new_v', 'new_v_ln_b': 'new_v'}


def _forward(args):
    return _fwd_reference(*[args[k] for k in FWD_PARAMS])


def _output_shape():
    out = _jax.eval_shape(lambda: _forward(_fwd_setup_inputs(0)))
    return out.shape, out.dtype

N_MICROBATCH = 1
ADAM_LR = 0.001
ADAM_B1 = 0.9
ADAM_B2 = 0.999
ADAM_EPS = 1e-08
ADAM_WD = 0.01
ADAM_STEP = 10
PER_EXAMPLE_BATCH_AXIS = {'x': 0, 'loss_target': 0}
SHARED_INPUTS = []
_WEIGHT_DTYPES = {'w_in': _jnp.float32, 'b_f': _jnp.float32, 'conv_w': _jnp.float32, 'pool_w': _jnp.float32, 'pool_scale': _jnp.float32, 'w_out': _jnp.float32, 'ln_g': _jnp.float32, 'ln_b': _jnp.float32}
MOMENT_SCALE = {'w_in': 9.524399e-03, 'b_f': 4.154667e-02, 'conv_w': 1.407248e-02, 'pool_w': 1.218713e-02, 'pool_scale': 1.218430e-02, 'w_out': 2.363351e-02, 'ln_g': 8.953442e+00, 'ln_b': 3.245290e-01}


def _to_microbatches(a, axis):
    t = _jnp.moveaxis(a, axis, 0)
    t = t.reshape((N_MICROBATCH, t.shape[0] // N_MICROBATCH) + t.shape[1:])
    return _jnp.moveaxis(t, 1, axis + 1)


def setup_inputs(seed: int = 0) -> dict:
    inp = _fwd_setup_inputs(seed)
    key = _jax.random.fold_in(_jax.random.key(seed), 7919)
    shape, _ = _output_shape()
    out = dict(inp)
    out["loss_target"] = _jax.random.normal(_jax.random.fold_in(key, 0), shape, _jnp.float32)
    for i, name in enumerate(TWIN_WEIGHTS):
        w = inp[name].astype(_jnp.float32)
        if MOMENT_SCALE is None:
            s = _jnp.sqrt(_jnp.mean(_jnp.square(w)) + 1e-30)
        else:
            s = MOMENT_SCALE[name]
        km, kv = _jax.random.split(_jax.random.fold_in(key, i + 1))
        out[name] = w
        out["m_" + name] = s * _jax.random.normal(km, w.shape, _jnp.float32)
        out["v_" + name] = (s * s) * _jax.random.uniform(kv, w.shape, _jnp.float32, 0.5, 1.5)
    if N_MICROBATCH > 1:
        for name, axis in PER_EXAMPLE_BATCH_AXIS.items():
            out[name] = _to_microbatches(out[name], axis)
    return {'x': out['x'], 'w_in': out['w_in'], 'b_f': out['b_f'], 'conv_w': out['conv_w'], 'pool_w': out['pool_w'], 'pool_scale': out['pool_scale'], 'w_out': out['w_out'], 'ln_g': out['ln_g'], 'ln_b': out['ln_b'], 'loss_target': out['loss_target'], 'm_w_in': out['m_w_in'], 'm_b_f': out['m_b_f'], 'm_conv_w': out['m_conv_w'], 'm_pool_w': out['m_pool_w'], 'm_pool_scale': out['m_pool_scale'], 'm_w_out': out['m_w_out'], 'm_ln_g': out['m_ln_g'], 'm_ln_b': out['m_ln_b'], 'v_w_in': out['v_w_in'], 'v_b_f': out['v_b_f'], 'v_conv_w': out['v_conv_w'], 'v_pool_w': out['v_pool_w'], 'v_pool_scale': out['v_pool_scale'], 'v_w_out': out['v_w_out'], 'v_ln_g': out['v_ln_g'], 'v_ln_b': out['v_ln_b']}


def _loss(weights, diff, rest, loss_target):
    with _jax.named_scope("forward"):
        args = {**rest, TWIN_DIFF_INPUT: diff, **{k: w.astype(_WEIGHT_DTYPES[k]) for k, w in weights.items()}}
        y = _forward(args)
    with _jax.named_scope("loss_head"):
        err = _jnp.square(y.astype(_jnp.float32) - loss_target)
        return 0.5 * _jnp.sum(_jnp.mean(err, axis=-1)) if err.ndim else 0.5 * err


def _adamw(w, g, m, v):
    m = ADAM_B1 * m + (1.0 - ADAM_B1) * g
    v = ADAM_B2 * v + (1.0 - ADAM_B2) * _jnp.square(g)
    m_hat = m / (1.0 - ADAM_B1 ** ADAM_STEP)
    v_hat = v / (1.0 - ADAM_B2 ** ADAM_STEP)
    delta = -ADAM_LR * (m_hat / (_jnp.sqrt(v_hat) + ADAM_EPS) + ADAM_WD * w)
    return delta, m, v


def reference(x, w_in, b_f, conv_w, pool_w, pool_scale, w_out, ln_g, ln_b, loss_target, m_w_in, m_b_f, m_conv_w, m_pool_w, m_pool_scale, m_w_out, m_ln_g, m_ln_b, v_w_in, v_b_f, v_conv_w, v_pool_w, v_pool_scale, v_w_out, v_ln_g, v_ln_b):
    given = dict(x=x, w_in=w_in, b_f=b_f, conv_w=conv_w, pool_w=pool_w, pool_scale=pool_scale, w_out=w_out, ln_g=ln_g, ln_b=ln_b, loss_target=loss_target, m_w_in=m_w_in, m_b_f=m_b_f, m_conv_w=m_conv_w, m_pool_w=m_pool_w, m_pool_scale=m_pool_scale, m_w_out=m_w_out, m_ln_g=m_ln_g, m_ln_b=m_ln_b, v_w_in=v_w_in, v_b_f=v_b_f, v_conv_w=v_conv_w, v_pool_w=v_pool_w, v_pool_scale=v_pool_scale, v_w_out=v_w_out, v_ln_g=v_ln_g, v_ln_b=v_ln_b)
    weights = {n: given[n] for n in TWIN_WEIGHTS}
    shared = {n: given[n] for n in SHARED_INPUTS}
    per_example = {n: given[n] for n in ['x']}
    grad_fn = _jax.value_and_grad(_loss, argnums=(0, 1))

    def one_microbatch(ex, loss_target):
        ex = dict(ex)
        diff = ex.pop(TWIN_DIFF_INPUT)
        return grad_fn(weights, diff, {**shared, **ex}, loss_target)

    if N_MICROBATCH == 1:
        loss, (grad_w, grad_x) = one_microbatch(per_example, given["loss_target"])
    else:
        def body(carry, xs):
            loss_sum, grad_sum = carry
            l_k, (gw_k, gx_k) = one_microbatch(xs[0], xs[1])
            with _jax.named_scope("update"):
                return (loss_sum + l_k, _jax.tree.map(_jnp.add, grad_sum, gw_k)), gx_k

        init = (_jnp.zeros((), _jnp.float32), _jax.tree.map(_jnp.zeros_like, weights))
        (loss, grad_w), grad_x = _jax.lax.scan(body, init, (per_example, given["loss_target"]))
    with _jax.named_scope("update"):
        delta_w, new_m, new_v = {}, {}, {}
        for n in TWIN_WEIGHTS:
            delta_w[n], new_m[n], new_v[n] = _adamw(weights[n], grad_w[n], given["m_" + n], given["v_" + n])
    return (loss, grad_x, *[grad_w[n] for n in TWIN_WEIGHTS], *[delta_w[n] for n in TWIN_WEIGHTS],
            *[new_m[n] for n in TWIN_WEIGHTS], *[new_v[n] for n in TWIN_WEIGHTS])
```

```python
import jax
import jax.numpy as jnp
from jax import lax
from jax.experimental import pallas as pl
from jax.experimental.pallas import tpu as pltpu

F32 = jnp.float32
BF16 = jnp.bfloat16

N_DEV = 8
D_MODEL = 2048
N_HEADS = 8
HEAD_DIM = 128
ATT_W = N_HEADS * HEAD_DIM
CONV_W = 512
CONV_TAPS = 3
POOL_W = 512
POOL_GROUP = 128
N_POOL = POOL_W // POOL_GROUP
FG_AT = 4 * ATT_W
IN_W = FG_AT + N_HEADS + 4 * CONV_W + 2 * POOL_W
MAIN_W = IN_W - N_HEADS
LANES = 128
LN_EPS = 1e-5
DEPTH = 4
ALPHA = (2 * DEPTH) ** 0.25
SCALE = HEAD_DIM ** -0.5
NEG = -1e30

ADAM_LR, ADAM_B1, ADAM_B2, ADAM_EPS, ADAM_WD, ADAM_STEP = 0.001, 0.9, 0.999, 1e-08, 0.01, 10

QKV_W = 3 * ATT_W
REST_W = MAIN_W - QKV_W
QB, KB, VB = 0, 8, 16
GB = 0
CBB, CCB, CHB, CGB = 8, 12, 16, 20
PUB, PGB = 24, 28
VMEM_LIMIT = 48 * 1024 * 1024

MESH = pl.DeviceIdType.MESH
ANY = pl.BlockSpec(memory_space=pl.ANY)


def _params(semantics):
    return pltpu.CompilerParams(dimension_semantics=semantics, vmem_limit_bytes=VMEM_LIMIT)


def _sigmoid(x):
    return 1.0 / (1.0 + jnp.exp(-x))


def _mm(a, b, *, out_dtype, tm, tn, tk, name, c=None, c_scale=1.0, n_off=0, n_cols=None):
    M, K = a.shape
    N = b.shape[1] if n_cols is None else n_cols
    tm, tn, tk = min(tm, M), min(tn, N), min(tk, K)
    assert M % tm == 0 and N % tn == 0 and K % tk == 0 and n_off % tn == 0, (a.shape, b.shape, tm, tn, tk)
    nk, joff = K // tk, n_off // tn
    has_c = c is not None

    def body(*refs):
        a_ref, b_ref = refs[0], refs[1]
        c_ref = refs[2] if has_c else None
        o_ref = refs[2 + has_c]

        def finish(r):
            if has_c:
                r = r + c_scale * c_ref[...]
            o_ref[...] = r.astype(out_dtype)

        if nk == 1:
            finish(jnp.dot(a_ref[...], b_ref[...], preferred_element_type=F32))
        else:
            acc_ref = refs[3 + has_c]
            k = pl.program_id(2)

            @pl.when(k == 0)
            def _():
                acc_ref[...] = jnp.zeros_like(acc_ref)

            acc_ref[...] += jnp.dot(a_ref[...], b_ref[...], preferred_element_type=F32)

            @pl.when(k == nk - 1)
            def _():
                finish(acc_ref[...])

    in_specs = [pl.BlockSpec((tm, tk), lambda i, j, k: (i, k)),
                pl.BlockSpec((tk, tn), lambda i, j, k: (k, j + joff))]
    args = [a, b]
    if has_c:
        in_specs.append(pl.BlockSpec((tm, tn), lambda i, j, k: (i, j)))
        args.append(c)
    return pl.pallas_call(
        body, name=name, grid=(M // tm, N // tn, nk),
        in_specs=in_specs,
        out_specs=pl.BlockSpec((tm, tn), lambda i, j, k: (i, j)),
        out_shape=jax.ShapeDtypeStruct((M, N), out_dtype),
        scratch_shapes=[pltpu.VMEM((tm, tn), F32)] if nk > 1 else [],
        compiler_params=_params(("parallel", "parallel", "arbitrary")),
    )(*args)


def _shift_down(x, k, row):
    return jnp.where(row >= k, pltpu.roll(x, k, 0), 0.0)


def _shift_up(x, k, row):
    s = x.shape[0]
    return jnp.where(row < s - k, pltpu.roll(x, s - k, 0), 0.0)


def _fg_fwd(fgp, bf):
    S = fgp.shape[0]

    def body(f_ref, b_ref, cum_ref):
        a = f_ref[...] + b_ref[...]
        x = jnp.minimum(a, 0.0) - jnp.log(1.0 + jnp.exp(-jnp.abs(a)))
        row = lax.broadcasted_iota(jnp.int32, x.shape, 0)
        k = 1
        while k < S:
            x = x + _shift_down(x, k, row)
            k *= 2
        cum_ref[...] = x

    return pl.pallas_call(body, name="fg_fwd", out_shape=jax.ShapeDtypeStruct((S, LANES), F32),
                          compiler_params=_params(None))(fgp, bf)


def _fg_bwd(dcum, fgp, bf):
    S = fgp.shape[0]

    def body(d_ref, f_ref, b_ref, dfg_ref, dbf_ref):
        x = d_ref[...]
        row = lax.broadcasted_iota(jnp.int32, x.shape, 0)
        k = 1
        while k < S:
            x = x + _shift_up(x, k, row)
            k *= 2
        dfg = x * _sigmoid(-(f_ref[...] + b_ref[...]))
        dfg_ref[...] = dfg.astype(BF16)
        dbf_ref[...] = jnp.sum(dfg, axis=0, keepdims=True)

    return pl.pallas_call(
        body, name="fg_bwd",
        out_shape=(jax.ShapeDtypeStruct((S, LANES), BF16), jax.ShapeDtypeStruct((1, LANES), F32)),
        compiler_params=_params(None),
    )(dcum, fgp, bf)


def _colblock(S, off):
    return pl.BlockSpec((S, LANES), lambda h: (0, h + off))


def _head_col(S):
    return pl.BlockSpec((None, S, 1), lambda h: (h, 0, 0))


def _head_row(nq, tq):
    return pl.BlockSpec((None, nq, 1, tq), lambda h: (h, 0, 0, 0))


def _lane_of(ref, h):
    lane = lax.broadcasted_iota(jnp.int32, ref.shape, 1)
    return jnp.sum(jnp.where(lane == h, ref[...], 0.0), axis=1, keepdims=True)


def _attn_fwd(qkv, rest, cum, cumr, *, tq):
    S = qkv.shape[0]
    nq = S // tq

    def body(q_ref, k_ref, v_ref, g_ref, cum_ref, cumr_ref, o_ref, y_ref, lse_ref, cc):
        cc[...] = _lane_of(cum_ref, pl.program_id(0))
        tri = (lax.broadcasted_iota(jnp.int32, (tq, tq), 1) <= lax.broadcasted_iota(jnp.int32, (tq, tq), 0))

        def q_step(qi, _):
            rows = pl.ds(pl.multiple_of(qi * tq, tq), tq)
            q = q_ref[rows, :]
            ci = cc[rows, :]

            def tile(kj, carry, masked):
                m, l, acc = carry
                cols = pl.ds(pl.multiple_of(kj * tq, tq), tq)
                s = lax.dot_general(q, k_ref[cols, :], (((1,), (1,)), ((), ())), preferred_element_type=F32) * SCALE
                s = s + ci - cumr_ref[kj]
                if masked:
                    s = jnp.where(tri, s, NEG)
                m_new = jnp.maximum(m, jnp.max(s, axis=1, keepdims=True))
                a = jnp.exp(m - m_new)
                p = jnp.exp(s - m_new)
                l = a * l + jnp.sum(p, axis=1, keepdims=True)
                acc = a * acc + jnp.dot(p.astype(BF16), v_ref[cols, :], preferred_element_type=F32)
                return m_new, l, acc

            init = (jnp.full((tq, 1), NEG, F32), jnp.zeros((tq, 1), F32), jnp.zeros((tq, HEAD_DIM), F32))
            carry = lax.fori_loop(0, qi, lambda kj, c: tile(kj, c, False), init)
            m, l, acc = tile(qi, carry, True)
            o = acc / l
            g = g_ref[rows, :]
            o_ref[rows, :] = o
            y_ref[rows, :] = (o * (g * _sigmoid(g))).astype(BF16)
            lse_ref[rows, :] = m + jnp.log(l)
            return 0

        lax.fori_loop(0, nq, q_step, 0)

    return pl.pallas_call(
        body, name="attn_fwd", grid=(N_HEADS,),
        in_specs=[_colblock(S, QB), _colblock(S, KB), _colblock(S, VB), _colblock(S, GB),
                  pl.BlockSpec((S, LANES), lambda h: (0, 0)), _head_row(nq, tq)],
        out_specs=(_colblock(S, 0), _colblock(S, 0), _head_col(S)),
        out_shape=(jax.ShapeDtypeStruct((S, ATT_W), F32), jax.ShapeDtypeStruct((S, ATT_W), BF16),
                   jax.ShapeDtypeStruct((N_HEADS, S, 1), F32)),
        scratch_shapes=[pltpu.VMEM((S, 1), F32)],
        compiler_params=_params(("arbitrary",)),
    )(qkv, qkv, qkv, rest, cum, cumr)


def _attn_bwd_pre(rest, o, dy, cum, lse):
    S = o.shape[0]

    def body(g_ref, o_ref, dy_ref, cum_ref, lse_ref, do_ref, dg_ref, a_ref, dl_ref):
        g = g_ref[...]
        sg = _sigmoid(g)
        ov = o_ref[...]
        dy = dy_ref[...]
        do = dy * (g * sg)
        do_ref[...] = do.astype(BF16)
        dg_ref[...] = (dy * ov * (sg * (1.0 + g * (1.0 - sg)))).astype(BF16)
        dl_ref[...] = jnp.sum(do * ov, axis=1, keepdims=True)
        a_ref[...] = _lane_of(cum_ref, pl.program_id(0)) - lse_ref[...]

    att = jax.ShapeDtypeStruct((S, ATT_W), BF16)
    col = jax.ShapeDtypeStruct((N_HEADS, S, 1), F32)
    return pl.pallas_call(
        body, name="attn_bwd_pre", grid=(N_HEADS,),
        in_specs=[_colblock(S, GB), _colblock(S, 0), _colblock(S, 0),
                  pl.BlockSpec((S, LANES), lambda h: (0, 0)), _head_col(S)],
        out_specs=(_colblock(S, 0), _colblock(S, 0), _head_col(S), _head_col(S)),
        out_shape=(att, att, col, col),
        compiler_params=_params(("parallel",)),
    )(rest, o, dy, cum, lse)


def _attn_bwd(qkv, do, a, delta, cumr, *, tq):
    S = qkv.shape[0]
    nq = S // tq
    tdot = (((0,), (0,)), ((), ()))

    def body(q_ref, k_ref, v_ref, do_ref, a_ref, dl_ref, cumr_ref, dq_ref, dk_ref, dv_ref, dcr_ref, dcc_ref, dqa):
        dqa[...] = jnp.zeros_like(dqa)
        dcr_ref[...] = jnp.zeros_like(dcr_ref)
        tri = (lax.broadcasted_iota(jnp.int32, (tq, tq), 1) <= lax.broadcasted_iota(jnp.int32, (tq, tq), 0))

        def k_step(kj, _):
            cols = pl.ds(pl.multiple_of(kj * tq, tq), tq)
            k = k_ref[cols, :]
            v = v_ref[cols, :]
            cj = cumr_ref[kj]

            def tile(qi, carry, masked):
                dk, dv, csum = carry
                rows = pl.ds(pl.multiple_of(qi * tq, tq), tq)
                q = q_ref[rows, :]
                dot = do_ref[rows, :]
                s = lax.dot_general(q, k, (((1,), (1,)), ((), ())), preferred_element_type=F32) * SCALE
                p = jnp.exp(s + a_ref[rows, :] - cj)
                if masked:
                    p = jnp.where(tri, p, 0.0)
                dv = dv + lax.dot_general(p.astype(BF16), dot, tdot, preferred_element_type=F32)
                dp = lax.dot_general(dot, v, (((1,), (1,)), ((), ())), preferred_element_type=F32)
                ds = p * (dp - dl_ref[rows, :])
                dcr_ref[rows, :] += jnp.sum(ds, axis=1, keepdims=True)
                csum = csum + jnp.sum(ds, axis=0, keepdims=True)
                dsb = (ds * SCALE).astype(BF16)
                dqa[rows, :] += jnp.dot(dsb, k, preferred_element_type=F32)
                dk = dk + lax.dot_general(dsb, q, tdot, preferred_element_type=F32)
                return dk, dv, csum

            init = (jnp.zeros((tq, HEAD_DIM), F32), jnp.zeros((tq, HEAD_DIM), F32), jnp.zeros((1, tq), F32))
            carry = tile(kj, init, True)
            dk, dv, csum = lax.fori_loop(kj + 1, nq, lambda qi, c: tile(qi, c, False), carry)
            dk_ref[cols, :] = dk.astype(BF16)
            dv_ref[cols, :] = dv.astype(BF16)
            dcc_ref[kj] = csum
            return 0

        lax.fori_loop(0, nq, k_step, 0)
        dq_ref[...] = dqa[...].astype(BF16)

    att = jax.ShapeDtypeStruct((S, ATT_W), BF16)
    return pl.pallas_call(
        body, name="attn_bwd", grid=(N_HEADS,),
        in_specs=[_colblock(S, QB), _colblock(S, KB), _colblock(S, VB), _colblock(S, 0),
                  _head_col(S), _head_col(S), _head_row(nq, tq)],
        out_specs=(_colblock(S, 0), _colblock(S, 0), _colblock(S, 0), _head_col(S), _head_row(nq, tq)),
        out_shape=(att, att, att, jax.ShapeDtypeStruct((N_HEADS, S, 1), F32),
                   jax.ShapeDtypeStruct((N_HEADS, nq, 1, tq), F32)),
        scratch_shapes=[pltpu.VMEM((S, HEAD_DIM), F32)],
        compiler_params=_params(("arbitrary",)),
    )(qkv, qkv, qkv, do, a, delta, cumr)


def _conv_taps(u, w_ref, row):
    return (w_ref[0:1, :] * _shift_down(u, 2, row) + w_ref[1:2, :] * _shift_down(u, 1, row)) + w_ref[2:3, :] * u


def _conv_fwd(rest, cw):
    S = rest.shape[0]

    def body(cb_ref, cc_ref, ch_ref, g_ref, w_ref, y_ref):
        row = lax.broadcasted_iota(jnp.int32, (S, LANES), 0)
        g = g_ref[...]
        y = _conv_taps(cc_ref[...] * ch_ref[...], w_ref, row)
        y_ref[...] = ((cb_ref[...] * y) * (g * _sigmoid(g))).astype(BF16)

    return pl.pallas_call(
        body, name="conv_fwd", grid=(CONV_W // LANES,),
        in_specs=[_colblock(S, CBB), _colblock(S, CCB), _colblock(S, CHB), _colblock(S, CGB),
                  pl.BlockSpec((CONV_TAPS, LANES), lambda j: (0, j))],
        out_specs=_colblock(S, 0),
        out_shape=jax.ShapeDtypeStruct((S, CONV_W), BF16),
        compiler_params=_params(("parallel",)),
    )(rest, rest, rest, rest, cw)


def _conv_bwd(rest, dy, cw):
    S = rest.shape[0]

    def body(cb_ref, cc_ref, ch_ref, g_ref, dy_ref, w_ref, dcb_ref, dcc_ref, dch_ref, dg_ref, dw_ref):
        row = lax.broadcasted_iota(jnp.int32, (S, LANES), 0)
        g = g_ref[...]
        sg = _sigmoid(g)
        silu = g * sg
        cb, cc, ch, dy = cb_ref[...], cc_ref[...], ch_ref[...], dy_ref[...]
        u = cc * ch
        u1 = _shift_down(u, 1, row)
        u2 = _shift_down(u, 2, row)
        y = (w_ref[0:1, :] * u2 + w_ref[1:2, :] * u1) + w_ref[2:3, :] * u
        dcb_ref[...] = (dy * silu * y).astype(BF16)
        dg_ref[...] = (dy * (cb * y) * (sg * (1.0 + g * (1.0 - sg)))).astype(BF16)
        dyv = dy * silu * cb
        du = w_ref[2:3, :] * dyv + w_ref[1:2, :] * _shift_up(dyv, 1, row) + w_ref[0:1, :] * _shift_up(dyv, 2, row)
        dcc_ref[...] = (du * ch).astype(BF16)
        dch_ref[...] = (du * cc).astype(BF16)
        dw_ref[0:1, :] = jnp.sum(dyv * u2, axis=0, keepdims=True)
        dw_ref[1:2, :] = jnp.sum(dyv * u1, axis=0, keepdims=True)
        dw_ref[2:3, :] = jnp.sum(dyv * u, axis=0, keepdims=True)

    act = jax.ShapeDtypeStruct((S, CONV_W), BF16)
    return pl.pallas_call(
        body, name="conv_bwd", grid=(CONV_W // LANES,),
        in_specs=[_colblock(S, CBB), _colblock(S, CCB), _colblock(S, CHB), _colblock(S, CGB),
                  _colblock(S, ATT_W // LANES), pl.BlockSpec((CONV_TAPS, LANES), lambda j: (0, j))],
        out_specs=(_colblock(S, 0),) * 4 + (pl.BlockSpec((CONV_TAPS, LANES), lambda j: (0, j)),),
        out_shape=(act, act, act, act, jax.ShapeDtypeStruct((CONV_TAPS, CONV_W), F32)),
        compiler_params=_params(("parallel",)),
    )(rest, rest, rest, rest, dy, cw)


def _pool_z(u, grp, row):
    s2 = u + _shift_down(u, 1, row)
    s4 = s2 + _shift_down(s2, 2, row)
    s8 = s4 + _shift_down(s4, 4, row)
    s16 = s8 + _shift_down(s8, 8, row)
    sw = jnp.where(grp == 0, s2, jnp.where(grp == 1, s4, jnp.where(grp == 2, s8, s16)))
    return sw / _pool_count(grp, row) - u


def _pool_count(grp, row):
    return jnp.minimum(row + 1, 2 << grp).astype(F32)


def _pool_fwd(rest, pw, scale):
    S = rest.shape[0]

    def body(u_ref, g_ref, pw_ref, sc_ref, y_ref):
        grp = pl.program_id(0)
        row = lax.broadcasted_iota(jnp.int32, (S, LANES), 0)
        z = _pool_z(u_ref[...], grp, row)
        zp = jnp.dot(z.astype(BF16), pw_ref[...].astype(BF16), preferred_element_type=F32)
        g = g_ref[...]
        y_ref[...] = ((zp * sc_ref[...]) * (g * _sigmoid(g))).astype(BF16)

    return pl.pallas_call(
        body, name="pool_fwd", grid=(N_POOL,),
        in_specs=[_colblock(S, PUB), _colblock(S, PGB),
                  pl.BlockSpec((None, POOL_GROUP, POOL_GROUP), lambda j: (j, 0, 0)),
                  pl.BlockSpec((1, LANES), lambda j: (0, j))],
        out_specs=_colblock(S, 0),
        out_shape=jax.ShapeDtypeStruct((S, POOL_W), BF16),
        compiler_params=_params(("parallel",)),
    )(rest, rest, pw, scale)


def _pool_bwd(rest, dy, pw, scale):
    S = rest.shape[0]

    def body(u_ref, g_ref, dy_ref, pw_ref, sc_ref, du_ref, dg_ref, dpw_ref, dsc_ref):
        grp = pl.program_id(0)
        row = lax.broadcasted_iota(jnp.int32, (S, LANES), 0)
        z = _pool_z(u_ref[...], grp, row).astype(BF16)
        pwb = pw_ref[...].astype(BF16)
        zp = jnp.dot(z, pwb, preferred_element_type=F32)
        g = g_ref[...]
        sg = _sigmoid(g)
        silu = g * sg
        dy = dy_ref[...]
        sc = sc_ref[...]
        dsc_ref[...] = jnp.sum(dy * silu * zp, axis=0, keepdims=True)
        dg_ref[...] = (dy * (zp * sc) * (sg * (1.0 + g * (1.0 - sg)))).astype(BF16)
        dzp = (dy * silu * sc).astype(BF16)
        dpw_ref[...] = lax.dot_general(z, dzp, (((0,), (0,)), ((), ())), preferred_element_type=F32)
        dz = lax.dot_general(dzp, pwb, (((1,), (1,)), ((), ())), preferred_element_type=F32)
        f1 = dz / _pool_count(grp, row)
        f2 = f1 + _shift_up(f1, 1, row)
        f4 = f2 + _shift_up(f2, 2, row)
        f8 = f4 + _shift_up(f4, 4, row)
        f16 = f8 + _shift_up(f8, 8, row)
        fw = jnp.where(grp == 0, f2, jnp.where(grp == 1, f4, jnp.where(grp == 2, f8, f16)))
        du_ref[...] = (fw - dz).astype(BF16)

    act = jax.ShapeDtypeStruct((S, POOL_W), BF16)
    return pl.pallas_call(
        body, name="pool_bwd", grid=(N_POOL,),
        in_specs=[_colblock(S, PUB), _colblock(S, PGB), _colblock(S, (ATT_W + CONV_W) // LANES),
                  pl.BlockSpec((None, POOL_GROUP, POOL_GROUP), lambda j: (j, 0, 0)),
                  pl.BlockSpec((1, LANES), lambda j: (0, j))],
        out_specs=(_colblock(S, 0), _colblock(S, 0),
                   pl.BlockSpec((None, POOL_GROUP, POOL_GROUP), lambda j: (j, 0, 0)),
                   pl.BlockSpec((1, LANES), lambda j: (0, j))),
        out_shape=(act, act, jax.ShapeDtypeStruct((N_POOL, POOL_GROUP, POOL_GROUP), F32),
                   jax.ShapeDtypeStruct((1, POOL_W), F32)),
        compiler_params=_params(("parallel",)),
    )(rest, rest, dy, pw, scale)


def _ln_fwd(z, g, b, *, ts):
    S, D = z.shape

    def body(z_ref, g_ref, b_ref, o_ref, ob_ref, xh_ref, rs_ref):
        zz = z_ref[...]
        mu = jnp.mean(zz, axis=1, keepdims=True)
        zc = zz - mu
        rstd = lax.rsqrt(jnp.mean(zc * zc, axis=1, keepdims=True) + LN_EPS)
        xh = zc * rstd
        out = xh * g_ref[...] + b_ref[...]
        o_ref[...] = out
        ob_ref[...] = out.astype(BF16)
        xh_ref[...] = xh
        rs_ref[...] = rstd

    rowblk = pl.BlockSpec((ts, D), lambda i: (i, 0))
    vec = pl.BlockSpec((1, D), lambda i: (0, 0))
    return pl.pallas_call(
        body, name="ln_fwd", grid=(S // ts,),
        in_specs=[rowblk, vec, vec],
        out_specs=(rowblk, rowblk, rowblk, pl.BlockSpec((ts, 1), lambda i: (i, 0))),
        out_shape=(jax.ShapeDtypeStruct((S, D), F32), jax.ShapeDtypeStruct((S, D), BF16),
                   jax.ShapeDtypeStruct((S, D), F32), jax.ShapeDtypeStruct((S, 1), F32)),
        compiler_params=_params(("parallel",)),
    )(z, g, b)


def _ln_bwd(dout, xhat, rstd, g, *, ts):
    S, D = dout.shape

    def body(d_ref, xh_ref, rs_ref, g_ref, dz_ref, dzb_ref, dg_ref, db_ref):
        @pl.when(pl.program_id(0) == 0)
        def _():
            dg_ref[...] = jnp.zeros_like(dg_ref)
            db_ref[...] = jnp.zeros_like(db_ref)

        d = d_ref[...]
        xh = xh_ref[...]
        dxh = d * g_ref[...]
        m1 = jnp.mean(dxh, axis=1, keepdims=True)
        m2 = jnp.mean(dxh * xh, axis=1, keepdims=True)
        dz = rs_ref[...] * (dxh - m1 - xh * m2)
        dz_ref[...] = dz
        dzb_ref[...] = dz.astype(BF16)
        dg_ref[...] += jnp.sum(d * xh, axis=0, keepdims=True)
        db_ref[...] += jnp.sum(d, axis=0, keepdims=True)

    rowblk = pl.BlockSpec((ts, D), lambda i: (i, 0))
    vec = pl.BlockSpec((1, D), lambda i: (0, 0))
    return pl.pallas_call(
        body, name="ln_bwd", grid=(S // ts,),
        in_specs=[rowblk, rowblk, pl.BlockSpec((ts, 1), lambda i: (i, 0)), vec],
        out_specs=(rowblk, rowblk, vec, vec),
        out_shape=(jax.ShapeDtypeStruct((S, D), F32), jax.ShapeDtypeStruct((S, D), BF16),
                   jax.ShapeDtypeStruct((1, D), F32), jax.ShapeDtypeStruct((1, D), F32)),
        compiler_params=_params(("arbitrary",)),
    )(dout, xhat, rstd, g)


def _loss_head(y, tgt, *, ts):
    S, D = y.shape

    def body(y_ref, t_ref, l_ref, dy_ref):
        @pl.when(pl.program_id(0) == 0)
        def _():
            l_ref[...] = jnp.zeros_like(l_ref)

        e = y_ref[...] - t_ref[...]
        dy_ref[...] = e * (1.0 / D)
        rowloss = jnp.sum(e * e, axis=1, keepdims=True) * (0.5 / D)
        l_ref[...] += jnp.sum(rowloss, axis=0, keepdims=True)

    rowblk = pl.BlockSpec((ts, D), lambda i: (i, 0))
    return pl.pallas_call(
        body, name="loss_head", grid=(S // ts,),
        in_specs=[rowblk, rowblk],
        out_specs=(pl.BlockSpec((1, 1), lambda i: (0, 0)), rowblk),
        out_shape=(jax.ShapeDtypeStruct((1, 1), F32), jax.ShapeDtypeStruct((S, D), F32)),
        compiler_params=_params(("arbitrary",)),
    )(y, tgt)


def _adamw(parts, w, m, v, *, tr, name):
    R, C = w.shape
    tr = min(tr, R)
    assert R % tr == 0
    c1 = 1.0 - ADAM_B1 ** ADAM_STEP
    c2 = 1.0 - ADAM_B2 ** ADAM_STEP

    def body(p_ref, w_ref, m_ref, v_ref, g_ref, d_ref, nm_ref, nv_ref):
        g = p_ref[0].astype(F32)
        for j in range(1, N_DEV):
            g = g + p_ref[j].astype(F32)
        nm = ADAM_B1 * m_ref[...] + (1.0 - ADAM_B1) * g
        nv = ADAM_B2 * v_ref[...] + (1.0 - ADAM_B2) * (g * g)
        g_ref[...] = g
        nm_ref[...] = nm
        nv_ref[...] = nv
        d_ref[...] = -ADAM_LR * ((nm / c1) / (jnp.sqrt(nv / c2) + ADAM_EPS) + ADAM_WD * w_ref[...])

    blk = pl.BlockSpec((tr, C), lambda i: (i, 0))
    out = jax.ShapeDtypeStruct((R, C), F32)
    return pl.pallas_call(
        body, name=name, grid=(R // tr,),
        in_specs=[pl.BlockSpec((N_DEV, tr, C), lambda i: (0, i, 0)), blk, blk, blk],
        out_specs=(blk, blk, blk, blk),
        out_shape=(out, out, out, out),
        compiler_params=_params(("parallel",)),
    )(parts, w, m, v)


def _my_place():
    return lax.axis_index("x"), lax.axis_index("y"), lax.axis_index("c")


def _all_gather(blocks, *, name):
    n = len(blocks)

    def body(*refs):
        ins, outs = refs[:n], refs[n:2 * n]
        send_sems, recv_sems, local_sems = refs[2 * n:]
        x, y, c = _my_place()
        me, sibling = (x, y, c), (x, y, 1 - c)
        chips = [(1 - x, y), (x, 1 - y), (1 - x, 1 - y)]

        def slot(a, place):
            px, py, pc = place
            return outs[a].at[4 * px + 2 * py + pc]

        def copy(a, k, block, to, src=None):
            return pltpu.make_async_remote_copy(
                src_ref=slot(a, block) if src is None else src, dst_ref=slot(a, block),
                send_sem=send_sems.at[a, k], recv_sem=recv_sems.at[a, k], device_id=to, device_id_type=MESH)

        mine = [pltpu.make_async_copy(ins[a], slot(a, me), local_sems.at[a]) for a in range(n)]
        for cp in mine:
            cp.start()
        first = []
        for a in range(n):
            first.append(copy(a, 0, me, sibling, src=ins[a]))
            first += [copy(a, 1 + j, me, (*chip, c), src=ins[a]) for j, chip in enumerate(chips)]
        for cp in first:
            cp.start()
        passed = []
        for j, chip in enumerate(chips):
            for a in range(n):
                copy(a, 1 + j, (*chip, c), me).wait_recv()
                fwd = copy(a, 4 + j, (*chip, c), sibling)
                fwd.start()
                passed.append(fwd)
        for a in range(n):
            copy(a, 0, sibling, me).wait_recv()
            for j, chip in enumerate(chips):
                copy(a, 4 + j, (*chip, 1 - c), me).wait_recv()
        for cp in first + passed:
            cp.wait_send()
        for cp in mine:
            cp.wait()

    return pl.pallas_call(
        body, name=name,
        in_specs=[ANY] * n, out_specs=[ANY] * n,
        out_shape=[jax.ShapeDtypeStruct((N_DEV,) + b.shape, b.dtype) for b in blocks],
        scratch_shapes=[pltpu.SemaphoreType.DMA((n, 7)), pltpu.SemaphoreType.DMA((n, 7)),
                        pltpu.SemaphoreType.DMA((n,))],
        compiler_params=pltpu.CompilerParams(has_side_effects=True),
    )(*blocks)


def _all_to_all(parts, *, name):
    n = len(parts)

    def body(*refs):
        ins, outs = refs[:n], refs[n:2 * n]
        send_sems, recv_sems, local_sems = refs[2 * n:]
        x, y, c = _my_place()
        me = 4 * x + 2 * y + c
        mine = [pltpu.make_async_copy(ins[a].at[me], outs[a].at[me], local_sems.at[a]) for a in range(n)]
        for cp in mine:
            cp.start()
        copies = []
        for k in range(1, N_DEV):
            px = x ^ ((k >> 2) & 1)
            py = y ^ ((k >> 1) & 1)
            pc = c ^ (k & 1)
            peer = 4 * px + 2 * py + pc
            for a in range(n):
                copies.append(pltpu.make_async_remote_copy(
                    src_ref=ins[a].at[peer], dst_ref=outs[a].at[me],
                    send_sem=send_sems.at[a, k - 1], recv_sem=recv_sems.at[a, k - 1],
                    device_id=(px, py, pc), device_id_type=MESH))
                copies[-1].start()
        i = 0
        for k in range(1, N_DEV):
            px = x ^ ((k >> 2) & 1)
            py = y ^ ((k >> 1) & 1)
            pc = c ^ (k & 1)
            peer = 4 * px + 2 * py + pc
            for a in range(n):
                pltpu.make_async_remote_copy(
                    src_ref=ins[a].at[peer], dst_ref=outs[a].at[peer],
                    send_sem=send_sems.at[a, k - 1], recv_sem=recv_sems.at[a, k - 1],
                    device_id=(px, py, pc), device_id_type=MESH).wait_recv()
                copies[i].wait_send()
                i += 1
        for cp in mine:
            cp.wait()

    return pl.pallas_call(
        body, name=name,
        in_specs=[ANY] * n, out_specs=[ANY] * n,
        out_shape=[jax.ShapeDtypeStruct(p.shape, p.dtype) for p in parts],
        scratch_shapes=[pltpu.SemaphoreType.DMA((n, 7)), pltpu.SemaphoreType.DMA((n, 7)),
                        pltpu.SemaphoreType.DMA((n,))],
        compiler_params=pltpu.CompilerParams(has_side_effects=True),
    )(*parts)


TQ = 256
TS = 256


def _layer_fwd(x, xb, wl):
    qkv = _mm(xb, wl["w_main"], out_dtype=BF16, tm=1024, tn=512, tk=D_MODEL, name="mm_qkv", n_cols=QKV_W)
    rest = _mm(xb, wl["w_main"], out_dtype=F32, tm=1024, tn=512, tk=D_MODEL, name="mm_rest", n_off=QKV_W, n_cols=REST_W)
    fgp = _mm(xb, wl["w_fg"], out_dtype=F32, tm=1024, tn=LANES, tk=D_MODEL, name="mm_fg")
    cum = _fg_fwd(fgp, wl["b_f"])
    S = x.shape[0]
    cumr = cum[:, :N_HEADS].T.reshape(N_HEADS, S // TQ, 1, TQ)
    o, y_att, lse = _attn_fwd(qkv, rest, cum, cumr, tq=TQ)
    y_conv = _conv_fwd(rest, wl["conv_w"])
    y_pool = _pool_fwd(rest, wl["pool_w"], wl["pool_scale"])
    y = jnp.concatenate([y_att, y_conv, y_pool], axis=1)
    z = _mm(y, wl["w_out"], out_dtype=F32, tm=1024, tn=512, tk=D_MODEL, name="mm_out", c=x, c_scale=ALPHA)
    out, outb, xhat, rstd = _ln_fwd(z, wl["ln_g"], wl["ln_b"], ts=TS)
    saved = dict(xb=xb, qkv=qkv, rest=rest, fgp=fgp, cum=cum, cumr=cumr, o=o, lse=lse, y=y, xhat=xhat, rstd=rstd)
    return out, outb, saved


def _layer_bwd(dout, sv, wl):
    S = dout.shape[0]
    dz, dzb, dln_g, dln_b = _ln_bwd(dout, sv["xhat"], sv["rstd"], wl["ln_g"], ts=TS)
    dy = _mm(dzb, wl["w_out_t"], out_dtype=F32, tm=1024, tn=512, tk=D_MODEL, name="mm_dy")
    dw_out = _mm(sv["y"].T, dzb, out_dtype=F32, tm=1024, tn=512, tk=2048, name="mm_dw_out")
    rest = sv["rest"]
    do, dg_att, a, delta = _attn_bwd_pre(rest, sv["o"], dy, sv["cum"], sv["lse"])
    dq, dk, dv, dcr, dcc = _attn_bwd(sv["qkv"], do, a, delta, sv["cumr"], tq=TQ)
    dcum = dcr[:, :, 0] - dcc.reshape(N_HEADS, S)
    dcum = jnp.pad(dcum.T, ((0, 0), (0, LANES - N_HEADS)))
    dfg, db_f = _fg_bwd(dcum, sv["fgp"], wl["b_f"])
    dcb, dcc_, dch, dgc, dconv_w = _conv_bwd(rest, dy, wl["conv_w"])
    dpu, dgp, dpool_w, dpool_scale = _pool_bwd(rest, dy, wl["pool_w"], wl["pool_scale"])
    dproj = jnp.concatenate([dq, dk, dv, dg_att, dcb, dcc_, dch, dgc, dpu, dgp], axis=1)
    xt = sv["xb"].T
    dw_main = _mm(xt, dproj, out_dtype=F32, tm=1024, tn=512, tk=2048, name="mm_dw_main")
    dw_fg = _mm(xt, dfg, out_dtype=F32, tm=1024, tn=LANES, tk=2048, name="mm_dw_fg")
    t = _mm(dfg, wl["w_fg_t"], out_dtype=F32, tm=1024, tn=512, tk=LANES, name="mm_dx_fg", c=dz, c_scale=ALPHA)
    dx = _mm(dproj, wl["w_main_t"], out_dtype=F32, tm=1024, tn=512, tk=1792, name="mm_dx", c=t, c_scale=1.0)
    dw_in = jnp.concatenate([dw_main[:, :FG_AT], dw_fg[:, :N_HEADS], dw_main[:, FG_AT:]], axis=1)
    grads = dict(w_in=dw_in, b_f=db_f[:, :N_HEADS], conv_w=dconv_w, pool_w=dpool_w, pool_scale=dpool_scale,
                 w_out=dw_out, ln_g=dln_g, ln_b=dln_b)
    return dx, grads


def _local_step(x0, tgt, weights):
    x, xb = x0, x0.astype(BF16)
    saved = []
    for wl in weights:
        x, xb, sv = _layer_fwd(x, xb, wl)
        saved.append(sv)
    loss, dx = _loss_head(x, tgt, ts=TS)
    grads = [None] * len(weights)
    for l in reversed(range(len(weights))):
        dx, grads[l] = _layer_bwd(dx, saved[l], weights[l])
    return loss[0, 0], dx, grads


def _layer_weights(w_full, w_o, conv_w, b_f, pool_w, pool_scale, ln_g, ln_b):
    w_main = jnp.concatenate([w_full[:, :FG_AT], w_full[:, FG_AT + N_HEADS:]], axis=1)
    w_fg = jnp.pad(w_full[:, FG_AT:FG_AT + N_HEADS], ((0, 0), (0, LANES - N_HEADS)))
    return dict(w_main=w_main, w_fg=w_fg, w_main_t=w_main.T, w_fg_t=w_fg.T, w_out=w_o, w_out_t=w_o.T,
                conv_w=conv_w, b_f=jnp.pad(b_f[None, :], ((0, 0), (0, LANES - N_HEADS))),
                pool_w=pool_w, pool_scale=pool_scale[None, :], ln_g=ln_g[None, :], ln_b=ln_b[None, :])


def kernel(x, w_in, b_f, conv_w, pool_w, pool_scale, w_out, ln_g, ln_b, loss_target, m_w_in, m_b_f, m_conv_w, m_pool_w, m_pool_scale, m_w_out, m_ln_g, m_ln_b, v_w_in, v_b_f, v_conv_w, v_pool_w, v_pool_scale, v_w_out, v_ln_g, v_ln_b):
    L, D, cols = w_in.shape
    rows_out = w_out.shape[1]
    ccols = conv_w.shape[2]

    g_in, g_out, g_conv = _all_gather([w_in.astype(BF16), w_out.astype(BF16), conv_w], name="gather_weights")
    w_full = jnp.transpose(g_in, (1, 2, 0, 3)).reshape(L, D, N_DEV * cols)
    w_o = jnp.transpose(g_out, (1, 0, 2, 3)).reshape(L, N_DEV * rows_out, D)
    cw = jnp.transpose(g_conv, (1, 2, 0, 3)).reshape(L, CONV_TAPS, N_DEV * ccols)
    weights = [_layer_weights(w_full[l], w_o[l], cw[l], b_f[l], pool_w[l], pool_scale[l], ln_g[l], ln_b[l])
               for l in range(L)]

    loss, dx, grads = _local_step(x[0], loss_target[0], weights)
    loss = lax.psum(loss, ("x", "y", "c"))

    def stacked(name):
        return jnp.stack([g[name] for g in grads])

    p_in = jnp.transpose(stacked("w_in").reshape(L, D, N_DEV, cols), (2, 0, 1, 3)).astype(BF16)
    p_out = jnp.transpose(stacked("w_out").reshape(L, N_DEV, rows_out, D), (1, 0, 2, 3)).astype(BF16)
    p_conv = jnp.transpose(stacked("conv_w").reshape(L, CONV_TAPS, N_DEV, ccols), (2, 0, 1, 3))
    r_in, r_out, r_conv = _all_to_all([p_in, p_out, p_conv], name="scatter_grads")
    small = ["b_f", "pool_w", "pool_scale", "ln_g", "ln_b"]
    gathered = _all_gather([stacked(k) for k in small], name="gather_small_grads")

    given = dict(w_in=(w_in, m_w_in, v_w_in), b_f=(b_f, m_b_f, v_b_f), conv_w=(conv_w, m_conv_w, v_conv_w),
                 pool_w=(pool_w, m_pool_w, v_pool_w), pool_scale=(pool_scale, m_pool_scale, v_pool_scale),
                 w_out=(w_out, m_w_out, v_w_out), ln_g=(ln_g, m_ln_g, v_ln_g), ln_b=(ln_b, m_ln_b, v_ln_b))
    parts = dict(w_in=r_in, w_out=r_out, conv_w=r_conv, **dict(zip(small, gathered)))
    res = {}
    for k, (w, m, v) in given.items():
        shape = w.shape
        two_d = (-1, shape[-1])
        outs = _adamw(parts[k].reshape((N_DEV,) + w.reshape(two_d).shape), w.reshape(two_d), m.reshape(two_d),
                      v.reshape(two_d), tr=128, name="adamw_" + k)
        res[k] = [t.reshape(shape) for t in outs]

    order = ["w_in", "b_f", "conv_w", "pool_w", "pool_scale", "w_out", "ln_g", "ln_b"]
    return (loss, dx[None], *[res[k][0] for k in order], *[res[k][1] for k in order],
            *[res[k][2] for k in order], *[res[k][3] for k in order])
```

```python
import jax
import jax.numpy as jnp
from jax import lax
from jax.experimental import pallas as pl
from jax.experimental.pallas import tpu as pltpu

F32 = jnp.float32
BF16 = jnp.bfloat16

N_DEV = 8
D_MODEL = 2048
N_HEADS = 8
HEAD_DIM = 128
ATT_W = N_HEADS * HEAD_DIM
CONV_W = 512
CONV_TAPS = 3
POOL_W = 512
POOL_GROUP = 128
N_POOL = POOL_W // POOL_GROUP
FG_AT = 4 * ATT_W
IN_W = FG_AT + N_HEADS + 4 * CONV_W + 2 * POOL_W
MAIN_W = IN_W - N_HEADS
LANES = 128
LN_EPS = 1e-5
DEPTH = 4
ALPHA = (2 * DEPTH) ** 0.25
SCALE = HEAD_DIM ** -0.5
NEG = -1e30

ADAM_LR, ADAM_B1, ADAM_B2, ADAM_EPS, ADAM_WD, ADAM_STEP = 0.001, 0.9, 0.999, 1e-08, 0.01, 10

QKV_W = 3 * ATT_W
REST_W = MAIN_W - QKV_W
QB, KB, VB = 0, 8, 16
GB = 0
CBB, CCB, CHB, CGB = 8, 12, 16, 20
PUB, PGB = 24, 28
VMEM_LIMIT = 48 * 1024 * 1024

MESH = pl.DeviceIdType.MESH
ANY = pl.BlockSpec(memory_space=pl.ANY)


def _params(semantics):
    return pltpu.CompilerParams(dimension_semantics=semantics, vmem_limit_bytes=VMEM_LIMIT)


def _sigmoid(x):
    return 1.0 / (1.0 + jnp.exp(-x))


def _mm(a, b, *, out_dtype, tm, tn, tk, name, c=None, c_scale=1.0, n_off=0, n_cols=None, bt=False):
    M, K = a.shape
    N = b.shape[0 if bt else 1] if n_cols is None else n_cols
    assert not (bt and n_off)
    dims = (((1,), (1 if bt else 0,)), ((), ()))
    tm, tn, tk = min(tm, M), min(tn, N), min(tk, K)
    assert M % tm == 0 and N % tn == 0 and K % tk == 0 and n_off % tn == 0, (a.shape, b.shape, tm, tn, tk)
    nk, joff = K // tk, n_off // tn
    has_c = c is not None

    def body(*refs):
        a_ref, b_ref = refs[0], refs[1]
        c_ref = refs[2] if has_c else None
        o_ref = refs[2 + has_c]

        def finish(r):
            if has_c:
                r = r + c_scale * c_ref[...]
            o_ref[...] = r.astype(out_dtype)

        if nk == 1:
            finish(lax.dot_general(a_ref[...], b_ref[...], dims, preferred_element_type=F32))
        else:
            acc_ref = refs[3 + has_c]
            k = pl.program_id(2)

            @pl.when(k == 0)
            def _():
                acc_ref[...] = jnp.zeros_like(acc_ref)

            acc_ref[...] += lax.dot_general(a_ref[...], b_ref[...], dims, preferred_element_type=F32)

            @pl.when(k == nk - 1)
            def _():
                finish(acc_ref[...])

    in_specs = [pl.BlockSpec((tm, tk), lambda i, j, k: (i, k)),
                pl.BlockSpec((tn, tk), lambda i, j, k: (j, k)) if bt
                else pl.BlockSpec((tk, tn), lambda i, j, k: (k, j + joff))]
    args = [a, b]
    if has_c:
        in_specs.append(pl.BlockSpec((tm, tn), lambda i, j, k: (i, j)))
        args.append(c)
    return pl.pallas_call(
        body, name=name, grid=(M // tm, N // tn, nk),
        in_specs=in_specs,
        out_specs=pl.BlockSpec((tm, tn), lambda i, j, k: (i, j)),
        out_shape=jax.ShapeDtypeStruct((M, N), out_dtype),
        scratch_shapes=[pltpu.VMEM((tm, tn), F32)] if nk > 1 else [],
        compiler_params=_params(("parallel", "parallel", "arbitrary")),
    )(*args)


def _shift_down(x, k, row):
    return jnp.where(row >= k, pltpu.roll(x, k, 0), 0.0)


def _shift_up(x, k, row):
    s = x.shape[0]
    return jnp.where(row < s - k, pltpu.roll(x, s - k, 0), 0.0)


def _fg_fwd(fgp, bf):
    S = fgp.shape[0]

    def body(f_ref, b_ref, cum_ref):
        a = f_ref[...] + b_ref[...]
        x = jnp.minimum(a, 0.0) - jnp.log(1.0 + jnp.exp(-jnp.abs(a)))
        row = lax.broadcasted_iota(jnp.int32, x.shape, 0)
        k = 1
        while k < S:
            x = x + _shift_down(x, k, row)
            k *= 2
        cum_ref[...] = x

    return pl.pallas_call(body, name="fg_fwd", out_shape=jax.ShapeDtypeStruct((S, LANES), F32),
                          compiler_params=_params(None))(fgp, bf)


def _fg_bwd(dcum, fgp, bf):
    S = fgp.shape[0]

    def body(d_ref, f_ref, b_ref, dfg_ref, dbf_ref):
        x = d_ref[...]
        row = lax.broadcasted_iota(jnp.int32, x.shape, 0)
        k = 1
        while k < S:
            x = x + _shift_up(x, k, row)
            k *= 2
        dfg = x * _sigmoid(-(f_ref[...] + b_ref[...]))
        dfg_ref[...] = dfg.astype(BF16)
        dbf_ref[...] = jnp.sum(dfg, axis=0, keepdims=True)

    return pl.pallas_call(
        body, name="fg_bwd",
        out_shape=(jax.ShapeDtypeStruct((S, LANES), BF16), jax.ShapeDtypeStruct((1, LANES), F32)),
        compiler_params=_params(None),
    )(dcum, fgp, bf)


def _colblock(S, off):
    return pl.BlockSpec((S, LANES), lambda h: (0, h + off))


def _head_col(S):
    return pl.BlockSpec((None, S, 1), lambda h: (h, 0, 0))


def _head_row(nq, tq):
    return pl.BlockSpec((None, nq, 1, tq), lambda h: (h, 0, 0, 0))


def _lane_of(ref, h):
    lane = lax.broadcasted_iota(jnp.int32, ref.shape, 1)
    return jnp.sum(jnp.where(lane == h, ref[...], 0.0), axis=1, keepdims=True)


def _attn_fwd(qkv, rest, cum, cumr, *, tq):
    S = qkv.shape[0]
    nq = S // tq

    def body(q_ref, k_ref, v_ref, g_ref, cum_ref, cumr_ref, o_ref, y_ref, lse_ref, cc):
        cc[...] = _lane_of(cum_ref, pl.program_id(0))
        tri = (lax.broadcasted_iota(jnp.int32, (tq, tq), 1) <= lax.broadcasted_iota(jnp.int32, (tq, tq), 0))

        def q_step(qi, _):
            rows = pl.ds(pl.multiple_of(qi * tq, tq), tq)
            q = q_ref[rows, :]
            ci = cc[rows, :]

            def tile(kj, carry, masked):
                m, l, acc = carry
                cols = pl.ds(pl.multiple_of(kj * tq, tq), tq)
                s = lax.dot_general(q, k_ref[cols, :], (((1,), (1,)), ((), ())), preferred_element_type=F32) * SCALE
                s = s + ci - cumr_ref[kj]
                if masked:
                    s = jnp.where(tri, s, NEG)
                m_new = jnp.maximum(m, jnp.max(s, axis=1, keepdims=True))
                a = jnp.exp(m - m_new)
                p = jnp.exp(s - m_new)
                l = a * l + jnp.sum(p, axis=1, keepdims=True)
                acc = a * acc + jnp.dot(p.astype(BF16), v_ref[cols, :], preferred_element_type=F32)
                return m_new, l, acc

            init = (jnp.full((tq, 1), NEG, F32), jnp.zeros((tq, 1), F32), jnp.zeros((tq, HEAD_DIM), F32))
            carry = lax.fori_loop(0, qi, lambda kj, c: tile(kj, c, False), init)
            m, l, acc = tile(qi, carry, True)
            o = acc / l
            g = g_ref[rows, :]
            o_ref[rows, :] = o
            y_ref[rows, :] = (o * (g * _sigmoid(g))).astype(BF16)
            lse_ref[rows, :] = m + jnp.log(l)
            return 0

        lax.fori_loop(0, nq, q_step, 0)

    return pl.pallas_call(
        body, name="attn_fwd", grid=(N_HEADS,),
        in_specs=[_colblock(S, QB), _colblock(S, KB), _colblock(S, VB), _colblock(S, GB),
                  pl.BlockSpec((S, LANES), lambda h: (0, 0)), _head_row(nq, tq)],
        out_specs=(_colblock(S, 0), _colblock(S, 0), _head_col(S)),
        out_shape=(jax.ShapeDtypeStruct((S, ATT_W), F32), jax.ShapeDtypeStruct((S, ATT_W), BF16),
                   jax.ShapeDtypeStruct((N_HEADS, S, 1), F32)),
        scratch_shapes=[pltpu.VMEM((S, 1), F32)],
        compiler_params=_params(("arbitrary",)),
    )(qkv, qkv, qkv, rest, cum, cumr)


def _attn_bwd_pre(rest, o, dy, cum, lse):
    S = o.shape[0]

    def body(g_ref, o_ref, dy_ref, cum_ref, lse_ref, do_ref, dg_ref, a_ref, dl_ref):
        g = g_ref[...]
        sg = _sigmoid(g)
        ov = o_ref[...]
        dy = dy_ref[...]
        do = dy * (g * sg)
        do_ref[...] = do.astype(BF16)
        dg_ref[...] = (dy * ov * (sg * (1.0 + g * (1.0 - sg)))).astype(BF16)
        dl_ref[...] = jnp.sum(do * ov, axis=1, keepdims=True)
        a_ref[...] = _lane_of(cum_ref, pl.program_id(0)) - lse_ref[...]

    att = jax.ShapeDtypeStruct((S, ATT_W), BF16)
    col = jax.ShapeDtypeStruct((N_HEADS, S, 1), F32)
    return pl.pallas_call(
        body, name="attn_bwd_pre", grid=(N_HEADS,),
        in_specs=[_colblock(S, GB), _colblock(S, 0), _colblock(S, 0),
                  pl.BlockSpec((S, LANES), lambda h: (0, 0)), _head_col(S)],
        out_specs=(_colblock(S, 0), _colblock(S, 0), _head_col(S), _head_col(S)),
        out_shape=(att, att, col, col),
        compiler_params=_params(("parallel",)),
    )(rest, o, dy, cum, lse)


def _attn_bwd(qkv, do, a, delta, cumr, *, tq):
    S = qkv.shape[0]
    nq = S // tq
    tdot = (((0,), (0,)), ((), ()))

    def body(q_ref, k_ref, v_ref, do_ref, a_ref, dl_ref, cumr_ref, dq_ref, dk_ref, dv_ref, dcr_ref, dcc_ref, dqa):
        dqa[...] = jnp.zeros_like(dqa)
        dcr_ref[...] = jnp.zeros_like(dcr_ref)
        tri = (lax.broadcasted_iota(jnp.int32, (tq, tq), 1) <= lax.broadcasted_iota(jnp.int32, (tq, tq), 0))

        def k_step(kj, _):
            cols = pl.ds(pl.multiple_of(kj * tq, tq), tq)
            k = k_ref[cols, :]
            v = v_ref[cols, :]
            cj = cumr_ref[kj]

            def tile(qi, carry, masked):
                dk, dv, csum = carry
                rows = pl.ds(pl.multiple_of(qi * tq, tq), tq)
                q = q_ref[rows, :]
                dot = do_ref[rows, :]
                s = lax.dot_general(q, k, (((1,), (1,)), ((), ())), preferred_element_type=F32) * SCALE
                p = jnp.exp(s + a_ref[rows, :] - cj)
                if masked:
                    p = jnp.where(tri, p, 0.0)
                dv = dv + lax.dot_general(p.astype(BF16), dot, tdot, preferred_element_type=F32)
                dp = lax.dot_general(dot, v, (((1,), (1,)), ((), ())), preferred_element_type=F32)
                ds = p * (dp - dl_ref[rows, :])
                dcr_ref[rows, :] += jnp.sum(ds, axis=1, keepdims=True)
                csum = csum + jnp.sum(ds, axis=0, keepdims=True)
                dsb = (ds * SCALE).astype(BF16)
                dqa[rows, :] += jnp.dot(dsb, k, preferred_element_type=F32)
                dk = dk + lax.dot_general(dsb, q, tdot, preferred_element_type=F32)
                return dk, dv, csum

            init = (jnp.zeros((tq, HEAD_DIM), F32), jnp.zeros((tq, HEAD_DIM), F32), jnp.zeros((1, tq), F32))
            carry = tile(kj, init, True)
            dk, dv, csum = lax.fori_loop(kj + 1, nq, lambda qi, c: tile(qi, c, False), carry)
            dk_ref[cols, :] = dk.astype(BF16)
            dv_ref[cols, :] = dv.astype(BF16)
            dcc_ref[kj] = csum
            return 0

        lax.fori_loop(0, nq, k_step, 0)
        dq_ref[...] = dqa[...].astype(BF16)

    att = jax.ShapeDtypeStruct((S, ATT_W), BF16)
    return pl.pallas_call(
        body, name="attn_bwd", grid=(N_HEADS,),
        in_specs=[_colblock(S, QB), _colblock(S, KB), _colblock(S, VB), _colblock(S, 0),
                  _head_col(S), _head_col(S), _head_row(nq, tq)],
        out_specs=(_colblock(S, 0), _colblock(S, 0), _colblock(S, 0), _head_col(S), _head_row(nq, tq)),
        out_shape=(att, att, att, jax.ShapeDtypeStruct((N_HEADS, S, 1), F32),
                   jax.ShapeDtypeStruct((N_HEADS, nq, 1, tq), F32)),
        scratch_shapes=[pltpu.VMEM((S, HEAD_DIM), F32)],
        compiler_params=_params(("arbitrary",)),
    )(qkv, qkv, qkv, do, a, delta, cumr)


def _conv_taps(u, w_ref, row):
    return (w_ref[0:1, :] * _shift_down(u, 2, row) + w_ref[1:2, :] * _shift_down(u, 1, row)) + w_ref[2:3, :] * u


def _conv_fwd(rest, cw):
    S = rest.shape[0]

    def body(cb_ref, cc_ref, ch_ref, g_ref, w_ref, y_ref):
        row = lax.broadcasted_iota(jnp.int32, (S, LANES), 0)
        g = g_ref[...]
        y = _conv_taps(cc_ref[...] * ch_ref[...], w_ref, row)
        y_ref[...] = ((cb_ref[...] * y) * (g * _sigmoid(g))).astype(BF16)

    return pl.pallas_call(
        body, name="conv_fwd", grid=(CONV_W // LANES,),
        in_specs=[_colblock(S, CBB), _colblock(S, CCB), _colblock(S, CHB), _colblock(S, CGB),
                  pl.BlockSpec((CONV_TAPS, LANES), lambda j: (0, j))],
        out_specs=_colblock(S, 0),
        out_shape=jax.ShapeDtypeStruct((S, CONV_W), BF16),
        compiler_params=_params(("parallel",)),
    )(rest, rest, rest, rest, cw)


def _conv_bwd(rest, dy, cw):
    S = rest.shape[0]

    def body(cb_ref, cc_ref, ch_ref, g_ref, dy_ref, w_ref, dcb_ref, dcc_ref, dch_ref, dg_ref, dw_ref):
        row = lax.broadcasted_iota(jnp.int32, (S, LANES), 0)
        g = g_ref[...]
        sg = _sigmoid(g)
        silu = g * sg
        cb, cc, ch, dy = cb_ref[...], cc_ref[...], ch_ref[...], dy_ref[...]
        u = cc * ch
        u1 = _shift_down(u, 1, row)
        u2 = _shift_down(u, 2, row)
        y = (w_ref[0:1, :] * u2 + w_ref[1:2, :] * u1) + w_ref[2:3, :] * u
        dcb_ref[...] = (dy * silu * y).astype(BF16)
        dg_ref[...] = (dy * (cb * y) * (sg * (1.0 + g * (1.0 - sg)))).astype(BF16)
        dyv = dy * silu * cb
        du = w_ref[2:3, :] * dyv + w_ref[1:2, :] * _shift_up(dyv, 1, row) + w_ref[0:1, :] * _shift_up(dyv, 2, row)
        dcc_ref[...] = (du * ch).astype(BF16)
        dch_ref[...] = (du * cc).astype(BF16)
        dw_ref[0:1, :] = jnp.sum(dyv * u2, axis=0, keepdims=True)
        dw_ref[1:2, :] = jnp.sum(dyv * u1, axis=0, keepdims=True)
        dw_ref[2:3, :] = jnp.sum(dyv * u, axis=0, keepdims=True)

    act = jax.ShapeDtypeStruct((S, CONV_W), BF16)
    return pl.pallas_call(
        body, name="conv_bwd", grid=(CONV_W // LANES,),
        in_specs=[_colblock(S, CBB), _colblock(S, CCB), _colblock(S, CHB), _colblock(S, CGB),
                  _colblock(S, ATT_W // LANES), pl.BlockSpec((CONV_TAPS, LANES), lambda j: (0, j))],
        out_specs=(_colblock(S, 0),) * 4 + (pl.BlockSpec((CONV_TAPS, LANES), lambda j: (0, j)),),
        out_shape=(act, act, act, act, jax.ShapeDtypeStruct((CONV_TAPS, CONV_W), F32)),
        compiler_params=_params(("parallel",)),
    )(rest, rest, rest, rest, dy, cw)


def _pool_z(u, grp, row):
    s2 = u + _shift_down(u, 1, row)
    s4 = s2 + _shift_down(s2, 2, row)
    s8 = s4 + _shift_down(s4, 4, row)
    s16 = s8 + _shift_down(s8, 8, row)
    sw = jnp.where(grp == 0, s2, jnp.where(grp == 1, s4, jnp.where(grp == 2, s8, s16)))
    return sw / _pool_count(grp, row) - u


def _pool_count(grp, row):
    return jnp.minimum(row + 1, 2 << grp).astype(F32)


def _pool_fwd(rest, pw, scale):
    S = rest.shape[0]

    def body(u_ref, g_ref, pw_ref, sc_ref, y_ref):
        grp = pl.program_id(0)
        row = lax.broadcasted_iota(jnp.int32, (S, LANES), 0)
        z = _pool_z(u_ref[...], grp, row)
        zp = jnp.dot(z.astype(BF16), pw_ref[...].astype(BF16), preferred_element_type=F32)
        g = g_ref[...]
        y_ref[...] = ((zp * sc_ref[...]) * (g * _sigmoid(g))).astype(BF16)

    return pl.pallas_call(
        body, name="pool_fwd", grid=(N_POOL,),
        in_specs=[_colblock(S, PUB), _colblock(S, PGB),
                  pl.BlockSpec((None, POOL_GROUP, POOL_GROUP), lambda j: (j, 0, 0)),
                  pl.BlockSpec((1, LANES), lambda j: (0, j))],
        out_specs=_colblock(S, 0),
        out_shape=jax.ShapeDtypeStruct((S, POOL_W), BF16),
        compiler_params=_params(("parallel",)),
    )(rest, rest, pw, scale)


def _pool_bwd(rest, dy, pw, scale):
    S = rest.shape[0]

    def body(u_ref, g_ref, dy_ref, pw_ref, sc_ref, du_ref, dg_ref, dpw_ref, dsc_ref):
        grp = pl.program_id(0)
        row = lax.broadcasted_iota(jnp.int32, (S, LANES), 0)
        z = _pool_z(u_ref[...], grp, row).astype(BF16)
        pwb = pw_ref[...].astype(BF16)
        zp = jnp.dot(z, pwb, preferred_element_type=F32)
        g = g_ref[...]
        sg = _sigmoid(g)
        silu = g * sg
        dy = dy_ref[...]
        sc = sc_ref[...]
        dsc_ref[...] = jnp.sum(dy * silu * zp, axis=0, keepdims=True)
        dg_ref[...] = (dy * (zp * sc) * (sg * (1.0 + g * (1.0 - sg)))).astype(BF16)
        dzp = (dy * silu * sc).astype(BF16)
        dpw_ref[...] = lax.dot_general(z, dzp, (((0,), (0,)), ((), ())), preferred_element_type=F32)
        dz = lax.dot_general(dzp, pwb, (((1,), (1,)), ((), ())), preferred_element_type=F32)
        f1 = dz / _pool_count(grp, row)
        f2 = f1 + _shift_up(f1, 1, row)
        f4 = f2 + _shift_up(f2, 2, row)
        f8 = f4 + _shift_up(f4, 4, row)
        f16 = f8 + _shift_up(f8, 8, row)
        fw = jnp.where(grp == 0, f2, jnp.where(grp == 1, f4, jnp.where(grp == 2, f8, f16)))
        du_ref[...] = (fw - dz).astype(BF16)

    act = jax.ShapeDtypeStruct((S, POOL_W), BF16)
    return pl.pallas_call(
        body, name="pool_bwd", grid=(N_POOL,),
        in_specs=[_colblock(S, PUB), _colblock(S, PGB), _colblock(S, (ATT_W + CONV_W) // LANES),
                  pl.BlockSpec((None, POOL_GROUP, POOL_GROUP), lambda j: (j, 0, 0)),
                  pl.BlockSpec((1, LANES), lambda j: (0, j))],
        out_specs=(_colblock(S, 0), _colblock(S, 0),
                   pl.BlockSpec((None, POOL_GROUP, POOL_GROUP), lambda j: (j, 0, 0)),
                   pl.BlockSpec((1, LANES), lambda j: (0, j))),
        out_shape=(act, act, jax.ShapeDtypeStruct((N_POOL, POOL_GROUP, POOL_GROUP), F32),
                   jax.ShapeDtypeStruct((1, POOL_W), F32)),
        compiler_params=_params(("parallel",)),
    )(rest, rest, dy, pw, scale)


def _ln_fwd(z, g, b, *, ts):
    S, D = z.shape

    def body(z_ref, g_ref, b_ref, o_ref, ob_ref, xh_ref, rs_ref):
        zz = z_ref[...]
        mu = jnp.mean(zz, axis=1, keepdims=True)
        zc = zz - mu
        rstd = lax.rsqrt(jnp.mean(zc * zc, axis=1, keepdims=True) + LN_EPS)
        xh = zc * rstd
        out = xh * g_ref[...] + b_ref[...]
        o_ref[...] = out
        ob_ref[...] = out.astype(BF16)
        xh_ref[...] = xh
        rs_ref[...] = rstd

    rowblk = pl.BlockSpec((ts, D), lambda i: (i, 0))
    vec = pl.BlockSpec((1, D), lambda i: (0, 0))
    return pl.pallas_call(
        body, name="ln_fwd", grid=(S // ts,),
        in_specs=[rowblk, vec, vec],
        out_specs=(rowblk, rowblk, rowblk, pl.BlockSpec((ts, 1), lambda i: (i, 0))),
        out_shape=(jax.ShapeDtypeStruct((S, D), F32), jax.ShapeDtypeStruct((S, D), BF16),
                   jax.ShapeDtypeStruct((S, D), F32), jax.ShapeDtypeStruct((S, 1), F32)),
        compiler_params=_params(("parallel",)),
    )(z, g, b)


def _ln_bwd(dout, xhat, rstd, g, *, ts):
    S, D = dout.shape

    def body(d_ref, xh_ref, rs_ref, g_ref, dz_ref, dzb_ref, dg_ref, db_ref):
        @pl.when(pl.program_id(0) == 0)
        def _():
            dg_ref[...] = jnp.zeros_like(dg_ref)
            db_ref[...] = jnp.zeros_like(db_ref)

        d = d_ref[...]
        xh = xh_ref[...]
        dxh = d * g_ref[...]
        m1 = jnp.mean(dxh, axis=1, keepdims=True)
        m2 = jnp.mean(dxh * xh, axis=1, keepdims=True)
        dz = rs_ref[...] * (dxh - m1 - xh * m2)
        dz_ref[...] = dz
        dzb_ref[...] = dz.astype(BF16)
        dg_ref[...] += jnp.sum(d * xh, axis=0, keepdims=True)
        db_ref[...] += jnp.sum(d, axis=0, keepdims=True)

    rowblk = pl.BlockSpec((ts, D), lambda i: (i, 0))
    vec = pl.BlockSpec((1, D), lambda i: (0, 0))
    return pl.pallas_call(
        body, name="ln_bwd", grid=(S // ts,),
        in_specs=[rowblk, rowblk, pl.BlockSpec((ts, 1), lambda i: (i, 0)), vec],
        out_specs=(rowblk, rowblk, vec, vec),
        out_shape=(jax.ShapeDtypeStruct((S, D), F32), jax.ShapeDtypeStruct((S, D), BF16),
                   jax.ShapeDtypeStruct((1, D), F32), jax.ShapeDtypeStruct((1, D), F32)),
        compiler_params=_params(("arbitrary",)),
    )(dout, xhat, rstd, g)


def _loss_head(y, tgt, *, ts):
    S, D = y.shape

    def body(y_ref, t_ref, l_ref, dy_ref):
        @pl.when(pl.program_id(0) == 0)
        def _():
            l_ref[...] = jnp.zeros_like(l_ref)

        e = y_ref[...] - t_ref[...]
        dy_ref[...] = e * (1.0 / D)
        rowloss = jnp.sum(e * e, axis=1, keepdims=True) * (0.5 / D)
        l_ref[...] += jnp.sum(rowloss, axis=0, keepdims=True)

    rowblk = pl.BlockSpec((ts, D), lambda i: (i, 0))
    return pl.pallas_call(
        body, name="loss_head", grid=(S // ts,),
        in_specs=[rowblk, rowblk],
        out_specs=(pl.BlockSpec((1, 1), lambda i: (0, 0)), rowblk),
        out_shape=(jax.ShapeDtypeStruct((1, 1), F32), jax.ShapeDtypeStruct((S, D), F32)),
        compiler_params=_params(("arbitrary",)),
    )(y, tgt)


def _adamw(parts, w, m, v, *, tr, name):
    R, C = w.shape
    tr = min(tr, R)
    assert R % tr == 0
    c1 = 1.0 - ADAM_B1 ** ADAM_STEP
    c2 = 1.0 - ADAM_B2 ** ADAM_STEP

    def body(p_ref, w_ref, m_ref, v_ref, g_ref, d_ref, nm_ref, nv_ref):
        g = p_ref[0].astype(F32)
        for j in range(1, N_DEV):
            g = g + p_ref[j].astype(F32)
        nm = ADAM_B1 * m_ref[...] + (1.0 - ADAM_B1) * g
        nv = ADAM_B2 * v_ref[...] + (1.0 - ADAM_B2) * (g * g)
        g_ref[...] = g
        nm_ref[...] = nm
        nv_ref[...] = nv
        d_ref[...] = -ADAM_LR * ((nm / c1) / (jnp.sqrt(nv / c2) + ADAM_EPS) + ADAM_WD * w_ref[...])

    blk = pl.BlockSpec((tr, C), lambda i: (i, 0))
    out = jax.ShapeDtypeStruct((R, C), F32)
    return pl.pallas_call(
        body, name=name, grid=(R // tr,),
        in_specs=[pl.BlockSpec((N_DEV, tr, C), lambda i: (0, i, 0)), blk, blk, blk],
        out_specs=(blk, blk, blk, blk),
        out_shape=(out, out, out, out),
        compiler_params=_params(("parallel",)),
    )(parts, w, m, v)


def _my_place():
    return lax.axis_index("x"), lax.axis_index("y"), lax.axis_index("c")


def _all_gather(blocks, *, name):
    n = len(blocks)

    def body(*refs):
        ins, outs = refs[:n], refs[n:2 * n]
        send_sems, recv_sems, local_sems = refs[2 * n:]
        x, y, c = _my_place()
        me, sibling = (x, y, c), (x, y, 1 - c)
        chips = [(1 - x, y), (x, 1 - y), (1 - x, 1 - y)]

        def slot(a, place):
            px, py, pc = place
            return outs[a].at[4 * px + 2 * py + pc]

        def copy(a, k, block, to, src=None):
            return pltpu.make_async_remote_copy(
                src_ref=slot(a, block) if src is None else src, dst_ref=slot(a, block),
                send_sem=send_sems.at[a, k], recv_sem=recv_sems.at[a, k], device_id=to, device_id_type=MESH)

        mine = [pltpu.make_async_copy(ins[a], slot(a, me), local_sems.at[a]) for a in range(n)]
        for cp in mine:
            cp.start()
        first = []
        for a in range(n):
            first.append(copy(a, 0, me, sibling, src=ins[a]))
            first += [copy(a, 1 + j, me, (*chip, c), src=ins[a]) for j, chip in enumerate(chips)]
        for cp in first:
            cp.start()
        passed = []
        for j, chip in enumerate(chips):
            for a in range(n):
                copy(a, 1 + j, (*chip, c), me).wait_recv()
                fwd = copy(a, 4 + j, (*chip, c), sibling)
                fwd.start()
                passed.append(fwd)
        for a in range(n):
            copy(a, 0, sibling, me).wait_recv()
            for j, chip in enumerate(chips):
                copy(a, 4 + j, (*chip, 1 - c), me).wait_recv()
        for cp in first + passed:
            cp.wait_send()
        for cp in mine:
            cp.wait()

    return pl.pallas_call(
        body, name=name,
        in_specs=[ANY] * n, out_specs=[ANY] * n,
        out_shape=[jax.ShapeDtypeStruct((N_DEV,) + b.shape, b.dtype) for b in blocks],
        scratch_shapes=[pltpu.SemaphoreType.DMA((n, 7)), pltpu.SemaphoreType.DMA((n, 7)),
                        pltpu.SemaphoreType.DMA((n,))],
        compiler_params=pltpu.CompilerParams(has_side_effects=True),
    )(*blocks)


def _all_to_all(groups, *, name):
    flat = [(g, l, p) for g, grp in enumerate(groups) for l, p in enumerate(grp)]
    n, ng = len(flat), len(groups)

    def body(*refs):
        ins, outs = refs[:n], refs[n:n + ng]
        send_sems, recv_sems, local_sems = refs[n + ng:]
        x, y, c = _my_place()
        me = 4 * x + 2 * y + c

        def land(a, src):
            g, l, _ = flat[a]
            return outs[g].at[src, l]

        mine = [pltpu.make_async_copy(ins[a].at[me], land(a, me), local_sems.at[a]) for a in range(n)]
        for cp in mine:
            cp.start()
        peers = [(x ^ ((k >> 2) & 1), y ^ ((k >> 1) & 1), c ^ (k & 1)) for k in range(1, N_DEV)]
        sends = []
        for k, (px, py, pc) in enumerate(peers):
            peer = 4 * px + 2 * py + pc
            for a in range(n):
                sends.append(pltpu.make_async_remote_copy(
                    src_ref=ins[a].at[peer], dst_ref=land(a, me), send_sem=send_sems.at[a, k],
                    recv_sem=recv_sems.at[a, k], device_id=(px, py, pc), device_id_type=MESH))
                sends[-1].start()
        for k, (px, py, pc) in enumerate(peers):
            peer = 4 * px + 2 * py + pc
            for a in range(n):
                pltpu.make_async_remote_copy(
                    src_ref=ins[a].at[peer], dst_ref=land(a, peer), send_sem=send_sems.at[a, k],
                    recv_sem=recv_sems.at[a, k], device_id=(px, py, pc), device_id_type=MESH).wait_recv()
        for cp in sends:
            cp.wait_send()
        for cp in mine:
            cp.wait()

    return pl.pallas_call(
        body, name=name,
        in_specs=[ANY] * n, out_specs=[ANY] * ng,
        out_shape=[jax.ShapeDtypeStruct((N_DEV, len(grp)) + grp[0].shape[1:], grp[0].dtype) for grp in groups],
        scratch_shapes=[pltpu.SemaphoreType.DMA((n, 7)), pltpu.SemaphoreType.DMA((n, 7)),
                        pltpu.SemaphoreType.DMA((n,))],
        compiler_params=pltpu.CompilerParams(has_side_effects=True),
    )(*[p for _, _, p in flat])


TQ = 512
TS = 256


def _layer_fwd(x, xb, wl):
    qkv = _mm(xb, wl["w_main"], out_dtype=BF16, tm=1024, tn=512, tk=D_MODEL, name="mm_qkv", n_cols=QKV_W)
    rest = _mm(xb, wl["w_main"], out_dtype=F32, tm=1024, tn=512, tk=D_MODEL, name="mm_rest", n_off=QKV_W, n_cols=REST_W)
    fgp = _mm(xb, wl["w_fg"], out_dtype=F32, tm=1024, tn=LANES, tk=D_MODEL, name="mm_fg")
    cum = _fg_fwd(fgp, wl["b_f"])
    S = x.shape[0]
    cumr = cum[:, :N_HEADS].T.reshape(N_HEADS, S // TQ, 1, TQ)
    o, y_att, lse = _attn_fwd(qkv, rest, cum, cumr, tq=TQ)
    y_conv = _conv_fwd(rest, wl["conv_w"])
    y_pool = _pool_fwd(rest, wl["pool_w"], wl["pool_scale"])
    y = jnp.concatenate([y_att, y_conv, y_pool], axis=1)
    z = _mm(y, wl["w_out"], out_dtype=F32, tm=1024, tn=512, tk=D_MODEL, name="mm_out", c=x, c_scale=ALPHA)
    out, outb, xhat, rstd = _ln_fwd(z, wl["ln_g"], wl["ln_b"], ts=TS)
    saved = dict(xb=xb, qkv=qkv, rest=rest, fgp=fgp, cum=cum, cumr=cumr, o=o, lse=lse, y=y, xhat=xhat, rstd=rstd)
    return out, outb, saved


def _layer_bwd(dout, sv, wl):
    S = dout.shape[0]
    dz, dzb, dln_g, dln_b = _ln_bwd(dout, sv["xhat"], sv["rstd"], wl["ln_g"], ts=TS)
    dy = _mm(dzb, wl["w_out"], out_dtype=F32, tm=1024, tn=512, tk=D_MODEL, name="mm_dy", bt=True)
    dw_out = _mm(sv["y"].T, dzb, out_dtype=BF16, tm=1024, tn=512, tk=2048, name="mm_dw_out")
    rest = sv["rest"]
    do, dg_att, a, delta = _attn_bwd_pre(rest, sv["o"], dy, sv["cum"], sv["lse"])
    dq, dk, dv, dcr, dcc = _attn_bwd(sv["qkv"], do, a, delta, sv["cumr"], tq=TQ)
    dcum = dcr[:, :, 0] - dcc.reshape(N_HEADS, S)
    dcum = jnp.pad(dcum.T, ((0, 0), (0, LANES - N_HEADS)))
    dfg, db_f = _fg_bwd(dcum, sv["fgp"], wl["b_f"])
    dcb, dcc_, dch, dgc, dconv_w = _conv_bwd(rest, dy, wl["conv_w"])
    dpu, dgp, dpool_w, dpool_scale = _pool_bwd(rest, dy, wl["pool_w"], wl["pool_scale"])
    dproj = jnp.concatenate([dq, dk, dv, dg_att, dcb, dcc_, dch, dgc, dpu, dgp], axis=1)
    xt = sv["xb"].T
    dw_main = _mm(xt, dproj, out_dtype=BF16, tm=1024, tn=512, tk=2048, name="mm_dw_main")
    dw_fg = _mm(xt, dfg, out_dtype=BF16, tm=1024, tn=LANES, tk=2048, name="mm_dw_fg")
    t = _mm(dfg, wl["w_fg"], out_dtype=F32, tm=1024, tn=512, tk=LANES, name="mm_dx_fg", c=dz, c_scale=ALPHA, bt=True)
    dx = _mm(dproj, wl["w_main"], out_dtype=F32, tm=1024, tn=512, tk=1792, name="mm_dx", c=t, c_scale=1.0, bt=True)
    dw_in = jnp.concatenate([dw_main[:, :FG_AT], dw_fg[:, :N_HEADS], dw_main[:, FG_AT:]], axis=1)
    grads = dict(w_in=dw_in, b_f=db_f[:, :N_HEADS], conv_w=dconv_w, pool_w=dpool_w, pool_scale=dpool_scale,
                 w_out=dw_out, ln_g=dln_g, ln_b=dln_b)
    return dx, grads


def _local_step(x0, tgt, weights):
    x, xb = x0, x0.astype(BF16)
    saved = []
    for wl in weights:
        x, xb, sv = _layer_fwd(x, xb, wl)
        saved.append(sv)
    loss, dx = _loss_head(x, tgt, ts=TS)
    grads = [None] * len(weights)
    for l in reversed(range(len(weights))):
        dx, grads[l] = _layer_bwd(dx, saved[l], weights[l])
    return loss[0, 0], dx, grads


def _layer_weights(w_full, w_o, conv_w, b_f, pool_w, pool_scale, ln_g, ln_b):
    w_main = jnp.concatenate([w_full[:, :FG_AT], w_full[:, FG_AT + N_HEADS:]], axis=1)
    w_fg = jnp.pad(w_full[:, FG_AT:FG_AT + N_HEADS], ((0, 0), (0, LANES - N_HEADS)))
    return dict(w_main=w_main, w_fg=w_fg, w_out=w_o, conv_w=conv_w, b_f=jnp.pad(b_f[None, :], ((0, 0), (0, LANES - N_HEADS))),
                pool_w=pool_w, pool_scale=pool_scale[None, :], ln_g=ln_g[None, :], ln_b=ln_b[None, :])


def kernel(x, w_in, b_f, conv_w, pool_w, pool_scale, w_out, ln_g, ln_b, loss_target, m_w_in, m_b_f, m_conv_w, m_pool_w, m_pool_scale, m_w_out, m_ln_g, m_ln_b, v_w_in, v_b_f, v_conv_w, v_pool_w, v_pool_scale, v_w_out, v_ln_g, v_ln_b):
    L, D, cols = w_in.shape
    rows_out = w_out.shape[1]
    ccols = conv_w.shape[2]

    g_in, g_out, g_conv = _all_gather([w_in.astype(BF16), w_out.astype(BF16), conv_w], name="gather_weights")
    w_full = jnp.transpose(g_in, (1, 2, 0, 3)).reshape(L, D, N_DEV * cols)
    w_o = jnp.transpose(g_out, (1, 0, 2, 3)).reshape(L, N_DEV * rows_out, D)
    cw = jnp.transpose(g_conv, (1, 2, 0, 3)).reshape(L, CONV_TAPS, N_DEV * ccols)
    weights = [_layer_weights(w_full[l], w_o[l], cw[l], b_f[l], pool_w[l], pool_scale[l], ln_g[l], ln_b[l])
               for l in range(L)]

    loss, dx, grads = _local_step(x[0], loss_target[0], weights)
    loss = lax.psum(loss, ("x", "y", "c"))

    def stacked(name):
        return jnp.stack([g[name] for g in grads])

    p_in = [jnp.transpose(g["w_in"].reshape(D, N_DEV, cols), (1, 0, 2)) for g in grads]
    p_out = [g["w_out"].reshape(N_DEV, rows_out, D) for g in grads]
    p_conv = [jnp.transpose(g["conv_w"].reshape(CONV_TAPS, N_DEV, ccols), (1, 0, 2)) for g in grads]
    r_in, r_out, r_conv = _all_to_all([p_in, p_out, p_conv], name="scatter_grads")
    small = ["b_f", "pool_w", "pool_scale", "ln_g", "ln_b"]
    gathered = _all_gather([stacked(k) for k in small], name="gather_small_grads")

    given = dict(w_in=(w_in, m_w_in, v_w_in), b_f=(b_f, m_b_f, v_b_f), conv_w=(conv_w, m_conv_w, v_conv_w),
                 pool_w=(pool_w, m_pool_w, v_pool_w), pool_scale=(pool_scale, m_pool_scale, v_pool_scale),
                 w_out=(w_out, m_w_out, v_w_out), ln_g=(ln_g, m_ln_g, v_ln_g), ln_b=(ln_b, m_ln_b, v_ln_b))
    parts = dict(w_in=r_in, w_out=r_out, conv_w=r_conv, **dict(zip(small, gathered)))
    res = {}
    for k, (w, m, v) in given.items():
        shape = w.shape
        two_d = (-1, shape[-1])
        outs = _adamw(parts[k].reshape((N_DEV,) + w.reshape(two_d).shape), w.reshape(two_d), m.reshape(two_d),
                      v.reshape(two_d), tr=128, name="adamw_" + k)
        res[k] = [t.reshape(shape) for t in outs]

    order = ["w_in", "b_f", "conv_w", "pool_w", "pool_scale", "w_out", "ln_g", "ln_b"]
    return (loss, dx[None], *[res[k][0] for k in order], *[res[k][1] for k in order],
            *[res[k][2] for k in order], *[res[k][3] for k in order])
```

```python
import jax
import jax.numpy as jnp
from jax import lax
from jax.experimental import pallas as pl
from jax.experimental.pallas import tpu as pltpu

F32 = jnp.float32
BF16 = jnp.bfloat16

N_DEV = 8
D_MODEL = 2048
N_HEADS = 8
HEAD_DIM = 128
ATT_W = N_HEADS * HEAD_DIM
CONV_W = 512
CONV_TAPS = 3
POOL_W = 512
POOL_GROUP = 128
N_POOL = POOL_W // POOL_GROUP
FG_AT = 4 * ATT_W
IN_W = FG_AT + N_HEADS + 4 * CONV_W + 2 * POOL_W
MAIN_W = IN_W - N_HEADS
LANES = 128
LN_EPS = 1e-5
DEPTH = 4
ALPHA = (2 * DEPTH) ** 0.25
SCALE = HEAD_DIM ** -0.5
NEG = -1e30

ADAM_LR, ADAM_B1, ADAM_B2, ADAM_EPS, ADAM_WD, ADAM_STEP = 0.001, 0.9, 0.999, 1e-08, 0.01, 10

QKV_W = 3 * ATT_W
REST_W = MAIN_W - QKV_W
QB, KB, VB = 0, 8, 16
GB = 0
CBB, CCB, CHB, CGB = 8, 12, 16, 20
PUB, PGB = 24, 28
VMEM_LIMIT = 48 * 1024 * 1024
ADAMW_BLOCK_ELEMS = 64 * 1024

MESH = pl.DeviceIdType.MESH
ANY = pl.BlockSpec(memory_space=pl.ANY)


def _params(semantics):
    return pltpu.CompilerParams(dimension_semantics=semantics, vmem_limit_bytes=VMEM_LIMIT)


def _sigmoid(x):
    return 1.0 / (1.0 + jnp.exp(-x))


def _mm(a, b, *, out_dtype, tm, tn, tk, name, c=None, c_scale=1.0, n_off=0, n_cols=None, bt=False):
    M, K = a.shape
    N = b.shape[0 if bt else 1] if n_cols is None else n_cols
    assert not (bt and n_off)
    dims = (((1,), (1 if bt else 0,)), ((), ()))
    tm, tn, tk = min(tm, M), min(tn, N), min(tk, K)
    assert M % tm == 0 and N % tn == 0 and K % tk == 0 and n_off % tn == 0, (a.shape, b.shape, tm, tn, tk)
    nk, joff = K // tk, n_off // tn
    has_c = c is not None

    def body(*refs):
        a_ref, b_ref = refs[0], refs[1]
        c_ref = refs[2] if has_c else None
        o_ref = refs[2 + has_c]

        def finish(r):
            if has_c:
                r = r + c_scale * c_ref[...]
            o_ref[...] = r.astype(out_dtype)

        if nk == 1:
            finish(lax.dot_general(a_ref[...], b_ref[...], dims, preferred_element_type=F32))
        else:
            acc_ref = refs[3 + has_c]
            k = pl.program_id(2)

            @pl.when(k == 0)
            def _():
                acc_ref[...] = jnp.zeros_like(acc_ref)

            acc_ref[...] += lax.dot_general(a_ref[...], b_ref[...], dims, preferred_element_type=F32)

            @pl.when(k == nk - 1)
            def _():
                finish(acc_ref[...])

    in_specs = [pl.BlockSpec((tm, tk), lambda i, j, k: (i, k)),
                pl.BlockSpec((tn, tk), lambda i, j, k: (j, k)) if bt
                else pl.BlockSpec((tk, tn), lambda i, j, k: (k, j + joff))]
    args = [a, b]
    if has_c:
        in_specs.append(pl.BlockSpec((tm, tn), lambda i, j, k: (i, j)))
        args.append(c)
    return pl.pallas_call(
        body, name=name, grid=(M // tm, N // tn, nk),
        in_specs=in_specs,
        out_specs=pl.BlockSpec((tm, tn), lambda i, j, k: (i, j)),
        out_shape=jax.ShapeDtypeStruct((M, N), out_dtype),
        scratch_shapes=[pltpu.VMEM((tm, tn), F32)] if nk > 1 else [],
        compiler_params=_params(("parallel", "parallel", "arbitrary")),
    )(*args)


def _shift_down(x, k, row):
    return jnp.where(row >= k, pltpu.roll(x, k, 0), 0.0)


def _shift_up(x, k, row):
    s = x.shape[0]
    return jnp.where(row < s - k, pltpu.roll(x, s - k, 0), 0.0)


def _fg_fwd(fgp, bf):
    S = fgp.shape[0]

    def body(f_ref, b_ref, cum_ref):
        a = f_ref[...] + b_ref[...]
        x = jnp.minimum(a, 0.0) - jnp.log(1.0 + jnp.exp(-jnp.abs(a)))
        row = lax.broadcasted_iota(jnp.int32, x.shape, 0)
        k = 1
        while k < S:
            x = x + _shift_down(x, k, row)
            k *= 2
        cum_ref[...] = x

    return pl.pallas_call(body, name="fg_fwd", out_shape=jax.ShapeDtypeStruct((S, LANES), F32),
                          compiler_params=_params(None))(fgp, bf)


def _fg_bwd(dcum, fgp, bf):
    S = fgp.shape[0]

    def body(d_ref, f_ref, b_ref, dfg_ref, dbf_ref):
        x = d_ref[...]
        row = lax.broadcasted_iota(jnp.int32, x.shape, 0)
        k = 1
        while k < S:
            x = x + _shift_up(x, k, row)
            k *= 2
        dfg = x * _sigmoid(-(f_ref[...] + b_ref[...]))
        dfg_ref[...] = dfg.astype(BF16)
        dbf_ref[...] = jnp.sum(dfg, axis=0, keepdims=True)

    return pl.pallas_call(
        body, name="fg_bwd",
        out_shape=(jax.ShapeDtypeStruct((S, LANES), BF16), jax.ShapeDtypeStruct((1, LANES), F32)),
        compiler_params=_params(None),
    )(dcum, fgp, bf)


def _colblock(S, off):
    return pl.BlockSpec((S, LANES), lambda h: (0, h + off))


def _head_col(S):
    return pl.BlockSpec((None, S, 1), lambda h: (h, 0, 0))


def _head_row(nq, tq):
    return pl.BlockSpec((None, nq, 1, tq), lambda h: (h, 0, 0, 0))


def _lane_of(ref, h):
    lane = lax.broadcasted_iota(jnp.int32, ref.shape, 1)
    return jnp.sum(jnp.where(lane == h, ref[...], 0.0), axis=1, keepdims=True)


def _attn_fwd(qkv, rest, cum, cumr, *, tq):
    S = qkv.shape[0]
    nq = S // tq

    def body(q_ref, k_ref, v_ref, g_ref, cum_ref, cumr_ref, o_ref, y_ref, lse_ref, cc):
        cc[...] = _lane_of(cum_ref, pl.program_id(0))
        tri = (lax.broadcasted_iota(jnp.int32, (tq, tq), 1) <= lax.broadcasted_iota(jnp.int32, (tq, tq), 0))

        def q_step(qi, _):
            rows = pl.ds(pl.multiple_of(qi * tq, tq), tq)
            q = q_ref[rows, :]
            ci = cc[rows, :]

            def tile(kj, carry, masked):
                m, l, acc = carry
                cols = pl.ds(pl.multiple_of(kj * tq, tq), tq)
                s = lax.dot_general(q, k_ref[cols, :], (((1,), (1,)), ((), ())), preferred_element_type=F32) * SCALE
                s = s + ci - cumr_ref[kj]
                if masked:
                    s = jnp.where(tri, s, NEG)
                m_new = jnp.maximum(m, jnp.max(s, axis=1, keepdims=True))
                a = jnp.exp(m - m_new)
                p = jnp.exp(s - m_new)
                l = a * l + jnp.sum(p, axis=1, keepdims=True)
                acc = a * acc + jnp.dot(p.astype(BF16), v_ref[cols, :], preferred_element_type=F32)
                return m_new, l, acc

            init = (jnp.full((tq, 1), NEG, F32), jnp.zeros((tq, 1), F32), jnp.zeros((tq, HEAD_DIM), F32))
            carry = lax.fori_loop(0, qi, lambda kj, c: tile(kj, c, False), init)
            m, l, acc = tile(qi, carry, True)
            o = acc / l
            g = g_ref[rows, :]
            o_ref[rows, :] = o
            y_ref[rows, :] = (o * (g * _sigmoid(g))).astype(BF16)
            lse_ref[rows, :] = m + jnp.log(l)
            return 0

        lax.fori_loop(0, nq, q_step, 0)

    return pl.pallas_call(
        body, name="attn_fwd", grid=(N_HEADS,),
        in_specs=[_colblock(S, QB), _colblock(S, KB), _colblock(S, VB), _colblock(S, GB),
                  pl.BlockSpec((S, LANES), lambda h: (0, 0)), _head_row(nq, tq)],
        out_specs=(_colblock(S, 0), _colblock(S, 0), _head_col(S)),
        out_shape=(jax.ShapeDtypeStruct((S, ATT_W), F32), jax.ShapeDtypeStruct((S, ATT_W), BF16),
                   jax.ShapeDtypeStruct((N_HEADS, S, 1), F32)),
        scratch_shapes=[pltpu.VMEM((S, 1), F32)],
        compiler_params=_params(("arbitrary",)),
    )(qkv, qkv, qkv, rest, cum, cumr)


def _attn_bwd_pre(rest, o, dy, cum, lse):
    S = o.shape[0]

    def body(g_ref, o_ref, dy_ref, cum_ref, lse_ref, do_ref, dg_ref, a_ref, dl_ref):
        g = g_ref[...]
        sg = _sigmoid(g)
        ov = o_ref[...]
        dy = dy_ref[...]
        do = dy * (g * sg)
        do_ref[...] = do.astype(BF16)
        dg_ref[...] = (dy * ov * (sg * (1.0 + g * (1.0 - sg)))).astype(BF16)
        dl_ref[...] = jnp.sum(do * ov, axis=1, keepdims=True)
        a_ref[...] = _lane_of(cum_ref, pl.program_id(0)) - lse_ref[...]

    att = jax.ShapeDtypeStruct((S, ATT_W), BF16)
    col = jax.ShapeDtypeStruct((N_HEADS, S, 1), F32)
    return pl.pallas_call(
        body, name="attn_bwd_pre", grid=(N_HEADS,),
        in_specs=[_colblock(S, GB), _colblock(S, 0), _colblock(S, 0),
                  pl.BlockSpec((S, LANES), lambda h: (0, 0)), _head_col(S)],
        out_specs=(_colblock(S, 0), _colblock(S, 0), _head_col(S), _head_col(S)),
        out_shape=(att, att, col, col),
        compiler_params=_params(("parallel",)),
    )(rest, o, dy, cum, lse)


def _attn_bwd(qkv, do, a, delta, cumr, *, tq):
    S = qkv.shape[0]
    nq = S // tq
    tdot = (((0,), (0,)), ((), ()))

    def body(q_ref, k_ref, v_ref, do_ref, a_ref, dl_ref, cumr_ref, dq_ref, dk_ref, dv_ref, dcr_ref, dcc_ref, dqa):
        dqa[...] = jnp.zeros_like(dqa)
        dcr_ref[...] = jnp.zeros_like(dcr_ref)
        tri = (lax.broadcasted_iota(jnp.int32, (tq, tq), 1) <= lax.broadcasted_iota(jnp.int32, (tq, tq), 0))

        def k_step(kj, _):
            cols = pl.ds(pl.multiple_of(kj * tq, tq), tq)
            k = k_ref[cols, :]
            v = v_ref[cols, :]
            cj = cumr_ref[kj]

            def tile(qi, carry, masked):
                dk, dv, csum = carry
                rows = pl.ds(pl.multiple_of(qi * tq, tq), tq)
                q = q_ref[rows, :]
                dot = do_ref[rows, :]
                s = lax.dot_general(q, k, (((1,), (1,)), ((), ())), preferred_element_type=F32) * SCALE
                p = jnp.exp(s + a_ref[rows, :] - cj)
                if masked:
                    p = jnp.where(tri, p, 0.0)
                dv = dv + lax.dot_general(p.astype(BF16), dot, tdot, preferred_element_type=F32)
                dp = lax.dot_general(dot, v, (((1,), (1,)), ((), ())), preferred_element_type=F32)
                ds = p * (dp - dl_ref[rows, :])
                dcr_ref[rows, :] += jnp.sum(ds, axis=1, keepdims=True)
                csum = csum + jnp.sum(ds, axis=0, keepdims=True)
                dsb = (ds * SCALE).astype(BF16)
                dqa[rows, :] += jnp.dot(dsb, k, preferred_element_type=F32)
                dk = dk + lax.dot_general(dsb, q, tdot, preferred_element_type=F32)
                return dk, dv, csum

            init = (jnp.zeros((tq, HEAD_DIM), F32), jnp.zeros((tq, HEAD_DIM), F32), jnp.zeros((1, tq), F32))
            carry = tile(kj, init, True)
            dk, dv, csum = lax.fori_loop(kj + 1, nq, lambda qi, c: tile(qi, c, False), carry)
            dk_ref[cols, :] = dk.astype(BF16)
            dv_ref[cols, :] = dv.astype(BF16)
            dcc_ref[kj] = csum
            return 0

        lax.fori_loop(0, nq, k_step, 0)
        dq_ref[...] = dqa[...].astype(BF16)

    att = jax.ShapeDtypeStruct((S, ATT_W), BF16)
    return pl.pallas_call(
        body, name="attn_bwd", grid=(N_HEADS,),
        in_specs=[_colblock(S, QB), _colblock(S, KB), _colblock(S, VB), _colblock(S, 0),
                  _head_col(S), _head_col(S), _head_row(nq, tq)],
        out_specs=(_colblock(S, 0), _colblock(S, 0), _colblock(S, 0), _head_col(S), _head_row(nq, tq)),
        out_shape=(att, att, att, jax.ShapeDtypeStruct((N_HEADS, S, 1), F32),
                   jax.ShapeDtypeStruct((N_HEADS, nq, 1, tq), F32)),
        scratch_shapes=[pltpu.VMEM((S, HEAD_DIM), F32)],
        compiler_params=_params(("arbitrary",)),
    )(qkv, qkv, qkv, do, a, delta, cumr)


def _conv_taps(u, w_ref, row):
    return (w_ref[0:1, :] * _shift_down(u, 2, row) + w_ref[1:2, :] * _shift_down(u, 1, row)) + w_ref[2:3, :] * u


def _conv_fwd(rest, cw):
    S = rest.shape[0]

    def body(cb_ref, cc_ref, ch_ref, g_ref, w_ref, y_ref):
        row = lax.broadcasted_iota(jnp.int32, (S, LANES), 0)
        g = g_ref[...]
        y = _conv_taps(cc_ref[...] * ch_ref[...], w_ref, row)
        y_ref[...] = ((cb_ref[...] * y) * (g * _sigmoid(g))).astype(BF16)

    return pl.pallas_call(
        body, name="conv_fwd", grid=(CONV_W // LANES,),
        in_specs=[_colblock(S, CBB), _colblock(S, CCB), _colblock(S, CHB), _colblock(S, CGB),
                  pl.BlockSpec((CONV_TAPS, LANES), lambda j: (0, j))],
        out_specs=_colblock(S, 0),
        out_shape=jax.ShapeDtypeStruct((S, CONV_W), BF16),
        compiler_params=_params(("parallel",)),
    )(rest, rest, rest, rest, cw)


def _conv_bwd(rest, dy, cw):
    S = rest.shape[0]

    def body(cb_ref, cc_ref, ch_ref, g_ref, dy_ref, w_ref, dcb_ref, dcc_ref, dch_ref, dg_ref, dw_ref):
        row = lax.broadcasted_iota(jnp.int32, (S, LANES), 0)
        g = g_ref[...]
        sg = _sigmoid(g)
        silu = g * sg
        cb, cc, ch, dy = cb_ref[...], cc_ref[...], ch_ref[...], dy_ref[...]
        u = cc * ch
        u1 = _shift_down(u, 1, row)
        u2 = _shift_down(u, 2, row)
        y = (w_ref[0:1, :] * u2 + w_ref[1:2, :] * u1) + w_ref[2:3, :] * u
        dcb_ref[...] = (dy * silu * y).astype(BF16)
        dg_ref[...] = (dy * (cb * y) * (sg * (1.0 + g * (1.0 - sg)))).astype(BF16)
        dyv = dy * silu * cb
        du = w_ref[2:3, :] * dyv + w_ref[1:2, :] * _shift_up(dyv, 1, row) + w_ref[0:1, :] * _shift_up(dyv, 2, row)
        dcc_ref[...] = (du * ch).astype(BF16)
        dch_ref[...] = (du * cc).astype(BF16)
        dw_ref[0:1, :] = jnp.sum(dyv * u2, axis=0, keepdims=True)
        dw_ref[1:2, :] = jnp.sum(dyv * u1, axis=0, keepdims=True)
        dw_ref[2:3, :] = jnp.sum(dyv * u, axis=0, keepdims=True)

    act = jax.ShapeDtypeStruct((S, CONV_W), BF16)
    return pl.pallas_call(
        body, name="conv_bwd", grid=(CONV_W // LANES,),
        in_specs=[_colblock(S, CBB), _colblock(S, CCB), _colblock(S, CHB), _colblock(S, CGB),
                  _colblock(S, ATT_W // LANES), pl.BlockSpec((CONV_TAPS, LANES), lambda j: (0, j))],
        out_specs=(_colblock(S, 0),) * 4 + (pl.BlockSpec((CONV_TAPS, LANES), lambda j: (0, j)),),
        out_shape=(act, act, act, act, jax.ShapeDtypeStruct((CONV_TAPS, CONV_W), F32)),
        compiler_params=_params(("parallel",)),
    )(rest, rest, rest, rest, dy, cw)


def _pool_z(u, grp, row):
    s2 = u + _shift_down(u, 1, row)
    s4 = s2 + _shift_down(s2, 2, row)
    s8 = s4 + _shift_down(s4, 4, row)
    s16 = s8 + _shift_down(s8, 8, row)
    sw = jnp.where(grp == 0, s2, jnp.where(grp == 1, s4, jnp.where(grp == 2, s8, s16)))
    return sw / _pool_count(grp, row) - u


def _pool_count(grp, row):
    return jnp.minimum(row + 1, 2 << grp).astype(F32)


def _pool_fwd(rest, pw, scale):
    S = rest.shape[0]

    def body(u_ref, g_ref, pw_ref, sc_ref, y_ref):
        grp = pl.program_id(0)
        row = lax.broadcasted_iota(jnp.int32, (S, LANES), 0)
        z = _pool_z(u_ref[...], grp, row)
        zp = jnp.dot(z.astype(BF16), pw_ref[...].astype(BF16), preferred_element_type=F32)
        g = g_ref[...]
        y_ref[...] = ((zp * sc_ref[...]) * (g * _sigmoid(g))).astype(BF16)

    return pl.pallas_call(
        body, name="pool_fwd", grid=(N_POOL,),
        in_specs=[_colblock(S, PUB), _colblock(S, PGB),
                  pl.BlockSpec((None, POOL_GROUP, POOL_GROUP), lambda j: (j, 0, 0)),
                  pl.BlockSpec((1, LANES), lambda j: (0, j))],
        out_specs=_colblock(S, 0),
        out_shape=jax.ShapeDtypeStruct((S, POOL_W), BF16),
        compiler_params=_params(("parallel",)),
    )(rest, rest, pw, scale)


def _pool_bwd(rest, dy, pw, scale):
    S = rest.shape[0]

    def body(u_ref, g_ref, dy_ref, pw_ref, sc_ref, du_ref, dg_ref, dpw_ref, dsc_ref):
        grp = pl.program_id(0)
        row = lax.broadcasted_iota(jnp.int32, (S, LANES), 0)
        z = _pool_z(u_ref[...], grp, row).astype(BF16)
        pwb = pw_ref[...].astype(BF16)
        zp = jnp.dot(z, pwb, preferred_element_type=F32)
        g = g_ref[...]
        sg = _sigmoid(g)
        silu = g * sg
        dy = dy_ref[...]
        sc = sc_ref[...]
        dsc_ref[...] = jnp.sum(dy * silu * zp, axis=0, keepdims=True)
        dg_ref[...] = (dy * (zp * sc) * (sg * (1.0 + g * (1.0 - sg)))).astype(BF16)
        dzp = (dy * silu * sc).astype(BF16)
        dpw_ref[...] = lax.dot_general(z, dzp, (((0,), (0,)), ((), ())), preferred_element_type=F32)
        dz = lax.dot_general(dzp, pwb, (((1,), (1,)), ((), ())), preferred_element_type=F32)
        f1 = dz / _pool_count(grp, row)
        f2 = f1 + _shift_up(f1, 1, row)
        f4 = f2 + _shift_up(f2, 2, row)
        f8 = f4 + _shift_up(f4, 4, row)
        f16 = f8 + _shift_up(f8, 8, row)
        fw = jnp.where(grp == 0, f2, jnp.where(grp == 1, f4, jnp.where(grp == 2, f8, f16)))
        du_ref[...] = (fw - dz).astype(BF16)

    act = jax.ShapeDtypeStruct((S, POOL_W), BF16)
    return pl.pallas_call(
        body, name="pool_bwd", grid=(N_POOL,),
        in_specs=[_colblock(S, PUB), _colblock(S, PGB), _colblock(S, (ATT_W + CONV_W) // LANES),
                  pl.BlockSpec((None, POOL_GROUP, POOL_GROUP), lambda j: (j, 0, 0)),
                  pl.BlockSpec((1, LANES), lambda j: (0, j))],
        out_specs=(_colblock(S, 0), _colblock(S, 0),
                   pl.BlockSpec((None, POOL_GROUP, POOL_GROUP), lambda j: (j, 0, 0)),
                   pl.BlockSpec((1, LANES), lambda j: (0, j))),
        out_shape=(act, act, jax.ShapeDtypeStruct((N_POOL, POOL_GROUP, POOL_GROUP), F32),
                   jax.ShapeDtypeStruct((1, POOL_W), F32)),
        compiler_params=_params(("parallel",)),
    )(rest, rest, dy, pw, scale)


def _ln_fwd(z, g, b, *, ts):
    S, D = z.shape

    def body(z_ref, g_ref, b_ref, o_ref, ob_ref, xh_ref, rs_ref):
        zz = z_ref[...]
        mu = jnp.mean(zz, axis=1, keepdims=True)
        zc = zz - mu
        rstd = lax.rsqrt(jnp.mean(zc * zc, axis=1, keepdims=True) + LN_EPS)
        xh = zc * rstd
        out = xh * g_ref[...] + b_ref[...]
        o_ref[...] = out
        ob_ref[...] = out.astype(BF16)
        xh_ref[...] = xh
        rs_ref[...] = rstd

    rowblk = pl.BlockSpec((ts, D), lambda i: (i, 0))
    vec = pl.BlockSpec((1, D), lambda i: (0, 0))
    return pl.pallas_call(
        body, name="ln_fwd", grid=(S // ts,),
        in_specs=[rowblk, vec, vec],
        out_specs=(rowblk, rowblk, rowblk, pl.BlockSpec((ts, 1), lambda i: (i, 0))),
        out_shape=(jax.ShapeDtypeStruct((S, D), F32), jax.ShapeDtypeStruct((S, D), BF16),
                   jax.ShapeDtypeStruct((S, D), F32), jax.ShapeDtypeStruct((S, 1), F32)),
        compiler_params=_params(("parallel",)),
    )(z, g, b)


def _ln_bwd(dout, xhat, rstd, g, *, ts):
    S, D = dout.shape

    def body(d_ref, xh_ref, rs_ref, g_ref, dz_ref, dzb_ref, dg_ref, db_ref):
        @pl.when(pl.program_id(0) == 0)
        def _():
            dg_ref[...] = jnp.zeros_like(dg_ref)
            db_ref[...] = jnp.zeros_like(db_ref)

        d = d_ref[...]
        xh = xh_ref[...]
        dxh = d * g_ref[...]
        m1 = jnp.mean(dxh, axis=1, keepdims=True)
        m2 = jnp.mean(dxh * xh, axis=1, keepdims=True)
        dz = rs_ref[...] * (dxh - m1 - xh * m2)
        dz_ref[...] = dz
        dzb_ref[...] = dz.astype(BF16)
        dg_ref[...] += jnp.sum(d * xh, axis=0, keepdims=True)
        db_ref[...] += jnp.sum(d, axis=0, keepdims=True)

    rowblk = pl.BlockSpec((ts, D), lambda i: (i, 0))
    vec = pl.BlockSpec((1, D), lambda i: (0, 0))
    return pl.pallas_call(
        body, name="ln_bwd", grid=(S // ts,),
        in_specs=[rowblk, rowblk, pl.BlockSpec((ts, 1), lambda i: (i, 0)), vec],
        out_specs=(rowblk, rowblk, vec, vec),
        out_shape=(jax.ShapeDtypeStruct((S, D), F32), jax.ShapeDtypeStruct((S, D), BF16),
                   jax.ShapeDtypeStruct((1, D), F32), jax.ShapeDtypeStruct((1, D), F32)),
        compiler_params=_params(("arbitrary",)),
    )(dout, xhat, rstd, g)


def _loss_head(y, tgt, *, ts):
    S, D = y.shape

    def body(y_ref, t_ref, l_ref, dy_ref):
        @pl.when(pl.program_id(0) == 0)
        def _():
            l_ref[...] = jnp.zeros_like(l_ref)

        e = y_ref[...] - t_ref[...]
        dy_ref[...] = e * (1.0 / D)
        rowloss = jnp.sum(e * e, axis=1, keepdims=True) * (0.5 / D)
        l_ref[...] += jnp.sum(rowloss, axis=0, keepdims=True)

    rowblk = pl.BlockSpec((ts, D), lambda i: (i, 0))
    return pl.pallas_call(
        body, name="loss_head", grid=(S // ts,),
        in_specs=[rowblk, rowblk],
        out_specs=(pl.BlockSpec((1, 1), lambda i: (0, 0)), rowblk),
        out_shape=(jax.ShapeDtypeStruct((1, 1), F32), jax.ShapeDtypeStruct((S, D), F32)),
        compiler_params=_params(("arbitrary",)),
    )(y, tgt)


def _adamw(parts, w, m, v, *, name):
    R, C = w.shape
    npart = len(parts)
    rp = R // npart
    tr = min(ADAMW_BLOCK_ELEMS // (LANES * pl.cdiv(C, LANES)), rp)
    assert R % npart == 0 and rp % tr == 0
    steps = rp // tr
    c1 = 1.0 - ADAM_B1 ** ADAM_STEP
    c2 = 1.0 - ADAM_B2 ** ADAM_STEP

    def body(*refs):
        p_refs = refs[:npart]
        w_ref, m_ref, v_ref, g_ref, d_ref, nm_ref, nv_ref = refs[npart:]
        for l, p_ref in enumerate(p_refs):
            @pl.when(pl.program_id(0) == l)
            def _():
                g = p_ref[0].astype(F32)
                for j in range(1, N_DEV):
                    g = g + p_ref[j].astype(F32)
                nm = ADAM_B1 * m_ref[...] + (1.0 - ADAM_B1) * g
                nv = ADAM_B2 * v_ref[...] + (1.0 - ADAM_B2) * (g * g)
                g_ref[...] = g
                nm_ref[...] = nm
                nv_ref[...] = nv
                d_ref[...] = -ADAM_LR * ((nm / c1) / (jnp.sqrt(nv / c2) + ADAM_EPS) + ADAM_WD * w_ref[...])

    def part_spec(l):
        return pl.BlockSpec((N_DEV, tr, C), lambda p, i: (0, jnp.where(p == l, i, 0), 0))

    blk = pl.BlockSpec((tr, C), lambda p, i: (p * steps + i, 0))
    out = jax.ShapeDtypeStruct((R, C), F32)
    return pl.pallas_call(
        body, name=name, grid=(npart, steps),
        in_specs=[part_spec(l) for l in range(npart)] + [blk, blk, blk],
        out_specs=(blk, blk, blk, blk),
        out_shape=(out, out, out, out),
        compiler_params=_params(("arbitrary", "arbitrary")),
    )(*parts, w, m, v)


def _my_place():
    return lax.axis_index("x"), lax.axis_index("y"), lax.axis_index("c")


def _all_gather(blocks, *, name):
    n = len(blocks)

    def body(*refs):
        ins, outs = refs[:n], refs[n:2 * n]
        send_sems, recv_sems, local_sems = refs[2 * n:]
        x, y, c = _my_place()
        me, sibling = (x, y, c), (x, y, 1 - c)
        chips = [(1 - x, y), (x, 1 - y), (1 - x, 1 - y)]

        def slot(a, place):
            px, py, pc = place
            return outs[a].at[4 * px + 2 * py + pc]

        def copy(a, k, block, to, src=None):
            return pltpu.make_async_remote_copy(
                src_ref=slot(a, block) if src is None else src, dst_ref=slot(a, block),
                send_sem=send_sems.at[a, k], recv_sem=recv_sems.at[a, k], device_id=to, device_id_type=MESH)

        mine = [pltpu.make_async_copy(ins[a], slot(a, me), local_sems.at[a]) for a in range(n)]
        for cp in mine:
            cp.start()
        first = []
        for a in range(n):
            first.append(copy(a, 0, me, sibling, src=ins[a]))
            first += [copy(a, 1 + j, me, (*chip, c), src=ins[a]) for j, chip in enumerate(chips)]
        for cp in first:
            cp.start()
        passed = []
        for j, chip in enumerate(chips):
            for a in range(n):
                copy(a, 1 + j, (*chip, c), me).wait_recv()
                fwd = copy(a, 4 + j, (*chip, c), sibling)
                fwd.start()
                passed.append(fwd)
        for a in range(n):
            copy(a, 0, sibling, me).wait_recv()
            for j, chip in enumerate(chips):
                copy(a, 4 + j, (*chip, 1 - c), me).wait_recv()
        for cp in first + passed:
            cp.wait_send()
        for cp in mine:
            cp.wait()

    return pl.pallas_call(
        body, name=name,
        in_specs=[ANY] * n, out_specs=[ANY] * n,
        out_shape=[jax.ShapeDtypeStruct((N_DEV,) + b.shape, b.dtype) for b in blocks],
        scratch_shapes=[pltpu.SemaphoreType.DMA((n, 7)), pltpu.SemaphoreType.DMA((n, 7)),
                        pltpu.SemaphoreType.DMA((n,))],
        compiler_params=pltpu.CompilerParams(has_side_effects=True),
    )(*blocks)


HBM = pl.BlockSpec(memory_space=pltpu.HBM)
SEM = pl.BlockSpec(memory_space=pltpu.SEMAPHORE)
DATAFLOW = pltpu.SideEffectType.DATAFLOW_SIDE_EFFECTING
N_PEER = N_DEV - 1


def _xchg_copies(src, land, send_sem, recv_sem, gather):
    x, y, c = _my_place()
    me = 4 * x + 2 * y + c
    copies = []
    for k in range(N_PEER):
        px, py, pc = x ^ (((k + 1) >> 2) & 1), y ^ (((k + 1) >> 1) & 1), c ^ ((k + 1) & 1)
        peer = 4 * px + 2 * py + pc
        for a in range(len(src)):
            i = a * N_PEER + k
            out = src[a] if gather else src[a].at[peer]
            sent, landed = [pltpu.make_async_remote_copy(
                src_ref=out, dst_ref=land[a].at[slab], send_sem=send_sem.at[i], recv_sem=recv_sem.at[i],
                device_id=(px, py, pc), device_id_type=MESH) for slab in (me, peer)]
            copies.append((sent, landed))
    return copies


def _xchg_start(srcs, *, gather, name):
    n = len(srcs)
    lands = [lax.empty(((N_DEV,) + v.shape) if gather else v.shape, v.dtype) for v in srcs]

    def body(*refs):
        src, land = refs[:n], refs[n:2 * n]
        send_sem, recv_sem = refs[2 * n], refs[2 * n + 1]
        token = refs[-1]
        for sent, _ in _xchg_copies(src, land, send_sem, recv_sem, gather):
            sent.start()
        token[...] = jnp.zeros_like(token)

    sems = pltpu.SemaphoreType.DMA((n * N_PEER,))
    outs = pl.pallas_call(
        body, name=name,
        out_shape=(sems, sems, *[pltpu.HBM(v.shape, v.dtype) for v in srcs + lands],
                   jax.ShapeDtypeStruct((8, LANES), F32)),
        in_specs=[HBM] * (2 * n),
        out_specs=(SEM, SEM, *[HBM] * (2 * n), pl.BlockSpec(memory_space=pltpu.VMEM)),
        input_output_aliases={i: 2 + i for i in range(2 * n)},
        compiler_params=pltpu.CompilerParams(has_side_effects=DATAFLOW),
    )(*[pltpu.with_memory_space_constraint(v, pltpu.HBM) for v in srcs + lands])
    return dict(send=outs[0], recv=outs[1], srcs=list(outs[2:2 + n]), lands=list(outs[2 + n:2 + 2 * n]),
                token=outs[-1][0, 0], gather=gather)


def _xchg_wait(h, after, *, name):
    n = len(h["srcs"])
    gather = h["gather"]

    def body(*refs):
        src, land = refs[:n], refs[n:2 * n]
        send_sem, recv_sem = refs[2 * n], refs[2 * n + 1]
        for sent, landed in _xchg_copies(src, land, send_sem, recv_sem, gather):
            sent.wait_send()
            landed.wait_recv()

    thru = h["srcs"] + h["lands"]
    outs = pl.pallas_call(
        body, name=name,
        out_shape=[pltpu.HBM(v.shape, v.dtype) for v in thru],
        in_specs=[HBM] * (2 * n) + [SEM, SEM, ANY],
        out_specs=[HBM] * (2 * n),
        input_output_aliases={i: i for i in range(2 * n)},
        compiler_params=pltpu.CompilerParams(has_side_effects=DATAFLOW),
    )(*thru, h["send"], h["recv"], after)
    return list(outs[:n]), list(outs[n:])


def _with_own(land, own, me):
    slot = lax.broadcasted_iota(jnp.int32, land.shape, 0)
    return jnp.where(slot == me, own if own.ndim == land.ndim else own[None], land)


TQ = 512
TS = 256


def _layer_fwd(x, xb, wl):
    qkv = _mm(xb, wl["w_main"], out_dtype=BF16, tm=1024, tn=512, tk=D_MODEL, name="mm_qkv", n_cols=QKV_W)
    rest = _mm(xb, wl["w_main"], out_dtype=F32, tm=1024, tn=512, tk=D_MODEL, name="mm_rest", n_off=QKV_W, n_cols=REST_W)
    fgp = _mm(xb, wl["w_fg"], out_dtype=F32, tm=1024, tn=LANES, tk=D_MODEL, name="mm_fg")
    cum = _fg_fwd(fgp, wl["b_f"])
    S = x.shape[0]
    cumr = cum[:, :N_HEADS].T.reshape(N_HEADS, S // TQ, 1, TQ)
    o, y_att, lse = _attn_fwd(qkv, rest, cum, cumr, tq=TQ)
    y_conv = _conv_fwd(rest, wl["conv_w"])
    y_pool = _pool_fwd(rest, wl["pool_w"], wl["pool_scale"])
    y = jnp.concatenate([y_att, y_conv, y_pool], axis=1)
    z = _mm(y, wl["w_out"], out_dtype=F32, tm=1024, tn=512, tk=D_MODEL, name="mm_out", c=x, c_scale=ALPHA)
    out, outb, xhat, rstd = _ln_fwd(z, wl["ln_g"], wl["ln_b"], ts=TS)
    saved = dict(xb=xb, qkv=qkv, rest=rest, fgp=fgp, cum=cum, cumr=cumr, o=o, lse=lse, y=y, xhat=xhat, rstd=rstd)
    return out, outb, saved


def _layer_bwd(dout, sv, wl):
    S = dout.shape[0]
    dz, dzb, dln_g, dln_b = _ln_bwd(dout, sv["xhat"], sv["rstd"], wl["ln_g"], ts=TS)
    dy = _mm(dzb, wl["w_out"], out_dtype=F32, tm=1024, tn=512, tk=D_MODEL, name="mm_dy", bt=True)
    dw_out = _mm(sv["y"].T, dzb, out_dtype=BF16, tm=1024, tn=512, tk=2048, name="mm_dw_out")
    rest = sv["rest"]
    do, dg_att, a, delta = _attn_bwd_pre(rest, sv["o"], dy, sv["cum"], sv["lse"])
    dq, dk, dv, dcr, dcc = _attn_bwd(sv["qkv"], do, a, delta, sv["cumr"], tq=TQ)
    dcum = dcr[:, :, 0] - dcc.reshape(N_HEADS, S)
    dcum = jnp.pad(dcum.T, ((0, 0), (0, LANES - N_HEADS)))
    dfg, db_f = _fg_bwd(dcum, sv["fgp"], wl["b_f"])
    dcb, dcc_, dch, dgc, dconv_w = _conv_bwd(rest, dy, wl["conv_w"])
    dpu, dgp, dpool_w, dpool_scale = _pool_bwd(rest, dy, wl["pool_w"], wl["pool_scale"])
    dproj = jnp.concatenate([dq, dk, dv, dg_att, dcb, dcc_, dch, dgc, dpu, dgp], axis=1)
    xt = sv["xb"].T
    dw_main = _mm(xt, dproj, out_dtype=BF16, tm=1024, tn=512, tk=2048, name="mm_dw_main")
    dw_fg = _mm(xt, dfg, out_dtype=BF16, tm=1024, tn=LANES, tk=2048, name="mm_dw_fg")
    t = _mm(dfg, wl["w_fg"], out_dtype=F32, tm=1024, tn=512, tk=LANES, name="mm_dx_fg", c=dz, c_scale=ALPHA, bt=True)
    dx = _mm(dproj, wl["w_main"], out_dtype=F32, tm=1024, tn=512, tk=1792, name="mm_dx", c=t, c_scale=1.0, bt=True)
    dw_in = jnp.concatenate([dw_main[:, :FG_AT], dw_fg[:, :N_HEADS], dw_main[:, FG_AT:]], axis=1)
    grads = dict(w_in=dw_in, b_f=db_f[:, :N_HEADS], conv_w=dconv_w, pool_w=dpool_w, pool_scale=dpool_scale,
                 w_out=dw_out, ln_g=dln_g, ln_b=dln_b)
    return dx, grads


def _layer_weights(w_full, w_o, conv_w, b_f, pool_w, pool_scale, ln_g, ln_b):
    w_main = jnp.concatenate([w_full[:, :FG_AT], w_full[:, FG_AT + N_HEADS:]], axis=1)
    w_fg = jnp.pad(w_full[:, FG_AT:FG_AT + N_HEADS], ((0, 0), (0, LANES - N_HEADS)))
    return dict(w_main=w_main, w_fg=w_fg, w_out=w_o, conv_w=conv_w, b_f=jnp.pad(b_f[None, :], ((0, 0), (0, LANES - N_HEADS))),
                pool_w=pool_w, pool_scale=pool_scale[None, :], ln_g=ln_g[None, :], ln_b=ln_b[None, :])


def kernel(x, w_in, b_f, conv_w, pool_w, pool_scale, w_out, ln_g, ln_b, loss_target, m_w_in, m_b_f, m_conv_w, m_pool_w, m_pool_scale, m_w_out, m_ln_g, m_ln_b, v_w_in, v_b_f, v_conv_w, v_pool_w, v_pool_scale, v_w_out, v_ln_g, v_ln_b):
    L, D, cols = w_in.shape
    rows_out = w_out.shape[1]
    ccols = conv_w.shape[2]
    mx, my, mc = _my_place()
    me = 4 * mx + 2 * my + mc

    w_in_b, w_out_b = w_in.astype(BF16), w_out.astype(BF16)
    gathers = [_xchg_start([w_in_b[l], w_out_b[l], conv_w[l]], gather=True, name=f"gather_start_{l}") for l in range(L)]
    started = gathers[0]["token"]
    for h in gathers[1:]:
        started = started + h["token"]

    x0 = x[0]
    xl, xb = x0, (x0 + started).astype(BF16)
    weights, saved = [], []
    for l in range(L):
        (own_in, own_out, own_cw), (g_in, g_out, g_cw) = _xchg_wait(gathers[l], xb, name=f"gather_wait_{l}")
        w_full = jnp.transpose(_with_own(g_in, own_in, me), (1, 0, 2)).reshape(D, N_DEV * cols)
        w_o = _with_own(g_out, own_out, me).reshape(N_DEV * rows_out, D)
        cw = jnp.transpose(_with_own(g_cw, own_cw, me), (1, 0, 2)).reshape(CONV_TAPS, N_DEV * ccols)
        wl = _layer_weights(w_full, w_o, cw, b_f[l], pool_w[l], pool_scale[l], ln_g[l], ln_b[l])
        xl, xb, sv = _layer_fwd(xl, xb, wl)
        weights.append(wl)
        saved.append(sv)

    loss, dx = _loss_head(xl, loss_target[0], ts=TS)
    loss = lax.psum(loss[0, 0], ("x", "y", "c"))

    grads, scatters = [None] * L, [None] * L
    started = None
    for l in reversed(range(L)):
        wl = weights[l]
        if started is not None:
            wl = dict(wl, ln_g=wl["ln_g"] + started)
        dx, g = _layer_bwd(dx, saved[l], wl)
        grads[l] = g
        by_dev = [g["w_in"].reshape(D, N_DEV, cols), g["w_out"].reshape(N_DEV, rows_out, D),
                  g["conv_w"].reshape(CONV_TAPS, N_DEV, ccols)]
        slabs = [jnp.transpose(by_dev[0], (1, 0, 2)), by_dev[1], jnp.transpose(by_dev[2], (1, 0, 2))]
        scatters[l] = _xchg_start(slabs, gather=False, name=f"scatter_start_{l}")
        started = scatters[l]["token"]

    r_in, r_out, r_conv = [None] * L, [None] * L, [None] * L
    for l in reversed(range(L)):
        (s_in, s_out, s_cw), (a_in, a_out, a_cw) = _xchg_wait(scatters[l], dx, name=f"scatter_wait_{l}")
        r_in[l] = _with_own(a_in, s_in, me)
        r_out[l] = _with_own(a_out, s_out, me)
        r_conv[l] = _with_own(a_cw, s_cw, me)

    def stacked(name):
        return jnp.stack([g[name] for g in grads])

    small = ["b_f", "pool_w", "pool_scale", "ln_g", "ln_b"]
    gathered = _all_gather([stacked(k) for k in small], name="gather_small_grads")

    given = dict(w_in=(w_in, m_w_in, v_w_in), b_f=(b_f, m_b_f, v_b_f), conv_w=(conv_w, m_conv_w, v_conv_w),
                 pool_w=(pool_w, m_pool_w, v_pool_w), pool_scale=(pool_scale, m_pool_scale, v_pool_scale),
                 w_out=(w_out, m_w_out, v_w_out), ln_g=(ln_g, m_ln_g, v_ln_g), ln_b=(ln_b, m_ln_b, v_ln_b))
    parts = dict(w_in=r_in, w_out=r_out, conv_w=[jnp.stack(r_conv, axis=1)], **{k: [t] for k, t in zip(small, gathered)})
    res = {}
    for k, (w, m, v) in given.items():
        shape = w.shape
        w2, m2, v2 = (t.reshape(-1, shape[-1]) for t in (w, m, v))
        ps = [p.reshape(N_DEV, w2.shape[0] // len(parts[k]), shape[-1]) for p in parts[k]]
        outs = _adamw(ps, w2, m2, v2, name="adamw_" + k)
        res[k] = [t.reshape(shape) for t in outs]

    order = ["w_in", "b_f", "conv_w", "pool_w", "pool_scale", "w_out", "ln_g", "ln_b"]
    return (loss, dx[None], *[res[k][0] for k in order], *[res[k][1] for k in order],
            *[res[k][2] for k in order], *[res[k][3] for k in order])
```

```python
import jax
import jax.numpy as jnp
from jax import lax
from jax.experimental import pallas as pl
from jax.experimental.pallas import tpu as pltpu

F32 = jnp.float32
BF16 = jnp.bfloat16

N_DEV = 8
D_MODEL = 2048
N_HEADS = 8
HEAD_DIM = 128
ATT_W = N_HEADS * HEAD_DIM
CONV_W = 512
CONV_TAPS = 3
POOL_W = 512
POOL_GROUP = 128
N_POOL = POOL_W // POOL_GROUP
FG_AT = 4 * ATT_W
IN_W = FG_AT + N_HEADS + 4 * CONV_W + 2 * POOL_W
MAIN_W = IN_W - N_HEADS
LANES = 128
LN_EPS = 1e-5
DEPTH = 4
ALPHA = (2 * DEPTH) ** 0.25
SCALE = HEAD_DIM ** -0.5
NEG = -1e30

ADAM_LR, ADAM_B1, ADAM_B2, ADAM_EPS, ADAM_WD, ADAM_STEP = 0.001, 0.9, 0.999, 1e-08, 0.01, 10

QKV_W = 3 * ATT_W
REST_W = MAIN_W - QKV_W
QB, KB, VB = 0, 8, 16
GB = 0
CBB, CCB, CHB, CGB = 8, 12, 16, 20
PUB, PGB = 24, 28
VMEM_LIMIT = 48 * 1024 * 1024
ADAMW_BLOCK_ELEMS = 64 * 1024

MESH = pl.DeviceIdType.MESH
ANY = pl.BlockSpec(memory_space=pl.ANY)


def _params(semantics):
    return pltpu.CompilerParams(dimension_semantics=semantics, vmem_limit_bytes=VMEM_LIMIT)


def _sigmoid(x):
    return 1.0 / (1.0 + jnp.exp(-x))


def _mm(a, b, *, out_dtype, tm, tn, tk, name, c=None, c_scale=1.0, n_off=0, n_cols=None, bt=False, at=False):
    M, K = a.shape[::-1] if at else a.shape
    N = b.shape[0 if bt else 1] if n_cols is None else n_cols
    assert not (bt and n_off)
    dims = (((0 if at else 1,), (1 if bt else 0,)), ((), ()))
    tm, tn, tk = min(tm, M), min(tn, N), min(tk, K)
    assert M % tm == 0 and N % tn == 0 and K % tk == 0 and n_off % tn == 0, (a.shape, b.shape, tm, tn, tk)
    nk, joff = K // tk, n_off // tn
    has_c = c is not None

    def body(*refs):
        a_ref, b_ref = refs[0], refs[1]
        c_ref = refs[2] if has_c else None
        o_ref = refs[2 + has_c]

        def finish(r):
            if has_c:
                r = r + c_scale * c_ref[...]
            o_ref[...] = r.astype(out_dtype)

        if nk == 1:
            finish(lax.dot_general(a_ref[...], b_ref[...], dims, preferred_element_type=F32))
        else:
            acc_ref = refs[3 + has_c]
            k = pl.program_id(2)

            @pl.when(k == 0)
            def _():
                acc_ref[...] = jnp.zeros_like(acc_ref)

            acc_ref[...] += lax.dot_general(a_ref[...], b_ref[...], dims, preferred_element_type=F32)

            @pl.when(k == nk - 1)
            def _():
                finish(acc_ref[...])

    in_specs = [pl.BlockSpec((tk, tm), lambda i, j, k: (k, i)) if at else pl.BlockSpec((tm, tk), lambda i, j, k: (i, k)),
                pl.BlockSpec((tn, tk), lambda i, j, k: (j, k)) if bt
                else pl.BlockSpec((tk, tn), lambda i, j, k: (k, j + joff))]
    args = [a, b]
    if has_c:
        in_specs.append(pl.BlockSpec((tm, tn), lambda i, j, k: (i, j)))
        args.append(c)
    return pl.pallas_call(
        body, name=name, grid=(M // tm, N // tn, nk),
        in_specs=in_specs,
        out_specs=pl.BlockSpec((tm, tn), lambda i, j, k: (i, j)),
        out_shape=jax.ShapeDtypeStruct((M, N), out_dtype),
        scratch_shapes=[pltpu.VMEM((tm, tn), F32)] if nk > 1 else [],
        compiler_params=_params(("parallel", "parallel", "arbitrary")),
    )(*args)


def _shift_down(x, k, row):
    return jnp.where(row >= k, pltpu.roll(x, k, 0), 0.0)


def _shift_up(x, k, row):
    s = x.shape[0]
    return jnp.where(row < s - k, pltpu.roll(x, s - k, 0), 0.0)


def _fg_fwd(fgp, bf):
    S = fgp.shape[0]

    def body(f_ref, b_ref, cum_ref):
        a = f_ref[...] + b_ref[...]
        x = jnp.minimum(a, 0.0) - jnp.log(1.0 + jnp.exp(-jnp.abs(a)))
        row = lax.broadcasted_iota(jnp.int32, x.shape, 0)
        k = 1
        while k < S:
            x = x + _shift_down(x, k, row)
            k *= 2
        cum_ref[...] = x

    return pl.pallas_call(body, name="fg_fwd", out_shape=jax.ShapeDtypeStruct((S, LANES), F32),
                          compiler_params=_params(None))(fgp, bf)


def _fg_bwd(dcum, fgp, bf):
    S = fgp.shape[0]

    def body(d_ref, f_ref, b_ref, dfg_ref, dbf_ref):
        x = d_ref[...]
        row = lax.broadcasted_iota(jnp.int32, x.shape, 0)
        k = 1
        while k < S:
            x = x + _shift_up(x, k, row)
            k *= 2
        dfg = x * _sigmoid(-(f_ref[...] + b_ref[...]))
        dfg_ref[...] = dfg.astype(BF16)
        dbf_ref[...] = jnp.sum(dfg, axis=0, keepdims=True)

    return pl.pallas_call(
        body, name="fg_bwd",
        out_shape=(jax.ShapeDtypeStruct((S, LANES), BF16), jax.ShapeDtypeStruct((1, LANES), F32)),
        compiler_params=_params(None),
    )(dcum, fgp, bf)


def _colblock(S, off):
    return pl.BlockSpec((S, LANES), lambda h: (0, h + off))


def _head_col(S):
    return pl.BlockSpec((None, S, 1), lambda h: (h, 0, 0))


def _head_row(nq, tq):
    return pl.BlockSpec((None, nq, 1, tq), lambda h: (h, 0, 0, 0))


def _lane_of(ref, h):
    lane = lax.broadcasted_iota(jnp.int32, ref.shape, 1)
    return jnp.sum(jnp.where(lane == h, ref[...], 0.0), axis=1, keepdims=True)


def _attn_fwd(qkv, rest, cum, cumr, *, tq):
    S = qkv.shape[0]
    nq = S // tq

    def body(q_ref, k_ref, v_ref, g_ref, cum_ref, cumr_ref, o_ref, y_ref, lse_ref, cc):
        cc[...] = _lane_of(cum_ref, pl.program_id(0))
        tri = (lax.broadcasted_iota(jnp.int32, (tq, tq), 1) <= lax.broadcasted_iota(jnp.int32, (tq, tq), 0))

        def q_step(qi, _):
            rows = pl.ds(pl.multiple_of(qi * tq, tq), tq)
            q = q_ref[rows, :]
            ci = cc[rows, :]

            def tile(kj, carry, masked):
                m, l, acc = carry
                cols = pl.ds(pl.multiple_of(kj * tq, tq), tq)
                s = lax.dot_general(q, k_ref[cols, :], (((1,), (1,)), ((), ())), preferred_element_type=F32) * SCALE
                s = s + ci - cumr_ref[kj]
                if masked:
                    s = jnp.where(tri, s, NEG)
                m_new = jnp.maximum(m, jnp.max(s, axis=1, keepdims=True))
                a = jnp.exp(m - m_new)
                p = jnp.exp(s - m_new)
                l = a * l + jnp.sum(p, axis=1, keepdims=True)
                acc = a * acc + jnp.dot(p.astype(BF16), v_ref[cols, :], preferred_element_type=F32)
                return m_new, l, acc

            init = (jnp.full((tq, 1), NEG, F32), jnp.zeros((tq, 1), F32), jnp.zeros((tq, HEAD_DIM), F32))
            carry = lax.fori_loop(0, qi, lambda kj, c: tile(kj, c, False), init)
            m, l, acc = tile(qi, carry, True)
            o = acc / l
            g = g_ref[rows, :]
            o_ref[rows, :] = o
            y_ref[rows, :] = (o * (g * _sigmoid(g))).astype(BF16)
            lse_ref[rows, :] = m + jnp.log(l)
            return 0

        lax.fori_loop(0, nq, q_step, 0)

    return pl.pallas_call(
        body, name="attn_fwd", grid=(N_HEADS,),
        in_specs=[_colblock(S, QB), _colblock(S, KB), _colblock(S, VB), _colblock(S, GB),
                  pl.BlockSpec((S, LANES), lambda h: (0, 0)), _head_row(nq, tq)],
        out_specs=(_colblock(S, 0), _colblock(S, 0), _head_col(S)),
        out_shape=(jax.ShapeDtypeStruct((S, ATT_W), F32), jax.ShapeDtypeStruct((S, ATT_W), BF16),
                   jax.ShapeDtypeStruct((N_HEADS, S, 1), F32)),
        scratch_shapes=[pltpu.VMEM((S, 1), F32)],
        compiler_params=_params(("arbitrary",)),
    )(qkv, qkv, qkv, rest, cum, cumr)


def _attn_bwd_pre(rest, o, dy, cum, lse):
    S = o.shape[0]

    def body(g_ref, o_ref, dy_ref, cum_ref, lse_ref, do_ref, dg_ref, a_ref, dl_ref):
        g = g_ref[...]
        sg = _sigmoid(g)
        ov = o_ref[...]
        dy = dy_ref[...]
        do = dy * (g * sg)
        do_ref[...] = do.astype(BF16)
        dg_ref[...] = (dy * ov * (sg * (1.0 + g * (1.0 - sg)))).astype(BF16)
        dl_ref[...] = jnp.sum(do * ov, axis=1, keepdims=True)
        a_ref[...] = _lane_of(cum_ref, pl.program_id(0)) - lse_ref[...]

    att = jax.ShapeDtypeStruct((S, ATT_W), BF16)
    col = jax.ShapeDtypeStruct((N_HEADS, S, 1), F32)
    return pl.pallas_call(
        body, name="attn_bwd_pre", grid=(N_HEADS,),
        in_specs=[_colblock(S, GB), _colblock(S, 0), _colblock(S, 0),
                  pl.BlockSpec((S, LANES), lambda h: (0, 0)), _head_col(S)],
        out_specs=(_colblock(S, 0), _colblock(S, 0), _head_col(S), _head_col(S)),
        out_shape=(att, att, col, col),
        compiler_params=_params(("parallel",)),
    )(rest, o, dy, cum, lse)


def _attn_bwd(qkv, do, a, delta, cumr, *, tq):
    S = qkv.shape[0]
    nq = S // tq
    tdot = (((0,), (0,)), ((), ()))

    def body(q_ref, k_ref, v_ref, do_ref, a_ref, dl_ref, cumr_ref, dq_ref, dk_ref, dv_ref, dcr_ref, dcc_ref, dqa):
        dqa[...] = jnp.zeros_like(dqa)

        @pl.when(pl.program_id(0) == 0)
        def _():
            dcr_ref[...] = jnp.zeros_like(dcr_ref)

        tri = (lax.broadcasted_iota(jnp.int32, (tq, tq), 1) <= lax.broadcasted_iota(jnp.int32, (tq, tq), 0))
        mine = lax.broadcasted_iota(jnp.int32, (tq, LANES), 1) == pl.program_id(0)

        def k_step(kj, _):
            cols = pl.ds(pl.multiple_of(kj * tq, tq), tq)
            k = k_ref[cols, :]
            v = v_ref[cols, :]
            cj = cumr_ref[kj]

            def tile(qi, carry, masked):
                dk, dv, csum = carry
                rows = pl.ds(pl.multiple_of(qi * tq, tq), tq)
                q = q_ref[rows, :]
                dot = do_ref[rows, :]
                s = lax.dot_general(q, k, (((1,), (1,)), ((), ())), preferred_element_type=F32) * SCALE
                p = jnp.exp(s + a_ref[rows, :] - cj)
                if masked:
                    p = jnp.where(tri, p, 0.0)
                dv = dv + lax.dot_general(p.astype(BF16), dot, tdot, preferred_element_type=F32)
                dp = lax.dot_general(dot, v, (((1,), (1,)), ((), ())), preferred_element_type=F32)
                ds = p * (dp - dl_ref[rows, :])
                dcr_ref[rows, :] += jnp.where(mine, jnp.sum(ds, axis=1, keepdims=True), 0.0)
                csum = csum + jnp.sum(ds, axis=0, keepdims=True)
                dsb = (ds * SCALE).astype(BF16)
                dqa[rows, :] += jnp.dot(dsb, k, preferred_element_type=F32)
                dk = dk + lax.dot_general(dsb, q, tdot, preferred_element_type=F32)
                return dk, dv, csum

            init = (jnp.zeros((tq, HEAD_DIM), F32), jnp.zeros((tq, HEAD_DIM), F32), jnp.zeros((1, tq), F32))
            carry = tile(kj, init, True)
            dk, dv, csum = lax.fori_loop(kj + 1, nq, lambda qi, c: tile(qi, c, False), carry)
            dk_ref[cols, :] = dk.astype(BF16)
            dv_ref[cols, :] = dv.astype(BF16)
            dcc_ref[kj] = csum
            return 0

        lax.fori_loop(0, nq, k_step, 0)
        dq_ref[...] = dqa[...].astype(BF16)

    att = jax.ShapeDtypeStruct((S, ATT_W), BF16)
    return pl.pallas_call(
        body, name="attn_bwd", grid=(N_HEADS,),
        in_specs=[_colblock(S, QB), _colblock(S, KB), _colblock(S, VB), _colblock(S, 0),
                  _head_col(S), _head_col(S), _head_row(nq, tq)],
        out_specs=(_colblock(S, 0), _colblock(S, 0), _colblock(S, 0), pl.BlockSpec((S, LANES), lambda h: (0, 0)),
                   _head_row(nq, tq)),
        out_shape=(att, att, att, jax.ShapeDtypeStruct((S, LANES), F32),
                   jax.ShapeDtypeStruct((N_HEADS, nq, 1, tq), F32)),
        scratch_shapes=[pltpu.VMEM((S, HEAD_DIM), F32)],
        compiler_params=_params(("arbitrary",)),
    )(qkv, qkv, qkv, do, a, delta, cumr)


def _conv_taps(u, w_ref, row):
    return (w_ref[0:1, :] * _shift_down(u, 2, row) + w_ref[1:2, :] * _shift_down(u, 1, row)) + w_ref[2:3, :] * u


def _conv_fwd(rest, cw):
    S = rest.shape[0]

    def body(cb_ref, cc_ref, ch_ref, g_ref, w_ref, y_ref):
        row = lax.broadcasted_iota(jnp.int32, (S, LANES), 0)
        g = g_ref[...]
        y = _conv_taps(cc_ref[...] * ch_ref[...], w_ref, row)
        y_ref[...] = ((cb_ref[...] * y) * (g * _sigmoid(g))).astype(BF16)

    return pl.pallas_call(
        body, name="conv_fwd", grid=(CONV_W // LANES,),
        in_specs=[_colblock(S, CBB), _colblock(S, CCB), _colblock(S, CHB), _colblock(S, CGB),
                  pl.BlockSpec((CONV_TAPS, LANES), lambda j: (0, j))],
        out_specs=_colblock(S, 0),
        out_shape=jax.ShapeDtypeStruct((S, CONV_W), BF16),
        compiler_params=_params(("parallel",)),
    )(rest, rest, rest, rest, cw)


def _conv_bwd(rest, dy, cw):
    S = rest.shape[0]

    def body(cb_ref, cc_ref, ch_ref, g_ref, dy_ref, w_ref, dcb_ref, dcc_ref, dch_ref, dg_ref, dw_ref):
        row = lax.broadcasted_iota(jnp.int32, (S, LANES), 0)
        g = g_ref[...]
        sg = _sigmoid(g)
        silu = g * sg
        cb, cc, ch, dy = cb_ref[...], cc_ref[...], ch_ref[...], dy_ref[...]
        u = cc * ch
        u1 = _shift_down(u, 1, row)
        u2 = _shift_down(u, 2, row)
        y = (w_ref[0:1, :] * u2 + w_ref[1:2, :] * u1) + w_ref[2:3, :] * u
        dcb_ref[...] = (dy * silu * y).astype(BF16)
        dg_ref[...] = (dy * (cb * y) * (sg * (1.0 + g * (1.0 - sg)))).astype(BF16)
        dyv = dy * silu * cb
        du = w_ref[2:3, :] * dyv + w_ref[1:2, :] * _shift_up(dyv, 1, row) + w_ref[0:1, :] * _shift_up(dyv, 2, row)
        dcc_ref[...] = (du * ch).astype(BF16)
        dch_ref[...] = (du * cc).astype(BF16)
        dw_ref[0:1, :] = jnp.sum(dyv * u2, axis=0, keepdims=True)
        dw_ref[1:2, :] = jnp.sum(dyv * u1, axis=0, keepdims=True)
        dw_ref[2:3, :] = jnp.sum(dyv * u, axis=0, keepdims=True)

    act = jax.ShapeDtypeStruct((S, CONV_W), BF16)
    return pl.pallas_call(
        body, name="conv_bwd", grid=(CONV_W // LANES,),
        in_specs=[_colblock(S, CBB), _colblock(S, CCB), _colblock(S, CHB), _colblock(S, CGB),
                  _colblock(S, ATT_W // LANES), pl.BlockSpec((CONV_TAPS, LANES), lambda j: (0, j))],
        out_specs=(_colblock(S, 0),) * 4 + (pl.BlockSpec((CONV_TAPS, LANES), lambda j: (0, j)),),
        out_shape=(act, act, act, act, jax.ShapeDtypeStruct((CONV_TAPS, CONV_W), F32)),
        compiler_params=_params(("parallel",)),
    )(rest, rest, rest, rest, dy, cw)


def _pool_z(u, grp, row):
    s2 = u + _shift_down(u, 1, row)
    s4 = s2 + _shift_down(s2, 2, row)
    s8 = s4 + _shift_down(s4, 4, row)
    s16 = s8 + _shift_down(s8, 8, row)
    sw = jnp.where(grp == 0, s2, jnp.where(grp == 1, s4, jnp.where(grp == 2, s8, s16)))
    return sw / _pool_count(grp, row) - u


def _pool_count(grp, row):
    return jnp.minimum(row + 1, 2 << grp).astype(F32)


def _pool_fwd(rest, pw, scale):
    S = rest.shape[0]

    def body(u_ref, g_ref, pw_ref, sc_ref, y_ref):
        grp = pl.program_id(0)
        row = lax.broadcasted_iota(jnp.int32, (S, LANES), 0)
        z = _pool_z(u_ref[...], grp, row)
        zp = jnp.dot(z.astype(BF16), pw_ref[...].astype(BF16), preferred_element_type=F32)
        g = g_ref[...]
        y_ref[...] = ((zp * sc_ref[...]) * (g * _sigmoid(g))).astype(BF16)

    return pl.pallas_call(
        body, name="pool_fwd", grid=(N_POOL,),
        in_specs=[_colblock(S, PUB), _colblock(S, PGB),
                  pl.BlockSpec((None, POOL_GROUP, POOL_GROUP), lambda j: (j, 0, 0)),
                  pl.BlockSpec((1, LANES), lambda j: (0, j))],
        out_specs=_colblock(S, 0),
        out_shape=jax.ShapeDtypeStruct((S, POOL_W), BF16),
        compiler_params=_params(("parallel",)),
    )(rest, rest, pw, scale)


def _pool_bwd(rest, dy, pw, scale):
    S = rest.shape[0]

    def body(u_ref, g_ref, dy_ref, pw_ref, sc_ref, du_ref, dg_ref, dpw_ref, dsc_ref):
        grp = pl.program_id(0)
        row = lax.broadcasted_iota(jnp.int32, (S, LANES), 0)
        z = _pool_z(u_ref[...], grp, row).astype(BF16)
        pwb = pw_ref[...].astype(BF16)
        zp = jnp.dot(z, pwb, preferred_element_type=F32)
        g = g_ref[...]
        sg = _sigmoid(g)
        silu = g * sg
        dy = dy_ref[...]
        sc = sc_ref[...]
        dsc_ref[...] = jnp.sum(dy * silu * zp, axis=0, keepdims=True)
        dg_ref[...] = (dy * (zp * sc) * (sg * (1.0 + g * (1.0 - sg)))).astype(BF16)
        dzp = (dy * silu * sc).astype(BF16)
        dpw_ref[...] = lax.dot_general(z, dzp, (((0,), (0,)), ((), ())), preferred_element_type=F32)
        dz = lax.dot_general(dzp, pwb, (((1,), (1,)), ((), ())), preferred_element_type=F32)
        f1 = dz / _pool_count(grp, row)
        f2 = f1 + _shift_up(f1, 1, row)
        f4 = f2 + _shift_up(f2, 2, row)
        f8 = f4 + _shift_up(f4, 4, row)
        f16 = f8 + _shift_up(f8, 8, row)
        fw = jnp.where(grp == 0, f2, jnp.where(grp == 1, f4, jnp.where(grp == 2, f8, f16)))
        du_ref[...] = (fw - dz).astype(BF16)

    act = jax.ShapeDtypeStruct((S, POOL_W), BF16)
    return pl.pallas_call(
        body, name="pool_bwd", grid=(N_POOL,),
        in_specs=[_colblock(S, PUB), _colblock(S, PGB), _colblock(S, (ATT_W + CONV_W) // LANES),
                  pl.BlockSpec((None, POOL_GROUP, POOL_GROUP), lambda j: (j, 0, 0)),
                  pl.BlockSpec((1, LANES), lambda j: (0, j))],
        out_specs=(_colblock(S, 0), _colblock(S, 0),
                   pl.BlockSpec((None, POOL_GROUP, POOL_GROUP), lambda j: (j, 0, 0)),
                   pl.BlockSpec((1, LANES), lambda j: (0, j))),
        out_shape=(act, act, jax.ShapeDtypeStruct((N_POOL, POOL_GROUP, POOL_GROUP), F32),
                   jax.ShapeDtypeStruct((1, POOL_W), F32)),
        compiler_params=_params(("parallel",)),
    )(rest, rest, dy, pw, scale)


def _ln_fwd(z, g, b, *, ts):
    S, D = z.shape

    def body(z_ref, g_ref, b_ref, o_ref, ob_ref, obt_ref, xh_ref, rs_ref):
        zz = z_ref[...]
        mu = jnp.mean(zz, axis=1, keepdims=True)
        zc = zz - mu
        rstd = lax.rsqrt(jnp.mean(zc * zc, axis=1, keepdims=True) + LN_EPS)
        xh = zc * rstd
        out = xh * g_ref[...] + b_ref[...]
        o_ref[...] = out
        ob_ref[...] = out.astype(BF16)
        obt_ref[...] = out.T.astype(BF16)
        xh_ref[...] = xh
        rs_ref[...] = rstd

    rowblk = pl.BlockSpec((ts, D), lambda i: (i, 0))
    vec = pl.BlockSpec((1, D), lambda i: (0, 0))
    return pl.pallas_call(
        body, name="ln_fwd", grid=(S // ts,),
        in_specs=[rowblk, vec, vec],
        out_specs=(rowblk, rowblk, pl.BlockSpec((D, ts), lambda i: (0, i)), rowblk,
                   pl.BlockSpec((ts, 1), lambda i: (i, 0))),
        out_shape=(jax.ShapeDtypeStruct((S, D), F32), jax.ShapeDtypeStruct((S, D), BF16),
                   jax.ShapeDtypeStruct((D, S), BF16), jax.ShapeDtypeStruct((S, D), F32),
                   jax.ShapeDtypeStruct((S, 1), F32)),
        compiler_params=_params(("parallel",)),
    )(z, g, b)


def _prep(x, after, *, ts):
    S, D = x.shape

    def body(x_ref, after_ref, xb_ref, xbt_ref):
        xb_ref[...] = x_ref[...].astype(BF16)
        xbt_ref[...] = x_ref[...].T.astype(BF16)

    return pl.pallas_call(
        body, name="prep", grid=(S // ts,),
        in_specs=[pl.BlockSpec((ts, D), lambda i: (i, 0)), ANY],
        out_specs=(pl.BlockSpec((ts, D), lambda i: (i, 0)), pl.BlockSpec((D, ts), lambda i: (0, i))),
        out_shape=(jax.ShapeDtypeStruct((S, D), BF16), jax.ShapeDtypeStruct((D, S), BF16)),
        compiler_params=_params(("parallel",)),
    )(x, after)


def _ln_bwd(dout, xhat, rstd, g, *, ts):
    S, D = dout.shape

    def body(d_ref, xh_ref, rs_ref, g_ref, dz_ref, dzb_ref, dg_ref, db_ref):
        @pl.when(pl.program_id(0) == 0)
        def _():
            dg_ref[...] = jnp.zeros_like(dg_ref)
            db_ref[...] = jnp.zeros_like(db_ref)

        d = d_ref[...]
        xh = xh_ref[...]
        dxh = d * g_ref[...]
        m1 = jnp.mean(dxh, axis=1, keepdims=True)
        m2 = jnp.mean(dxh * xh, axis=1, keepdims=True)
        dz = rs_ref[...] * (dxh - m1 - xh * m2)
        dz_ref[...] = dz
        dzb_ref[...] = dz.astype(BF16)
        dg_ref[...] += jnp.sum(d * xh, axis=0, keepdims=True)
        db_ref[...] += jnp.sum(d, axis=0, keepdims=True)

    rowblk = pl.BlockSpec((ts, D), lambda i: (i, 0))
    vec = pl.BlockSpec((1, D), lambda i: (0, 0))
    return pl.pallas_call(
        body, name="ln_bwd", grid=(S // ts,),
        in_specs=[rowblk, rowblk, pl.BlockSpec((ts, 1), lambda i: (i, 0)), vec],
        out_specs=(rowblk, rowblk, vec, vec),
        out_shape=(jax.ShapeDtypeStruct((S, D), F32), jax.ShapeDtypeStruct((S, D), BF16),
                   jax.ShapeDtypeStruct((1, D), F32), jax.ShapeDtypeStruct((1, D), F32)),
        compiler_params=_params(("arbitrary",)),
    )(dout, xhat, rstd, g)


def _loss_head(y, tgt, *, ts):
    S, D = y.shape

    def body(y_ref, t_ref, l_ref, dy_ref):
        @pl.when(pl.program_id(0) == 0)
        def _():
            l_ref[...] = jnp.zeros_like(l_ref)

        e = y_ref[...] - t_ref[...]
        dy_ref[...] = e * (1.0 / D)
        rowloss = jnp.sum(e * e, axis=1, keepdims=True) * (0.5 / D)
        l_ref[...] += jnp.sum(rowloss, axis=0, keepdims=True)

    rowblk = pl.BlockSpec((ts, D), lambda i: (i, 0))
    return pl.pallas_call(
        body, name="loss_head", grid=(S // ts,),
        in_specs=[rowblk, rowblk],
        out_specs=(pl.BlockSpec((1, 1), lambda i: (0, 0)), rowblk),
        out_shape=(jax.ShapeDtypeStruct((1, 1), F32), jax.ShapeDtypeStruct((S, D), F32)),
        compiler_params=_params(("arbitrary",)),
    )(y, tgt)


def _adamw(lands, srcs, w, m, v, *, name):
    R, C = w.shape
    npart = len(lands)
    rp = R // npart
    assert R % npart == 0 and all(p.shape == (N_DEV, rp, C) for p in lands)
    tiled = rp % 8 == 0
    tr = min(ADAMW_BLOCK_ELEMS // (LANES * pl.cdiv(C, LANES)), rp) if tiled else rp
    assert rp % tr == 0
    steps = rp // tr
    c1 = 1.0 - ADAM_B1 ** ADAM_STEP
    c2 = 1.0 - ADAM_B2 ** ADAM_STEP
    slabbed = [p.ndim == 3 for p in srcs]

    def body(*refs):
        l_refs, s_refs = refs[:npart], refs[npart:2 * npart]
        w_ref, m_ref, v_ref, g_ref, d_ref, nm_ref, nv_ref = refs[2 * npart:]
        x, y, c = _my_place()
        me = 4 * x + 2 * y + c

        def update(l, rows):
            g = None
            for j in range(N_DEV):
                own = s_refs[l][j] if slabbed[l] else s_refs[l][...]
                p = jnp.where(me == j, own, l_refs[l][j]).astype(F32)
                g = p if g is None else g + p
            nm = ADAM_B1 * m_ref[rows, :] + (1.0 - ADAM_B1) * g
            nv = ADAM_B2 * v_ref[rows, :] + (1.0 - ADAM_B2) * (g * g)
            g_ref[rows, :] = g
            nm_ref[rows, :] = nm
            nv_ref[rows, :] = nv
            d_ref[rows, :] = -ADAM_LR * ((nm / c1) / (jnp.sqrt(nv / c2) + ADAM_EPS) + ADAM_WD * w_ref[rows, :])

        for l in range(npart):
            if tiled:
                pl.when(pl.program_id(0) == l)(lambda l=l: update(l, slice(None)))
            else:
                update(l, slice(l * rp, (l + 1) * rp))

    def part_spec(l, p):
        if p.ndim == 3:
            return pl.BlockSpec((N_DEV, tr, C), lambda q, i: (0, jnp.where(q == l, i, 0), 0))
        return pl.BlockSpec((tr, C), lambda q, i: (jnp.where(q == l, i, 0), 0))

    blk = pl.BlockSpec((tr, C), lambda q, i: (q * steps + i, 0)) if tiled else pl.BlockSpec((R, C), lambda q, i: (0, 0))
    out = jax.ShapeDtypeStruct((R, C), F32)
    return pl.pallas_call(
        body, name=name, grid=(npart, steps) if tiled else (1, 1),
        in_specs=[part_spec(l, p) for l, p in enumerate(lands)] + [part_spec(l, p) for l, p in enumerate(srcs)]
        + [blk, blk, blk],
        out_specs=(blk, blk, blk, blk),
        out_shape=(out, out, out, out),
        compiler_params=_params(("arbitrary", "arbitrary")),
    )(*lands, *srcs, w, m, v)


def _my_place():
    return lax.axis_index("x"), lax.axis_index("y"), lax.axis_index("c")


HBM = pl.BlockSpec(memory_space=pltpu.HBM)
SEM = pl.BlockSpec(memory_space=pltpu.SEMAPHORE)
DATAFLOW = pltpu.SideEffectType.DATAFLOW_SIDE_EFFECTING
N_PEER = N_DEV - 1


def _xchg_copies(src, land, send_sem, recv_sem, gather):
    x, y, c = _my_place()
    me = 4 * x + 2 * y + c
    copies = []
    for k in range(N_PEER):
        px, py, pc = x ^ (((k + 1) >> 2) & 1), y ^ (((k + 1) >> 1) & 1), c ^ ((k + 1) & 1)
        peer = 4 * px + 2 * py + pc
        for a in range(len(src)):
            i = a * N_PEER + k
            out = src[a] if gather[a] else src[a].at[peer]
            sent, landed = [pltpu.make_async_remote_copy(
                src_ref=out, dst_ref=land[a].at[slab], send_sem=send_sem.at[i], recv_sem=recv_sem.at[i],
                device_id=(px, py, pc), device_id_type=MESH) for slab in (me, peer)]
            copies.append((sent, landed))
    return copies


def _xchg_start(srcs, *, gather, name):
    n = len(srcs)
    lands = [lax.empty(((N_DEV,) + v.shape) if g else v.shape, v.dtype) for v, g in zip(srcs, gather)]

    def body(*refs):
        src, land = refs[:n], refs[n:2 * n]
        send_sem, recv_sem = refs[2 * n], refs[2 * n + 1]
        token = refs[-1]
        for sent, _ in _xchg_copies(src, land, send_sem, recv_sem, gather):
            sent.start()
        token[...] = jnp.zeros_like(token)

    sems = pltpu.SemaphoreType.DMA((n * N_PEER,))
    outs = pl.pallas_call(
        body, name=name,
        out_shape=(sems, sems, *[pltpu.HBM(v.shape, v.dtype) for v in srcs + lands],
                   jax.ShapeDtypeStruct((8, LANES), F32)),
        in_specs=[HBM] * (2 * n),
        out_specs=(SEM, SEM, *[HBM] * (2 * n), pl.BlockSpec(memory_space=pltpu.VMEM)),
        input_output_aliases={i: 2 + i for i in range(2 * n)},
        compiler_params=pltpu.CompilerParams(has_side_effects=DATAFLOW),
    )(*[pltpu.with_memory_space_constraint(v, pltpu.HBM) for v in srcs + lands])
    return dict(send=outs[0], recv=outs[1], srcs=list(outs[2:2 + n]), lands=list(outs[2 + n:2 + 2 * n]),
                token=outs[-1], gather=gather)


def _xchg_wait(h, after, *, name):
    n = len(h["srcs"])
    gather = h["gather"]

    def body(*refs):
        src, land = refs[:n], refs[n:2 * n]
        send_sem, recv_sem = refs[2 * n], refs[2 * n + 1]
        for sent, landed in _xchg_copies(src, land, send_sem, recv_sem, gather):
            sent.wait_send()
            landed.wait_recv()

    thru = h["srcs"] + h["lands"]
    outs = pl.pallas_call(
        body, name=name,
        out_shape=[pltpu.HBM(v.shape, v.dtype) for v in thru],
        in_specs=[HBM] * (2 * n) + [SEM, SEM, ANY],
        out_specs=[HBM] * (2 * n),
        input_output_aliases={i: i for i in range(2 * n)},
        compiler_params=pltpu.CompilerParams(has_side_effects=DATAFLOW),
    )(*thru, h["send"], h["recv"], after)
    return list(outs[:n]), list(outs[n:])


def _with_own(land, own, me):
    slot = lax.broadcasted_iota(jnp.int32, land.shape, 0)
    return jnp.where(slot == me, own if own.ndim == land.ndim else own[None], land)


def _w_in_segments(j, cols):
    lo, hi = j * cols, (j + 1) * cols
    segs = []
    for a, b, where, shift in ((0, FG_AT, "main", 0), (FG_AT, FG_AT + N_HEADS, "fg", -FG_AT),
                               (FG_AT + N_HEADS, IN_W, "main", -N_HEADS)):
        s0, s1 = max(lo, a), min(hi, b)
        if s0 < s1:
            segs.append((s0 - lo, s1 - s0, where, s0 + shift))
    return segs


SLAB_W = 1024


def _window(off, width):
    base = min((off // LANES) * LANES, width - SLAB_W)
    return base, off - base


def _w_unpack(land, own, *, tr):
    _, D, cols = land.shape
    assert cols + LANES - 1 <= SLAB_W

    def body(land_ref, own_ref, main_ref, fg_ref, slab):
        x, y, c = _my_place()
        me = 4 * x + 2 * y + c
        lane = lax.broadcasted_iota(jnp.int32, (tr, SLAB_W), 1)
        main_ref[...] = jnp.zeros_like(main_ref)
        slab[...] = jnp.zeros_like(slab)
        for j in range(N_DEV):
            slab[:, :cols] = jnp.where(me == j, own_ref[...], land_ref[j])
            v = slab[...]
            segs = _w_in_segments(j, cols)
            for lo, n, where, dst in segs:
                part = v if len(segs) == 1 else jnp.where((lane >= lo) & (lane < lo + n), v, jnp.zeros_like(v))
                if where == "fg":
                    fg_ref[...] = pltpu.roll(part, (SLAB_W - lo) % SLAB_W, 1)[:, :LANES]
                else:
                    base, r = _window(dst - lo, MAIN_W)
                    assert 0 <= r and lo + n + r <= SLAB_W
                    main_ref[:, base:base + SLAB_W] += pltpu.roll(part, r, 1)

    return pl.pallas_call(
        body, name="w_unpack", grid=(D // tr,),
        in_specs=[pl.BlockSpec((N_DEV, tr, cols), lambda i: (0, i, 0)), pl.BlockSpec((tr, cols), lambda i: (i, 0))],
        out_specs=(pl.BlockSpec((tr, MAIN_W), lambda i: (i, 0)), pl.BlockSpec((tr, LANES), lambda i: (i, 0))),
        out_shape=(jax.ShapeDtypeStruct((D, MAIN_W), BF16), jax.ShapeDtypeStruct((D, LANES), BF16)),
        scratch_shapes=[pltpu.VMEM((tr, SLAB_W), BF16)],
        compiler_params=_params(("parallel",)),
    )(land, own)


def _g_pack(dw_main, dw_fg, cols, *, tr):
    D = dw_main.shape[0]

    def body(main_ref, fg_ref, out_ref, slab):
        lane = lax.broadcasted_iota(jnp.int32, (tr, SLAB_W), 1)
        slab[...] = jnp.zeros_like(slab)
        slab[:, :LANES] = fg_ref[...]
        fg = slab[...]
        for j in range(N_DEV):
            v = None
            for lo, n, where, src in _w_in_segments(j, cols):
                if where == "fg":
                    part = pltpu.roll(fg, lo, 1)
                else:
                    base, r = _window(src - lo, MAIN_W)
                    assert 0 <= r and lo + n + r <= SLAB_W
                    part = pltpu.roll(main_ref[:, base:base + SLAB_W], (SLAB_W - r) % SLAB_W, 1)
                v = part if v is None else jnp.where((lane >= lo) & (lane < lo + n), part, v)
            out_ref[j] = v[:, :cols]

    return pl.pallas_call(
        body, name="g_pack", grid=(D // tr,),
        in_specs=[pl.BlockSpec((tr, MAIN_W), lambda i: (i, 0)), pl.BlockSpec((tr, LANES), lambda i: (i, 0))],
        out_specs=pl.BlockSpec((N_DEV, tr, cols), lambda i: (0, i, 0)),
        out_shape=jax.ShapeDtypeStruct((N_DEV, D, cols), BF16),
        scratch_shapes=[pltpu.VMEM((tr, SLAB_W), BF16)],
        compiler_params=_params(("parallel",)),
    )(dw_main, dw_fg)


TQ = 512
TS = 256
W_TILE_ROWS = 128


def _layer_fwd(x, xb, xbt, wl):
    qkv = _mm(xb, wl["w_main"], out_dtype=BF16, tm=1024, tn=512, tk=D_MODEL, name="mm_qkv", n_cols=QKV_W)
    rest = _mm(xb, wl["w_main"], out_dtype=F32, tm=1024, tn=512, tk=D_MODEL, name="mm_rest", n_off=QKV_W, n_cols=REST_W)
    fgp = _mm(xb, wl["w_fg"], out_dtype=F32, tm=1024, tn=LANES, tk=D_MODEL, name="mm_fg")
    cum = _fg_fwd(fgp, wl["b_f"])
    S = x.shape[0]
    cumr = cum[:, :N_HEADS].T.reshape(N_HEADS, S // TQ, 1, TQ)
    o, y_att, lse = _attn_fwd(qkv, rest, cum, cumr, tq=TQ)
    y_conv = _conv_fwd(rest, wl["conv_w"])
    y_pool = _pool_fwd(rest, wl["pool_w"], wl["pool_scale"])
    y = jnp.concatenate([y_att, y_conv, y_pool], axis=1)
    z = _mm(y, wl["w_out"], out_dtype=F32, tm=1024, tn=512, tk=D_MODEL, name="mm_out", c=x, c_scale=ALPHA)
    out, outb, outbt, xhat, rstd = _ln_fwd(z, wl["ln_g"], wl["ln_b"], ts=TS)
    saved = dict(xbt=xbt, qkv=qkv, rest=rest, fgp=fgp, cum=cum, cumr=cumr, o=o, lse=lse, y=y, xhat=xhat, rstd=rstd)
    return out, outb, outbt, saved


def _layer_bwd(dout, sv, wl):
    S = dout.shape[0]
    dz, dzb, dln_g, dln_b = _ln_bwd(dout, sv["xhat"], sv["rstd"], wl["ln_g"], ts=TS)
    dy = _mm(dzb, wl["w_out"], out_dtype=F32, tm=1024, tn=512, tk=D_MODEL, name="mm_dy", bt=True)
    dw_out = _mm(sv["y"], dzb, out_dtype=BF16, tm=512, tn=D_MODEL, tk=512, name="mm_dw_out", at=True)
    rest = sv["rest"]
    do, dg_att, a, delta = _attn_bwd_pre(rest, sv["o"], dy, sv["cum"], sv["lse"])
    dq, dk, dv, dcr, dcc = _attn_bwd(sv["qkv"], do, a, delta, sv["cumr"], tq=TQ)
    dcum = dcr - jnp.pad(dcc.reshape(N_HEADS, S).T, ((0, 0), (0, LANES - N_HEADS)))
    dfg, db_f = _fg_bwd(dcum, sv["fgp"], wl["b_f"])
    dcb, dcc_, dch, dgc, dconv_w = _conv_bwd(rest, dy, wl["conv_w"])
    dpu, dgp, dpool_w, dpool_scale = _pool_bwd(rest, dy, wl["pool_w"], wl["pool_scale"])
    dproj = jnp.concatenate([dq, dk, dv, dg_att, dcb, dcc_, dch, dgc, dpu, dgp], axis=1)
    xt = sv["xbt"]
    dw_main = _mm(xt, dproj, out_dtype=BF16, tm=1024, tn=512, tk=2048, name="mm_dw_main")
    dw_fg = _mm(xt, dfg, out_dtype=BF16, tm=1024, tn=LANES, tk=2048, name="mm_dw_fg")
    t = _mm(dfg, wl["w_fg"], out_dtype=F32, tm=1024, tn=512, tk=LANES, name="mm_dx_fg", c=dz, c_scale=ALPHA, bt=True)
    dx = _mm(dproj, wl["w_main"], out_dtype=F32, tm=1024, tn=512, tk=1792, name="mm_dx", c=t, c_scale=1.0, bt=True)
    grads = dict(w_main=dw_main, w_fg=dw_fg, b_f=db_f[:, :N_HEADS], conv_w=dconv_w, pool_w=dpool_w, pool_scale=dpool_scale,
                 w_out=dw_out, ln_g=dln_g, ln_b=dln_b)
    return dx, grads


def kernel(x, w_in, b_f, conv_w, pool_w, pool_scale, w_out, ln_g, ln_b, loss_target, m_w_in, m_b_f, m_conv_w, m_pool_w, m_pool_scale, m_w_out, m_ln_g, m_ln_b, v_w_in, v_b_f, v_conv_w, v_pool_w, v_pool_scale, v_w_out, v_ln_g, v_ln_b):
    L, D, cols = w_in.shape
    rows_out = w_out.shape[1]
    ccols = conv_w.shape[2]
    mx, my, mc = _my_place()
    me = 4 * mx + 2 * my + mc

    w_in_b, w_out_b = w_in.astype(BF16), w_out.astype(BF16)
    gathers = [_xchg_start([w_in_b[l], w_out_b[l], conv_w[l]], gather=[True] * 3, name=f"gather_start_{l}")
               for l in range(L)]
    started = gathers[0]["token"]
    for h in gathers[1:]:
        started = started + h["token"]

    xl = x[0]
    xb, xbt = _prep(xl, started, ts=TS)
    weights, saved = [], []
    for l in range(L):
        (own_in, own_out, own_cw), (g_in, g_out, g_cw) = _xchg_wait(gathers[l], xb, name=f"gather_wait_{l}")
        w_main, w_fg = _w_unpack(g_in, own_in, tr=W_TILE_ROWS)
        w_o = _with_own(g_out, own_out, me).reshape(N_DEV * rows_out, D)
        cw = jnp.transpose(_with_own(g_cw, own_cw, me), (1, 0, 2)).reshape(CONV_TAPS, N_DEV * ccols)
        wl = dict(w_main=w_main, w_fg=w_fg, w_out=w_o, conv_w=cw,
                  b_f=jnp.pad(b_f[l][None, :], ((0, 0), (0, LANES - N_HEADS))), pool_w=pool_w[l],
                  pool_scale=pool_scale[l][None, :], ln_g=ln_g[l][None, :], ln_b=ln_b[l][None, :])
        xl, xb, xbt, sv = _layer_fwd(xl, xb, xbt, wl)
        weights.append(wl)
        saved.append(sv)

    loss, dx = _loss_head(xl, loss_target[0], ts=TS)
    loss = lax.psum(loss[0, 0], ("x", "y", "c"))

    sharded, small = ["w_in", "w_out", "conv_w"], ["b_f", "pool_w", "pool_scale", "ln_g", "ln_b"]
    scatters = [None] * L
    started = None
    for l in reversed(range(L)):
        wl = weights[l]
        if started is not None:
            wl = dict(wl, ln_g=wl["ln_g"] + started[0, 0])
        dx, g = _layer_bwd(dx, saved[l], wl)
        slabs = [_g_pack(g["w_main"], g["w_fg"], cols, tr=W_TILE_ROWS), g["w_out"].reshape(N_DEV, rows_out, D),
                 jnp.transpose(g["conv_w"].reshape(CONV_TAPS, N_DEV, ccols), (1, 0, 2))]
        scatters[l] = _xchg_start(slabs + [g[k] for k in small], gather=[False] * 3 + [True] * 5,
                                  name=f"scatter_start_{l}")
        started = scatters[l]["token"]

    sent, landed = [None] * L, [None] * L
    for l in reversed(range(L)):
        sent[l], landed[l] = _xchg_wait(scatters[l], dx, name=f"scatter_wait_{l}")

    given = dict(w_in=(w_in, m_w_in, v_w_in), b_f=(b_f, m_b_f, v_b_f), conv_w=(conv_w, m_conv_w, v_conv_w),
                 pool_w=(pool_w, m_pool_w, v_pool_w), pool_scale=(pool_scale, m_pool_scale, v_pool_scale),
                 w_out=(w_out, m_w_out, v_w_out), ln_g=(ln_g, m_ln_g, v_ln_g), ln_b=(ln_b, m_ln_b, v_ln_b))
    res = {}
    for a, k in enumerate(sharded + small):
        w, m, v = given[k]
        shape = w.shape
        w2, m2, v2 = (t.reshape(-1, shape[-1]) for t in (w, m, v))
        rp = w2.shape[0] // L
        lands = [landed[l][a].reshape(N_DEV, rp, shape[-1]) for l in range(L)]
        srcs = [sent[l][a].reshape(lands[l].shape if a < len(sharded) else lands[l].shape[1:]) for l in range(L)]
        outs = _adamw(lands, srcs, w2, m2, v2, name="adamw_" + k)
        res[k] = [t.reshape(shape) for t in outs]

    order = ["w_in", "b_f", "conv_w", "pool_w", "pool_scale", "w_out", "ln_g", "ln_b"]
    return (loss, dx[None], *[res[k][0] for k in order], *[res[k][1] for k in order],
            *[res[k][2] for k in order], *[res[k][3] for k in order])
```

```python
import jax
import jax.numpy as jnp
from jax import lax
from jax.experimental import pallas as pl
from jax.experimental.pallas import tpu as pltpu

F32 = jnp.float32
BF16 = jnp.bfloat16

N_DEV = 8
D_MODEL = 2048
N_HEADS = 8
HEAD_DIM = 128
ATT_W = N_HEADS * HEAD_DIM
CONV_W = 512
CONV_TAPS = 3
POOL_W = 512
POOL_GROUP = 128
N_POOL = POOL_W // POOL_GROUP
FG_AT = 4 * ATT_W
IN_W = FG_AT + N_HEADS + 4 * CONV_W + 2 * POOL_W
MAIN_W = IN_W - N_HEADS
LANES = 128
LN_EPS = 1e-5
DEPTH = 4
ALPHA = (2 * DEPTH) ** 0.25
SCALE = HEAD_DIM ** -0.5
LOG2E = 1.4426950408889634
LN2 = 0.6931471805599453
NEG = -1e30

ADAM_LR, ADAM_B1, ADAM_B2, ADAM_EPS, ADAM_WD, ADAM_STEP = 0.001, 0.9, 0.999, 1e-08, 0.01, 10

QKV_W = 3 * ATT_W
REST_W = MAIN_W - QKV_W
QB, KB, VB = 0, 8, 16
GB = 0
CBB, CCB, CHB, CGB = 8, 12, 16, 20
PUB, PGB = 24, 28
VMEM_LIMIT = 48 * 1024 * 1024
ADAMW_BLOCK_ELEMS = 64 * 1024

MESH = pl.DeviceIdType.MESH
ANY = pl.BlockSpec(memory_space=pl.ANY)


def _params(semantics):
    return pltpu.CompilerParams(dimension_semantics=semantics, vmem_limit_bytes=VMEM_LIMIT)


def _sigmoid(x):
    return 1.0 / (1.0 + jnp.exp(-x))


def _mm(a, b, *, out_dtype, tm, tn, tk, name, c=None, c_scale=1.0, n_off=0, n_cols=None, bt=False, at=False,
        after=None):
    M, K = a.shape[::-1] if at else a.shape
    N = b.shape[0 if bt else 1] if n_cols is None else n_cols
    assert not (bt and n_off)
    dims = (((0 if at else 1,), (1 if bt else 0,)), ((), ()))
    tm, tn, tk = min(tm, M), min(tn, N), min(tk, K)
    assert M % tm == 0 and N % tn == 0 and K % tk == 0 and n_off % tn == 0, (a.shape, b.shape, tm, tn, tk)
    nk, joff = K // tk, n_off // tn
    has_c, has_after = c is not None, after is not None

    def body(*refs):
        a_ref, b_ref = refs[0], refs[1]
        c_ref = refs[2] if has_c else None
        o_ref = refs[2 + has_c + has_after]

        def finish(r):
            if has_c:
                r = r + c_scale * c_ref[...]
            o_ref[...] = r.astype(out_dtype)

        if nk == 1:
            finish(lax.dot_general(a_ref[...], b_ref[...], dims, preferred_element_type=F32))
        else:
            acc_ref = refs[3 + has_c + has_after]
            k = pl.program_id(2)

            @pl.when(k == 0)
            def _():
                acc_ref[...] = jnp.zeros_like(acc_ref)

            acc_ref[...] += lax.dot_general(a_ref[...], b_ref[...], dims, preferred_element_type=F32)

            @pl.when(k == nk - 1)
            def _():
                finish(acc_ref[...])

    in_specs = [pl.BlockSpec((tk, tm), lambda i, j, k: (k, i)) if at else pl.BlockSpec((tm, tk), lambda i, j, k: (i, k)),
                pl.BlockSpec((tn, tk), lambda i, j, k: (j, k)) if bt
                else pl.BlockSpec((tk, tn), lambda i, j, k: (k, j + joff))]
    args = [a, b]
    if has_c:
        in_specs.append(pl.BlockSpec((tm, tn), lambda i, j, k: (i, j)))
        args.append(c)
    if has_after:
        in_specs.append(ANY)
        args.append(after)
    return pl.pallas_call(
        body, name=name, grid=(M // tm, N // tn, nk),
        in_specs=in_specs,
        out_specs=pl.BlockSpec((tm, tn), lambda i, j, k: (i, j)),
        out_shape=jax.ShapeDtypeStruct((M, N), out_dtype),
        scratch_shapes=[pltpu.VMEM((tm, tn), F32)] if nk > 1 else [],
        compiler_params=_params(("parallel", "parallel", "arbitrary")),
    )(*args)


def _shift_down(x, k, row):
    return jnp.where(row >= k, pltpu.roll(x, k, 0), 0.0)


def _shift_up(x, k, row):
    s = x.shape[0]
    return jnp.where(row < s - k, pltpu.roll(x, s - k, 0), 0.0)


def _fg_fwd(fgp, bf):
    S = fgp.shape[0]

    def body(f_ref, b_ref, cum_ref):
        a = f_ref[...] + b_ref[...]
        x = jnp.minimum(a, 0.0) - jnp.log(1.0 + jnp.exp(-jnp.abs(a)))
        row = lax.broadcasted_iota(jnp.int32, x.shape, 0)
        k = 1
        while k < S:
            x = x + _shift_down(x, k, row)
            k *= 2
        cum_ref[...] = x

    return pl.pallas_call(body, name="fg_fwd", out_shape=jax.ShapeDtypeStruct((S, LANES), F32),
                          compiler_params=_params(None))(fgp, bf)


def _fg_bwd(dcum, fgp, bf):
    S = fgp.shape[0]

    def body(d_ref, f_ref, b_ref, dfg_ref, dbf_ref):
        x = d_ref[...]
        row = lax.broadcasted_iota(jnp.int32, x.shape, 0)
        k = 1
        while k < S:
            x = x + _shift_up(x, k, row)
            k *= 2
        dfg = x * _sigmoid(-(f_ref[...] + b_ref[...]))
        dfg_ref[...] = dfg.astype(BF16)
        dbf_ref[...] = jnp.sum(dfg, axis=0, keepdims=True)

    return pl.pallas_call(
        body, name="fg_bwd",
        out_shape=(jax.ShapeDtypeStruct((S, LANES), BF16), jax.ShapeDtypeStruct((1, LANES), F32)),
        compiler_params=_params(None),
    )(dcum, fgp, bf)


def _colblock(S, off):
    return pl.BlockSpec((S, LANES), lambda h: (0, h + off))


def _head_col(S):
    return pl.BlockSpec((None, S, 1), lambda h: (h, 0, 0))


def _head_row(nq, tq):
    return pl.BlockSpec((None, nq, 1, tq), lambda h: (h, 0, 0, 0))


def _lane_of(ref, h):
    lane = lax.broadcasted_iota(jnp.int32, ref.shape, 1)
    return jnp.sum(jnp.where(lane == h, ref[...], 0.0), axis=1, keepdims=True)


def _attn_fwd(qkv, rest, cum, cumr, *, tq):
    S = qkv.shape[0]
    nq = S // tq

    def body(q_ref, k_ref, v_ref, g_ref, cum_ref, cumr_ref, o_ref, y_ref, lse_ref, cc):
        cc[...] = _lane_of(cum_ref, pl.program_id(0)) * LOG2E
        tri = (lax.broadcasted_iota(jnp.int32, (tq, tq), 1) <= lax.broadcasted_iota(jnp.int32, (tq, tq), 0))

        def q_step(qi, _):
            rows = pl.ds(pl.multiple_of(qi * tq, tq), tq)
            q = q_ref[rows, :]
            ci = cc[rows, :]

            def tile(kj, carry, masked):
                m, l, acc = carry
                cols = pl.ds(pl.multiple_of(kj * tq, tq), tq)
                s = lax.dot_general(q, k_ref[cols, :], (((1,), (1,)), ((), ())), preferred_element_type=F32)
                s = s * (SCALE * LOG2E) - cumr_ref[kj]
                if masked:
                    s = jnp.where(tri, s, NEG)
                m_new = jnp.maximum(m, jnp.max(s, axis=1, keepdims=True) + ci)
                a = jnp.exp2(m - m_new)
                p = jnp.exp2(s + (ci - m_new))
                l = a * l + jnp.sum(p, axis=1, keepdims=True)
                acc = a * acc + jnp.dot(p.astype(BF16), v_ref[cols, :], preferred_element_type=F32)
                return m_new, l, acc

            init = (jnp.full((tq, 1), NEG, F32), jnp.zeros((tq, 1), F32), jnp.zeros((tq, HEAD_DIM), F32))
            carry = lax.fori_loop(0, qi, lambda kj, c: tile(kj, c, False), init)
            m, l, acc = tile(qi, carry, True)
            o = acc / l
            g = g_ref[rows, :]
            o_ref[rows, :] = o
            y_ref[rows, :] = (o * (g * _sigmoid(g))).astype(BF16)
            lse_ref[rows, :] = m * LN2 + jnp.log(l)
            return 0

        lax.fori_loop(0, nq, q_step, 0)

    return pl.pallas_call(
        body, name="attn_fwd", grid=(N_HEADS,),
        in_specs=[_colblock(S, QB), _colblock(S, KB), _colblock(S, VB), _colblock(S, GB),
                  pl.BlockSpec((S, LANES), lambda h: (0, 0)), _head_row(nq, tq)],
        out_specs=(_colblock(S, 0), _colblock(S, 0), _head_col(S)),
        out_shape=(jax.ShapeDtypeStruct((S, ATT_W), F32), jax.ShapeDtypeStruct((S, D_MODEL), BF16),
                   jax.ShapeDtypeStruct((N_HEADS, S, 1), F32)),
        scratch_shapes=[pltpu.VMEM((S, 1), F32)],
        compiler_params=_params(("arbitrary",)),
    )(qkv, qkv, qkv, rest, cum, cumr)


def _attn_bwd_pre(rest, o, dy, cum, lse):
    S = o.shape[0]

    def body(g_ref, o_ref, dy_ref, cum_ref, lse_ref, do_ref, dg_ref, a_ref, dl_ref):
        g = g_ref[...]
        sg = _sigmoid(g)
        ov = o_ref[...]
        dy = dy_ref[...]
        do = dy * (g * sg)
        do_ref[...] = do.astype(BF16)
        dg_ref[...] = (dy * ov * (sg * (1.0 + g * (1.0 - sg)))).astype(BF16)
        dl_ref[...] = jnp.sum(do * ov, axis=1, keepdims=True)
        a_ref[...] = (_lane_of(cum_ref, pl.program_id(0)) - lse_ref[...]) * LOG2E

    att = jax.ShapeDtypeStruct((S, ATT_W), BF16)
    col = jax.ShapeDtypeStruct((N_HEADS, S, 1), F32)
    return pl.pallas_call(
        body, name="attn_bwd_pre", grid=(N_HEADS,),
        in_specs=[_colblock(S, GB), _colblock(S, 0), _colblock(S, 0),
                  pl.BlockSpec((S, LANES), lambda h: (0, 0)), _head_col(S)],
        out_specs=(_colblock(S, 0), _colblock(S, 0), _head_col(S), _head_col(S)),
        out_shape=(att, att, col, col),
        compiler_params=_params(("parallel",)),
    )(rest, o, dy, cum, lse)


def _attn_bwd(qkv, do, a, delta, cumr, *, tq):
    S = qkv.shape[0]
    nq = S // tq
    tdot = (((0,), (0,)), ((), ()))

    def body(q_ref, k_ref, v_ref, do_ref, a_ref, dl_ref, cumr_ref, dq_ref, dk_ref, dv_ref, dcr_ref, dcc_ref, dqa):
        dqa[...] = jnp.zeros_like(dqa)

        @pl.when(pl.program_id(0) == 0)
        def _():
            dcr_ref[...] = jnp.zeros_like(dcr_ref)

        tri = (lax.broadcasted_iota(jnp.int32, (tq, tq), 1) <= lax.broadcasted_iota(jnp.int32, (tq, tq), 0))
        mine = lax.broadcasted_iota(jnp.int32, (tq, LANES), 1) == pl.program_id(0)

        def k_step(kj, _):
            cols = pl.ds(pl.multiple_of(kj * tq, tq), tq)
            k = k_ref[cols, :]
            v = v_ref[cols, :]
            cj = cumr_ref[kj]

            def tile(qi, carry, masked):
                dk, dv, csum = carry
                rows = pl.ds(pl.multiple_of(qi * tq, tq), tq)
                q = q_ref[rows, :]
                dot = do_ref[rows, :]
                s = lax.dot_general(q, k, (((1,), (1,)), ((), ())), preferred_element_type=F32)
                p = jnp.exp2(s * (SCALE * LOG2E) + a_ref[rows, :] - cj)
                if masked:
                    p = jnp.where(tri, p, 0.0)
                dv = dv + lax.dot_general(p.astype(BF16), dot, tdot, preferred_element_type=F32)
                dp = lax.dot_general(dot, v, (((1,), (1,)), ((), ())), preferred_element_type=F32)
                ds = p * (dp - dl_ref[rows, :])
                dcr_ref[rows, :] += jnp.where(mine, jnp.sum(ds, axis=1, keepdims=True), 0.0)
                csum = csum + jnp.sum(ds, axis=0, keepdims=True)
                dsb = (ds * SCALE).astype(BF16)
                dqa[rows, :] += jnp.dot(dsb, k, preferred_element_type=F32)
                dk = dk + lax.dot_general(dsb, q, tdot, preferred_element_type=F32)
                return dk, dv, csum

            init = (jnp.zeros((tq, HEAD_DIM), F32), jnp.zeros((tq, HEAD_DIM), F32), jnp.zeros((1, tq), F32))
            carry = tile(kj, init, True)
            dk, dv, csum = lax.fori_loop(kj + 1, nq, lambda qi, c: tile(qi, c, False), carry)
            dk_ref[cols, :] = dk.astype(BF16)
            dv_ref[cols, :] = dv.astype(BF16)
            dcc_ref[kj] = csum
            return 0

        lax.fori_loop(0, nq, k_step, 0)
        dq_ref[...] = dqa[...].astype(BF16)

    att = jax.ShapeDtypeStruct((S, ATT_W), BF16)
    return pl.pallas_call(
        body, name="attn_bwd", grid=(N_HEADS,),
        in_specs=[_colblock(S, QB), _colblock(S, KB), _colblock(S, VB), _colblock(S, 0),
                  _head_col(S), _head_col(S), _head_row(nq, tq)],
        out_specs=(_colblock(S, 0), _colblock(S, 0), _colblock(S, 0), pl.BlockSpec((S, LANES), lambda h: (0, 0)),
                   _head_row(nq, tq)),
        out_shape=(att, att, att, jax.ShapeDtypeStruct((S, LANES), F32),
                   jax.ShapeDtypeStruct((N_HEADS, nq, 1, tq), F32)),
        scratch_shapes=[pltpu.VMEM((S, HEAD_DIM), F32)],
        compiler_params=_params(("arbitrary",)),
    )(qkv, qkv, qkv, do, a, delta, cumr)


def _conv_taps(u, w_ref, row):
    return (w_ref[0:1, :] * _shift_down(u, 2, row) + w_ref[1:2, :] * _shift_down(u, 1, row)) + w_ref[2:3, :] * u


def _conv_fwd(rest, cw, y):
    S = rest.shape[0]

    def body(cb_ref, cc_ref, ch_ref, g_ref, w_ref, y_in, y_ref):
        row = lax.broadcasted_iota(jnp.int32, (S, LANES), 0)
        g = g_ref[...]
        y = _conv_taps(cc_ref[...] * ch_ref[...], w_ref, row)
        y_ref[...] = ((cb_ref[...] * y) * (g * _sigmoid(g))).astype(BF16)

    return pl.pallas_call(
        body, name="conv_fwd", grid=(CONV_W // LANES,),
        in_specs=[_colblock(S, CBB), _colblock(S, CCB), _colblock(S, CHB), _colblock(S, CGB),
                  pl.BlockSpec((CONV_TAPS, LANES), lambda j: (0, j)), ANY],
        out_specs=_colblock(S, ATT_W // LANES),
        out_shape=jax.ShapeDtypeStruct(y.shape, y.dtype),
        input_output_aliases={5: 0},
        compiler_params=_params(("parallel",)),
    )(rest, rest, rest, rest, cw, y)


def _conv_bwd(rest, dy, cw):
    S = rest.shape[0]

    def body(cb_ref, cc_ref, ch_ref, g_ref, dy_ref, w_ref, dcb_ref, dcc_ref, dch_ref, dg_ref, dw_ref):
        row = lax.broadcasted_iota(jnp.int32, (S, LANES), 0)
        g = g_ref[...]
        sg = _sigmoid(g)
        silu = g * sg
        cb, cc, ch, dy = cb_ref[...], cc_ref[...], ch_ref[...], dy_ref[...]
        u = cc * ch
        u1 = _shift_down(u, 1, row)
        u2 = _shift_down(u, 2, row)
        y = (w_ref[0:1, :] * u2 + w_ref[1:2, :] * u1) + w_ref[2:3, :] * u
        dcb_ref[...] = (dy * silu * y).astype(BF16)
        dg_ref[...] = (dy * (cb * y) * (sg * (1.0 + g * (1.0 - sg)))).astype(BF16)
        dyv = dy * silu * cb
        du = w_ref[2:3, :] * dyv + w_ref[1:2, :] * _shift_up(dyv, 1, row) + w_ref[0:1, :] * _shift_up(dyv, 2, row)
        dcc_ref[...] = (du * ch).astype(BF16)
        dch_ref[...] = (du * cc).astype(BF16)
        dw_ref[0:1, :] = jnp.sum(dyv * u2, axis=0, keepdims=True)
        dw_ref[1:2, :] = jnp.sum(dyv * u1, axis=0, keepdims=True)
        dw_ref[2:3, :] = jnp.sum(dyv * u, axis=0, keepdims=True)

    act = jax.ShapeDtypeStruct((S, CONV_W), BF16)
    return pl.pallas_call(
        body, name="conv_bwd", grid=(CONV_W // LANES,),
        in_specs=[_colblock(S, CBB), _colblock(S, CCB), _colblock(S, CHB), _colblock(S, CGB),
                  _colblock(S, ATT_W // LANES), pl.BlockSpec((CONV_TAPS, LANES), lambda j: (0, j))],
        out_specs=(_colblock(S, 0),) * 4 + (pl.BlockSpec((CONV_TAPS, LANES), lambda j: (0, j)),),
        out_shape=(act, act, act, act, jax.ShapeDtypeStruct((CONV_TAPS, CONV_W), F32)),
        compiler_params=_params(("parallel",)),
    )(rest, rest, rest, rest, dy, cw)


def _pool_z(u, grp, row):
    s2 = u + _shift_down(u, 1, row)
    s4 = s2 + _shift_down(s2, 2, row)
    s8 = s4 + _shift_down(s4, 4, row)
    s16 = s8 + _shift_down(s8, 8, row)
    sw = jnp.where(grp == 0, s2, jnp.where(grp == 1, s4, jnp.where(grp == 2, s8, s16)))
    return sw / _pool_count(grp, row) - u


def _pool_count(grp, row):
    return jnp.minimum(row + 1, 2 << grp).astype(F32)


def _pool_fwd(rest, pw, scale, y):
    S = rest.shape[0]

    def body(u_ref, g_ref, pw_ref, sc_ref, y_in, y_ref):
        grp = pl.program_id(0)
        row = lax.broadcasted_iota(jnp.int32, (S, LANES), 0)
        z = _pool_z(u_ref[...], grp, row)
        zp = jnp.dot(z.astype(BF16), pw_ref[...].astype(BF16), preferred_element_type=F32)
        g = g_ref[...]
        y_ref[...] = ((zp * sc_ref[...]) * (g * _sigmoid(g))).astype(BF16)

    return pl.pallas_call(
        body, name="pool_fwd", grid=(N_POOL,),
        in_specs=[_colblock(S, PUB), _colblock(S, PGB),
                  pl.BlockSpec((None, POOL_GROUP, POOL_GROUP), lambda j: (j, 0, 0)),
                  pl.BlockSpec((1, LANES), lambda j: (0, j)), ANY],
        out_specs=_colblock(S, (ATT_W + CONV_W) // LANES),
        out_shape=jax.ShapeDtypeStruct(y.shape, y.dtype),
        input_output_aliases={4: 0},
        compiler_params=_params(("parallel",)),
    )(rest, rest, pw, scale, y)


def _pool_bwd(rest, dy, pw, scale):
    S = rest.shape[0]

    def body(u_ref, g_ref, dy_ref, pw_ref, sc_ref, du_ref, dg_ref, dpw_ref, dsc_ref):
        grp = pl.program_id(0)
        row = lax.broadcasted_iota(jnp.int32, (S, LANES), 0)
        z = _pool_z(u_ref[...], grp, row).astype(BF16)
        pwb = pw_ref[...].astype(BF16)
        zp = jnp.dot(z, pwb, preferred_element_type=F32)
        g = g_ref[...]
        sg = _sigmoid(g)
        silu = g * sg
        dy = dy_ref[...]
        sc = sc_ref[...]
        dsc_ref[...] = jnp.sum(dy * silu * zp, axis=0, keepdims=True)
        dg_ref[...] = (dy * (zp * sc) * (sg * (1.0 + g * (1.0 - sg)))).astype(BF16)
        dzp = (dy * silu * sc).astype(BF16)
        dpw_ref[...] = lax.dot_general(z, dzp, (((0,), (0,)), ((), ())), preferred_element_type=F32)
        dz = lax.dot_general(dzp, pwb, (((1,), (1,)), ((), ())), preferred_element_type=F32)
        f1 = dz / _pool_count(grp, row)
        f2 = f1 + _shift_up(f1, 1, row)
        f4 = f2 + _shift_up(f2, 2, row)
        f8 = f4 + _shift_up(f4, 4, row)
        f16 = f8 + _shift_up(f8, 8, row)
        fw = jnp.where(grp == 0, f2, jnp.where(grp == 1, f4, jnp.where(grp == 2, f8, f16)))
        du_ref[...] = (fw - dz).astype(BF16)

    act = jax.ShapeDtypeStruct((S, POOL_W), BF16)
    return pl.pallas_call(
        body, name="pool_bwd", grid=(N_POOL,),
        in_specs=[_colblock(S, PUB), _colblock(S, PGB), _colblock(S, (ATT_W + CONV_W) // LANES),
                  pl.BlockSpec((None, POOL_GROUP, POOL_GROUP), lambda j: (j, 0, 0)),
                  pl.BlockSpec((1, LANES), lambda j: (0, j))],
        out_specs=(_colblock(S, 0), _colblock(S, 0),
                   pl.BlockSpec((None, POOL_GROUP, POOL_GROUP), lambda j: (j, 0, 0)),
                   pl.BlockSpec((1, LANES), lambda j: (0, j))),
        out_shape=(act, act, jax.ShapeDtypeStruct((N_POOL, POOL_GROUP, POOL_GROUP), F32),
                   jax.ShapeDtypeStruct((1, POOL_W), F32)),
        compiler_params=_params(("parallel",)),
    )(rest, rest, dy, pw, scale)


def _ln_fwd(z, g, b, *, ts):
    S, D = z.shape

    def body(z_ref, g_ref, b_ref, o_ref, ob_ref, obt_ref, xh_ref, rs_ref):
        zz = z_ref[...]
        mu = jnp.mean(zz, axis=1, keepdims=True)
        zc = zz - mu
        rstd = lax.rsqrt(jnp.mean(zc * zc, axis=1, keepdims=True) + LN_EPS)
        xh = zc * rstd
        out = xh * g_ref[...] + b_ref[...]
        o_ref[...] = out
        ob_ref[...] = out.astype(BF16)
        obt_ref[...] = out.T.astype(BF16)
        xh_ref[...] = xh
        rs_ref[...] = rstd

    rowblk = pl.BlockSpec((ts, D), lambda i: (i, 0))
    vec = pl.BlockSpec((1, D), lambda i: (0, 0))
    return pl.pallas_call(
        body, name="ln_fwd", grid=(S // ts,),
        in_specs=[rowblk, vec, vec],
        out_specs=(rowblk, rowblk, pl.BlockSpec((D, ts), lambda i: (0, i)), rowblk,
                   pl.BlockSpec((ts, 1), lambda i: (i, 0))),
        out_shape=(jax.ShapeDtypeStruct((S, D), F32), jax.ShapeDtypeStruct((S, D), BF16),
                   jax.ShapeDtypeStruct((D, S), BF16), jax.ShapeDtypeStruct((S, D), F32),
                   jax.ShapeDtypeStruct((S, 1), F32)),
        compiler_params=_params(("parallel",)),
    )(z, g, b)


def _prep(x, after, *, ts):
    S, D = x.shape

    def body(x_ref, after_ref, xb_ref, xbt_ref):
        xb_ref[...] = x_ref[...].astype(BF16)
        xbt_ref[...] = x_ref[...].T.astype(BF16)

    return pl.pallas_call(
        body, name="prep", grid=(S // ts,),
        in_specs=[pl.BlockSpec((ts, D), lambda i: (i, 0)), ANY],
        out_specs=(pl.BlockSpec((ts, D), lambda i: (i, 0)), pl.BlockSpec((D, ts), lambda i: (0, i))),
        out_shape=(jax.ShapeDtypeStruct((S, D), BF16), jax.ShapeDtypeStruct((D, S), BF16)),
        compiler_params=_params(("parallel",)),
    )(x, after)


def _ln_bwd(dout, xhat, rstd, g, *, ts):
    S, D = dout.shape

    def body(d_ref, xh_ref, rs_ref, g_ref, dz_ref, dzb_ref, dg_ref, db_ref):
        @pl.when(pl.program_id(0) == 0)
        def _():
            dg_ref[...] = jnp.zeros_like(dg_ref)
            db_ref[...] = jnp.zeros_like(db_ref)

        d = d_ref[...]
        xh = xh_ref[...]
        dxh = d * g_ref[...]
        m1 = jnp.mean(dxh, axis=1, keepdims=True)
        m2 = jnp.mean(dxh * xh, axis=1, keepdims=True)
        dz = rs_ref[...] * (dxh - m1 - xh * m2)
        dz_ref[...] = dz
        dzb_ref[...] = dz.astype(BF16)
        dg_ref[...] += jnp.sum(d * xh, axis=0, keepdims=True)
        db_ref[...] += jnp.sum(d, axis=0, keepdims=True)

    rowblk = pl.BlockSpec((ts, D), lambda i: (i, 0))
    vec = pl.BlockSpec((1, D), lambda i: (0, 0))
    return pl.pallas_call(
        body, name="ln_bwd", grid=(S // ts,),
        in_specs=[rowblk, rowblk, pl.BlockSpec((ts, 1), lambda i: (i, 0)), vec],
        out_specs=(rowblk, rowblk, vec, vec),
        out_shape=(jax.ShapeDtypeStruct((S, D), F32), jax.ShapeDtypeStruct((S, D), BF16),
                   jax.ShapeDtypeStruct((1, D), F32), jax.ShapeDtypeStruct((1, D), F32)),
        compiler_params=_params(("arbitrary",)),
    )(dout, xhat, rstd, g)


def _loss_head(y, tgt, *, ts):
    S, D = y.shape

    def body(y_ref, t_ref, l_ref, dy_ref):
        @pl.when(pl.program_id(0) == 0)
        def _():
            l_ref[...] = jnp.zeros_like(l_ref)

        e = y_ref[...] - t_ref[...]
        dy_ref[...] = e * (1.0 / D)
        rowloss = jnp.sum(e * e, axis=1, keepdims=True) * (0.5 / D)
        l_ref[...] += jnp.sum(rowloss, axis=0, keepdims=True)

    rowblk = pl.BlockSpec((ts, D), lambda i: (i, 0))
    return pl.pallas_call(
        body, name="loss_head", grid=(S // ts,),
        in_specs=[rowblk, rowblk],
        out_specs=(pl.BlockSpec((1, 1), lambda i: (0, 0)), rowblk),
        out_shape=(jax.ShapeDtypeStruct((1, 1), F32), jax.ShapeDtypeStruct((S, D), F32)),
        compiler_params=_params(("arbitrary",)),
    )(y, tgt)


def _adamw(lands, srcs, w, m, v, *, name, first=0, into=None):
    R, C = w.shape
    npart = len(lands)
    rp = lands[0].shape[1]
    assert R % rp == 0 and all(p.shape == (N_DEV, rp, C) for p in lands)
    tiled = rp % 8 == 0
    assert tiled or (first == 0 and npart * rp == R and into is None)
    tr = min(ADAMW_BLOCK_ELEMS // (LANES * pl.cdiv(C, LANES)), rp) if tiled else rp
    assert rp % tr == 0
    steps = rp // tr
    c1 = 1.0 - ADAM_B1 ** ADAM_STEP
    c2 = 1.0 - ADAM_B2 ** ADAM_STEP
    slabbed = [p.ndim == 3 for p in srcs]
    n_into = 0 if into is None else 4

    def body(*refs):
        l_refs, s_refs = refs[:npart], refs[npart:2 * npart]
        w_ref, m_ref, v_ref = refs[2 * npart:2 * npart + 3]
        g_ref, d_ref, nm_ref, nv_ref = refs[2 * npart + 3 + n_into:]
        x, y, c = _my_place()
        me = 4 * x + 2 * y + c

        def update(l, rows):
            g = None
            for j in range(N_DEV):
                own = s_refs[l][j] if slabbed[l] else s_refs[l][...]
                p = jnp.where(me == j, own, l_refs[l][j]).astype(F32)
                g = p if g is None else g + p
            nm = ADAM_B1 * m_ref[rows, :] + (1.0 - ADAM_B1) * g
            nv = ADAM_B2 * v_ref[rows, :] + (1.0 - ADAM_B2) * (g * g)
            g_ref[rows, :] = g
            nm_ref[rows, :] = nm
            nv_ref[rows, :] = nv
            d_ref[rows, :] = -ADAM_LR * ((nm / c1) / (jnp.sqrt(nv / c2) + ADAM_EPS) + ADAM_WD * w_ref[rows, :])

        for l in range(npart):
            if tiled:
                pl.when(pl.program_id(0) == l)(lambda l=l: update(l, slice(None)))
            else:
                update(l, slice(l * rp, (l + 1) * rp))

    def part_spec(l, p):
        if p.ndim == 3:
            return pl.BlockSpec((N_DEV, tr, C), lambda q, i: (0, jnp.where(q == l, i, 0), 0))
        return pl.BlockSpec((tr, C), lambda q, i: (jnp.where(q == l, i, 0), 0))

    if tiled:
        blk = pl.BlockSpec((tr, C), lambda q, i: ((first + q) * steps + i, 0))
    else:
        blk = pl.BlockSpec((R, C), lambda q, i: (0, 0))
    out = jax.ShapeDtypeStruct((R, C), F32)
    n_in = 2 * npart + 3
    return pl.pallas_call(
        body, name=name, grid=(npart, steps) if tiled else (1, 1),
        in_specs=[part_spec(l, p) for l, p in enumerate(lands)] + [part_spec(l, p) for l, p in enumerate(srcs)]
        + [blk, blk, blk] + [ANY] * n_into,
        out_specs=(blk, blk, blk, blk),
        out_shape=(out, out, out, out),
        input_output_aliases={n_in + i: i for i in range(n_into)},
        compiler_params=_params(("arbitrary", "arbitrary")),
    )(*lands, *srcs, w, m, v, *(into or ()))


def _my_place():
    return lax.axis_index("x"), lax.axis_index("y"), lax.axis_index("c")


HBM = pl.BlockSpec(memory_space=pltpu.HBM)
SEM = pl.BlockSpec(memory_space=pltpu.SEMAPHORE)
DATAFLOW = pltpu.SideEffectType.DATAFLOW_SIDE_EFFECTING
N_PEER = N_DEV - 1


def _xchg_copies(src, land, send_sem, recv_sem, gather):
    x, y, c = _my_place()
    me = 4 * x + 2 * y + c
    copies = []
    for k in range(N_PEER):
        px, py, pc = x ^ (((k + 1) >> 2) & 1), y ^ (((k + 1) >> 1) & 1), c ^ ((k + 1) & 1)
        peer = 4 * px + 2 * py + pc
        for a in range(len(src)):
            i = a * N_PEER + k
            out = src[a] if gather[a] else src[a].at[peer]
            sent, landed = [pltpu.make_async_remote_copy(
                src_ref=out, dst_ref=land[a].at[slab], send_sem=send_sem.at[i], recv_sem=recv_sem.at[i],
                device_id=(px, py, pc), device_id_type=MESH) for slab in (me, peer)]
            copies.append((sent, landed))
    return copies


def _xchg_start(srcs, *, gather, name):
    n = len(srcs)
    lands = [lax.empty(((N_DEV,) + v.shape) if g else v.shape, v.dtype) for v, g in zip(srcs, gather)]

    def body(*refs):
        src, land = refs[:n], refs[n:2 * n]
        send_sem, recv_sem = refs[2 * n], refs[2 * n + 1]
        token = refs[-1]
        for sent, _ in _xchg_copies(src, land, send_sem, recv_sem, gather):
            sent.start()
        token[...] = jnp.zeros_like(token)

    sems = pltpu.SemaphoreType.DMA((n * N_PEER,))
    outs = pl.pallas_call(
        body, name=name,
        out_shape=(sems, sems, *[pltpu.HBM(v.shape, v.dtype) for v in srcs + lands],
                   jax.ShapeDtypeStruct((8, LANES), F32)),
        in_specs=[HBM] * (2 * n),
        out_specs=(SEM, SEM, *[HBM] * (2 * n), pl.BlockSpec(memory_space=pltpu.VMEM)),
        input_output_aliases={i: 2 + i for i in range(2 * n)},
        compiler_params=pltpu.CompilerParams(has_side_effects=DATAFLOW),
    )(*[pltpu.with_memory_space_constraint(v, pltpu.HBM) for v in srcs + lands])
    return dict(send=outs[0], recv=outs[1], srcs=list(outs[2:2 + n]), lands=list(outs[2 + n:2 + 2 * n]),
                token=outs[-1], gather=gather)


def _xchg_wait(h, after, *, name):
    n = len(h["srcs"])
    gather = h["gather"]

    def body(*refs):
        src, land = refs[:n], refs[n:2 * n]
        send_sem, recv_sem = refs[2 * n], refs[2 * n + 1]
        for sent, landed in _xchg_copies(src, land, send_sem, recv_sem, gather):
            sent.wait_send()
            landed.wait_recv()

    thru = h["srcs"] + h["lands"]
    outs = pl.pallas_call(
        body, name=name,
        out_shape=[pltpu.HBM(v.shape, v.dtype) for v in thru],
        in_specs=[HBM] * (2 * n) + [SEM, SEM, ANY],
        out_specs=[HBM] * (2 * n),
        input_output_aliases={i: i for i in range(2 * n)},
        compiler_params=pltpu.CompilerParams(has_side_effects=DATAFLOW),
    )(*thru, h["send"], h["recv"], after)
    return list(outs[:n]), list(outs[n:])


def _with_own(land, own, me):
    slot = lax.broadcasted_iota(jnp.int32, land.shape, 0)
    return jnp.where(slot == me, own if own.ndim == land.ndim else own[None], land)


def _w_in_segments(j, cols):
    lo, hi = j * cols, (j + 1) * cols
    segs = []
    for a, b, where, shift in ((0, FG_AT, "main", 0), (FG_AT, FG_AT + N_HEADS, "fg", -FG_AT),
                               (FG_AT + N_HEADS, IN_W, "main", -N_HEADS)):
        s0, s1 = max(lo, a), min(hi, b)
        if s0 < s1:
            segs.append((s0 - lo, s1 - s0, where, s0 + shift))
    return segs


SLAB_W = 1024


def _window(off, width):
    base = min((off // LANES) * LANES, width - SLAB_W)
    return base, off - base


def _w_unpack(land, own, *, tr):
    _, D, cols = land.shape
    assert cols + LANES - 1 <= SLAB_W

    def body(land_ref, own_ref, main_ref, fg_ref, slab):
        x, y, c = _my_place()
        me = 4 * x + 2 * y + c
        lane = lax.broadcasted_iota(jnp.int32, (tr, SLAB_W), 1)
        main_ref[...] = jnp.zeros_like(main_ref)
        slab[...] = jnp.zeros_like(slab)
        for j in range(N_DEV):
            slab[:, :cols] = jnp.where(me == j, own_ref[...], land_ref[j])
            v = slab[...]
            segs = _w_in_segments(j, cols)
            for lo, n, where, dst in segs:
                part = v if len(segs) == 1 else jnp.where((lane >= lo) & (lane < lo + n), v, jnp.zeros_like(v))
                if where == "fg":
                    fg_ref[...] = pltpu.roll(part, (SLAB_W - lo) % SLAB_W, 1)[:, :LANES]
                else:
                    base, r = _window(dst - lo, MAIN_W)
                    assert 0 <= r and lo + n + r <= SLAB_W
                    main_ref[:, base:base + SLAB_W] += pltpu.roll(part, r, 1)

    return pl.pallas_call(
        body, name="w_unpack", grid=(D // tr,),
        in_specs=[pl.BlockSpec((N_DEV, tr, cols), lambda i: (0, i, 0)), pl.BlockSpec((tr, cols), lambda i: (i, 0))],
        out_specs=(pl.BlockSpec((tr, MAIN_W), lambda i: (i, 0)), pl.BlockSpec((tr, LANES), lambda i: (i, 0))),
        out_shape=(jax.ShapeDtypeStruct((D, MAIN_W), BF16), jax.ShapeDtypeStruct((D, LANES), BF16)),
        scratch_shapes=[pltpu.VMEM((tr, SLAB_W), BF16)],
        compiler_params=_params(("parallel",)),
    )(land, own)


def _g_pack(dw_main, dw_fg, cols, *, tr):
    D = dw_main.shape[0]

    def body(main_ref, fg_ref, out_ref, slab):
        lane = lax.broadcasted_iota(jnp.int32, (tr, SLAB_W), 1)
        slab[...] = jnp.zeros_like(slab)
        slab[:, :LANES] = fg_ref[...]
        fg = slab[...]
        for j in range(N_DEV):
            v = None
            for lo, n, where, src in _w_in_segments(j, cols):
                if where == "fg":
                    part = pltpu.roll(fg, lo, 1)
                else:
                    base, r = _window(src - lo, MAIN_W)
                    assert 0 <= r and lo + n + r <= SLAB_W
                    part = pltpu.roll(main_ref[:, base:base + SLAB_W], (SLAB_W - r) % SLAB_W, 1)
                v = part if v is None else jnp.where((lane >= lo) & (lane < lo + n), part, v)
            out_ref[j] = v[:, :cols]

    return pl.pallas_call(
        body, name="g_pack", grid=(D // tr,),
        in_specs=[pl.BlockSpec((tr, MAIN_W), lambda i: (i, 0)), pl.BlockSpec((tr, LANES), lambda i: (i, 0))],
        out_specs=pl.BlockSpec((N_DEV, tr, cols), lambda i: (0, i, 0)),
        out_shape=jax.ShapeDtypeStruct((N_DEV, D, cols), BF16),
        scratch_shapes=[pltpu.VMEM((tr, SLAB_W), BF16)],
        compiler_params=_params(("parallel",)),
    )(dw_main, dw_fg)


TQ = 512
TS = 256
W_TILE_ROWS = 128


def _layer_fwd(x, xb, xbt, wl):
    qkv = _mm(xb, wl["w_main"], out_dtype=BF16, tm=1024, tn=512, tk=D_MODEL, name="mm_qkv", n_cols=QKV_W)
    rest = _mm(xb, wl["w_main"], out_dtype=F32, tm=1024, tn=512, tk=D_MODEL, name="mm_rest", n_off=QKV_W, n_cols=REST_W)
    fgp = _mm(xb, wl["w_fg"], out_dtype=F32, tm=1024, tn=LANES, tk=D_MODEL, name="mm_fg")
    cum = _fg_fwd(fgp, wl["b_f"])
    S = x.shape[0]
    cumr = (cum[:, :N_HEADS].T * LOG2E).reshape(N_HEADS, S // TQ, 1, TQ)
    o, y, lse = _attn_fwd(qkv, rest, cum, cumr, tq=TQ)
    y = _conv_fwd(rest, wl["conv_w"], y)
    y = _pool_fwd(rest, wl["pool_w"], wl["pool_scale"], y)
    z = _mm(y, wl["w_out"], out_dtype=F32, tm=1024, tn=512, tk=D_MODEL, name="mm_out", c=x, c_scale=ALPHA)
    out, outb, outbt, xhat, rstd = _ln_fwd(z, wl["ln_g"], wl["ln_b"], ts=TS)
    saved = dict(xbt=xbt, qkv=qkv, rest=rest, fgp=fgp, cum=cum, cumr=cumr, o=o, lse=lse, y=y, xhat=xhat, rstd=rstd)
    return out, outb, outbt, saved


def _layer_bwd_weights(dout, sv, wl):
    S = dout.shape[0]
    dz, dzb, dln_g, dln_b = _ln_bwd(dout, sv["xhat"], sv["rstd"], wl["ln_g"], ts=TS)
    dy = _mm(dzb, wl["w_out"], out_dtype=F32, tm=1024, tn=512, tk=D_MODEL, name="mm_dy", bt=True)
    dw_out = _mm(sv["y"], dzb, out_dtype=BF16, tm=512, tn=D_MODEL, tk=512, name="mm_dw_out", at=True)
    rest = sv["rest"]
    do, dg_att, a, delta = _attn_bwd_pre(rest, sv["o"], dy, sv["cum"], sv["lse"])
    dq, dk, dv, dcr, dcc = _attn_bwd(sv["qkv"], do, a, delta, sv["cumr"], tq=TQ)
    dcum = dcr - jnp.pad(dcc.reshape(N_HEADS, S).T, ((0, 0), (0, LANES - N_HEADS)))
    dfg, db_f = _fg_bwd(dcum, sv["fgp"], wl["b_f"])
    dcb, dcc_, dch, dgc, dconv_w = _conv_bwd(rest, dy, wl["conv_w"])
    dpu, dgp, dpool_w, dpool_scale = _pool_bwd(rest, dy, wl["pool_w"], wl["pool_scale"])
    dproj = jnp.concatenate([dq, dk, dv, dg_att, dcb, dcc_, dch, dgc, dpu, dgp], axis=1)
    xt = sv["xbt"]
    dw_main = _mm(xt, dproj, out_dtype=BF16, tm=1024, tn=512, tk=2048, name="mm_dw_main")
    dw_fg = _mm(xt, dfg, out_dtype=BF16, tm=1024, tn=LANES, tk=2048, name="mm_dw_fg")
    grads = dict(w_main=dw_main, w_fg=dw_fg, b_f=db_f[:, :N_HEADS], conv_w=dconv_w, pool_w=dpool_w, pool_scale=dpool_scale,
                 w_out=dw_out, ln_g=dln_g, ln_b=dln_b)
    return grads, (dproj, dfg, dz)


def _layer_bwd_input(ctx, wl, after):
    dproj, dfg, dz = ctx
    t = _mm(dfg, wl["w_fg"], out_dtype=F32, tm=1024, tn=512, tk=LANES, name="mm_dx_fg", c=dz, c_scale=ALPHA, bt=True,
            after=after)
    return _mm(dproj, wl["w_main"], out_dtype=F32, tm=1024, tn=512, tk=MAIN_W // 2, name="mm_dx", c=t, c_scale=1.0, bt=True)


def kernel(x, w_in, b_f, conv_w, pool_w, pool_scale, w_out, ln_g, ln_b, loss_target, m_w_in, m_b_f, m_conv_w, m_pool_w, m_pool_scale, m_w_out, m_ln_g, m_ln_b, v_w_in, v_b_f, v_conv_w, v_pool_w, v_pool_scale, v_w_out, v_ln_g, v_ln_b):
    L, D, cols = w_in.shape
    rows_out = w_out.shape[1]
    ccols = conv_w.shape[2]
    mx, my, mc = _my_place()
    me = 4 * mx + 2 * my + mc

    w_in_b, w_out_b = w_in.astype(BF16), w_out.astype(BF16)
    gathers = [_xchg_start([w_in_b[l], w_out_b[l], conv_w[l]], gather=[True] * 3, name=f"gather_start_{l}")
               for l in range(L)]
    started = gathers[0]["token"]
    for h in gathers[1:]:
        started = started + h["token"]

    xl = x[0]
    xb, xbt = _prep(xl, started, ts=TS)
    weights, saved = [], []
    for l in range(L):
        (own_in, own_out, own_cw), (g_in, g_out, g_cw) = _xchg_wait(gathers[l], xb, name=f"gather_wait_{l}")
        w_main, w_fg = _w_unpack(g_in, own_in, tr=W_TILE_ROWS)
        w_o = _with_own(g_out, own_out, me).reshape(N_DEV * rows_out, D)
        cw = jnp.transpose(_with_own(g_cw, own_cw, me), (1, 0, 2)).reshape(CONV_TAPS, N_DEV * ccols)
        wl = dict(w_main=w_main, w_fg=w_fg, w_out=w_o, conv_w=cw,
                  b_f=jnp.pad(b_f[l][None, :], ((0, 0), (0, LANES - N_HEADS))), pool_w=pool_w[l],
                  pool_scale=pool_scale[l][None, :], ln_g=ln_g[l][None, :], ln_b=ln_b[l][None, :])
        xl, xb, xbt, sv = _layer_fwd(xl, xb, xbt, wl)
        weights.append(wl)
        saved.append(sv)

    loss, dx = _loss_head(xl, loss_target[0], ts=TS)
    loss = lax.psum(loss[0, 0], ("x", "y", "c"))

    sharded, small = ["w_in", "w_out", "conv_w"], ["b_f", "pool_w", "pool_scale", "ln_g", "ln_b"]
    names = sharded + small
    scatters = [None] * L
    for l in reversed(range(L)):
        g, ctx = _layer_bwd_weights(dx, saved[l], weights[l])
        slabs = [_g_pack(g["w_main"], g["w_fg"], cols, tr=W_TILE_ROWS), g["w_out"].reshape(N_DEV, rows_out, D),
                 jnp.transpose(g["conv_w"].reshape(CONV_TAPS, N_DEV, ccols), (1, 0, 2))]
        scatters[l] = _xchg_start(slabs + [g[k] for k in small], gather=[False] * 3 + [True] * 5,
                                  name=f"scatter_start_{l}")
        dx = _layer_bwd_input(ctx, weights[l], scatters[l]["token"])

    given = dict(w_in=(w_in, m_w_in, v_w_in), b_f=(b_f, m_b_f, v_b_f), conv_w=(conv_w, m_conv_w, v_conv_w),
                 pool_w=(pool_w, m_pool_w, v_pool_w), pool_scale=(pool_scale, m_pool_scale, v_pool_scale),
                 w_out=(w_out, m_w_out, v_w_out), ln_g=(ln_g, m_ln_g, v_ln_g), ln_b=(ln_b, m_ln_b, v_ln_b))

    def update(k, layers, into=None):
        a = names.index(k)
        w, m, v = given[k]
        C = w.shape[-1]
        w2, m2, v2 = (t.reshape(-1, C) for t in (w, m, v))
        rp = w2.shape[0] // L
        lands = [landed[l][a].reshape(N_DEV, rp, C) for l in layers]
        srcs = [sent[l][a].reshape((N_DEV, rp, C) if k in sharded else (rp, C)) for l in layers]
        return _adamw(lands, srcs, w2, m2, v2, name=f"adamw_{k}_{layers[0]}", first=layers[0], into=into)

    sent, landed = [None] * L, [None] * L
    for l in reversed(range(1, L)):
        sent[l], landed[l] = _xchg_wait(scatters[l], dx, name=f"scatter_wait_{l}")
    big = ["w_in", "w_out", "pool_w"]
    upper = {k: update(k, list(range(1, L))) for k in big} if L > 1 else {}
    sent[0], landed[0] = _xchg_wait(scatters[0], upper["w_in"][0] if upper else dx, name="scatter_wait_0")
    res = {}
    for k in names:
        outs = update(k, [0], into=upper[k]) if k in upper else update(k, list(range(L)))
        res[k] = [t.reshape(given[k][0].shape) for t in outs]

    order = ["w_in", "b_f", "conv_w", "pool_w", "pool_scale", "w_out", "ln_g", "ln_b"]
    return (loss, dx[None], *[res[k][0] for k in order], *[res[k][1] for k in order],
            *[res[k][2] for k in order], *[res[k][3] for k in order])
```

```python
import jax
import jax.numpy as jnp
from jax import lax
from jax.experimental import pallas as pl
from jax.experimental.pallas import tpu as pltpu

F32 = jnp.float32
BF16 = jnp.bfloat16

N_DEV = 8
D_MODEL = 2048
N_HEADS = 8
HEAD_DIM = 128
ATT_W = N_HEADS * HEAD_DIM
CONV_W = 512
CONV_TAPS = 3
POOL_W = 512
POOL_GROUP = 128
N_POOL = POOL_W // POOL_GROUP
FG_AT = 4 * ATT_W
IN_W = FG_AT + N_HEADS + 4 * CONV_W + 2 * POOL_W
MAIN_W = IN_W - N_HEADS
LANES = 128
LN_EPS = 1e-5
DEPTH = 4
ALPHA = (2 * DEPTH) ** 0.25
SCALE = HEAD_DIM ** -0.5
LOG2E = 1.4426950408889634
LN2 = 0.6931471805599453
NEG = -1e30

ADAM_LR, ADAM_B1, ADAM_B2, ADAM_EPS, ADAM_WD, ADAM_STEP = 0.001, 0.9, 0.999, 1e-08, 0.01, 10

QKV_W = 3 * ATT_W
REST_W = MAIN_W - QKV_W
QB, KB, VB = 0, 8, 16
GB = 0
CBB, CCB, CHB, CGB = 8, 12, 16, 20
PUB, PGB = 24, 28
VMEM_LIMIT = 48 * 1024 * 1024
ADAMW_BLOCK_ELEMS = 64 * 1024

MESH = pl.DeviceIdType.MESH
ANY = pl.BlockSpec(memory_space=pl.ANY)


def _params(semantics):
    return pltpu.CompilerParams(dimension_semantics=semantics, vmem_limit_bytes=VMEM_LIMIT)


def _sigmoid(x):
    return 1.0 / (1.0 + jnp.exp(-x))


def _mm(a, b, *, out_dtype, tm, tn, tk, name, c=None, c_scale=1.0, n_off=0, n_cols=None, bt=False, at=False,
        after=None):
    M, K = a.shape[::-1] if at else a.shape
    N = b.shape[0 if bt else 1] if n_cols is None else n_cols
    assert not (bt and n_off)
    dims = (((0 if at else 1,), (1 if bt else 0,)), ((), ()))
    tm, tn, tk = min(tm, M), min(tn, N), min(tk, K)
    assert M % tm == 0 and N % tn == 0 and K % tk == 0 and n_off % tn == 0, (a.shape, b.shape, tm, tn, tk)
    nk, joff = K // tk, n_off // tn
    has_c, has_after = c is not None, after is not None

    def body(*refs):
        a_ref, b_ref = refs[0], refs[1]
        c_ref = refs[2] if has_c else None
        o_ref = refs[2 + has_c + has_after]

        def finish(r):
            if has_c:
                r = r + c_scale * c_ref[...]
            o_ref[...] = r.astype(out_dtype)

        if nk == 1:
            finish(lax.dot_general(a_ref[...], b_ref[...], dims, preferred_element_type=F32))
        else:
            acc_ref = refs[3 + has_c + has_after]
            k = pl.program_id(2)

            @pl.when(k == 0)
            def _():
                acc_ref[...] = jnp.zeros_like(acc_ref)

            acc_ref[...] += lax.dot_general(a_ref[...], b_ref[...], dims, preferred_element_type=F32)

            @pl.when(k == nk - 1)
            def _():
                finish(acc_ref[...])

    in_specs = [pl.BlockSpec((tk, tm), lambda i, j, k: (k, i)) if at else pl.BlockSpec((tm, tk), lambda i, j, k: (i, k)),
                pl.BlockSpec((tn, tk), lambda i, j, k: (j, k)) if bt
                else pl.BlockSpec((tk, tn), lambda i, j, k: (k, j + joff))]
    args = [a, b]
    if has_c:
        in_specs.append(pl.BlockSpec((tm, tn), lambda i, j, k: (i, j)))
        args.append(c)
    if has_after:
        in_specs.append(ANY)
        args.append(after)
    return pl.pallas_call(
        body, name=name, grid=(M // tm, N // tn, nk),
        in_specs=in_specs,
        out_specs=pl.BlockSpec((tm, tn), lambda i, j, k: (i, j)),
        out_shape=jax.ShapeDtypeStruct((M, N), out_dtype),
        scratch_shapes=[pltpu.VMEM((tm, tn), F32)] if nk > 1 else [],
        compiler_params=_params(("parallel", "parallel", "arbitrary")),
    )(*args)


def _shift_down(x, k, row):
    return jnp.where(row >= k, pltpu.roll(x, k, 0), 0.0)


def _shift_up(x, k, row):
    s = x.shape[0]
    return jnp.where(row < s - k, pltpu.roll(x, s - k, 0), 0.0)


def _fg_fwd(fgp, bf):
    S = fgp.shape[0]

    def body(f_ref, b_ref, cum_ref):
        a = f_ref[...] + b_ref[...]
        x = jnp.minimum(a, 0.0) - jnp.log(1.0 + jnp.exp(-jnp.abs(a)))
        row = lax.broadcasted_iota(jnp.int32, x.shape, 0)
        k = 1
        while k < S:
            x = x + _shift_down(x, k, row)
            k *= 2
        cum_ref[...] = x

    return pl.pallas_call(body, name="fg_fwd", out_shape=jax.ShapeDtypeStruct((S, LANES), F32),
                          compiler_params=_params(None))(fgp, bf)


def _fg_bwd(dcum, fgp, bf):
    S = fgp.shape[0]

    def body(d_ref, f_ref, b_ref, dfg_ref, dbf_ref):
        x = d_ref[...]
        row = lax.broadcasted_iota(jnp.int32, x.shape, 0)
        k = 1
        while k < S:
            x = x + _shift_up(x, k, row)
            k *= 2
        dfg = x * _sigmoid(-(f_ref[...] + b_ref[...]))
        dfg_ref[...] = dfg.astype(BF16)
        dbf_ref[...] = jnp.sum(dfg, axis=0, keepdims=True)

    return pl.pallas_call(
        body, name="fg_bwd",
        out_shape=(jax.ShapeDtypeStruct((S, LANES), BF16), jax.ShapeDtypeStruct((1, LANES), F32)),
        compiler_params=_params(None),
    )(dcum, fgp, bf)


def _colblock(S, off):
    return pl.BlockSpec((S, LANES), lambda h: (0, h + off))


def _head_col(S):
    return pl.BlockSpec((None, S, 1), lambda h: (h, 0, 0))


def _head_row(nq, tq):
    return pl.BlockSpec((None, nq, 1, tq), lambda h: (h, 0, 0, 0))


def _lane_of(ref, h):
    lane = lax.broadcasted_iota(jnp.int32, ref.shape, 1)
    return jnp.sum(jnp.where(lane == h, ref[...], 0.0), axis=1, keepdims=True)


def _attn_fwd(qkv, rest, cum, cumr, *, tq):
    S = qkv.shape[0]
    nq = S // tq

    def body(q_ref, k_ref, v_ref, g_ref, cum_ref, cumr_ref, o_ref, y_ref, lse_ref, cc):
        cc[...] = _lane_of(cum_ref, pl.program_id(0)) * LOG2E
        tri = (lax.broadcasted_iota(jnp.int32, (tq, tq), 1) <= lax.broadcasted_iota(jnp.int32, (tq, tq), 0))

        def q_step(qi, _):
            rows = pl.ds(pl.multiple_of(qi * tq, tq), tq)
            q = q_ref[rows, :]
            ci = cc[rows, :]

            def tile(kj, carry, masked):
                m, l, acc = carry
                cols = pl.ds(pl.multiple_of(kj * tq, tq), tq)
                s = lax.dot_general(q, k_ref[cols, :], (((1,), (1,)), ((), ())), preferred_element_type=F32)
                s = s * (SCALE * LOG2E) + ci - cumr_ref[kj]
                if masked:
                    s = jnp.where(tri, s, NEG)
                m_new = jnp.maximum(m, jnp.max(s, axis=1, keepdims=True))
                a = jnp.exp2(m - m_new)
                p = jnp.exp2(s - m_new)
                l = a * l + jnp.sum(p, axis=1, keepdims=True)
                acc = a * acc + jnp.dot(p.astype(BF16), v_ref[cols, :], preferred_element_type=F32)
                return m_new, l, acc

            init = (jnp.full((tq, 1), NEG, F32), jnp.zeros((tq, 1), F32), jnp.zeros((tq, HEAD_DIM), F32))
            carry = lax.fori_loop(0, qi, lambda kj, c: tile(kj, c, False), init)
            m, l, acc = tile(qi, carry, True)
            o = acc / l
            g = g_ref[rows, :]
            o_ref[rows, :] = o
            y_ref[rows, :] = (o * (g * _sigmoid(g))).astype(BF16)
            lse_ref[rows, :] = m * LN2 + jnp.log(l)
            return 0

        lax.fori_loop(0, nq, q_step, 0)

    return pl.pallas_call(
        body, name="attn_fwd", grid=(N_HEADS,),
        in_specs=[_colblock(S, QB), _colblock(S, KB), _colblock(S, VB), _colblock(S, GB),
                  pl.BlockSpec((S, LANES), lambda h: (0, 0)), _head_row(nq, tq)],
        out_specs=(_colblock(S, 0), _colblock(S, 0), _head_col(S)),
        out_shape=(jax.ShapeDtypeStruct((S, ATT_W), F32), jax.ShapeDtypeStruct((S, D_MODEL), BF16),
                   jax.ShapeDtypeStruct((N_HEADS, S, 1), F32)),
        scratch_shapes=[pltpu.VMEM((S, 1), F32)],
        compiler_params=_params(("arbitrary",)),
    )(qkv, qkv, qkv, rest, cum, cumr)


def _attn_bwd_pre(rest, o, dy, cum, lse):
    S = o.shape[0]

    def body(g_ref, o_ref, dy_ref, cum_ref, lse_ref, do_ref, dg_ref, a_ref, dl_ref):
        g = g_ref[...]
        sg = _sigmoid(g)
        ov = o_ref[...]
        dy = dy_ref[...]
        do = dy * (g * sg)
        do_ref[...] = do.astype(BF16)
        dg_ref[...] = (dy * ov * (sg * (1.0 + g * (1.0 - sg)))).astype(BF16)
        dl_ref[...] = jnp.sum(do * ov, axis=1, keepdims=True)
        a_ref[...] = (_lane_of(cum_ref, pl.program_id(0)) - lse_ref[...]) * LOG2E

    att = jax.ShapeDtypeStruct((S, ATT_W), BF16)
    col = jax.ShapeDtypeStruct((N_HEADS, S, 1), F32)
    return pl.pallas_call(
        body, name="attn_bwd_pre", grid=(N_HEADS,),
        in_specs=[_colblock(S, GB), _colblock(S, 0), _colblock(S, 0),
                  pl.BlockSpec((S, LANES), lambda h: (0, 0)), _head_col(S)],
        out_specs=(_colblock(S, 0), _colblock(S, 0), _head_col(S), _head_col(S)),
        out_shape=(att, att, col, col),
        compiler_params=_params(("parallel",)),
    )(rest, o, dy, cum, lse)


def _attn_bwd(qkv, do, a, delta, cumr, *, tq):
    S = qkv.shape[0]
    nq = S // tq
    tdot = (((0,), (0,)), ((), ()))

    def body(q_ref, k_ref, v_ref, do_ref, a_ref, dl_ref, cumr_ref, dq_ref, dk_ref, dv_ref, dcr_ref, dcc_ref, dqa):
        dqa[...] = jnp.zeros_like(dqa)

        @pl.when(pl.program_id(0) == 0)
        def _():
            dcr_ref[...] = jnp.zeros_like(dcr_ref)

        tri = (lax.broadcasted_iota(jnp.int32, (tq, tq), 1) <= lax.broadcasted_iota(jnp.int32, (tq, tq), 0))
        mine = lax.broadcasted_iota(jnp.int32, (tq, LANES), 1) == pl.program_id(0)

        def k_step(kj, _):
            cols = pl.ds(pl.multiple_of(kj * tq, tq), tq)
            k = k_ref[cols, :]
            v = v_ref[cols, :]
            cj = cumr_ref[kj]

            def tile(qi, carry, masked):
                dk, dv, csum = carry
                rows = pl.ds(pl.multiple_of(qi * tq, tq), tq)
                q = q_ref[rows, :]
                dot = do_ref[rows, :]
                s = lax.dot_general(q, k, (((1,), (1,)), ((), ())), preferred_element_type=F32)
                p = jnp.exp2(s * (SCALE * LOG2E) + a_ref[rows, :] - cj)
                if masked:
                    p = jnp.where(tri, p, 0.0)
                dv = dv + lax.dot_general(p.astype(BF16), dot, tdot, preferred_element_type=F32)
                dp = lax.dot_general(dot, v, (((1,), (1,)), ((), ())), preferred_element_type=F32)
                ds = p * (dp - dl_ref[rows, :])
                dcr_ref[rows, :] += jnp.where(mine, jnp.sum(ds, axis=1, keepdims=True), 0.0)
                csum = csum + jnp.sum(ds, axis=0, keepdims=True)
                dsb = (ds * SCALE).astype(BF16)
                dqa[rows, :] += jnp.dot(dsb, k, preferred_element_type=F32)
                dk = dk + lax.dot_general(dsb, q, tdot, preferred_element_type=F32)
                return dk, dv, csum

            init = (jnp.zeros((tq, HEAD_DIM), F32), jnp.zeros((tq, HEAD_DIM), F32), jnp.zeros((1, tq), F32))
            carry = tile(kj, init, True)
            dk, dv, csum = lax.fori_loop(kj + 1, nq, lambda qi, c: tile(qi, c, False), carry)
            dk_ref[cols, :] = dk.astype(BF16)
            dv_ref[cols, :] = dv.astype(BF16)
            dcc_ref[kj] = csum
            return 0

        lax.fori_loop(0, nq, k_step, 0)
        dq_ref[...] = dqa[...].astype(BF16)

    att = jax.ShapeDtypeStruct((S, ATT_W), BF16)
    return pl.pallas_call(
        body, name="attn_bwd", grid=(N_HEADS,),
        in_specs=[_colblock(S, QB), _colblock(S, KB), _colblock(S, VB), _colblock(S, 0),
                  _head_col(S), _head_col(S), _head_row(nq, tq)],
        out_specs=(_colblock(S, 0), _colblock(S, 0), _colblock(S, 0), pl.BlockSpec((S, LANES), lambda h: (0, 0)),
                   _head_row(nq, tq)),
        out_shape=(att, att, att, jax.ShapeDtypeStruct((S, LANES), F32),
                   jax.ShapeDtypeStruct((N_HEADS, nq, 1, tq), F32)),
        scratch_shapes=[pltpu.VMEM((S, HEAD_DIM), F32)],
        compiler_params=_params(("arbitrary",)),
    )(qkv, qkv, qkv, do, a, delta, cumr)


def _conv_taps(u, w_ref, row):
    return (w_ref[0:1, :] * _shift_down(u, 2, row) + w_ref[1:2, :] * _shift_down(u, 1, row)) + w_ref[2:3, :] * u


def _conv_fwd(rest, cw, y):
    S = rest.shape[0]

    def body(cb_ref, cc_ref, ch_ref, g_ref, w_ref, y_in, y_ref):
        row = lax.broadcasted_iota(jnp.int32, (S, LANES), 0)
        g = g_ref[...]
        y = _conv_taps(cc_ref[...] * ch_ref[...], w_ref, row)
        y_ref[...] = ((cb_ref[...] * y) * (g * _sigmoid(g))).astype(BF16)

    return pl.pallas_call(
        body, name="conv_fwd", grid=(CONV_W // LANES,),
        in_specs=[_colblock(S, CBB), _colblock(S, CCB), _colblock(S, CHB), _colblock(S, CGB),
                  pl.BlockSpec((CONV_TAPS, LANES), lambda j: (0, j)), ANY],
        out_specs=_colblock(S, ATT_W // LANES),
        out_shape=jax.ShapeDtypeStruct(y.shape, y.dtype),
        input_output_aliases={5: 0},
        compiler_params=_params(("parallel",)),
    )(rest, rest, rest, rest, cw, y)


def _conv_bwd(rest, dy, cw):
    S = rest.shape[0]

    def body(cb_ref, cc_ref, ch_ref, g_ref, dy_ref, w_ref, dcb_ref, dcc_ref, dch_ref, dg_ref, dw_ref):
        row = lax.broadcasted_iota(jnp.int32, (S, LANES), 0)
        g = g_ref[...]
        sg = _sigmoid(g)
        silu = g * sg
        cb, cc, ch, dy = cb_ref[...], cc_ref[...], ch_ref[...], dy_ref[...]
        u = cc * ch
        u1 = _shift_down(u, 1, row)
        u2 = _shift_down(u, 2, row)
        y = (w_ref[0:1, :] * u2 + w_ref[1:2, :] * u1) + w_ref[2:3, :] * u
        dcb_ref[...] = (dy * silu * y).astype(BF16)
        dg_ref[...] = (dy * (cb * y) * (sg * (1.0 + g * (1.0 - sg)))).astype(BF16)
        dyv = dy * silu * cb
        du = w_ref[2:3, :] * dyv + w_ref[1:2, :] * _shift_up(dyv, 1, row) + w_ref[0:1, :] * _shift_up(dyv, 2, row)
        dcc_ref[...] = (du * ch).astype(BF16)
        dch_ref[...] = (du * cc).astype(BF16)
        dw_ref[0:1, :] = jnp.sum(dyv * u2, axis=0, keepdims=True)
        dw_ref[1:2, :] = jnp.sum(dyv * u1, axis=0, keepdims=True)
        dw_ref[2:3, :] = jnp.sum(dyv * u, axis=0, keepdims=True)

    act = jax.ShapeDtypeStruct((S, CONV_W), BF16)
    return pl.pallas_call(
        body, name="conv_bwd", grid=(CONV_W // LANES,),
        in_specs=[_colblock(S, CBB), _colblock(S, CCB), _colblock(S, CHB), _colblock(S, CGB),
                  _colblock(S, ATT_W // LANES), pl.BlockSpec((CONV_TAPS, LANES), lambda j: (0, j))],
        out_specs=(_colblock(S, 0),) * 4 + (pl.BlockSpec((CONV_TAPS, LANES), lambda j: (0, j)),),
        out_shape=(act, act, act, act, jax.ShapeDtypeStruct((CONV_TAPS, CONV_W), F32)),
        compiler_params=_params(("parallel",)),
    )(rest, rest, rest, rest, dy, cw)


def _pool_z(u, grp, row):
    s2 = u + _shift_down(u, 1, row)
    s4 = s2 + _shift_down(s2, 2, row)
    s8 = s4 + _shift_down(s4, 4, row)
    s16 = s8 + _shift_down(s8, 8, row)
    sw = jnp.where(grp == 0, s2, jnp.where(grp == 1, s4, jnp.where(grp == 2, s8, s16)))
    return sw / _pool_count(grp, row) - u


def _pool_count(grp, row):
    return jnp.minimum(row + 1, 2 << grp).astype(F32)


def _pool_fwd(rest, pw, scale, y):
    S = rest.shape[0]

    def body(u_ref, g_ref, pw_ref, sc_ref, y_in, y_ref):
        grp = pl.program_id(0)
        row = lax.broadcasted_iota(jnp.int32, (S, LANES), 0)
        z = _pool_z(u_ref[...], grp, row)
        zp = jnp.dot(z.astype(BF16), pw_ref[...].astype(BF16), preferred_element_type=F32)
        g = g_ref[...]
        y_ref[...] = ((zp * sc_ref[...]) * (g * _sigmoid(g))).astype(BF16)

    return pl.pallas_call(
        body, name="pool_fwd", grid=(N_POOL,),
        in_specs=[_colblock(S, PUB), _colblock(S, PGB),
                  pl.BlockSpec((None, POOL_GROUP, POOL_GROUP), lambda j: (j, 0, 0)),
                  pl.BlockSpec((1, LANES), lambda j: (0, j)), ANY],
        out_specs=_colblock(S, (ATT_W + CONV_W) // LANES),
        out_shape=jax.ShapeDtypeStruct(y.shape, y.dtype),
        input_output_aliases={4: 0},
        compiler_params=_params(("parallel",)),
    )(rest, rest, pw, scale, y)


def _pool_bwd(rest, dy, pw, scale):
    S = rest.shape[0]

    def body(u_ref, g_ref, dy_ref, pw_ref, sc_ref, du_ref, dg_ref, dpw_ref, dsc_ref):
        grp = pl.program_id(0)
        row = lax.broadcasted_iota(jnp.int32, (S, LANES), 0)
        z = _pool_z(u_ref[...], grp, row).astype(BF16)
        pwb = pw_ref[...].astype(BF16)
        zp = jnp.dot(z, pwb, preferred_element_type=F32)
        g = g_ref[...]
        sg = _sigmoid(g)
        silu = g * sg
        dy = dy_ref[...]
        sc = sc_ref[...]
        dsc_ref[...] = jnp.sum(dy * silu * zp, axis=0, keepdims=True)
        dg_ref[...] = (dy * (zp * sc) * (sg * (1.0 + g * (1.0 - sg)))).astype(BF16)
        dzp = (dy * silu * sc).astype(BF16)
        dpw_ref[...] = lax.dot_general(z, dzp, (((0,), (0,)), ((), ())), preferred_element_type=F32)
        dz = lax.dot_general(dzp, pwb, (((1,), (1,)), ((), ())), preferred_element_type=F32)
        f1 = dz / _pool_count(grp, row)
        f2 = f1 + _shift_up(f1, 1, row)
        f4 = f2 + _shift_up(f2, 2, row)
        f8 = f4 + _shift_up(f4, 4, row)
        f16 = f8 + _shift_up(f8, 8, row)
        fw = jnp.where(grp == 0, f2, jnp.where(grp == 1, f4, jnp.where(grp == 2, f8, f16)))
        du_ref[...] = (fw - dz).astype(BF16)

    act = jax.ShapeDtypeStruct((S, POOL_W), BF16)
    return pl.pallas_call(
        body, name="pool_bwd", grid=(N_POOL,),
        in_specs=[_colblock(S, PUB), _colblock(S, PGB), _colblock(S, (ATT_W + CONV_W) // LANES),
                  pl.BlockSpec((None, POOL_GROUP, POOL_GROUP), lambda j: (j, 0, 0)),
                  pl.BlockSpec((1, LANES), lambda j: (0, j))],
        out_specs=(_colblock(S, 0), _colblock(S, 0),
                   pl.BlockSpec((None, POOL_GROUP, POOL_GROUP), lambda j: (j, 0, 0)),
                   pl.BlockSpec((1, LANES), lambda j: (0, j))),
        out_shape=(act, act, jax.ShapeDtypeStruct((N_POOL, POOL_GROUP, POOL_GROUP), F32),
                   jax.ShapeDtypeStruct((1, POOL_W), F32)),
        compiler_params=_params(("parallel",)),
    )(rest, rest, dy, pw, scale)


def _ln_fwd(z, g, b, *, ts):
    S, D = z.shape

    def body(z_ref, g_ref, b_ref, o_ref, ob_ref, obt_ref, xh_ref, rs_ref):
        zz = z_ref[...]
        mu = jnp.mean(zz, axis=1, keepdims=True)
        zc = zz - mu
        rstd = lax.rsqrt(jnp.mean(zc * zc, axis=1, keepdims=True) + LN_EPS)
        xh = zc * rstd
        out = xh * g_ref[...] + b_ref[...]
        o_ref[...] = out
        ob_ref[...] = out.astype(BF16)
        obt_ref[...] = out.T.astype(BF16)
        xh_ref[...] = xh
        rs_ref[...] = rstd

    rowblk = pl.BlockSpec((ts, D), lambda i: (i, 0))
    vec = pl.BlockSpec((1, D), lambda i: (0, 0))
    return pl.pallas_call(
        body, name="ln_fwd", grid=(S // ts,),
        in_specs=[rowblk, vec, vec],
        out_specs=(rowblk, rowblk, pl.BlockSpec((D, ts), lambda i: (0, i)), rowblk,
                   pl.BlockSpec((ts, 1), lambda i: (i, 0))),
        out_shape=(jax.ShapeDtypeStruct((S, D), F32), jax.ShapeDtypeStruct((S, D), BF16),
                   jax.ShapeDtypeStruct((D, S), BF16), jax.ShapeDtypeStruct((S, D), F32),
                   jax.ShapeDtypeStruct((S, 1), F32)),
        compiler_params=_params(("parallel",)),
    )(z, g, b)


def _prep(x, after, *, ts):
    S, D = x.shape

    def body(x_ref, after_ref, xb_ref, xbt_ref):
        xb_ref[...] = x_ref[...].astype(BF16)
        xbt_ref[...] = x_ref[...].T.astype(BF16)

    return pl.pallas_call(
        body, name="prep", grid=(S // ts,),
        in_specs=[pl.BlockSpec((ts, D), lambda i: (i, 0)), ANY],
        out_specs=(pl.BlockSpec((ts, D), lambda i: (i, 0)), pl.BlockSpec((D, ts), lambda i: (0, i))),
        out_shape=(jax.ShapeDtypeStruct((S, D), BF16), jax.ShapeDtypeStruct((D, S), BF16)),
        compiler_params=_params(("parallel",)),
    )(x, after)


def _ln_bwd(dout, xhat, rstd, g, *, ts):
    S, D = dout.shape

    def body(d_ref, xh_ref, rs_ref, g_ref, dz_ref, dzb_ref, dg_ref, db_ref):
        @pl.when(pl.program_id(0) == 0)
        def _():
            dg_ref[...] = jnp.zeros_like(dg_ref)
            db_ref[...] = jnp.zeros_like(db_ref)

        d = d_ref[...]
        xh = xh_ref[...]
        dxh = d * g_ref[...]
        m1 = jnp.mean(dxh, axis=1, keepdims=True)
        m2 = jnp.mean(dxh * xh, axis=1, keepdims=True)
        dz = rs_ref[...] * (dxh - m1 - xh * m2)
        dz_ref[...] = dz
        dzb_ref[...] = dz.astype(BF16)
        dg_ref[...] += jnp.sum(d * xh, axis=0, keepdims=True)
        db_ref[...] += jnp.sum(d, axis=0, keepdims=True)

    rowblk = pl.BlockSpec((ts, D), lambda i: (i, 0))
    vec = pl.BlockSpec((1, D), lambda i: (0, 0))
    return pl.pallas_call(
        body, name="ln_bwd", grid=(S // ts,),
        in_specs=[rowblk, rowblk, pl.BlockSpec((ts, 1), lambda i: (i, 0)), vec],
        out_specs=(rowblk, rowblk, vec, vec),
        out_shape=(jax.ShapeDtypeStruct((S, D), F32), jax.ShapeDtypeStruct((S, D), BF16),
                   jax.ShapeDtypeStruct((1, D), F32), jax.ShapeDtypeStruct((1, D), F32)),
        compiler_params=_params(("arbitrary",)),
    )(dout, xhat, rstd, g)


def _loss_head(y, tgt, *, ts):
    S, D = y.shape

    def body(y_ref, t_ref, l_ref, dy_ref):
        @pl.when(pl.program_id(0) == 0)
        def _():
            l_ref[...] = jnp.zeros_like(l_ref)

        e = y_ref[...] - t_ref[...]
        dy_ref[...] = e * (1.0 / D)
        rowloss = jnp.sum(e * e, axis=1, keepdims=True) * (0.5 / D)
        l_ref[...] += jnp.sum(rowloss, axis=0, keepdims=True)

    rowblk = pl.BlockSpec((ts, D), lambda i: (i, 0))
    return pl.pallas_call(
        body, name="loss_head", grid=(S // ts,),
        in_specs=[rowblk, rowblk],
        out_specs=(pl.BlockSpec((1, 1), lambda i: (0, 0)), rowblk),
        out_shape=(jax.ShapeDtypeStruct((1, 1), F32), jax.ShapeDtypeStruct((S, D), F32)),
        compiler_params=_params(("arbitrary",)),
    )(y, tgt)


def _adamw(lands, srcs, w, m, v, *, name, first=0, into=None):
    R, C = w.shape
    npart = len(lands)
    rp = lands[0].shape[1]
    assert R % rp == 0 and all(p.shape == (N_DEV, rp, C) for p in lands)
    tiled = rp % 8 == 0
    assert tiled or (first == 0 and npart * rp == R and into is None)
    tr = min(ADAMW_BLOCK_ELEMS // (LANES * pl.cdiv(C, LANES)), rp) if tiled else rp
    assert rp % tr == 0
    steps = rp // tr
    c1 = 1.0 - ADAM_B1 ** ADAM_STEP
    c2 = 1.0 - ADAM_B2 ** ADAM_STEP
    slabbed = [p.ndim == 3 for p in srcs]
    n_into = 0 if into is None else 4

    def body(*refs):
        l_refs, s_refs = refs[:npart], refs[npart:2 * npart]
        w_ref, m_ref, v_ref = refs[2 * npart:2 * npart + 3]
        g_ref, d_ref, nm_ref, nv_ref = refs[2 * npart + 3 + n_into:]
        x, y, c = _my_place()
        me = 4 * x + 2 * y + c

        def update(l, rows):
            g = None
            for j in range(N_DEV):
                own = s_refs[l][j] if slabbed[l] else s_refs[l][...]
                p = jnp.where(me == j, own, l_refs[l][j]).astype(F32)
                g = p if g is None else g + p
            nm = ADAM_B1 * m_ref[rows, :] + (1.0 - ADAM_B1) * g
            nv = ADAM_B2 * v_ref[rows, :] + (1.0 - ADAM_B2) * (g * g)
            g_ref[rows, :] = g
            nm_ref[rows, :] = nm
            nv_ref[rows, :] = nv
            d_ref[rows, :] = -ADAM_LR * ((nm / c1) / (jnp.sqrt(nv / c2) + ADAM_EPS) + ADAM_WD * w_ref[rows, :])

        for l in range(npart):
            if tiled:
                pl.when(pl.program_id(0) == l)(lambda l=l: update(l, slice(None)))
            else:
                update(l, slice(l * rp, (l + 1) * rp))

    def part_spec(l, p):
        if p.ndim == 3:
            return pl.BlockSpec((N_DEV, tr, C), lambda q, i: (0, jnp.where(q == l, i, 0), 0))
        return pl.BlockSpec((tr, C), lambda q, i: (jnp.where(q == l, i, 0), 0))

    if tiled:
        blk = pl.BlockSpec((tr, C), lambda q, i: ((first + q) * steps + i, 0))
    else:
        blk = pl.BlockSpec((R, C), lambda q, i: (0, 0))
    out = jax.ShapeDtypeStruct((R, C), F32)
    n_in = 2 * npart + 3
    return pl.pallas_call(
        body, name=name, grid=(npart, steps) if tiled else (1, 1),
        in_specs=[part_spec(l, p) for l, p in enumerate(lands)] + [part_spec(l, p) for l, p in enumerate(srcs)]
        + [blk, blk, blk] + [ANY] * n_into,
        out_specs=(blk, blk, blk, blk),
        out_shape=(out, out, out, out),
        input_output_aliases={n_in + i: i for i in range(n_into)},
        compiler_params=_params(("arbitrary", "arbitrary")),
    )(*lands, *srcs, w, m, v, *(into or ()))


def _my_place():
    return lax.axis_index("x"), lax.axis_index("y"), lax.axis_index("c")


HBM = pl.BlockSpec(memory_space=pltpu.HBM)
SEM = pl.BlockSpec(memory_space=pltpu.SEMAPHORE)
DATAFLOW = pltpu.SideEffectType.DATAFLOW_SIDE_EFFECTING
N_PEER = N_DEV - 1


def _xchg_copies(src, land, send_sem, recv_sem, gather):
    x, y, c = _my_place()
    me = 4 * x + 2 * y + c
    copies = []
    for k in range(N_PEER):
        px, py, pc = x ^ (((k + 1) >> 2) & 1), y ^ (((k + 1) >> 1) & 1), c ^ ((k + 1) & 1)
        peer = 4 * px + 2 * py + pc
        for a in range(len(src)):
            i = a * N_PEER + k
            out = src[a] if gather[a] else src[a].at[peer]
            sent, landed = [pltpu.make_async_remote_copy(
                src_ref=out, dst_ref=land[a].at[slab], send_sem=send_sem.at[i], recv_sem=recv_sem.at[i],
                device_id=(px, py, pc), device_id_type=MESH) for slab in (me, peer)]
            copies.append((sent, landed))
    return copies


def _xchg_start(srcs, *, gather, name):
    n = len(srcs)
    lands = [lax.empty(((N_DEV,) + v.shape) if g else v.shape, v.dtype) for v, g in zip(srcs, gather)]

    def body(*refs):
        src, land = refs[:n], refs[n:2 * n]
        send_sem, recv_sem = refs[2 * n], refs[2 * n + 1]
        token = refs[-1]
        for sent, _ in _xchg_copies(src, land, send_sem, recv_sem, gather):
            sent.start()
        token[...] = jnp.zeros_like(token)

    sems = pltpu.SemaphoreType.DMA((n * N_PEER,))
    outs = pl.pallas_call(
        body, name=name,
        out_shape=(sems, sems, *[pltpu.HBM(v.shape, v.dtype) for v in srcs + lands],
                   jax.ShapeDtypeStruct((8, LANES), F32)),
        in_specs=[HBM] * (2 * n),
        out_specs=(SEM, SEM, *[HBM] * (2 * n), pl.BlockSpec(memory_space=pltpu.VMEM)),
        input_output_aliases={i: 2 + i for i in range(2 * n)},
        compiler_params=pltpu.CompilerParams(has_side_effects=DATAFLOW),
    )(*[pltpu.with_memory_space_constraint(v, pltpu.HBM) for v in srcs + lands])
    return dict(send=outs[0], recv=outs[1], srcs=list(outs[2:2 + n]), lands=list(outs[2 + n:2 + 2 * n]),
                token=outs[-1], gather=gather)


def _xchg_wait(h, after, *, name):
    n = len(h["srcs"])
    gather = h["gather"]

    def body(*refs):
        src, land = refs[:n], refs[n:2 * n]
        send_sem, recv_sem = refs[2 * n], refs[2 * n + 1]
        for sent, landed in _xchg_copies(src, land, send_sem, recv_sem, gather):
            sent.wait_send()
            landed.wait_recv()

    thru = h["srcs"] + h["lands"]
    outs = pl.pallas_call(
        body, name=name,
        out_shape=[pltpu.HBM(v.shape, v.dtype) for v in thru],
        in_specs=[HBM] * (2 * n) + [SEM, SEM, ANY],
        out_specs=[HBM] * (2 * n),
        input_output_aliases={i: i for i in range(2 * n)},
        compiler_params=pltpu.CompilerParams(has_side_effects=DATAFLOW),
    )(*thru, h["send"], h["recv"], after)
    return list(outs[:n]), list(outs[n:])


def _with_own(land, own, me):
    slot = lax.broadcasted_iota(jnp.int32, land.shape, 0)
    return jnp.where(slot == me, own if own.ndim == land.ndim else own[None], land)


def _w_in_segments(j, cols):
    lo, hi = j * cols, (j + 1) * cols
    segs = []
    for a, b, where, shift in ((0, FG_AT, "main", 0), (FG_AT, FG_AT + N_HEADS, "fg", -FG_AT),
                               (FG_AT + N_HEADS, IN_W, "main", -N_HEADS)):
        s0, s1 = max(lo, a), min(hi, b)
        if s0 < s1:
            segs.append((s0 - lo, s1 - s0, where, s0 + shift))
    return segs


SLAB_W = 1024


def _window(off, width):
    base = min((off // LANES) * LANES, width - SLAB_W)
    return base, off - base


def _w_unpack(land, own, *, tr):
    _, D, cols = land.shape
    assert cols + LANES - 1 <= SLAB_W

    def body(land_ref, own_ref, main_ref, fg_ref, slab):
        x, y, c = _my_place()
        me = 4 * x + 2 * y + c
        lane = lax.broadcasted_iota(jnp.int32, (tr, SLAB_W), 1)
        main_ref[...] = jnp.zeros_like(main_ref)
        slab[...] = jnp.zeros_like(slab)
        for j in range(N_DEV):
            slab[:, :cols] = jnp.where(me == j, own_ref[...], land_ref[j])
            v = slab[...]
            segs = _w_in_segments(j, cols)
            for lo, n, where, dst in segs:
                part = v if len(segs) == 1 else jnp.where((lane >= lo) & (lane < lo + n), v, jnp.zeros_like(v))
                if where == "fg":
                    fg_ref[...] = pltpu.roll(part, (SLAB_W - lo) % SLAB_W, 1)[:, :LANES]
                else:
                    base, r = _window(dst - lo, MAIN_W)
                    assert 0 <= r and lo + n + r <= SLAB_W
                    main_ref[:, base:base + SLAB_W] += pltpu.roll(part, r, 1)

    return pl.pallas_call(
        body, name="w_unpack", grid=(D // tr,),
        in_specs=[pl.BlockSpec((N_DEV, tr, cols), lambda i: (0, i, 0)), pl.BlockSpec((tr, cols), lambda i: (i, 0))],
        out_specs=(pl.BlockSpec((tr, MAIN_W), lambda i: (i, 0)), pl.BlockSpec((tr, LANES), lambda i: (i, 0))),
        out_shape=(jax.ShapeDtypeStruct((D, MAIN_W), BF16), jax.ShapeDtypeStruct((D, LANES), BF16)),
        scratch_shapes=[pltpu.VMEM((tr, SLAB_W), BF16)],
        compiler_params=_params(("parallel",)),
    )(land, own)


def _g_pack(dw_main, dw_fg, cols, *, tr):
    D = dw_main.shape[0]

    def body(main_ref, fg_ref, out_ref, slab):
        lane = lax.broadcasted_iota(jnp.int32, (tr, SLAB_W), 1)
        slab[...] = jnp.zeros_like(slab)
        slab[:, :LANES] = fg_ref[...]
        fg = slab[...]
        for j in range(N_DEV):
            v = None
            for lo, n, where, src in _w_in_segments(j, cols):
                if where == "fg":
                    part = pltpu.roll(fg, lo, 1)
                else:
                    base, r = _window(src - lo, MAIN_W)
                    assert 0 <= r and lo + n + r <= SLAB_W
                    part = pltpu.roll(main_ref[:, base:base + SLAB_W], (SLAB_W - r) % SLAB_W, 1)
                v = part if v is None else jnp.where((lane >= lo) & (lane < lo + n), part, v)
            out_ref[j] = v[:, :cols]

    return pl.pallas_call(
        body, name="g_pack", grid=(D // tr,),
        in_specs=[pl.BlockSpec((tr, MAIN_W), lambda i: (i, 0)), pl.BlockSpec((tr, LANES), lambda i: (i, 0))],
        out_specs=pl.BlockSpec((N_DEV, tr, cols), lambda i: (0, i, 0)),
        out_shape=jax.ShapeDtypeStruct((N_DEV, D, cols), BF16),
        scratch_shapes=[pltpu.VMEM((tr, SLAB_W), BF16)],
        compiler_params=_params(("parallel",)),
    )(dw_main, dw_fg)


TQ = 512
TS = 256
W_TILE_ROWS = 128


def _layer_fwd(x, xb, xbt, wl):
    qkv = _mm(xb, wl["w_main"], out_dtype=BF16, tm=1024, tn=512, tk=D_MODEL, name="mm_qkv", n_cols=QKV_W)
    rest = _mm(xb, wl["w_main"], out_dtype=F32, tm=1024, tn=512, tk=D_MODEL, name="mm_rest", n_off=QKV_W, n_cols=REST_W)
    fgp = _mm(xb, wl["w_fg"], out_dtype=F32, tm=1024, tn=LANES, tk=D_MODEL, name="mm_fg")
    cum = _fg_fwd(fgp, wl["b_f"])
    S = x.shape[0]
    cumr = (cum[:, :N_HEADS].T * LOG2E).reshape(N_HEADS, S // TQ, 1, TQ)
    o, y, lse = _attn_fwd(qkv, rest, cum, cumr, tq=TQ)
    y = _conv_fwd(rest, wl["conv_w"], y)
    y = _pool_fwd(rest, wl["pool_w"], wl["pool_scale"], y)
    z = _mm(y, wl["w_out"], out_dtype=F32, tm=1024, tn=512, tk=D_MODEL, name="mm_out", c=x, c_scale=ALPHA)
    out, outb, outbt, xhat, rstd = _ln_fwd(z, wl["ln_g"], wl["ln_b"], ts=TS)
    saved = dict(xbt=xbt, qkv=qkv, rest=rest, fgp=fgp, cum=cum, cumr=cumr, o=o, lse=lse, y=y, xhat=xhat, rstd=rstd)
    return out, outb, outbt, saved


def _layer_bwd_weights(dout, sv, wl):
    S = dout.shape[0]
    dz, dzb, dln_g, dln_b = _ln_bwd(dout, sv["xhat"], sv["rstd"], wl["ln_g"], ts=TS)
    dy = _mm(dzb, wl["w_out"], out_dtype=F32, tm=1024, tn=512, tk=D_MODEL, name="mm_dy", bt=True)
    dw_out = _mm(sv["y"], dzb, out_dtype=BF16, tm=512, tn=D_MODEL, tk=512, name="mm_dw_out", at=True)
    rest = sv["rest"]
    do, dg_att, a, delta = _attn_bwd_pre(rest, sv["o"], dy, sv["cum"], sv["lse"])
    dq, dk, dv, dcr, dcc = _attn_bwd(sv["qkv"], do, a, delta, sv["cumr"], tq=TQ)
    dcum = dcr - jnp.pad(dcc.reshape(N_HEADS, S).T, ((0, 0), (0, LANES - N_HEADS)))
    dfg, db_f = _fg_bwd(dcum, sv["fgp"], wl["b_f"])
    dcb, dcc_, dch, dgc, dconv_w = _conv_bwd(rest, dy, wl["conv_w"])
    dpu, dgp, dpool_w, dpool_scale = _pool_bwd(rest, dy, wl["pool_w"], wl["pool_scale"])
    dproj = jnp.concatenate([dq, dk, dv, dg_att, dcb, dcc_, dch, dgc, dpu, dgp], axis=1)
    xt = sv["xbt"]
    dw_main = _mm(xt, dproj, out_dtype=BF16, tm=1024, tn=512, tk=2048, name="mm_dw_main")
    dw_fg = _mm(xt, dfg, out_dtype=BF16, tm=1024, tn=LANES, tk=2048, name="mm_dw_fg")
    grads = dict(w_main=dw_main, w_fg=dw_fg, b_f=db_f[:, :N_HEADS], conv_w=dconv_w, pool_w=dpool_w, pool_scale=dpool_scale,
                 w_out=dw_out, ln_g=dln_g, ln_b=dln_b)
    return grads, (dproj, dfg, dz)


def _layer_bwd_input(ctx, wl, after):
    dproj, dfg, dz = ctx
    t = _mm(dfg, wl["w_fg"], out_dtype=F32, tm=1024, tn=512, tk=LANES, name="mm_dx_fg", c=dz, c_scale=ALPHA, bt=True,
            after=after)
    return _mm(dproj, wl["w_main"], out_dtype=F32, tm=1024, tn=512, tk=MAIN_W // 2, name="mm_dx", c=t, c_scale=1.0, bt=True)


def kernel(x, w_in, b_f, conv_w, pool_w, pool_scale, w_out, ln_g, ln_b, loss_target, m_w_in, m_b_f, m_conv_w, m_pool_w, m_pool_scale, m_w_out, m_ln_g, m_ln_b, v_w_in, v_b_f, v_conv_w, v_pool_w, v_pool_scale, v_w_out, v_ln_g, v_ln_b):
    L, D, cols = w_in.shape
    rows_out = w_out.shape[1]
    ccols = conv_w.shape[2]
    mx, my, mc = _my_place()
    me = 4 * mx + 2 * my + mc

    w_in_b, w_out_b = w_in.astype(BF16), w_out.astype(BF16)
    gathers = [_xchg_start([w_in_b[l], w_out_b[l], conv_w[l]], gather=[True] * 3, name=f"gather_start_{l}")
               for l in range(L)]
    started = gathers[0]["token"]
    for h in gathers[1:]:
        started = started + h["token"]

    xl = x[0]
    xb, xbt = _prep(xl, started, ts=TS)
    weights, saved = [], []
    for l in range(L):
        (own_in, own_out, own_cw), (g_in, g_out, g_cw) = _xchg_wait(gathers[l], xb, name=f"gather_wait_{l}")
        w_main, w_fg = _w_unpack(g_in, own_in, tr=W_TILE_ROWS)
        w_o = _with_own(g_out, own_out, me).reshape(N_DEV * rows_out, D)
        cw = jnp.transpose(_with_own(g_cw, own_cw, me), (1, 0, 2)).reshape(CONV_TAPS, N_DEV * ccols)
        wl = dict(w_main=w_main, w_fg=w_fg, w_out=w_o, conv_w=cw,
                  b_f=jnp.pad(b_f[l][None, :], ((0, 0), (0, LANES - N_HEADS))), pool_w=pool_w[l],
                  pool_scale=pool_scale[l][None, :], ln_g=ln_g[l][None, :], ln_b=ln_b[l][None, :])
        xl, xb, xbt, sv = _layer_fwd(xl, xb, xbt, wl)
        weights.append(wl)
        saved.append(sv)

    loss, dx = _loss_head(xl, loss_target[0], ts=TS)
    loss = lax.psum(loss[0, 0], ("x", "y", "c"))

    sharded, small = ["w_in", "w_out", "conv_w"], ["b_f", "pool_w", "pool_scale", "ln_g", "ln_b"]
    names = sharded + small
    scatters = [None] * L
    for l in reversed(range(L)):
        g, ctx = _layer_bwd_weights(dx, saved[l], weights[l])
        slabs = [_g_pack(g["w_main"], g["w_fg"], cols, tr=W_TILE_ROWS), g["w_out"].reshape(N_DEV, rows_out, D),
                 jnp.transpose(g["conv_w"].reshape(CONV_TAPS, N_DEV, ccols), (1, 0, 2))]
        scatters[l] = _xchg_start(slabs + [g[k] for k in small], gather=[False] * 3 + [True] * 5,
                                  name=f"scatter_start_{l}")
        dx = _layer_bwd_input(ctx, weights[l], scatters[l]["token"])

    given = dict(w_in=(w_in, m_w_in, v_w_in), b_f=(b_f, m_b_f, v_b_f), conv_w=(conv_w, m_conv_w, v_conv_w),
                 pool_w=(pool_w, m_pool_w, v_pool_w), pool_scale=(pool_scale, m_pool_scale, v_pool_scale),
                 w_out=(w_out, m_w_out, v_w_out), ln_g=(ln_g, m_ln_g, v_ln_g), ln_b=(ln_b, m_ln_b, v_ln_b))

    def update(k, layers, into=None):
        a = names.index(k)
        w, m, v = given[k]
        C = w.shape[-1]
        w2, m2, v2 = (t.reshape(-1, C) for t in (w, m, v))
        rp = w2.shape[0] // L
        lands = [landed[l][a].reshape(N_DEV, rp, C) for l in layers]
        srcs = [sent[l][a].reshape((N_DEV, rp, C) if k in sharded else (rp, C)) for l in layers]
        return _adamw(lands, srcs, w2, m2, v2, name=f"adamw_{k}_{layers[0]}", first=layers[0], into=into)

    sent, landed = [None] * L, [None] * L
    for l in reversed(range(1, L)):
        sent[l], landed[l] = _xchg_wait(scatters[l], dx, name=f"scatter_wait_{l}")
    big = ["w_in", "w_out", "pool_w"]
    upper = {k: update(k, list(range(1, L))) for k in big} if L > 1 else {}
    sent[0], landed[0] = _xchg_wait(scatters[0], upper["w_in"][0] if upper else dx, name="scatter_wait_0")
    res = {}
    for k in names:
        outs = update(k, [0], into=upper[k]) if k in upper else update(k, list(range(L)))
        res[k] = [t.reshape(given[k][0].shape) for t in outs]

    order = ["w_in", "b_f", "conv_w", "pool_w", "pool_scale", "w_out", "ln_g", "ln_b"]
    return (loss, dx[None], *[res[k][0] for k in order], *[res[k][1] for k in order],
            *[res[k][2] for k in order], *[res[k][3] for k in order])
```

```python
import jax
import jax.numpy as jnp
from jax import lax
from jax.experimental import pallas as pl
from jax.experimental.pallas import tpu as pltpu

F32 = jnp.float32
BF16 = jnp.bfloat16

N_DEV = 8
D_MODEL = 2048
N_HEADS = 8
HEAD_DIM = 128
ATT_W = N_HEADS * HEAD_DIM
CONV_W = 512
CONV_TAPS = 3
POOL_W = 512
POOL_GROUP = 128
N_POOL = POOL_W // POOL_GROUP
FG_AT = 4 * ATT_W
IN_W = FG_AT + N_HEADS + 4 * CONV_W + 2 * POOL_W
MAIN_W = IN_W - N_HEADS
LANES = 128
ALL_W = MAIN_W + LANES
LN_EPS = 1e-5
DEPTH = 4
ALPHA = (2 * DEPTH) ** 0.25
SCALE = HEAD_DIM ** -0.5
LOG2E = 1.4426950408889634
LN2 = 0.6931471805599453
NEG = -1e30

ADAM_LR, ADAM_B1, ADAM_B2, ADAM_EPS, ADAM_WD, ADAM_STEP = 0.001, 0.9, 0.999, 1e-08, 0.01, 10

QKV_W = 3 * ATT_W
REST_W = MAIN_W - QKV_W
QB, KB, VB = 0, 8, 16
GB = 0
CBB, CCB, CHB, CGB = 8, 12, 16, 20
PUB, PGB = 24, 28
VMEM_LIMIT = 48 * 1024 * 1024
ADAMW_BLOCK_ELEMS = 64 * 1024

MESH = pl.DeviceIdType.MESH
ANY = pl.BlockSpec(memory_space=pl.ANY)


def _params(semantics):
    return pltpu.CompilerParams(dimension_semantics=semantics, vmem_limit_bytes=VMEM_LIMIT)


def _sigmoid(x):
    return 1.0 / (1.0 + jnp.exp(-x))


def _mm(a, b, *, out_dtype, tm, tn, tk, name, c=None, c_scale=1.0, n_off=0, n_cols=None, bt=False, at=False,
        after=None):
    M, K = a.shape[::-1] if at else a.shape
    N = b.shape[0 if bt else 1] if n_cols is None else n_cols
    assert not (bt and n_off)
    dims = (((0 if at else 1,), (1 if bt else 0,)), ((), ()))
    tm, tn, tk = min(tm, M), min(tn, N), min(tk, K)
    assert M % tm == 0 and N % tn == 0 and K % tk == 0 and n_off % tn == 0, (a.shape, b.shape, tm, tn, tk)
    nk, joff = K // tk, n_off // tn
    has_c, has_after = c is not None, after is not None

    def body(*refs):
        a_ref, b_ref = refs[0], refs[1]
        c_ref = refs[2] if has_c else None
        o_ref = refs[2 + has_c + has_after]

        def finish(r):
            if has_c:
                r = r + c_scale * c_ref[...]
            o_ref[...] = r.astype(out_dtype)

        if nk == 1:
            finish(lax.dot_general(a_ref[...], b_ref[...], dims, preferred_element_type=F32))
        else:
            acc_ref = refs[3 + has_c + has_after]
            k = pl.program_id(2)

            @pl.when(k == 0)
            def _():
                acc_ref[...] = jnp.zeros_like(acc_ref)

            acc_ref[...] += lax.dot_general(a_ref[...], b_ref[...], dims, preferred_element_type=F32)

            @pl.when(k == nk - 1)
            def _():
                finish(acc_ref[...])

    in_specs = [pl.BlockSpec((tk, tm), lambda i, j, k: (k, i)) if at else pl.BlockSpec((tm, tk), lambda i, j, k: (i, k)),
                pl.BlockSpec((tn, tk), lambda i, j, k: (j, k)) if bt
                else pl.BlockSpec((tk, tn), lambda i, j, k: (k, j + joff))]
    args = [a, b]
    if has_c:
        in_specs.append(pl.BlockSpec((tm, tn), lambda i, j, k: (i, j)))
        args.append(c)
    if has_after:
        in_specs.append(ANY)
        args.append(after)
    return pl.pallas_call(
        body, name=name, grid=(M // tm, N // tn, nk),
        in_specs=in_specs,
        out_specs=pl.BlockSpec((tm, tn), lambda i, j, k: (i, j)),
        out_shape=jax.ShapeDtypeStruct((M, N), out_dtype),
        scratch_shapes=[pltpu.VMEM((tm, tn), F32)] if nk > 1 else [],
        compiler_params=_params(("parallel", "parallel", "arbitrary")),
    )(*args)


def _shift_down(x, k, row):
    return jnp.where(row >= k, pltpu.roll(x, k, 0), 0.0)


def _shift_up(x, k, row):
    s = x.shape[0]
    return jnp.where(row < s - k, pltpu.roll(x, s - k, 0), 0.0)


def _fg_fwd(fgp, bf):
    S = fgp.shape[0]

    def body(f_ref, b_ref, cum_ref):
        a = f_ref[...] + b_ref[...]
        x = jnp.minimum(a, 0.0) - jnp.log(1.0 + jnp.exp(-jnp.abs(a)))
        row = lax.broadcasted_iota(jnp.int32, x.shape, 0)
        k = 1
        while k < S:
            x = x + _shift_down(x, k, row)
            k *= 2
        cum_ref[...] = x

    return pl.pallas_call(body, name="fg_fwd", out_shape=jax.ShapeDtypeStruct((S, LANES), F32),
                          compiler_params=_params(None))(fgp, bf)


def _fg_bwd(dcum, fgp, bf):
    S = fgp.shape[0]

    def body(d_ref, f_ref, b_ref, dfg_ref, dbf_ref):
        x = d_ref[...]
        row = lax.broadcasted_iota(jnp.int32, x.shape, 0)
        k = 1
        while k < S:
            x = x + _shift_up(x, k, row)
            k *= 2
        dfg = x * _sigmoid(-(f_ref[...] + b_ref[...]))
        dfg_ref[...] = dfg.astype(BF16)
        dbf_ref[...] = jnp.sum(dfg, axis=0, keepdims=True)

    return pl.pallas_call(
        body, name="fg_bwd",
        out_shape=(jax.ShapeDtypeStruct((S, LANES), BF16), jax.ShapeDtypeStruct((1, LANES), F32)),
        compiler_params=_params(None),
    )(dcum, fgp, bf)


def _colblock(S, off):
    return pl.BlockSpec((S, LANES), lambda h: (0, h + off))


def _head_col(S):
    return pl.BlockSpec((None, S, 1), lambda h: (h, 0, 0))


def _head_row(nq, tq):
    return pl.BlockSpec((None, nq, 1, tq), lambda h: (h, 0, 0, 0))


def _lane_of(ref, h):
    lane = lax.broadcasted_iota(jnp.int32, ref.shape, 1)
    return jnp.sum(jnp.where(lane == h, ref[...], 0.0), axis=1, keepdims=True)


def _attn_fwd(qkv, rest, cum, cumr, *, tq):
    S = qkv.shape[0]
    nq = S // tq

    def body(q_ref, k_ref, v_ref, g_ref, cum_ref, cumr_ref, o_ref, y_ref, lse_ref, cc):
        cc[...] = _lane_of(cum_ref, pl.program_id(0)) * LOG2E
        tri = (lax.broadcasted_iota(jnp.int32, (tq, tq), 1) <= lax.broadcasted_iota(jnp.int32, (tq, tq), 0))

        def q_step(qi, _):
            rows = pl.ds(pl.multiple_of(qi * tq, tq), tq)
            q = q_ref[rows, :]
            ci = cc[rows, :]

            def tile(kj, carry, masked):
                m, l, acc = carry
                cols = pl.ds(pl.multiple_of(kj * tq, tq), tq)
                s = lax.dot_general(q, k_ref[cols, :], (((1,), (1,)), ((), ())), preferred_element_type=F32)
                s = s * (SCALE * LOG2E) + ci - cumr_ref[kj]
                if masked:
                    s = jnp.where(tri, s, NEG)
                m_new = jnp.maximum(m, jnp.max(s, axis=1, keepdims=True))
                a = jnp.exp2(m - m_new)
                p = jnp.exp2(s - m_new)
                l = a * l + jnp.sum(p, axis=1, keepdims=True)
                acc = a * acc + jnp.dot(p.astype(BF16), v_ref[cols, :], preferred_element_type=F32)
                return m_new, l, acc

            init = (jnp.full((tq, 1), NEG, F32), jnp.zeros((tq, 1), F32), jnp.zeros((tq, HEAD_DIM), F32))
            carry = lax.fori_loop(0, qi, lambda kj, c: tile(kj, c, False), init)
            m, l, acc = tile(qi, carry, True)
            o = acc / l
            g = g_ref[rows, :]
            o_ref[rows, :] = o
            y_ref[rows, :] = (o * (g * _sigmoid(g))).astype(BF16)
            lse_ref[rows, :] = m * LN2 + jnp.log(l)
            return 0

        lax.fori_loop(0, nq, q_step, 0)

    return pl.pallas_call(
        body, name="attn_fwd", grid=(N_HEADS,),
        in_specs=[_colblock(S, QB), _colblock(S, KB), _colblock(S, VB), _colblock(S, GB),
                  pl.BlockSpec((S, LANES), lambda h: (0, 0)), _head_row(nq, tq)],
        out_specs=(_colblock(S, 0), _colblock(S, 0), _head_col(S)),
        out_shape=(jax.ShapeDtypeStruct((S, ATT_W), F32), jax.ShapeDtypeStruct((S, D_MODEL), BF16),
                   jax.ShapeDtypeStruct((N_HEADS, S, 1), F32)),
        scratch_shapes=[pltpu.VMEM((S, 1), F32)],
        compiler_params=_params(("arbitrary",)),
    )(qkv, qkv, qkv, rest, cum, cumr)


def _attn_bwd_pre(rest, o, dy, cum, lse):
    S = o.shape[0]

    def body(g_ref, o_ref, dy_ref, cum_ref, lse_ref, do_ref, dg_ref, a_ref, dl_ref):
        g = g_ref[...]
        sg = _sigmoid(g)
        ov = o_ref[...]
        dy = dy_ref[...]
        do = dy * (g * sg)
        do_ref[...] = do.astype(BF16)
        dg_ref[...] = (dy * ov * (sg * (1.0 + g * (1.0 - sg)))).astype(BF16)
        dl_ref[...] = jnp.sum(do * ov, axis=1, keepdims=True)
        a_ref[...] = (_lane_of(cum_ref, pl.program_id(0)) - lse_ref[...]) * LOG2E

    att = jax.ShapeDtypeStruct((S, ATT_W), BF16)
    col = jax.ShapeDtypeStruct((N_HEADS, S, 1), F32)
    return pl.pallas_call(
        body, name="attn_bwd_pre", grid=(N_HEADS,),
        in_specs=[_colblock(S, GB), _colblock(S, 0), _colblock(S, 0),
                  pl.BlockSpec((S, LANES), lambda h: (0, 0)), _head_col(S)],
        out_specs=(_colblock(S, 0), _colblock(S, 0), _head_col(S), _head_col(S)),
        out_shape=(att, att, col, col),
        compiler_params=_params(("parallel",)),
    )(rest, o, dy, cum, lse)


def _attn_bwd(qkv, do, a, delta, cumr, *, tq):
    S = qkv.shape[0]
    nq = S // tq
    tdot = (((0,), (0,)), ((), ()))

    def body(q_ref, k_ref, v_ref, do_ref, a_ref, dl_ref, cumr_ref, dq_ref, dk_ref, dv_ref, dcr_ref, dcc_ref, dqa):
        dqa[...] = jnp.zeros_like(dqa)

        @pl.when(pl.program_id(0) == 0)
        def _():
            dcr_ref[...] = jnp.zeros_like(dcr_ref)

        tri = (lax.broadcasted_iota(jnp.int32, (tq, tq), 1) <= lax.broadcasted_iota(jnp.int32, (tq, tq), 0))
        mine = lax.broadcasted_iota(jnp.int32, (tq, LANES), 1) == pl.program_id(0)

        def k_step(kj, _):
            cols = pl.ds(pl.multiple_of(kj * tq, tq), tq)
            k = k_ref[cols, :]
            v = v_ref[cols, :]
            cj = cumr_ref[kj]

            def tile(qi, carry, masked):
                dk, dv, csum = carry
                rows = pl.ds(pl.multiple_of(qi * tq, tq), tq)
                q = q_ref[rows, :]
                dot = do_ref[rows, :]
                s = lax.dot_general(q, k, (((1,), (1,)), ((), ())), preferred_element_type=F32)
                p = jnp.exp2(s * (SCALE * LOG2E) + a_ref[rows, :] - cj)
                if masked:
                    p = jnp.where(tri, p, 0.0)
                dv = dv + lax.dot_general(p.astype(BF16), dot, tdot, preferred_element_type=F32)
                dp = lax.dot_general(dot, v, (((1,), (1,)), ((), ())), preferred_element_type=F32)
                ds = p * (dp - dl_ref[rows, :])
                dcr_ref[rows, :] += jnp.where(mine, jnp.sum(ds, axis=1, keepdims=True), 0.0)
                csum = csum + jnp.sum(ds, axis=0, keepdims=True)
                dsb = (ds * SCALE).astype(BF16)
                dqa[rows, :] += jnp.dot(dsb, k, preferred_element_type=F32)
                dk = dk + lax.dot_general(dsb, q, tdot, preferred_element_type=F32)
                return dk, dv, csum

            init = (jnp.zeros((tq, HEAD_DIM), F32), jnp.zeros((tq, HEAD_DIM), F32), jnp.zeros((1, tq), F32))
            carry = tile(kj, init, True)
            dk, dv, csum = lax.fori_loop(kj + 1, nq, lambda qi, c: tile(qi, c, False), carry)
            dk_ref[cols, :] = dk.astype(BF16)
            dv_ref[cols, :] = dv.astype(BF16)
            dcc_ref[kj] = csum
            return 0

        lax.fori_loop(0, nq, k_step, 0)
        dq_ref[...] = dqa[...].astype(BF16)

    att = jax.ShapeDtypeStruct((S, ATT_W), BF16)
    return pl.pallas_call(
        body, name="attn_bwd", grid=(N_HEADS,),
        in_specs=[_colblock(S, QB), _colblock(S, KB), _colblock(S, VB), _colblock(S, 0),
                  _head_col(S), _head_col(S), _head_row(nq, tq)],
        out_specs=(_colblock(S, 0), _colblock(S, 0), _colblock(S, 0), pl.BlockSpec((S, LANES), lambda h: (0, 0)),
                   _head_row(nq, tq)),
        out_shape=(att, att, att, jax.ShapeDtypeStruct((S, LANES), F32),
                   jax.ShapeDtypeStruct((N_HEADS, nq, 1, tq), F32)),
        scratch_shapes=[pltpu.VMEM((S, HEAD_DIM), F32)],
        compiler_params=_params(("arbitrary",)),
    )(qkv, qkv, qkv, do, a, delta, cumr)


def _conv_taps(u, w_ref, row):
    return (w_ref[0:1, :] * _shift_down(u, 2, row) + w_ref[1:2, :] * _shift_down(u, 1, row)) + w_ref[2:3, :] * u


def _conv_fwd(rest, cw, y):
    S = rest.shape[0]

    def body(cb_ref, cc_ref, ch_ref, g_ref, w_ref, y_in, y_ref):
        row = lax.broadcasted_iota(jnp.int32, (S, LANES), 0)
        g = g_ref[...]
        y = _conv_taps(cc_ref[...] * ch_ref[...], w_ref, row)
        y_ref[...] = ((cb_ref[...] * y) * (g * _sigmoid(g))).astype(BF16)

    return pl.pallas_call(
        body, name="conv_fwd", grid=(CONV_W // LANES,),
        in_specs=[_colblock(S, CBB), _colblock(S, CCB), _colblock(S, CHB), _colblock(S, CGB),
                  pl.BlockSpec((CONV_TAPS, LANES), lambda j: (0, j)), ANY],
        out_specs=_colblock(S, ATT_W // LANES),
        out_shape=jax.ShapeDtypeStruct(y.shape, y.dtype),
        input_output_aliases={5: 0},
        compiler_params=_params(("parallel",)),
    )(rest, rest, rest, rest, cw, y)


def _conv_bwd(rest, dy, cw):
    S = rest.shape[0]

    def body(cb_ref, cc_ref, ch_ref, g_ref, dy_ref, w_ref, dcb_ref, dcc_ref, dch_ref, dg_ref, dw_ref):
        row = lax.broadcasted_iota(jnp.int32, (S, LANES), 0)
        g = g_ref[...]
        sg = _sigmoid(g)
        silu = g * sg
        cb, cc, ch, dy = cb_ref[...], cc_ref[...], ch_ref[...], dy_ref[...]
        u = cc * ch
        u1 = _shift_down(u, 1, row)
        u2 = _shift_down(u, 2, row)
        y = (w_ref[0:1, :] * u2 + w_ref[1:2, :] * u1) + w_ref[2:3, :] * u
        dcb_ref[...] = (dy * silu * y).astype(BF16)
        dg_ref[...] = (dy * (cb * y) * (sg * (1.0 + g * (1.0 - sg)))).astype(BF16)
        dyv = dy * silu * cb
        du = w_ref[2:3, :] * dyv + w_ref[1:2, :] * _shift_up(dyv, 1, row) + w_ref[0:1, :] * _shift_up(dyv, 2, row)
        dcc_ref[...] = (du * ch).astype(BF16)
        dch_ref[...] = (du * cc).astype(BF16)
        dw_ref[0:1, :] = jnp.sum(dyv * u2, axis=0, keepdims=True)
        dw_ref[1:2, :] = jnp.sum(dyv * u1, axis=0, keepdims=True)
        dw_ref[2:3, :] = jnp.sum(dyv * u, axis=0, keepdims=True)

    act = jax.ShapeDtypeStruct((S, CONV_W), BF16)
    return pl.pallas_call(
        body, name="conv_bwd", grid=(CONV_W // LANES,),
        in_specs=[_colblock(S, CBB), _colblock(S, CCB), _colblock(S, CHB), _colblock(S, CGB),
                  _colblock(S, ATT_W // LANES), pl.BlockSpec((CONV_TAPS, LANES), lambda j: (0, j))],
        out_specs=(_colblock(S, 0),) * 4 + (pl.BlockSpec((CONV_TAPS, LANES), lambda j: (0, j)),),
        out_shape=(act, act, act, act, jax.ShapeDtypeStruct((CONV_TAPS, CONV_W), F32)),
        compiler_params=_params(("parallel",)),
    )(rest, rest, rest, rest, dy, cw)


def _pool_z(u, grp, row):
    s2 = u + _shift_down(u, 1, row)
    s4 = s2 + _shift_down(s2, 2, row)
    s8 = s4 + _shift_down(s4, 4, row)
    s16 = s8 + _shift_down(s8, 8, row)
    sw = jnp.where(grp == 0, s2, jnp.where(grp == 1, s4, jnp.where(grp == 2, s8, s16)))
    return sw / _pool_count(grp, row) - u


def _pool_count(grp, row):
    return jnp.minimum(row + 1, 2 << grp).astype(F32)


def _pool_fwd(rest, pw, scale, y):
    S = rest.shape[0]

    def body(u_ref, g_ref, pw_ref, sc_ref, y_in, y_ref):
        grp = pl.program_id(0)
        row = lax.broadcasted_iota(jnp.int32, (S, LANES), 0)
        z = _pool_z(u_ref[...], grp, row)
        zp = jnp.dot(z.astype(BF16), pw_ref[...].astype(BF16), preferred_element_type=F32)
        g = g_ref[...]
        y_ref[...] = ((zp * sc_ref[...]) * (g * _sigmoid(g))).astype(BF16)

    return pl.pallas_call(
        body, name="pool_fwd", grid=(N_POOL,),
        in_specs=[_colblock(S, PUB), _colblock(S, PGB),
                  pl.BlockSpec((None, POOL_GROUP, POOL_GROUP), lambda j: (j, 0, 0)),
                  pl.BlockSpec((1, LANES), lambda j: (0, j)), ANY],
        out_specs=_colblock(S, (ATT_W + CONV_W) // LANES),
        out_shape=jax.ShapeDtypeStruct(y.shape, y.dtype),
        input_output_aliases={4: 0},
        compiler_params=_params(("parallel",)),
    )(rest, rest, pw, scale, y)


def _pool_bwd(rest, dy, pw, scale):
    S = rest.shape[0]

    def body(u_ref, g_ref, dy_ref, pw_ref, sc_ref, du_ref, dg_ref, dpw_ref, dsc_ref):
        grp = pl.program_id(0)
        row = lax.broadcasted_iota(jnp.int32, (S, LANES), 0)
        z = _pool_z(u_ref[...], grp, row).astype(BF16)
        pwb = pw_ref[...].astype(BF16)
        zp = jnp.dot(z, pwb, preferred_element_type=F32)
        g = g_ref[...]
        sg = _sigmoid(g)
        silu = g * sg
        dy = dy_ref[...]
        sc = sc_ref[...]
        dsc_ref[...] = jnp.sum(dy * silu * zp, axis=0, keepdims=True)
        dg_ref[...] = (dy * (zp * sc) * (sg * (1.0 + g * (1.0 - sg)))).astype(BF16)
        dzp = (dy * silu * sc).astype(BF16)
        dpw_ref[...] = lax.dot_general(z, dzp, (((0,), (0,)), ((), ())), preferred_element_type=F32)
        dz = lax.dot_general(dzp, pwb, (((1,), (1,)), ((), ())), preferred_element_type=F32)
        f1 = dz / _pool_count(grp, row)
        f2 = f1 + _shift_up(f1, 1, row)
        f4 = f2 + _shift_up(f2, 2, row)
        f8 = f4 + _shift_up(f4, 4, row)
        f16 = f8 + _shift_up(f8, 8, row)
        fw = jnp.where(grp == 0, f2, jnp.where(grp == 1, f4, jnp.where(grp == 2, f8, f16)))
        du_ref[...] = (fw - dz).astype(BF16)

    act = jax.ShapeDtypeStruct((S, POOL_W), BF16)
    return pl.pallas_call(
        body, name="pool_bwd", grid=(N_POOL,),
        in_specs=[_colblock(S, PUB), _colblock(S, PGB), _colblock(S, (ATT_W + CONV_W) // LANES),
                  pl.BlockSpec((None, POOL_GROUP, POOL_GROUP), lambda j: (j, 0, 0)),
                  pl.BlockSpec((1, LANES), lambda j: (0, j))],
        out_specs=(_colblock(S, 0), _colblock(S, 0),
                   pl.BlockSpec((None, POOL_GROUP, POOL_GROUP), lambda j: (j, 0, 0)),
                   pl.BlockSpec((1, LANES), lambda j: (0, j))),
        out_shape=(act, act, jax.ShapeDtypeStruct((N_POOL, POOL_GROUP, POOL_GROUP), F32),
                   jax.ShapeDtypeStruct((1, POOL_W), F32)),
        compiler_params=_params(("parallel",)),
    )(rest, rest, dy, pw, scale)


def _ln_fwd(z, g, b, *, ts):
    S, D = z.shape

    def body(z_ref, g_ref, b_ref, o_ref, ob_ref, obt_ref, xh_ref, rs_ref):
        zz = z_ref[...]
        mu = jnp.mean(zz, axis=1, keepdims=True)
        zc = zz - mu
        rstd = lax.rsqrt(jnp.mean(zc * zc, axis=1, keepdims=True) + LN_EPS)
        xh = zc * rstd
        out = xh * g_ref[...] + b_ref[...]
        o_ref[...] = out
        ob_ref[...] = out.astype(BF16)
        obt_ref[...] = out.T.astype(BF16)
        xh_ref[...] = xh
        rs_ref[...] = rstd

    rowblk = pl.BlockSpec((ts, D), lambda i: (i, 0))
    vec = pl.BlockSpec((1, D), lambda i: (0, 0))
    return pl.pallas_call(
        body, name="ln_fwd", grid=(S // ts,),
        in_specs=[rowblk, vec, vec],
        out_specs=(rowblk, rowblk, pl.BlockSpec((D, ts), lambda i: (0, i)), rowblk,
                   pl.BlockSpec((ts, 1), lambda i: (i, 0))),
        out_shape=(jax.ShapeDtypeStruct((S, D), F32), jax.ShapeDtypeStruct((S, D), BF16),
                   jax.ShapeDtypeStruct((D, S), BF16), jax.ShapeDtypeStruct((S, D), F32),
                   jax.ShapeDtypeStruct((S, 1), F32)),
        compiler_params=_params(("parallel",)),
    )(z, g, b)


def _prep(x, after, *, ts):
    S, D = x.shape

    def body(x_ref, after_ref, xb_ref, xbt_ref):
        xb_ref[...] = x_ref[...].astype(BF16)
        xbt_ref[...] = x_ref[...].T.astype(BF16)

    return pl.pallas_call(
        body, name="prep", grid=(S // ts,),
        in_specs=[pl.BlockSpec((ts, D), lambda i: (i, 0)), ANY],
        out_specs=(pl.BlockSpec((ts, D), lambda i: (i, 0)), pl.BlockSpec((D, ts), lambda i: (0, i))),
        out_shape=(jax.ShapeDtypeStruct((S, D), BF16), jax.ShapeDtypeStruct((D, S), BF16)),
        compiler_params=_params(("parallel",)),
    )(x, after)


def _ln_bwd(dout, xhat, rstd, g, *, ts):
    S, D = dout.shape

    def body(d_ref, xh_ref, rs_ref, g_ref, dz_ref, dzb_ref, dg_ref, db_ref):
        @pl.when(pl.program_id(0) == 0)
        def _():
            dg_ref[...] = jnp.zeros_like(dg_ref)
            db_ref[...] = jnp.zeros_like(db_ref)

        d = d_ref[...]
        xh = xh_ref[...]
        dxh = d * g_ref[...]
        m1 = jnp.mean(dxh, axis=1, keepdims=True)
        m2 = jnp.mean(dxh * xh, axis=1, keepdims=True)
        dz = rs_ref[...] * (dxh - m1 - xh * m2)
        dz_ref[...] = dz
        dzb_ref[...] = dz.astype(BF16)
        dg_ref[...] += jnp.sum(d * xh, axis=0, keepdims=True)
        db_ref[...] += jnp.sum(d, axis=0, keepdims=True)

    rowblk = pl.BlockSpec((ts, D), lambda i: (i, 0))
    vec = pl.BlockSpec((1, D), lambda i: (0, 0))
    return pl.pallas_call(
        body, name="ln_bwd", grid=(S // ts,),
        in_specs=[rowblk, rowblk, pl.BlockSpec((ts, 1), lambda i: (i, 0)), vec],
        out_specs=(rowblk, rowblk, vec, vec),
        out_shape=(jax.ShapeDtypeStruct((S, D), F32), jax.ShapeDtypeStruct((S, D), BF16),
                   jax.ShapeDtypeStruct((1, D), F32), jax.ShapeDtypeStruct((1, D), F32)),
        compiler_params=_params(("arbitrary",)),
    )(dout, xhat, rstd, g)


def _loss_head(y, tgt, *, ts):
    S, D = y.shape

    def body(y_ref, t_ref, l_ref, dy_ref):
        @pl.when(pl.program_id(0) == 0)
        def _():
            l_ref[...] = jnp.zeros_like(l_ref)

        e = y_ref[...] - t_ref[...]
        dy_ref[...] = e * (1.0 / D)
        rowloss = jnp.sum(e * e, axis=1, keepdims=True) * (0.5 / D)
        l_ref[...] += jnp.sum(rowloss, axis=0, keepdims=True)

    rowblk = pl.BlockSpec((ts, D), lambda i: (i, 0))
    return pl.pallas_call(
        body, name="loss_head", grid=(S // ts,),
        in_specs=[rowblk, rowblk],
        out_specs=(pl.BlockSpec((1, 1), lambda i: (0, 0)), rowblk),
        out_shape=(jax.ShapeDtypeStruct((1, 1), F32), jax.ShapeDtypeStruct((S, D), F32)),
        compiler_params=_params(("arbitrary",)),
    )(y, tgt)


def _adamw(lands, srcs, w, m, v, *, name, first=0, into=None):
    R, C = w.shape
    npart = len(lands)
    rp = lands[0].shape[1]
    assert R % rp == 0 and all(p.shape == (N_DEV, rp, C) for p in lands)
    tiled = rp % 8 == 0
    assert tiled or (first == 0 and npart * rp == R and into is None)
    tr = min(ADAMW_BLOCK_ELEMS // (LANES * pl.cdiv(C, LANES)), rp) if tiled else rp
    assert rp % tr == 0
    steps = rp // tr
    c1 = 1.0 - ADAM_B1 ** ADAM_STEP
    c2 = 1.0 - ADAM_B2 ** ADAM_STEP
    slabbed = [p.ndim == 3 for p in srcs]
    n_into = 0 if into is None else 4

    def body(*refs):
        l_refs, s_refs = refs[:npart], refs[npart:2 * npart]
        w_ref, m_ref, v_ref = refs[2 * npart:2 * npart + 3]
        g_ref, d_ref, nm_ref, nv_ref = refs[2 * npart + 3 + n_into:]
        x, y, c = _my_place()
        me = 4 * x + 2 * y + c

        def update(l, rows):
            g = None
            for j in range(N_DEV):
                own = s_refs[l][j] if slabbed[l] else s_refs[l][...]
                p = jnp.where(me == j, own, l_refs[l][j]).astype(F32)
                g = p if g is None else g + p
            nm = ADAM_B1 * m_ref[rows, :] + (1.0 - ADAM_B1) * g
            nv = ADAM_B2 * v_ref[rows, :] + (1.0 - ADAM_B2) * (g * g)
            g_ref[rows, :] = g
            nm_ref[rows, :] = nm
            nv_ref[rows, :] = nv
            d_ref[rows, :] = -ADAM_LR * ((nm / c1) / (jnp.sqrt(nv / c2) + ADAM_EPS) + ADAM_WD * w_ref[rows, :])

        for l in range(npart):
            if tiled:
                pl.when(pl.program_id(0) == l)(lambda l=l: update(l, slice(None)))
            else:
                update(l, slice(l * rp, (l + 1) * rp))

    def part_spec(l, p):
        if p.ndim == 3:
            return pl.BlockSpec((N_DEV, tr, C), lambda q, i: (0, jnp.where(q == l, i, 0), 0))
        return pl.BlockSpec((tr, C), lambda q, i: (jnp.where(q == l, i, 0), 0))

    if tiled:
        blk = pl.BlockSpec((tr, C), lambda q, i: ((first + q) * steps + i, 0))
    else:
        blk = pl.BlockSpec((R, C), lambda q, i: (0, 0))
    out = jax.ShapeDtypeStruct((R, C), F32)
    n_in = 2 * npart + 3
    return pl.pallas_call(
        body, name=name, grid=(npart, steps) if tiled else (1, 1),
        in_specs=[part_spec(l, p) for l, p in enumerate(lands)] + [part_spec(l, p) for l, p in enumerate(srcs)]
        + [blk, blk, blk] + [ANY] * n_into,
        out_specs=(blk, blk, blk, blk),
        out_shape=(out, out, out, out),
        input_output_aliases={n_in + i: i for i in range(n_into)},
        compiler_params=_params(("arbitrary", "arbitrary")),
    )(*lands, *srcs, w, m, v, *(into or ()))


def _my_place():
    return lax.axis_index("x"), lax.axis_index("y"), lax.axis_index("c")


HBM = pl.BlockSpec(memory_space=pltpu.HBM)
SEM = pl.BlockSpec(memory_space=pltpu.SEMAPHORE)
DATAFLOW = pltpu.SideEffectType.DATAFLOW_SIDE_EFFECTING
N_PEER = N_DEV - 1


def _xchg_copies(src, land, send_sem, recv_sem, gather):
    x, y, c = _my_place()
    me = 4 * x + 2 * y + c
    copies = []
    for k in range(N_PEER):
        px, py, pc = x ^ (((k + 1) >> 2) & 1), y ^ (((k + 1) >> 1) & 1), c ^ ((k + 1) & 1)
        peer = 4 * px + 2 * py + pc
        for a in range(len(src)):
            i = a * N_PEER + k
            out = src[a] if gather[a] else src[a].at[peer]
            sent, landed = [pltpu.make_async_remote_copy(
                src_ref=out, dst_ref=land[a].at[slab], send_sem=send_sem.at[i], recv_sem=recv_sem.at[i],
                device_id=(px, py, pc), device_id_type=MESH) for slab in (me, peer)]
            copies.append((sent, landed))
    return copies


def _xchg_start(srcs, *, gather, name):
    n = len(srcs)
    lands = [lax.empty(((N_DEV,) + v.shape) if g else v.shape, v.dtype) for v, g in zip(srcs, gather)]

    def body(*refs):
        src, land = refs[:n], refs[n:2 * n]
        send_sem, recv_sem = refs[2 * n], refs[2 * n + 1]
        token = refs[-1]
        for sent, _ in _xchg_copies(src, land, send_sem, recv_sem, gather):
            sent.start()
        token[...] = jnp.zeros_like(token)

    sems = pltpu.SemaphoreType.DMA((n * N_PEER,))
    outs = pl.pallas_call(
        body, name=name,
        out_shape=(sems, sems, *[pltpu.HBM(v.shape, v.dtype) for v in srcs + lands],
                   jax.ShapeDtypeStruct((8, LANES), F32)),
        in_specs=[HBM] * (2 * n),
        out_specs=(SEM, SEM, *[HBM] * (2 * n), pl.BlockSpec(memory_space=pltpu.VMEM)),
        input_output_aliases={i: 2 + i for i in range(2 * n)},
        compiler_params=pltpu.CompilerParams(has_side_effects=DATAFLOW),
    )(*[pltpu.with_memory_space_constraint(v, pltpu.HBM) for v in srcs + lands])
    return dict(send=outs[0], recv=outs[1], srcs=list(outs[2:2 + n]), lands=list(outs[2 + n:2 + 2 * n]),
                token=outs[-1], gather=gather)


def _xchg_wait(h, after, *, name):
    n = len(h["srcs"])
    gather = h["gather"]

    def body(*refs):
        src, land = refs[:n], refs[n:2 * n]
        send_sem, recv_sem = refs[2 * n], refs[2 * n + 1]
        for sent, landed in _xchg_copies(src, land, send_sem, recv_sem, gather):
            sent.wait_send()
            landed.wait_recv()

    thru = h["srcs"] + h["lands"]
    outs = pl.pallas_call(
        body, name=name,
        out_shape=[pltpu.HBM(v.shape, v.dtype) for v in thru],
        in_specs=[HBM] * (2 * n) + [SEM, SEM, ANY],
        out_specs=[HBM] * (2 * n),
        input_output_aliases={i: i for i in range(2 * n)},
        compiler_params=pltpu.CompilerParams(has_side_effects=DATAFLOW),
    )(*thru, h["send"], h["recv"], after)
    return list(outs[:n]), list(outs[n:])


def _with_own(land, own, me):
    slot = lax.broadcasted_iota(jnp.int32, land.shape, 0)
    return jnp.where(slot == me, own if own.ndim == land.ndim else own[None], land)


def _w_in_segments(j, cols):
    lo, hi = j * cols, (j + 1) * cols
    segs = []
    for a, b, where, shift in ((0, FG_AT, "main", 0), (FG_AT, FG_AT + N_HEADS, "fg", -FG_AT),
                               (FG_AT + N_HEADS, IN_W, "main", -N_HEADS)):
        s0, s1 = max(lo, a), min(hi, b)
        if s0 < s1:
            segs.append((s0 - lo, s1 - s0, where, s0 + shift))
    return segs


SLAB_W = 1024


def _window(off, width):
    base = min((off // LANES) * LANES, width - SLAB_W)
    return base, off - base


def _w_unpack(land, own, *, tr):
    _, D, cols = land.shape
    assert cols + LANES - 1 <= SLAB_W

    def body(land_ref, own_ref, main_ref, slab):
        x, y, c = _my_place()
        me = 4 * x + 2 * y + c
        lane = lax.broadcasted_iota(jnp.int32, (tr, SLAB_W), 1)
        main_ref[...] = jnp.zeros_like(main_ref)
        slab[...] = jnp.zeros_like(slab)
        for j in range(N_DEV):
            slab[:, :cols] = jnp.where(me == j, own_ref[...], land_ref[j])
            v = slab[...]
            segs = _w_in_segments(j, cols)
            for lo, n, where, dst in segs:
                part = v if len(segs) == 1 else jnp.where((lane >= lo) & (lane < lo + n), v, jnp.zeros_like(v))
                if where == "fg":
                    main_ref[:, MAIN_W:] = pltpu.roll(part, (SLAB_W - lo) % SLAB_W, 1)[:, :LANES]
                else:
                    base, r = _window(dst - lo, MAIN_W)
                    assert 0 <= r and lo + n + r <= SLAB_W
                    main_ref[:, base:base + SLAB_W] += pltpu.roll(part, r, 1)

    return pl.pallas_call(
        body, name="w_unpack", grid=(D // tr,),
        in_specs=[pl.BlockSpec((N_DEV, tr, cols), lambda i: (0, i, 0)), pl.BlockSpec((tr, cols), lambda i: (i, 0))],
        out_specs=pl.BlockSpec((tr, ALL_W), lambda i: (i, 0)),
        out_shape=jax.ShapeDtypeStruct((D, ALL_W), BF16),
        scratch_shapes=[pltpu.VMEM((tr, SLAB_W), BF16)],
        compiler_params=_params(("parallel",)),
    )(land, own)


def _g_pack(dw_main, dw_fg, cols, *, tr):
    D = dw_main.shape[0]

    def body(main_ref, fg_ref, out_ref, slab):
        lane = lax.broadcasted_iota(jnp.int32, (tr, SLAB_W), 1)
        slab[...] = jnp.zeros_like(slab)
        slab[:, :LANES] = fg_ref[...]
        fg = slab[...]
        for j in range(N_DEV):
            v = None
            for lo, n, where, src in _w_in_segments(j, cols):
                if where == "fg":
                    part = pltpu.roll(fg, lo, 1)
                else:
                    base, r = _window(src - lo, MAIN_W)
                    assert 0 <= r and lo + n + r <= SLAB_W
                    part = pltpu.roll(main_ref[:, base:base + SLAB_W], (SLAB_W - r) % SLAB_W, 1)
                v = part if v is None else jnp.where((lane >= lo) & (lane < lo + n), part, v)
            out_ref[j] = v[:, :cols]

    return pl.pallas_call(
        body, name="g_pack", grid=(D // tr,),
        in_specs=[pl.BlockSpec((tr, MAIN_W), lambda i: (i, 0)), pl.BlockSpec((tr, LANES), lambda i: (i, 0))],
        out_specs=pl.BlockSpec((N_DEV, tr, cols), lambda i: (0, i, 0)),
        out_shape=jax.ShapeDtypeStruct((N_DEV, D, cols), BF16),
        scratch_shapes=[pltpu.VMEM((tr, SLAB_W), BF16)],
        compiler_params=_params(("parallel",)),
    )(dw_main, dw_fg)


TQ = 512
TS = 256
W_TILE_ROWS = 128


def _layer_fwd(x, xb, xbt, wl):
    qkv = _mm(xb, wl["w_in"], out_dtype=BF16, tm=1024, tn=512, tk=D_MODEL, name="mm_qkv", n_cols=QKV_W)
    rest = _mm(xb, wl["w_in"], out_dtype=F32, tm=1024, tn=512, tk=D_MODEL, name="mm_rest", n_off=QKV_W, n_cols=REST_W)
    fgp = _mm(xb, wl["w_in"], out_dtype=F32, tm=1024, tn=LANES, tk=D_MODEL, name="mm_fg", n_off=MAIN_W, n_cols=LANES)
    cum = _fg_fwd(fgp, wl["b_f"])
    S = x.shape[0]
    cumr = (cum[:, :N_HEADS].T * LOG2E).reshape(N_HEADS, S // TQ, 1, TQ)
    o, y, lse = _attn_fwd(qkv, rest, cum, cumr, tq=TQ)
    y = _conv_fwd(rest, wl["conv_w"], y)
    y = _pool_fwd(rest, wl["pool_w"], wl["pool_scale"], y)
    z = _mm(y, wl["w_out"], out_dtype=F32, tm=1024, tn=512, tk=D_MODEL, name="mm_out", c=x, c_scale=ALPHA)
    out, outb, outbt, xhat, rstd = _ln_fwd(z, wl["ln_g"], wl["ln_b"], ts=TS)
    saved = dict(xbt=xbt, qkv=qkv, rest=rest, fgp=fgp, cum=cum, cumr=cumr, o=o, lse=lse, y=y, xhat=xhat, rstd=rstd)
    return out, outb, outbt, saved


def _layer_bwd_weights(dout, sv, wl):
    S = dout.shape[0]
    dz, dzb, dln_g, dln_b = _ln_bwd(dout, sv["xhat"], sv["rstd"], wl["ln_g"], ts=TS)
    dy = _mm(dzb, wl["w_out"], out_dtype=F32, tm=1024, tn=512, tk=D_MODEL, name="mm_dy", bt=True)
    dw_out = _mm(sv["y"], dzb, out_dtype=BF16, tm=512, tn=D_MODEL, tk=2048, name="mm_dw_out", at=True)
    rest = sv["rest"]
    do, dg_att, a, delta = _attn_bwd_pre(rest, sv["o"], dy, sv["cum"], sv["lse"])
    dq, dk, dv, dcr, dcc = _attn_bwd(sv["qkv"], do, a, delta, sv["cumr"], tq=TQ)
    dcum = dcr - jnp.pad(dcc.reshape(N_HEADS, S).T, ((0, 0), (0, LANES - N_HEADS)))
    dfg, db_f = _fg_bwd(dcum, sv["fgp"], wl["b_f"])
    dcb, dcc_, dch, dgc, dconv_w = _conv_bwd(rest, dy, wl["conv_w"])
    dpu, dgp, dpool_w, dpool_scale = _pool_bwd(rest, dy, wl["pool_w"], wl["pool_scale"])
    dproj = jnp.concatenate([dq, dk, dv, dg_att, dcb, dcc_, dch, dgc, dpu, dgp, dfg], axis=1)
    xt = sv["xbt"]
    dw_main = _mm(xt, dproj, out_dtype=BF16, tm=1024, tn=512, tk=S, name="mm_dw_main", n_cols=MAIN_W)
    dw_fg = _mm(xt, dproj, out_dtype=BF16, tm=1024, tn=LANES, tk=2048, name="mm_dw_fg", n_off=MAIN_W, n_cols=LANES)
    grads = dict(w_main=dw_main, w_fg=dw_fg, b_f=db_f[:, :N_HEADS], conv_w=dconv_w, pool_w=dpool_w, pool_scale=dpool_scale,
                 w_out=dw_out, ln_g=dln_g, ln_b=dln_b)
    return grads, (dproj, dz)


def _layer_bwd_input(ctx, wl, after):
    dproj, dz = ctx
    return _mm(dproj, wl["w_in"], out_dtype=F32, tm=512, tn=512, tk=ALL_W, name="mm_dx", c=dz, c_scale=ALPHA, bt=True,
               after=after)


def kernel(x, w_in, b_f, conv_w, pool_w, pool_scale, w_out, ln_g, ln_b, loss_target, m_w_in, m_b_f, m_conv_w, m_pool_w, m_pool_scale, m_w_out, m_ln_g, m_ln_b, v_w_in, v_b_f, v_conv_w, v_pool_w, v_pool_scale, v_w_out, v_ln_g, v_ln_b):
    L, D, cols = w_in.shape
    rows_out = w_out.shape[1]
    ccols = conv_w.shape[2]
    mx, my, mc = _my_place()
    me = 4 * mx + 2 * my + mc

    w_in_b, w_out_b = w_in.astype(BF16), w_out.astype(BF16)
    gathers = [_xchg_start([w_in_b[l], w_out_b[l], conv_w[l]], gather=[True] * 3, name=f"gather_start_{l}")
               for l in range(L)]
    started = gathers[0]["token"]
    for h in gathers[1:]:
        started = started + h["token"]

    xl = x[0]
    xb, xbt = _prep(xl, started, ts=TS)
    weights, saved = [], []
    for l in range(L):
        (own_in, own_out, own_cw), (g_in, g_out, g_cw) = _xchg_wait(gathers[l], xb, name=f"gather_wait_{l}")
        w_all = _w_unpack(g_in, own_in, tr=W_TILE_ROWS)
        w_o = _with_own(g_out, own_out, me).reshape(N_DEV * rows_out, D)
        cw = jnp.transpose(_with_own(g_cw, own_cw, me), (1, 0, 2)).reshape(CONV_TAPS, N_DEV * ccols)
        wl = dict(w_in=w_all, w_out=w_o, conv_w=cw,
                  b_f=jnp.pad(b_f[l][None, :], ((0, 0), (0, LANES - N_HEADS))), pool_w=pool_w[l],
                  pool_scale=pool_scale[l][None, :], ln_g=ln_g[l][None, :], ln_b=ln_b[l][None, :])
        xl, xb, xbt, sv = _layer_fwd(xl, xb, xbt, wl)
        weights.append(wl)
        saved.append(sv)

    loss, dx = _loss_head(xl, loss_target[0], ts=TS)
    loss = lax.psum(loss[0, 0], ("x", "y", "c"))

    sharded, small = ["w_in", "w_out", "conv_w"], ["b_f", "pool_w", "pool_scale", "ln_g", "ln_b"]
    names = sharded + small
    scatters = [None] * L
    for l in reversed(range(L)):
        g, ctx = _layer_bwd_weights(dx, saved[l], weights[l])
        slabs = [_g_pack(g["w_main"], g["w_fg"], cols, tr=W_TILE_ROWS), g["w_out"].reshape(N_DEV, rows_out, D),
                 jnp.transpose(g["conv_w"].reshape(CONV_TAPS, N_DEV, ccols), (1, 0, 2))]
        scatters[l] = _xchg_start(slabs + [g[k] for k in small], gather=[False] * 3 + [True] * 5,
                                  name=f"scatter_start_{l}")
        dx = _layer_bwd_input(ctx, weights[l], scatters[l]["token"])

    given = dict(w_in=(w_in, m_w_in, v_w_in), b_f=(b_f, m_b_f, v_b_f), conv_w=(conv_w, m_conv_w, v_conv_w),
                 pool_w=(pool_w, m_pool_w, v_pool_w), pool_scale=(pool_scale, m_pool_scale, v_pool_scale),
                 w_out=(w_out, m_w_out, v_w_out), ln_g=(ln_g, m_ln_g, v_ln_g), ln_b=(ln_b, m_ln_b, v_ln_b))

    def update(k, layers, into=None):
        a = names.index(k)
        w, m, v = given[k]
        C = w.shape[-1]
        w2, m2, v2 = (t.reshape(-1, C) for t in (w, m, v))
        rp = w2.shape[0] // L
        lands = [landed[l][a].reshape(N_DEV, rp, C) for l in layers]
        srcs = [sent[l][a].reshape((N_DEV, rp, C) if k in sharded else (rp, C)) for l in layers]
        return _adamw(lands, srcs, w2, m2, v2, name=f"adamw_{k}_{layers[0]}", first=layers[0], into=into)

    sent, landed = [None] * L, [None] * L
    for l in reversed(range(1, L)):
        sent[l], landed[l] = _xchg_wait(scatters[l], dx, name=f"scatter_wait_{l}")
    big = ["w_in", "w_out", "pool_w"]
    upper = {k: update(k, list(range(1, L))) for k in big} if L > 1 else {}
    sent[0], landed[0] = _xchg_wait(scatters[0], upper["w_in"][0] if upper else dx, name="scatter_wait_0")
    res = {}
    for k in names:
        outs = update(k, [0], into=upper[k]) if k in upper else update(k, list(range(L)))
        res[k] = [t.reshape(given[k][0].shape) for t in outs]

    order = ["w_in", "b_f", "conv_w", "pool_w", "pool_scale", "w_out", "ln_g", "ln_b"]
    return (loss, dx[None], *[res[k][0] for k in order], *[res[k][1] for k in order],
            *[res[k][2] for k in order], *[res[k][3] for k in order])
```

```python
import jax
import jax.numpy as jnp
from jax import lax
from jax.experimental import pallas as pl
from jax.experimental.pallas import tpu as pltpu

F32 = jnp.float32
BF16 = jnp.bfloat16

N_DEV = 8
D_MODEL = 2048
N_HEADS = 8
HEAD_DIM = 128
ATT_W = N_HEADS * HEAD_DIM
CONV_W = 512
CONV_TAPS = 3
POOL_W = 512
POOL_GROUP = 128
N_POOL = POOL_W // POOL_GROUP
FG_AT = 4 * ATT_W
IN_W = FG_AT + N_HEADS + 4 * CONV_W + 2 * POOL_W
MAIN_W = IN_W - N_HEADS
LANES = 128
ALL_W = MAIN_W + LANES
LN_EPS = 1e-5
DEPTH = 4
ALPHA = (2 * DEPTH) ** 0.25
SCALE = HEAD_DIM ** -0.5
LOG2E = 1.4426950408889634
LN2 = 0.6931471805599453
NEG = -1e30

ADAM_LR, ADAM_B1, ADAM_B2, ADAM_EPS, ADAM_WD, ADAM_STEP = 0.001, 0.9, 0.999, 1e-08, 0.01, 10

QKV_W = 3 * ATT_W
REST_W = MAIN_W - QKV_W
QB, KB, VB = 0, 8, 16
GB = 0
CBB, CCB, CHB, CGB = 8, 12, 16, 20
PUB, PGB = 24, 28
VMEM_LIMIT = 48 * 1024 * 1024
ADAMW_BLOCK_ELEMS = 64 * 1024

MESH = pl.DeviceIdType.MESH
ANY = pl.BlockSpec(memory_space=pl.ANY)


def _params(semantics):
    return pltpu.CompilerParams(dimension_semantics=semantics, vmem_limit_bytes=VMEM_LIMIT)


def _sigmoid(x):
    return 1.0 / (1.0 + jnp.exp(-x))


def _mm(a, b, *, out_dtype, tm, tn, tk, name, c=None, c_scale=1.0, n_off=0, n_cols=None, bt=False, at=False,
        after=None):
    M, K = a.shape[::-1] if at else a.shape
    N = b.shape[0 if bt else 1] if n_cols is None else n_cols
    assert not (bt and n_off)
    dims = (((0 if at else 1,), (1 if bt else 0,)), ((), ()))
    tm, tn, tk = min(tm, M), min(tn, N), min(tk, K)
    assert M % tm == 0 and N % tn == 0 and K % tk == 0 and n_off % tn == 0, (a.shape, b.shape, tm, tn, tk)
    nk, joff = K // tk, n_off // tn
    has_c, has_after = c is not None, after is not None

    def body(*refs):
        a_ref, b_ref = refs[0], refs[1]
        c_ref = refs[2] if has_c else None
        o_ref = refs[2 + has_c + has_after]

        def finish(r):
            if has_c:
                r = r + c_scale * c_ref[...]
            o_ref[...] = r.astype(out_dtype)

        if nk == 1:
            finish(lax.dot_general(a_ref[...], b_ref[...], dims, preferred_element_type=F32))
        else:
            acc_ref = refs[3 + has_c + has_after]
            k = pl.program_id(2)

            @pl.when(k == 0)
            def _():
                acc_ref[...] = jnp.zeros_like(acc_ref)

            acc_ref[...] += lax.dot_general(a_ref[...], b_ref[...], dims, preferred_element_type=F32)

            @pl.when(k == nk - 1)
            def _():
                finish(acc_ref[...])

    in_specs = [pl.BlockSpec((tk, tm), lambda i, j, k: (k, i)) if at else pl.BlockSpec((tm, tk), lambda i, j, k: (i, k)),
                pl.BlockSpec((tn, tk), lambda i, j, k: (j, k)) if bt
                else pl.BlockSpec((tk, tn), lambda i, j, k: (k, j + joff))]
    args = [a, b]
    if has_c:
        in_specs.append(pl.BlockSpec((tm, tn), lambda i, j, k: (i, j)))
        args.append(c)
    if has_after:
        in_specs.append(ANY)
        args.append(after)
    return pl.pallas_call(
        body, name=name, grid=(M // tm, N // tn, nk),
        in_specs=in_specs,
        out_specs=pl.BlockSpec((tm, tn), lambda i, j, k: (i, j)),
        out_shape=jax.ShapeDtypeStruct((M, N), out_dtype),
        scratch_shapes=[pltpu.VMEM((tm, tn), F32)] if nk > 1 else [],
        compiler_params=_params(("parallel", "parallel", "arbitrary")),
    )(*args)


def _shift_down(x, k, row):
    return jnp.where(row >= k, pltpu.roll(x, k, 0), 0.0)


def _shift_up(x, k, row):
    s = x.shape[0]
    return jnp.where(row < s - k, pltpu.roll(x, s - k, 0), 0.0)


def _fg_fwd(fgp, bf):
    S = fgp.shape[0]

    def body(f_ref, b_ref, cum_ref):
        a = f_ref[...] + b_ref[...]
        x = jnp.minimum(a, 0.0) - jnp.log(1.0 + jnp.exp(-jnp.abs(a)))
        row = lax.broadcasted_iota(jnp.int32, x.shape, 0)
        k = 1
        while k < S:
            x = x + _shift_down(x, k, row)
            k *= 2
        cum_ref[...] = x

    return pl.pallas_call(body, name="fg_fwd", out_shape=jax.ShapeDtypeStruct((S, LANES), F32),
                          compiler_params=_params(None))(fgp, bf)


def _fg_bwd(dcum, fgp, bf):
    S = fgp.shape[0]

    def body(d_ref, f_ref, b_ref, dfg_ref, dbf_ref):
        x = d_ref[...]
        row = lax.broadcasted_iota(jnp.int32, x.shape, 0)
        k = 1
        while k < S:
            x = x + _shift_up(x, k, row)
            k *= 2
        dfg = x * _sigmoid(-(f_ref[...] + b_ref[...]))
        dfg_ref[...] = dfg.astype(BF16)
        dbf_ref[...] = jnp.sum(dfg, axis=0, keepdims=True)

    return pl.pallas_call(
        body, name="fg_bwd",
        out_shape=(jax.ShapeDtypeStruct((S, LANES), BF16), jax.ShapeDtypeStruct((1, LANES), F32)),
        compiler_params=_params(None),
    )(dcum, fgp, bf)


def _colblock(S, off):
    return pl.BlockSpec((S, LANES), lambda h: (0, h + off))


def _head_col(S):
    return pl.BlockSpec((None, S, 1), lambda h: (h, 0, 0))


def _head_row(nq, tq):
    return pl.BlockSpec((None, nq, 1, tq), lambda h: (h, 0, 0, 0))


def _lane_of(ref, h):
    lane = lax.broadcasted_iota(jnp.int32, ref.shape, 1)
    return jnp.sum(jnp.where(lane == h, ref[...], 0.0), axis=1, keepdims=True)


def _attn_fwd(qkv, rest, cum, cumr, *, tq):
    S = qkv.shape[0]
    nq = S // tq

    def body(q_ref, k_ref, v_ref, g_ref, cum_ref, cumr_ref, o_ref, y_ref, lse_ref, cc):
        cc[...] = _lane_of(cum_ref, pl.program_id(0)) * LOG2E
        tri = (lax.broadcasted_iota(jnp.int32, (tq, tq), 1) <= lax.broadcasted_iota(jnp.int32, (tq, tq), 0))

        def q_step(qi, _):
            rows = pl.ds(pl.multiple_of(qi * tq, tq), tq)
            q = q_ref[rows, :]
            ci = cc[rows, :]

            def tile(kj, carry, masked):
                m, l, acc = carry
                cols = pl.ds(pl.multiple_of(kj * tq, tq), tq)
                s = lax.dot_general(q, k_ref[cols, :], (((1,), (1,)), ((), ())), preferred_element_type=F32)
                s = s * (SCALE * LOG2E) + ci - cumr_ref[kj]
                if masked:
                    s = jnp.where(tri, s, NEG)
                m_new = jnp.maximum(m, jnp.max(s, axis=1, keepdims=True))
                a = jnp.exp2(m - m_new)
                p = jnp.exp2(s - m_new)
                l = a * l + jnp.sum(p, axis=1, keepdims=True)
                acc = a * acc + jnp.dot(p.astype(BF16), v_ref[cols, :], preferred_element_type=F32)
                return m_new, l, acc

            init = (jnp.full((tq, 1), NEG, F32), jnp.zeros((tq, 1), F32), jnp.zeros((tq, HEAD_DIM), F32))
            carry = lax.fori_loop(0, qi, lambda kj, c: tile(kj, c, False), init)
            m, l, acc = tile(qi, carry, True)
            o = acc / l
            g = g_ref[rows, :]
            o_ref[rows, :] = o
            y_ref[rows, :] = (o * (g * _sigmoid(g))).astype(BF16)
            lse_ref[rows, :] = m * LN2 + jnp.log(l)
            return 0

        lax.fori_loop(0, nq, q_step, 0)

    return pl.pallas_call(
        body, name="attn_fwd", grid=(N_HEADS,),
        in_specs=[_colblock(S, QB), _colblock(S, KB), _colblock(S, VB), _colblock(S, GB),
                  pl.BlockSpec((S, LANES), lambda h: (0, 0)), _head_row(nq, tq)],
        out_specs=(_colblock(S, 0), _colblock(S, 0), _head_col(S)),
        out_shape=(jax.ShapeDtypeStruct((S, ATT_W), F32), jax.ShapeDtypeStruct((S, D_MODEL), BF16),
                   jax.ShapeDtypeStruct((N_HEADS, S, 1), F32)),
        scratch_shapes=[pltpu.VMEM((S, 1), F32)],
        compiler_params=_params(("arbitrary",)),
    )(qkv, qkv, qkv, rest, cum, cumr)


def _attn_bwd_pre(rest, o, dy, cum, lse):
    S = o.shape[0]

    def body(g_ref, o_ref, dy_ref, cum_ref, lse_ref, do_ref, dg_ref, a_ref, dl_ref):
        g = g_ref[...]
        sg = _sigmoid(g)
        ov = o_ref[...]
        dy = dy_ref[...]
        do = dy * (g * sg)
        do_ref[...] = do.astype(BF16)
        dg_ref[...] = (dy * ov * (sg * (1.0 + g * (1.0 - sg)))).astype(BF16)
        dl_ref[...] = jnp.sum(do * ov, axis=1, keepdims=True)
        a_ref[...] = (_lane_of(cum_ref, pl.program_id(0)) - lse_ref[...]) * LOG2E

    att = jax.ShapeDtypeStruct((S, ATT_W), BF16)
    col = jax.ShapeDtypeStruct((N_HEADS, S, 1), F32)
    return pl.pallas_call(
        body, name="attn_bwd_pre", grid=(N_HEADS,),
        in_specs=[_colblock(S, GB), _colblock(S, 0), _colblock(S, 0),
                  pl.BlockSpec((S, LANES), lambda h: (0, 0)), _head_col(S)],
        out_specs=(_colblock(S, 0), _colblock(S, 0), _head_col(S), _head_col(S)),
        out_shape=(att, att, col, col),
        compiler_params=_params(("parallel",)),
    )(rest, o, dy, cum, lse)


def _attn_bwd(qkv, do, a, delta, cumr, *, tq):
    S = qkv.shape[0]
    nq = S // tq
    tdot = (((0,), (0,)), ((), ()))

    def body(q_ref, k_ref, v_ref, do_ref, a_ref, dl_ref, cumr_ref, dq_ref, dk_ref, dv_ref, dcr_ref, dcc_ref, dqa):
        dqa[...] = jnp.zeros_like(dqa)

        @pl.when(pl.program_id(0) == 0)
        def _():
            dcr_ref[...] = jnp.zeros_like(dcr_ref)

        tri = (lax.broadcasted_iota(jnp.int32, (tq, tq), 1) <= lax.broadcasted_iota(jnp.int32, (tq, tq), 0))
        mine = lax.broadcasted_iota(jnp.int32, (tq, LANES), 1) == pl.program_id(0)

        def k_step(kj, _):
            cols = pl.ds(pl.multiple_of(kj * tq, tq), tq)
            k = k_ref[cols, :]
            v = v_ref[cols, :]
            cj = cumr_ref[kj]

            def tile(qi, carry, masked):
                dk, dv, csum = carry
                rows = pl.ds(pl.multiple_of(qi * tq, tq), tq)
                q = q_ref[rows, :]
                dot = do_ref[rows, :]
                s = lax.dot_general(q, k, (((1,), (1,)), ((), ())), preferred_element_type=F32)
                p = jnp.exp2(s * (SCALE * LOG2E) + a_ref[rows, :] - cj)
                if masked:
                    p = jnp.where(tri, p, 0.0)
                dv = dv + lax.dot_general(p.astype(BF16), dot, tdot, preferred_element_type=F32)
                dp = lax.dot_general(dot, v, (((1,), (1,)), ((), ())), preferred_element_type=F32)
                ds = p * (dp - dl_ref[rows, :])
                dcr_ref[rows, :] += jnp.where(mine, jnp.sum(ds, axis=1, keepdims=True), 0.0)
                csum = csum + jnp.sum(ds, axis=0, keepdims=True)
                dsb = (ds * SCALE).astype(BF16)
                dqa[rows, :] += jnp.dot(dsb, k, preferred_element_type=F32)
                dk = dk + lax.dot_general(dsb, q, tdot, preferred_element_type=F32)
                return dk, dv, csum

            init = (jnp.zeros((tq, HEAD_DIM), F32), jnp.zeros((tq, HEAD_DIM), F32), jnp.zeros((1, tq), F32))
            carry = tile(kj, init, True)
            dk, dv, csum = lax.fori_loop(kj + 1, nq, lambda qi, c: tile(qi, c, False), carry)
            dk_ref[cols, :] = dk.astype(BF16)
            dv_ref[cols, :] = dv.astype(BF16)
            dcc_ref[kj] = csum
            return 0

        lax.fori_loop(0, nq, k_step, 0)
        dq_ref[...] = dqa[...].astype(BF16)

    att = jax.ShapeDtypeStruct((S, ATT_W), BF16)
    return pl.pallas_call(
        body, name="attn_bwd", grid=(N_HEADS,),
        in_specs=[_colblock(S, QB), _colblock(S, KB), _colblock(S, VB), _colblock(S, 0),
                  _head_col(S), _head_col(S), _head_row(nq, tq)],
        out_specs=(_colblock(S, 0), _colblock(S, 0), _colblock(S, 0), pl.BlockSpec((S, LANES), lambda h: (0, 0)),
                   _head_row(nq, tq)),
        out_shape=(att, att, att, jax.ShapeDtypeStruct((S, LANES), F32),
                   jax.ShapeDtypeStruct((N_HEADS, nq, 1, tq), F32)),
        scratch_shapes=[pltpu.VMEM((S, HEAD_DIM), F32)],
        compiler_params=_params(("arbitrary",)),
    )(qkv, qkv, qkv, do, a, delta, cumr)


def _conv_taps(u, w_ref, row):
    return (w_ref[0:1, :] * _shift_down(u, 2, row) + w_ref[1:2, :] * _shift_down(u, 1, row)) + w_ref[2:3, :] * u


def _conv_fwd(rest, cw, y):
    S = rest.shape[0]

    def body(cb_ref, cc_ref, ch_ref, g_ref, w_ref, y_in, y_ref):
        row = lax.broadcasted_iota(jnp.int32, (S, LANES), 0)
        g = g_ref[...]
        y = _conv_taps(cc_ref[...] * ch_ref[...], w_ref, row)
        y_ref[...] = ((cb_ref[...] * y) * (g * _sigmoid(g))).astype(BF16)

    return pl.pallas_call(
        body, name="conv_fwd", grid=(CONV_W // LANES,),
        in_specs=[_colblock(S, CBB), _colblock(S, CCB), _colblock(S, CHB), _colblock(S, CGB),
                  pl.BlockSpec((CONV_TAPS, LANES), lambda j: (0, j)), ANY],
        out_specs=_colblock(S, ATT_W // LANES),
        out_shape=jax.ShapeDtypeStruct(y.shape, y.dtype),
        input_output_aliases={5: 0},
        compiler_params=_params(("parallel",)),
    )(rest, rest, rest, rest, cw, y)


def _conv_bwd(rest, dy, cw):
    S = rest.shape[0]

    def body(cb_ref, cc_ref, ch_ref, g_ref, dy_ref, w_ref, dcb_ref, dcc_ref, dch_ref, dg_ref, dw_ref):
        row = lax.broadcasted_iota(jnp.int32, (S, LANES), 0)
        g = g_ref[...]
        sg = _sigmoid(g)
        silu = g * sg
        cb, cc, ch, dy = cb_ref[...], cc_ref[...], ch_ref[...], dy_ref[...]
        u = cc * ch
        u1 = _shift_down(u, 1, row)
        u2 = _shift_down(u, 2, row)
        y = (w_ref[0:1, :] * u2 + w_ref[1:2, :] * u1) + w_ref[2:3, :] * u
        dcb_ref[...] = (dy * silu * y).astype(BF16)
        dg_ref[...] = (dy * (cb * y) * (sg * (1.0 + g * (1.0 - sg)))).astype(BF16)
        dyv = dy * silu * cb
        du = w_ref[2:3, :] * dyv + w_ref[1:2, :] * _shift_up(dyv, 1, row) + w_ref[0:1, :] * _shift_up(dyv, 2, row)
        dcc_ref[...] = (du * ch).astype(BF16)
        dch_ref[...] = (du * cc).astype(BF16)
        dw_ref[0:1, :] = jnp.sum(dyv * u2, axis=0, keepdims=True)
        dw_ref[1:2, :] = jnp.sum(dyv * u1, axis=0, keepdims=True)
        dw_ref[2:3, :] = jnp.sum(dyv * u, axis=0, keepdims=True)

    act = jax.ShapeDtypeStruct((S, CONV_W), BF16)
    return pl.pallas_call(
        body, name="conv_bwd", grid=(CONV_W // LANES,),
        in_specs=[_colblock(S, CBB), _colblock(S, CCB), _colblock(S, CHB), _colblock(S, CGB),
                  _colblock(S, ATT_W // LANES), pl.BlockSpec((CONV_TAPS, LANES), lambda j: (0, j))],
        out_specs=(_colblock(S, 0),) * 4 + (pl.BlockSpec((CONV_TAPS, LANES), lambda j: (0, j)),),
        out_shape=(act, act, act, act, jax.ShapeDtypeStruct((CONV_TAPS, CONV_W), F32)),
        compiler_params=_params(("parallel",)),
    )(rest, rest, rest, rest, dy, cw)


def _pool_z(u, grp, row):
    s2 = u + _shift_down(u, 1, row)
    s4 = s2 + _shift_down(s2, 2, row)
    s8 = s4 + _shift_down(s4, 4, row)
    s16 = s8 + _shift_down(s8, 8, row)
    sw = jnp.where(grp == 0, s2, jnp.where(grp == 1, s4, jnp.where(grp == 2, s8, s16)))
    return sw / _pool_count(grp, row) - u


def _pool_count(grp, row):
    return jnp.minimum(row + 1, 2 << grp).astype(F32)


def _pool_fwd(rest, pw, scale, y):
    S = rest.shape[0]

    def body(u_ref, g_ref, pw_ref, sc_ref, y_in, y_ref):
        grp = pl.program_id(0)
        row = lax.broadcasted_iota(jnp.int32, (S, LANES), 0)
        z = _pool_z(u_ref[...], grp, row)
        zp = jnp.dot(z.astype(BF16), pw_ref[...].astype(BF16), preferred_element_type=F32)
        g = g_ref[...]
        y_ref[...] = ((zp * sc_ref[...]) * (g * _sigmoid(g))).astype(BF16)

    return pl.pallas_call(
        body, name="pool_fwd", grid=(N_POOL,),
        in_specs=[_colblock(S, PUB), _colblock(S, PGB),
                  pl.BlockSpec((None, POOL_GROUP, POOL_GROUP), lambda j: (j, 0, 0)),
                  pl.BlockSpec((1, LANES), lambda j: (0, j)), ANY],
        out_specs=_colblock(S, (ATT_W + CONV_W) // LANES),
        out_shape=jax.ShapeDtypeStruct(y.shape, y.dtype),
        input_output_aliases={4: 0},
        compiler_params=_params(("parallel",)),
    )(rest, rest, pw, scale, y)


def _pool_bwd(rest, dy, pw, scale):
    S = rest.shape[0]

    def body(u_ref, g_ref, dy_ref, pw_ref, sc_ref, du_ref, dg_ref, dpw_ref, dsc_ref):
        grp = pl.program_id(0)
        row = lax.broadcasted_iota(jnp.int32, (S, LANES), 0)
        z = _pool_z(u_ref[...], grp, row).astype(BF16)
        pwb = pw_ref[...].astype(BF16)
        zp = jnp.dot(z, pwb, preferred_element_type=F32)
        g = g_ref[...]
        sg = _sigmoid(g)
        silu = g * sg
        dy = dy_ref[...]
        sc = sc_ref[...]
        dsc_ref[...] = jnp.sum(dy * silu * zp, axis=0, keepdims=True)
        dg_ref[...] = (dy * (zp * sc) * (sg * (1.0 + g * (1.0 - sg)))).astype(BF16)
        dzp = (dy * silu * sc).astype(BF16)
        dpw_ref[...] = lax.dot_general(z, dzp, (((0,), (0,)), ((), ())), preferred_element_type=F32)
        dz = lax.dot_general(dzp, pwb, (((1,), (1,)), ((), ())), preferred_element_type=F32)
        f1 = dz / _pool_count(grp, row)
        f2 = f1 + _shift_up(f1, 1, row)
        f4 = f2 + _shift_up(f2, 2, row)
        f8 = f4 + _shift_up(f4, 4, row)
        f16 = f8 + _shift_up(f8, 8, row)
        fw = jnp.where(grp == 0, f2, jnp.where(grp == 1, f4, jnp.where(grp == 2, f8, f16)))
        du_ref[...] = (fw - dz).astype(BF16)

    act = jax.ShapeDtypeStruct((S, POOL_W), BF16)
    return pl.pallas_call(
        body, name="pool_bwd", grid=(N_POOL,),
        in_specs=[_colblock(S, PUB), _colblock(S, PGB), _colblock(S, (ATT_W + CONV_W) // LANES),
                  pl.BlockSpec((None, POOL_GROUP, POOL_GROUP), lambda j: (j, 0, 0)),
                  pl.BlockSpec((1, LANES), lambda j: (0, j))],
        out_specs=(_colblock(S, 0), _colblock(S, 0),
                   pl.BlockSpec((None, POOL_GROUP, POOL_GROUP), lambda j: (j, 0, 0)),
                   pl.BlockSpec((1, LANES), lambda j: (0, j))),
        out_shape=(act, act, jax.ShapeDtypeStruct((N_POOL, POOL_GROUP, POOL_GROUP), F32),
                   jax.ShapeDtypeStruct((1, POOL_W), F32)),
        compiler_params=_params(("parallel",)),
    )(rest, rest, dy, pw, scale)


def _ln_fwd(z, g, b, *, ts):
    S, D = z.shape

    def body(z_ref, g_ref, b_ref, o_ref, ob_ref, obt_ref, xh_ref, rs_ref):
        zz = z_ref[...]
        mu = jnp.mean(zz, axis=1, keepdims=True)
        zc = zz - mu
        rstd = lax.rsqrt(jnp.mean(zc * zc, axis=1, keepdims=True) + LN_EPS)
        xh = zc * rstd
        out = xh * g_ref[...] + b_ref[...]
        o_ref[...] = out
        ob_ref[...] = out.astype(BF16)
        obt_ref[...] = out.T.astype(BF16)
        xh_ref[...] = xh
        rs_ref[...] = rstd

    rowblk = pl.BlockSpec((ts, D), lambda i: (i, 0))
    vec = pl.BlockSpec((1, D), lambda i: (0, 0))
    return pl.pallas_call(
        body, name="ln_fwd", grid=(S // ts,),
        in_specs=[rowblk, vec, vec],
        out_specs=(rowblk, rowblk, pl.BlockSpec((D, ts), lambda i: (0, i)), rowblk,
                   pl.BlockSpec((ts, 1), lambda i: (i, 0))),
        out_shape=(jax.ShapeDtypeStruct((S, D), F32), jax.ShapeDtypeStruct((S, D), BF16),
                   jax.ShapeDtypeStruct((D, S), BF16), jax.ShapeDtypeStruct((S, D), F32),
                   jax.ShapeDtypeStruct((S, 1), F32)),
        compiler_params=_params(("parallel",)),
    )(z, g, b)


def _prep(x, after, *, ts):
    S, D = x.shape

    def body(x_ref, after_ref, xb_ref, xbt_ref):
        xb_ref[...] = x_ref[...].astype(BF16)
        xbt_ref[...] = x_ref[...].T.astype(BF16)

    return pl.pallas_call(
        body, name="prep", grid=(S // ts,),
        in_specs=[pl.BlockSpec((ts, D), lambda i: (i, 0)), ANY],
        out_specs=(pl.BlockSpec((ts, D), lambda i: (i, 0)), pl.BlockSpec((D, ts), lambda i: (0, i))),
        out_shape=(jax.ShapeDtypeStruct((S, D), BF16), jax.ShapeDtypeStruct((D, S), BF16)),
        compiler_params=_params(("parallel",)),
    )(x, after)


def _ln_bwd(dout, xhat, rstd, g, *, ts):
    S, D = dout.shape

    def body(d_ref, xh_ref, rs_ref, g_ref, dz_ref, dzb_ref, dg_ref, db_ref):
        @pl.when(pl.program_id(0) == 0)
        def _():
            dg_ref[...] = jnp.zeros_like(dg_ref)
            db_ref[...] = jnp.zeros_like(db_ref)

        d = d_ref[...]
        xh = xh_ref[...]
        dxh = d * g_ref[...]
        m1 = jnp.mean(dxh, axis=1, keepdims=True)
        m2 = jnp.mean(dxh * xh, axis=1, keepdims=True)
        dz = rs_ref[...] * (dxh - m1 - xh * m2)
        dz_ref[...] = dz
        dzb_ref[...] = dz.astype(BF16)
        dg_ref[...] += jnp.sum(d * xh, axis=0, keepdims=True)
        db_ref[...] += jnp.sum(d, axis=0, keepdims=True)

    rowblk = pl.BlockSpec((ts, D), lambda i: (i, 0))
    vec = pl.BlockSpec((1, D), lambda i: (0, 0))
    return pl.pallas_call(
        body, name="ln_bwd", grid=(S // ts,),
        in_specs=[rowblk, rowblk, pl.BlockSpec((ts, 1), lambda i: (i, 0)), vec],
        out_specs=(rowblk, rowblk, vec, vec),
        out_shape=(jax.ShapeDtypeStruct((S, D), F32), jax.ShapeDtypeStruct((S, D), BF16),
                   jax.ShapeDtypeStruct((1, D), F32), jax.ShapeDtypeStruct((1, D), F32)),
        compiler_params=_params(("arbitrary",)),
    )(dout, xhat, rstd, g)


def _loss_head(y, tgt, *, ts):
    S, D = y.shape

    def body(y_ref, t_ref, l_ref, dy_ref):
        @pl.when(pl.program_id(0) == 0)
        def _():
            l_ref[...] = jnp.zeros_like(l_ref)

        e = y_ref[...] - t_ref[...]
        dy_ref[...] = e * (1.0 / D)
        rowloss = jnp.sum(e * e, axis=1, keepdims=True) * (0.5 / D)
        l_ref[...] += jnp.sum(rowloss, axis=0, keepdims=True)

    rowblk = pl.BlockSpec((ts, D), lambda i: (i, 0))
    return pl.pallas_call(
        body, name="loss_head", grid=(S // ts,),
        in_specs=[rowblk, rowblk],
        out_specs=(pl.BlockSpec((1, 1), lambda i: (0, 0)), rowblk),
        out_shape=(jax.ShapeDtypeStruct((1, 1), F32), jax.ShapeDtypeStruct((S, D), F32)),
        compiler_params=_params(("arbitrary",)),
    )(y, tgt)


def _adamw(lands, srcs, w, m, v, *, name, first=0, into=None):
    R, C = w.shape
    npart = len(lands)
    rp = lands[0].shape[1]
    assert R % rp == 0 and all(p.shape == (N_DEV, rp, C) for p in lands)
    tiled = rp % 8 == 0
    assert tiled or (first == 0 and npart * rp == R and into is None)
    tr = min(ADAMW_BLOCK_ELEMS // (LANES * pl.cdiv(C, LANES)), rp) if tiled else rp
    assert rp % tr == 0
    steps = rp // tr
    c1 = 1.0 - ADAM_B1 ** ADAM_STEP
    c2 = 1.0 - ADAM_B2 ** ADAM_STEP
    slabbed = [p.ndim == 3 for p in srcs]
    n_into = 0 if into is None else 4

    def body(*refs):
        l_refs, s_refs = refs[:npart], refs[npart:2 * npart]
        w_ref, m_ref, v_ref = refs[2 * npart:2 * npart + 3]
        g_ref, d_ref, nm_ref, nv_ref = refs[2 * npart + 3 + n_into:]
        x, y, c = _my_place()
        me = 4 * x + 2 * y + c

        def update(l, rows):
            g = None
            for j in range(N_DEV):
                own = s_refs[l][j] if slabbed[l] else s_refs[l][...]
                p = jnp.where(me == j, own, l_refs[l][j]).astype(F32)
                g = p if g is None else g + p
            nm = ADAM_B1 * m_ref[rows, :] + (1.0 - ADAM_B1) * g
            nv = ADAM_B2 * v_ref[rows, :] + (1.0 - ADAM_B2) * (g * g)
            g_ref[rows, :] = g
            nm_ref[rows, :] = nm
            nv_ref[rows, :] = nv
            d_ref[rows, :] = -ADAM_LR * ((nm / c1) / (jnp.sqrt(nv / c2) + ADAM_EPS) + ADAM_WD * w_ref[rows, :])

        for l in range(npart):
            if tiled:
                pl.when(pl.program_id(0) == l)(lambda l=l: update(l, slice(None)))
            else:
                update(l, slice(l * rp, (l + 1) * rp))

    def part_spec(l, p):
        if p.ndim == 3:
            return pl.BlockSpec((N_DEV, tr, C), lambda q, i: (0, jnp.where(q == l, i, 0), 0))
        return pl.BlockSpec((tr, C), lambda q, i: (jnp.where(q == l, i, 0), 0))

    if tiled:
        blk = pl.BlockSpec((tr, C), lambda q, i: ((first + q) * steps + i, 0))
    else:
        blk = pl.BlockSpec((R, C), lambda q, i: (0, 0))
    out = jax.ShapeDtypeStruct((R, C), F32)
    n_in = 2 * npart + 3
    return pl.pallas_call(
        body, name=name, grid=(npart, steps) if tiled else (1, 1),
        in_specs=[part_spec(l, p) for l, p in enumerate(lands)] + [part_spec(l, p) for l, p in enumerate(srcs)]
        + [blk, blk, blk] + [ANY] * n_into,
        out_specs=(blk, blk, blk, blk),
        out_shape=(out, out, out, out),
        input_output_aliases={n_in + i: i for i in range(n_into)},
        compiler_params=_params(("arbitrary", "arbitrary")),
    )(*lands, *srcs, w, m, v, *(into or ()))


def _my_place():
    return lax.axis_index("x"), lax.axis_index("y"), lax.axis_index("c")


HBM = pl.BlockSpec(memory_space=pltpu.HBM)
SEM = pl.BlockSpec(memory_space=pltpu.SEMAPHORE)
DATAFLOW = pltpu.SideEffectType.DATAFLOW_SIDE_EFFECTING
N_PEER = N_DEV - 1


def _xchg_copies(src, land, send_sem, recv_sem, gather):
    x, y, c = _my_place()
    me = 4 * x + 2 * y + c
    copies = []
    for k in range(N_PEER):
        px, py, pc = x ^ (((k + 1) >> 2) & 1), y ^ (((k + 1) >> 1) & 1), c ^ ((k + 1) & 1)
        peer = 4 * px + 2 * py + pc
        for a in range(len(src)):
            i = a * N_PEER + k
            out = src[a] if gather[a] else src[a].at[peer]
            sent, landed = [pltpu.make_async_remote_copy(
                src_ref=out, dst_ref=land[a].at[slab], send_sem=send_sem.at[i], recv_sem=recv_sem.at[i],
                device_id=(px, py, pc), device_id_type=MESH) for slab in (me, peer)]
            copies.append((sent, landed))
    return copies


def _xchg_start(srcs, *, gather, name):
    n = len(srcs)
    lands = [lax.empty(((N_DEV,) + v.shape) if g else v.shape, v.dtype) for v, g in zip(srcs, gather)]

    def body(*refs):
        src, land = refs[:n], refs[n:2 * n]
        send_sem, recv_sem = refs[2 * n], refs[2 * n + 1]
        token = refs[-1]
        for sent, _ in _xchg_copies(src, land, send_sem, recv_sem, gather):
            sent.start()
        token[...] = jnp.zeros_like(token)

    sems = pltpu.SemaphoreType.DMA((n * N_PEER,))
    outs = pl.pallas_call(
        body, name=name,
        out_shape=(sems, sems, *[pltpu.HBM(v.shape, v.dtype) for v in srcs + lands],
                   jax.ShapeDtypeStruct((8, LANES), F32)),
        in_specs=[HBM] * (2 * n),
        out_specs=(SEM, SEM, *[HBM] * (2 * n), pl.BlockSpec(memory_space=pltpu.VMEM)),
        input_output_aliases={i: 2 + i for i in range(2 * n)},
        compiler_params=pltpu.CompilerParams(has_side_effects=DATAFLOW),
    )(*[pltpu.with_memory_space_constraint(v, pltpu.HBM) for v in srcs + lands])
    return dict(send=outs[0], recv=outs[1], srcs=list(outs[2:2 + n]), lands=list(outs[2 + n:2 + 2 * n]),
                token=outs[-1], gather=gather)


def _xchg_wait(h, after, *, name):
    n = len(h["srcs"])
    gather = h["gather"]

    def body(*refs):
        src, land = refs[:n], refs[n:2 * n]
        send_sem, recv_sem = refs[2 * n], refs[2 * n + 1]
        for sent, landed in _xchg_copies(src, land, send_sem, recv_sem, gather):
            sent.wait_send()
            landed.wait_recv()

    thru = h["srcs"] + h["lands"]
    outs = pl.pallas_call(
        body, name=name,
        out_shape=[pltpu.HBM(v.shape, v.dtype) for v in thru],
        in_specs=[HBM] * (2 * n) + [SEM, SEM, ANY],
        out_specs=[HBM] * (2 * n),
        input_output_aliases={i: i for i in range(2 * n)},
        compiler_params=pltpu.CompilerParams(has_side_effects=DATAFLOW),
    )(*thru, h["send"], h["recv"], after)
    return list(outs[:n]), list(outs[n:])


def _with_own(land, own, me):
    slot = lax.broadcasted_iota(jnp.int32, land.shape, 0)
    return jnp.where(slot == me, own if own.ndim == land.ndim else own[None], land)


def _w_in_segments(j, cols):
    lo, hi = j * cols, (j + 1) * cols
    segs = []
    for a, b, where, shift in ((0, FG_AT, "main", 0), (FG_AT, FG_AT + N_HEADS, "fg", -FG_AT),
                               (FG_AT + N_HEADS, IN_W, "main", -N_HEADS)):
        s0, s1 = max(lo, a), min(hi, b)
        if s0 < s1:
            segs.append((s0 - lo, s1 - s0, where, s0 + shift))
    return segs


SLAB_W = 1024


def _window(off, width):
    base = min((off // LANES) * LANES, width - SLAB_W)
    return base, off - base


def _w_unpack(land, own, *, tr):
    _, D, cols = land.shape
    assert cols + LANES - 1 <= SLAB_W

    def body(land_ref, own_ref, main_ref, slab):
        x, y, c = _my_place()
        me = 4 * x + 2 * y + c
        lane = lax.broadcasted_iota(jnp.int32, (tr, SLAB_W), 1)
        main_ref[...] = jnp.zeros_like(main_ref)
        slab[...] = jnp.zeros_like(slab)
        for j in range(N_DEV):
            slab[:, :cols] = jnp.where(me == j, own_ref[...], land_ref[j])
            v = slab[...]
            segs = _w_in_segments(j, cols)
            for lo, n, where, dst in segs:
                part = v if len(segs) == 1 else jnp.where((lane >= lo) & (lane < lo + n), v, jnp.zeros_like(v))
                if where == "fg":
                    main_ref[:, MAIN_W:] = pltpu.roll(part, (SLAB_W - lo) % SLAB_W, 1)[:, :LANES]
                else:
                    base, r = _window(dst - lo, MAIN_W)
                    assert 0 <= r and lo + n + r <= SLAB_W
                    main_ref[:, base:base + SLAB_W] += pltpu.roll(part, r, 1)

    return pl.pallas_call(
        body, name="w_unpack", grid=(D // tr,),
        in_specs=[pl.BlockSpec((N_DEV, tr, cols), lambda i: (0, i, 0)), pl.BlockSpec((tr, cols), lambda i: (i, 0))],
        out_specs=pl.BlockSpec((tr, ALL_W), lambda i: (i, 0)),
        out_shape=jax.ShapeDtypeStruct((D, ALL_W), BF16),
        scratch_shapes=[pltpu.VMEM((tr, SLAB_W), BF16)],
        compiler_params=_params(("parallel",)),
    )(land, own)


def _g_pack(dw_main, dw_fg, cols, *, tr):
    D = dw_main.shape[0]

    def body(main_ref, fg_ref, out_ref, slab):
        lane = lax.broadcasted_iota(jnp.int32, (tr, SLAB_W), 1)
        slab[...] = jnp.zeros_like(slab)
        slab[:, :LANES] = fg_ref[...]
        fg = slab[...]
        for j in range(N_DEV):
            v = None
            for lo, n, where, src in _w_in_segments(j, cols):
                if where == "fg":
                    part = pltpu.roll(fg, lo, 1)
                else:
                    base, r = _window(src - lo, MAIN_W)
                    assert 0 <= r and lo + n + r <= SLAB_W
                    part = pltpu.roll(main_ref[:, base:base + SLAB_W], (SLAB_W - r) % SLAB_W, 1)
                v = part if v is None else jnp.where((lane >= lo) & (lane < lo + n), part, v)
            out_ref[j] = v[:, :cols]

    return pl.pallas_call(
        body, name="g_pack", grid=(D // tr,),
        in_specs=[pl.BlockSpec((tr, MAIN_W), lambda i: (i, 0)), pl.BlockSpec((tr, LANES), lambda i: (i, 0))],
        out_specs=pl.BlockSpec((N_DEV, tr, cols), lambda i: (0, i, 0)),
        out_shape=jax.ShapeDtypeStruct((N_DEV, D, cols), BF16),
        scratch_shapes=[pltpu.VMEM((tr, SLAB_W), BF16)],
        compiler_params=_params(("parallel",)),
    )(dw_main, dw_fg)


TQ = 512
TS = 256
W_TILE_ROWS = 128


def _layer_fwd_proj(xb, w_in):
    qkv = _mm(xb, w_in, out_dtype=BF16, tm=1024, tn=512, tk=D_MODEL, name="mm_qkv", n_cols=QKV_W)
    rest = _mm(xb, w_in, out_dtype=F32, tm=1024, tn=512, tk=D_MODEL, name="mm_rest", n_off=QKV_W, n_cols=REST_W)
    fgp = _mm(xb, w_in, out_dtype=F32, tm=1024, tn=LANES, tk=D_MODEL, name="mm_fg", n_off=MAIN_W, n_cols=LANES)
    return qkv, rest, fgp


def _layer_fwd_mix(x, xbt, proj, wl):
    qkv, rest, fgp = proj
    cum = _fg_fwd(fgp, wl["b_f"])
    S = x.shape[0]
    cumr = (cum[:, :N_HEADS].T * LOG2E).reshape(N_HEADS, S // TQ, 1, TQ)
    o, y, lse = _attn_fwd(qkv, rest, cum, cumr, tq=TQ)
    y = _conv_fwd(rest, wl["conv_w"], y)
    y = _pool_fwd(rest, wl["pool_w"], wl["pool_scale"], y)
    z = _mm(y, wl["w_out"], out_dtype=F32, tm=1024, tn=512, tk=D_MODEL, name="mm_out", c=x, c_scale=ALPHA)
    out, outb, outbt, xhat, rstd = _ln_fwd(z, wl["ln_g"], wl["ln_b"], ts=TS)
    saved = dict(xbt=xbt, qkv=qkv, rest=rest, fgp=fgp, cum=cum, cumr=cumr, o=o, lse=lse, y=y, xhat=xhat, rstd=rstd)
    return out, outb, outbt, saved


def _layer_bwd_mix(dout, sv, wl):
    S = dout.shape[0]
    dz, dzb, dln_g, dln_b = _ln_bwd(dout, sv["xhat"], sv["rstd"], wl["ln_g"], ts=TS)
    dy = _mm(dzb, wl["w_out"], out_dtype=F32, tm=1024, tn=512, tk=D_MODEL, name="mm_dy", bt=True)
    dw_out = _mm(sv["y"], dzb, out_dtype=BF16, tm=512, tn=D_MODEL, tk=2048, name="mm_dw_out", at=True)
    rest = sv["rest"]
    do, dg_att, a, delta = _attn_bwd_pre(rest, sv["o"], dy, sv["cum"], sv["lse"])
    dq, dk, dv, dcr, dcc = _attn_bwd(sv["qkv"], do, a, delta, sv["cumr"], tq=TQ)
    dcum = dcr - jnp.pad(dcc.reshape(N_HEADS, S).T, ((0, 0), (0, LANES - N_HEADS)))
    dfg, db_f = _fg_bwd(dcum, sv["fgp"], wl["b_f"])
    dcb, dcc_, dch, dgc, dconv_w = _conv_bwd(rest, dy, wl["conv_w"])
    dpu, dgp, dpool_w, dpool_scale = _pool_bwd(rest, dy, wl["pool_w"], wl["pool_scale"])
    grads = dict(b_f=db_f[:, :N_HEADS], conv_w=dconv_w, pool_w=dpool_w, pool_scale=dpool_scale, w_out=dw_out,
                 ln_g=dln_g, ln_b=dln_b)
    return grads, ([dq, dk, dv, dg_att, dcb, dcc_, dch, dgc, dpu, dgp, dfg], dz)


def _layer_bwd_w_in(mid, sv):
    pieces, dz = mid
    dproj = jnp.concatenate(pieces, axis=1)
    xt = sv["xbt"]
    S = xt.shape[1]
    dw_main = _mm(xt, dproj, out_dtype=BF16, tm=1024, tn=512, tk=S, name="mm_dw_main", n_cols=MAIN_W)
    dw_fg = _mm(xt, dproj, out_dtype=BF16, tm=1024, tn=LANES, tk=2048, name="mm_dw_fg", n_off=MAIN_W, n_cols=LANES)
    return (dw_main, dw_fg), (dproj, dz)


def _layer_bwd_input(ctx, wl, after):
    dproj, dz = ctx
    return _mm(dproj, wl["w_in"], out_dtype=F32, tm=512, tn=512, tk=ALL_W, name="mm_dx", c=dz, c_scale=ALPHA, bt=True,
               after=after)


def kernel(x, w_in, b_f, conv_w, pool_w, pool_scale, w_out, ln_g, ln_b, loss_target, m_w_in, m_b_f, m_conv_w, m_pool_w, m_pool_scale, m_w_out, m_ln_g, m_ln_b, v_w_in, v_b_f, v_conv_w, v_pool_w, v_pool_scale, v_w_out, v_ln_g, v_ln_b):
    L, D, cols = w_in.shape
    rows_out = w_out.shape[1]
    ccols = conv_w.shape[2]
    mx, my, mc = _my_place()
    me = 4 * mx + 2 * my + mc

    w_in_b, w_out_b = w_in.astype(BF16), w_out.astype(BF16)
    gathers = []
    for l in range(L):
        gathers.append((_xchg_start([w_in_b[l]], gather=[True], name=f"gather_in_start_{l}"),
                        _xchg_start([w_out_b[l], conv_w[l]], gather=[True] * 2, name=f"gather_out_start_{l}")))
    started = sum(h["token"] for pair in gathers for h in pair)

    xl = x[0]
    xb, xbt = _prep(xl, started, ts=TS)
    weights, saved = [], []
    for l in range(L):
        (own_in,), (g_in,) = _xchg_wait(gathers[l][0], xb, name=f"gather_in_wait_{l}")
        w_all = _w_unpack(g_in, own_in, tr=W_TILE_ROWS)
        proj = _layer_fwd_proj(xb, w_all)
        (own_out, own_cw), (g_out, g_cw) = _xchg_wait(gathers[l][1], proj[0], name=f"gather_out_wait_{l}")
        w_o = _with_own(g_out, own_out, me).reshape(N_DEV * rows_out, D)
        cw = jnp.transpose(_with_own(g_cw, own_cw, me), (1, 0, 2)).reshape(CONV_TAPS, N_DEV * ccols)
        wl = dict(w_in=w_all, w_out=w_o, conv_w=cw,
                  b_f=jnp.pad(b_f[l][None, :], ((0, 0), (0, LANES - N_HEADS))), pool_w=pool_w[l],
                  pool_scale=pool_scale[l][None, :], ln_g=ln_g[l][None, :], ln_b=ln_b[l][None, :])
        xl, xb, xbt, sv = _layer_fwd_mix(xl, xbt, proj, wl)
        weights.append(wl)
        saved.append(sv)

    loss, dx = _loss_head(xl, loss_target[0], ts=TS)
    loss = lax.psum(loss[0, 0], ("x", "y", "c"))

    first_x = ["w_out", "conv_w", "b_f", "pool_w", "pool_scale", "ln_g", "ln_b"]
    sharded = ["w_in", "w_out", "conv_w"]
    scatters = [None] * L
    for l in reversed(range(L)):
        g, mid = _layer_bwd_mix(dx, saved[l], weights[l])
        g["w_out"] = g["w_out"].reshape(N_DEV, rows_out, D)
        g["conv_w"] = jnp.transpose(g["conv_w"].reshape(CONV_TAPS, N_DEV, ccols), (1, 0, 2))
        x_rest = _xchg_start([g[k] for k in first_x], gather=[k not in sharded for k in first_x],
                             name=f"scatter_rest_start_{l}")
        (dw_main, dw_fg), ctx = _layer_bwd_w_in(mid, saved[l])
        x_in = _xchg_start([_g_pack(dw_main, dw_fg, cols, tr=W_TILE_ROWS)], gather=[False], name=f"scatter_in_start_{l}")
        scatters[l] = (x_rest, x_in)
        dx = _layer_bwd_input(ctx, weights[l], x_in["token"])

    given = dict(w_in=(w_in, m_w_in, v_w_in), b_f=(b_f, m_b_f, v_b_f), conv_w=(conv_w, m_conv_w, v_conv_w),
                 pool_w=(pool_w, m_pool_w, v_pool_w), pool_scale=(pool_scale, m_pool_scale, v_pool_scale),
                 w_out=(w_out, m_w_out, v_w_out), ln_g=(ln_g, m_ln_g, v_ln_g), ln_b=(ln_b, m_ln_b, v_ln_b))
    sent, landed = [{} for _ in range(L)], [{} for _ in range(L)]

    def wait(l, which, after):
        ks = first_x if which == 0 else ["w_in"]
        s_, l_ = _xchg_wait(scatters[l][which], after, name=f"scatter_{'rest' if which == 0 else 'in'}_wait_{l}")
        sent[l].update(zip(ks, s_))
        landed[l].update(zip(ks, l_))

    def update(k, layers, into=None):
        w, m, v = given[k]
        C = w.shape[-1]
        w2, m2, v2 = (t.reshape(-1, C) for t in (w, m, v))
        rp = w2.shape[0] // L
        lands = [landed[l][k].reshape(N_DEV, rp, C) for l in layers]
        srcs = [sent[l][k].reshape((N_DEV, rp, C) if k in sharded else (rp, C)) for l in layers]
        return _adamw(lands, srcs, w2, m2, v2, name=f"adamw_{k}_{layers[0]}", first=layers[0], into=into)

    res = {}
    big = ["w_in", "w_out", "pool_w"]
    for l in reversed(range(1, L)):
        wait(l, 0, dx)
        wait(l, 1, dx)
    upper = {k: update(k, list(range(1, L))) for k in big} if L > 1 else {}
    wait(0, 0, upper["w_in"][0] if upper else dx)
    for k in first_x:
        res[k] = update(k, [0], into=upper[k]) if k in upper else update(k, list(range(L)))
    wait(0, 1, res["w_out"][0])
    res["w_in"] = update("w_in", [0], into=upper["w_in"]) if upper else update("w_in", [0])
    res = {k: [t.reshape(given[k][0].shape) for t in outs] for k, outs in res.items()}

    order = ["w_in", "b_f", "conv_w", "pool_w", "pool_scale", "w_out", "ln_g", "ln_b"]
    return (loss, dx[None], *[res[k][0] for k in order], *[res[k][1] for k in order],
            *[res[k][2] for k in order], *[res[k][3] for k in order])
```

```python
import jax
import jax.numpy as jnp
from jax import lax
from jax.experimental import pallas as pl
from jax.experimental.pallas import tpu as pltpu

F32 = jnp.float32
BF16 = jnp.bfloat16

N_DEV = 8
D_MODEL = 2048
N_HEADS = 8
HEAD_DIM = 128
ATT_W = N_HEADS * HEAD_DIM
CONV_W = 512
CONV_TAPS = 3
POOL_W = 512
POOL_GROUP = 128
N_POOL = POOL_W // POOL_GROUP
FG_AT = 4 * ATT_W
IN_W = FG_AT + N_HEADS + 4 * CONV_W + 2 * POOL_W
MAIN_W = IN_W - N_HEADS
LANES = 128
ALL_W = MAIN_W + LANES
LN_EPS = 1e-5
DEPTH = 4
ALPHA = (2 * DEPTH) ** 0.25
SCALE = HEAD_DIM ** -0.5
LOG2E = 1.4426950408889634
LN2 = 0.6931471805599453
NEG = -1e30

ADAM_LR, ADAM_B1, ADAM_B2, ADAM_EPS, ADAM_WD, ADAM_STEP = 0.001, 0.9, 0.999, 1e-08, 0.01, 10

QKV_W = 3 * ATT_W
REST_W = MAIN_W - QKV_W
QB, KB, VB = 0, 8, 16
GB = 0
CBB, CCB, CHB, CGB = 8, 12, 16, 20
PUB, PGB = 24, 28
FGB = MAIN_W // LANES
VMEM_LIMIT = 48 * 1024 * 1024
ADAMW_BLOCK_ELEMS = 64 * 1024

MESH = pl.DeviceIdType.MESH
ANY = pl.BlockSpec(memory_space=pl.ANY)


def _params(semantics):
    return pltpu.CompilerParams(dimension_semantics=semantics, vmem_limit_bytes=VMEM_LIMIT)


def _sigmoid(x):
    return 1.0 / (1.0 + jnp.exp(-x))


def _mm(a, b, *, out_dtype, tm, tn, tk, name, c=None, c_scale=1.0, n_off=0, n_cols=None, bt=False, at=False,
        after=None):
    M, K = a.shape[::-1] if at else a.shape
    N = b.shape[0 if bt else 1] if n_cols is None else n_cols
    assert not (bt and n_off)
    dims = (((0 if at else 1,), (1 if bt else 0,)), ((), ()))
    tm, tn, tk = min(tm, M), min(tn, N), min(tk, K)
    assert M % tm == 0 and N % tn == 0 and K % tk == 0 and n_off % tn == 0, (a.shape, b.shape, tm, tn, tk)
    nk, joff = K // tk, n_off // tn
    has_c, has_after = c is not None, after is not None

    def body(*refs):
        a_ref, b_ref = refs[0], refs[1]
        c_ref = refs[2] if has_c else None
        o_ref = refs[2 + has_c + has_after]

        def finish(r):
            if has_c:
                r = r + c_scale * c_ref[...]
            o_ref[...] = r.astype(out_dtype)

        if nk == 1:
            finish(lax.dot_general(a_ref[...], b_ref[...], dims, preferred_element_type=F32))
        else:
            acc_ref = refs[3 + has_c + has_after]
            k = pl.program_id(2)

            @pl.when(k == 0)
            def _():
                acc_ref[...] = jnp.zeros_like(acc_ref)

            acc_ref[...] += lax.dot_general(a_ref[...], b_ref[...], dims, preferred_element_type=F32)

            @pl.when(k == nk - 1)
            def _():
                finish(acc_ref[...])

    in_specs = [pl.BlockSpec((tk, tm), lambda i, j, k: (k, i)) if at else pl.BlockSpec((tm, tk), lambda i, j, k: (i, k)),
                pl.BlockSpec((tn, tk), lambda i, j, k: (j, k)) if bt
                else pl.BlockSpec((tk, tn), lambda i, j, k: (k, j + joff))]
    args = [a, b]
    if has_c:
        in_specs.append(pl.BlockSpec((tm, tn), lambda i, j, k: (i, j)))
        args.append(c)
    if has_after:
        in_specs.append(ANY)
        args.append(after)
    return pl.pallas_call(
        body, name=name, grid=(M // tm, N // tn, nk),
        in_specs=in_specs,
        out_specs=pl.BlockSpec((tm, tn), lambda i, j, k: (i, j)),
        out_shape=jax.ShapeDtypeStruct((M, N), out_dtype),
        scratch_shapes=[pltpu.VMEM((tm, tn), F32)] if nk > 1 else [],
        compiler_params=_params(("parallel", "parallel", "arbitrary")),
    )(*args)


def _shift_down(x, k, row):
    return jnp.where(row >= k, pltpu.roll(x, k, 0), 0.0)


def _shift_up(x, k, row):
    s = x.shape[0]
    return jnp.where(row < s - k, pltpu.roll(x, s - k, 0), 0.0)


def _fg_fwd(fgp, bf):
    S = fgp.shape[0]

    def body(f_ref, b_ref, cum_ref):
        a = f_ref[...] + b_ref[...]
        x = jnp.minimum(a, 0.0) - jnp.log(1.0 + jnp.exp(-jnp.abs(a)))
        row = lax.broadcasted_iota(jnp.int32, x.shape, 0)
        k = 1
        while k < S:
            x = x + _shift_down(x, k, row)
            k *= 2
        cum_ref[...] = x

    return pl.pallas_call(body, name="fg_fwd", out_shape=jax.ShapeDtypeStruct((S, LANES), F32),
                          compiler_params=_params(None))(fgp, bf)


def _fg_bwd(dcum, fgp, bf, dproj):
    S = fgp.shape[0]

    def body(d_ref, f_ref, b_ref, dproj_in, dfg_ref, dbf_ref):
        x = d_ref[...]
        row = lax.broadcasted_iota(jnp.int32, x.shape, 0)
        k = 1
        while k < S:
            x = x + _shift_up(x, k, row)
            k *= 2
        dfg = x * _sigmoid(-(f_ref[...] + b_ref[...]))
        dfg_ref[...] = dfg.astype(BF16)
        dbf_ref[...] = jnp.sum(dfg, axis=0, keepdims=True)

    whole = pl.BlockSpec((S, LANES), lambda i: (0, 0))
    vec = pl.BlockSpec((1, LANES), lambda i: (0, 0))
    return pl.pallas_call(
        body, name="fg_bwd", grid=(1,),
        in_specs=[whole, whole, vec, ANY],
        out_specs=(pl.BlockSpec((S, LANES), lambda i: (0, FGB)), vec),
        out_shape=(jax.ShapeDtypeStruct(dproj.shape, dproj.dtype), jax.ShapeDtypeStruct((1, LANES), F32)),
        input_output_aliases={3: 0},
        compiler_params=_params(("arbitrary",)),
    )(dcum, fgp, bf, dproj)


def _put_columns(pieces, dst, blocks, sems):
    copies = [pltpu.make_async_copy(p, dst.at[:, pl.ds(pl.multiple_of(b * LANES, LANES), LANES)], sems.at[i])
              for i, (p, b) in enumerate(zip(pieces, blocks))]
    for cp in copies:
        cp.start()
    for cp in copies:
        cp.wait()


def _colblock(S, off):
    return pl.BlockSpec((S, LANES), lambda h: (0, h + off))


def _head_col(S):
    return pl.BlockSpec((None, S, 1), lambda h: (h, 0, 0))


def _head_row(nq, tq):
    return pl.BlockSpec((None, nq, 1, tq), lambda h: (h, 0, 0, 0))


def _lane_of(ref, h):
    lane = lax.broadcasted_iota(jnp.int32, ref.shape, 1)
    return jnp.sum(jnp.where(lane == h, ref[...], 0.0), axis=1, keepdims=True)


def _attn_fwd(qkv, rest, cum, cumr, *, tq):
    S = qkv.shape[0]
    nq = S // tq

    def body(q_ref, k_ref, v_ref, g_ref, cum_ref, cumr_ref, o_ref, y_ref, lse_ref, cc):
        cc[...] = _lane_of(cum_ref, pl.program_id(0)) * LOG2E
        tri = (lax.broadcasted_iota(jnp.int32, (tq, tq), 1) <= lax.broadcasted_iota(jnp.int32, (tq, tq), 0))

        def q_step(qi, _):
            rows = pl.ds(pl.multiple_of(qi * tq, tq), tq)
            q = q_ref[rows, :]
            ci = cc[rows, :]

            def tile(kj, carry, masked):
                m, l, acc = carry
                cols = pl.ds(pl.multiple_of(kj * tq, tq), tq)
                s = lax.dot_general(q, k_ref[cols, :], (((1,), (1,)), ((), ())), preferred_element_type=F32)
                s = s * (SCALE * LOG2E) + ci - cumr_ref[kj]
                if masked:
                    s = jnp.where(tri, s, NEG)
                m_new = jnp.maximum(m, jnp.max(s, axis=1, keepdims=True))
                a = jnp.exp2(m - m_new)
                p = jnp.exp2(s - m_new)
                l = a * l + jnp.sum(p, axis=1, keepdims=True)
                acc = a * acc + jnp.dot(p.astype(BF16), v_ref[cols, :], preferred_element_type=F32)
                return m_new, l, acc

            init = (jnp.full((tq, 1), NEG, F32), jnp.zeros((tq, 1), F32), jnp.zeros((tq, HEAD_DIM), F32))
            carry = lax.fori_loop(0, qi, lambda kj, c: tile(kj, c, False), init)
            m, l, acc = tile(qi, carry, True)
            o = acc / l
            g = g_ref[rows, :]
            o_ref[rows, :] = o
            y_ref[rows, :] = (o * (g * _sigmoid(g))).astype(BF16)
            lse_ref[rows, :] = m * LN2 + jnp.log(l)
            return 0

        lax.fori_loop(0, nq, q_step, 0)

    return pl.pallas_call(
        body, name="attn_fwd", grid=(N_HEADS,),
        in_specs=[_colblock(S, QB), _colblock(S, KB), _colblock(S, VB), _colblock(S, GB),
                  pl.BlockSpec((S, LANES), lambda h: (0, 0)), _head_row(nq, tq)],
        out_specs=(_colblock(S, 0), _colblock(S, 0), _head_col(S)),
        out_shape=(jax.ShapeDtypeStruct((S, ATT_W), F32), jax.ShapeDtypeStruct((S, D_MODEL), BF16),
                   jax.ShapeDtypeStruct((N_HEADS, S, 1), F32)),
        scratch_shapes=[pltpu.VMEM((S, 1), F32)],
        compiler_params=_params(("arbitrary",)),
    )(qkv, qkv, qkv, rest, cum, cumr)


def _attn_bwd_pre(rest, o, dy, cum, lse):
    S = o.shape[0]

    def body(g_ref, o_ref, dy_ref, cum_ref, lse_ref, do_ref, dg_ref, a_ref, dl_ref):
        g = g_ref[...]
        sg = _sigmoid(g)
        ov = o_ref[...]
        dy = dy_ref[...]
        do = dy * (g * sg)
        do_ref[...] = do.astype(BF16)
        dg_ref[...] = (dy * ov * (sg * (1.0 + g * (1.0 - sg)))).astype(BF16)
        dl_ref[...] = jnp.sum(do * ov, axis=1, keepdims=True)
        a_ref[...] = (_lane_of(cum_ref, pl.program_id(0)) - lse_ref[...]) * LOG2E

    att = jax.ShapeDtypeStruct((S, ATT_W), BF16)
    col = jax.ShapeDtypeStruct((N_HEADS, S, 1), F32)
    return pl.pallas_call(
        body, name="attn_bwd_pre", grid=(N_HEADS,),
        in_specs=[_colblock(S, GB), _colblock(S, 0), _colblock(S, 0),
                  pl.BlockSpec((S, LANES), lambda h: (0, 0)), _head_col(S)],
        out_specs=(_colblock(S, 0), _colblock(S, QKV_W // LANES), _head_col(S), _head_col(S)),
        out_shape=(att, jax.ShapeDtypeStruct((S, ALL_W), BF16), col, col),
        compiler_params=_params(("parallel",)),
    )(rest, o, dy, cum, lse)


def _attn_bwd(qkv, do, a, delta, cumr, dproj, *, tq):
    S = qkv.shape[0]
    nq = S // tq
    tdot = (((0,), (0,)), ((), ()))

    def body(q_ref, k_ref, v_ref, do_ref, a_ref, dl_ref, cumr_ref, dproj_in, dproj_ref, dcr_ref, dcc_ref,
             dqa, dq_ref, dk_ref, dv_ref, sems):
        dqa[...] = jnp.zeros_like(dqa)

        @pl.when(pl.program_id(0) == 0)
        def _():
            dcr_ref[...] = jnp.zeros_like(dcr_ref)

        tri = (lax.broadcasted_iota(jnp.int32, (tq, tq), 1) <= lax.broadcasted_iota(jnp.int32, (tq, tq), 0))
        mine = lax.broadcasted_iota(jnp.int32, (tq, LANES), 1) == pl.program_id(0)

        def k_step(kj, _):
            cols = pl.ds(pl.multiple_of(kj * tq, tq), tq)
            k = k_ref[cols, :]
            v = v_ref[cols, :]
            cj = cumr_ref[kj]

            def tile(qi, carry, masked):
                dk, dv, csum = carry
                rows = pl.ds(pl.multiple_of(qi * tq, tq), tq)
                q = q_ref[rows, :]
                dot = do_ref[rows, :]
                s = lax.dot_general(q, k, (((1,), (1,)), ((), ())), preferred_element_type=F32)
                p = jnp.exp2(s * (SCALE * LOG2E) + a_ref[rows, :] - cj)
                if masked:
                    p = jnp.where(tri, p, 0.0)
                dv = dv + lax.dot_general(p.astype(BF16), dot, tdot, preferred_element_type=F32)
                dp = lax.dot_general(dot, v, (((1,), (1,)), ((), ())), preferred_element_type=F32)
                ds = p * (dp - dl_ref[rows, :])
                dcr_ref[rows, :] += jnp.where(mine, jnp.sum(ds, axis=1, keepdims=True), 0.0)
                csum = csum + jnp.sum(ds, axis=0, keepdims=True)
                dsb = (ds * SCALE).astype(BF16)
                dqa[rows, :] += jnp.dot(dsb, k, preferred_element_type=F32)
                dk = dk + lax.dot_general(dsb, q, tdot, preferred_element_type=F32)
                return dk, dv, csum

            init = (jnp.zeros((tq, HEAD_DIM), F32), jnp.zeros((tq, HEAD_DIM), F32), jnp.zeros((1, tq), F32))
            carry = tile(kj, init, True)
            dk, dv, csum = lax.fori_loop(kj + 1, nq, lambda qi, c: tile(qi, c, False), carry)
            dk_ref[cols, :] = dk.astype(BF16)
            dv_ref[cols, :] = dv.astype(BF16)
            dcc_ref[kj] = csum
            return 0

        lax.fori_loop(0, nq, k_step, 0)
        dq_ref[...] = dqa[...].astype(BF16)
        h = pl.program_id(0)
        _put_columns([dq_ref, dk_ref, dv_ref], dproj_ref, [QB + h, KB + h, VB + h], sems)

    head = pltpu.VMEM((S, HEAD_DIM), BF16)
    return pl.pallas_call(
        body, name="attn_bwd", grid=(N_HEADS,),
        in_specs=[_colblock(S, QB), _colblock(S, KB), _colblock(S, VB), _colblock(S, 0),
                  _head_col(S), _head_col(S), _head_row(nq, tq), ANY],
        out_specs=(ANY, pl.BlockSpec((S, LANES), lambda h: (0, 0)), _head_row(nq, tq)),
        out_shape=(jax.ShapeDtypeStruct(dproj.shape, dproj.dtype), jax.ShapeDtypeStruct((S, LANES), F32),
                   jax.ShapeDtypeStruct((N_HEADS, nq, 1, tq), F32)),
        input_output_aliases={7: 0},
        scratch_shapes=[pltpu.VMEM((S, HEAD_DIM), F32), head, head, head, pltpu.SemaphoreType.DMA((3,))],
        compiler_params=_params(("arbitrary",)),
    )(qkv, qkv, qkv, do, a, delta, cumr, dproj)


def _conv_taps(u, w_ref, row):
    return (w_ref[0:1, :] * _shift_down(u, 2, row) + w_ref[1:2, :] * _shift_down(u, 1, row)) + w_ref[2:3, :] * u


def _conv_fwd(rest, cw, y):
    S = rest.shape[0]

    def body(cb_ref, cc_ref, ch_ref, g_ref, w_ref, y_in, y_ref):
        row = lax.broadcasted_iota(jnp.int32, (S, LANES), 0)
        g = g_ref[...]
        y = _conv_taps(cc_ref[...] * ch_ref[...], w_ref, row)
        y_ref[...] = ((cb_ref[...] * y) * (g * _sigmoid(g))).astype(BF16)

    return pl.pallas_call(
        body, name="conv_fwd", grid=(CONV_W // LANES,),
        in_specs=[_colblock(S, CBB), _colblock(S, CCB), _colblock(S, CHB), _colblock(S, CGB),
                  pl.BlockSpec((CONV_TAPS, LANES), lambda j: (0, j)), ANY],
        out_specs=_colblock(S, ATT_W // LANES),
        out_shape=jax.ShapeDtypeStruct(y.shape, y.dtype),
        input_output_aliases={5: 0},
        compiler_params=_params(("parallel",)),
    )(rest, rest, rest, rest, cw, y)


def _conv_bwd(rest, dy, cw, dproj):
    S = rest.shape[0]

    def body(cb_ref, cc_ref, ch_ref, g_ref, dy_ref, w_ref, dproj_in, dproj_ref, dw_ref,
             dcb_ref, dcc_ref, dch_ref, dg_ref, sems):
        row = lax.broadcasted_iota(jnp.int32, (S, LANES), 0)
        g = g_ref[...]
        sg = _sigmoid(g)
        silu = g * sg
        cb, cc, ch, dy = cb_ref[...], cc_ref[...], ch_ref[...], dy_ref[...]
        u = cc * ch
        u1 = _shift_down(u, 1, row)
        u2 = _shift_down(u, 2, row)
        y = (w_ref[0:1, :] * u2 + w_ref[1:2, :] * u1) + w_ref[2:3, :] * u
        dcb_ref[...] = (dy * silu * y).astype(BF16)
        dg_ref[...] = (dy * (cb * y) * (sg * (1.0 + g * (1.0 - sg)))).astype(BF16)
        dyv = dy * silu * cb
        du = w_ref[2:3, :] * dyv + w_ref[1:2, :] * _shift_up(dyv, 1, row) + w_ref[0:1, :] * _shift_up(dyv, 2, row)
        dcc_ref[...] = (du * ch).astype(BF16)
        dch_ref[...] = (du * cc).astype(BF16)
        dw_ref[0:1, :] = jnp.sum(dyv * u2, axis=0, keepdims=True)
        dw_ref[1:2, :] = jnp.sum(dyv * u1, axis=0, keepdims=True)
        dw_ref[2:3, :] = jnp.sum(dyv * u, axis=0, keepdims=True)
        j = pl.program_id(0)
        first = QKV_W // LANES
        _put_columns([dcb_ref, dcc_ref, dch_ref, dg_ref], dproj_ref,
                     [first + CBB + j, first + CCB + j, first + CHB + j, first + CGB + j], sems)

    piece = pltpu.VMEM((S, LANES), BF16)
    return pl.pallas_call(
        body, name="conv_bwd", grid=(CONV_W // LANES,),
        in_specs=[_colblock(S, CBB), _colblock(S, CCB), _colblock(S, CHB), _colblock(S, CGB),
                  _colblock(S, ATT_W // LANES), pl.BlockSpec((CONV_TAPS, LANES), lambda j: (0, j)), ANY],
        out_specs=(ANY, pl.BlockSpec((CONV_TAPS, LANES), lambda j: (0, j))),
        out_shape=(jax.ShapeDtypeStruct(dproj.shape, dproj.dtype), jax.ShapeDtypeStruct((CONV_TAPS, CONV_W), F32)),
        input_output_aliases={6: 0},
        scratch_shapes=[piece] * 4 + [pltpu.SemaphoreType.DMA((4,))],
        compiler_params=_params(("arbitrary",)),
    )(rest, rest, rest, rest, dy, cw, dproj)


def _pool_z(u, grp, row):
    s2 = u + _shift_down(u, 1, row)
    s4 = s2 + _shift_down(s2, 2, row)
    s8 = s4 + _shift_down(s4, 4, row)
    s16 = s8 + _shift_down(s8, 8, row)
    sw = jnp.where(grp == 0, s2, jnp.where(grp == 1, s4, jnp.where(grp == 2, s8, s16)))
    return sw / _pool_count(grp, row) - u


def _pool_count(grp, row):
    return jnp.minimum(row + 1, 2 << grp).astype(F32)


def _pool_fwd(rest, pw, scale, y):
    S = rest.shape[0]

    def body(u_ref, g_ref, pw_ref, sc_ref, y_in, y_ref):
        grp = pl.program_id(0)
        row = lax.broadcasted_iota(jnp.int32, (S, LANES), 0)
        z = _pool_z(u_ref[...], grp, row)
        zp = jnp.dot(z.astype(BF16), pw_ref[...].astype(BF16), preferred_element_type=F32)
        g = g_ref[...]
        y_ref[...] = ((zp * sc_ref[...]) * (g * _sigmoid(g))).astype(BF16)

    return pl.pallas_call(
        body, name="pool_fwd", grid=(N_POOL,),
        in_specs=[_colblock(S, PUB), _colblock(S, PGB),
                  pl.BlockSpec((None, POOL_GROUP, POOL_GROUP), lambda j: (j, 0, 0)),
                  pl.BlockSpec((1, LANES), lambda j: (0, j)), ANY],
        out_specs=_colblock(S, (ATT_W + CONV_W) // LANES),
        out_shape=jax.ShapeDtypeStruct(y.shape, y.dtype),
        input_output_aliases={4: 0},
        compiler_params=_params(("parallel",)),
    )(rest, rest, pw, scale, y)


def _pool_bwd(rest, dy, pw, scale, dproj):
    S = rest.shape[0]

    def body(u_ref, g_ref, dy_ref, pw_ref, sc_ref, dproj_in, dproj_ref, dpw_ref, dsc_ref, du_ref, dg_ref, sems):
        grp = pl.program_id(0)
        row = lax.broadcasted_iota(jnp.int32, (S, LANES), 0)
        z = _pool_z(u_ref[...], grp, row).astype(BF16)
        pwb = pw_ref[...].astype(BF16)
        zp = jnp.dot(z, pwb, preferred_element_type=F32)
        g = g_ref[...]
        sg = _sigmoid(g)
        silu = g * sg
        dy = dy_ref[...]
        sc = sc_ref[...]
        dsc_ref[...] = jnp.sum(dy * silu * zp, axis=0, keepdims=True)
        dg_ref[...] = (dy * (zp * sc) * (sg * (1.0 + g * (1.0 - sg)))).astype(BF16)
        dzp = (dy * silu * sc).astype(BF16)
        dpw_ref[...] = lax.dot_general(z, dzp, (((0,), (0,)), ((), ())), preferred_element_type=F32)
        dz = lax.dot_general(dzp, pwb, (((1,), (1,)), ((), ())), preferred_element_type=F32)
        f1 = dz / _pool_count(grp, row)
        f2 = f1 + _shift_up(f1, 1, row)
        f4 = f2 + _shift_up(f2, 2, row)
        f8 = f4 + _shift_up(f4, 4, row)
        f16 = f8 + _shift_up(f8, 8, row)
        fw = jnp.where(grp == 0, f2, jnp.where(grp == 1, f4, jnp.where(grp == 2, f8, f16)))
        du_ref[...] = (fw - dz).astype(BF16)
        first = QKV_W // LANES
        _put_columns([du_ref, dg_ref], dproj_ref, [first + PUB + grp, first + PGB + grp], sems)

    piece = pltpu.VMEM((S, LANES), BF16)
    return pl.pallas_call(
        body, name="pool_bwd", grid=(N_POOL,),
        in_specs=[_colblock(S, PUB), _colblock(S, PGB), _colblock(S, (ATT_W + CONV_W) // LANES),
                  pl.BlockSpec((None, POOL_GROUP, POOL_GROUP), lambda j: (j, 0, 0)),
                  pl.BlockSpec((1, LANES), lambda j: (0, j)), ANY],
        out_specs=(ANY, pl.BlockSpec((None, POOL_GROUP, POOL_GROUP), lambda j: (j, 0, 0)),
                   pl.BlockSpec((1, LANES), lambda j: (0, j))),
        out_shape=(jax.ShapeDtypeStruct(dproj.shape, dproj.dtype),
                   jax.ShapeDtypeStruct((N_POOL, POOL_GROUP, POOL_GROUP), F32), jax.ShapeDtypeStruct((1, POOL_W), F32)),
        input_output_aliases={5: 0},
        scratch_shapes=[piece] * 2 + [pltpu.SemaphoreType.DMA((2,))],
        compiler_params=_params(("arbitrary",)),
    )(rest, rest, dy, pw, scale, dproj)


def _ln_fwd(z, g, b, *, ts):
    S, D = z.shape

    def body(z_ref, g_ref, b_ref, o_ref, ob_ref, obt_ref, xh_ref, rs_ref):
        zz = z_ref[...]
        mu = jnp.mean(zz, axis=1, keepdims=True)
        zc = zz - mu
        rstd = lax.rsqrt(jnp.mean(zc * zc, axis=1, keepdims=True) + LN_EPS)
        xh = zc * rstd
        out = xh * g_ref[...] + b_ref[...]
        o_ref[...] = out
        ob_ref[...] = out.astype(BF16)
        obt_ref[...] = out.T.astype(BF16)
        xh_ref[...] = xh
        rs_ref[...] = rstd

    rowblk = pl.BlockSpec((ts, D), lambda i: (i, 0))
    vec = pl.BlockSpec((1, D), lambda i: (0, 0))
    return pl.pallas_call(
        body, name="ln_fwd", grid=(S // ts,),
        in_specs=[rowblk, vec, vec],
        out_specs=(rowblk, rowblk, pl.BlockSpec((D, ts), lambda i: (0, i)), rowblk,
                   pl.BlockSpec((ts, 1), lambda i: (i, 0))),
        out_shape=(jax.ShapeDtypeStruct((S, D), F32), jax.ShapeDtypeStruct((S, D), BF16),
                   jax.ShapeDtypeStruct((D, S), BF16), jax.ShapeDtypeStruct((S, D), F32),
                   jax.ShapeDtypeStruct((S, 1), F32)),
        compiler_params=_params(("parallel",)),
    )(z, g, b)


def _prep(x, after, *, ts):
    S, D = x.shape

    def body(x_ref, after_ref, xb_ref, xbt_ref):
        xb_ref[...] = x_ref[...].astype(BF16)
        xbt_ref[...] = x_ref[...].T.astype(BF16)

    return pl.pallas_call(
        body, name="prep", grid=(S // ts,),
        in_specs=[pl.BlockSpec((ts, D), lambda i: (i, 0)), ANY],
        out_specs=(pl.BlockSpec((ts, D), lambda i: (i, 0)), pl.BlockSpec((D, ts), lambda i: (0, i))),
        out_shape=(jax.ShapeDtypeStruct((S, D), BF16), jax.ShapeDtypeStruct((D, S), BF16)),
        compiler_params=_params(("parallel",)),
    )(x, after)


def _ln_bwd(dout, xhat, rstd, g, *, ts):
    S, D = dout.shape

    def body(d_ref, xh_ref, rs_ref, g_ref, dz_ref, dzb_ref, dg_ref, db_ref):
        @pl.when(pl.program_id(0) == 0)
        def _():
            dg_ref[...] = jnp.zeros_like(dg_ref)
            db_ref[...] = jnp.zeros_like(db_ref)

        d = d_ref[...]
        xh = xh_ref[...]
        dxh = d * g_ref[...]
        m1 = jnp.mean(dxh, axis=1, keepdims=True)
        m2 = jnp.mean(dxh * xh, axis=1, keepdims=True)
        dz = rs_ref[...] * (dxh - m1 - xh * m2)
        dz_ref[...] = dz
        dzb_ref[...] = dz.astype(BF16)
        dg_ref[...] += jnp.sum(d * xh, axis=0, keepdims=True)
        db_ref[...] += jnp.sum(d, axis=0, keepdims=True)

    rowblk = pl.BlockSpec((ts, D), lambda i: (i, 0))
    vec = pl.BlockSpec((1, D), lambda i: (0, 0))
    return pl.pallas_call(
        body, name="ln_bwd", grid=(S // ts,),
        in_specs=[rowblk, rowblk, pl.BlockSpec((ts, 1), lambda i: (i, 0)), vec],
        out_specs=(rowblk, rowblk, vec, vec),
        out_shape=(jax.ShapeDtypeStruct((S, D), F32), jax.ShapeDtypeStruct((S, D), BF16),
                   jax.ShapeDtypeStruct((1, D), F32), jax.ShapeDtypeStruct((1, D), F32)),
        compiler_params=_params(("arbitrary",)),
    )(dout, xhat, rstd, g)


def _loss_head(y, tgt, *, ts):
    S, D = y.shape

    def body(y_ref, t_ref, l_ref, dy_ref):
        @pl.when(pl.program_id(0) == 0)
        def _():
            l_ref[...] = jnp.zeros_like(l_ref)

        e = y_ref[...] - t_ref[...]
        dy_ref[...] = e * (1.0 / D)
        rowloss = jnp.sum(e * e, axis=1, keepdims=True) * (0.5 / D)
        l_ref[...] += jnp.sum(rowloss, axis=0, keepdims=True)

    rowblk = pl.BlockSpec((ts, D), lambda i: (i, 0))
    return pl.pallas_call(
        body, name="loss_head", grid=(S // ts,),
        in_specs=[rowblk, rowblk],
        out_specs=(pl.BlockSpec((1, 1), lambda i: (0, 0)), rowblk),
        out_shape=(jax.ShapeDtypeStruct((1, 1), F32), jax.ShapeDtypeStruct((S, D), F32)),
        compiler_params=_params(("arbitrary",)),
    )(y, tgt)


def _adamw(lands, srcs, w, m, v, *, name, first=0, into=None):
    R, C = w.shape
    npart = len(lands)
    rp = lands[0].shape[1]
    assert R % rp == 0 and all(p.shape == (N_DEV, rp, C) for p in lands)
    tiled = rp % 8 == 0
    assert tiled or (first == 0 and npart * rp == R and into is None)
    tr = min(ADAMW_BLOCK_ELEMS // (LANES * pl.cdiv(C, LANES)), rp) if tiled else rp
    assert rp % tr == 0
    steps = rp // tr
    c1 = 1.0 - ADAM_B1 ** ADAM_STEP
    c2 = 1.0 - ADAM_B2 ** ADAM_STEP
    slabbed = [p.ndim == 3 for p in srcs]
    n_into = 0 if into is None else 4

    def body(*refs):
        l_refs, s_refs = refs[:npart], refs[npart:2 * npart]
        w_ref, m_ref, v_ref = refs[2 * npart:2 * npart + 3]
        g_ref, d_ref, nm_ref, nv_ref = refs[2 * npart + 3 + n_into:]
        x, y, c = _my_place()
        me = 4 * x + 2 * y + c

        def update(l, rows):
            g = None
            for j in range(N_DEV):
                own = s_refs[l][j] if slabbed[l] else s_refs[l][...]
                p = jnp.where(me == j, own, l_refs[l][j]).astype(F32)
                g = p if g is None else g + p
            nm = ADAM_B1 * m_ref[rows, :] + (1.0 - ADAM_B1) * g
            nv = ADAM_B2 * v_ref[rows, :] + (1.0 - ADAM_B2) * (g * g)
            g_ref[rows, :] = g
            nm_ref[rows, :] = nm
            nv_ref[rows, :] = nv
            d_ref[rows, :] = -ADAM_LR * ((nm / c1) / (jnp.sqrt(nv / c2) + ADAM_EPS) + ADAM_WD * w_ref[rows, :])

        for l in range(npart):
            if tiled:
                pl.when(pl.program_id(0) == l)(lambda l=l: update(l, slice(None)))
            else:
                update(l, slice(l * rp, (l + 1) * rp))

    def part_spec(l, p):
        if p.ndim == 3:
            return pl.BlockSpec((N_DEV, tr, C), lambda q, i: (0, jnp.where(q == l, i, 0), 0))
        return pl.BlockSpec((tr, C), lambda q, i: (jnp.where(q == l, i, 0), 0))

    if tiled:
        blk = pl.BlockSpec((tr, C), lambda q, i: ((first + q) * steps + i, 0))
    else:
        blk = pl.BlockSpec((R, C), lambda q, i: (0, 0))
    out = jax.ShapeDtypeStruct((R, C), F32)
    n_in = 2 * npart + 3
    return pl.pallas_call(
        body, name=name, grid=(npart, steps) if tiled else (1, 1),
        in_specs=[part_spec(l, p) for l, p in enumerate(lands)] + [part_spec(l, p) for l, p in enumerate(srcs)]
        + [blk, blk, blk] + [ANY] * n_into,
        out_specs=(blk, blk, blk, blk),
        out_shape=(out, out, out, out),
        input_output_aliases={n_in + i: i for i in range(n_into)},
        compiler_params=_params(("arbitrary", "arbitrary")),
    )(*lands, *srcs, w, m, v, *(into or ()))


def _my_place():
    return lax.axis_index("x"), lax.axis_index("y"), lax.axis_index("c")


HBM = pl.BlockSpec(memory_space=pltpu.HBM)
SEM = pl.BlockSpec(memory_space=pltpu.SEMAPHORE)
DATAFLOW = pltpu.SideEffectType.DATAFLOW_SIDE_EFFECTING
N_PEER = N_DEV - 1


def _xchg_copies(src, land, send_sem, recv_sem, gather):
    x, y, c = _my_place()
    me = 4 * x + 2 * y + c
    copies = []
    for k in range(N_PEER):
        px, py, pc = x ^ (((k + 1) >> 2) & 1), y ^ (((k + 1) >> 1) & 1), c ^ ((k + 1) & 1)
        peer = 4 * px + 2 * py + pc
        for a in range(len(src)):
            i = a * N_PEER + k
            out = src[a] if gather[a] else src[a].at[peer]
            sent, landed = [pltpu.make_async_remote_copy(
                src_ref=out, dst_ref=land[a].at[slab], send_sem=send_sem.at[i], recv_sem=recv_sem.at[i],
                device_id=(px, py, pc), device_id_type=MESH) for slab in (me, peer)]
            copies.append((sent, landed))
    return copies


def _xchg_start(srcs, *, gather, name):
    n = len(srcs)
    lands = [lax.empty(((N_DEV,) + v.shape) if g else v.shape, v.dtype) for v, g in zip(srcs, gather)]

    def body(*refs):
        src, land = refs[:n], refs[n:2 * n]
        send_sem, recv_sem = refs[2 * n], refs[2 * n + 1]
        token = refs[-1]
        for sent, _ in _xchg_copies(src, land, send_sem, recv_sem, gather):
            sent.start()
        token[...] = jnp.zeros_like(token)

    sems = pltpu.SemaphoreType.DMA((n * N_PEER,))
    outs = pl.pallas_call(
        body, name=name,
        out_shape=(sems, sems, *[pltpu.HBM(v.shape, v.dtype) for v in srcs + lands],
                   jax.ShapeDtypeStruct((8, LANES), F32)),
        in_specs=[HBM] * (2 * n),
        out_specs=(SEM, SEM, *[HBM] * (2 * n), pl.BlockSpec(memory_space=pltpu.VMEM)),
        input_output_aliases={i: 2 + i for i in range(2 * n)},
        compiler_params=pltpu.CompilerParams(has_side_effects=DATAFLOW),
    )(*[pltpu.with_memory_space_constraint(v, pltpu.HBM) for v in srcs + lands])
    return dict(send=outs[0], recv=outs[1], srcs=list(outs[2:2 + n]), lands=list(outs[2 + n:2 + 2 * n]),
                token=outs[-1], gather=gather)


def _xchg_wait(h, after, *, name):
    n = len(h["srcs"])
    gather = h["gather"]

    def body(*refs):
        src, land = refs[:n], refs[n:2 * n]
        send_sem, recv_sem = refs[2 * n], refs[2 * n + 1]
        for sent, landed in _xchg_copies(src, land, send_sem, recv_sem, gather):
            sent.wait_send()
            landed.wait_recv()

    thru = h["srcs"] + h["lands"]
    outs = pl.pallas_call(
        body, name=name,
        out_shape=[pltpu.HBM(v.shape, v.dtype) for v in thru],
        in_specs=[HBM] * (2 * n) + [SEM, SEM, ANY],
        out_specs=[HBM] * (2 * n),
        input_output_aliases={i: i for i in range(2 * n)},
        compiler_params=pltpu.CompilerParams(has_side_effects=DATAFLOW),
    )(*thru, h["send"], h["recv"], after)
    return list(outs[:n]), list(outs[n:])


def _with_own(land, own, me):
    slot = lax.broadcasted_iota(jnp.int32, land.shape, 0)
    return jnp.where(slot == me, own if own.ndim == land.ndim else own[None], land)


def _w_in_segments(j, cols):
    lo, hi = j * cols, (j + 1) * cols
    segs = []
    for a, b, where, shift in ((0, FG_AT, "main", 0), (FG_AT, FG_AT + N_HEADS, "fg", -FG_AT),
                               (FG_AT + N_HEADS, IN_W, "main", -N_HEADS)):
        s0, s1 = max(lo, a), min(hi, b)
        if s0 < s1:
            segs.append((s0 - lo, s1 - s0, where, s0 + shift))
    return segs


SLAB_W = 1024


def _window(off, width):
    base = min((off // LANES) * LANES, width - SLAB_W)
    return base, off - base


def _w_unpack(land, own, *, tr):
    _, D, cols = land.shape
    assert cols + LANES - 1 <= SLAB_W

    def body(land_ref, own_ref, main_ref, slab):
        x, y, c = _my_place()
        me = 4 * x + 2 * y + c
        lane = lax.broadcasted_iota(jnp.int32, (tr, SLAB_W), 1)
        main_ref[...] = jnp.zeros_like(main_ref)
        slab[...] = jnp.zeros_like(slab)
        for j in range(N_DEV):
            slab[:, :cols] = jnp.where(me == j, own_ref[...], land_ref[j])
            v = slab[...]
            segs = _w_in_segments(j, cols)
            for lo, n, where, dst in segs:
                part = v if len(segs) == 1 else jnp.where((lane >= lo) & (lane < lo + n), v, jnp.zeros_like(v))
                if where == "fg":
                    main_ref[:, MAIN_W:] = pltpu.roll(part, (SLAB_W - lo) % SLAB_W, 1)[:, :LANES]
                else:
                    base, r = _window(dst - lo, MAIN_W)
                    assert 0 <= r and lo + n + r <= SLAB_W
                    main_ref[:, base:base + SLAB_W] += pltpu.roll(part, r, 1)

    return pl.pallas_call(
        body, name="w_unpack", grid=(D // tr,),
        in_specs=[pl.BlockSpec((N_DEV, tr, cols), lambda i: (0, i, 0)), pl.BlockSpec((tr, cols), lambda i: (i, 0))],
        out_specs=pl.BlockSpec((tr, ALL_W), lambda i: (i, 0)),
        out_shape=jax.ShapeDtypeStruct((D, ALL_W), BF16),
        scratch_shapes=[pltpu.VMEM((tr, SLAB_W), BF16)],
        compiler_params=_params(("parallel",)),
    )(land, own)


def _g_pack(dw_main, dw_fg, cols, *, tr):
    D = dw_main.shape[0]

    def body(main_ref, fg_ref, out_ref, slab):
        lane = lax.broadcasted_iota(jnp.int32, (tr, SLAB_W), 1)
        slab[...] = jnp.zeros_like(slab)
        slab[:, :LANES] = fg_ref[...]
        fg = slab[...]
        for j in range(N_DEV):
            v = None
            for lo, n, where, src in _w_in_segments(j, cols):
                if where == "fg":
                    part = pltpu.roll(fg, lo, 1)
                else:
                    base, r = _window(src - lo, MAIN_W)
                    assert 0 <= r and lo + n + r <= SLAB_W
                    part = pltpu.roll(main_ref[:, base:base + SLAB_W], (SLAB_W - r) % SLAB_W, 1)
                v = part if v is None else jnp.where((lane >= lo) & (lane < lo + n), part, v)
            out_ref[j] = v[:, :cols]

    return pl.pallas_call(
        body, name="g_pack", grid=(D // tr,),
        in_specs=[pl.BlockSpec((tr, MAIN_W), lambda i: (i, 0)), pl.BlockSpec((tr, LANES), lambda i: (i, 0))],
        out_specs=pl.BlockSpec((N_DEV, tr, cols), lambda i: (0, i, 0)),
        out_shape=jax.ShapeDtypeStruct((N_DEV, D, cols), BF16),
        scratch_shapes=[pltpu.VMEM((tr, SLAB_W), BF16)],
        compiler_params=_params(("parallel",)),
    )(dw_main, dw_fg)


TQ = 512
TS = 256
W_TILE_ROWS = 128


def _layer_fwd_proj(xb, w_in):
    qkv = _mm(xb, w_in, out_dtype=BF16, tm=1024, tn=512, tk=D_MODEL, name="mm_qkv", n_cols=QKV_W)
    rest = _mm(xb, w_in, out_dtype=F32, tm=1024, tn=512, tk=D_MODEL, name="mm_rest", n_off=QKV_W, n_cols=REST_W)
    fgp = _mm(xb, w_in, out_dtype=F32, tm=1024, tn=LANES, tk=D_MODEL, name="mm_fg", n_off=MAIN_W, n_cols=LANES)
    return qkv, rest, fgp


def _layer_fwd_mix(x, xbt, proj, wl):
    qkv, rest, fgp = proj
    cum = _fg_fwd(fgp, wl["b_f"])
    S = x.shape[0]
    cumr = (cum[:, :N_HEADS].T * LOG2E).reshape(N_HEADS, S // TQ, 1, TQ)
    o, y, lse = _attn_fwd(qkv, rest, cum, cumr, tq=TQ)
    y = _conv_fwd(rest, wl["conv_w"], y)
    y = _pool_fwd(rest, wl["pool_w"], wl["pool_scale"], y)
    z = _mm(y, wl["w_out"], out_dtype=F32, tm=1024, tn=512, tk=D_MODEL, name="mm_out", c=x, c_scale=ALPHA)
    out, outb, outbt, xhat, rstd = _ln_fwd(z, wl["ln_g"], wl["ln_b"], ts=TS)
    saved = dict(xbt=xbt, qkv=qkv, rest=rest, fgp=fgp, cum=cum, cumr=cumr, o=o, lse=lse, y=y, xhat=xhat, rstd=rstd)
    return out, outb, outbt, saved


def _layer_bwd_mix(dout, sv, wl):
    S = dout.shape[0]
    dz, dzb, dln_g, dln_b = _ln_bwd(dout, sv["xhat"], sv["rstd"], wl["ln_g"], ts=TS)
    dy = _mm(dzb, wl["w_out"], out_dtype=F32, tm=1024, tn=512, tk=D_MODEL, name="mm_dy", bt=True)
    dw_out = _mm(sv["y"], dzb, out_dtype=BF16, tm=512, tn=D_MODEL, tk=2048, name="mm_dw_out", at=True)
    rest = sv["rest"]
    do, dproj, a, delta = _attn_bwd_pre(rest, sv["o"], dy, sv["cum"], sv["lse"])
    dproj, dcr, dcc = _attn_bwd(sv["qkv"], do, a, delta, sv["cumr"], dproj, tq=TQ)
    dcum = dcr - jnp.pad(dcc.reshape(N_HEADS, S).T, ((0, 0), (0, LANES - N_HEADS)))
    dproj, db_f = _fg_bwd(dcum, sv["fgp"], wl["b_f"], dproj)
    dproj, dconv_w = _conv_bwd(rest, dy, wl["conv_w"], dproj)
    dproj, dpool_w, dpool_scale = _pool_bwd(rest, dy, wl["pool_w"], wl["pool_scale"], dproj)
    grads = dict(b_f=db_f[:, :N_HEADS], conv_w=dconv_w, pool_w=dpool_w, pool_scale=dpool_scale, w_out=dw_out,
                 ln_g=dln_g, ln_b=dln_b)
    return grads, (dproj, dz)


def _layer_bwd_w_in(mid, sv):
    dproj, dz = mid
    xt = sv["xbt"]
    S = xt.shape[1]
    dw_main = _mm(xt, dproj, out_dtype=BF16, tm=1024, tn=512, tk=S, name="mm_dw_main", n_cols=MAIN_W)
    dw_fg = _mm(xt, dproj, out_dtype=BF16, tm=1024, tn=LANES, tk=2048, name="mm_dw_fg", n_off=MAIN_W, n_cols=LANES)
    return (dw_main, dw_fg), (dproj, dz)


def _layer_bwd_input(ctx, wl, after):
    dproj, dz = ctx
    return _mm(dproj, wl["w_in"], out_dtype=F32, tm=512, tn=512, tk=ALL_W, name="mm_dx", c=dz, c_scale=ALPHA, bt=True,
               after=after)


def kernel(x, w_in, b_f, conv_w, pool_w, pool_scale, w_out, ln_g, ln_b, loss_target, m_w_in, m_b_f, m_conv_w, m_pool_w, m_pool_scale, m_w_out, m_ln_g, m_ln_b, v_w_in, v_b_f, v_conv_w, v_pool_w, v_pool_scale, v_w_out, v_ln_g, v_ln_b):
    L, D, cols = w_in.shape
    rows_out = w_out.shape[1]
    ccols = conv_w.shape[2]
    mx, my, mc = _my_place()
    me = 4 * mx + 2 * my + mc

    w_in_b, w_out_b = w_in.astype(BF16), w_out.astype(BF16)
    gathers = [_xchg_start([w_in_b[l], w_out_b[l], conv_w[l]], gather=[True] * 3, name=f"gather_start_{l}")
               for l in range(L)]
    started = sum(h["token"] for h in gathers)

    xl = x[0]
    xb, xbt = _prep(xl, started, ts=TS)
    weights, saved = [], []
    for l in range(L):
        (own_in, own_out, own_cw), (g_in, g_out, g_cw) = _xchg_wait(gathers[l], xb, name=f"gather_wait_{l}")
        w_all = _w_unpack(g_in, own_in, tr=W_TILE_ROWS)
        proj = _layer_fwd_proj(xb, w_all)
        w_o = _with_own(g_out, own_out, me).reshape(N_DEV * rows_out, D)
        cw = jnp.transpose(_with_own(g_cw, own_cw, me), (1, 0, 2)).reshape(CONV_TAPS, N_DEV * ccols)
        wl = dict(w_in=w_all, w_out=w_o, conv_w=cw,
                  b_f=jnp.pad(b_f[l][None, :], ((0, 0), (0, LANES - N_HEADS))), pool_w=pool_w[l],
                  pool_scale=pool_scale[l][None, :], ln_g=ln_g[l][None, :], ln_b=ln_b[l][None, :])
        xl, xb, xbt, sv = _layer_fwd_mix(xl, xbt, proj, wl)
        weights.append(wl)
        saved.append(sv)

    loss, dx = _loss_head(xl, loss_target[0], ts=TS)
    loss = lax.psum(loss[0, 0], ("x", "y", "c"))

    names = ["w_in", "w_out", "conv_w", "b_f", "pool_w", "pool_scale", "ln_g", "ln_b"]
    sharded = names[:3]
    scatters = [None] * L
    for l in reversed(range(L)):
        g, mid = _layer_bwd_mix(dx, saved[l], weights[l])
        (dw_main, dw_fg), ctx = _layer_bwd_w_in(mid, saved[l])
        g["w_in"] = _g_pack(dw_main, dw_fg, cols, tr=W_TILE_ROWS)
        g["w_out"] = g["w_out"].reshape(N_DEV, rows_out, D)
        g["conv_w"] = jnp.transpose(g["conv_w"].reshape(CONV_TAPS, N_DEV, ccols), (1, 0, 2))
        scatters[l] = _xchg_start([g[k] for k in names], gather=[k not in sharded for k in names],
                                  name=f"scatter_start_{l}")
        dx = _layer_bwd_input(ctx, weights[l], scatters[l]["token"])

    given = dict(w_in=(w_in, m_w_in, v_w_in), b_f=(b_f, m_b_f, v_b_f), conv_w=(conv_w, m_conv_w, v_conv_w),
                 pool_w=(pool_w, m_pool_w, v_pool_w), pool_scale=(pool_scale, m_pool_scale, v_pool_scale),
                 w_out=(w_out, m_w_out, v_w_out), ln_g=(ln_g, m_ln_g, v_ln_g), ln_b=(ln_b, m_ln_b, v_ln_b))
    sent, landed = [{} for _ in range(L)], [{} for _ in range(L)]

    def wait(l, after):
        s_, l_ = _xchg_wait(scatters[l], after, name=f"scatter_wait_{l}")
        sent[l].update(zip(names, s_))
        landed[l].update(zip(names, l_))

    def update(k, layers, into=None):
        w, m, v = given[k]
        C = w.shape[-1]
        w2, m2, v2 = (t.reshape(-1, C) for t in (w, m, v))
        rp = w2.shape[0] // L
        lands = [landed[l][k].reshape(N_DEV, rp, C) for l in layers]
        srcs = [sent[l][k].reshape((N_DEV, rp, C) if k in sharded else (rp, C)) for l in layers]
        return _adamw(lands, srcs, w2, m2, v2, name=f"adamw_{k}_{layers[0]}", first=layers[0], into=into)

    big = ["w_in", "w_out", "pool_w"]
    for l in reversed(range(1, L)):
        wait(l, dx)
    upper = {k: update(k, list(range(1, L))) for k in big} if L > 1 else {}
    wait(0, upper["w_in"][0] if upper else dx)
    res = {k: update(k, [0], into=upper[k]) if k in upper else update(k, list(range(L))) for k in names}
    res = {k: [t.reshape(given[k][0].shape) for t in outs] for k, outs in res.items()}

    order = ["w_in", "b_f", "conv_w", "pool_w", "pool_scale", "w_out", "ln_g", "ln_b"]
    return (loss, dx[None], *[res[k][0] for k in order], *[res[k][1] for k in order],
            *[res[k][2] for k in order], *[res[k][3] for k in order])
```

```python
import jax
import jax.numpy as jnp
from jax import lax
from jax.experimental import pallas as pl
from jax.experimental.pallas import tpu as pltpu

F32 = jnp.float32
BF16 = jnp.bfloat16

N_DEV = 8
D_MODEL = 2048
N_HEADS = 8
HEAD_DIM = 128
ATT_W = N_HEADS * HEAD_DIM
CONV_W = 512
CONV_TAPS = 3
POOL_W = 512
POOL_GROUP = 128
N_POOL = POOL_W // POOL_GROUP
FG_AT = 4 * ATT_W
IN_W = FG_AT + N_HEADS + 4 * CONV_W + 2 * POOL_W
MAIN_W = IN_W - N_HEADS
LANES = 128
ALL_W = MAIN_W + LANES
LN_EPS = 1e-5
DEPTH = 4
ALPHA = (2 * DEPTH) ** 0.25
SCALE = HEAD_DIM ** -0.5
LOG2E = 1.4426950408889634
LN2 = 0.6931471805599453
NEG = -1e30

ADAM_LR, ADAM_B1, ADAM_B2, ADAM_EPS, ADAM_WD, ADAM_STEP = 0.001, 0.9, 0.999, 1e-08, 0.01, 10

QKV_W = 3 * ATT_W
REST_W = MAIN_W - QKV_W
QB, KB, VB = 0, 8, 16
GB = 0
CBB, CCB, CHB, CGB = 8, 12, 16, 20
PUB, PGB = 24, 28
FGB = MAIN_W // LANES
VMEM_LIMIT = 48 * 1024 * 1024
ADAMW_BLOCK_ELEMS = 64 * 1024

MESH = pl.DeviceIdType.MESH
ANY = pl.BlockSpec(memory_space=pl.ANY)


def _params(semantics):
    return pltpu.CompilerParams(dimension_semantics=semantics, vmem_limit_bytes=VMEM_LIMIT)


def _sigmoid(x):
    return 1.0 / (1.0 + jnp.exp(-x))


def _mm(a, b, *, out_dtype, tm, tn, tk, name, c=None, c_scale=1.0, n_off=0, n_cols=None, bt=False, at=False,
        after=None):
    M, K = a.shape[::-1] if at else a.shape
    N = b.shape[0 if bt else 1] if n_cols is None else n_cols
    assert not (bt and n_off)
    dims = (((0 if at else 1,), (1 if bt else 0,)), ((), ()))
    tm, tn, tk = min(tm, M), min(tn, N), min(tk, K)
    assert M % tm == 0 and N % tn == 0 and K % tk == 0 and n_off % tn == 0, (a.shape, b.shape, tm, tn, tk)
    nk, joff = K // tk, n_off // tn
    has_c, has_after = c is not None, after is not None

    def body(*refs):
        a_ref, b_ref = refs[0], refs[1]
        c_ref = refs[2] if has_c else None
        o_ref = refs[2 + has_c + has_after]

        def finish(r):
            if has_c:
                r = r + c_scale * c_ref[...]
            o_ref[...] = r.astype(out_dtype)

        if nk == 1:
            finish(lax.dot_general(a_ref[...], b_ref[...], dims, preferred_element_type=F32))
        else:
            acc_ref = refs[3 + has_c + has_after]
            k = pl.program_id(2)

            @pl.when(k == 0)
            def _():
                acc_ref[...] = jnp.zeros_like(acc_ref)

            acc_ref[...] += lax.dot_general(a_ref[...], b_ref[...], dims, preferred_element_type=F32)

            @pl.when(k == nk - 1)
            def _():
                finish(acc_ref[...])

    in_specs = [pl.BlockSpec((tk, tm), lambda i, j, k: (k, i)) if at else pl.BlockSpec((tm, tk), lambda i, j, k: (i, k)),
                pl.BlockSpec((tn, tk), lambda i, j, k: (j, k)) if bt
                else pl.BlockSpec((tk, tn), lambda i, j, k: (k, j + joff))]
    args = [a, b]
    if has_c:
        in_specs.append(pl.BlockSpec((tm, tn), lambda i, j, k: (i, j)))
        args.append(c)
    if has_after:
        in_specs.append(ANY)
        args.append(after)
    return pl.pallas_call(
        body, name=name, grid=(M // tm, N // tn, nk),
        in_specs=in_specs,
        out_specs=pl.BlockSpec((tm, tn), lambda i, j, k: (i, j)),
        out_shape=jax.ShapeDtypeStruct((M, N), out_dtype),
        scratch_shapes=[pltpu.VMEM((tm, tn), F32)] if nk > 1 else [],
        compiler_params=_params(("parallel", "parallel", "arbitrary")),
    )(*args)


def _shift_down(x, k, row):
    return jnp.where(row >= k, pltpu.roll(x, k, 0), 0.0)


def _shift_up(x, k, row):
    s = x.shape[0]
    return jnp.where(row < s - k, pltpu.roll(x, s - k, 0), 0.0)


def _fg_fwd(fgp, bf):
    S = fgp.shape[0]

    def body(f_ref, b_ref, cum_ref):
        a = f_ref[...] + b_ref[...]
        x = jnp.minimum(a, 0.0) - jnp.log(1.0 + jnp.exp(-jnp.abs(a)))
        row = lax.broadcasted_iota(jnp.int32, x.shape, 0)
        k = 1
        while k < S:
            x = x + _shift_down(x, k, row)
            k *= 2
        cum_ref[...] = x

    return pl.pallas_call(body, name="fg_fwd", out_shape=jax.ShapeDtypeStruct((S, LANES), F32),
                          compiler_params=_params(None))(fgp, bf)


def _fg_bwd(dcum, fgp, bf, dproj):
    S = fgp.shape[0]

    def body(d_ref, f_ref, b_ref, dproj_in, dfg_ref, dbf_ref):
        x = d_ref[...]
        row = lax.broadcasted_iota(jnp.int32, x.shape, 0)
        k = 1
        while k < S:
            x = x + _shift_up(x, k, row)
            k *= 2
        dfg = x * _sigmoid(-(f_ref[...] + b_ref[...]))
        dfg_ref[...] = dfg.astype(BF16)
        dbf_ref[...] = jnp.sum(dfg, axis=0, keepdims=True)

    whole = pl.BlockSpec((S, LANES), lambda i: (0, 0))
    vec = pl.BlockSpec((1, LANES), lambda i: (0, 0))
    return pl.pallas_call(
        body, name="fg_bwd", grid=(1,),
        in_specs=[whole, whole, vec, ANY],
        out_specs=(pl.BlockSpec((S, LANES), lambda i: (0, FGB)), vec),
        out_shape=(jax.ShapeDtypeStruct(dproj.shape, dproj.dtype), jax.ShapeDtypeStruct((1, LANES), F32)),
        input_output_aliases={3: 0},
        compiler_params=_params(("arbitrary",)),
    )(dcum, fgp, bf, dproj)


def _put_columns(pieces, dst, blocks, sems):
    copies = [pltpu.make_async_copy(p, dst.at[:, pl.ds(pl.multiple_of(b * LANES, LANES), LANES)], sems.at[i])
              for i, (p, b) in enumerate(zip(pieces, blocks))]
    for cp in copies:
        cp.start()
    for cp in copies:
        cp.wait()


def _colblock(S, off):
    return pl.BlockSpec((S, LANES), lambda h: (0, h + off))


def _head_col(S):
    return pl.BlockSpec((None, S, 1), lambda h: (h, 0, 0))


def _head_row(nq, tq):
    return pl.BlockSpec((None, nq, 1, tq), lambda h: (h, 0, 0, 0))


def _lane_of(ref, h):
    lane = lax.broadcasted_iota(jnp.int32, ref.shape, 1)
    return jnp.sum(jnp.where(lane == h, ref[...], 0.0), axis=1, keepdims=True)


def _attn_fwd(qkv, rest, cum, cumr, *, tq):
    S = qkv.shape[0]
    nq = S // tq

    def body(q_ref, k_ref, v_ref, g_ref, cum_ref, cumr_ref, o_ref, y_ref, lse_ref, cc):
        cc[...] = _lane_of(cum_ref, pl.program_id(0)) * LOG2E
        tri = (lax.broadcasted_iota(jnp.int32, (tq, tq), 1) <= lax.broadcasted_iota(jnp.int32, (tq, tq), 0))

        def q_step(qi, _):
            rows = pl.ds(pl.multiple_of(qi * tq, tq), tq)
            q = q_ref[rows, :]
            ci = cc[rows, :]

            def tile(kj, carry, masked):
                m, l, acc = carry
                cols = pl.ds(pl.multiple_of(kj * tq, tq), tq)
                s = lax.dot_general(q, k_ref[cols, :], (((1,), (1,)), ((), ())), preferred_element_type=F32)
                s = s * (SCALE * LOG2E) + ci - cumr_ref[kj]
                if masked:
                    s = jnp.where(tri, s, NEG)
                m_new = jnp.maximum(m, jnp.max(s, axis=1, keepdims=True))
                a = jnp.exp2(m - m_new)
                p = jnp.exp2(s - m_new)
                l = a * l + jnp.sum(p, axis=1, keepdims=True)
                acc = a * acc + jnp.dot(p.astype(BF16), v_ref[cols, :], preferred_element_type=F32)
                return m_new, l, acc

            init = (jnp.full((tq, 1), NEG, F32), jnp.zeros((tq, 1), F32), jnp.zeros((tq, HEAD_DIM), F32))
            carry = lax.fori_loop(0, qi, lambda kj, c: tile(kj, c, False), init)
            m, l, acc = tile(qi, carry, True)
            o = acc / l
            g = g_ref[rows, :]
            o_ref[rows, :] = o
            y_ref[rows, :] = (o * (g * _sigmoid(g))).astype(BF16)
            lse_ref[rows, :] = m * LN2 + jnp.log(l)
            return 0

        lax.fori_loop(0, nq, q_step, 0)

    return pl.pallas_call(
        body, name="attn_fwd", grid=(N_HEADS,),
        in_specs=[_colblock(S, QB), _colblock(S, KB), _colblock(S, VB), _colblock(S, GB),
                  pl.BlockSpec((S, LANES), lambda h: (0, 0)), _head_row(nq, tq)],
        out_specs=(_colblock(S, 0), _colblock(S, 0), _head_col(S)),
        out_shape=(jax.ShapeDtypeStruct((S, ATT_W), F32), jax.ShapeDtypeStruct((S, D_MODEL), BF16),
                   jax.ShapeDtypeStruct((N_HEADS, S, 1), F32)),
        scratch_shapes=[pltpu.VMEM((S, 1), F32)],
        compiler_params=_params(("arbitrary",)),
    )(qkv, qkv, qkv, rest, cum, cumr)


def _attn_bwd_pre(rest, o, dy, cum, lse):
    S = o.shape[0]

    def body(g_ref, o_ref, dy_ref, cum_ref, lse_ref, do_ref, dg_ref, a_ref, dl_ref):
        g = g_ref[...]
        sg = _sigmoid(g)
        ov = o_ref[...]
        dy = dy_ref[...]
        do = dy * (g * sg)
        do_ref[...] = do.astype(BF16)
        dg_ref[...] = (dy * ov * (sg * (1.0 + g * (1.0 - sg)))).astype(BF16)
        dl_ref[...] = jnp.sum(do * ov, axis=1, keepdims=True)
        a_ref[...] = (_lane_of(cum_ref, pl.program_id(0)) - lse_ref[...]) * LOG2E

    att = jax.ShapeDtypeStruct((S, ATT_W), BF16)
    col = jax.ShapeDtypeStruct((N_HEADS, S, 1), F32)
    return pl.pallas_call(
        body, name="attn_bwd_pre", grid=(N_HEADS,),
        in_specs=[_colblock(S, GB), _colblock(S, 0), _colblock(S, 0),
                  pl.BlockSpec((S, LANES), lambda h: (0, 0)), _head_col(S)],
        out_specs=(_colblock(S, 0), _colblock(S, QKV_W // LANES), _head_col(S), _head_col(S)),
        out_shape=(att, jax.ShapeDtypeStruct((S, ALL_W), BF16), col, col),
        compiler_params=_params(("parallel",)),
    )(rest, o, dy, cum, lse)


def _attn_bwd(qkv, do, a, delta, cumr, dproj, *, tq):
    S = qkv.shape[0]
    nq = S // tq
    tdot = (((0,), (0,)), ((), ()))

    def body(q_ref, k_ref, v_ref, do_ref, a_ref, dl_ref, cumr_ref, dproj_in, dproj_ref, dcr_ref, dcc_ref,
             dqa, dq_ref, dk_ref, dv_ref, sems):
        dqa[...] = jnp.zeros_like(dqa)

        @pl.when(pl.program_id(0) == 0)
        def _():
            dcr_ref[...] = jnp.zeros_like(dcr_ref)

        tri = (lax.broadcasted_iota(jnp.int32, (tq, tq), 1) <= lax.broadcasted_iota(jnp.int32, (tq, tq), 0))
        mine = lax.broadcasted_iota(jnp.int32, (tq, LANES), 1) == pl.program_id(0)

        def k_step(kj, _):
            cols = pl.ds(pl.multiple_of(kj * tq, tq), tq)
            k = k_ref[cols, :]
            v = v_ref[cols, :]
            cj = cumr_ref[kj]

            def tile(qi, carry, masked):
                dk, dv, csum = carry
                rows = pl.ds(pl.multiple_of(qi * tq, tq), tq)
                q = q_ref[rows, :]
                dot = do_ref[rows, :]
                s = lax.dot_general(q, k, (((1,), (1,)), ((), ())), preferred_element_type=F32)
                p = jnp.exp2(s * (SCALE * LOG2E) + a_ref[rows, :] - cj)
                if masked:
                    p = jnp.where(tri, p, 0.0)
                dv = dv + lax.dot_general(p.astype(BF16), dot, tdot, preferred_element_type=F32)
                dp = lax.dot_general(dot, v, (((1,), (1,)), ((), ())), preferred_element_type=F32)
                ds = p * (dp - dl_ref[rows, :])
                dcr_ref[rows, :] += jnp.where(mine, jnp.sum(ds, axis=1, keepdims=True), 0.0)
                csum = csum + jnp.sum(ds, axis=0, keepdims=True)
                dsb = (ds * SCALE).astype(BF16)
                dqa[rows, :] += jnp.dot(dsb, k, preferred_element_type=F32)
                dk = dk + lax.dot_general(dsb, q, tdot, preferred_element_type=F32)
                return dk, dv, csum

            init = (jnp.zeros((tq, HEAD_DIM), F32), jnp.zeros((tq, HEAD_DIM), F32), jnp.zeros((1, tq), F32))
            carry = tile(kj, init, True)
            dk, dv, csum = lax.fori_loop(kj + 1, nq, lambda qi, c: tile(qi, c, False), carry)
            dk_ref[cols, :] = dk.astype(BF16)
            dv_ref[cols, :] = dv.astype(BF16)
            dcc_ref[kj] = csum
            return 0

        lax.fori_loop(0, nq, k_step, 0)
        dq_ref[...] = dqa[...].astype(BF16)
        h = pl.program_id(0)
        _put_columns([dq_ref, dk_ref, dv_ref], dproj_ref, [QB + h, KB + h, VB + h], sems)

    head = pltpu.VMEM((S, HEAD_DIM), BF16)
    return pl.pallas_call(
        body, name="attn_bwd", grid=(N_HEADS,),
        in_specs=[_colblock(S, QB), _colblock(S, KB), _colblock(S, VB), _colblock(S, 0),
                  _head_col(S), _head_col(S), _head_row(nq, tq), ANY],
        out_specs=(ANY, pl.BlockSpec((S, LANES), lambda h: (0, 0)), _head_row(nq, tq)),
        out_shape=(jax.ShapeDtypeStruct(dproj.shape, dproj.dtype), jax.ShapeDtypeStruct((S, LANES), F32),
                   jax.ShapeDtypeStruct((N_HEADS, nq, 1, tq), F32)),
        input_output_aliases={7: 0},
        scratch_shapes=[pltpu.VMEM((S, HEAD_DIM), F32), head, head, head, pltpu.SemaphoreType.DMA((3,))],
        compiler_params=_params(("arbitrary",)),
    )(qkv, qkv, qkv, do, a, delta, cumr, dproj)


def _conv_taps(u, w_ref, row):
    return (w_ref[0:1, :] * _shift_down(u, 2, row) + w_ref[1:2, :] * _shift_down(u, 1, row)) + w_ref[2:3, :] * u


def _conv_fwd(rest, cw, y):
    S = rest.shape[0]

    def body(cb_ref, cc_ref, ch_ref, g_ref, w_ref, y_in, y_ref):
        row = lax.broadcasted_iota(jnp.int32, (S, LANES), 0)
        g = g_ref[...]
        y = _conv_taps(cc_ref[...] * ch_ref[...], w_ref, row)
        y_ref[...] = ((cb_ref[...] * y) * (g * _sigmoid(g))).astype(BF16)

    return pl.pallas_call(
        body, name="conv_fwd", grid=(CONV_W // LANES,),
        in_specs=[_colblock(S, CBB), _colblock(S, CCB), _colblock(S, CHB), _colblock(S, CGB),
                  pl.BlockSpec((CONV_TAPS, LANES), lambda j: (0, j)), ANY],
        out_specs=_colblock(S, ATT_W // LANES),
        out_shape=jax.ShapeDtypeStruct(y.shape, y.dtype),
        input_output_aliases={5: 0},
        compiler_params=_params(("parallel",)),
    )(rest, rest, rest, rest, cw, y)


def _conv_bwd(rest, dy, cw, dproj):
    S = rest.shape[0]

    def body(cb_ref, cc_ref, ch_ref, g_ref, dy_ref, w_ref, dproj_in, dproj_ref, dw_ref,
             dcb_ref, dcc_ref, dch_ref, dg_ref, sems):
        row = lax.broadcasted_iota(jnp.int32, (S, LANES), 0)
        g = g_ref[...]
        sg = _sigmoid(g)
        silu = g * sg
        cb, cc, ch, dy = cb_ref[...], cc_ref[...], ch_ref[...], dy_ref[...]
        u = cc * ch
        u1 = _shift_down(u, 1, row)
        u2 = _shift_down(u, 2, row)
        y = (w_ref[0:1, :] * u2 + w_ref[1:2, :] * u1) + w_ref[2:3, :] * u
        dcb_ref[...] = (dy * silu * y).astype(BF16)
        dg_ref[...] = (dy * (cb * y) * (sg * (1.0 + g * (1.0 - sg)))).astype(BF16)
        dyv = dy * silu * cb
        du = w_ref[2:3, :] * dyv + w_ref[1:2, :] * _shift_up(dyv, 1, row) + w_ref[0:1, :] * _shift_up(dyv, 2, row)
        dcc_ref[...] = (du * ch).astype(BF16)
        dch_ref[...] = (du * cc).astype(BF16)
        dw_ref[0:1, :] = jnp.sum(dyv * u2, axis=0, keepdims=True)
        dw_ref[1:2, :] = jnp.sum(dyv * u1, axis=0, keepdims=True)
        dw_ref[2:3, :] = jnp.sum(dyv * u, axis=0, keepdims=True)
        j = pl.program_id(0)
        first = QKV_W // LANES
        _put_columns([dcb_ref, dcc_ref, dch_ref, dg_ref], dproj_ref,
                     [first + CBB + j, first + CCB + j, first + CHB + j, first + CGB + j], sems)

    piece = pltpu.VMEM((S, LANES), BF16)
    return pl.pallas_call(
        body, name="conv_bwd", grid=(CONV_W // LANES,),
        in_specs=[_colblock(S, CBB), _colblock(S, CCB), _colblock(S, CHB), _colblock(S, CGB),
                  _colblock(S, ATT_W // LANES), pl.BlockSpec((CONV_TAPS, LANES), lambda j: (0, j)), ANY],
        out_specs=(ANY, pl.BlockSpec((CONV_TAPS, LANES), lambda j: (0, j))),
        out_shape=(jax.ShapeDtypeStruct(dproj.shape, dproj.dtype), jax.ShapeDtypeStruct((CONV_TAPS, CONV_W), F32)),
        input_output_aliases={6: 0},
        scratch_shapes=[piece] * 4 + [pltpu.SemaphoreType.DMA((4,))],
        compiler_params=_params(("arbitrary",)),
    )(rest, rest, rest, rest, dy, cw, dproj)


def _pool_z(u, grp, row):
    s2 = u + _shift_down(u, 1, row)
    s4 = s2 + _shift_down(s2, 2, row)
    s8 = s4 + _shift_down(s4, 4, row)
    s16 = s8 + _shift_down(s8, 8, row)
    sw = jnp.where(grp == 0, s2, jnp.where(grp == 1, s4, jnp.where(grp == 2, s8, s16)))
    return sw / _pool_count(grp, row) - u


def _pool_count(grp, row):
    return jnp.minimum(row + 1, 2 << grp).astype(F32)


def _pool_fwd(rest, pw, scale, y):
    S = rest.shape[0]

    def body(u_ref, g_ref, pw_ref, sc_ref, y_in, y_ref):
        grp = pl.program_id(0)
        row = lax.broadcasted_iota(jnp.int32, (S, LANES), 0)
        z = _pool_z(u_ref[...], grp, row)
        zp = jnp.dot(z.astype(BF16), pw_ref[...].astype(BF16), preferred_element_type=F32)
        g = g_ref[...]
        y_ref[...] = ((zp * sc_ref[...]) * (g * _sigmoid(g))).astype(BF16)

    return pl.pallas_call(
        body, name="pool_fwd", grid=(N_POOL,),
        in_specs=[_colblock(S, PUB), _colblock(S, PGB),
                  pl.BlockSpec((None, POOL_GROUP, POOL_GROUP), lambda j: (j, 0, 0)),
                  pl.BlockSpec((1, LANES), lambda j: (0, j)), ANY],
        out_specs=_colblock(S, (ATT_W + CONV_W) // LANES),
        out_shape=jax.ShapeDtypeStruct(y.shape, y.dtype),
        input_output_aliases={4: 0},
        compiler_params=_params(("parallel",)),
    )(rest, rest, pw, scale, y)


def _pool_bwd(rest, dy, pw, scale, dproj):
    S = rest.shape[0]

    def body(u_ref, g_ref, dy_ref, pw_ref, sc_ref, dproj_in, dproj_ref, dpw_ref, dsc_ref, du_ref, dg_ref, sems):
        grp = pl.program_id(0)
        row = lax.broadcasted_iota(jnp.int32, (S, LANES), 0)
        z = _pool_z(u_ref[...], grp, row).astype(BF16)
        pwb = pw_ref[...].astype(BF16)
        zp = jnp.dot(z, pwb, preferred_element_type=F32)
        g = g_ref[...]
        sg = _sigmoid(g)
        silu = g * sg
        dy = dy_ref[...]
        sc = sc_ref[...]
        dsc_ref[...] = jnp.sum(dy * silu * zp, axis=0, keepdims=True)
        dg_ref[...] = (dy * (zp * sc) * (sg * (1.0 + g * (1.0 - sg)))).astype(BF16)
        dzp = (dy * silu * sc).astype(BF16)
        dpw_ref[...] = lax.dot_general(z, dzp, (((0,), (0,)), ((), ())), preferred_element_type=F32)
        dz = lax.dot_general(dzp, pwb, (((1,), (1,)), ((), ())), preferred_element_type=F32)
        f1 = dz / _pool_count(grp, row)
        f2 = f1 + _shift_up(f1, 1, row)
        f4 = f2 + _shift_up(f2, 2, row)
        f8 = f4 + _shift_up(f4, 4, row)
        f16 = f8 + _shift_up(f8, 8, row)
        fw = jnp.where(grp == 0, f2, jnp.where(grp == 1, f4, jnp.where(grp == 2, f8, f16)))
        du_ref[...] = (fw - dz).astype(BF16)
        first = QKV_W // LANES
        _put_columns([du_ref, dg_ref], dproj_ref, [first + PUB + grp, first + PGB + grp], sems)

    piece = pltpu.VMEM((S, LANES), BF16)
    return pl.pallas_call(
        body, name="pool_bwd", grid=(N_POOL,),
        in_specs=[_colblock(S, PUB), _colblock(S, PGB), _colblock(S, (ATT_W + CONV_W) // LANES),
                  pl.BlockSpec((None, POOL_GROUP, POOL_GROUP), lambda j: (j, 0, 0)),
                  pl.BlockSpec((1, LANES), lambda j: (0, j)), ANY],
        out_specs=(ANY, pl.BlockSpec((None, POOL_GROUP, POOL_GROUP), lambda j: (j, 0, 0)),
                   pl.BlockSpec((1, LANES), lambda j: (0, j))),
        out_shape=(jax.ShapeDtypeStruct(dproj.shape, dproj.dtype),
                   jax.ShapeDtypeStruct((N_POOL, POOL_GROUP, POOL_GROUP), F32), jax.ShapeDtypeStruct((1, POOL_W), F32)),
        input_output_aliases={5: 0},
        scratch_shapes=[piece] * 2 + [pltpu.SemaphoreType.DMA((2,))],
        compiler_params=_params(("arbitrary",)),
    )(rest, rest, dy, pw, scale, dproj)


def _ln_fwd(z, g, b, *, ts):
    S, D = z.shape

    def body(z_ref, g_ref, b_ref, o_ref, ob_ref, obt_ref, xh_ref, rs_ref):
        zz = z_ref[...]
        mu = jnp.mean(zz, axis=1, keepdims=True)
        zc = zz - mu
        rstd = lax.rsqrt(jnp.mean(zc * zc, axis=1, keepdims=True) + LN_EPS)
        xh = zc * rstd
        out = xh * g_ref[...] + b_ref[...]
        o_ref[...] = out
        ob_ref[...] = out.astype(BF16)
        obt_ref[...] = out.T.astype(BF16)
        xh_ref[...] = xh
        rs_ref[...] = rstd

    rowblk = pl.BlockSpec((ts, D), lambda i: (i, 0))
    vec = pl.BlockSpec((1, D), lambda i: (0, 0))
    return pl.pallas_call(
        body, name="ln_fwd", grid=(S // ts,),
        in_specs=[rowblk, vec, vec],
        out_specs=(rowblk, rowblk, pl.BlockSpec((D, ts), lambda i: (0, i)), rowblk,
                   pl.BlockSpec((ts, 1), lambda i: (i, 0))),
        out_shape=(jax.ShapeDtypeStruct((S, D), F32), jax.ShapeDtypeStruct((S, D), BF16),
                   jax.ShapeDtypeStruct((D, S), BF16), jax.ShapeDtypeStruct((S, D), F32),
                   jax.ShapeDtypeStruct((S, 1), F32)),
        compiler_params=_params(("parallel",)),
    )(z, g, b)


def _prep(x, after, *, ts):
    S, D = x.shape

    def body(x_ref, after_ref, xb_ref, xbt_ref):
        xb_ref[...] = x_ref[...].astype(BF16)
        xbt_ref[...] = x_ref[...].T.astype(BF16)

    return pl.pallas_call(
        body, name="prep", grid=(S // ts,),
        in_specs=[pl.BlockSpec((ts, D), lambda i: (i, 0)), ANY],
        out_specs=(pl.BlockSpec((ts, D), lambda i: (i, 0)), pl.BlockSpec((D, ts), lambda i: (0, i))),
        out_shape=(jax.ShapeDtypeStruct((S, D), BF16), jax.ShapeDtypeStruct((D, S), BF16)),
        compiler_params=_params(("parallel",)),
    )(x, after)


def _ln_bwd(dout, xhat, rstd, g, *, ts):
    S, D = dout.shape

    def body(d_ref, xh_ref, rs_ref, g_ref, dz_ref, dzb_ref, dg_ref, db_ref):
        @pl.when(pl.program_id(0) == 0)
        def _():
            dg_ref[...] = jnp.zeros_like(dg_ref)
            db_ref[...] = jnp.zeros_like(db_ref)

        d = d_ref[...]
        xh = xh_ref[...]
        dxh = d * g_ref[...]
        m1 = jnp.mean(dxh, axis=1, keepdims=True)
        m2 = jnp.mean(dxh * xh, axis=1, keepdims=True)
        dz = rs_ref[...] * (dxh - m1 - xh * m2)
        dz_ref[...] = dz
        dzb_ref[...] = dz.astype(BF16)
        dg_ref[...] += jnp.sum(d * xh, axis=0, keepdims=True)
        db_ref[...] += jnp.sum(d, axis=0, keepdims=True)

    rowblk = pl.BlockSpec((ts, D), lambda i: (i, 0))
    vec = pl.BlockSpec((1, D), lambda i: (0, 0))
    return pl.pallas_call(
        body, name="ln_bwd", grid=(S // ts,),
        in_specs=[rowblk, rowblk, pl.BlockSpec((ts, 1), lambda i: (i, 0)), vec],
        out_specs=(rowblk, rowblk, vec, vec),
        out_shape=(jax.ShapeDtypeStruct((S, D), F32), jax.ShapeDtypeStruct((S, D), BF16),
                   jax.ShapeDtypeStruct((1, D), F32), jax.ShapeDtypeStruct((1, D), F32)),
        compiler_params=_params(("arbitrary",)),
    )(dout, xhat, rstd, g)


def _loss_head(y, tgt, *, ts):
    S, D = y.shape

    def body(y_ref, t_ref, l_ref, dy_ref):
        @pl.when(pl.program_id(0) == 0)
        def _():
            l_ref[...] = jnp.zeros_like(l_ref)

        e = y_ref[...] - t_ref[...]
        dy_ref[...] = e * (1.0 / D)
        rowloss = jnp.sum(e * e, axis=1, keepdims=True) * (0.5 / D)
        l_ref[...] += jnp.sum(rowloss, axis=0, keepdims=True)

    rowblk = pl.BlockSpec((ts, D), lambda i: (i, 0))
    return pl.pallas_call(
        body, name="loss_head", grid=(S // ts,),
        in_specs=[rowblk, rowblk],
        out_specs=(pl.BlockSpec((1, 1), lambda i: (0, 0)), rowblk),
        out_shape=(jax.ShapeDtypeStruct((1, 1), F32), jax.ShapeDtypeStruct((S, D), F32)),
        compiler_params=_params(("arbitrary",)),
    )(y, tgt)


def _adamw(lands, srcs, w, m, v, *, name, first=0, into=None):
    R, C = w.shape
    npart = len(lands)
    rp = lands[0].shape[1]
    assert R % rp == 0 and all(p.shape == (N_DEV, rp, C) for p in lands)
    tiled = rp % 8 == 0
    assert tiled or (first == 0 and npart * rp == R and into is None)
    tr = min(ADAMW_BLOCK_ELEMS // (LANES * pl.cdiv(C, LANES)), rp) if tiled else rp
    assert rp % tr == 0
    steps = rp // tr
    c1 = 1.0 - ADAM_B1 ** ADAM_STEP
    c2 = 1.0 - ADAM_B2 ** ADAM_STEP
    slabbed = [p.ndim == 3 for p in srcs]
    n_into = 0 if into is None else 4

    def body(*refs):
        l_refs, s_refs = refs[:npart], refs[npart:2 * npart]
        w_ref, m_ref, v_ref = refs[2 * npart:2 * npart + 3]
        g_ref, d_ref, nm_ref, nv_ref = refs[2 * npart + 3 + n_into:]
        x, y, c = _my_place()
        me = 4 * x + 2 * y + c

        def update(l, rows):
            g = None
            for j in range(N_DEV):
                own = s_refs[l][j] if slabbed[l] else s_refs[l][...]
                p = jnp.where(me == j, own, l_refs[l][j]).astype(F32)
                g = p if g is None else g + p
            nm = ADAM_B1 * m_ref[rows, :] + (1.0 - ADAM_B1) * g
            nv = ADAM_B2 * v_ref[rows, :] + (1.0 - ADAM_B2) * (g * g)
            g_ref[rows, :] = g
            nm_ref[rows, :] = nm
            nv_ref[rows, :] = nv
            d_ref[rows, :] = -ADAM_LR * ((nm / c1) / (jnp.sqrt(nv / c2) + ADAM_EPS) + ADAM_WD * w_ref[rows, :])

        for l in range(npart):
            if tiled:
                pl.when(pl.program_id(0) == l)(lambda l=l: update(l, slice(None)))
            else:
                update(l, slice(l * rp, (l + 1) * rp))

    def part_spec(l, p):
        if p.ndim == 3:
            return pl.BlockSpec((N_DEV, tr, C), lambda q, i: (0, jnp.where(q == l, i, 0), 0))
        return pl.BlockSpec((tr, C), lambda q, i: (jnp.where(q == l, i, 0), 0))

    if tiled:
        blk = pl.BlockSpec((tr, C), lambda q, i: ((first + q) * steps + i, 0))
    else:
        blk = pl.BlockSpec((R, C), lambda q, i: (0, 0))
    out = jax.ShapeDtypeStruct((R, C), F32)
    n_in = 2 * npart + 3
    return pl.pallas_call(
        body, name=name, grid=(npart, steps) if tiled else (1, 1),
        in_specs=[part_spec(l, p) for l, p in enumerate(lands)] + [part_spec(l, p) for l, p in enumerate(srcs)]
        + [blk, blk, blk] + [ANY] * n_into,
        out_specs=(blk, blk, blk, blk),
        out_shape=(out, out, out, out),
        input_output_aliases={n_in + i: i for i in range(n_into)},
        compiler_params=_params(("arbitrary", "arbitrary")),
    )(*lands, *srcs, w, m, v, *(into or ()))


def _my_place():
    return lax.axis_index("x"), lax.axis_index("y"), lax.axis_index("c")


HBM = pl.BlockSpec(memory_space=pltpu.HBM)
SEM = pl.BlockSpec(memory_space=pltpu.SEMAPHORE)
DATAFLOW = pltpu.SideEffectType.DATAFLOW_SIDE_EFFECTING
EVERYONE = (1, 2, 3, 4, 5, 6, 7)
SIBLING = 1
SAME_CORE = (4, 2, 6)


def _xchg_copies(src, land, send_sem, recv_sem, gather, peers):
    x, y, c = _my_place()
    me = 4 * x + 2 * y + c
    copies = []
    for k, bits in enumerate(peers):
        px, py, pc = x ^ ((bits >> 2) & 1), y ^ ((bits >> 1) & 1), c ^ (bits & 1)
        peer = 4 * px + 2 * py + pc
        for a in range(len(src)):
            i = a * len(peers) + k
            out = src[a] if gather[a] else src[a].at[peer]
            sent, landed = [pltpu.make_async_remote_copy(
                src_ref=out, dst_ref=land[a].at[slab], send_sem=send_sem.at[i], recv_sem=recv_sem.at[i],
                device_id=(px, py, pc), device_id_type=MESH) for slab in (me, peer)]
            copies.append((sent, landed))
    return copies


def _xchg_start(srcs, *, gather, name, peers=EVERYONE, after=None):
    n = len(srcs)
    follows = [] if after is None else [after]
    lands = [lax.empty(((N_DEV,) + v.shape) if g else v.shape, v.dtype) for v, g in zip(srcs, gather)]

    def body(*refs):
        src, land = refs[:n], refs[n:2 * n]
        send_sem, recv_sem = refs[2 * n + len(follows)], refs[2 * n + len(follows) + 1]
        token = refs[-1]
        for sent, _ in _xchg_copies(src, land, send_sem, recv_sem, gather, peers):
            sent.start()
        token[...] = jnp.zeros_like(token)

    sems = pltpu.SemaphoreType.DMA((n * len(peers),))
    outs = pl.pallas_call(
        body, name=name,
        out_shape=(sems, sems, *[pltpu.HBM(v.shape, v.dtype) for v in srcs + lands],
                   jax.ShapeDtypeStruct((8, LANES), F32)),
        in_specs=[HBM] * (2 * n) + [ANY] * len(follows),
        out_specs=(SEM, SEM, *[HBM] * (2 * n), pl.BlockSpec(memory_space=pltpu.VMEM)),
        input_output_aliases={i: 2 + i for i in range(2 * n)},
        compiler_params=pltpu.CompilerParams(has_side_effects=DATAFLOW),
    )(*[pltpu.with_memory_space_constraint(v, pltpu.HBM) for v in srcs + lands], *follows)
    return dict(send=outs[0], recv=outs[1], srcs=list(outs[2:2 + n]), lands=list(outs[2 + n:2 + 2 * n]),
                token=outs[-1], gather=gather, peers=peers)


def _xchg_wait(h, after, *, name):
    n = len(h["srcs"])
    gather, peers = h["gather"], h["peers"]

    def body(*refs):
        src, land = refs[:n], refs[n:2 * n]
        send_sem, recv_sem = refs[2 * n], refs[2 * n + 1]
        for sent, landed in _xchg_copies(src, land, send_sem, recv_sem, gather, peers):
            sent.wait_send()
            landed.wait_recv()

    thru = h["srcs"] + h["lands"]
    outs = pl.pallas_call(
        body, name=name,
        out_shape=[pltpu.HBM(v.shape, v.dtype) for v in thru],
        in_specs=[HBM] * (2 * n) + [SEM, SEM, ANY],
        out_specs=[HBM] * (2 * n),
        input_output_aliases={i: i for i in range(2 * n)},
        compiler_params=pltpu.CompilerParams(has_side_effects=DATAFLOW),
    )(*thru, h["send"], h["recv"], after)
    return list(outs[:n]), list(outs[n:])


def _relay_copies(land, send_sem, recv_sem):
    x, y, c = _my_place()
    me = 4 * x + 2 * y + c
    copies = []
    for k, bits in enumerate(SAME_CORE):
        for a in range(len(land)):
            i = a * len(SAME_CORE) + k
            sent, landed = [pltpu.make_async_remote_copy(
                src_ref=land[a].at[me ^ bits], dst_ref=land[a].at[slab], send_sem=send_sem.at[i], recv_sem=recv_sem.at[i],
                device_id=(x, y, 1 - c), device_id_type=MESH) for slab in (me ^ bits, me ^ bits ^ SIBLING)]
            copies.append((sent, landed))
    return copies


def _relay_start(lands, *, name):
    n = len(lands)

    def body(*refs):
        land = refs[:n]
        send_sem, recv_sem = refs[n], refs[n + 1]
        token = refs[-1]
        for sent, _ in _relay_copies(land, send_sem, recv_sem):
            sent.start()
        token[...] = jnp.zeros_like(token)

    sems = pltpu.SemaphoreType.DMA((n * len(SAME_CORE),))
    outs = pl.pallas_call(
        body, name=name,
        out_shape=(sems, sems, *[pltpu.HBM(v.shape, v.dtype) for v in lands], jax.ShapeDtypeStruct((8, LANES), F32)),
        in_specs=[HBM] * n,
        out_specs=(SEM, SEM, *[HBM] * n, pl.BlockSpec(memory_space=pltpu.VMEM)),
        input_output_aliases={i: 2 + i for i in range(n)},
        compiler_params=pltpu.CompilerParams(has_side_effects=DATAFLOW),
    )(*[pltpu.with_memory_space_constraint(v, pltpu.HBM) for v in lands])
    return dict(send=outs[0], recv=outs[1], lands=list(outs[2:2 + n]), token=outs[-1])


def _relay_wait(h, after, *, name):
    n = len(h["lands"])

    def body(*refs):
        land = refs[:n]
        send_sem, recv_sem = refs[n], refs[n + 1]
        for sent, landed in _relay_copies(land, send_sem, recv_sem):
            sent.wait_send()
            landed.wait_recv()

    outs = pl.pallas_call(
        body, name=name,
        out_shape=[pltpu.HBM(v.shape, v.dtype) for v in h["lands"]],
        in_specs=[HBM] * n + [SEM, SEM, ANY],
        out_specs=[HBM] * n,
        input_output_aliases={i: i for i in range(n)},
        compiler_params=pltpu.CompilerParams(has_side_effects=DATAFLOW),
    )(*h["lands"], h["send"], h["recv"], after)
    return list(outs)


def _with_own(land, own, me):
    slot = lax.broadcasted_iota(jnp.int32, land.shape, 0)
    return jnp.where(slot == me, own if own.ndim == land.ndim else own[None], land)


def _w_in_segments(j, cols):
    lo, hi = j * cols, (j + 1) * cols
    segs = []
    for a, b, where, shift in ((0, FG_AT, "main", 0), (FG_AT, FG_AT + N_HEADS, "fg", -FG_AT),
                               (FG_AT + N_HEADS, IN_W, "main", -N_HEADS)):
        s0, s1 = max(lo, a), min(hi, b)
        if s0 < s1:
            segs.append((s0 - lo, s1 - s0, where, s0 + shift))
    return segs


SLAB_W = 1024


def _window(off, width):
    base = min((off // LANES) * LANES, width - SLAB_W)
    return base, off - base


def _w_unpack(land, own, *, tr):
    _, D, cols = land.shape
    assert cols + LANES - 1 <= SLAB_W

    def body(land_ref, own_ref, main_ref, slab):
        x, y, c = _my_place()
        me = 4 * x + 2 * y + c
        lane = lax.broadcasted_iota(jnp.int32, (tr, SLAB_W), 1)
        main_ref[...] = jnp.zeros_like(main_ref)
        slab[...] = jnp.zeros_like(slab)
        for j in range(N_DEV):
            slab[:, :cols] = jnp.where(me == j, own_ref[...], land_ref[j])
            v = slab[...]
            segs = _w_in_segments(j, cols)
            for lo, n, where, dst in segs:
                part = v if len(segs) == 1 else jnp.where((lane >= lo) & (lane < lo + n), v, jnp.zeros_like(v))
                if where == "fg":
                    main_ref[:, MAIN_W:] = pltpu.roll(part, (SLAB_W - lo) % SLAB_W, 1)[:, :LANES]
                else:
                    base, r = _window(dst - lo, MAIN_W)
                    assert 0 <= r and lo + n + r <= SLAB_W
                    main_ref[:, base:base + SLAB_W] += pltpu.roll(part, r, 1)

    return pl.pallas_call(
        body, name="w_unpack", grid=(D // tr,),
        in_specs=[pl.BlockSpec((N_DEV, tr, cols), lambda i: (0, i, 0)), pl.BlockSpec((tr, cols), lambda i: (i, 0))],
        out_specs=pl.BlockSpec((tr, ALL_W), lambda i: (i, 0)),
        out_shape=jax.ShapeDtypeStruct((D, ALL_W), BF16),
        scratch_shapes=[pltpu.VMEM((tr, SLAB_W), BF16)],
        compiler_params=_params(("parallel",)),
    )(land, own)


def _g_pack(dw_main, dw_fg, cols, *, tr):
    D = dw_main.shape[0]

    def body(main_ref, fg_ref, out_ref, slab):
        lane = lax.broadcasted_iota(jnp.int32, (tr, SLAB_W), 1)
        slab[...] = jnp.zeros_like(slab)
        slab[:, :LANES] = fg_ref[...]
        fg = slab[...]
        for j in range(N_DEV):
            v = None
            for lo, n, where, src in _w_in_segments(j, cols):
                if where == "fg":
                    part = pltpu.roll(fg, lo, 1)
                else:
                    base, r = _window(src - lo, MAIN_W)
                    assert 0 <= r and lo + n + r <= SLAB_W
                    part = pltpu.roll(main_ref[:, base:base + SLAB_W], (SLAB_W - r) % SLAB_W, 1)
                v = part if v is None else jnp.where((lane >= lo) & (lane < lo + n), part, v)
            out_ref[j] = v[:, :cols]

    return pl.pallas_call(
        body, name="g_pack", grid=(D // tr,),
        in_specs=[pl.BlockSpec((tr, MAIN_W), lambda i: (i, 0)), pl.BlockSpec((tr, LANES), lambda i: (i, 0))],
        out_specs=pl.BlockSpec((N_DEV, tr, cols), lambda i: (0, i, 0)),
        out_shape=jax.ShapeDtypeStruct((N_DEV, D, cols), BF16),
        scratch_shapes=[pltpu.VMEM((tr, SLAB_W), BF16)],
        compiler_params=_params(("parallel",)),
    )(dw_main, dw_fg)


TQ = 512
TS = 256
W_TILE_ROWS = 128


def _layer_fwd_proj(xb, w_in):
    qkv = _mm(xb, w_in, out_dtype=BF16, tm=1024, tn=512, tk=D_MODEL, name="mm_qkv", n_cols=QKV_W)
    rest = _mm(xb, w_in, out_dtype=F32, tm=1024, tn=512, tk=D_MODEL, name="mm_rest", n_off=QKV_W, n_cols=REST_W)
    fgp = _mm(xb, w_in, out_dtype=F32, tm=1024, tn=LANES, tk=D_MODEL, name="mm_fg", n_off=MAIN_W, n_cols=LANES)
    return qkv, rest, fgp


def _layer_fwd_mix(x, xbt, proj, wl):
    qkv, rest, fgp = proj
    cum = _fg_fwd(fgp, wl["b_f"])
    S = x.shape[0]
    cumr = (cum[:, :N_HEADS].T * LOG2E).reshape(N_HEADS, S // TQ, 1, TQ)
    o, y, lse = _attn_fwd(qkv, rest, cum, cumr, tq=TQ)
    y = _conv_fwd(rest, wl["conv_w"], y)
    y = _pool_fwd(rest, wl["pool_w"], wl["pool_scale"], y)
    z = _mm(y, wl["w_out"], out_dtype=F32, tm=1024, tn=512, tk=D_MODEL, name="mm_out", c=x, c_scale=ALPHA)
    out, outb, outbt, xhat, rstd = _ln_fwd(z, wl["ln_g"], wl["ln_b"], ts=TS)
    saved = dict(xbt=xbt, qkv=qkv, rest=rest, fgp=fgp, cum=cum, cumr=cumr, o=o, lse=lse, y=y, xhat=xhat, rstd=rstd)
    return out, outb, outbt, saved


def _layer_bwd_mix(dout, sv, wl):
    S = dout.shape[0]
    dz, dzb, dln_g, dln_b = _ln_bwd(dout, sv["xhat"], sv["rstd"], wl["ln_g"], ts=TS)
    dy = _mm(dzb, wl["w_out"], out_dtype=F32, tm=1024, tn=512, tk=D_MODEL, name="mm_dy", bt=True)
    dw_out = _mm(sv["y"], dzb, out_dtype=BF16, tm=512, tn=D_MODEL, tk=2048, name="mm_dw_out", at=True)
    rest = sv["rest"]
    do, dproj, a, delta = _attn_bwd_pre(rest, sv["o"], dy, sv["cum"], sv["lse"])
    dproj, dcr, dcc = _attn_bwd(sv["qkv"], do, a, delta, sv["cumr"], dproj, tq=TQ)
    dcum = dcr - jnp.pad(dcc.reshape(N_HEADS, S).T, ((0, 0), (0, LANES - N_HEADS)))
    dproj, db_f = _fg_bwd(dcum, sv["fgp"], wl["b_f"], dproj)
    dproj, dconv_w = _conv_bwd(rest, dy, wl["conv_w"], dproj)
    dproj, dpool_w, dpool_scale = _pool_bwd(rest, dy, wl["pool_w"], wl["pool_scale"], dproj)
    grads = dict(b_f=db_f[:, :N_HEADS], conv_w=dconv_w, pool_w=dpool_w, pool_scale=dpool_scale, w_out=dw_out,
                 ln_g=dln_g, ln_b=dln_b)
    return grads, (dproj, dz)


def _layer_bwd_w_in(mid, sv):
    dproj, dz = mid
    xt = sv["xbt"]
    S = xt.shape[1]
    dw_main = _mm(xt, dproj, out_dtype=BF16, tm=1024, tn=512, tk=S, name="mm_dw_main", n_cols=MAIN_W)
    dw_fg = _mm(xt, dproj, out_dtype=BF16, tm=1024, tn=LANES, tk=2048, name="mm_dw_fg", n_off=MAIN_W, n_cols=LANES)
    return (dw_main, dw_fg), (dproj, dz)


def _layer_bwd_input(ctx, wl, after):
    dproj, dz = ctx
    return _mm(dproj, wl["w_in"], out_dtype=F32, tm=512, tn=512, tk=ALL_W, name="mm_dx", c=dz, c_scale=ALPHA, bt=True,
               after=after)


def kernel(x, w_in, b_f, conv_w, pool_w, pool_scale, w_out, ln_g, ln_b, loss_target, m_w_in, m_b_f, m_conv_w, m_pool_w, m_pool_scale, m_w_out, m_ln_g, m_ln_b, v_w_in, v_b_f, v_conv_w, v_pool_w, v_pool_scale, v_w_out, v_ln_g, v_ln_b):
    L, D, cols = w_in.shape
    rows_out = w_out.shape[1]
    ccols = conv_w.shape[2]
    mx, my, mc = _my_place()
    me = 4 * mx + 2 * my + mc

    w_in_b, w_out_b = w_in.astype(BF16), w_out.astype(BF16)
    gathers = []
    for l in range(L):
        gathers.append(_xchg_start([w_in_b[l], w_out_b[l], conv_w[l]], gather=[True] * 3, name=f"gather_start_{l}",
                                   peers=(SIBLING,) + SAME_CORE if l == 0 else EVERYONE,
                                   after=gathers[-1]["token"] if gathers else None))
    started = gathers[-1]["token"]

    xl = x[0]
    xb, xbt = _prep(xl, started, ts=TS)
    weights, saved = [], []
    for l in range(L):
        (own_in, own_out, own_cw), (g_in, g_out, g_cw) = _xchg_wait(gathers[l], xb, name=f"gather_wait_{l}")
        if l == 0:
            g_in, g_out, g_cw = _relay_wait(_relay_start([g_in, g_out, g_cw], name="gather_relay_start"), xb,
                                            name="gather_relay_wait")
        w_all = _w_unpack(g_in, own_in, tr=W_TILE_ROWS)
        proj = _layer_fwd_proj(xb, w_all)
        w_o = _with_own(g_out, own_out, me).reshape(N_DEV * rows_out, D)
        cw = jnp.transpose(_with_own(g_cw, own_cw, me), (1, 0, 2)).reshape(CONV_TAPS, N_DEV * ccols)
        wl = dict(w_in=w_all, w_out=w_o, conv_w=cw,
                  b_f=jnp.pad(b_f[l][None, :], ((0, 0), (0, LANES - N_HEADS))), pool_w=pool_w[l],
                  pool_scale=pool_scale[l][None, :], ln_g=ln_g[l][None, :], ln_b=ln_b[l][None, :])
        xl, xb, xbt, sv = _layer_fwd_mix(xl, xbt, proj, wl)
        weights.append(wl)
        saved.append(sv)

    loss, dx = _loss_head(xl, loss_target[0], ts=TS)
    loss = lax.psum(loss[0, 0], ("x", "y", "c"))

    names = ["w_in", "w_out", "conv_w", "b_f", "pool_w", "pool_scale", "ln_g", "ln_b"]
    sharded = names[:3]
    scatters = [None] * L
    for l in reversed(range(L)):
        g, mid = _layer_bwd_mix(dx, saved[l], weights[l])
        (dw_main, dw_fg), ctx = _layer_bwd_w_in(mid, saved[l])
        g["w_in"] = _g_pack(dw_main, dw_fg, cols, tr=W_TILE_ROWS)
        g["w_out"] = g["w_out"].reshape(N_DEV, rows_out, D)
        g["conv_w"] = jnp.transpose(g["conv_w"].reshape(CONV_TAPS, N_DEV, ccols), (1, 0, 2))
        scatters[l] = _xchg_start([g[k] for k in names], gather=[k not in sharded for k in names],
                                  name=f"scatter_start_{l}")
        dx = _layer_bwd_input(ctx, weights[l], scatters[l]["token"])

    given = dict(w_in=(w_in, m_w_in, v_w_in), b_f=(b_f, m_b_f, v_b_f), conv_w=(conv_w, m_conv_w, v_conv_w),
                 pool_w=(pool_w, m_pool_w, v_pool_w), pool_scale=(pool_scale, m_pool_scale, v_pool_scale),
                 w_out=(w_out, m_w_out, v_w_out), ln_g=(ln_g, m_ln_g, v_ln_g), ln_b=(ln_b, m_ln_b, v_ln_b))
    sent, landed = [{} for _ in range(L)], [{} for _ in range(L)]

    def wait(l, after):
        s_, l_ = _xchg_wait(scatters[l], after, name=f"scatter_wait_{l}")
        sent[l].update(zip(names, s_))
        landed[l].update(zip(names, l_))

    def update(k, layers, into=None):
        w, m, v = given[k]
        C = w.shape[-1]
        w2, m2, v2 = (t.reshape(-1, C) for t in (w, m, v))
        rp = w2.shape[0] // L
        lands = [landed[l][k].reshape(N_DEV, rp, C) for l in layers]
        srcs = [sent[l][k].reshape((N_DEV, rp, C) if k in sharded else (rp, C)) for l in layers]
        return _adamw(lands, srcs, w2, m2, v2, name=f"adamw_{k}_{layers[0]}", first=layers[0], into=into)

    big = ["w_in", "w_out", "pool_w"]
    for l in reversed(range(1, L)):
        wait(l, dx)
    upper = {k: update(k, list(range(1, L))) for k in big} if L > 1 else {}
    wait(0, upper["w_in"][0] if upper else dx)
    res = {k: update(k, [0], into=upper[k]) if k in upper else update(k, list(range(L))) for k in names}
    res = {k: [t.reshape(given[k][0].shape) for t in outs] for k, outs in res.items()}

    order = ["w_in", "b_f", "conv_w", "pool_w", "pool_scale", "w_out", "ln_g", "ln_b"]
    return (loss, dx[None], *[res[k][0] for k in order], *[res[k][1] for k in order],
            *[res[k][2] for k in order], *[res[k][3] for k in order])
```

```python
import jax
import jax.numpy as jnp
from jax import lax
from jax.experimental import pallas as pl
from jax.experimental.pallas import tpu as pltpu

F32 = jnp.float32
BF16 = jnp.bfloat16

N_DEV = 8
D_MODEL = 2048
N_HEADS = 8
HEAD_DIM = 128
ATT_W = N_HEADS * HEAD_DIM
CONV_W = 512
CONV_TAPS = 3
POOL_W = 512
POOL_GROUP = 128
N_POOL = POOL_W // POOL_GROUP
FG_AT = 4 * ATT_W
IN_W = FG_AT + N_HEADS + 4 * CONV_W + 2 * POOL_W
MAIN_W = IN_W - N_HEADS
LANES = 128
ALL_W = MAIN_W + LANES
LN_EPS = 1e-5
DEPTH = 4
ALPHA = (2 * DEPTH) ** 0.25
SCALE = HEAD_DIM ** -0.5
LOG2E = 1.4426950408889634
LN2 = 0.6931471805599453
NEG = -1e30

ADAM_LR, ADAM_B1, ADAM_B2, ADAM_EPS, ADAM_WD, ADAM_STEP = 0.001, 0.9, 0.999, 1e-08, 0.01, 10

QKV_W = 3 * ATT_W
REST_W = MAIN_W - QKV_W
QB, KB, VB = 0, 8, 16
GB = 0
CBB, CCB, CHB, CGB = 8, 12, 16, 20
PUB, PGB = 24, 28
FGB = MAIN_W // LANES
VMEM_LIMIT = 48 * 1024 * 1024
ADAMW_BLOCK_ELEMS = 64 * 1024

MESH = pl.DeviceIdType.MESH
ANY = pl.BlockSpec(memory_space=pl.ANY)


def _params(semantics):
    return pltpu.CompilerParams(dimension_semantics=semantics, vmem_limit_bytes=VMEM_LIMIT)


def _sigmoid(x):
    return 1.0 / (1.0 + jnp.exp(-x))


def _mm(a, b, *, out_dtype, tm, tn, tk, name, c=None, c_scale=1.0, n_off=0, n_cols=None, bt=False, at=False,
        after=None):
    M, K = a.shape[::-1] if at else a.shape
    N = b.shape[0 if bt else 1] if n_cols is None else n_cols
    assert not (bt and n_off)
    dims = (((0 if at else 1,), (1 if bt else 0,)), ((), ()))
    tm, tn, tk = min(tm, M), min(tn, N), min(tk, K)
    assert M % tm == 0 and N % tn == 0 and K % tk == 0 and n_off % tn == 0, (a.shape, b.shape, tm, tn, tk)
    nk, joff = K // tk, n_off // tn
    has_c, has_after = c is not None, after is not None

    def body(*refs):
        a_ref, b_ref = refs[0], refs[1]
        c_ref = refs[2] if has_c else None
        o_ref = refs[2 + has_c + has_after]

        def finish(r):
            if has_c:
                r = r + c_scale * c_ref[...]
            o_ref[...] = r.astype(out_dtype)

        if nk == 1:
            finish(lax.dot_general(a_ref[...], b_ref[...], dims, preferred_element_type=F32))
        else:
            acc_ref = refs[3 + has_c + has_after]
            k = pl.program_id(2)

            @pl.when(k == 0)
            def _():
                acc_ref[...] = jnp.zeros_like(acc_ref)

            acc_ref[...] += lax.dot_general(a_ref[...], b_ref[...], dims, preferred_element_type=F32)

            @pl.when(k == nk - 1)
            def _():
                finish(acc_ref[...])

    in_specs = [pl.BlockSpec((tk, tm), lambda i, j, k: (k, i)) if at else pl.BlockSpec((tm, tk), lambda i, j, k: (i, k)),
                pl.BlockSpec((tn, tk), lambda i, j, k: (j, k)) if bt
                else pl.BlockSpec((tk, tn), lambda i, j, k: (k, j + joff))]
    args = [a, b]
    if has_c:
        in_specs.append(pl.BlockSpec((tm, tn), lambda i, j, k: (i, j)))
        args.append(c)
    if has_after:
        in_specs.append(ANY)
        args.append(after)
    return pl.pallas_call(
        body, name=name, grid=(M // tm, N // tn, nk),
        in_specs=in_specs,
        out_specs=pl.BlockSpec((tm, tn), lambda i, j, k: (i, j)),
        out_shape=jax.ShapeDtypeStruct((M, N), out_dtype),
        scratch_shapes=[pltpu.VMEM((tm, tn), F32)] if nk > 1 else [],
        compiler_params=_params(("parallel", "parallel", "arbitrary")),
    )(*args)


def _shift_down(x, k, row):
    return jnp.where(row >= k, pltpu.roll(x, k, 0), 0.0)


def _shift_up(x, k, row):
    s = x.shape[0]
    return jnp.where(row < s - k, pltpu.roll(x, s - k, 0), 0.0)


def _fg_fwd(fgp, bf):
    S = fgp.shape[0]

    def body(f_ref, b_ref, cum_ref):
        a = f_ref[...] + b_ref[...]
        x = jnp.minimum(a, 0.0) - jnp.log(1.0 + jnp.exp(-jnp.abs(a)))
        row = lax.broadcasted_iota(jnp.int32, x.shape, 0)
        k = 1
        while k < S:
            x = x + _shift_down(x, k, row)
            k *= 2
        cum_ref[...] = x

    return pl.pallas_call(body, name="fg_fwd", out_shape=jax.ShapeDtypeStruct((S, LANES), F32),
                          compiler_params=_params(None))(fgp, bf)


def _fg_bwd(dcum, fgp, bf, dproj):
    S = fgp.shape[0]

    def body(d_ref, f_ref, b_ref, dproj_in, dfg_ref, dbf_ref):
        x = d_ref[...]
        row = lax.broadcasted_iota(jnp.int32, x.shape, 0)
        k = 1
        while k < S:
            x = x + _shift_up(x, k, row)
            k *= 2
        dfg = x * _sigmoid(-(f_ref[...] + b_ref[...]))
        dfg_ref[...] = dfg.astype(BF16)
        dbf_ref[...] = jnp.sum(dfg, axis=0, keepdims=True)

    whole = pl.BlockSpec((S, LANES), lambda i: (0, 0))
    vec = pl.BlockSpec((1, LANES), lambda i: (0, 0))
    return pl.pallas_call(
        body, name="fg_bwd", grid=(1,),
        in_specs=[whole, whole, vec, ANY],
        out_specs=(pl.BlockSpec((S, LANES), lambda i: (0, FGB)), vec),
        out_shape=(jax.ShapeDtypeStruct(dproj.shape, dproj.dtype), jax.ShapeDtypeStruct((1, LANES), F32)),
        input_output_aliases={3: 0},
        compiler_params=_params(("arbitrary",)),
    )(dcum, fgp, bf, dproj)


def _put_columns(pieces, dst, blocks, sems):
    copies = [pltpu.make_async_copy(p, dst.at[:, pl.ds(pl.multiple_of(b * LANES, LANES), LANES)], sems.at[i])
              for i, (p, b) in enumerate(zip(pieces, blocks))]
    for cp in copies:
        cp.start()
    for cp in copies:
        cp.wait()


def _colblock(S, off):
    return pl.BlockSpec((S, LANES), lambda h: (0, h + off))


def _head_col(S):
    return pl.BlockSpec((None, S, 1), lambda h: (h, 0, 0))


def _head_row(nq, tq):
    return pl.BlockSpec((None, nq, 1, tq), lambda h: (h, 0, 0, 0))


def _lane_of(ref, h):
    lane = lax.broadcasted_iota(jnp.int32, ref.shape, 1)
    return jnp.sum(jnp.where(lane == h, ref[...], 0.0), axis=1, keepdims=True)


def _attn_fwd(qkv, rest, cum, cumr, *, tq):
    S = qkv.shape[0]
    nq = S // tq

    def body(q_ref, k_ref, v_ref, g_ref, cum_ref, cumr_ref, o_ref, y_ref, lse_ref, cc):
        cc[...] = _lane_of(cum_ref, pl.program_id(0)) * LOG2E
        tri = (lax.broadcasted_iota(jnp.int32, (tq, tq), 1) <= lax.broadcasted_iota(jnp.int32, (tq, tq), 0))

        def q_step(qi, _):
            rows = pl.ds(pl.multiple_of(qi * tq, tq), tq)
            q = q_ref[rows, :]
            ci = cc[rows, :]

            def tile(kj, carry, masked):
                m, l, acc = carry
                cols = pl.ds(pl.multiple_of(kj * tq, tq), tq)
                s = lax.dot_general(q, k_ref[cols, :], (((1,), (1,)), ((), ())), preferred_element_type=F32)
                s = s * (SCALE * LOG2E) + ci - cumr_ref[kj]
                if masked:
                    s = jnp.where(tri, s, NEG)
                m_new = jnp.maximum(m, jnp.max(s, axis=1, keepdims=True))
                a = jnp.exp2(m - m_new)
                p = jnp.exp2(s - m_new)
                l = a * l + jnp.sum(p, axis=1, keepdims=True)
                acc = a * acc + jnp.dot(p.astype(BF16), v_ref[cols, :], preferred_element_type=F32)
                return m_new, l, acc

            init = (jnp.full((tq, 1), NEG, F32), jnp.zeros((tq, 1), F32), jnp.zeros((tq, HEAD_DIM), F32))
            carry = lax.fori_loop(0, qi, lambda kj, c: tile(kj, c, False), init)
            m, l, acc = tile(qi, carry, True)
            o = acc / l
            g = g_ref[rows, :]
            o_ref[rows, :] = o
            y_ref[rows, :] = (o * (g * _sigmoid(g))).astype(BF16)
            lse_ref[rows, :] = m * LN2 + jnp.log(l)
            return 0

        lax.fori_loop(0, nq, q_step, 0)

    return pl.pallas_call(
        body, name="attn_fwd", grid=(N_HEADS,),
        in_specs=[_colblock(S, QB), _colblock(S, KB), _colblock(S, VB), _colblock(S, GB),
                  pl.BlockSpec((S, LANES), lambda h: (0, 0)), _head_row(nq, tq)],
        out_specs=(_colblock(S, 0), _colblock(S, 0), _head_col(S)),
        out_shape=(jax.ShapeDtypeStruct((S, ATT_W), F32), jax.ShapeDtypeStruct((S, D_MODEL), BF16),
                   jax.ShapeDtypeStruct((N_HEADS, S, 1), F32)),
        scratch_shapes=[pltpu.VMEM((S, 1), F32)],
        compiler_params=_params(("arbitrary",)),
    )(qkv, qkv, qkv, rest, cum, cumr)


def _attn_bwd_pre(rest, o, dy, cum, lse):
    S = o.shape[0]

    def body(g_ref, o_ref, dy_ref, cum_ref, lse_ref, do_ref, dg_ref, a_ref, dl_ref):
        g = g_ref[...]
        sg = _sigmoid(g)
        ov = o_ref[...]
        dy = dy_ref[...]
        do = dy * (g * sg)
        do_ref[...] = do.astype(BF16)
        dg_ref[...] = (dy * ov * (sg * (1.0 + g * (1.0 - sg)))).astype(BF16)
        dl_ref[...] = jnp.sum(do * ov, axis=1, keepdims=True)
        a_ref[...] = (_lane_of(cum_ref, pl.program_id(0)) - lse_ref[...]) * LOG2E

    att = jax.ShapeDtypeStruct((S, ATT_W), BF16)
    col = jax.ShapeDtypeStruct((N_HEADS, S, 1), F32)
    return pl.pallas_call(
        body, name="attn_bwd_pre", grid=(N_HEADS,),
        in_specs=[_colblock(S, GB), _colblock(S, 0), _colblock(S, 0),
                  pl.BlockSpec((S, LANES), lambda h: (0, 0)), _head_col(S)],
        out_specs=(_colblock(S, 0), _colblock(S, QKV_W // LANES), _head_col(S), _head_col(S)),
        out_shape=(att, jax.ShapeDtypeStruct((S, ALL_W), BF16), col, col),
        compiler_params=_params(("parallel",)),
    )(rest, o, dy, cum, lse)


def _attn_bwd(qkv, do, a, delta, cumr, dproj, *, tq):
    S = qkv.shape[0]
    nq = S // tq
    tdot = (((0,), (0,)), ((), ()))

    def body(q_ref, k_ref, v_ref, do_ref, a_ref, dl_ref, cumr_ref, dproj_in, dproj_ref, dcr_ref, dcc_ref,
             dqa, dq_ref, dk_ref, dv_ref, sems):
        dqa[...] = jnp.zeros_like(dqa)

        @pl.when(pl.program_id(0) == 0)
        def _():
            dcr_ref[...] = jnp.zeros_like(dcr_ref)

        tri = (lax.broadcasted_iota(jnp.int32, (tq, tq), 1) <= lax.broadcasted_iota(jnp.int32, (tq, tq), 0))
        mine = lax.broadcasted_iota(jnp.int32, (tq, LANES), 1) == pl.program_id(0)

        def k_step(kj, _):
            cols = pl.ds(pl.multiple_of(kj * tq, tq), tq)
            k = k_ref[cols, :]
            v = v_ref[cols, :]
            cj = cumr_ref[kj]

            def tile(qi, carry, masked):
                dk, dv, csum = carry
                rows = pl.ds(pl.multiple_of(qi * tq, tq), tq)
                q = q_ref[rows, :]
                dot = do_ref[rows, :]
                s = lax.dot_general(q, k, (((1,), (1,)), ((), ())), preferred_element_type=F32)
                p = jnp.exp2(s * (SCALE * LOG2E) + a_ref[rows, :] - cj)
                if masked:
                    p = jnp.where(tri, p, 0.0)
                dv = dv + lax.dot_general(p.astype(BF16), dot, tdot, preferred_element_type=F32)
                dp = lax.dot_general(dot, v, (((1,), (1,)), ((), ())), preferred_element_type=F32)
                ds = p * (dp - dl_ref[rows, :])
                dcr_ref[rows, :] += jnp.where(mine, jnp.sum(ds, axis=1, keepdims=True), 0.0)
                csum = csum + jnp.sum(ds, axis=0, keepdims=True)
                dsb = (ds * SCALE).astype(BF16)
                dqa[rows, :] += jnp.dot(dsb, k, preferred_element_type=F32)
                dk = dk + lax.dot_general(dsb, q, tdot, preferred_element_type=F32)
                return dk, dv, csum

            init = (jnp.zeros((tq, HEAD_DIM), F32), jnp.zeros((tq, HEAD_DIM), F32), jnp.zeros((1, tq), F32))
            carry = tile(kj, init, True)
            dk, dv, csum = lax.fori_loop(kj + 1, nq, lambda qi, c: tile(qi, c, False), carry)
            dk_ref[cols, :] = dk.astype(BF16)
            dv_ref[cols, :] = dv.astype(BF16)
            dcc_ref[kj] = csum
            return 0

        lax.fori_loop(0, nq, k_step, 0)
        dq_ref[...] = dqa[...].astype(BF16)
        h = pl.program_id(0)
        _put_columns([dq_ref, dk_ref, dv_ref], dproj_ref, [QB + h, KB + h, VB + h], sems)

    head = pltpu.VMEM((S, HEAD_DIM), BF16)
    return pl.pallas_call(
        body, name="attn_bwd", grid=(N_HEADS,),
        in_specs=[_colblock(S, QB), _colblock(S, KB), _colblock(S, VB), _colblock(S, 0),
                  _head_col(S), _head_col(S), _head_row(nq, tq), ANY],
        out_specs=(ANY, pl.BlockSpec((S, LANES), lambda h: (0, 0)), _head_row(nq, tq)),
        out_shape=(jax.ShapeDtypeStruct(dproj.shape, dproj.dtype), jax.ShapeDtypeStruct((S, LANES), F32),
                   jax.ShapeDtypeStruct((N_HEADS, nq, 1, tq), F32)),
        input_output_aliases={7: 0},
        scratch_shapes=[pltpu.VMEM((S, HEAD_DIM), F32), head, head, head, pltpu.SemaphoreType.DMA((3,))],
        compiler_params=_params(("arbitrary",)),
    )(qkv, qkv, qkv, do, a, delta, cumr, dproj)


def _conv_taps(u, w_ref, row):
    return (w_ref[0:1, :] * _shift_down(u, 2, row) + w_ref[1:2, :] * _shift_down(u, 1, row)) + w_ref[2:3, :] * u


def _conv_fwd(rest, cw, y):
    S = rest.shape[0]

    def body(cb_ref, cc_ref, ch_ref, g_ref, w_ref, y_in, y_ref):
        row = lax.broadcasted_iota(jnp.int32, (S, LANES), 0)
        g = g_ref[...]
        y = _conv_taps(cc_ref[...] * ch_ref[...], w_ref, row)
        y_ref[...] = ((cb_ref[...] * y) * (g * _sigmoid(g))).astype(BF16)

    return pl.pallas_call(
        body, name="conv_fwd", grid=(CONV_W // LANES,),
        in_specs=[_colblock(S, CBB), _colblock(S, CCB), _colblock(S, CHB), _colblock(S, CGB),
                  pl.BlockSpec((CONV_TAPS, LANES), lambda j: (0, j)), ANY],
        out_specs=_colblock(S, ATT_W // LANES),
        out_shape=jax.ShapeDtypeStruct(y.shape, y.dtype),
        input_output_aliases={5: 0},
        compiler_params=_params(("parallel",)),
    )(rest, rest, rest, rest, cw, y)


def _conv_bwd(rest, dy, cw, dproj):
    S = rest.shape[0]

    def body(cb_ref, cc_ref, ch_ref, g_ref, dy_ref, w_ref, dproj_in, dproj_ref, dw_ref,
             dcb_ref, dcc_ref, dch_ref, dg_ref, sems):
        row = lax.broadcasted_iota(jnp.int32, (S, LANES), 0)
        g = g_ref[...]
        sg = _sigmoid(g)
        silu = g * sg
        cb, cc, ch, dy = cb_ref[...], cc_ref[...], ch_ref[...], dy_ref[...]
        u = cc * ch
        u1 = _shift_down(u, 1, row)
        u2 = _shift_down(u, 2, row)
        y = (w_ref[0:1, :] * u2 + w_ref[1:2, :] * u1) + w_ref[2:3, :] * u
        dcb_ref[...] = (dy * silu * y).astype(BF16)
        dg_ref[...] = (dy * (cb * y) * (sg * (1.0 + g * (1.0 - sg)))).astype(BF16)
        dyv = dy * silu * cb
        du = w_ref[2:3, :] * dyv + w_ref[1:2, :] * _shift_up(dyv, 1, row) + w_ref[0:1, :] * _shift_up(dyv, 2, row)
        dcc_ref[...] = (du * ch).astype(BF16)
        dch_ref[...] = (du * cc).astype(BF16)
        dw_ref[0:1, :] = jnp.sum(dyv * u2, axis=0, keepdims=True)
        dw_ref[1:2, :] = jnp.sum(dyv * u1, axis=0, keepdims=True)
        dw_ref[2:3, :] = jnp.sum(dyv * u, axis=0, keepdims=True)
        j = pl.program_id(0)
        first = QKV_W // LANES
        _put_columns([dcb_ref, dcc_ref, dch_ref, dg_ref], dproj_ref,
                     [first + CBB + j, first + CCB + j, first + CHB + j, first + CGB + j], sems)

    piece = pltpu.VMEM((S, LANES), BF16)
    return pl.pallas_call(
        body, name="conv_bwd", grid=(CONV_W // LANES,),
        in_specs=[_colblock(S, CBB), _colblock(S, CCB), _colblock(S, CHB), _colblock(S, CGB),
                  _colblock(S, ATT_W // LANES), pl.BlockSpec((CONV_TAPS, LANES), lambda j: (0, j)), ANY],
        out_specs=(ANY, pl.BlockSpec((CONV_TAPS, LANES), lambda j: (0, j))),
        out_shape=(jax.ShapeDtypeStruct(dproj.shape, dproj.dtype), jax.ShapeDtypeStruct((CONV_TAPS, CONV_W), F32)),
        input_output_aliases={6: 0},
        scratch_shapes=[piece] * 4 + [pltpu.SemaphoreType.DMA((4,))],
        compiler_params=_params(("arbitrary",)),
    )(rest, rest, rest, rest, dy, cw, dproj)


def _pool_z(u, grp, row):
    s2 = u + _shift_down(u, 1, row)
    s4 = s2 + _shift_down(s2, 2, row)
    s8 = s4 + _shift_down(s4, 4, row)
    s16 = s8 + _shift_down(s8, 8, row)
    sw = jnp.where(grp == 0, s2, jnp.where(grp == 1, s4, jnp.where(grp == 2, s8, s16)))
    return sw / _pool_count(grp, row) - u


def _pool_count(grp, row):
    return jnp.minimum(row + 1, 2 << grp).astype(F32)


def _pool_fwd(rest, pw, scale, y):
    S = rest.shape[0]

    def body(u_ref, g_ref, pw_ref, sc_ref, y_in, y_ref):
        grp = pl.program_id(0)
        row = lax.broadcasted_iota(jnp.int32, (S, LANES), 0)
        z = _pool_z(u_ref[...], grp, row)
        zp = jnp.dot(z.astype(BF16), pw_ref[...].astype(BF16), preferred_element_type=F32)
        g = g_ref[...]
        y_ref[...] = ((zp * sc_ref[...]) * (g * _sigmoid(g))).astype(BF16)

    return pl.pallas_call(
        body, name="pool_fwd", grid=(N_POOL,),
        in_specs=[_colblock(S, PUB), _colblock(S, PGB),
                  pl.BlockSpec((None, POOL_GROUP, POOL_GROUP), lambda j: (j, 0, 0)),
                  pl.BlockSpec((1, LANES), lambda j: (0, j)), ANY],
        out_specs=_colblock(S, (ATT_W + CONV_W) // LANES),
        out_shape=jax.ShapeDtypeStruct(y.shape, y.dtype),
        input_output_aliases={4: 0},
        compiler_params=_params(("parallel",)),
    )(rest, rest, pw, scale, y)


def _pool_bwd(rest, dy, pw, scale, dproj):
    S = rest.shape[0]

    def body(u_ref, g_ref, dy_ref, pw_ref, sc_ref, dproj_in, dproj_ref, dpw_ref, dsc_ref, du_ref, dg_ref, sems):
        grp = pl.program_id(0)
        row = lax.broadcasted_iota(jnp.int32, (S, LANES), 0)
        z = _pool_z(u_ref[...], grp, row).astype(BF16)
        pwb = pw_ref[...].astype(BF16)
        zp = jnp.dot(z, pwb, preferred_element_type=F32)
        g = g_ref[...]
        sg = _sigmoid(g)
        silu = g * sg
        dy = dy_ref[...]
        sc = sc_ref[...]
        dsc_ref[...] = jnp.sum(dy * silu * zp, axis=0, keepdims=True)
        dg_ref[...] = (dy * (zp * sc) * (sg * (1.0 + g * (1.0 - sg)))).astype(BF16)
        dzp = (dy * silu * sc).astype(BF16)
        dpw_ref[...] = lax.dot_general(z, dzp, (((0,), (0,)), ((), ())), preferred_element_type=F32)
        dz = lax.dot_general(dzp, pwb, (((1,), (1,)), ((), ())), preferred_element_type=F32)
        f1 = dz / _pool_count(grp, row)
        f2 = f1 + _shift_up(f1, 1, row)
        f4 = f2 + _shift_up(f2, 2, row)
        f8 = f4 + _shift_up(f4, 4, row)
        f16 = f8 + _shift_up(f8, 8, row)
        fw = jnp.where(grp == 0, f2, jnp.where(grp == 1, f4, jnp.where(grp == 2, f8, f16)))
        du_ref[...] = (fw - dz).astype(BF16)
        first = QKV_W // LANES
        _put_columns([du_ref, dg_ref], dproj_ref, [first + PUB + grp, first + PGB + grp], sems)

    piece = pltpu.VMEM((S, LANES), BF16)
    return pl.pallas_call(
        body, name="pool_bwd", grid=(N_POOL,),
        in_specs=[_colblock(S, PUB), _colblock(S, PGB), _colblock(S, (ATT_W + CONV_W) // LANES),
                  pl.BlockSpec((None, POOL_GROUP, POOL_GROUP), lambda j: (j, 0, 0)),
                  pl.BlockSpec((1, LANES), lambda j: (0, j)), ANY],
        out_specs=(ANY, pl.BlockSpec((None, POOL_GROUP, POOL_GROUP), lambda j: (j, 0, 0)),
                   pl.BlockSpec((1, LANES), lambda j: (0, j))),
        out_shape=(jax.ShapeDtypeStruct(dproj.shape, dproj.dtype),
                   jax.ShapeDtypeStruct((N_POOL, POOL_GROUP, POOL_GROUP), F32), jax.ShapeDtypeStruct((1, POOL_W), F32)),
        input_output_aliases={5: 0},
        scratch_shapes=[piece] * 2 + [pltpu.SemaphoreType.DMA((2,))],
        compiler_params=_params(("arbitrary",)),
    )(rest, rest, dy, pw, scale, dproj)


def _ln_fwd(z, g, b, *, ts):
    S, D = z.shape

    def body(z_ref, g_ref, b_ref, o_ref, ob_ref, obt_ref, xh_ref, rs_ref):
        zz = z_ref[...]
        mu = jnp.mean(zz, axis=1, keepdims=True)
        zc = zz - mu
        rstd = lax.rsqrt(jnp.mean(zc * zc, axis=1, keepdims=True) + LN_EPS)
        xh = zc * rstd
        out = xh * g_ref[...] + b_ref[...]
        o_ref[...] = out
        ob_ref[...] = out.astype(BF16)
        obt_ref[...] = out.T.astype(BF16)
        xh_ref[...] = xh
        rs_ref[...] = rstd

    rowblk = pl.BlockSpec((ts, D), lambda i: (i, 0))
    vec = pl.BlockSpec((1, D), lambda i: (0, 0))
    return pl.pallas_call(
        body, name="ln_fwd", grid=(S // ts,),
        in_specs=[rowblk, vec, vec],
        out_specs=(rowblk, rowblk, pl.BlockSpec((D, ts), lambda i: (0, i)), rowblk,
                   pl.BlockSpec((ts, 1), lambda i: (i, 0))),
        out_shape=(jax.ShapeDtypeStruct((S, D), F32), jax.ShapeDtypeStruct((S, D), BF16),
                   jax.ShapeDtypeStruct((D, S), BF16), jax.ShapeDtypeStruct((S, D), F32),
                   jax.ShapeDtypeStruct((S, 1), F32)),
        compiler_params=_params(("parallel",)),
    )(z, g, b)


def _prep(x, after, *, ts):
    S, D = x.shape

    def body(x_ref, after_ref, xb_ref, xbt_ref):
        xb_ref[...] = x_ref[...].astype(BF16)
        xbt_ref[...] = x_ref[...].T.astype(BF16)

    return pl.pallas_call(
        body, name="prep", grid=(S // ts,),
        in_specs=[pl.BlockSpec((ts, D), lambda i: (i, 0)), ANY],
        out_specs=(pl.BlockSpec((ts, D), lambda i: (i, 0)), pl.BlockSpec((D, ts), lambda i: (0, i))),
        out_shape=(jax.ShapeDtypeStruct((S, D), BF16), jax.ShapeDtypeStruct((D, S), BF16)),
        compiler_params=_params(("parallel",)),
    )(x, after)


def _ln_bwd(dout, xhat, rstd, g, *, ts):
    S, D = dout.shape

    def body(d_ref, xh_ref, rs_ref, g_ref, dz_ref, dzb_ref, dg_ref, db_ref):
        @pl.when(pl.program_id(0) == 0)
        def _():
            dg_ref[...] = jnp.zeros_like(dg_ref)
            db_ref[...] = jnp.zeros_like(db_ref)

        d = d_ref[...]
        xh = xh_ref[...]
        dxh = d * g_ref[...]
        m1 = jnp.mean(dxh, axis=1, keepdims=True)
        m2 = jnp.mean(dxh * xh, axis=1, keepdims=True)
        dz = rs_ref[...] * (dxh - m1 - xh * m2)
        dz_ref[...] = dz
        dzb_ref[...] = dz.astype(BF16)
        dg_ref[...] += jnp.sum(d * xh, axis=0, keepdims=True)
        db_ref[...] += jnp.sum(d, axis=0, keepdims=True)

    rowblk = pl.BlockSpec((ts, D), lambda i: (i, 0))
    vec = pl.BlockSpec((1, D), lambda i: (0, 0))
    return pl.pallas_call(
        body, name="ln_bwd", grid=(S // ts,),
        in_specs=[rowblk, rowblk, pl.BlockSpec((ts, 1), lambda i: (i, 0)), vec],
        out_specs=(rowblk, rowblk, vec, vec),
        out_shape=(jax.ShapeDtypeStruct((S, D), F32), jax.ShapeDtypeStruct((S, D), BF16),
                   jax.ShapeDtypeStruct((1, D), F32), jax.ShapeDtypeStruct((1, D), F32)),
        compiler_params=_params(("arbitrary",)),
    )(dout, xhat, rstd, g)


def _loss_head(y, tgt, *, ts):
    S, D = y.shape

    def body(y_ref, t_ref, l_ref, dy_ref):
        @pl.when(pl.program_id(0) == 0)
        def _():
            l_ref[...] = jnp.zeros_like(l_ref)

        e = y_ref[...] - t_ref[...]
        dy_ref[...] = e * (1.0 / D)
        rowloss = jnp.sum(e * e, axis=1, keepdims=True) * (0.5 / D)
        l_ref[...] += jnp.sum(rowloss, axis=0, keepdims=True)

    rowblk = pl.BlockSpec((ts, D), lambda i: (i, 0))
    return pl.pallas_call(
        body, name="loss_head", grid=(S // ts,),
        in_specs=[rowblk, rowblk],
        out_specs=(pl.BlockSpec((1, 1), lambda i: (0, 0)), rowblk),
        out_shape=(jax.ShapeDtypeStruct((1, 1), F32), jax.ShapeDtypeStruct((S, D), F32)),
        compiler_params=_params(("arbitrary",)),
    )(y, tgt)


def _adamw(lands, srcs, w, m, v, *, name, first=0, into=None):
    R, C = w.shape
    npart = len(lands)
    rp = lands[0].shape[1]
    assert R % rp == 0 and all(p.shape == (N_DEV, rp, C) for p in lands)
    tiled = rp % 8 == 0
    assert tiled or (first == 0 and npart * rp == R and into is None)
    tr = min(ADAMW_BLOCK_ELEMS // (LANES * pl.cdiv(C, LANES)), rp) if tiled else rp
    assert rp % tr == 0
    steps = rp // tr
    c1 = 1.0 - ADAM_B1 ** ADAM_STEP
    c2 = 1.0 - ADAM_B2 ** ADAM_STEP
    slabbed = [p.ndim == 3 for p in srcs]
    n_into = 0 if into is None else 4

    def body(*refs):
        l_refs, s_refs = refs[:npart], refs[npart:2 * npart]
        w_ref, m_ref, v_ref = refs[2 * npart:2 * npart + 3]
        g_ref, d_ref, nm_ref, nv_ref = refs[2 * npart + 3 + n_into:]
        x, y, c = _my_place()
        me = 4 * x + 2 * y + c

        def update(l, rows):
            g = None
            for j in range(N_DEV):
                own = s_refs[l][j] if slabbed[l] else s_refs[l][...]
                p = jnp.where(me == j, own, l_refs[l][j]).astype(F32)
                g = p if g is None else g + p
            nm = ADAM_B1 * m_ref[rows, :] + (1.0 - ADAM_B1) * g
            nv = ADAM_B2 * v_ref[rows, :] + (1.0 - ADAM_B2) * (g * g)
            g_ref[rows, :] = g
            nm_ref[rows, :] = nm
            nv_ref[rows, :] = nv
            d_ref[rows, :] = -ADAM_LR * ((nm / c1) / (jnp.sqrt(nv / c2) + ADAM_EPS) + ADAM_WD * w_ref[rows, :])

        for l in range(npart):
            if tiled:
                pl.when(pl.program_id(0) == l)(lambda l=l: update(l, slice(None)))
            else:
                update(l, slice(l * rp, (l + 1) * rp))

    def part_spec(l, p):
        if p.ndim == 3:
            return pl.BlockSpec((N_DEV, tr, C), lambda q, i: (0, jnp.where(q == l, i, 0), 0))
        return pl.BlockSpec((tr, C), lambda q, i: (jnp.where(q == l, i, 0), 0))

    if tiled:
        blk = pl.BlockSpec((tr, C), lambda q, i: ((first + q) * steps + i, 0))
    else:
        blk = pl.BlockSpec((R, C), lambda q, i: (0, 0))
    out = jax.ShapeDtypeStruct((R, C), F32)
    n_in = 2 * npart + 3
    return pl.pallas_call(
        body, name=name, grid=(npart, steps) if tiled else (1, 1),
        in_specs=[part_spec(l, p) for l, p in enumerate(lands)] + [part_spec(l, p) for l, p in enumerate(srcs)]
        + [blk, blk, blk] + [ANY] * n_into,
        out_specs=(blk, blk, blk, blk),
        out_shape=(out, out, out, out),
        input_output_aliases={n_in + i: i for i in range(n_into)},
        compiler_params=_params(("arbitrary", "arbitrary")),
    )(*lands, *srcs, w, m, v, *(into or ()))


def _my_place():
    return lax.axis_index("x"), lax.axis_index("y"), lax.axis_index("c")


HBM = pl.BlockSpec(memory_space=pltpu.HBM)
SEM = pl.BlockSpec(memory_space=pltpu.SEMAPHORE)
DATAFLOW = pltpu.SideEffectType.DATAFLOW_SIDE_EFFECTING
EVERYONE = (1, 2, 3, 4, 5, 6, 7)
SIBLING = 1
SAME_CORE = (4, 2, 6)


def _xchg_copies(src, land, send_sem, recv_sem, gather, peers):
    x, y, c = _my_place()
    me = 4 * x + 2 * y + c
    copies = []
    for k, bits in enumerate(peers):
        px, py, pc = x ^ ((bits >> 2) & 1), y ^ ((bits >> 1) & 1), c ^ (bits & 1)
        peer = 4 * px + 2 * py + pc
        for a in range(len(src)):
            i = a * len(peers) + k
            out = src[a] if gather[a] else src[a].at[peer]
            sent, landed = [pltpu.make_async_remote_copy(
                src_ref=out, dst_ref=land[a].at[slab], send_sem=send_sem.at[i], recv_sem=recv_sem.at[i],
                device_id=(px, py, pc), device_id_type=MESH) for slab in (me, peer)]
            copies.append((sent, landed))
    return copies


def _xchg_start(srcs, *, gather, name, peers=EVERYONE, after=None):
    n = len(srcs)
    follows = [] if after is None else [after]
    lands = [lax.empty(((N_DEV,) + v.shape) if g else v.shape, v.dtype) for v, g in zip(srcs, gather)]

    def body(*refs):
        src, land = refs[:n], refs[n:2 * n]
        send_sem, recv_sem = refs[2 * n + len(follows)], refs[2 * n + len(follows) + 1]
        token = refs[-1]
        for sent, _ in _xchg_copies(src, land, send_sem, recv_sem, gather, peers):
            sent.start()
        token[...] = jnp.zeros_like(token)

    sems = pltpu.SemaphoreType.DMA((n * len(peers),))
    outs = pl.pallas_call(
        body, name=name,
        out_shape=(sems, sems, *[pltpu.HBM(v.shape, v.dtype) for v in srcs + lands],
                   jax.ShapeDtypeStruct((8, LANES), F32)),
        in_specs=[HBM] * (2 * n) + [ANY] * len(follows),
        out_specs=(SEM, SEM, *[HBM] * (2 * n), pl.BlockSpec(memory_space=pltpu.VMEM)),
        input_output_aliases={i: 2 + i for i in range(2 * n)},
        compiler_params=pltpu.CompilerParams(has_side_effects=DATAFLOW),
    )(*[pltpu.with_memory_space_constraint(v, pltpu.HBM) for v in srcs + lands], *follows)
    return dict(send=outs[0], recv=outs[1], srcs=list(outs[2:2 + n]), lands=list(outs[2 + n:2 + 2 * n]),
                token=outs[-1], gather=gather, peers=peers)


def _xchg_wait(h, after, *, name):
    n = len(h["srcs"])
    gather, peers = h["gather"], h["peers"]

    def body(*refs):
        src, land = refs[:n], refs[n:2 * n]
        send_sem, recv_sem = refs[2 * n], refs[2 * n + 1]
        for sent, landed in _xchg_copies(src, land, send_sem, recv_sem, gather, peers):
            sent.wait_send()
            landed.wait_recv()

    thru = h["srcs"] + h["lands"]
    outs = pl.pallas_call(
        body, name=name,
        out_shape=[pltpu.HBM(v.shape, v.dtype) for v in thru],
        in_specs=[HBM] * (2 * n) + [SEM, SEM, ANY],
        out_specs=[HBM] * (2 * n),
        input_output_aliases={i: i for i in range(2 * n)},
        compiler_params=pltpu.CompilerParams(has_side_effects=DATAFLOW),
    )(*thru, h["send"], h["recv"], after)
    return list(outs[:n]), list(outs[n:])


def _relay_copies(land, send_sem, recv_sem):
    x, y, c = _my_place()
    me = 4 * x + 2 * y + c
    copies = []
    for k, bits in enumerate(SAME_CORE):
        for a in range(len(land)):
            i = a * len(SAME_CORE) + k
            sent, landed = [pltpu.make_async_remote_copy(
                src_ref=land[a].at[me ^ bits], dst_ref=land[a].at[slab], send_sem=send_sem.at[i], recv_sem=recv_sem.at[i],
                device_id=(x, y, 1 - c), device_id_type=MESH) for slab in (me ^ bits, me ^ bits ^ SIBLING)]
            copies.append((sent, landed))
    return copies


def _relay_start(lands, *, name):
    n = len(lands)

    def body(*refs):
        land = refs[:n]
        send_sem, recv_sem = refs[n], refs[n + 1]
        token = refs[-1]
        for sent, _ in _relay_copies(land, send_sem, recv_sem):
            sent.start()
        token[...] = jnp.zeros_like(token)

    sems = pltpu.SemaphoreType.DMA((n * len(SAME_CORE),))
    outs = pl.pallas_call(
        body, name=name,
        out_shape=(sems, sems, *[pltpu.HBM(v.shape, v.dtype) for v in lands], jax.ShapeDtypeStruct((8, LANES), F32)),
        in_specs=[HBM] * n,
        out_specs=(SEM, SEM, *[HBM] * n, pl.BlockSpec(memory_space=pltpu.VMEM)),
        input_output_aliases={i: 2 + i for i in range(n)},
        compiler_params=pltpu.CompilerParams(has_side_effects=DATAFLOW),
    )(*[pltpu.with_memory_space_constraint(v, pltpu.HBM) for v in lands])
    return dict(send=outs[0], recv=outs[1], lands=list(outs[2:2 + n]), token=outs[-1])


def _relay_wait(h, after, *, name):
    n = len(h["lands"])

    def body(*refs):
        land = refs[:n]
        send_sem, recv_sem = refs[n], refs[n + 1]
        for sent, landed in _relay_copies(land, send_sem, recv_sem):
            sent.wait_send()
            landed.wait_recv()

    outs = pl.pallas_call(
        body, name=name,
        out_shape=[pltpu.HBM(v.shape, v.dtype) for v in h["lands"]],
        in_specs=[HBM] * n + [SEM, SEM, ANY],
        out_specs=[HBM] * n,
        input_output_aliases={i: i for i in range(n)},
        compiler_params=pltpu.CompilerParams(has_side_effects=DATAFLOW),
    )(*h["lands"], h["send"], h["recv"], after)
    return list(outs)


def _with_own(land, own, me):
    slot = lax.broadcasted_iota(jnp.int32, land.shape, 0)
    return jnp.where(slot == me, own if own.ndim == land.ndim else own[None], land)


def _w_in_segments(j, cols):
    lo, hi = j * cols, (j + 1) * cols
    segs = []
    for a, b, where, shift in ((0, FG_AT, "main", 0), (FG_AT, FG_AT + N_HEADS, "fg", -FG_AT),
                               (FG_AT + N_HEADS, IN_W, "main", -N_HEADS)):
        s0, s1 = max(lo, a), min(hi, b)
        if s0 < s1:
            segs.append((s0 - lo, s1 - s0, where, s0 + shift))
    return segs


SLAB_W = 1024


def _window(off, width):
    base = min((off // LANES) * LANES, width - SLAB_W)
    return base, off - base


def _w_unpack(land, own, *, tr):
    _, D, cols = land.shape
    assert cols + LANES - 1 <= SLAB_W

    def body(land_ref, own_ref, main_ref, slab):
        x, y, c = _my_place()
        me = 4 * x + 2 * y + c
        lane = lax.broadcasted_iota(jnp.int32, (tr, SLAB_W), 1)
        main_ref[...] = jnp.zeros_like(main_ref)
        slab[...] = jnp.zeros_like(slab)
        for j in range(N_DEV):
            slab[:, :cols] = jnp.where(me == j, own_ref[...], land_ref[j])
            v = slab[...]
            segs = _w_in_segments(j, cols)
            for lo, n, where, dst in segs:
                part = v if len(segs) == 1 else jnp.where((lane >= lo) & (lane < lo + n), v, jnp.zeros_like(v))
                if where == "fg":
                    main_ref[:, MAIN_W:] = pltpu.roll(part, (SLAB_W - lo) % SLAB_W, 1)[:, :LANES]
                else:
                    base, r = _window(dst - lo, MAIN_W)
                    assert 0 <= r and lo + n + r <= SLAB_W
                    main_ref[:, base:base + SLAB_W] += pltpu.roll(part, r, 1)

    return pl.pallas_call(
        body, name="w_unpack", grid=(D // tr,),
        in_specs=[pl.BlockSpec((N_DEV, tr, cols), lambda i: (0, i, 0)), pl.BlockSpec((tr, cols), lambda i: (i, 0))],
        out_specs=pl.BlockSpec((tr, ALL_W), lambda i: (i, 0)),
        out_shape=jax.ShapeDtypeStruct((D, ALL_W), BF16),
        scratch_shapes=[pltpu.VMEM((tr, SLAB_W), BF16)],
        compiler_params=_params(("parallel",)),
    )(land, own)


def _g_pack(dw_main, dw_fg, cols, *, tr):
    D = dw_main.shape[0]

    def body(main_ref, fg_ref, out_ref, slab):
        lane = lax.broadcasted_iota(jnp.int32, (tr, SLAB_W), 1)
        slab[...] = jnp.zeros_like(slab)
        slab[:, :LANES] = fg_ref[...]
        fg = slab[...]
        for j in range(N_DEV):
            v = None
            for lo, n, where, src in _w_in_segments(j, cols):
                if where == "fg":
                    part = pltpu.roll(fg, lo, 1)
                else:
                    base, r = _window(src - lo, MAIN_W)
                    assert 0 <= r and lo + n + r <= SLAB_W
                    part = pltpu.roll(main_ref[:, base:base + SLAB_W], (SLAB_W - r) % SLAB_W, 1)
                v = part if v is None else jnp.where((lane >= lo) & (lane < lo + n), part, v)
            out_ref[j] = v[:, :cols]

    return pl.pallas_call(
        body, name="g_pack", grid=(D // tr,),
        in_specs=[pl.BlockSpec((tr, MAIN_W), lambda i: (i, 0)), pl.BlockSpec((tr, LANES), lambda i: (i, 0))],
        out_specs=pl.BlockSpec((N_DEV, tr, cols), lambda i: (0, i, 0)),
        out_shape=jax.ShapeDtypeStruct((N_DEV, D, cols), BF16),
        scratch_shapes=[pltpu.VMEM((tr, SLAB_W), BF16)],
        compiler_params=_params(("parallel",)),
    )(dw_main, dw_fg)


TQ = 512
TS = 256
W_TILE_ROWS = 128


def _layer_fwd_proj(xb, w_in):
    qkv = _mm(xb, w_in, out_dtype=BF16, tm=2048, tn=512, tk=D_MODEL, name="mm_qkv", n_cols=QKV_W)
    rest = _mm(xb, w_in, out_dtype=F32, tm=2048, tn=512, tk=D_MODEL, name="mm_rest", n_off=QKV_W, n_cols=REST_W)
    fgp = _mm(xb, w_in, out_dtype=F32, tm=1024, tn=LANES, tk=D_MODEL, name="mm_fg", n_off=MAIN_W, n_cols=LANES)
    return qkv, rest, fgp


def _layer_fwd_mix(x, xbt, proj, wl):
    qkv, rest, fgp = proj
    cum = _fg_fwd(fgp, wl["b_f"])
    S = x.shape[0]
    cumr = (cum[:, :N_HEADS].T * LOG2E).reshape(N_HEADS, S // TQ, 1, TQ)
    o, y, lse = _attn_fwd(qkv, rest, cum, cumr, tq=TQ)
    y = _conv_fwd(rest, wl["conv_w"], y)
    y = _pool_fwd(rest, wl["pool_w"], wl["pool_scale"], y)
    z = _mm(y, wl["w_out"], out_dtype=F32, tm=2048, tn=512, tk=D_MODEL, name="mm_out", c=x, c_scale=ALPHA)
    out, outb, outbt, xhat, rstd = _ln_fwd(z, wl["ln_g"], wl["ln_b"], ts=TS)
    saved = dict(xbt=xbt, qkv=qkv, rest=rest, fgp=fgp, cum=cum, cumr=cumr, o=o, lse=lse, y=y, xhat=xhat, rstd=rstd)
    return out, outb, outbt, saved


def _layer_bwd_mix(dout, sv, wl):
    S = dout.shape[0]
    dz, dzb, dln_g, dln_b = _ln_bwd(dout, sv["xhat"], sv["rstd"], wl["ln_g"], ts=TS)
    dy = _mm(dzb, wl["w_out"], out_dtype=F32, tm=2048, tn=512, tk=D_MODEL, name="mm_dy", bt=True)
    dw_out = _mm(sv["y"], dzb, out_dtype=BF16, tm=512, tn=D_MODEL, tk=2048, name="mm_dw_out", at=True)
    rest = sv["rest"]
    do, dproj, a, delta = _attn_bwd_pre(rest, sv["o"], dy, sv["cum"], sv["lse"])
    dproj, dcr, dcc = _attn_bwd(sv["qkv"], do, a, delta, sv["cumr"], dproj, tq=TQ)
    dcum = dcr - jnp.pad(dcc.reshape(N_HEADS, S).T, ((0, 0), (0, LANES - N_HEADS)))
    dproj, db_f = _fg_bwd(dcum, sv["fgp"], wl["b_f"], dproj)
    dproj, dconv_w = _conv_bwd(rest, dy, wl["conv_w"], dproj)
    dproj, dpool_w, dpool_scale = _pool_bwd(rest, dy, wl["pool_w"], wl["pool_scale"], dproj)
    grads = dict(b_f=db_f[:, :N_HEADS], conv_w=dconv_w, pool_w=dpool_w, pool_scale=dpool_scale, w_out=dw_out,
                 ln_g=dln_g, ln_b=dln_b)
    return grads, (dproj, dz)


def _layer_bwd_w_in(mid, sv):
    dproj, dz = mid
    xt = sv["xbt"]
    S = xt.shape[1]
    dw_main = _mm(xt, dproj, out_dtype=BF16, tm=1024, tn=512, tk=S, name="mm_dw_main", n_cols=MAIN_W)
    dw_fg = _mm(xt, dproj, out_dtype=BF16, tm=1024, tn=LANES, tk=2048, name="mm_dw_fg", n_off=MAIN_W, n_cols=LANES)
    return (dw_main, dw_fg), (dproj, dz)


def _layer_bwd_input(ctx, wl, after):
    dproj, dz = ctx
    return _mm(dproj, wl["w_in"], out_dtype=F32, tm=1024, tn=256, tk=ALL_W, name="mm_dx", c=dz, c_scale=ALPHA, bt=True,
               after=after)


def kernel(x, w_in, b_f, conv_w, pool_w, pool_scale, w_out, ln_g, ln_b, loss_target, m_w_in, m_b_f, m_conv_w, m_pool_w, m_pool_scale, m_w_out, m_ln_g, m_ln_b, v_w_in, v_b_f, v_conv_w, v_pool_w, v_pool_scale, v_w_out, v_ln_g, v_ln_b):
    L, D, cols = w_in.shape
    rows_out = w_out.shape[1]
    ccols = conv_w.shape[2]
    mx, my, mc = _my_place()
    me = 4 * mx + 2 * my + mc

    w_in_b, w_out_b = w_in.astype(BF16), w_out.astype(BF16)
    gathers = []
    for l in range(L):
        gathers.append(_xchg_start([w_in_b[l], w_out_b[l], conv_w[l]], gather=[True] * 3, name=f"gather_start_{l}",
                                   peers=(SIBLING,) + SAME_CORE if l == 0 else EVERYONE,
                                   after=gathers[-1]["token"] if gathers else None))
    started = gathers[-1]["token"]

    xl = x[0]
    xb, xbt = _prep(xl, started, ts=TS)
    weights, saved = [], []
    for l in range(L):
        (own_in, own_out, own_cw), (g_in, g_out, g_cw) = _xchg_wait(gathers[l], xb, name=f"gather_wait_{l}")
        if l == 0:
            g_in, g_out, g_cw = _relay_wait(_relay_start([g_in, g_out, g_cw], name="gather_relay_start"), xb,
                                            name="gather_relay_wait")
        w_all = _w_unpack(g_in, own_in, tr=W_TILE_ROWS)
        proj = _layer_fwd_proj(xb, w_all)
        w_o = _with_own(g_out, own_out, me).reshape(N_DEV * rows_out, D)
        cw = jnp.transpose(_with_own(g_cw, own_cw, me), (1, 0, 2)).reshape(CONV_TAPS, N_DEV * ccols)
        wl = dict(w_in=w_all, w_out=w_o, conv_w=cw,
                  b_f=jnp.pad(b_f[l][None, :], ((0, 0), (0, LANES - N_HEADS))), pool_w=pool_w[l],
                  pool_scale=pool_scale[l][None, :], ln_g=ln_g[l][None, :], ln_b=ln_b[l][None, :])
        xl, xb, xbt, sv = _layer_fwd_mix(xl, xbt, proj, wl)
        weights.append(wl)
        saved.append(sv)

    loss, dx = _loss_head(xl, loss_target[0], ts=TS)
    loss = lax.psum(loss[0, 0], ("x", "y", "c"))

    names = ["w_in", "w_out", "conv_w", "b_f", "pool_w", "pool_scale", "ln_g", "ln_b"]
    sharded = names[:3]
    scatters = [None] * L
    for l in reversed(range(L)):
        g, mid = _layer_bwd_mix(dx, saved[l], weights[l])
        (dw_main, dw_fg), ctx = _layer_bwd_w_in(mid, saved[l])
        g["w_in"] = _g_pack(dw_main, dw_fg, cols, tr=W_TILE_ROWS)
        g["w_out"] = g["w_out"].reshape(N_DEV, rows_out, D)
        g["conv_w"] = jnp.transpose(g["conv_w"].reshape(CONV_TAPS, N_DEV, ccols), (1, 0, 2))
        scatters[l] = _xchg_start([g[k] for k in names], gather=[k not in sharded for k in names],
                                  name=f"scatter_start_{l}")
        dx = _layer_bwd_input(ctx, weights[l], scatters[l]["token"])

    given = dict(w_in=(w_in, m_w_in, v_w_in), b_f=(b_f, m_b_f, v_b_f), conv_w=(conv_w, m_conv_w, v_conv_w),
                 pool_w=(pool_w, m_pool_w, v_pool_w), pool_scale=(pool_scale, m_pool_scale, v_pool_scale),
                 w_out=(w_out, m_w_out, v_w_out), ln_g=(ln_g, m_ln_g, v_ln_g), ln_b=(ln_b, m_ln_b, v_ln_b))
    sent, landed = [{} for _ in range(L)], [{} for _ in range(L)]

    def wait(l, after):
        s_, l_ = _xchg_wait(scatters[l], after, name=f"scatter_wait_{l}")
        sent[l].update(zip(names, s_))
        landed[l].update(zip(names, l_))

    def update(k, layers, into=None):
        w, m, v = given[k]
        C = w.shape[-1]
        w2, m2, v2 = (t.reshape(-1, C) for t in (w, m, v))
        rp = w2.shape[0] // L
        lands = [landed[l][k].reshape(N_DEV, rp, C) for l in layers]
        srcs = [sent[l][k].reshape((N_DEV, rp, C) if k in sharded else (rp, C)) for l in layers]
        return _adamw(lands, srcs, w2, m2, v2, name=f"adamw_{k}_{layers[0]}", first=layers[0], into=into)

    big = ["w_in", "w_out", "pool_w"]
    for l in reversed(range(1, L)):
        wait(l, dx)
    upper = {k: update(k, list(range(1, L))) for k in big} if L > 1 else {}
    wait(0, upper["w_in"][0] if upper else dx)
    res = {k: update(k, [0], into=upper[k]) if k in upper else update(k, list(range(L))) for k in names}
    res = {k: [t.reshape(given[k][0].shape) for t in outs] for k, outs in res.items()}

    order = ["w_in", "b_f", "conv_w", "pool_w", "pool_scale", "w_out", "ln_g", "ln_b"]
    return (loss, dx[None], *[res[k][0] for k in order], *[res[k][1] for k in order],
            *[res[k][2] for k in order], *[res[k][3] for k in order])
```

```python
import jax
import jax.numpy as jnp
from jax import lax
from jax.experimental import pallas as pl
from jax.experimental.pallas import tpu as pltpu

F32 = jnp.float32
BF16 = jnp.bfloat16

N_DEV = 8
D_MODEL = 2048
N_HEADS = 8
HEAD_DIM = 128
ATT_W = N_HEADS * HEAD_DIM
CONV_W = 512
CONV_TAPS = 3
POOL_W = 512
POOL_GROUP = 128
N_POOL = POOL_W // POOL_GROUP
FG_AT = 4 * ATT_W
IN_W = FG_AT + N_HEADS + 4 * CONV_W + 2 * POOL_W
MAIN_W = IN_W - N_HEADS
LANES = 128
ALL_W = MAIN_W + LANES
LN_EPS = 1e-5
DEPTH = 4
ALPHA = (2 * DEPTH) ** 0.25
SCALE = HEAD_DIM ** -0.5
LOG2E = 1.4426950408889634
LN2 = 0.6931471805599453
NEG = -1e30

ADAM_LR, ADAM_B1, ADAM_B2, ADAM_EPS, ADAM_WD, ADAM_STEP = 0.001, 0.9, 0.999, 1e-08, 0.01, 10

QKV_W = 3 * ATT_W
REST_W = MAIN_W - QKV_W
QB, KB, VB = 0, 8, 16
GB = 0
CBB, CCB, CHB, CGB = 8, 12, 16, 20
PUB, PGB = 24, 28
FGB = MAIN_W // LANES
VMEM_LIMIT = 48 * 1024 * 1024
ADAMW_BLOCK_ELEMS = 64 * 1024

MESH = pl.DeviceIdType.MESH
ANY = pl.BlockSpec(memory_space=pl.ANY)


def _params(semantics):
    return pltpu.CompilerParams(dimension_semantics=semantics, vmem_limit_bytes=VMEM_LIMIT)


def _sigmoid(x):
    return 1.0 / (1.0 + jnp.exp(-x))


def _mm(a, b, *, out_dtype, tm, tn, tk, name, c=None, c_scale=1.0, n_off=0, n_cols=None, bt=False, at=False,
        after=None):
    M, K = a.shape[::-1] if at else a.shape
    N = b.shape[0 if bt else 1] if n_cols is None else n_cols
    assert not (bt and n_off)
    dims = (((0 if at else 1,), (1 if bt else 0,)), ((), ()))
    tm, tn, tk = min(tm, M), min(tn, N), min(tk, K)
    assert M % tm == 0 and N % tn == 0 and K % tk == 0 and n_off % tn == 0, (a.shape, b.shape, tm, tn, tk)
    nk, joff = K // tk, n_off // tn
    has_c, has_after = c is not None, after is not None

    def body(*refs):
        a_ref, b_ref = refs[0], refs[1]
        c_ref = refs[2] if has_c else None
        o_ref = refs[2 + has_c + has_after]

        def finish(r):
            if has_c:
                r = r + c_scale * c_ref[...]
            o_ref[...] = r.astype(out_dtype)

        if nk == 1:
            finish(lax.dot_general(a_ref[...], b_ref[...], dims, preferred_element_type=F32))
        else:
            acc_ref = refs[3 + has_c + has_after]
            k = pl.program_id(2)

            @pl.when(k == 0)
            def _():
                acc_ref[...] = jnp.zeros_like(acc_ref)

            acc_ref[...] += lax.dot_general(a_ref[...], b_ref[...], dims, preferred_element_type=F32)

            @pl.when(k == nk - 1)
            def _():
                finish(acc_ref[...])

    in_specs = [pl.BlockSpec((tk, tm), lambda i, j, k: (k, i)) if at else pl.BlockSpec((tm, tk), lambda i, j, k: (i, k)),
                pl.BlockSpec((tn, tk), lambda i, j, k: (j, k)) if bt
                else pl.BlockSpec((tk, tn), lambda i, j, k: (k, j + joff))]
    args = [a, b]
    if has_c:
        in_specs.append(pl.BlockSpec((tm, tn), lambda i, j, k: (i, j)))
        args.append(c)
    if has_after:
        in_specs.append(ANY)
        args.append(after)
    return pl.pallas_call(
        body, name=name, grid=(M // tm, N // tn, nk),
        in_specs=in_specs,
        out_specs=pl.BlockSpec((tm, tn), lambda i, j, k: (i, j)),
        out_shape=jax.ShapeDtypeStruct((M, N), out_dtype),
        scratch_shapes=[pltpu.VMEM((tm, tn), F32)] if nk > 1 else [],
        compiler_params=_params(("parallel", "parallel", "arbitrary")),
    )(*args)


def _shift_down(x, k, row):
    return jnp.where(row >= k, pltpu.roll(x, k, 0), 0.0)


def _shift_up(x, k, row):
    s = x.shape[0]
    return jnp.where(row < s - k, pltpu.roll(x, s - k, 0), 0.0)


def _fg_fwd(fgp, bf):
    S = fgp.shape[0]

    def body(f_ref, b_ref, cum_ref):
        a = f_ref[...] + b_ref[...]
        x = jnp.minimum(a, 0.0) - jnp.log(1.0 + jnp.exp(-jnp.abs(a)))
        row = lax.broadcasted_iota(jnp.int32, x.shape, 0)
        k = 1
        while k < S:
            x = x + _shift_down(x, k, row)
            k *= 2
        cum_ref[...] = x

    return pl.pallas_call(body, name="fg_fwd", out_shape=jax.ShapeDtypeStruct((S, LANES), F32),
                          compiler_params=_params(None))(fgp, bf)


def _fg_bwd(dcum, fgp, bf, dproj):
    S = fgp.shape[0]

    def body(d_ref, f_ref, b_ref, dproj_in, dfg_ref, dbf_ref):
        x = d_ref[...]
        row = lax.broadcasted_iota(jnp.int32, x.shape, 0)
        k = 1
        while k < S:
            x = x + _shift_up(x, k, row)
            k *= 2
        dfg = x * _sigmoid(-(f_ref[...] + b_ref[...]))
        dfg_ref[...] = dfg.astype(BF16)
        dbf_ref[...] = jnp.sum(dfg, axis=0, keepdims=True)

    whole = pl.BlockSpec((S, LANES), lambda i: (0, 0))
    vec = pl.BlockSpec((1, LANES), lambda i: (0, 0))
    return pl.pallas_call(
        body, name="fg_bwd", grid=(1,),
        in_specs=[whole, whole, vec, ANY],
        out_specs=(pl.BlockSpec((S, LANES), lambda i: (0, FGB)), vec),
        out_shape=(jax.ShapeDtypeStruct(dproj.shape, dproj.dtype), jax.ShapeDtypeStruct((1, LANES), F32)),
        input_output_aliases={3: 0},
        compiler_params=_params(("arbitrary",)),
    )(dcum, fgp, bf, dproj)


def _put_columns(pieces, dst, blocks, sems):
    copies = [pltpu.make_async_copy(p, dst.at[:, pl.ds(pl.multiple_of(b * LANES, LANES), LANES)], sems.at[i])
              for i, (p, b) in enumerate(zip(pieces, blocks))]
    for cp in copies:
        cp.start()
    for cp in copies:
        cp.wait()


def _colblock(S, off):
    return pl.BlockSpec((S, LANES), lambda h: (0, h + off))


def _head_col(S):
    return pl.BlockSpec((None, S, 1), lambda h: (h, 0, 0))


def _head_row(nq, tq):
    return pl.BlockSpec((None, nq, 1, tq), lambda h: (h, 0, 0, 0))


def _lane_of(ref, h):
    lane = lax.broadcasted_iota(jnp.int32, ref.shape, 1)
    return jnp.sum(jnp.where(lane == h, ref[...], 0.0), axis=1, keepdims=True)


def _attn_fwd(qkv, rest, cum, cumr, *, tq):
    S = qkv.shape[0]
    nq = S // tq

    def body(q_ref, k_ref, v_ref, g_ref, cum_ref, cumr_ref, o_ref, y_ref, lse_ref, cc):
        cc[...] = _lane_of(cum_ref, pl.program_id(0)) * LOG2E
        tri = (lax.broadcasted_iota(jnp.int32, (tq, tq), 0) <= lax.broadcasted_iota(jnp.int32, (tq, tq), 1))

        def q_step(qi, _):
            rows = pl.ds(pl.multiple_of(qi * tq, tq), tq)
            q = q_ref[rows, :]
            ci = cumr_ref[qi]

            def tile(kj, carry, masked):
                m, l, acc = carry
                cols = pl.ds(pl.multiple_of(kj * tq, tq), tq)
                s = lax.dot_general(k_ref[cols, :], q, (((1,), (1,)), ((), ())), preferred_element_type=F32)
                s = s * (SCALE * LOG2E) + ci - cc[cols, :]
                if masked:
                    s = jnp.where(tri, s, NEG)
                m_new = jnp.maximum(m, jnp.max(s, axis=0, keepdims=True))
                a = jnp.exp2(m - m_new)
                p = jnp.exp2(s - m_new)
                l = a * l + jnp.sum(p, axis=0, keepdims=True)
                acc = a * acc + lax.dot_general(v_ref[cols, :], p.astype(BF16), (((0,), (0,)), ((), ())),
                                                preferred_element_type=F32)
                return m_new, l, acc

            init = (jnp.full((1, tq), NEG, F32), jnp.zeros((1, tq), F32), jnp.zeros((HEAD_DIM, tq), F32))
            carry = lax.fori_loop(0, qi, lambda kj, c: tile(kj, c, False), init)
            m, l, acc = tile(qi, carry, True)
            o = (acc / l).T
            g = g_ref[rows, :]
            o_ref[rows, :] = o
            y_ref[rows, :] = (o * (g * _sigmoid(g))).astype(BF16)
            lse_ref[qi] = m * LN2 + jnp.log(l)
            return 0

        lax.fori_loop(0, nq, q_step, 0)

    o, y, lse_rows = pl.pallas_call(
        body, name="attn_fwd", grid=(N_HEADS,),
        in_specs=[_colblock(S, QB), _colblock(S, KB), _colblock(S, VB), _colblock(S, GB),
                  pl.BlockSpec((S, LANES), lambda h: (0, 0)), _head_row(nq, tq)],
        out_specs=(_colblock(S, 0), _colblock(S, 0), _head_row(nq, tq)),
        out_shape=(jax.ShapeDtypeStruct((S, ATT_W), F32), jax.ShapeDtypeStruct((S, D_MODEL), BF16),
                   jax.ShapeDtypeStruct((N_HEADS, nq, 1, tq), F32)),
        scratch_shapes=[pltpu.VMEM((S, 1), F32)],
        compiler_params=_params(("arbitrary",)),
    )(qkv, qkv, qkv, rest, cum, cumr)
    return o, y, lse_rows.reshape(N_HEADS, S, 1)


def _attn_bwd_pre(rest, o, dy, cum, lse):
    S = o.shape[0]

    def body(g_ref, o_ref, dy_ref, cum_ref, lse_ref, do_ref, dg_ref, a_ref, dl_ref):
        g = g_ref[...]
        sg = _sigmoid(g)
        ov = o_ref[...]
        dy = dy_ref[...]
        do = dy * (g * sg)
        do_ref[...] = do.astype(BF16)
        dg_ref[...] = (dy * ov * (sg * (1.0 + g * (1.0 - sg)))).astype(BF16)
        dl_ref[...] = jnp.sum(do * ov, axis=1, keepdims=True)
        a_ref[...] = (_lane_of(cum_ref, pl.program_id(0)) - lse_ref[...]) * LOG2E

    att = jax.ShapeDtypeStruct((S, ATT_W), BF16)
    col = jax.ShapeDtypeStruct((N_HEADS, S, 1), F32)
    return pl.pallas_call(
        body, name="attn_bwd_pre", grid=(N_HEADS,),
        in_specs=[_colblock(S, GB), _colblock(S, 0), _colblock(S, 0),
                  pl.BlockSpec((S, LANES), lambda h: (0, 0)), _head_col(S)],
        out_specs=(_colblock(S, 0), _colblock(S, QKV_W // LANES), _head_col(S), _head_col(S)),
        out_shape=(att, jax.ShapeDtypeStruct((S, ALL_W), BF16), col, col),
        compiler_params=_params(("parallel",)),
    )(rest, o, dy, cum, lse)


def _attn_bwd(qkv, do, a, delta, cumr, dproj, *, tq):
    S = qkv.shape[0]
    nq = S // tq
    tdot = (((0,), (0,)), ((), ()))

    def body(q_ref, k_ref, v_ref, do_ref, a_ref, dl_ref, cumr_ref, dproj_in, dproj_ref, dcr_ref, dcc_ref,
             dqa, dq_ref, dk_ref, dv_ref, sems):
        dqa[...] = jnp.zeros_like(dqa)

        @pl.when(pl.program_id(0) == 0)
        def _():
            dcr_ref[...] = jnp.zeros_like(dcr_ref)

        tri = (lax.broadcasted_iota(jnp.int32, (tq, tq), 1) <= lax.broadcasted_iota(jnp.int32, (tq, tq), 0))
        mine = lax.broadcasted_iota(jnp.int32, (tq, LANES), 1) == pl.program_id(0)

        def k_step(kj, _):
            cols = pl.ds(pl.multiple_of(kj * tq, tq), tq)
            k = k_ref[cols, :]
            v = v_ref[cols, :]
            cj = cumr_ref[kj]

            def tile(qi, carry, masked):
                dk, dv, csum = carry
                rows = pl.ds(pl.multiple_of(qi * tq, tq), tq)
                q = q_ref[rows, :]
                dot = do_ref[rows, :]
                s = lax.dot_general(q, k, (((1,), (1,)), ((), ())), preferred_element_type=F32)
                p = jnp.exp2(s * (SCALE * LOG2E) + a_ref[rows, :] - cj)
                if masked:
                    p = jnp.where(tri, p, 0.0)
                dv = dv + lax.dot_general(p.astype(BF16), dot, tdot, preferred_element_type=F32)
                dp = lax.dot_general(dot, v, (((1,), (1,)), ((), ())), preferred_element_type=F32)
                ds = p * (dp - dl_ref[rows, :])
                dcr_ref[rows, :] += jnp.where(mine, jnp.sum(ds, axis=1, keepdims=True), 0.0)
                csum = csum + jnp.sum(ds, axis=0, keepdims=True)
                dsb = (ds * SCALE).astype(BF16)
                dqa[rows, :] += jnp.dot(dsb, k, preferred_element_type=F32)
                dk = dk + lax.dot_general(dsb, q, tdot, preferred_element_type=F32)
                return dk, dv, csum

            init = (jnp.zeros((tq, HEAD_DIM), F32), jnp.zeros((tq, HEAD_DIM), F32), jnp.zeros((1, tq), F32))
            carry = tile(kj, init, True)
            dk, dv, csum = lax.fori_loop(kj + 1, nq, lambda qi, c: tile(qi, c, False), carry)
            dk_ref[cols, :] = dk.astype(BF16)
            dv_ref[cols, :] = dv.astype(BF16)
            dcc_ref[kj] = csum
            return 0

        lax.fori_loop(0, nq, k_step, 0)
        dq_ref[...] = dqa[...].astype(BF16)
        h = pl.program_id(0)
        _put_columns([dq_ref, dk_ref, dv_ref], dproj_ref, [QB + h, KB + h, VB + h], sems)

    head = pltpu.VMEM((S, HEAD_DIM), BF16)
    return pl.pallas_call(
        body, name="attn_bwd", grid=(N_HEADS,),
        in_specs=[_colblock(S, QB), _colblock(S, KB), _colblock(S, VB), _colblock(S, 0),
                  _head_col(S), _head_col(S), _head_row(nq, tq), ANY],
        out_specs=(ANY, pl.BlockSpec((S, LANES), lambda h: (0, 0)), _head_row(nq, tq)),
        out_shape=(jax.ShapeDtypeStruct(dproj.shape, dproj.dtype), jax.ShapeDtypeStruct((S, LANES), F32),
                   jax.ShapeDtypeStruct((N_HEADS, nq, 1, tq), F32)),
        input_output_aliases={7: 0},
        scratch_shapes=[pltpu.VMEM((S, HEAD_DIM), F32), head, head, head, pltpu.SemaphoreType.DMA((3,))],
        compiler_params=_params(("arbitrary",)),
    )(qkv, qkv, qkv, do, a, delta, cumr, dproj)


def _conv_taps(u, w_ref, row):
    return (w_ref[0:1, :] * _shift_down(u, 2, row) + w_ref[1:2, :] * _shift_down(u, 1, row)) + w_ref[2:3, :] * u


def _conv_fwd(rest, cw, y):
    S = rest.shape[0]

    def body(cb_ref, cc_ref, ch_ref, g_ref, w_ref, y_in, y_ref):
        row = lax.broadcasted_iota(jnp.int32, (S, LANES), 0)
        g = g_ref[...]
        y = _conv_taps(cc_ref[...] * ch_ref[...], w_ref, row)
        y_ref[...] = ((cb_ref[...] * y) * (g * _sigmoid(g))).astype(BF16)

    return pl.pallas_call(
        body, name="conv_fwd", grid=(CONV_W // LANES,),
        in_specs=[_colblock(S, CBB), _colblock(S, CCB), _colblock(S, CHB), _colblock(S, CGB),
                  pl.BlockSpec((CONV_TAPS, LANES), lambda j: (0, j)), ANY],
        out_specs=_colblock(S, ATT_W // LANES),
        out_shape=jax.ShapeDtypeStruct(y.shape, y.dtype),
        input_output_aliases={5: 0},
        compiler_params=_params(("parallel",)),
    )(rest, rest, rest, rest, cw, y)


def _conv_bwd(rest, dy, cw, dproj):
    S = rest.shape[0]

    def body(cb_ref, cc_ref, ch_ref, g_ref, dy_ref, w_ref, dproj_in, dproj_ref, dw_ref,
             dcb_ref, dcc_ref, dch_ref, dg_ref, sems):
        row = lax.broadcasted_iota(jnp.int32, (S, LANES), 0)
        g = g_ref[...]
        sg = _sigmoid(g)
        silu = g * sg
        cb, cc, ch, dy = cb_ref[...], cc_ref[...], ch_ref[...], dy_ref[...]
        u = cc * ch
        u1 = _shift_down(u, 1, row)
        u2 = _shift_down(u, 2, row)
        y = (w_ref[0:1, :] * u2 + w_ref[1:2, :] * u1) + w_ref[2:3, :] * u
        dcb_ref[...] = (dy * silu * y).astype(BF16)
        dg_ref[...] = (dy * (cb * y) * (sg * (1.0 + g * (1.0 - sg)))).astype(BF16)
        dyv = dy * silu * cb
        du = w_ref[2:3, :] * dyv + w_ref[1:2, :] * _shift_up(dyv, 1, row) + w_ref[0:1, :] * _shift_up(dyv, 2, row)
        dcc_ref[...] = (du * ch).astype(BF16)
        dch_ref[...] = (du * cc).astype(BF16)
        dw_ref[0:1, :] = jnp.sum(dyv * u2, axis=0, keepdims=True)
        dw_ref[1:2, :] = jnp.sum(dyv * u1, axis=0, keepdims=True)
        dw_ref[2:3, :] = jnp.sum(dyv * u, axis=0, keepdims=True)
        j = pl.program_id(0)
        first = QKV_W // LANES
        _put_columns([dcb_ref, dcc_ref, dch_ref, dg_ref], dproj_ref,
                     [first + CBB + j, first + CCB + j, first + CHB + j, first + CGB + j], sems)

    piece = pltpu.VMEM((S, LANES), BF16)
    return pl.pallas_call(
        body, name="conv_bwd", grid=(CONV_W // LANES,),
        in_specs=[_colblock(S, CBB), _colblock(S, CCB), _colblock(S, CHB), _colblock(S, CGB),
                  _colblock(S, ATT_W // LANES), pl.BlockSpec((CONV_TAPS, LANES), lambda j: (0, j)), ANY],
        out_specs=(ANY, pl.BlockSpec((CONV_TAPS, LANES), lambda j: (0, j))),
        out_shape=(jax.ShapeDtypeStruct(dproj.shape, dproj.dtype), jax.ShapeDtypeStruct((CONV_TAPS, CONV_W), F32)),
        input_output_aliases={6: 0},
        scratch_shapes=[piece] * 4 + [pltpu.SemaphoreType.DMA((4,))],
        compiler_params=_params(("arbitrary",)),
    )(rest, rest, rest, rest, dy, cw, dproj)


def _pool_z(u, grp, row):
    s2 = u + _shift_down(u, 1, row)
    s4 = s2 + _shift_down(s2, 2, row)
    s8 = s4 + _shift_down(s4, 4, row)
    s16 = s8 + _shift_down(s8, 8, row)
    sw = jnp.where(grp == 0, s2, jnp.where(grp == 1, s4, jnp.where(grp == 2, s8, s16)))
    return sw / _pool_count(grp, row) - u


def _pool_count(grp, row):
    return jnp.minimum(row + 1, 2 << grp).astype(F32)


def _pool_fwd(rest, pw, scale, y):
    S = rest.shape[0]

    def body(u_ref, g_ref, pw_ref, sc_ref, y_in, y_ref):
        grp = pl.program_id(0)
        row = lax.broadcasted_iota(jnp.int32, (S, LANES), 0)
        z = _pool_z(u_ref[...], grp, row)
        zp = jnp.dot(z.astype(BF16), pw_ref[...].astype(BF16), preferred_element_type=F32)
        g = g_ref[...]
        y_ref[...] = ((zp * sc_ref[...]) * (g * _sigmoid(g))).astype(BF16)

    return pl.pallas_call(
        body, name="pool_fwd", grid=(N_POOL,),
        in_specs=[_colblock(S, PUB), _colblock(S, PGB),
                  pl.BlockSpec((None, POOL_GROUP, POOL_GROUP), lambda j: (j, 0, 0)),
                  pl.BlockSpec((1, LANES), lambda j: (0, j)), ANY],
        out_specs=_colblock(S, (ATT_W + CONV_W) // LANES),
        out_shape=jax.ShapeDtypeStruct(y.shape, y.dtype),
        input_output_aliases={4: 0},
        compiler_params=_params(("parallel",)),
    )(rest, rest, pw, scale, y)


def _pool_bwd(rest, dy, pw, scale, dproj):
    S = rest.shape[0]

    def body(u_ref, g_ref, dy_ref, pw_ref, sc_ref, dproj_in, dproj_ref, dpw_ref, dsc_ref, du_ref, dg_ref, sems):
        grp = pl.program_id(0)
        row = lax.broadcasted_iota(jnp.int32, (S, LANES), 0)
        z = _pool_z(u_ref[...], grp, row).astype(BF16)
        pwb = pw_ref[...].astype(BF16)
        zp = jnp.dot(z, pwb, preferred_element_type=F32)
        g = g_ref[...]
        sg = _sigmoid(g)
        silu = g * sg
        dy = dy_ref[...]
        sc = sc_ref[...]
        dsc_ref[...] = jnp.sum(dy * silu * zp, axis=0, keepdims=True)
        dg_ref[...] = (dy * (zp * sc) * (sg * (1.0 + g * (1.0 - sg)))).astype(BF16)
        dzp = (dy * silu * sc).astype(BF16)
        dpw_ref[...] = lax.dot_general(z, dzp, (((0,), (0,)), ((), ())), preferred_element_type=F32)
        dz = lax.dot_general(dzp, pwb, (((1,), (1,)), ((), ())), preferred_element_type=F32)
        f1 = dz / _pool_count(grp, row)
        f2 = f1 + _shift_up(f1, 1, row)
        f4 = f2 + _shift_up(f2, 2, row)
        f8 = f4 + _shift_up(f4, 4, row)
        f16 = f8 + _shift_up(f8, 8, row)
        fw = jnp.where(grp == 0, f2, jnp.where(grp == 1, f4, jnp.where(grp == 2, f8, f16)))
        du_ref[...] = (fw - dz).astype(BF16)
        first = QKV_W // LANES
        _put_columns([du_ref, dg_ref], dproj_ref, [first + PUB + grp, first + PGB + grp], sems)

    piece = pltpu.VMEM((S, LANES), BF16)
    return pl.pallas_call(
        body, name="pool_bwd", grid=(N_POOL,),
        in_specs=[_colblock(S, PUB), _colblock(S, PGB), _colblock(S, (ATT_W + CONV_W) // LANES),
                  pl.BlockSpec((None, POOL_GROUP, POOL_GROUP), lambda j: (j, 0, 0)),
                  pl.BlockSpec((1, LANES), lambda j: (0, j)), ANY],
        out_specs=(ANY, pl.BlockSpec((None, POOL_GROUP, POOL_GROUP), lambda j: (j, 0, 0)),
                   pl.BlockSpec((1, LANES), lambda j: (0, j))),
        out_shape=(jax.ShapeDtypeStruct(dproj.shape, dproj.dtype),
                   jax.ShapeDtypeStruct((N_POOL, POOL_GROUP, POOL_GROUP), F32), jax.ShapeDtypeStruct((1, POOL_W), F32)),
        input_output_aliases={5: 0},
        scratch_shapes=[piece] * 2 + [pltpu.SemaphoreType.DMA((2,))],
        compiler_params=_params(("arbitrary",)),
    )(rest, rest, dy, pw, scale, dproj)


def _ln_fwd(z, g, b, *, ts):
    S, D = z.shape

    def body(z_ref, g_ref, b_ref, o_ref, ob_ref, obt_ref, xh_ref, rs_ref):
        zz = z_ref[...]
        mu = jnp.mean(zz, axis=1, keepdims=True)
        zc = zz - mu
        rstd = lax.rsqrt(jnp.mean(zc * zc, axis=1, keepdims=True) + LN_EPS)
        xh = zc * rstd
        out = xh * g_ref[...] + b_ref[...]
        o_ref[...] = out
        ob_ref[...] = out.astype(BF16)
        obt_ref[...] = out.T.astype(BF16)
        xh_ref[...] = xh
        rs_ref[...] = rstd

    rowblk = pl.BlockSpec((ts, D), lambda i: (i, 0))
    vec = pl.BlockSpec((1, D), lambda i: (0, 0))
    return pl.pallas_call(
        body, name="ln_fwd", grid=(S // ts,),
        in_specs=[rowblk, vec, vec],
        out_specs=(rowblk, rowblk, pl.BlockSpec((D, ts), lambda i: (0, i)), rowblk,
                   pl.BlockSpec((ts, 1), lambda i: (i, 0))),
        out_shape=(jax.ShapeDtypeStruct((S, D), F32), jax.ShapeDtypeStruct((S, D), BF16),
                   jax.ShapeDtypeStruct((D, S), BF16), jax.ShapeDtypeStruct((S, D), F32),
                   jax.ShapeDtypeStruct((S, 1), F32)),
        compiler_params=_params(("parallel",)),
    )(z, g, b)


def _prep(x, after, *, ts):
    S, D = x.shape

    def body(x_ref, after_ref, xb_ref, xbt_ref):
        xb_ref[...] = x_ref[...].astype(BF16)
        xbt_ref[...] = x_ref[...].T.astype(BF16)

    return pl.pallas_call(
        body, name="prep", grid=(S // ts,),
        in_specs=[pl.BlockSpec((ts, D), lambda i: (i, 0)), ANY],
        out_specs=(pl.BlockSpec((ts, D), lambda i: (i, 0)), pl.BlockSpec((D, ts), lambda i: (0, i))),
        out_shape=(jax.ShapeDtypeStruct((S, D), BF16), jax.ShapeDtypeStruct((D, S), BF16)),
        compiler_params=_params(("parallel",)),
    )(x, after)


def _ln_bwd(dout, xhat, rstd, g, *, ts):
    S, D = dout.shape

    def body(d_ref, xh_ref, rs_ref, g_ref, dz_ref, dzb_ref, dg_ref, db_ref):
        @pl.when(pl.program_id(0) == 0)
        def _():
            dg_ref[...] = jnp.zeros_like(dg_ref)
            db_ref[...] = jnp.zeros_like(db_ref)

        d = d_ref[...]
        xh = xh_ref[...]
        dxh = d * g_ref[...]
        m1 = jnp.mean(dxh, axis=1, keepdims=True)
        m2 = jnp.mean(dxh * xh, axis=1, keepdims=True)
        dz = rs_ref[...] * (dxh - m1 - xh * m2)
        dz_ref[...] = dz
        dzb_ref[...] = dz.astype(BF16)
        dg_ref[...] += jnp.sum(d * xh, axis=0, keepdims=True)
        db_ref[...] += jnp.sum(d, axis=0, keepdims=True)

    rowblk = pl.BlockSpec((ts, D), lambda i: (i, 0))
    vec = pl.BlockSpec((1, D), lambda i: (0, 0))
    return pl.pallas_call(
        body, name="ln_bwd", grid=(S // ts,),
        in_specs=[rowblk, rowblk, pl.BlockSpec((ts, 1), lambda i: (i, 0)), vec],
        out_specs=(rowblk, rowblk, vec, vec),
        out_shape=(jax.ShapeDtypeStruct((S, D), F32), jax.ShapeDtypeStruct((S, D), BF16),
                   jax.ShapeDtypeStruct((1, D), F32), jax.ShapeDtypeStruct((1, D), F32)),
        compiler_params=_params(("arbitrary",)),
    )(dout, xhat, rstd, g)


def _loss_head(y, tgt, *, ts):
    S, D = y.shape

    def body(y_ref, t_ref, l_ref, dy_ref):
        @pl.when(pl.program_id(0) == 0)
        def _():
            l_ref[...] = jnp.zeros_like(l_ref)

        e = y_ref[...] - t_ref[...]
        dy_ref[...] = e * (1.0 / D)
        rowloss = jnp.sum(e * e, axis=1, keepdims=True) * (0.5 / D)
        l_ref[...] += jnp.sum(rowloss, axis=0, keepdims=True)

    rowblk = pl.BlockSpec((ts, D), lambda i: (i, 0))
    return pl.pallas_call(
        body, name="loss_head", grid=(S // ts,),
        in_specs=[rowblk, rowblk],
        out_specs=(pl.BlockSpec((1, 1), lambda i: (0, 0)), rowblk),
        out_shape=(jax.ShapeDtypeStruct((1, 1), F32), jax.ShapeDtypeStruct((S, D), F32)),
        compiler_params=_params(("arbitrary",)),
    )(y, tgt)


def _adamw(lands, srcs, w, m, v, *, name, first=0, into=None):
    R, C = w.shape
    npart = len(lands)
    rp = lands[0].shape[1]
    assert R % rp == 0 and all(p.shape == (N_DEV, rp, C) for p in lands)
    tiled = rp % 8 == 0
    assert tiled or (first == 0 and npart * rp == R and into is None)
    tr = min(ADAMW_BLOCK_ELEMS // (LANES * pl.cdiv(C, LANES)), rp) if tiled else rp
    assert rp % tr == 0
    steps = rp // tr
    c1 = 1.0 - ADAM_B1 ** ADAM_STEP
    c2 = 1.0 - ADAM_B2 ** ADAM_STEP
    slabbed = [p.ndim == 3 for p in srcs]
    n_into = 0 if into is None else 4

    def body(*refs):
        l_refs, s_refs = refs[:npart], refs[npart:2 * npart]
        w_ref, m_ref, v_ref = refs[2 * npart:2 * npart + 3]
        g_ref, d_ref, nm_ref, nv_ref = refs[2 * npart + 3 + n_into:]
        x, y, c = _my_place()
        me = 4 * x + 2 * y + c

        def update(l, rows):
            g = None
            for j in range(N_DEV):
                own = s_refs[l][j] if slabbed[l] else s_refs[l][...]
                p = jnp.where(me == j, own, l_refs[l][j]).astype(F32)
                g = p if g is None else g + p
            nm = ADAM_B1 * m_ref[rows, :] + (1.0 - ADAM_B1) * g
            nv = ADAM_B2 * v_ref[rows, :] + (1.0 - ADAM_B2) * (g * g)
            g_ref[rows, :] = g
            nm_ref[rows, :] = nm
            nv_ref[rows, :] = nv
            d_ref[rows, :] = -ADAM_LR * ((nm / c1) / (jnp.sqrt(nv / c2) + ADAM_EPS) + ADAM_WD * w_ref[rows, :])

        for l in range(npart):
            if tiled:
                pl.when(pl.program_id(0) == l)(lambda l=l: update(l, slice(None)))
            else:
                update(l, slice(l * rp, (l + 1) * rp))

    def part_spec(l, p):
        if p.ndim == 3:
            return pl.BlockSpec((N_DEV, tr, C), lambda q, i: (0, jnp.where(q == l, i, 0), 0))
        return pl.BlockSpec((tr, C), lambda q, i: (jnp.where(q == l, i, 0), 0))

    if tiled:
        blk = pl.BlockSpec((tr, C), lambda q, i: ((first + q) * steps + i, 0))
    else:
        blk = pl.BlockSpec((R, C), lambda q, i: (0, 0))
    out = jax.ShapeDtypeStruct((R, C), F32)
    n_in = 2 * npart + 3
    return pl.pallas_call(
        body, name=name, grid=(npart, steps) if tiled else (1, 1),
        in_specs=[part_spec(l, p) for l, p in enumerate(lands)] + [part_spec(l, p) for l, p in enumerate(srcs)]
        + [blk, blk, blk] + [ANY] * n_into,
        out_specs=(blk, blk, blk, blk),
        out_shape=(out, out, out, out),
        input_output_aliases={n_in + i: i for i in range(n_into)},
        compiler_params=_params(("arbitrary", "arbitrary")),
    )(*lands, *srcs, w, m, v, *(into or ()))


def _my_place():
    return lax.axis_index("x"), lax.axis_index("y"), lax.axis_index("c")


HBM = pl.BlockSpec(memory_space=pltpu.HBM)
SEM = pl.BlockSpec(memory_space=pltpu.SEMAPHORE)
DATAFLOW = pltpu.SideEffectType.DATAFLOW_SIDE_EFFECTING
EVERYONE = (1, 2, 3, 4, 5, 6, 7)
SIBLING = 1
SAME_CORE = (4, 2, 6)


def _xchg_copies(src, land, send_sem, recv_sem, gather, peers):
    x, y, c = _my_place()
    me = 4 * x + 2 * y + c
    copies = []
    for k, bits in enumerate(peers):
        px, py, pc = x ^ ((bits >> 2) & 1), y ^ ((bits >> 1) & 1), c ^ (bits & 1)
        peer = 4 * px + 2 * py + pc
        for a in range(len(src)):
            i = a * len(peers) + k
            out = src[a] if gather[a] else src[a].at[peer]
            sent, landed = [pltpu.make_async_remote_copy(
                src_ref=out, dst_ref=land[a].at[slab], send_sem=send_sem.at[i], recv_sem=recv_sem.at[i],
                device_id=(px, py, pc), device_id_type=MESH) for slab in (me, peer)]
            copies.append((sent, landed))
    return copies


def _xchg_start(srcs, *, gather, name, peers=EVERYONE, after=None):
    n = len(srcs)
    follows = [] if after is None else [after]
    lands = [lax.empty(((N_DEV,) + v.shape) if g else v.shape, v.dtype) for v, g in zip(srcs, gather)]

    def body(*refs):
        src, land = refs[:n], refs[n:2 * n]
        send_sem, recv_sem = refs[2 * n + len(follows)], refs[2 * n + len(follows) + 1]
        token = refs[-1]
        for sent, _ in _xchg_copies(src, land, send_sem, recv_sem, gather, peers):
            sent.start()
        token[...] = jnp.zeros_like(token)

    sems = pltpu.SemaphoreType.DMA((n * len(peers),))
    outs = pl.pallas_call(
        body, name=name,
        out_shape=(sems, sems, *[pltpu.HBM(v.shape, v.dtype) for v in srcs + lands],
                   jax.ShapeDtypeStruct((8, LANES), F32)),
        in_specs=[HBM] * (2 * n) + [ANY] * len(follows),
        out_specs=(SEM, SEM, *[HBM] * (2 * n), pl.BlockSpec(memory_space=pltpu.VMEM)),
        input_output_aliases={i: 2 + i for i in range(2 * n)},
        compiler_params=pltpu.CompilerParams(has_side_effects=DATAFLOW),
    )(*[pltpu.with_memory_space_constraint(v, pltpu.HBM) for v in srcs + lands], *follows)
    return dict(send=outs[0], recv=outs[1], srcs=list(outs[2:2 + n]), lands=list(outs[2 + n:2 + 2 * n]),
                token=outs[-1], gather=gather, peers=peers)


def _xchg_wait(h, after, *, name):
    n = len(h["srcs"])
    gather, peers = h["gather"], h["peers"]

    def body(*refs):
        src, land = refs[:n], refs[n:2 * n]
        send_sem, recv_sem = refs[2 * n], refs[2 * n + 1]
        for sent, landed in _xchg_copies(src, land, send_sem, recv_sem, gather, peers):
            sent.wait_send()
            landed.wait_recv()

    thru = h["srcs"] + h["lands"]
    outs = pl.pallas_call(
        body, name=name,
        out_shape=[pltpu.HBM(v.shape, v.dtype) for v in thru],
        in_specs=[HBM] * (2 * n) + [SEM, SEM, ANY],
        out_specs=[HBM] * (2 * n),
        input_output_aliases={i: i for i in range(2 * n)},
        compiler_params=pltpu.CompilerParams(has_side_effects=DATAFLOW),
    )(*thru, h["send"], h["recv"], after)
    return list(outs[:n]), list(outs[n:])


def _relay_copies(land, send_sem, recv_sem):
    x, y, c = _my_place()
    me = 4 * x + 2 * y + c
    copies = []
    for k, bits in enumerate(SAME_CORE):
        for a in range(len(land)):
            i = a * len(SAME_CORE) + k
            sent, landed = [pltpu.make_async_remote_copy(
                src_ref=land[a].at[me ^ bits], dst_ref=land[a].at[slab], send_sem=send_sem.at[i], recv_sem=recv_sem.at[i],
                device_id=(x, y, 1 - c), device_id_type=MESH) for slab in (me ^ bits, me ^ bits ^ SIBLING)]
            copies.append((sent, landed))
    return copies


def _relay_start(lands, *, name):
    n = len(lands)

    def body(*refs):
        land = refs[:n]
        send_sem, recv_sem = refs[n], refs[n + 1]
        token = refs[-1]
        for sent, _ in _relay_copies(land, send_sem, recv_sem):
            sent.start()
        token[...] = jnp.zeros_like(token)

    sems = pltpu.SemaphoreType.DMA((n * len(SAME_CORE),))
    outs = pl.pallas_call(
        body, name=name,
        out_shape=(sems, sems, *[pltpu.HBM(v.shape, v.dtype) for v in lands], jax.ShapeDtypeStruct((8, LANES), F32)),
        in_specs=[HBM] * n,
        out_specs=(SEM, SEM, *[HBM] * n, pl.BlockSpec(memory_space=pltpu.VMEM)),
        input_output_aliases={i: 2 + i for i in range(n)},
        compiler_params=pltpu.CompilerParams(has_side_effects=DATAFLOW),
    )(*[pltpu.with_memory_space_constraint(v, pltpu.HBM) for v in lands])
    return dict(send=outs[0], recv=outs[1], lands=list(outs[2:2 + n]), token=outs[-1])


def _relay_wait(h, after, *, name):
    n = len(h["lands"])

    def body(*refs):
        land = refs[:n]
        send_sem, recv_sem = refs[n], refs[n + 1]
        for sent, landed in _relay_copies(land, send_sem, recv_sem):
            sent.wait_send()
            landed.wait_recv()

    outs = pl.pallas_call(
        body, name=name,
        out_shape=[pltpu.HBM(v.shape, v.dtype) for v in h["lands"]],
        in_specs=[HBM] * n + [SEM, SEM, ANY],
        out_specs=[HBM] * n,
        input_output_aliases={i: i for i in range(n)},
        compiler_params=pltpu.CompilerParams(has_side_effects=DATAFLOW),
    )(*h["lands"], h["send"], h["recv"], after)
    return list(outs)


def _with_own(land, own, me):
    slot = lax.broadcasted_iota(jnp.int32, land.shape, 0)
    return jnp.where(slot == me, own if own.ndim == land.ndim else own[None], land)


def _w_in_segments(j, cols):
    lo, hi = j * cols, (j + 1) * cols
    segs = []
    for a, b, where, shift in ((0, FG_AT, "main", 0), (FG_AT, FG_AT + N_HEADS, "fg", -FG_AT),
                               (FG_AT + N_HEADS, IN_W, "main", -N_HEADS)):
        s0, s1 = max(lo, a), min(hi, b)
        if s0 < s1:
            segs.append((s0 - lo, s1 - s0, where, s0 + shift))
    return segs


SLAB_W = 1024


def _window(off, width):
    base = min((off // LANES) * LANES, width - SLAB_W)
    return base, off - base


def _w_unpack(land, own, *, tr):
    _, D, cols = land.shape
    assert cols + LANES - 1 <= SLAB_W

    def body(land_ref, own_ref, main_ref, slab):
        x, y, c = _my_place()
        me = 4 * x + 2 * y + c
        lane = lax.broadcasted_iota(jnp.int32, (tr, SLAB_W), 1)
        main_ref[...] = jnp.zeros_like(main_ref)
        slab[...] = jnp.zeros_like(slab)
        for j in range(N_DEV):
            slab[:, :cols] = jnp.where(me == j, own_ref[...], land_ref[j])
            v = slab[...]
            segs = _w_in_segments(j, cols)
            for lo, n, where, dst in segs:
                part = v if len(segs) == 1 else jnp.where((lane >= lo) & (lane < lo + n), v, jnp.zeros_like(v))
                if where == "fg":
                    main_ref[:, MAIN_W:] = pltpu.roll(part, (SLAB_W - lo) % SLAB_W, 1)[:, :LANES]
                else:
                    base, r = _window(dst - lo, MAIN_W)
                    assert 0 <= r and lo + n + r <= SLAB_W
                    main_ref[:, base:base + SLAB_W] += pltpu.roll(part, r, 1)

    return pl.pallas_call(
        body, name="w_unpack", grid=(D // tr,),
        in_specs=[pl.BlockSpec((N_DEV, tr, cols), lambda i: (0, i, 0)), pl.BlockSpec((tr, cols), lambda i: (i, 0))],
        out_specs=pl.BlockSpec((tr, ALL_W), lambda i: (i, 0)),
        out_shape=jax.ShapeDtypeStruct((D, ALL_W), BF16),
        scratch_shapes=[pltpu.VMEM((tr, SLAB_W), BF16)],
        compiler_params=_params(("parallel",)),
    )(land, own)


def _g_pack(dw_main, dw_fg, cols, *, tr):
    D = dw_main.shape[0]

    def body(main_ref, fg_ref, out_ref, slab):
        lane = lax.broadcasted_iota(jnp.int32, (tr, SLAB_W), 1)
        slab[...] = jnp.zeros_like(slab)
        slab[:, :LANES] = fg_ref[...]
        fg = slab[...]
        for j in range(N_DEV):
            v = None
            for lo, n, where, src in _w_in_segments(j, cols):
                if where == "fg":
                    part = pltpu.roll(fg, lo, 1)
                else:
                    base, r = _window(src - lo, MAIN_W)
                    assert 0 <= r and lo + n + r <= SLAB_W
                    part = pltpu.roll(main_ref[:, base:base + SLAB_W], (SLAB_W - r) % SLAB_W, 1)
                v = part if v is None else jnp.where((lane >= lo) & (lane < lo + n), part, v)
            out_ref[j] = v[:, :cols]

    return pl.pallas_call(
        body, name="g_pack", grid=(D // tr,),
        in_specs=[pl.BlockSpec((tr, MAIN_W), lambda i: (i, 0)), pl.BlockSpec((tr, LANES), lambda i: (i, 0))],
        out_specs=pl.BlockSpec((N_DEV, tr, cols), lambda i: (0, i, 0)),
        out_shape=jax.ShapeDtypeStruct((N_DEV, D, cols), BF16),
        scratch_shapes=[pltpu.VMEM((tr, SLAB_W), BF16)],
        compiler_params=_params(("parallel",)),
    )(dw_main, dw_fg)


TQ = 512
TS = 256
W_TILE_ROWS = 128


def _layer_fwd_proj(xb, w_in):
    qkv = _mm(xb, w_in, out_dtype=BF16, tm=2048, tn=512, tk=D_MODEL, name="mm_qkv", n_cols=QKV_W)
    rest = _mm(xb, w_in, out_dtype=F32, tm=2048, tn=512, tk=D_MODEL, name="mm_rest", n_off=QKV_W, n_cols=REST_W)
    fgp = _mm(xb, w_in, out_dtype=F32, tm=1024, tn=LANES, tk=D_MODEL, name="mm_fg", n_off=MAIN_W, n_cols=LANES)
    return qkv, rest, fgp


def _layer_fwd_mix(x, xbt, proj, wl):
    qkv, rest, fgp = proj
    cum = _fg_fwd(fgp, wl["b_f"])
    S = x.shape[0]
    cumr = (cum[:, :N_HEADS].T * LOG2E).reshape(N_HEADS, S // TQ, 1, TQ)
    o, y, lse = _attn_fwd(qkv, rest, cum, cumr, tq=TQ)
    y = _conv_fwd(rest, wl["conv_w"], y)
    y = _pool_fwd(rest, wl["pool_w"], wl["pool_scale"], y)
    z = _mm(y, wl["w_out"], out_dtype=F32, tm=2048, tn=512, tk=D_MODEL, name="mm_out", c=x, c_scale=ALPHA)
    out, outb, outbt, xhat, rstd = _ln_fwd(z, wl["ln_g"], wl["ln_b"], ts=TS)
    saved = dict(xbt=xbt, qkv=qkv, rest=rest, fgp=fgp, cum=cum, cumr=cumr, o=o, lse=lse, y=y, xhat=xhat, rstd=rstd)
    return out, outb, outbt, saved


def _layer_bwd_mix(dout, sv, wl):
    S = dout.shape[0]
    dz, dzb, dln_g, dln_b = _ln_bwd(dout, sv["xhat"], sv["rstd"], wl["ln_g"], ts=TS)
    dy = _mm(dzb, wl["w_out"], out_dtype=F32, tm=2048, tn=512, tk=D_MODEL, name="mm_dy", bt=True)
    dw_out = _mm(sv["y"], dzb, out_dtype=BF16, tm=512, tn=D_MODEL, tk=2048, name="mm_dw_out", at=True)
    rest = sv["rest"]
    do, dproj, a, delta = _attn_bwd_pre(rest, sv["o"], dy, sv["cum"], sv["lse"])
    dproj, dcr, dcc = _attn_bwd(sv["qkv"], do, a, delta, sv["cumr"], dproj, tq=TQ)
    dcum = dcr - jnp.pad(dcc.reshape(N_HEADS, S).T, ((0, 0), (0, LANES - N_HEADS)))
    dproj, db_f = _fg_bwd(dcum, sv["fgp"], wl["b_f"], dproj)
    dproj, dconv_w = _conv_bwd(rest, dy, wl["conv_w"], dproj)
    dproj, dpool_w, dpool_scale = _pool_bwd(rest, dy, wl["pool_w"], wl["pool_scale"], dproj)
    grads = dict(b_f=db_f[:, :N_HEADS], conv_w=dconv_w, pool_w=dpool_w, pool_scale=dpool_scale, w_out=dw_out,
                 ln_g=dln_g, ln_b=dln_b)
    return grads, (dproj, dz)


def _layer_bwd_w_in(mid, sv):
    dproj, dz = mid
    xt = sv["xbt"]
    S = xt.shape[1]
    dw_main = _mm(xt, dproj, out_dtype=BF16, tm=1024, tn=512, tk=S, name="mm_dw_main", n_cols=MAIN_W)
    dw_fg = _mm(xt, dproj, out_dtype=BF16, tm=1024, tn=LANES, tk=2048, name="mm_dw_fg", n_off=MAIN_W, n_cols=LANES)
    return (dw_main, dw_fg), (dproj, dz)


def _layer_bwd_input(ctx, wl, after):
    dproj, dz = ctx
    return _mm(dproj, wl["w_in"], out_dtype=F32, tm=1024, tn=256, tk=ALL_W, name="mm_dx", c=dz, c_scale=ALPHA, bt=True,
               after=after)


def kernel(x, w_in, b_f, conv_w, pool_w, pool_scale, w_out, ln_g, ln_b, loss_target, m_w_in, m_b_f, m_conv_w, m_pool_w, m_pool_scale, m_w_out, m_ln_g, m_ln_b, v_w_in, v_b_f, v_conv_w, v_pool_w, v_pool_scale, v_w_out, v_ln_g, v_ln_b):
    L, D, cols = w_in.shape
    rows_out = w_out.shape[1]
    ccols = conv_w.shape[2]
    mx, my, mc = _my_place()
    me = 4 * mx + 2 * my + mc

    w_in_b, w_out_b = w_in.astype(BF16), w_out.astype(BF16)
    gathers = []
    for l in range(L):
        gathers.append(_xchg_start([w_in_b[l], w_out_b[l], conv_w[l]], gather=[True] * 3, name=f"gather_start_{l}",
                                   peers=(SIBLING,) + SAME_CORE if l == 0 else EVERYONE,
                                   after=gathers[-1]["token"] if gathers else None))
    started = gathers[-1]["token"]

    xl = x[0]
    xb, xbt = _prep(xl, started, ts=TS)
    weights, saved = [], []
    for l in range(L):
        (own_in, own_out, own_cw), (g_in, g_out, g_cw) = _xchg_wait(gathers[l], xb, name=f"gather_wait_{l}")
        if l == 0:
            g_in, g_out, g_cw = _relay_wait(_relay_start([g_in, g_out, g_cw], name="gather_relay_start"), xb,
                                            name="gather_relay_wait")
        w_all = _w_unpack(g_in, own_in, tr=W_TILE_ROWS)
        proj = _layer_fwd_proj(xb, w_all)
        w_o = _with_own(g_out, own_out, me).reshape(N_DEV * rows_out, D)
        cw = jnp.transpose(_with_own(g_cw, own_cw, me), (1, 0, 2)).reshape(CONV_TAPS, N_DEV * ccols)
        wl = dict(w_in=w_all, w_out=w_o, conv_w=cw,
                  b_f=jnp.pad(b_f[l][None, :], ((0, 0), (0, LANES - N_HEADS))), pool_w=pool_w[l],
                  pool_scale=pool_scale[l][None, :], ln_g=ln_g[l][None, :], ln_b=ln_b[l][None, :])
        xl, xb, xbt, sv = _layer_fwd_mix(xl, xbt, proj, wl)
        weights.append(wl)
        saved.append(sv)

    loss, dx = _loss_head(xl, loss_target[0], ts=TS)
    loss = lax.psum(loss[0, 0], ("x", "y", "c"))

    names = ["w_in", "w_out", "conv_w", "b_f", "pool_w", "pool_scale", "ln_g", "ln_b"]
    sharded = names[:3]
    scatters = [None] * L
    for l in reversed(range(L)):
        g, mid = _layer_bwd_mix(dx, saved[l], weights[l])
        (dw_main, dw_fg), ctx = _layer_bwd_w_in(mid, saved[l])
        g["w_in"] = _g_pack(dw_main, dw_fg, cols, tr=W_TILE_ROWS)
        g["w_out"] = g["w_out"].reshape(N_DEV, rows_out, D)
        g["conv_w"] = jnp.transpose(g["conv_w"].reshape(CONV_TAPS, N_DEV, ccols), (1, 0, 2))
        scatters[l] = _xchg_start([g[k] for k in names], gather=[k not in sharded for k in names],
                                  name=f"scatter_start_{l}")
        dx = _layer_bwd_input(ctx, weights[l], scatters[l]["token"])

    given = dict(w_in=(w_in, m_w_in, v_w_in), b_f=(b_f, m_b_f, v_b_f), conv_w=(conv_w, m_conv_w, v_conv_w),
                 pool_w=(pool_w, m_pool_w, v_pool_w), pool_scale=(pool_scale, m_pool_scale, v_pool_scale),
                 w_out=(w_out, m_w_out, v_w_out), ln_g=(ln_g, m_ln_g, v_ln_g), ln_b=(ln_b, m_ln_b, v_ln_b))
    sent, landed = [{} for _ in range(L)], [{} for _ in range(L)]

    def wait(l, after):
        s_, l_ = _xchg_wait(scatters[l], after, name=f"scatter_wait_{l}")
        sent[l].update(zip(names, s_))
        landed[l].update(zip(names, l_))

    def update(k, layers, into=None):
        w, m, v = given[k]
        C = w.shape[-1]
        w2, m2, v2 = (t.reshape(-1, C) for t in (w, m, v))
        rp = w2.shape[0] // L
        lands = [landed[l][k].reshape(N_DEV, rp, C) for l in layers]
        srcs = [sent[l][k].reshape((N_DEV, rp, C) if k in sharded else (rp, C)) for l in layers]
        return _adamw(lands, srcs, w2, m2, v2, name=f"adamw_{k}_{layers[0]}", first=layers[0], into=into)

    big = ["w_in", "w_out", "pool_w"]
    for l in reversed(range(1, L)):
        wait(l, dx)
    upper = {k: update(k, list(range(1, L))) for k in big} if L > 1 else {}
    wait(0, upper["w_in"][0] if upper else dx)
    res = {k: update(k, [0], into=upper[k]) if k in upper else update(k, list(range(L))) for k in names}
    res = {k: [t.reshape(given[k][0].shape) for t in outs] for k, outs in res.items()}

    order = ["w_in", "b_f", "conv_w", "pool_w", "pool_scale", "w_out", "ln_g", "ln_b"]
    return (loss, dx[None], *[res[k][0] for k in order], *[res[k][1] for k in order],
            *[res[k][2] for k in order], *[res[k][3] for k in order])
```

```python
import jax
import jax.numpy as jnp
from jax import lax
from jax.experimental import pallas as pl
from jax.experimental.pallas import tpu as pltpu

F32 = jnp.float32
BF16 = jnp.bfloat16

N_DEV = 8
D_MODEL = 2048
N_HEADS = 8
HEAD_DIM = 128
ATT_W = N_HEADS * HEAD_DIM
CONV_W = 512
CONV_TAPS = 3
POOL_W = 512
POOL_GROUP = 128
N_POOL = POOL_W // POOL_GROUP
FG_AT = 4 * ATT_W
IN_W = FG_AT + N_HEADS + 4 * CONV_W + 2 * POOL_W
MAIN_W = IN_W - N_HEADS
LANES = 128
ALL_W = MAIN_W + LANES
LN_EPS = 1e-5
DEPTH = 4
ALPHA = (2 * DEPTH) ** 0.25
SCALE = HEAD_DIM ** -0.5
LOG2E = 1.4426950408889634
LN2 = 0.6931471805599453
NEG = -1e30

ADAM_LR, ADAM_B1, ADAM_B2, ADAM_EPS, ADAM_WD, ADAM_STEP = 0.001, 0.9, 0.999, 1e-08, 0.01, 10

QKV_W = 3 * ATT_W
REST_W = MAIN_W - QKV_W
QB, KB, VB = 0, 8, 16
GB = 0
CBB, CCB, CHB, CGB = 8, 12, 16, 20
PUB, PGB = 24, 28
FGB = MAIN_W // LANES
VMEM_LIMIT = 48 * 1024 * 1024
ADAMW_BLOCK_ELEMS = 64 * 1024

MESH = pl.DeviceIdType.MESH
ANY = pl.BlockSpec(memory_space=pl.ANY)


def _params(semantics):
    return pltpu.CompilerParams(dimension_semantics=semantics, vmem_limit_bytes=VMEM_LIMIT)


def _sigmoid(x):
    return 1.0 / (1.0 + jnp.exp(-x))


def _mm(a, b, *, out_dtype, tm, tn, tk, name, c=None, c_scale=1.0, n_off=0, n_cols=None, bt=False, at=False,
        after=None):
    M, K = a.shape[::-1] if at else a.shape
    N = b.shape[0 if bt else 1] if n_cols is None else n_cols
    assert not (bt and n_off)
    dims = (((0 if at else 1,), (1 if bt else 0,)), ((), ()))
    tm, tn, tk = min(tm, M), min(tn, N), min(tk, K)
    assert M % tm == 0 and N % tn == 0 and K % tk == 0 and n_off % tn == 0, (a.shape, b.shape, tm, tn, tk)
    nk, joff = K // tk, n_off // tn
    has_c, has_after = c is not None, after is not None

    def body(*refs):
        a_ref, b_ref = refs[0], refs[1]
        c_ref = refs[2] if has_c else None
        o_ref = refs[2 + has_c + has_after]

        def finish(r):
            if has_c:
                r = r + c_scale * c_ref[...]
            o_ref[...] = r.astype(out_dtype)

        if nk == 1:
            finish(lax.dot_general(a_ref[...], b_ref[...], dims, preferred_element_type=F32))
        else:
            acc_ref = refs[3 + has_c + has_after]
            k = pl.program_id(2)

            @pl.when(k == 0)
            def _():
                acc_ref[...] = jnp.zeros_like(acc_ref)

            acc_ref[...] += lax.dot_general(a_ref[...], b_ref[...], dims, preferred_element_type=F32)

            @pl.when(k == nk - 1)
            def _():
                finish(acc_ref[...])

    in_specs = [pl.BlockSpec((tk, tm), lambda i, j, k: (k, i)) if at else pl.BlockSpec((tm, tk), lambda i, j, k: (i, k)),
                pl.BlockSpec((tn, tk), lambda i, j, k: (j, k)) if bt
                else pl.BlockSpec((tk, tn), lambda i, j, k: (k, j + joff))]
    args = [a, b]
    if has_c:
        in_specs.append(pl.BlockSpec((tm, tn), lambda i, j, k: (i, j)))
        args.append(c)
    if has_after:
        in_specs.append(ANY)
        args.append(after)
    return pl.pallas_call(
        body, name=name, grid=(M // tm, N // tn, nk),
        in_specs=in_specs,
        out_specs=pl.BlockSpec((tm, tn), lambda i, j, k: (i, j)),
        out_shape=jax.ShapeDtypeStruct((M, N), out_dtype),
        scratch_shapes=[pltpu.VMEM((tm, tn), F32)] if nk > 1 else [],
        compiler_params=_params(("parallel", "parallel", "arbitrary")),
    )(*args)


def _shift_down(x, k, row):
    return jnp.where(row >= k, pltpu.roll(x, k, 0), 0.0)


def _shift_up(x, k, row):
    s = x.shape[0]
    return jnp.where(row < s - k, pltpu.roll(x, s - k, 0), 0.0)


def _fg_fwd(fgp, bf):
    S = fgp.shape[0]

    def body(f_ref, b_ref, cum_ref):
        a = f_ref[...] + b_ref[...]
        x = jnp.minimum(a, 0.0) - jnp.log(1.0 + jnp.exp(-jnp.abs(a)))
        row = lax.broadcasted_iota(jnp.int32, x.shape, 0)
        k = 1
        while k < S:
            x = x + _shift_down(x, k, row)
            k *= 2
        cum_ref[...] = x

    return pl.pallas_call(body, name="fg_fwd", out_shape=jax.ShapeDtypeStruct((S, LANES), F32),
                          compiler_params=_params(None))(fgp, bf)


def _fg_bwd(dcum, fgp, bf, dproj):
    S = fgp.shape[0]

    def body(d_ref, f_ref, b_ref, dproj_in, dfg_ref, dbf_ref):
        x = d_ref[...]
        row = lax.broadcasted_iota(jnp.int32, x.shape, 0)
        k = 1
        while k < S:
            x = x + _shift_up(x, k, row)
            k *= 2
        dfg = x * _sigmoid(-(f_ref[...] + b_ref[...]))
        dfg_ref[...] = dfg.astype(BF16)
        dbf_ref[...] = jnp.sum(dfg, axis=0, keepdims=True)

    whole = pl.BlockSpec((S, LANES), lambda i: (0, 0))
    vec = pl.BlockSpec((1, LANES), lambda i: (0, 0))
    return pl.pallas_call(
        body, name="fg_bwd", grid=(1,),
        in_specs=[whole, whole, vec, ANY],
        out_specs=(pl.BlockSpec((S, LANES), lambda i: (0, FGB)), vec),
        out_shape=(jax.ShapeDtypeStruct(dproj.shape, dproj.dtype), jax.ShapeDtypeStruct((1, LANES), F32)),
        input_output_aliases={3: 0},
        compiler_params=_params(("arbitrary",)),
    )(dcum, fgp, bf, dproj)


def _put_columns(pieces, dst, blocks, sems):
    copies = [pltpu.make_async_copy(p, dst.at[:, pl.ds(pl.multiple_of(b * LANES, LANES), LANES)], sems.at[i])
              for i, (p, b) in enumerate(zip(pieces, blocks))]
    for cp in copies:
        cp.start()
    for cp in copies:
        cp.wait()


def _colblock(S, off):
    return pl.BlockSpec((S, LANES), lambda h: (0, h + off))


def _head_row(nq, tq):
    return pl.BlockSpec((None, nq, 1, tq), lambda h: (h, 0, 0, 0))


def _lane_of(ref, h):
    lane = lax.broadcasted_iota(jnp.int32, ref.shape, 1)
    return jnp.sum(jnp.where(lane == h, ref[...], 0.0), axis=1, keepdims=True)


def _attn_fwd(qkv, rest, cum, cumr, *, tq):
    S = qkv.shape[0]
    nq = S // tq

    def body(q_ref, k_ref, v_ref, g_ref, cum_ref, cumr_ref, o_ref, y_ref, lse_ref, cc):
        cc[...] = _lane_of(cum_ref, pl.program_id(0)) * LOG2E
        tri = (lax.broadcasted_iota(jnp.int32, (tq, tq), 0) <= lax.broadcasted_iota(jnp.int32, (tq, tq), 1))

        def q_step(qi, _):
            rows = pl.ds(pl.multiple_of(qi * tq, tq), tq)
            q = q_ref[rows, :]
            ci = cumr_ref[qi]

            def tile(kj, carry, masked):
                m, l, acc = carry
                cols = pl.ds(pl.multiple_of(kj * tq, tq), tq)
                s = lax.dot_general(k_ref[cols, :], q, (((1,), (1,)), ((), ())), preferred_element_type=F32)
                s = s * (SCALE * LOG2E) + ci - cc[cols, :]
                if masked:
                    s = jnp.where(tri, s, NEG)
                m_new = jnp.maximum(m, jnp.max(s, axis=0, keepdims=True))
                a = jnp.exp2(m - m_new)
                p = jnp.exp2(s - m_new)
                l = a * l + jnp.sum(p, axis=0, keepdims=True)
                acc = a * acc + lax.dot_general(v_ref[cols, :], p.astype(BF16), (((0,), (0,)), ((), ())),
                                                preferred_element_type=F32)
                return m_new, l, acc

            init = (jnp.full((1, tq), NEG, F32), jnp.zeros((1, tq), F32), jnp.zeros((HEAD_DIM, tq), F32))
            carry = lax.fori_loop(0, qi, lambda kj, c: tile(kj, c, False), init)
            m, l, acc = tile(qi, carry, True)
            o = (acc / l).T
            g = g_ref[rows, :]
            o_ref[rows, :] = o
            y_ref[rows, :] = (o * (g * _sigmoid(g))).astype(BF16)
            lse_ref[qi] = m * LN2 + jnp.log(l)
            return 0

        lax.fori_loop(0, nq, q_step, 0)

    return pl.pallas_call(
        body, name="attn_fwd", grid=(N_HEADS,),
        in_specs=[_colblock(S, QB), _colblock(S, KB), _colblock(S, VB), _colblock(S, GB),
                  pl.BlockSpec((S, LANES), lambda h: (0, 0)), _head_row(nq, tq)],
        out_specs=(_colblock(S, 0), _colblock(S, 0), _head_row(nq, tq)),
        out_shape=(jax.ShapeDtypeStruct((S, ATT_W), F32), jax.ShapeDtypeStruct((S, D_MODEL), BF16),
                   jax.ShapeDtypeStruct((N_HEADS, nq, 1, tq), F32)),
        scratch_shapes=[pltpu.VMEM((S, 1), F32)],
        compiler_params=_params(("arbitrary",)),
    )(qkv, qkv, qkv, rest, cum, cumr)


def _attn_bwd_pre(rest, o, dy, *, tq):
    S = o.shape[0]
    nq = S // tq

    def body(g_ref, o_ref, dy_ref, do_ref, dg_ref, dl_ref):
        g = g_ref[...]
        sg = _sigmoid(g)
        ov = o_ref[...]
        dy = dy_ref[...]
        do = dy * (g * sg)
        do_ref[...] = do.astype(BF16)
        dg_ref[...] = (dy * ov * (sg * (1.0 + g * (1.0 - sg)))).astype(BF16)
        rows = lax.dot_general(jnp.ones((8, HEAD_DIM), F32), do * ov, (((1,), (1,)), ((), ())),
                               precision=lax.Precision.HIGHEST, preferred_element_type=F32)
        for qi in range(nq):
            dl_ref[qi] = rows[0:1, qi * tq:(qi + 1) * tq]

    att = jax.ShapeDtypeStruct((S, ATT_W), BF16)
    return pl.pallas_call(
        body, name="attn_bwd_pre", grid=(N_HEADS,),
        in_specs=[_colblock(S, GB), _colblock(S, 0), _colblock(S, 0)],
        out_specs=(_colblock(S, 0), _colblock(S, QKV_W // LANES), _head_row(nq, tq)),
        out_shape=(att, jax.ShapeDtypeStruct((S, ALL_W), BF16), jax.ShapeDtypeStruct((N_HEADS, nq, 1, tq), F32)),
        compiler_params=_params(("parallel",)),
    )(rest, o, dy)


def _attn_bwd(qkv, do, a2, delta, cum, dproj, *, tq):
    S = qkv.shape[0]
    nq = S // tq
    tdot = (((0,), (0,)), ((), ()))
    ndot = (((1,), (1,)), ((), ()))

    def body(q_ref, k_ref, v_ref, do_ref, a_ref, dl_ref, cum_ref, dproj_in, dproj_ref, dsq_ref, dsk_ref,
             dqa, dq_ref, dk_ref, dv_ref, cc, sems):
        h = pl.program_id(0)
        dqa[...] = jnp.zeros_like(dqa)
        dsq_ref[...] = jnp.zeros_like(dsq_ref)
        cc[...] = _lane_of(cum_ref, h) * LOG2E

        @pl.when(h == 0)
        def _():
            dsk_ref[...] = jnp.zeros_like(dsk_ref)

        tri = (lax.broadcasted_iota(jnp.int32, (tq, tq), 0) <= lax.broadcasted_iota(jnp.int32, (tq, tq), 1))
        mine = lax.broadcasted_iota(jnp.int32, (tq, LANES), 1) == h

        def k_step(kj, _):
            cols = pl.ds(pl.multiple_of(kj * tq, tq), tq)
            k = k_ref[cols, :]
            v = v_ref[cols, :]
            cj = cc[cols, :]

            def tile(qi, carry, masked):
                dk, dv, ksum = carry
                rows = pl.ds(pl.multiple_of(qi * tq, tq), tq)
                q = q_ref[rows, :]
                dot = do_ref[rows, :]
                s = lax.dot_general(k, q, ndot, preferred_element_type=F32)
                p = jnp.exp2(s * (SCALE * LOG2E) + a_ref[qi] - cj)
                if masked:
                    p = jnp.where(tri, p, 0.0)
                dv = dv + jnp.dot(p.astype(BF16), dot, preferred_element_type=F32)
                dp = lax.dot_general(v, dot, ndot, preferred_element_type=F32)
                ds = p * (dp - dl_ref[qi])
                dsq_ref[qi] += jnp.sum(ds, axis=0, keepdims=True)
                ksum = ksum + jnp.sum(ds, axis=1, keepdims=True)
                dsb = (ds * SCALE).astype(BF16)
                dk = dk + jnp.dot(dsb, q, preferred_element_type=F32)
                dqa[rows, :] += lax.dot_general(dsb, k, tdot, preferred_element_type=F32)
                return dk, dv, ksum

            init = (jnp.zeros((tq, HEAD_DIM), F32), jnp.zeros((tq, HEAD_DIM), F32), jnp.zeros((tq, 1), F32))
            carry = tile(kj, init, True)
            dk, dv, ksum = lax.fori_loop(kj + 1, nq, lambda qi, c: tile(qi, c, False), carry)
            dk_ref[cols, :] = dk.astype(BF16)
            dv_ref[cols, :] = dv.astype(BF16)
            dsk_ref[cols, :] += jnp.where(mine, ksum, 0.0)
            return 0

        lax.fori_loop(0, nq, k_step, 0)
        dq_ref[...] = dqa[...].astype(BF16)
        _put_columns([dq_ref, dk_ref, dv_ref], dproj_ref, [QB + h, KB + h, VB + h], sems)

    head = pltpu.VMEM((S, HEAD_DIM), BF16)
    return pl.pallas_call(
        body, name="attn_bwd", grid=(N_HEADS,),
        in_specs=[_colblock(S, QB), _colblock(S, KB), _colblock(S, VB), _colblock(S, 0),
                  _head_row(nq, tq), _head_row(nq, tq), pl.BlockSpec((S, LANES), lambda h: (0, 0)), ANY],
        out_specs=(ANY, _head_row(nq, tq), pl.BlockSpec((S, LANES), lambda h: (0, 0))),
        out_shape=(jax.ShapeDtypeStruct(dproj.shape, dproj.dtype), jax.ShapeDtypeStruct((N_HEADS, nq, 1, tq), F32),
                   jax.ShapeDtypeStruct((S, LANES), F32)),
        input_output_aliases={7: 0},
        scratch_shapes=[pltpu.VMEM((S, HEAD_DIM), F32), head, head, head, pltpu.VMEM((S, 1), F32),
                        pltpu.SemaphoreType.DMA((3,))],
        compiler_params=_params(("arbitrary",)),
    )(qkv, qkv, qkv, do, a2, delta, cum, dproj)


def _conv_taps(u, w_ref, row):
    return (w_ref[0:1, :] * _shift_down(u, 2, row) + w_ref[1:2, :] * _shift_down(u, 1, row)) + w_ref[2:3, :] * u


def _conv_fwd(rest, cw, y):
    S = rest.shape[0]

    def body(cb_ref, cc_ref, ch_ref, g_ref, w_ref, y_in, y_ref):
        row = lax.broadcasted_iota(jnp.int32, (S, LANES), 0)
        g = g_ref[...]
        y = _conv_taps(cc_ref[...] * ch_ref[...], w_ref, row)
        y_ref[...] = ((cb_ref[...] * y) * (g * _sigmoid(g))).astype(BF16)

    return pl.pallas_call(
        body, name="conv_fwd", grid=(CONV_W // LANES,),
        in_specs=[_colblock(S, CBB), _colblock(S, CCB), _colblock(S, CHB), _colblock(S, CGB),
                  pl.BlockSpec((CONV_TAPS, LANES), lambda j: (0, j)), ANY],
        out_specs=_colblock(S, ATT_W // LANES),
        out_shape=jax.ShapeDtypeStruct(y.shape, y.dtype),
        input_output_aliases={5: 0},
        compiler_params=_params(("parallel",)),
    )(rest, rest, rest, rest, cw, y)


def _conv_bwd(rest, dy, cw, dproj):
    S = rest.shape[0]

    def body(cb_ref, cc_ref, ch_ref, g_ref, dy_ref, w_ref, dproj_in, dproj_ref, dw_ref,
             dcb_ref, dcc_ref, dch_ref, dg_ref, sems):
        row = lax.broadcasted_iota(jnp.int32, (S, LANES), 0)
        g = g_ref[...]
        sg = _sigmoid(g)
        silu = g * sg
        cb, cc, ch, dy = cb_ref[...], cc_ref[...], ch_ref[...], dy_ref[...]
        u = cc * ch
        u1 = _shift_down(u, 1, row)
        u2 = _shift_down(u, 2, row)
        y = (w_ref[0:1, :] * u2 + w_ref[1:2, :] * u1) + w_ref[2:3, :] * u
        dcb_ref[...] = (dy * silu * y).astype(BF16)
        dg_ref[...] = (dy * (cb * y) * (sg * (1.0 + g * (1.0 - sg)))).astype(BF16)
        dyv = dy * silu * cb
        du = w_ref[2:3, :] * dyv + w_ref[1:2, :] * _shift_up(dyv, 1, row) + w_ref[0:1, :] * _shift_up(dyv, 2, row)
        dcc_ref[...] = (du * ch).astype(BF16)
        dch_ref[...] = (du * cc).astype(BF16)
        dw_ref[0:1, :] = jnp.sum(dyv * u2, axis=0, keepdims=True)
        dw_ref[1:2, :] = jnp.sum(dyv * u1, axis=0, keepdims=True)
        dw_ref[2:3, :] = jnp.sum(dyv * u, axis=0, keepdims=True)
        j = pl.program_id(0)
        first = QKV_W // LANES
        _put_columns([dcb_ref, dcc_ref, dch_ref, dg_ref], dproj_ref,
                     [first + CBB + j, first + CCB + j, first + CHB + j, first + CGB + j], sems)

    piece = pltpu.VMEM((S, LANES), BF16)
    return pl.pallas_call(
        body, name="conv_bwd", grid=(CONV_W // LANES,),
        in_specs=[_colblock(S, CBB), _colblock(S, CCB), _colblock(S, CHB), _colblock(S, CGB),
                  _colblock(S, ATT_W // LANES), pl.BlockSpec((CONV_TAPS, LANES), lambda j: (0, j)), ANY],
        out_specs=(ANY, pl.BlockSpec((CONV_TAPS, LANES), lambda j: (0, j))),
        out_shape=(jax.ShapeDtypeStruct(dproj.shape, dproj.dtype), jax.ShapeDtypeStruct((CONV_TAPS, CONV_W), F32)),
        input_output_aliases={6: 0},
        scratch_shapes=[piece] * 4 + [pltpu.SemaphoreType.DMA((4,))],
        compiler_params=_params(("arbitrary",)),
    )(rest, rest, rest, rest, dy, cw, dproj)


def _pool_z(u, grp, row):
    s2 = u + _shift_down(u, 1, row)
    s4 = s2 + _shift_down(s2, 2, row)
    s8 = s4 + _shift_down(s4, 4, row)
    s16 = s8 + _shift_down(s8, 8, row)
    sw = jnp.where(grp == 0, s2, jnp.where(grp == 1, s4, jnp.where(grp == 2, s8, s16)))
    return sw / _pool_count(grp, row) - u


def _pool_count(grp, row):
    return jnp.minimum(row + 1, 2 << grp).astype(F32)


def _pool_fwd(rest, pw, scale, y):
    S = rest.shape[0]

    def body(u_ref, g_ref, pw_ref, sc_ref, y_in, y_ref):
        grp = pl.program_id(0)
        row = lax.broadcasted_iota(jnp.int32, (S, LANES), 0)
        z = _pool_z(u_ref[...], grp, row)
        zp = jnp.dot(z.astype(BF16), pw_ref[...].astype(BF16), preferred_element_type=F32)
        g = g_ref[...]
        y_ref[...] = ((zp * sc_ref[...]) * (g * _sigmoid(g))).astype(BF16)

    return pl.pallas_call(
        body, name="pool_fwd", grid=(N_POOL,),
        in_specs=[_colblock(S, PUB), _colblock(S, PGB),
                  pl.BlockSpec((None, POOL_GROUP, POOL_GROUP), lambda j: (j, 0, 0)),
                  pl.BlockSpec((1, LANES), lambda j: (0, j)), ANY],
        out_specs=_colblock(S, (ATT_W + CONV_W) // LANES),
        out_shape=jax.ShapeDtypeStruct(y.shape, y.dtype),
        input_output_aliases={4: 0},
        compiler_params=_params(("parallel",)),
    )(rest, rest, pw, scale, y)


def _pool_bwd(rest, dy, pw, scale, dproj):
    S = rest.shape[0]

    def body(u_ref, g_ref, dy_ref, pw_ref, sc_ref, dproj_in, dproj_ref, dpw_ref, dsc_ref, du_ref, dg_ref, sems):
        grp = pl.program_id(0)
        row = lax.broadcasted_iota(jnp.int32, (S, LANES), 0)
        z = _pool_z(u_ref[...], grp, row).astype(BF16)
        pwb = pw_ref[...].astype(BF16)
        zp = jnp.dot(z, pwb, preferred_element_type=F32)
        g = g_ref[...]
        sg = _sigmoid(g)
        silu = g * sg
        dy = dy_ref[...]
        sc = sc_ref[...]
        dsc_ref[...] = jnp.sum(dy * silu * zp, axis=0, keepdims=True)
        dg_ref[...] = (dy * (zp * sc) * (sg * (1.0 + g * (1.0 - sg)))).astype(BF16)
        dzp = (dy * silu * sc).astype(BF16)
        dpw_ref[...] = lax.dot_general(z, dzp, (((0,), (0,)), ((), ())), preferred_element_type=F32)
        dz = lax.dot_general(dzp, pwb, (((1,), (1,)), ((), ())), preferred_element_type=F32)
        f1 = dz / _pool_count(grp, row)
        f2 = f1 + _shift_up(f1, 1, row)
        f4 = f2 + _shift_up(f2, 2, row)
        f8 = f4 + _shift_up(f4, 4, row)
        f16 = f8 + _shift_up(f8, 8, row)
        fw = jnp.where(grp == 0, f2, jnp.where(grp == 1, f4, jnp.where(grp == 2, f8, f16)))
        du_ref[...] = (fw - dz).astype(BF16)
        first = QKV_W // LANES
        _put_columns([du_ref, dg_ref], dproj_ref, [first + PUB + grp, first + PGB + grp], sems)

    piece = pltpu.VMEM((S, LANES), BF16)
    return pl.pallas_call(
        body, name="pool_bwd", grid=(N_POOL,),
        in_specs=[_colblock(S, PUB), _colblock(S, PGB), _colblock(S, (ATT_W + CONV_W) // LANES),
                  pl.BlockSpec((None, POOL_GROUP, POOL_GROUP), lambda j: (j, 0, 0)),
                  pl.BlockSpec((1, LANES), lambda j: (0, j)), ANY],
        out_specs=(ANY, pl.BlockSpec((None, POOL_GROUP, POOL_GROUP), lambda j: (j, 0, 0)),
                   pl.BlockSpec((1, LANES), lambda j: (0, j))),
        out_shape=(jax.ShapeDtypeStruct(dproj.shape, dproj.dtype),
                   jax.ShapeDtypeStruct((N_POOL, POOL_GROUP, POOL_GROUP), F32), jax.ShapeDtypeStruct((1, POOL_W), F32)),
        input_output_aliases={5: 0},
        scratch_shapes=[piece] * 2 + [pltpu.SemaphoreType.DMA((2,))],
        compiler_params=_params(("arbitrary",)),
    )(rest, rest, dy, pw, scale, dproj)


def _ln_fwd(z, g, b, *, ts):
    S, D = z.shape

    def body(z_ref, g_ref, b_ref, o_ref, ob_ref, obt_ref, xh_ref, rs_ref):
        zz = z_ref[...]
        mu = jnp.mean(zz, axis=1, keepdims=True)
        zc = zz - mu
        rstd = lax.rsqrt(jnp.mean(zc * zc, axis=1, keepdims=True) + LN_EPS)
        xh = zc * rstd
        out = xh * g_ref[...] + b_ref[...]
        o_ref[...] = out
        ob_ref[...] = out.astype(BF16)
        obt_ref[...] = out.T.astype(BF16)
        xh_ref[...] = xh
        rs_ref[...] = rstd

    rowblk = pl.BlockSpec((ts, D), lambda i: (i, 0))
    vec = pl.BlockSpec((1, D), lambda i: (0, 0))
    return pl.pallas_call(
        body, name="ln_fwd", grid=(S // ts,),
        in_specs=[rowblk, vec, vec],
        out_specs=(rowblk, rowblk, pl.BlockSpec((D, ts), lambda i: (0, i)), rowblk,
                   pl.BlockSpec((ts, 1), lambda i: (i, 0))),
        out_shape=(jax.ShapeDtypeStruct((S, D), F32), jax.ShapeDtypeStruct((S, D), BF16),
                   jax.ShapeDtypeStruct((D, S), BF16), jax.ShapeDtypeStruct((S, D), F32),
                   jax.ShapeDtypeStruct((S, 1), F32)),
        compiler_params=_params(("parallel",)),
    )(z, g, b)


def _prep(x, after, *, ts):
    S, D = x.shape

    def body(x_ref, after_ref, xb_ref, xbt_ref):
        xb_ref[...] = x_ref[...].astype(BF16)
        xbt_ref[...] = x_ref[...].T.astype(BF16)

    return pl.pallas_call(
        body, name="prep", grid=(S // ts,),
        in_specs=[pl.BlockSpec((ts, D), lambda i: (i, 0)), ANY],
        out_specs=(pl.BlockSpec((ts, D), lambda i: (i, 0)), pl.BlockSpec((D, ts), lambda i: (0, i))),
        out_shape=(jax.ShapeDtypeStruct((S, D), BF16), jax.ShapeDtypeStruct((D, S), BF16)),
        compiler_params=_params(("parallel",)),
    )(x, after)


def _ln_bwd(dout, xhat, rstd, g, *, ts):
    S, D = dout.shape

    def body(d_ref, xh_ref, rs_ref, g_ref, dz_ref, dzb_ref, dg_ref, db_ref):
        @pl.when(pl.program_id(0) == 0)
        def _():
            dg_ref[...] = jnp.zeros_like(dg_ref)
            db_ref[...] = jnp.zeros_like(db_ref)

        d = d_ref[...]
        xh = xh_ref[...]
        dxh = d * g_ref[...]
        m1 = jnp.mean(dxh, axis=1, keepdims=True)
        m2 = jnp.mean(dxh * xh, axis=1, keepdims=True)
        dz = rs_ref[...] * (dxh - m1 - xh * m2)
        dz_ref[...] = dz
        dzb_ref[...] = dz.astype(BF16)
        dg_ref[...] += jnp.sum(d * xh, axis=0, keepdims=True)
        db_ref[...] += jnp.sum(d, axis=0, keepdims=True)

    rowblk = pl.BlockSpec((ts, D), lambda i: (i, 0))
    vec = pl.BlockSpec((1, D), lambda i: (0, 0))
    return pl.pallas_call(
        body, name="ln_bwd", grid=(S // ts,),
        in_specs=[rowblk, rowblk, pl.BlockSpec((ts, 1), lambda i: (i, 0)), vec],
        out_specs=(rowblk, rowblk, vec, vec),
        out_shape=(jax.ShapeDtypeStruct((S, D), F32), jax.ShapeDtypeStruct((S, D), BF16),
                   jax.ShapeDtypeStruct((1, D), F32), jax.ShapeDtypeStruct((1, D), F32)),
        compiler_params=_params(("arbitrary",)),
    )(dout, xhat, rstd, g)


def _loss_head(y, tgt, *, ts):
    S, D = y.shape

    def body(y_ref, t_ref, l_ref, dy_ref):
        @pl.when(pl.program_id(0) == 0)
        def _():
            l_ref[...] = jnp.zeros_like(l_ref)

        e = y_ref[...] - t_ref[...]
        dy_ref[...] = e * (1.0 / D)
        rowloss = jnp.sum(e * e, axis=1, keepdims=True) * (0.5 / D)
        l_ref[...] += jnp.sum(rowloss, axis=0, keepdims=True)

    rowblk = pl.BlockSpec((ts, D), lambda i: (i, 0))
    return pl.pallas_call(
        body, name="loss_head", grid=(S // ts,),
        in_specs=[rowblk, rowblk],
        out_specs=(pl.BlockSpec((1, 1), lambda i: (0, 0)), rowblk),
        out_shape=(jax.ShapeDtypeStruct((1, 1), F32), jax.ShapeDtypeStruct((S, D), F32)),
        compiler_params=_params(("arbitrary",)),
    )(y, tgt)


def _adamw(lands, srcs, w, m, v, *, name, first=0, into=None):
    R, C = w.shape
    npart = len(lands)
    rp = lands[0].shape[1]
    assert R % rp == 0 and all(p.shape == (N_DEV, rp, C) for p in lands)
    tiled = rp % 8 == 0
    assert tiled or (first == 0 and npart * rp == R and into is None)
    tr = min(ADAMW_BLOCK_ELEMS // (LANES * pl.cdiv(C, LANES)), rp) if tiled else rp
    assert rp % tr == 0
    steps = rp // tr
    c1 = 1.0 - ADAM_B1 ** ADAM_STEP
    c2 = 1.0 - ADAM_B2 ** ADAM_STEP
    slabbed = [p.ndim == 3 for p in srcs]
    n_into = 0 if into is None else 4

    def body(*refs):
        l_refs, s_refs = refs[:npart], refs[npart:2 * npart]
        w_ref, m_ref, v_ref = refs[2 * npart:2 * npart + 3]
        g_ref, d_ref, nm_ref, nv_ref = refs[2 * npart + 3 + n_into:]
        x, y, c = _my_place()
        me = 4 * x + 2 * y + c

        def update(l, rows):
            g = None
            for j in range(N_DEV):
                own = s_refs[l][j] if slabbed[l] else s_refs[l][...]
                p = jnp.where(me == j, own, l_refs[l][j]).astype(F32)
                g = p if g is None else g + p
            nm = ADAM_B1 * m_ref[rows, :] + (1.0 - ADAM_B1) * g
            nv = ADAM_B2 * v_ref[rows, :] + (1.0 - ADAM_B2) * (g * g)
            g_ref[rows, :] = g
            nm_ref[rows, :] = nm
            nv_ref[rows, :] = nv
            d_ref[rows, :] = -ADAM_LR * ((nm / c1) / (jnp.sqrt(nv / c2) + ADAM_EPS) + ADAM_WD * w_ref[rows, :])

        for l in range(npart):
            if tiled:
                pl.when(pl.program_id(0) == l)(lambda l=l: update(l, slice(None)))
            else:
                update(l, slice(l * rp, (l + 1) * rp))

    def part_spec(l, p):
        if p.ndim == 3:
            return pl.BlockSpec((N_DEV, tr, C), lambda q, i: (0, jnp.where(q == l, i, 0), 0))
        return pl.BlockSpec((tr, C), lambda q, i: (jnp.where(q == l, i, 0), 0))

    if tiled:
        blk = pl.BlockSpec((tr, C), lambda q, i: ((first + q) * steps + i, 0))
    else:
        blk = pl.BlockSpec((R, C), lambda q, i: (0, 0))
    out = jax.ShapeDtypeStruct((R, C), F32)
    n_in = 2 * npart + 3
    return pl.pallas_call(
        body, name=name, grid=(npart, steps) if tiled else (1, 1),
        in_specs=[part_spec(l, p) for l, p in enumerate(lands)] + [part_spec(l, p) for l, p in enumerate(srcs)]
        + [blk, blk, blk] + [ANY] * n_into,
        out_specs=(blk, blk, blk, blk),
        out_shape=(out, out, out, out),
        input_output_aliases={n_in + i: i for i in range(n_into)},
        compiler_params=_params(("arbitrary", "arbitrary")),
    )(*lands, *srcs, w, m, v, *(into or ()))


def _my_place():
    return lax.axis_index("x"), lax.axis_index("y"), lax.axis_index("c")


HBM = pl.BlockSpec(memory_space=pltpu.HBM)
SEM = pl.BlockSpec(memory_space=pltpu.SEMAPHORE)
DATAFLOW = pltpu.SideEffectType.DATAFLOW_SIDE_EFFECTING
EVERYONE = (1, 2, 3, 4, 5, 6, 7)
SIBLING = 1
SAME_CORE = (4, 2, 6)


def _xchg_copies(src, land, send_sem, recv_sem, gather, peers):
    x, y, c = _my_place()
    me = 4 * x + 2 * y + c
    copies = []
    for k, bits in enumerate(peers):
        px, py, pc = x ^ ((bits >> 2) & 1), y ^ ((bits >> 1) & 1), c ^ (bits & 1)
        peer = 4 * px + 2 * py + pc
        for a in range(len(src)):
            i = a * len(peers) + k
            out = src[a] if gather[a] else src[a].at[peer]
            sent, landed = [pltpu.make_async_remote_copy(
                src_ref=out, dst_ref=land[a].at[slab], send_sem=send_sem.at[i], recv_sem=recv_sem.at[i],
                device_id=(px, py, pc), device_id_type=MESH) for slab in (me, peer)]
            copies.append((sent, landed))
    return copies


def _xchg_start(srcs, *, gather, name, peers=EVERYONE, after=None):
    n = len(srcs)
    follows = [] if after is None else [after]
    lands = [lax.empty(((N_DEV,) + v.shape) if g else v.shape, v.dtype) for v, g in zip(srcs, gather)]

    def body(*refs):
        src, land = refs[:n], refs[n:2 * n]
        send_sem, recv_sem = refs[2 * n + len(follows)], refs[2 * n + len(follows) + 1]
        token = refs[-1]
        for sent, _ in _xchg_copies(src, land, send_sem, recv_sem, gather, peers):
            sent.start()
        token[...] = jnp.zeros_like(token)

    sems = pltpu.SemaphoreType.DMA((n * len(peers),))
    outs = pl.pallas_call(
        body, name=name,
        out_shape=(sems, sems, *[pltpu.HBM(v.shape, v.dtype) for v in srcs + lands],
                   jax.ShapeDtypeStruct((8, LANES), F32)),
        in_specs=[HBM] * (2 * n) + [ANY] * len(follows),
        out_specs=(SEM, SEM, *[HBM] * (2 * n), pl.BlockSpec(memory_space=pltpu.VMEM)),
        input_output_aliases={i: 2 + i for i in range(2 * n)},
        compiler_params=pltpu.CompilerParams(has_side_effects=DATAFLOW),
    )(*[pltpu.with_memory_space_constraint(v, pltpu.HBM) for v in srcs + lands], *follows)
    return dict(send=outs[0], recv=outs[1], srcs=list(outs[2:2 + n]), lands=list(outs[2 + n:2 + 2 * n]),
                token=outs[-1], gather=gather, peers=peers)


def _xchg_wait(h, after, *, name):
    n = len(h["srcs"])
    gather, peers = h["gather"], h["peers"]

    def body(*refs):
        src, land = refs[:n], refs[n:2 * n]
        send_sem, recv_sem = refs[2 * n], refs[2 * n + 1]
        for sent, landed in _xchg_copies(src, land, send_sem, recv_sem, gather, peers):
            sent.wait_send()
            landed.wait_recv()

    thru = h["srcs"] + h["lands"]
    outs = pl.pallas_call(
        body, name=name,
        out_shape=[pltpu.HBM(v.shape, v.dtype) for v in thru],
        in_specs=[HBM] * (2 * n) + [SEM, SEM, ANY],
        out_specs=[HBM] * (2 * n),
        input_output_aliases={i: i for i in range(2 * n)},
        compiler_params=pltpu.CompilerParams(has_side_effects=DATAFLOW),
    )(*thru, h["send"], h["recv"], after)
    return list(outs[:n]), list(outs[n:])


def _relay_copies(land, send_sem, recv_sem):
    x, y, c = _my_place()
    me = 4 * x + 2 * y + c
    copies = []
    for k, bits in enumerate(SAME_CORE):
        for a in range(len(land)):
            i = a * len(SAME_CORE) + k
            sent, landed = [pltpu.make_async_remote_copy(
                src_ref=land[a].at[me ^ bits], dst_ref=land[a].at[slab], send_sem=send_sem.at[i], recv_sem=recv_sem.at[i],
                device_id=(x, y, 1 - c), device_id_type=MESH) for slab in (me ^ bits, me ^ bits ^ SIBLING)]
            copies.append((sent, landed))
    return copies


def _relay_start(lands, *, name):
    n = len(lands)

    def body(*refs):
        land = refs[:n]
        send_sem, recv_sem = refs[n], refs[n + 1]
        token = refs[-1]
        for sent, _ in _relay_copies(land, send_sem, recv_sem):
            sent.start()
        token[...] = jnp.zeros_like(token)

    sems = pltpu.SemaphoreType.DMA((n * len(SAME_CORE),))
    outs = pl.pallas_call(
        body, name=name,
        out_shape=(sems, sems, *[pltpu.HBM(v.shape, v.dtype) for v in lands], jax.ShapeDtypeStruct((8, LANES), F32)),
        in_specs=[HBM] * n,
        out_specs=(SEM, SEM, *[HBM] * n, pl.BlockSpec(memory_space=pltpu.VMEM)),
        input_output_aliases={i: 2 + i for i in range(n)},
        compiler_params=pltpu.CompilerParams(has_side_effects=DATAFLOW),
    )(*[pltpu.with_memory_space_constraint(v, pltpu.HBM) for v in lands])
    return dict(send=outs[0], recv=outs[1], lands=list(outs[2:2 + n]), token=outs[-1])


def _relay_wait(h, after, *, name):
    n = len(h["lands"])

    def body(*refs):
        land = refs[:n]
        send_sem, recv_sem = refs[n], refs[n + 1]
        for sent, landed in _relay_copies(land, send_sem, recv_sem):
            sent.wait_send()
            landed.wait_recv()

    outs = pl.pallas_call(
        body, name=name,
        out_shape=[pltpu.HBM(v.shape, v.dtype) for v in h["lands"]],
        in_specs=[HBM] * n + [SEM, SEM, ANY],
        out_specs=[HBM] * n,
        input_output_aliases={i: i for i in range(n)},
        compiler_params=pltpu.CompilerParams(has_side_effects=DATAFLOW),
    )(*h["lands"], h["send"], h["recv"], after)
    return list(outs)


def _with_own(land, own, me):
    slot = lax.broadcasted_iota(jnp.int32, land.shape, 0)
    return jnp.where(slot == me, own if own.ndim == land.ndim else own[None], land)


def _w_in_segments(j, cols):
    lo, hi = j * cols, (j + 1) * cols
    segs = []
    for a, b, where, shift in ((0, FG_AT, "main", 0), (FG_AT, FG_AT + N_HEADS, "fg", -FG_AT),
                               (FG_AT + N_HEADS, IN_W, "main", -N_HEADS)):
        s0, s1 = max(lo, a), min(hi, b)
        if s0 < s1:
            segs.append((s0 - lo, s1 - s0, where, s0 + shift))
    return segs


SLAB_W = 1024


def _window(off, width):
    base = min((off // LANES) * LANES, width - SLAB_W)
    return base, off - base


def _w_unpack(land, own, *, tr):
    _, D, cols = land.shape
    assert cols + LANES - 1 <= SLAB_W

    def body(land_ref, own_ref, main_ref, slab):
        x, y, c = _my_place()
        me = 4 * x + 2 * y + c
        lane = lax.broadcasted_iota(jnp.int32, (tr, SLAB_W), 1)
        main_ref[...] = jnp.zeros_like(main_ref)
        slab[...] = jnp.zeros_like(slab)
        for j in range(N_DEV):
            slab[:, :cols] = jnp.where(me == j, own_ref[...], land_ref[j])
            v = slab[...]
            segs = _w_in_segments(j, cols)
            for lo, n, where, dst in segs:
                part = v if len(segs) == 1 else jnp.where((lane >= lo) & (lane < lo + n), v, jnp.zeros_like(v))
                if where == "fg":
                    main_ref[:, MAIN_W:] = pltpu.roll(part, (SLAB_W - lo) % SLAB_W, 1)[:, :LANES]
                else:
                    base, r = _window(dst - lo, MAIN_W)
                    assert 0 <= r and lo + n + r <= SLAB_W
                    main_ref[:, base:base + SLAB_W] += pltpu.roll(part, r, 1)

    return pl.pallas_call(
        body, name="w_unpack", grid=(D // tr,),
        in_specs=[pl.BlockSpec((N_DEV, tr, cols), lambda i: (0, i, 0)), pl.BlockSpec((tr, cols), lambda i: (i, 0))],
        out_specs=pl.BlockSpec((tr, ALL_W), lambda i: (i, 0)),
        out_shape=jax.ShapeDtypeStruct((D, ALL_W), BF16),
        scratch_shapes=[pltpu.VMEM((tr, SLAB_W), BF16)],
        compiler_params=_params(("parallel",)),
    )(land, own)


def _g_pack(dw_main, dw_fg, cols, *, tr):
    D = dw_main.shape[0]

    def body(main_ref, fg_ref, out_ref, slab):
        lane = lax.broadcasted_iota(jnp.int32, (tr, SLAB_W), 1)
        slab[...] = jnp.zeros_like(slab)
        slab[:, :LANES] = fg_ref[...]
        fg = slab[...]
        for j in range(N_DEV):
            v = None
            for lo, n, where, src in _w_in_segments(j, cols):
                if where == "fg":
                    part = pltpu.roll(fg, lo, 1)
                else:
                    base, r = _window(src - lo, MAIN_W)
                    assert 0 <= r and lo + n + r <= SLAB_W
                    part = pltpu.roll(main_ref[:, base:base + SLAB_W], (SLAB_W - r) % SLAB_W, 1)
                v = part if v is None else jnp.where((lane >= lo) & (lane < lo + n), part, v)
            out_ref[j] = v[:, :cols]

    return pl.pallas_call(
        body, name="g_pack", grid=(D // tr,),
        in_specs=[pl.BlockSpec((tr, MAIN_W), lambda i: (i, 0)), pl.BlockSpec((tr, LANES), lambda i: (i, 0))],
        out_specs=pl.BlockSpec((N_DEV, tr, cols), lambda i: (0, i, 0)),
        out_shape=jax.ShapeDtypeStruct((N_DEV, D, cols), BF16),
        scratch_shapes=[pltpu.VMEM((tr, SLAB_W), BF16)],
        compiler_params=_params(("parallel",)),
    )(dw_main, dw_fg)


TQ = 512
TS = 256
W_TILE_ROWS = 128


def _layer_fwd_proj(xb, w_in):
    qkv = _mm(xb, w_in, out_dtype=BF16, tm=2048, tn=512, tk=D_MODEL, name="mm_qkv", n_cols=QKV_W)
    rest = _mm(xb, w_in, out_dtype=F32, tm=2048, tn=512, tk=D_MODEL, name="mm_rest", n_off=QKV_W, n_cols=REST_W)
    fgp = _mm(xb, w_in, out_dtype=F32, tm=1024, tn=LANES, tk=D_MODEL, name="mm_fg", n_off=MAIN_W, n_cols=LANES)
    return qkv, rest, fgp


def _layer_fwd_mix(x, xbt, proj, wl):
    qkv, rest, fgp = proj
    cum = _fg_fwd(fgp, wl["b_f"])
    S = x.shape[0]
    cumr = (cum[:, :N_HEADS].T * LOG2E).reshape(N_HEADS, S // TQ, 1, TQ)
    o, y, lse = _attn_fwd(qkv, rest, cum, cumr, tq=TQ)
    y = _conv_fwd(rest, wl["conv_w"], y)
    y = _pool_fwd(rest, wl["pool_w"], wl["pool_scale"], y)
    z = _mm(y, wl["w_out"], out_dtype=F32, tm=2048, tn=512, tk=D_MODEL, name="mm_out", c=x, c_scale=ALPHA)
    out, outb, outbt, xhat, rstd = _ln_fwd(z, wl["ln_g"], wl["ln_b"], ts=TS)
    saved = dict(xbt=xbt, qkv=qkv, rest=rest, fgp=fgp, cum=cum, cumr=cumr, o=o, lse=lse, y=y, xhat=xhat, rstd=rstd)
    return out, outb, outbt, saved


def _layer_bwd_mix(dout, sv, wl):
    S = dout.shape[0]
    dz, dzb, dln_g, dln_b = _ln_bwd(dout, sv["xhat"], sv["rstd"], wl["ln_g"], ts=TS)
    dy = _mm(dzb, wl["w_out"], out_dtype=F32, tm=2048, tn=512, tk=D_MODEL, name="mm_dy", bt=True)
    dw_out = _mm(sv["y"], dzb, out_dtype=BF16, tm=512, tn=D_MODEL, tk=2048, name="mm_dw_out", at=True)
    rest = sv["rest"]
    do, dproj, delta = _attn_bwd_pre(rest, sv["o"], dy, tq=TQ)
    a2 = sv["cumr"] - sv["lse"] * LOG2E
    dproj, ds_q, ds_k = _attn_bwd(sv["qkv"], do, a2, delta, sv["cum"], dproj, tq=TQ)
    dcum = jnp.pad(ds_q.reshape(N_HEADS, S).T, ((0, 0), (0, LANES - N_HEADS))) - ds_k
    dproj, db_f = _fg_bwd(dcum, sv["fgp"], wl["b_f"], dproj)
    dproj, dconv_w = _conv_bwd(rest, dy, wl["conv_w"], dproj)
    dproj, dpool_w, dpool_scale = _pool_bwd(rest, dy, wl["pool_w"], wl["pool_scale"], dproj)
    grads = dict(b_f=db_f[:, :N_HEADS], conv_w=dconv_w, pool_w=dpool_w, pool_scale=dpool_scale, w_out=dw_out,
                 ln_g=dln_g, ln_b=dln_b)
    return grads, (dproj, dz)


def _layer_bwd_w_in(mid, sv):
    dproj, dz = mid
    xt = sv["xbt"]
    S = xt.shape[1]
    dw_main = _mm(xt, dproj, out_dtype=BF16, tm=1024, tn=512, tk=S, name="mm_dw_main", n_cols=MAIN_W)
    dw_fg = _mm(xt, dproj, out_dtype=BF16, tm=1024, tn=LANES, tk=2048, name="mm_dw_fg", n_off=MAIN_W, n_cols=LANES)
    return (dw_main, dw_fg), (dproj, dz)


def _layer_bwd_input(ctx, wl, after):
    dproj, dz = ctx
    return _mm(dproj, wl["w_in"], out_dtype=F32, tm=1024, tn=256, tk=ALL_W, name="mm_dx", c=dz, c_scale=ALPHA, bt=True,
               after=after)


def kernel(x, w_in, b_f, conv_w, pool_w, pool_scale, w_out, ln_g, ln_b, loss_target, m_w_in, m_b_f, m_conv_w, m_pool_w, m_pool_scale, m_w_out, m_ln_g, m_ln_b, v_w_in, v_b_f, v_conv_w, v_pool_w, v_pool_scale, v_w_out, v_ln_g, v_ln_b):
    L, D, cols = w_in.shape
    rows_out = w_out.shape[1]
    ccols = conv_w.shape[2]
    mx, my, mc = _my_place()
    me = 4 * mx + 2 * my + mc

    w_in_b, w_out_b = w_in.astype(BF16), w_out.astype(BF16)
    gathers = []
    for l in range(L):
        gathers.append(_xchg_start([w_in_b[l], w_out_b[l], conv_w[l]], gather=[True] * 3, name=f"gather_start_{l}",
                                   peers=(SIBLING,) + SAME_CORE if l == 0 else EVERYONE,
                                   after=gathers[-1]["token"] if gathers else None))
    started = gathers[-1]["token"]

    xl = x[0]
    xb, xbt = _prep(xl, started, ts=TS)
    weights, saved = [], []
    for l in range(L):
        (own_in, own_out, own_cw), (g_in, g_out, g_cw) = _xchg_wait(gathers[l], xb, name=f"gather_wait_{l}")
        if l == 0:
            g_in, g_out, g_cw = _relay_wait(_relay_start([g_in, g_out, g_cw], name="gather_relay_start"), xb,
                                            name="gather_relay_wait")
        w_all = _w_unpack(g_in, own_in, tr=W_TILE_ROWS)
        proj = _layer_fwd_proj(xb, w_all)
        w_o = _with_own(g_out, own_out, me).reshape(N_DEV * rows_out, D)
        cw = jnp.transpose(_with_own(g_cw, own_cw, me), (1, 0, 2)).reshape(CONV_TAPS, N_DEV * ccols)
        wl = dict(w_in=w_all, w_out=w_o, conv_w=cw,
                  b_f=jnp.pad(b_f[l][None, :], ((0, 0), (0, LANES - N_HEADS))), pool_w=pool_w[l],
                  pool_scale=pool_scale[l][None, :], ln_g=ln_g[l][None, :], ln_b=ln_b[l][None, :])
        xl, xb, xbt, sv = _layer_fwd_mix(xl, xbt, proj, wl)
        weights.append(wl)
        saved.append(sv)

    loss, dx = _loss_head(xl, loss_target[0], ts=TS)
    loss = lax.psum(loss[0, 0], ("x", "y", "c"))

    names = ["w_in", "w_out", "conv_w", "b_f", "pool_w", "pool_scale", "ln_g", "ln_b"]
    sharded = names[:3]
    scatters = [None] * L
    for l in reversed(range(L)):
        g, mid = _layer_bwd_mix(dx, saved[l], weights[l])
        (dw_main, dw_fg), ctx = _layer_bwd_w_in(mid, saved[l])
        g["w_in"] = _g_pack(dw_main, dw_fg, cols, tr=W_TILE_ROWS)
        g["w_out"] = g["w_out"].reshape(N_DEV, rows_out, D)
        g["conv_w"] = jnp.transpose(g["conv_w"].reshape(CONV_TAPS, N_DEV, ccols), (1, 0, 2))
        scatters[l] = _xchg_start([g[k] for k in names], gather=[k not in sharded for k in names],
                                  name=f"scatter_start_{l}")
        dx = _layer_bwd_input(ctx, weights[l], scatters[l]["token"])

    given = dict(w_in=(w_in, m_w_in, v_w_in), b_f=(b_f, m_b_f, v_b_f), conv_w=(conv_w, m_conv_w, v_conv_w),
                 pool_w=(pool_w, m_pool_w, v_pool_w), pool_scale=(pool_scale, m_pool_scale, v_pool_scale),
                 w_out=(w_out, m_w_out, v_w_out), ln_g=(ln_g, m_ln_g, v_ln_g), ln_b=(ln_b, m_ln_b, v_ln_b))
    sent, landed = [{} for _ in range(L)], [{} for _ in range(L)]

    def wait(l, after):
        s_, l_ = _xchg_wait(scatters[l], after, name=f"scatter_wait_{l}")
        sent[l].update(zip(names, s_))
        landed[l].update(zip(names, l_))

    def update(k, layers, into=None):
        w, m, v = given[k]
        C = w.shape[-1]
        w2, m2, v2 = (t.reshape(-1, C) for t in (w, m, v))
        rp = w2.shape[0] // L
        lands = [landed[l][k].reshape(N_DEV, rp, C) for l in layers]
        srcs = [sent[l][k].reshape((N_DEV, rp, C) if k in sharded else (rp, C)) for l in layers]
        return _adamw(lands, srcs, w2, m2, v2, name=f"adamw_{k}_{layers[0]}", first=layers[0], into=into)

    big = ["w_in", "w_out", "pool_w"]
    for l in reversed(range(1, L)):
        wait(l, dx)
    upper = {k: update(k, list(range(1, L))) for k in big} if L > 1 else {}
    wait(0, upper["w_in"][0] if upper else dx)
    res = {k: update(k, [0], into=upper[k]) if k in upper else update(k, list(range(L))) for k in names}
    res = {k: [t.reshape(given[k][0].shape) for t in outs] for k, outs in res.items()}

    order = ["w_in", "b_f", "conv_w", "pool_w", "pool_scale", "w_out", "ln_g", "ln_b"]
    return (loss, dx[None], *[res[k][0] for k in order], *[res[k][1] for k in order],
            *[res[k][2] for k in order], *[res[k][3] for k in order])
```

```python
import jax
import jax.numpy as jnp
from jax import lax
from jax.experimental import pallas as pl
from jax.experimental.pallas import tpu as pltpu

F32 = jnp.float32
BF16 = jnp.bfloat16

N_DEV = 8
D_MODEL = 2048
N_HEADS = 8
HEAD_DIM = 128
ATT_W = N_HEADS * HEAD_DIM
CONV_W = 512
CONV_TAPS = 3
POOL_W = 512
POOL_GROUP = 128
N_POOL = POOL_W // POOL_GROUP
FG_AT = 4 * ATT_W
IN_W = FG_AT + N_HEADS + 4 * CONV_W + 2 * POOL_W
MAIN_W = IN_W - N_HEADS
LANES = 128
ALL_W = MAIN_W + LANES
LN_EPS = 1e-5
DEPTH = 4
ALPHA = (2 * DEPTH) ** 0.25
SCALE = HEAD_DIM ** -0.5
LOG2E = 1.4426950408889634
LN2 = 0.6931471805599453
NEG = -1e30

ADAM_LR, ADAM_B1, ADAM_B2, ADAM_EPS, ADAM_WD, ADAM_STEP = 0.001, 0.9, 0.999, 1e-08, 0.01, 10

QKV_W = 3 * ATT_W
REST_W = MAIN_W - QKV_W
QB, KB, VB = 0, 8, 16
GB = 0
CBB, CCB, CHB, CGB = 8, 12, 16, 20
PUB, PGB = 24, 28
FGB = MAIN_W // LANES
VMEM_LIMIT = 48 * 1024 * 1024
ADAMW_BLOCK_ELEMS = 64 * 1024

MESH = pl.DeviceIdType.MESH
ANY = pl.BlockSpec(memory_space=pl.ANY)


def _params(semantics):
    return pltpu.CompilerParams(dimension_semantics=semantics, vmem_limit_bytes=VMEM_LIMIT)


def _sigmoid(x):
    return 1.0 / (1.0 + jnp.exp(-x))


def _mm(a, b, *, out_dtype, tm, tn, tk, name, c=None, c_scale=1.0, n_off=0, n_cols=None, bt=False, at=False,
        after=None):
    M, K = a.shape[::-1] if at else a.shape
    N = b.shape[0 if bt else 1] if n_cols is None else n_cols
    assert not (bt and n_off)
    dims = (((0 if at else 1,), (1 if bt else 0,)), ((), ()))
    tm, tn, tk = min(tm, M), min(tn, N), min(tk, K)
    assert M % tm == 0 and N % tn == 0 and K % tk == 0 and n_off % tn == 0, (a.shape, b.shape, tm, tn, tk)
    nk, joff = K // tk, n_off // tn
    has_c, has_after = c is not None, after is not None

    def body(*refs):
        a_ref, b_ref = refs[0], refs[1]
        c_ref = refs[2] if has_c else None
        o_ref = refs[2 + has_c + has_after]

        def finish(r):
            if has_c:
                r = r + c_scale * c_ref[...]
            o_ref[...] = r.astype(out_dtype)

        if nk == 1:
            finish(lax.dot_general(a_ref[...], b_ref[...], dims, preferred_element_type=F32))
        else:
            acc_ref = refs[3 + has_c + has_after]
            k = pl.program_id(2)

            @pl.when(k == 0)
            def _():
                acc_ref[...] = jnp.zeros_like(acc_ref)

            acc_ref[...] += lax.dot_general(a_ref[...], b_ref[...], dims, preferred_element_type=F32)

            @pl.when(k == nk - 1)
            def _():
                finish(acc_ref[...])

    in_specs = [pl.BlockSpec((tk, tm), lambda i, j, k: (k, i)) if at else pl.BlockSpec((tm, tk), lambda i, j, k: (i, k)),
                pl.BlockSpec((tn, tk), lambda i, j, k: (j, k)) if bt
                else pl.BlockSpec((tk, tn), lambda i, j, k: (k, j + joff))]
    args = [a, b]
    if has_c:
        in_specs.append(pl.BlockSpec((tm, tn), lambda i, j, k: (i, j)))
        args.append(c)
    if has_after:
        in_specs.append(ANY)
        args.append(after)
    return pl.pallas_call(
        body, name=name, grid=(M // tm, N // tn, nk),
        in_specs=in_specs,
        out_specs=pl.BlockSpec((tm, tn), lambda i, j, k: (i, j)),
        out_shape=jax.ShapeDtypeStruct((M, N), out_dtype),
        scratch_shapes=[pltpu.VMEM((tm, tn), F32)] if nk > 1 else [],
        compiler_params=_params(("parallel", "parallel", "arbitrary")),
    )(*args)


def _shift_down(x, k, row):
    return jnp.where(row >= k, pltpu.roll(x, k, 0), 0.0)


def _shift_up(x, k, row):
    s = x.shape[0]
    return jnp.where(row < s - k, pltpu.roll(x, s - k, 0), 0.0)


def _fg_fwd(fgp, bf):
    S = fgp.shape[0]

    def body(f_ref, b_ref, cum_ref):
        a = f_ref[...] + b_ref[...]
        x = jnp.minimum(a, 0.0) - jnp.log(1.0 + jnp.exp(-jnp.abs(a)))
        row = lax.broadcasted_iota(jnp.int32, x.shape, 0)
        k = 1
        while k < S:
            x = x + _shift_down(x, k, row)
            k *= 2
        cum_ref[...] = x

    return pl.pallas_call(body, name="fg_fwd", out_shape=jax.ShapeDtypeStruct((S, LANES), F32),
                          compiler_params=_params(None))(fgp, bf)


def _fg_bwd(dcum, fgp, bf, dproj):
    S = fgp.shape[0]

    def body(d_ref, f_ref, b_ref, dproj_in, dfg_ref, dbf_ref):
        x = d_ref[...]
        row = lax.broadcasted_iota(jnp.int32, x.shape, 0)
        k = 1
        while k < S:
            x = x + _shift_up(x, k, row)
            k *= 2
        dfg = x * _sigmoid(-(f_ref[...] + b_ref[...]))
        dfg_ref[...] = dfg.astype(BF16)
        dbf_ref[...] = jnp.sum(dfg, axis=0, keepdims=True)

    whole = pl.BlockSpec((S, LANES), lambda i: (0, 0))
    vec = pl.BlockSpec((1, LANES), lambda i: (0, 0))
    return pl.pallas_call(
        body, name="fg_bwd", grid=(1,),
        in_specs=[whole, whole, vec, ANY],
        out_specs=(pl.BlockSpec((S, LANES), lambda i: (0, FGB)), vec),
        out_shape=(jax.ShapeDtypeStruct(dproj.shape, dproj.dtype), jax.ShapeDtypeStruct((1, LANES), F32)),
        input_output_aliases={3: 0},
        compiler_params=_params(("arbitrary",)),
    )(dcum, fgp, bf, dproj)


def _put_columns(pieces, dst, blocks, sems):
    copies = [pltpu.make_async_copy(p, dst.at[:, pl.ds(pl.multiple_of(b * LANES, LANES), LANES)], sems.at[i])
              for i, (p, b) in enumerate(zip(pieces, blocks))]
    for cp in copies:
        cp.start()
    for cp in copies:
        cp.wait()


def _colblock(S, off):
    return pl.BlockSpec((S, LANES), lambda h: (0, h + off))


def _head_row(nq, tq):
    return pl.BlockSpec((None, nq, 1, tq), lambda h: (h, 0, 0, 0))


def _lane_of(ref, h):
    lane = lax.broadcasted_iota(jnp.int32, ref.shape, 1)
    return jnp.sum(jnp.where(lane == h, ref[...], 0.0), axis=1, keepdims=True)


def _attn_fwd(qkv, rest, cum, cumr, *, tq):
    S = qkv.shape[0]
    nq = S // tq

    def body(q_ref, k_ref, v_ref, g_ref, cum_ref, cumr_ref, o_ref, y_ref, lse_ref, cc):
        cc[...] = _lane_of(cum_ref, pl.program_id(0)) * LOG2E
        tri = (lax.broadcasted_iota(jnp.int32, (tq, tq), 0) <= lax.broadcasted_iota(jnp.int32, (tq, tq), 1))

        def q_step(qi, _):
            rows = pl.ds(pl.multiple_of(qi * tq, tq), tq)
            q = q_ref[rows, :]
            ci = cumr_ref[qi]

            def tile(kj, carry, masked):
                m, l, acc = carry
                cols = pl.ds(pl.multiple_of(kj * tq, tq), tq)
                s = lax.dot_general(k_ref[cols, :], q, (((1,), (1,)), ((), ())), preferred_element_type=F32)
                s = s * (SCALE * LOG2E) + ci - cc[cols, :]
                if masked:
                    s = jnp.where(tri, s, NEG)
                m_new = jnp.maximum(m, jnp.max(s, axis=0, keepdims=True))
                a = jnp.exp2(m - m_new)
                p = jnp.exp2(s - m_new)
                l = a * l + jnp.sum(p, axis=0, keepdims=True)
                acc = a * acc + lax.dot_general(v_ref[cols, :], p.astype(BF16), (((0,), (0,)), ((), ())),
                                                preferred_element_type=F32)
                return m_new, l, acc

            init = (jnp.full((1, tq), NEG, F32), jnp.zeros((1, tq), F32), jnp.zeros((HEAD_DIM, tq), F32))
            carry = lax.fori_loop(0, qi, lambda kj, c: tile(kj, c, False), init)
            m, l, acc = tile(qi, carry, True)
            o = (acc / l).T
            g = g_ref[rows, :]
            o_ref[rows, :] = o
            y_ref[rows, :] = (o * (g * _sigmoid(g))).astype(BF16)
            lse_ref[qi] = m * LN2 + jnp.log(l)
            return 0

        lax.fori_loop(0, nq, q_step, 0)

    return pl.pallas_call(
        body, name="attn_fwd", grid=(N_HEADS,),
        in_specs=[_colblock(S, QB), _colblock(S, KB), _colblock(S, VB), _colblock(S, GB),
                  pl.BlockSpec((S, LANES), lambda h: (0, 0)), _head_row(nq, tq)],
        out_specs=(_colblock(S, 0), _colblock(S, 0), _head_row(nq, tq)),
        out_shape=(jax.ShapeDtypeStruct((S, ATT_W), F32), jax.ShapeDtypeStruct((S, D_MODEL), BF16),
                   jax.ShapeDtypeStruct((N_HEADS, nq, 1, tq), F32)),
        scratch_shapes=[pltpu.VMEM((S, 1), F32)],
        compiler_params=_params(("arbitrary",)),
    )(qkv, qkv, qkv, rest, cum, cumr)


def _attn_bwd_pre(rest, o, dy, *, tq):
    S = o.shape[0]
    nq = S // tq

    def body(g_ref, o_ref, dy_ref, do_ref, dg_ref, dl_ref):
        g = g_ref[...]
        sg = _sigmoid(g)
        ov = o_ref[...]
        dy = dy_ref[...]
        do = dy * (g * sg)
        do_ref[...] = do.astype(BF16)
        dg_ref[...] = (dy * ov * (sg * (1.0 + g * (1.0 - sg)))).astype(BF16)
        rows = lax.dot_general(jnp.ones((8, HEAD_DIM), F32), do * ov, (((1,), (1,)), ((), ())),
                               precision=lax.Precision.HIGHEST, preferred_element_type=F32)
        for qi in range(nq):
            dl_ref[qi] = rows[0:1, qi * tq:(qi + 1) * tq]

    att = jax.ShapeDtypeStruct((S, ATT_W), BF16)
    return pl.pallas_call(
        body, name="attn_bwd_pre", grid=(N_HEADS,),
        in_specs=[_colblock(S, GB), _colblock(S, 0), _colblock(S, 0)],
        out_specs=(_colblock(S, 0), _colblock(S, QKV_W // LANES), _head_row(nq, tq)),
        out_shape=(att, jax.ShapeDtypeStruct((S, ALL_W), BF16), jax.ShapeDtypeStruct((N_HEADS, nq, 1, tq), F32)),
        compiler_params=_params(("parallel",)),
    )(rest, o, dy)


def _attn_bwd(qkv, do, a2, delta, cum, dproj, *, tq):
    S = qkv.shape[0]
    nq = S // tq
    tdot = (((0,), (0,)), ((), ()))
    ndot = (((1,), (1,)), ((), ()))

    def body(q_ref, k_ref, v_ref, do_ref, a_ref, dl_ref, cum_ref, dproj_in, dproj_ref, dsq_ref, dsk_ref,
             dqa, dq_ref, dk_ref, dv_ref, cc, sems):
        h = pl.program_id(0)
        dqa[...] = jnp.zeros_like(dqa)
        dsq_ref[...] = jnp.zeros_like(dsq_ref)
        cc[...] = _lane_of(cum_ref, h) * LOG2E

        @pl.when(h == 0)
        def _():
            dsk_ref[...] = jnp.zeros_like(dsk_ref)

        tri = (lax.broadcasted_iota(jnp.int32, (tq, tq), 0) <= lax.broadcasted_iota(jnp.int32, (tq, tq), 1))
        mine = lax.broadcasted_iota(jnp.int32, (tq, LANES), 1) == h

        def k_step(kj, _):
            cols = pl.ds(pl.multiple_of(kj * tq, tq), tq)
            k = k_ref[cols, :]
            v = v_ref[cols, :]
            cj = cc[cols, :]

            def tile(qi, carry, masked):
                dk, dv, ksum = carry
                rows = pl.ds(pl.multiple_of(qi * tq, tq), tq)
                q = q_ref[rows, :]
                dot = do_ref[rows, :]
                s = lax.dot_general(k, q, ndot, preferred_element_type=F32)
                p = jnp.exp2(s * (SCALE * LOG2E) + a_ref[qi] - cj)
                if masked:
                    p = jnp.where(tri, p, 0.0)
                dv = dv + jnp.dot(p.astype(BF16), dot, preferred_element_type=F32)
                dp = lax.dot_general(v, dot, ndot, preferred_element_type=F32)
                ds = p * (dp - dl_ref[qi])
                dsq_ref[qi] += jnp.sum(ds, axis=0, keepdims=True)
                ksum = ksum + jnp.sum(ds, axis=1, keepdims=True)
                dsb = (ds * SCALE).astype(BF16)
                dk = dk + jnp.dot(dsb, q, preferred_element_type=F32)
                dqa[rows, :] += lax.dot_general(dsb, k, tdot, preferred_element_type=F32)
                return dk, dv, ksum

            init = (jnp.zeros((tq, HEAD_DIM), F32), jnp.zeros((tq, HEAD_DIM), F32), jnp.zeros((tq, 1), F32))
            carry = tile(kj, init, True)
            dk, dv, ksum = lax.fori_loop(kj + 1, nq, lambda qi, c: tile(qi, c, False), carry)
            dk_ref[cols, :] = dk.astype(BF16)
            dv_ref[cols, :] = dv.astype(BF16)
            dsk_ref[cols, :] += jnp.where(mine, ksum, 0.0)
            return 0

        lax.fori_loop(0, nq, k_step, 0)
        dq_ref[...] = dqa[...].astype(BF16)
        _put_columns([dq_ref, dk_ref, dv_ref], dproj_ref, [QB + h, KB + h, VB + h], sems)

    head = pltpu.VMEM((S, HEAD_DIM), BF16)
    return pl.pallas_call(
        body, name="attn_bwd", grid=(N_HEADS,),
        in_specs=[_colblock(S, QB), _colblock(S, KB), _colblock(S, VB), _colblock(S, 0),
                  _head_row(nq, tq), _head_row(nq, tq), pl.BlockSpec((S, LANES), lambda h: (0, 0)), ANY],
        out_specs=(ANY, _head_row(nq, tq), pl.BlockSpec((S, LANES), lambda h: (0, 0))),
        out_shape=(jax.ShapeDtypeStruct(dproj.shape, dproj.dtype), jax.ShapeDtypeStruct((N_HEADS, nq, 1, tq), F32),
                   jax.ShapeDtypeStruct((S, LANES), F32)),
        input_output_aliases={7: 0},
        scratch_shapes=[pltpu.VMEM((S, HEAD_DIM), F32), head, head, head, pltpu.VMEM((S, 1), F32),
                        pltpu.SemaphoreType.DMA((3,))],
        compiler_params=_params(("arbitrary",)),
    )(qkv, qkv, qkv, do, a2, delta, cum, dproj)


def _conv_taps(u, w_ref, row):
    return (w_ref[0:1, :] * _shift_down(u, 2, row) + w_ref[1:2, :] * _shift_down(u, 1, row)) + w_ref[2:3, :] * u


def _conv_fwd(rest, cw, y):
    S = rest.shape[0]

    def body(cb_ref, cc_ref, ch_ref, g_ref, w_ref, y_in, y_ref):
        row = lax.broadcasted_iota(jnp.int32, (S, LANES), 0)
        g = g_ref[...]
        y = _conv_taps(cc_ref[...] * ch_ref[...], w_ref, row)
        y_ref[...] = ((cb_ref[...] * y) * (g * _sigmoid(g))).astype(BF16)

    return pl.pallas_call(
        body, name="conv_fwd", grid=(CONV_W // LANES,),
        in_specs=[_colblock(S, CBB), _colblock(S, CCB), _colblock(S, CHB), _colblock(S, CGB),
                  pl.BlockSpec((CONV_TAPS, LANES), lambda j: (0, j)), ANY],
        out_specs=_colblock(S, ATT_W // LANES),
        out_shape=jax.ShapeDtypeStruct(y.shape, y.dtype),
        input_output_aliases={5: 0},
        compiler_params=_params(("parallel",)),
    )(rest, rest, rest, rest, cw, y)


def _conv_bwd(rest, dy, cw, dproj):
    S = rest.shape[0]

    def body(cb_ref, cc_ref, ch_ref, g_ref, dy_ref, w_ref, dproj_in, dproj_ref, dw_ref,
             dcb_ref, dcc_ref, dch_ref, dg_ref, sems):
        row = lax.broadcasted_iota(jnp.int32, (S, LANES), 0)
        g = g_ref[...]
        sg = _sigmoid(g)
        silu = g * sg
        cb, cc, ch, dy = cb_ref[...], cc_ref[...], ch_ref[...], dy_ref[...]
        u = cc * ch
        u1 = _shift_down(u, 1, row)
        u2 = _shift_down(u, 2, row)
        y = (w_ref[0:1, :] * u2 + w_ref[1:2, :] * u1) + w_ref[2:3, :] * u
        dcb_ref[...] = (dy * silu * y).astype(BF16)
        dg_ref[...] = (dy * (cb * y) * (sg * (1.0 + g * (1.0 - sg)))).astype(BF16)
        dyv = dy * silu * cb
        du = w_ref[2:3, :] * dyv + w_ref[1:2, :] * _shift_up(dyv, 1, row) + w_ref[0:1, :] * _shift_up(dyv, 2, row)
        dcc_ref[...] = (du * ch).astype(BF16)
        dch_ref[...] = (du * cc).astype(BF16)
        dw_ref[0:1, :] = jnp.sum(dyv * u2, axis=0, keepdims=True)
        dw_ref[1:2, :] = jnp.sum(dyv * u1, axis=0, keepdims=True)
        dw_ref[2:3, :] = jnp.sum(dyv * u, axis=0, keepdims=True)
        j = pl.program_id(0)
        first = QKV_W // LANES
        _put_columns([dcb_ref, dcc_ref, dch_ref, dg_ref], dproj_ref,
                     [first + CBB + j, first + CCB + j, first + CHB + j, first + CGB + j], sems)

    piece = pltpu.VMEM((S, LANES), BF16)
    return pl.pallas_call(
        body, name="conv_bwd", grid=(CONV_W // LANES,),
        in_specs=[_colblock(S, CBB), _colblock(S, CCB), _colblock(S, CHB), _colblock(S, CGB),
                  _colblock(S, ATT_W // LANES), pl.BlockSpec((CONV_TAPS, LANES), lambda j: (0, j)), ANY],
        out_specs=(ANY, pl.BlockSpec((CONV_TAPS, LANES), lambda j: (0, j))),
        out_shape=(jax.ShapeDtypeStruct(dproj.shape, dproj.dtype), jax.ShapeDtypeStruct((CONV_TAPS, CONV_W), F32)),
        input_output_aliases={6: 0},
        scratch_shapes=[piece] * 4 + [pltpu.SemaphoreType.DMA((4,))],
        compiler_params=_params(("arbitrary",)),
    )(rest, rest, rest, rest, dy, cw, dproj)


def _pool_z(u, grp, row):
    s2 = u + _shift_down(u, 1, row)
    s4 = s2 + _shift_down(s2, 2, row)
    s8 = s4 + _shift_down(s4, 4, row)
    s16 = s8 + _shift_down(s8, 8, row)
    sw = jnp.where(grp == 0, s2, jnp.where(grp == 1, s4, jnp.where(grp == 2, s8, s16)))
    return sw / _pool_count(grp, row) - u


def _pool_count(grp, row):
    return jnp.minimum(row + 1, 2 << grp).astype(F32)


def _pool_fwd(rest, pw, scale, y):
    S = rest.shape[0]

    def body(u_ref, g_ref, pw_ref, sc_ref, y_in, y_ref):
        grp = pl.program_id(0)
        row = lax.broadcasted_iota(jnp.int32, (S, LANES), 0)
        z = _pool_z(u_ref[...], grp, row)
        zp = jnp.dot(z.astype(BF16), pw_ref[...].astype(BF16), preferred_element_type=F32)
        g = g_ref[...]
        y_ref[...] = ((zp * sc_ref[...]) * (g * _sigmoid(g))).astype(BF16)

    return pl.pallas_call(
        body, name="pool_fwd", grid=(N_POOL,),
        in_specs=[_colblock(S, PUB), _colblock(S, PGB),
                  pl.BlockSpec((None, POOL_GROUP, POOL_GROUP), lambda j: (j, 0, 0)),
                  pl.BlockSpec((1, LANES), lambda j: (0, j)), ANY],
        out_specs=_colblock(S, (ATT_W + CONV_W) // LANES),
        out_shape=jax.ShapeDtypeStruct(y.shape, y.dtype),
        input_output_aliases={4: 0},
        compiler_params=_params(("parallel",)),
    )(rest, rest, pw, scale, y)


def _pool_bwd(rest, dy, pw, scale, dproj):
    S = rest.shape[0]

    def body(u_ref, g_ref, dy_ref, pw_ref, sc_ref, dproj_in, dproj_ref, dpw_ref, dsc_ref, du_ref, dg_ref, sems):
        grp = pl.program_id(0)
        row = lax.broadcasted_iota(jnp.int32, (S, LANES), 0)
        z = _pool_z(u_ref[...], grp, row).astype(BF16)
        pwb = pw_ref[...].astype(BF16)
        zp = jnp.dot(z, pwb, preferred_element_type=F32)
        g = g_ref[...]
        sg = _sigmoid(g)
        silu = g * sg
        dy = dy_ref[...]
        sc = sc_ref[...]
        dsc_ref[...] = jnp.sum(dy * silu * zp, axis=0, keepdims=True)
        dg_ref[...] = (dy * (zp * sc) * (sg * (1.0 + g * (1.0 - sg)))).astype(BF16)
        dzp = (dy * silu * sc).astype(BF16)
        dpw_ref[...] = lax.dot_general(z, dzp, (((0,), (0,)), ((), ())), preferred_element_type=F32)
        dz = lax.dot_general(dzp, pwb, (((1,), (1,)), ((), ())), preferred_element_type=F32)
        f1 = dz / _pool_count(grp, row)
        f2 = f1 + _shift_up(f1, 1, row)
        f4 = f2 + _shift_up(f2, 2, row)
        f8 = f4 + _shift_up(f4, 4, row)
        f16 = f8 + _shift_up(f8, 8, row)
        fw = jnp.where(grp == 0, f2, jnp.where(grp == 1, f4, jnp.where(grp == 2, f8, f16)))
        du_ref[...] = (fw - dz).astype(BF16)
        first = QKV_W // LANES
        _put_columns([du_ref, dg_ref], dproj_ref, [first + PUB + grp, first + PGB + grp], sems)

    piece = pltpu.VMEM((S, LANES), BF16)
    return pl.pallas_call(
        body, name="pool_bwd", grid=(N_POOL,),
        in_specs=[_colblock(S, PUB), _colblock(S, PGB), _colblock(S, (ATT_W + CONV_W) // LANES),
                  pl.BlockSpec((None, POOL_GROUP, POOL_GROUP), lambda j: (j, 0, 0)),
                  pl.BlockSpec((1, LANES), lambda j: (0, j)), ANY],
        out_specs=(ANY, pl.BlockSpec((None, POOL_GROUP, POOL_GROUP), lambda j: (j, 0, 0)),
                   pl.BlockSpec((1, LANES), lambda j: (0, j))),
        out_shape=(jax.ShapeDtypeStruct(dproj.shape, dproj.dtype),
                   jax.ShapeDtypeStruct((N_POOL, POOL_GROUP, POOL_GROUP), F32), jax.ShapeDtypeStruct((1, POOL_W), F32)),
        input_output_aliases={5: 0},
        scratch_shapes=[piece] * 2 + [pltpu.SemaphoreType.DMA((2,))],
        compiler_params=_params(("arbitrary",)),
    )(rest, rest, dy, pw, scale, dproj)


def _ln_fwd(z, g, b, *, ts):
    S, D = z.shape

    def body(z_ref, g_ref, b_ref, o_ref, ob_ref, obt_ref, xh_ref, rs_ref):
        zz = z_ref[...]
        mu = jnp.mean(zz, axis=1, keepdims=True)
        zc = zz - mu
        rstd = lax.rsqrt(jnp.mean(zc * zc, axis=1, keepdims=True) + LN_EPS)
        xh = zc * rstd
        out = xh * g_ref[...] + b_ref[...]
        o_ref[...] = out
        ob_ref[...] = out.astype(BF16)
        obt_ref[...] = out.T.astype(BF16)
        xh_ref[...] = xh
        rs_ref[...] = rstd

    rowblk = pl.BlockSpec((ts, D), lambda i: (i, 0))
    vec = pl.BlockSpec((1, D), lambda i: (0, 0))
    return pl.pallas_call(
        body, name="ln_fwd", grid=(S // ts,),
        in_specs=[rowblk, vec, vec],
        out_specs=(rowblk, rowblk, pl.BlockSpec((D, ts), lambda i: (0, i)), rowblk,
                   pl.BlockSpec((ts, 1), lambda i: (i, 0))),
        out_shape=(jax.ShapeDtypeStruct((S, D), F32), jax.ShapeDtypeStruct((S, D), BF16),
                   jax.ShapeDtypeStruct((D, S), BF16), jax.ShapeDtypeStruct((S, D), F32),
                   jax.ShapeDtypeStruct((S, 1), F32)),
        compiler_params=_params(("parallel",)),
    )(z, g, b)


def _prep(x, after, *, ts):
    S, D = x.shape

    def body(x_ref, after_ref, xb_ref, xbt_ref):
        xb_ref[...] = x_ref[...].astype(BF16)
        xbt_ref[...] = x_ref[...].T.astype(BF16)

    return pl.pallas_call(
        body, name="prep", grid=(S // ts,),
        in_specs=[pl.BlockSpec((ts, D), lambda i: (i, 0)), ANY],
        out_specs=(pl.BlockSpec((ts, D), lambda i: (i, 0)), pl.BlockSpec((D, ts), lambda i: (0, i))),
        out_shape=(jax.ShapeDtypeStruct((S, D), BF16), jax.ShapeDtypeStruct((D, S), BF16)),
        compiler_params=_params(("parallel",)),
    )(x, after)


def _ln_bwd(dout, xhat, rstd, g, *, ts):
    S, D = dout.shape

    def body(d_ref, xh_ref, rs_ref, g_ref, dz_ref, dzb_ref, dg_ref, db_ref):
        @pl.when(pl.program_id(0) == 0)
        def _():
            dg_ref[...] = jnp.zeros_like(dg_ref)
            db_ref[...] = jnp.zeros_like(db_ref)

        d = d_ref[...]
        xh = xh_ref[...]
        dxh = d * g_ref[...]
        m1 = jnp.mean(dxh, axis=1, keepdims=True)
        m2 = jnp.mean(dxh * xh, axis=1, keepdims=True)
        dz = rs_ref[...] * (dxh - m1 - xh * m2)
        dz_ref[...] = dz
        dzb_ref[...] = dz.astype(BF16)
        dg_ref[...] += jnp.sum(d * xh, axis=0, keepdims=True)
        db_ref[...] += jnp.sum(d, axis=0, keepdims=True)

    rowblk = pl.BlockSpec((ts, D), lambda i: (i, 0))
    vec = pl.BlockSpec((1, D), lambda i: (0, 0))
    return pl.pallas_call(
        body, name="ln_bwd", grid=(S // ts,),
        in_specs=[rowblk, rowblk, pl.BlockSpec((ts, 1), lambda i: (i, 0)), vec],
        out_specs=(rowblk, rowblk, vec, vec),
        out_shape=(jax.ShapeDtypeStruct((S, D), F32), jax.ShapeDtypeStruct((S, D), BF16),
                   jax.ShapeDtypeStruct((1, D), F32), jax.ShapeDtypeStruct((1, D), F32)),
        compiler_params=_params(("arbitrary",)),
    )(dout, xhat, rstd, g)


def _loss_head(y, tgt, *, ts):
    S, D = y.shape

    def body(y_ref, t_ref, l_ref, dy_ref):
        @pl.when(pl.program_id(0) == 0)
        def _():
            l_ref[...] = jnp.zeros_like(l_ref)

        e = y_ref[...] - t_ref[...]
        dy_ref[...] = e * (1.0 / D)
        rowloss = jnp.sum(e * e, axis=1, keepdims=True) * (0.5 / D)
        l_ref[...] += jnp.sum(rowloss, axis=0, keepdims=True)

    rowblk = pl.BlockSpec((ts, D), lambda i: (i, 0))
    return pl.pallas_call(
        body, name="loss_head", grid=(S // ts,),
        in_specs=[rowblk, rowblk],
        out_specs=(pl.BlockSpec((1, 1), lambda i: (0, 0)), rowblk),
        out_shape=(jax.ShapeDtypeStruct((1, 1), F32), jax.ShapeDtypeStruct((S, D), F32)),
        compiler_params=_params(("arbitrary",)),
    )(y, tgt)


def _adamw(lands, srcs, w, m, v, *, name, first=0, into=None):
    R, C = w.shape
    npart = len(lands)
    rp = lands[0].shape[1]
    assert R % rp == 0 and all(p.shape == (N_DEV, rp, C) for p in lands)
    tiled = rp % 8 == 0
    assert tiled or (first == 0 and npart * rp == R and into is None)
    tr = min(ADAMW_BLOCK_ELEMS // (LANES * pl.cdiv(C, LANES)), rp) if tiled else rp
    assert rp % tr == 0
    steps = rp // tr
    c1 = 1.0 - ADAM_B1 ** ADAM_STEP
    c2 = 1.0 - ADAM_B2 ** ADAM_STEP
    slabbed = [p.ndim == 3 for p in srcs]
    n_into = 0 if into is None else 4

    def body(*refs):
        l_refs, s_refs = refs[:npart], refs[npart:2 * npart]
        w_ref, m_ref, v_ref = refs[2 * npart:2 * npart + 3]
        g_ref, d_ref, nm_ref, nv_ref = refs[2 * npart + 3 + n_into:]
        x, y, c = _my_place()
        me = 4 * x + 2 * y + c

        def update(l, rows):
            g = None
            for j in range(N_DEV):
                own = s_refs[l][j] if slabbed[l] else s_refs[l][...]
                p = jnp.where(me == j, own, l_refs[l][j]).astype(F32)
                g = p if g is None else g + p
            nm = ADAM_B1 * m_ref[rows, :] + (1.0 - ADAM_B1) * g
            nv = ADAM_B2 * v_ref[rows, :] + (1.0 - ADAM_B2) * (g * g)
            g_ref[rows, :] = g
            nm_ref[rows, :] = nm
            nv_ref[rows, :] = nv
            d_ref[rows, :] = -ADAM_LR * ((nm / c1) / (jnp.sqrt(nv / c2) + ADAM_EPS) + ADAM_WD * w_ref[rows, :])

        for l in range(npart):
            if tiled:
                pl.when(pl.program_id(0) == l)(lambda l=l: update(l, slice(None)))
            else:
                update(l, slice(l * rp, (l + 1) * rp))

    def part_spec(l, p):
        if p.ndim == 3:
            return pl.BlockSpec((N_DEV, tr, C), lambda q, i: (0, jnp.where(q == l, i, 0), 0))
        return pl.BlockSpec((tr, C), lambda q, i: (jnp.where(q == l, i, 0), 0))

    if tiled:
        blk = pl.BlockSpec((tr, C), lambda q, i: ((first + q) * steps + i, 0))
    else:
        blk = pl.BlockSpec((R, C), lambda q, i: (0, 0))
    out = jax.ShapeDtypeStruct((R, C), F32)
    n_in = 2 * npart + 3
    return pl.pallas_call(
        body, name=name, grid=(npart, steps) if tiled else (1, 1),
        in_specs=[part_spec(l, p) for l, p in enumerate(lands)] + [part_spec(l, p) for l, p in enumerate(srcs)]
        + [blk, blk, blk] + [ANY] * n_into,
        out_specs=(blk, blk, blk, blk),
        out_shape=(out, out, out, out),
        input_output_aliases={n_in + i: i for i in range(n_into)},
        compiler_params=_params(("arbitrary", "arbitrary")),
    )(*lands, *srcs, w, m, v, *(into or ()))


def _my_place():
    return lax.axis_index("x"), lax.axis_index("y"), lax.axis_index("c")


HBM = pl.BlockSpec(memory_space=pltpu.HBM)
SEM = pl.BlockSpec(memory_space=pltpu.SEMAPHORE)
DATAFLOW = pltpu.SideEffectType.DATAFLOW_SIDE_EFFECTING
EVERYONE = (1, 2, 3, 4, 5, 6, 7)
SIBLING = 1
SAME_CORE = (4, 2, 6)


def _xchg_copies(src, land, send_sem, recv_sem, gather, peers):
    x, y, c = _my_place()
    me = 4 * x + 2 * y + c
    copies = []
    for k, bits in enumerate(peers):
        px, py, pc = x ^ ((bits >> 2) & 1), y ^ ((bits >> 1) & 1), c ^ (bits & 1)
        peer = 4 * px + 2 * py + pc
        for a in range(len(src)):
            i = a * len(peers) + k
            out = src[a] if gather[a] else src[a].at[peer]
            sent, landed = [pltpu.make_async_remote_copy(
                src_ref=out, dst_ref=land[a].at[slab], send_sem=send_sem.at[i], recv_sem=recv_sem.at[i],
                device_id=(px, py, pc), device_id_type=MESH) for slab in (me, peer)]
            copies.append((sent, landed))
    return copies


def _xchg_start(srcs, *, gather, name, peers=EVERYONE, after=None):
    n = len(srcs)
    follows = [] if after is None else [after]
    lands = [lax.empty(((N_DEV,) + v.shape) if g else v.shape, v.dtype) for v, g in zip(srcs, gather)]

    def body(*refs):
        src, land = refs[:n], refs[n:2 * n]
        send_sem, recv_sem = refs[2 * n + len(follows)], refs[2 * n + len(follows) + 1]
        token = refs[-1]
        for sent, _ in _xchg_copies(src, land, send_sem, recv_sem, gather, peers):
            sent.start()
        token[...] = jnp.zeros_like(token)

    sems = pltpu.SemaphoreType.DMA((n * len(peers),))
    outs = pl.pallas_call(
        body, name=name,
        out_shape=(sems, sems, *[pltpu.HBM(v.shape, v.dtype) for v in srcs + lands],
                   jax.ShapeDtypeStruct((8, LANES), F32)),
        in_specs=[HBM] * (2 * n) + [ANY] * len(follows),
        out_specs=(SEM, SEM, *[HBM] * (2 * n), pl.BlockSpec(memory_space=pltpu.VMEM)),
        input_output_aliases={i: 2 + i for i in range(2 * n)},
        compiler_params=pltpu.CompilerParams(has_side_effects=DATAFLOW),
    )(*[pltpu.with_memory_space_constraint(v, pltpu.HBM) for v in srcs + lands], *follows)
    return dict(send=outs[0], recv=outs[1], srcs=list(outs[2:2 + n]), lands=list(outs[2 + n:2 + 2 * n]),
                token=outs[-1], gather=gather, peers=peers)


def _xchg_wait(h, after, *, name):
    n = len(h["srcs"])
    gather, peers = h["gather"], h["peers"]

    def body(*refs):
        src, land = refs[:n], refs[n:2 * n]
        send_sem, recv_sem = refs[2 * n], refs[2 * n + 1]
        for sent, landed in _xchg_copies(src, land, send_sem, recv_sem, gather, peers):
            sent.wait_send()
            landed.wait_recv()

    thru = h["srcs"] + h["lands"]
    outs = pl.pallas_call(
        body, name=name,
        out_shape=[pltpu.HBM(v.shape, v.dtype) for v in thru],
        in_specs=[HBM] * (2 * n) + [SEM, SEM, ANY],
        out_specs=[HBM] * (2 * n),
        input_output_aliases={i: i for i in range(2 * n)},
        compiler_params=pltpu.CompilerParams(has_side_effects=DATAFLOW),
    )(*thru, h["send"], h["recv"], after)
    return list(outs[:n]), list(outs[n:])


def _relay_copies(land, send_sem, recv_sem):
    x, y, c = _my_place()
    me = 4 * x + 2 * y + c
    copies = []
    for k, bits in enumerate(SAME_CORE):
        for a in range(len(land)):
            i = a * len(SAME_CORE) + k
            sent, landed = [pltpu.make_async_remote_copy(
                src_ref=land[a].at[me ^ bits], dst_ref=land[a].at[slab], send_sem=send_sem.at[i], recv_sem=recv_sem.at[i],
                device_id=(x, y, 1 - c), device_id_type=MESH) for slab in (me ^ bits, me ^ bits ^ SIBLING)]
            copies.append((sent, landed))
    return copies


def _relay_start(lands, *, name):
    n = len(lands)

    def body(*refs):
        land = refs[:n]
        send_sem, recv_sem = refs[n], refs[n + 1]
        token = refs[-1]
        for sent, _ in _relay_copies(land, send_sem, recv_sem):
            sent.start()
        token[...] = jnp.zeros_like(token)

    sems = pltpu.SemaphoreType.DMA((n * len(SAME_CORE),))
    outs = pl.pallas_call(
        body, name=name,
        out_shape=(sems, sems, *[pltpu.HBM(v.shape, v.dtype) for v in lands], jax.ShapeDtypeStruct((8, LANES), F32)),
        in_specs=[HBM] * n,
        out_specs=(SEM, SEM, *[HBM] * n, pl.BlockSpec(memory_space=pltpu.VMEM)),
        input_output_aliases={i: 2 + i for i in range(n)},
        compiler_params=pltpu.CompilerParams(has_side_effects=DATAFLOW),
    )(*[pltpu.with_memory_space_constraint(v, pltpu.HBM) for v in lands])
    return dict(send=outs[0], recv=outs[1], lands=list(outs[2:2 + n]), token=outs[-1])


def _relay_wait(h, after, *, name):
    n = len(h["lands"])

    def body(*refs):
        land = refs[:n]
        send_sem, recv_sem = refs[n], refs[n + 1]
        for sent, landed in _relay_copies(land, send_sem, recv_sem):
            sent.wait_send()
            landed.wait_recv()

    outs = pl.pallas_call(
        body, name=name,
        out_shape=[pltpu.HBM(v.shape, v.dtype) for v in h["lands"]],
        in_specs=[HBM] * n + [SEM, SEM, ANY],
        out_specs=[HBM] * n,
        input_output_aliases={i: i for i in range(n)},
        compiler_params=pltpu.CompilerParams(has_side_effects=DATAFLOW),
    )(*h["lands"], h["send"], h["recv"], after)
    return list(outs)


def _with_own(land, own, me):
    slot = lax.broadcasted_iota(jnp.int32, land.shape, 0)
    return jnp.where(slot == me, own if own.ndim == land.ndim else own[None], land)


def _w_in_segments(j, cols):
    lo, hi = j * cols, (j + 1) * cols
    segs = []
    for a, b, where, shift in ((0, FG_AT, "main", 0), (FG_AT, FG_AT + N_HEADS, "fg", -FG_AT),
                               (FG_AT + N_HEADS, IN_W, "main", -N_HEADS)):
        s0, s1 = max(lo, a), min(hi, b)
        if s0 < s1:
            segs.append((s0 - lo, s1 - s0, where, s0 + shift))
    return segs


SLAB_W = 1024


def _window(off, width):
    base = min((off // LANES) * LANES, width - SLAB_W)
    return base, off - base


def _w_unpack(land, own, *, tr):
    _, D, cols = land.shape
    assert cols + LANES - 1 <= SLAB_W

    def body(land_ref, own_ref, main_ref, slab):
        x, y, c = _my_place()
        me = 4 * x + 2 * y + c
        lane = lax.broadcasted_iota(jnp.int32, (tr, SLAB_W), 1)
        main_ref[...] = jnp.zeros_like(main_ref)
        slab[...] = jnp.zeros_like(slab)
        for j in range(N_DEV):
            slab[:, :cols] = jnp.where(me == j, own_ref[...], land_ref[j])
            v = slab[...]
            segs = _w_in_segments(j, cols)
            for lo, n, where, dst in segs:
                part = v if len(segs) == 1 else jnp.where((lane >= lo) & (lane < lo + n), v, jnp.zeros_like(v))
                if where == "fg":
                    main_ref[:, MAIN_W:] = pltpu.roll(part, (SLAB_W - lo) % SLAB_W, 1)[:, :LANES]
                else:
                    base, r = _window(dst - lo, MAIN_W)
                    assert 0 <= r and lo + n + r <= SLAB_W
                    main_ref[:, base:base + SLAB_W] += pltpu.roll(part, r, 1)

    return pl.pallas_call(
        body, name="w_unpack", grid=(D // tr,),
        in_specs=[pl.BlockSpec((N_DEV, tr, cols), lambda i: (0, i, 0)), pl.BlockSpec((tr, cols), lambda i: (i, 0))],
        out_specs=pl.BlockSpec((tr, ALL_W), lambda i: (i, 0)),
        out_shape=jax.ShapeDtypeStruct((D, ALL_W), BF16),
        scratch_shapes=[pltpu.VMEM((tr, SLAB_W), BF16)],
        compiler_params=_params(("parallel",)),
    )(land, own)


def _g_pack(dw_main, dw_fg, cols, *, tr):
    D = dw_main.shape[0]

    def body(main_ref, fg_ref, out_ref, slab):
        lane = lax.broadcasted_iota(jnp.int32, (tr, SLAB_W), 1)
        slab[...] = jnp.zeros_like(slab)
        slab[:, :LANES] = fg_ref[...]
        fg = slab[...]
        for j in range(N_DEV):
            v = None
            for lo, n, where, src in _w_in_segments(j, cols):
                if where == "fg":
                    part = pltpu.roll(fg, lo, 1)
                else:
                    base, r = _window(src - lo, MAIN_W)
                    assert 0 <= r and lo + n + r <= SLAB_W
                    part = pltpu.roll(main_ref[:, base:base + SLAB_W], (SLAB_W - r) % SLAB_W, 1)
                v = part if v is None else jnp.where((lane >= lo) & (lane < lo + n), part, v)
            out_ref[j] = v[:, :cols]

    return pl.pallas_call(
        body, name="g_pack", grid=(D // tr,),
        in_specs=[pl.BlockSpec((tr, MAIN_W), lambda i: (i, 0)), pl.BlockSpec((tr, LANES), lambda i: (i, 0))],
        out_specs=pl.BlockSpec((N_DEV, tr, cols), lambda i: (0, i, 0)),
        out_shape=jax.ShapeDtypeStruct((N_DEV, D, cols), BF16),
        scratch_shapes=[pltpu.VMEM((tr, SLAB_W), BF16)],
        compiler_params=_params(("parallel",)),
    )(dw_main, dw_fg)


TQ = 1024
TS = 256
W_TILE_ROWS = 128


def _layer_fwd_proj(xb, w_in):
    qkv = _mm(xb, w_in, out_dtype=BF16, tm=2048, tn=512, tk=D_MODEL, name="mm_qkv", n_cols=QKV_W)
    rest = _mm(xb, w_in, out_dtype=F32, tm=2048, tn=512, tk=D_MODEL, name="mm_rest", n_off=QKV_W, n_cols=REST_W)
    fgp = _mm(xb, w_in, out_dtype=F32, tm=1024, tn=LANES, tk=D_MODEL, name="mm_fg", n_off=MAIN_W, n_cols=LANES)
    return qkv, rest, fgp


def _layer_fwd_mix(x, xbt, proj, wl):
    qkv, rest, fgp = proj
    cum = _fg_fwd(fgp, wl["b_f"])
    S = x.shape[0]
    cumr = (cum[:, :N_HEADS].T * LOG2E).reshape(N_HEADS, S // TQ, 1, TQ)
    o, y, lse = _attn_fwd(qkv, rest, cum, cumr, tq=TQ)
    y = _conv_fwd(rest, wl["conv_w"], y)
    y = _pool_fwd(rest, wl["pool_w"], wl["pool_scale"], y)
    z = _mm(y, wl["w_out"], out_dtype=F32, tm=2048, tn=512, tk=D_MODEL, name="mm_out", c=x, c_scale=ALPHA)
    out, outb, outbt, xhat, rstd = _ln_fwd(z, wl["ln_g"], wl["ln_b"], ts=TS)
    saved = dict(xbt=xbt, qkv=qkv, rest=rest, fgp=fgp, cum=cum, cumr=cumr, o=o, lse=lse, y=y, xhat=xhat, rstd=rstd)
    return out, outb, outbt, saved


def _layer_bwd_mix(dout, sv, wl):
    S = dout.shape[0]
    dz, dzb, dln_g, dln_b = _ln_bwd(dout, sv["xhat"], sv["rstd"], wl["ln_g"], ts=TS)
    dy = _mm(dzb, wl["w_out"], out_dtype=F32, tm=2048, tn=512, tk=D_MODEL, name="mm_dy", bt=True)
    dw_out = _mm(sv["y"], dzb, out_dtype=BF16, tm=512, tn=D_MODEL, tk=2048, name="mm_dw_out", at=True)
    rest = sv["rest"]
    do, dproj, delta = _attn_bwd_pre(rest, sv["o"], dy, tq=TQ)
    a2 = sv["cumr"] - sv["lse"] * LOG2E
    dproj, ds_q, ds_k = _attn_bwd(sv["qkv"], do, a2, delta, sv["cum"], dproj, tq=TQ)
    dcum = jnp.pad(ds_q.reshape(N_HEADS, S).T, ((0, 0), (0, LANES - N_HEADS))) - ds_k
    dproj, db_f = _fg_bwd(dcum, sv["fgp"], wl["b_f"], dproj)
    dproj, dconv_w = _conv_bwd(rest, dy, wl["conv_w"], dproj)
    dproj, dpool_w, dpool_scale = _pool_bwd(rest, dy, wl["pool_w"], wl["pool_scale"], dproj)
    grads = dict(b_f=db_f[:, :N_HEADS], conv_w=dconv_w, pool_w=dpool_w, pool_scale=dpool_scale, w_out=dw_out,
                 ln_g=dln_g, ln_b=dln_b)
    return grads, (dproj, dz)


def _layer_bwd_w_in(mid, sv):
    dproj, dz = mid
    xt = sv["xbt"]
    S = xt.shape[1]
    dw_main = _mm(xt, dproj, out_dtype=BF16, tm=1024, tn=512, tk=S, name="mm_dw_main", n_cols=MAIN_W)
    dw_fg = _mm(xt, dproj, out_dtype=BF16, tm=1024, tn=LANES, tk=2048, name="mm_dw_fg", n_off=MAIN_W, n_cols=LANES)
    return (dw_main, dw_fg), (dproj, dz)


def _layer_bwd_input(ctx, wl, after):
    dproj, dz = ctx
    return _mm(dproj, wl["w_in"], out_dtype=F32, tm=1024, tn=256, tk=ALL_W, name="mm_dx", c=dz, c_scale=ALPHA, bt=True,
               after=after)


def kernel(x, w_in, b_f, conv_w, pool_w, pool_scale, w_out, ln_g, ln_b, loss_target, m_w_in, m_b_f, m_conv_w, m_pool_w, m_pool_scale, m_w_out, m_ln_g, m_ln_b, v_w_in, v_b_f, v_conv_w, v_pool_w, v_pool_scale, v_w_out, v_ln_g, v_ln_b):
    L, D, cols = w_in.shape
    rows_out = w_out.shape[1]
    ccols = conv_w.shape[2]
    mx, my, mc = _my_place()
    me = 4 * mx + 2 * my + mc

    w_in_b, w_out_b = w_in.astype(BF16), w_out.astype(BF16)
    gathers = []
    for l in range(L):
        gathers.append(_xchg_start([w_in_b[l], w_out_b[l], conv_w[l]], gather=[True] * 3, name=f"gather_start_{l}",
                                   peers=(SIBLING,) + SAME_CORE if l == 0 else EVERYONE,
                                   after=gathers[-1]["token"] if gathers else None))
    started = gathers[-1]["token"]

    xl = x[0]
    xb, xbt = _prep(xl, started, ts=TS)
    weights, saved = [], []
    for l in range(L):
        (own_in, own_out, own_cw), (g_in, g_out, g_cw) = _xchg_wait(gathers[l], xb, name=f"gather_wait_{l}")
        if l == 0:
            g_in, g_out, g_cw = _relay_wait(_relay_start([g_in, g_out, g_cw], name="gather_relay_start"), xb,
                                            name="gather_relay_wait")
        w_all = _w_unpack(g_in, own_in, tr=W_TILE_ROWS)
        proj = _layer_fwd_proj(xb, w_all)
        w_o = _with_own(g_out, own_out, me).reshape(N_DEV * rows_out, D)
        cw = jnp.transpose(_with_own(g_cw, own_cw, me), (1, 0, 2)).reshape(CONV_TAPS, N_DEV * ccols)
        wl = dict(w_in=w_all, w_out=w_o, conv_w=cw,
                  b_f=jnp.pad(b_f[l][None, :], ((0, 0), (0, LANES - N_HEADS))), pool_w=pool_w[l],
                  pool_scale=pool_scale[l][None, :], ln_g=ln_g[l][None, :], ln_b=ln_b[l][None, :])
        xl, xb, xbt, sv = _layer_fwd_mix(xl, xbt, proj, wl)
        weights.append(wl)
        saved.append(sv)

    loss, dx = _loss_head(xl, loss_target[0], ts=TS)
    loss = lax.psum(loss[0, 0], ("x", "y", "c"))

    names = ["w_in", "w_out", "conv_w", "b_f", "pool_w", "pool_scale", "ln_g", "ln_b"]
    sharded = names[:3]
    scatters = [None] * L
    for l in reversed(range(L)):
        g, mid = _layer_bwd_mix(dx, saved[l], weights[l])
        (dw_main, dw_fg), ctx = _layer_bwd_w_in(mid, saved[l])
        g["w_in"] = _g_pack(dw_main, dw_fg, cols, tr=W_TILE_ROWS)
        g["w_out"] = g["w_out"].reshape(N_DEV, rows_out, D)
        g["conv_w"] = jnp.transpose(g["conv_w"].reshape(CONV_TAPS, N_DEV, ccols), (1, 0, 2))
        scatters[l] = _xchg_start([g[k] for k in names], gather=[k not in sharded for k in names],
                                  name=f"scatter_start_{l}")
        dx = _layer_bwd_input(ctx, weights[l], scatters[l]["token"])

    given = dict(w_in=(w_in, m_w_in, v_w_in), b_f=(b_f, m_b_f, v_b_f), conv_w=(conv_w, m_conv_w, v_conv_w),
                 pool_w=(pool_w, m_pool_w, v_pool_w), pool_scale=(pool_scale, m_pool_scale, v_pool_scale),
                 w_out=(w_out, m_w_out, v_w_out), ln_g=(ln_g, m_ln_g, v_ln_g), ln_b=(ln_b, m_ln_b, v_ln_b))
    sent, landed = [{} for _ in range(L)], [{} for _ in range(L)]

    def wait(l, after):
        s_, l_ = _xchg_wait(scatters[l], after, name=f"scatter_wait_{l}")
        sent[l].update(zip(names, s_))
        landed[l].update(zip(names, l_))

    def update(k, layers, into=None):
        w, m, v = given[k]
        C = w.shape[-1]
        w2, m2, v2 = (t.reshape(-1, C) for t in (w, m, v))
        rp = w2.shape[0] // L
        lands = [landed[l][k].reshape(N_DEV, rp, C) for l in layers]
        srcs = [sent[l][k].reshape((N_DEV, rp, C) if k in sharded else (rp, C)) for l in layers]
        return _adamw(lands, srcs, w2, m2, v2, name=f"adamw_{k}_{layers[0]}", first=layers[0], into=into)

    big = ["w_in", "w_out", "pool_w"]
    for l in reversed(range(1, L)):
        wait(l, dx)
    upper = {k: update(k, list(range(1, L))) for k in big} if L > 1 else {}
    wait(0, upper["w_in"][0] if upper else dx)
    res = {k: update(k, [0], into=upper[k]) if k in upper else update(k, list(range(L))) for k in names}
    res = {k: [t.reshape(given[k][0].shape) for t in outs] for k, outs in res.items()}

    order = ["w_in", "b_f", "conv_w", "pool_w", "pool_scale", "w_out", "ln_g", "ln_b"]
    return (loss, dx[None], *[res[k][0] for k in order], *[res[k][1] for k in order],
            *[res[k][2] for k in order], *[res[k][3] for k in order])
```

```python
import jax
import jax.numpy as jnp
from jax import lax
from jax.experimental import pallas as pl
from jax.experimental.pallas import tpu as pltpu

F32 = jnp.float32
BF16 = jnp.bfloat16

N_DEV = 8
D_MODEL = 2048
N_HEADS = 8
HEAD_DIM = 128
ATT_W = N_HEADS * HEAD_DIM
CONV_W = 512
CONV_TAPS = 3
POOL_W = 512
POOL_GROUP = 128
N_POOL = POOL_W // POOL_GROUP
FG_AT = 4 * ATT_W
IN_W = FG_AT + N_HEADS + 4 * CONV_W + 2 * POOL_W
MAIN_W = IN_W - N_HEADS
LANES = 128
ALL_W = MAIN_W + LANES
LN_EPS = 1e-5
DEPTH = 4
ALPHA = (2 * DEPTH) ** 0.25
SCALE = HEAD_DIM ** -0.5
LOG2E = 1.4426950408889634
LN2 = 0.6931471805599453
NEG = -1e30

ADAM_LR, ADAM_B1, ADAM_B2, ADAM_EPS, ADAM_WD, ADAM_STEP = 0.001, 0.9, 0.999, 1e-08, 0.01, 10

QKV_W = 3 * ATT_W
REST_W = MAIN_W - QKV_W
QB, KB, VB = 0, 8, 16
GB = 0
CBB, CCB, CHB, CGB = 8, 12, 16, 20
PUB, PGB = 24, 28
FGB = MAIN_W // LANES
VMEM_LIMIT = 48 * 1024 * 1024
ADAMW_BLOCK_ELEMS = 64 * 1024

MESH = pl.DeviceIdType.MESH
ANY = pl.BlockSpec(memory_space=pl.ANY)


def _params(semantics):
    return pltpu.CompilerParams(dimension_semantics=semantics, vmem_limit_bytes=VMEM_LIMIT)


def _sigmoid(x):
    return 1.0 / (1.0 + jnp.exp(-x))


def _mm(a, b, *, out_dtype, tm, tn, tk, name, c=None, c_scale=1.0, n_off=0, n_cols=None, bt=False, at=False,
        after=None):
    M, K = a.shape[::-1] if at else a.shape
    N = b.shape[0 if bt else 1] if n_cols is None else n_cols
    assert not (bt and n_off)
    dims = (((0 if at else 1,), (1 if bt else 0,)), ((), ()))
    tm, tn, tk = min(tm, M), min(tn, N), min(tk, K)
    assert M % tm == 0 and N % tn == 0 and K % tk == 0 and n_off % tn == 0, (a.shape, b.shape, tm, tn, tk)
    nk, joff = K // tk, n_off // tn
    has_c, has_after = c is not None, after is not None

    def body(*refs):
        a_ref, b_ref = refs[0], refs[1]
        c_ref = refs[2] if has_c else None
        o_ref = refs[2 + has_c + has_after]

        def finish(r):
            if has_c:
                r = r + c_scale * c_ref[...]
            o_ref[...] = r.astype(out_dtype)

        if nk == 1:
            finish(lax.dot_general(a_ref[...], b_ref[...], dims, preferred_element_type=F32))
        else:
            acc_ref = refs[3 + has_c + has_after]
            k = pl.program_id(2)

            @pl.when(k == 0)
            def _():
                acc_ref[...] = jnp.zeros_like(acc_ref)

            acc_ref[...] += lax.dot_general(a_ref[...], b_ref[...], dims, preferred_element_type=F32)

            @pl.when(k == nk - 1)
            def _():
                finish(acc_ref[...])

    in_specs = [pl.BlockSpec((tk, tm), lambda i, j, k: (k, i)) if at else pl.BlockSpec((tm, tk), lambda i, j, k: (i, k)),
                pl.BlockSpec((tn, tk), lambda i, j, k: (j, k)) if bt
                else pl.BlockSpec((tk, tn), lambda i, j, k: (k, j + joff))]
    args = [a, b]
    if has_c:
        in_specs.append(pl.BlockSpec((tm, tn), lambda i, j, k: (i, j)))
        args.append(c)
    if has_after:
        in_specs.append(ANY)
        args.append(after)
    return pl.pallas_call(
        body, name=name, grid=(M // tm, N // tn, nk),
        in_specs=in_specs,
        out_specs=pl.BlockSpec((tm, tn), lambda i, j, k: (i, j)),
        out_shape=jax.ShapeDtypeStruct((M, N), out_dtype),
        scratch_shapes=[pltpu.VMEM((tm, tn), F32)] if nk > 1 else [],
        compiler_params=_params(("parallel", "parallel", "arbitrary")),
    )(*args)


def _shift_down(x, k, row):
    return jnp.where(row >= k, pltpu.roll(x, k, 0), 0.0)


def _shift_up(x, k, row):
    s = x.shape[0]
    return jnp.where(row < s - k, pltpu.roll(x, s - k, 0), 0.0)


def _fg_fwd(fgp, bf):
    S = fgp.shape[0]

    def body(f_ref, b_ref, cum_ref):
        a = f_ref[...] + b_ref[...]
        x = jnp.minimum(a, 0.0) - jnp.log(1.0 + jnp.exp(-jnp.abs(a)))
        row = lax.broadcasted_iota(jnp.int32, x.shape, 0)
        k = 1
        while k < S:
            x = x + _shift_down(x, k, row)
            k *= 2
        cum_ref[...] = x

    return pl.pallas_call(body, name="fg_fwd", out_shape=jax.ShapeDtypeStruct((S, LANES), F32),
                          compiler_params=_params(None))(fgp, bf)


def _fg_bwd(dcum, fgp, bf, dproj):
    S = fgp.shape[0]

    def body(d_ref, f_ref, b_ref, dproj_in, dfg_ref, dbf_ref):
        x = d_ref[...]
        row = lax.broadcasted_iota(jnp.int32, x.shape, 0)
        k = 1
        while k < S:
            x = x + _shift_up(x, k, row)
            k *= 2
        dfg = x * _sigmoid(-(f_ref[...] + b_ref[...]))
        dfg_ref[...] = dfg.astype(BF16)
        dbf_ref[...] = jnp.sum(dfg, axis=0, keepdims=True)

    whole = pl.BlockSpec((S, LANES), lambda i: (0, 0))
    vec = pl.BlockSpec((1, LANES), lambda i: (0, 0))
    return pl.pallas_call(
        body, name="fg_bwd", grid=(1,),
        in_specs=[whole, whole, vec, ANY],
        out_specs=(pl.BlockSpec((S, LANES), lambda i: (0, FGB)), vec),
        out_shape=(jax.ShapeDtypeStruct(dproj.shape, dproj.dtype), jax.ShapeDtypeStruct((1, LANES), F32)),
        input_output_aliases={3: 0},
        compiler_params=_params(("arbitrary",)),
    )(dcum, fgp, bf, dproj)


def _put_columns(pieces, dst, blocks, sems):
    copies = [pltpu.make_async_copy(p, dst.at[:, pl.ds(pl.multiple_of(b * LANES, LANES), LANES)], sems.at[i])
              for i, (p, b) in enumerate(zip(pieces, blocks))]
    for cp in copies:
        cp.start()
    for cp in copies:
        cp.wait()


def _colblock(S, off):
    return pl.BlockSpec((S, LANES), lambda h: (0, h + off))


def _head_row(nq, tq):
    return pl.BlockSpec((None, nq, 1, tq), lambda h: (h, 0, 0, 0))


def _lane_of(ref, h):
    lane = lax.broadcasted_iota(jnp.int32, ref.shape, 1)
    return jnp.sum(jnp.where(lane == h, ref[...], 0.0), axis=1, keepdims=True)


def _attn_fwd(qkv, rest, cum, cumr, *, tq):
    S = qkv.shape[0]
    nq = S // tq

    def body(q_ref, k_ref, v_ref, g_ref, cum_ref, cumr_ref, o_ref, y_ref, lse_ref, cc):
        cc[...] = _lane_of(cum_ref, pl.program_id(0)) * LOG2E
        tri = (lax.broadcasted_iota(jnp.int32, (tq, tq), 0) <= lax.broadcasted_iota(jnp.int32, (tq, tq), 1))

        def q_step(qi, _):
            rows = pl.ds(pl.multiple_of(qi * tq, tq), tq)
            q = q_ref[rows, :]
            ci = cumr_ref[qi]

            def tile(kj, carry, masked):
                m, l, acc = carry
                cols = pl.ds(pl.multiple_of(kj * tq, tq), tq)
                s = lax.dot_general(k_ref[cols, :], q, (((1,), (1,)), ((), ())), preferred_element_type=F32)
                s = s * (SCALE * LOG2E) + ci - cc[cols, :]
                if masked:
                    s = jnp.where(tri, s, NEG)
                m_new = jnp.maximum(m, jnp.max(s, axis=0, keepdims=True))
                a = jnp.exp2(m - m_new)
                p = jnp.exp2(s - m_new)
                l = a * l + jnp.sum(p, axis=0, keepdims=True)
                acc = a * acc + lax.dot_general(v_ref[cols, :], p.astype(BF16), (((0,), (0,)), ((), ())),
                                                preferred_element_type=F32)
                return m_new, l, acc

            init = (jnp.full((1, tq), NEG, F32), jnp.zeros((1, tq), F32), jnp.zeros((HEAD_DIM, tq), F32))
            carry = lax.fori_loop(0, qi, lambda kj, c: tile(kj, c, False), init)
            m, l, acc = tile(qi, carry, True)
            o = (acc / l).T
            g = g_ref[rows, :]
            o_ref[rows, :] = o
            y_ref[rows, :] = (o * (g * _sigmoid(g))).astype(BF16)
            lse_ref[qi] = m * LN2 + jnp.log(l)
            return 0

        lax.fori_loop(0, nq, q_step, 0)

    return pl.pallas_call(
        body, name="attn_fwd", grid=(N_HEADS,),
        in_specs=[_colblock(S, QB), _colblock(S, KB), _colblock(S, VB), _colblock(S, GB),
                  pl.BlockSpec((S, LANES), lambda h: (0, 0)), _head_row(nq, tq)],
        out_specs=(_colblock(S, 0), _colblock(S, 0), _head_row(nq, tq)),
        out_shape=(jax.ShapeDtypeStruct((S, ATT_W), F32), jax.ShapeDtypeStruct((S, D_MODEL), BF16),
                   jax.ShapeDtypeStruct((N_HEADS, nq, 1, tq), F32)),
        scratch_shapes=[pltpu.VMEM((S, 1), F32)],
        compiler_params=_params(("arbitrary",)),
    )(qkv, qkv, qkv, rest, cum, cumr)


def _attn_bwd_pre(rest, o, dy, *, tq):
    S = o.shape[0]
    nq = S // tq

    def body(g_ref, o_ref, dy_ref, do_ref, dg_ref, dl_ref):
        g = g_ref[...]
        sg = _sigmoid(g)
        ov = o_ref[...]
        dy = dy_ref[...]
        do = dy * (g * sg)
        do_ref[...] = do.astype(BF16)
        dg_ref[...] = (dy * ov * (sg * (1.0 + g * (1.0 - sg)))).astype(BF16)
        rows = lax.dot_general(jnp.ones((8, HEAD_DIM), F32), do * ov, (((1,), (1,)), ((), ())),
                               precision=lax.Precision.HIGHEST, preferred_element_type=F32)
        for qi in range(nq):
            dl_ref[qi] = rows[0:1, qi * tq:(qi + 1) * tq]

    att = jax.ShapeDtypeStruct((S, ATT_W), BF16)
    return pl.pallas_call(
        body, name="attn_bwd_pre", grid=(N_HEADS,),
        in_specs=[_colblock(S, GB), _colblock(S, 0), _colblock(S, 0)],
        out_specs=(_colblock(S, 0), _colblock(S, QKV_W // LANES), _head_row(nq, tq)),
        out_shape=(att, jax.ShapeDtypeStruct((S, ALL_W), BF16), jax.ShapeDtypeStruct((N_HEADS, nq, 1, tq), F32)),
        compiler_params=_params(("parallel",)),
    )(rest, o, dy)


def _attn_bwd(qkv, do, a2, delta, cum, dproj, *, tq):
    S = qkv.shape[0]
    nq = S // tq
    tdot = (((0,), (0,)), ((), ()))
    ndot = (((1,), (1,)), ((), ()))

    def body(q_ref, k_ref, v_ref, do_ref, a_ref, dl_ref, cum_ref, dproj_in, dproj_ref, dsq_ref, dsk_ref,
             dqa, dq_ref, dk_ref, dv_ref, cc, sems):
        h = pl.program_id(0)
        dqa[...] = jnp.zeros_like(dqa)
        dsq_ref[...] = jnp.zeros_like(dsq_ref)
        cc[...] = _lane_of(cum_ref, h) * LOG2E

        @pl.when(h == 0)
        def _():
            dsk_ref[...] = jnp.zeros_like(dsk_ref)

        tri = (lax.broadcasted_iota(jnp.int32, (tq, tq), 0) <= lax.broadcasted_iota(jnp.int32, (tq, tq), 1))
        mine = lax.broadcasted_iota(jnp.int32, (tq, LANES), 1) == h

        def k_step(kj, _):
            cols = pl.ds(pl.multiple_of(kj * tq, tq), tq)
            k = k_ref[cols, :]
            v = v_ref[cols, :]
            cj = cc[cols, :]

            def tile(qi, carry, masked):
                dk, dv, ksum = carry
                rows = pl.ds(pl.multiple_of(qi * tq, tq), tq)
                q = q_ref[rows, :]
                dot = do_ref[rows, :]
                s = lax.dot_general(k, q, ndot, preferred_element_type=F32)
                p = jnp.exp2(s * (SCALE * LOG2E) + a_ref[qi] - cj)
                if masked:
                    p = jnp.where(tri, p, 0.0)
                dv = dv + jnp.dot(p.astype(BF16), dot, preferred_element_type=F32)
                dp = lax.dot_general(v, dot, ndot, preferred_element_type=F32)
                ds = p * (dp - dl_ref[qi])
                dsq_ref[qi] += jnp.sum(ds, axis=0, keepdims=True)
                ksum = ksum + jnp.sum(ds, axis=1, keepdims=True)
                dsb = (ds * SCALE).astype(BF16)
                dk = dk + jnp.dot(dsb, q, preferred_element_type=F32)
                dqa[rows, :] += lax.dot_general(dsb, k, tdot, preferred_element_type=F32)
                return dk, dv, ksum

            init = (jnp.zeros((tq, HEAD_DIM), F32), jnp.zeros((tq, HEAD_DIM), F32), jnp.zeros((tq, 1), F32))
            carry = tile(kj, init, True)
            dk, dv, ksum = lax.fori_loop(kj + 1, nq, lambda qi, c: tile(qi, c, False), carry)
            dk_ref[cols, :] = dk.astype(BF16)
            dv_ref[cols, :] = dv.astype(BF16)
            dsk_ref[cols, :] += jnp.where(mine, ksum, 0.0)
            return 0

        lax.fori_loop(0, nq, k_step, 0)
        dq_ref[...] = dqa[...].astype(BF16)
        _put_columns([dq_ref, dk_ref, dv_ref], dproj_ref, [QB + h, KB + h, VB + h], sems)

    head = pltpu.VMEM((S, HEAD_DIM), BF16)
    return pl.pallas_call(
        body, name="attn_bwd", grid=(N_HEADS,),
        in_specs=[_colblock(S, QB), _colblock(S, KB), _colblock(S, VB), _colblock(S, 0),
                  _head_row(nq, tq), _head_row(nq, tq), pl.BlockSpec((S, LANES), lambda h: (0, 0)), ANY],
        out_specs=(ANY, _head_row(nq, tq), pl.BlockSpec((S, LANES), lambda h: (0, 0))),
        out_shape=(jax.ShapeDtypeStruct(dproj.shape, dproj.dtype), jax.ShapeDtypeStruct((N_HEADS, nq, 1, tq), F32),
                   jax.ShapeDtypeStruct((S, LANES), F32)),
        input_output_aliases={7: 0},
        scratch_shapes=[pltpu.VMEM((S, HEAD_DIM), F32), head, head, head, pltpu.VMEM((S, 1), F32),
                        pltpu.SemaphoreType.DMA((3,))],
        compiler_params=_params(("arbitrary",)),
    )(qkv, qkv, qkv, do, a2, delta, cum, dproj)


def _conv_taps(u, w_ref, row):
    return (w_ref[0:1, :] * _shift_down(u, 2, row) + w_ref[1:2, :] * _shift_down(u, 1, row)) + w_ref[2:3, :] * u


def _conv_fwd(rest, cw, y):
    S = rest.shape[0]

    def body(cb_ref, cc_ref, ch_ref, g_ref, w_ref, y_in, y_ref):
        row = lax.broadcasted_iota(jnp.int32, (S, LANES), 0)
        g = g_ref[...]
        y = _conv_taps(cc_ref[...] * ch_ref[...], w_ref, row)
        y_ref[...] = ((cb_ref[...] * y) * (g * _sigmoid(g))).astype(BF16)

    return pl.pallas_call(
        body, name="conv_fwd", grid=(CONV_W // LANES,),
        in_specs=[_colblock(S, CBB), _colblock(S, CCB), _colblock(S, CHB), _colblock(S, CGB),
                  pl.BlockSpec((CONV_TAPS, LANES), lambda j: (0, j)), ANY],
        out_specs=_colblock(S, ATT_W // LANES),
        out_shape=jax.ShapeDtypeStruct(y.shape, y.dtype),
        input_output_aliases={5: 0},
        compiler_params=_params(("parallel",)),
    )(rest, rest, rest, rest, cw, y)


def _conv_bwd(rest, dy, cw, dproj):
    S = rest.shape[0]

    def body(cb_ref, cc_ref, ch_ref, g_ref, dy_ref, w_ref, dproj_in, dproj_ref, dw_ref,
             dcb_ref, dcc_ref, dch_ref, dg_ref, sems):
        row = lax.broadcasted_iota(jnp.int32, (S, LANES), 0)
        g = g_ref[...]
        sg = _sigmoid(g)
        silu = g * sg
        cb, cc, ch, dy = cb_ref[...], cc_ref[...], ch_ref[...], dy_ref[...]
        u = cc * ch
        u1 = _shift_down(u, 1, row)
        u2 = _shift_down(u, 2, row)
        y = (w_ref[0:1, :] * u2 + w_ref[1:2, :] * u1) + w_ref[2:3, :] * u
        dcb_ref[...] = (dy * silu * y).astype(BF16)
        dg_ref[...] = (dy * (cb * y) * (sg * (1.0 + g * (1.0 - sg)))).astype(BF16)
        dyv = dy * silu * cb
        du = w_ref[2:3, :] * dyv + w_ref[1:2, :] * _shift_up(dyv, 1, row) + w_ref[0:1, :] * _shift_up(dyv, 2, row)
        dcc_ref[...] = (du * ch).astype(BF16)
        dch_ref[...] = (du * cc).astype(BF16)
        dw_ref[0:1, :] = jnp.sum(dyv * u2, axis=0, keepdims=True)
        dw_ref[1:2, :] = jnp.sum(dyv * u1, axis=0, keepdims=True)
        dw_ref[2:3, :] = jnp.sum(dyv * u, axis=0, keepdims=True)
        j = pl.program_id(0)
        first = QKV_W // LANES
        _put_columns([dcb_ref, dcc_ref, dch_ref, dg_ref], dproj_ref,
                     [first + CBB + j, first + CCB + j, first + CHB + j, first + CGB + j], sems)

    piece = pltpu.VMEM((S, LANES), BF16)
    return pl.pallas_call(
        body, name="conv_bwd", grid=(CONV_W // LANES,),
        in_specs=[_colblock(S, CBB), _colblock(S, CCB), _colblock(S, CHB), _colblock(S, CGB),
                  _colblock(S, ATT_W // LANES), pl.BlockSpec((CONV_TAPS, LANES), lambda j: (0, j)), ANY],
        out_specs=(ANY, pl.BlockSpec((CONV_TAPS, LANES), lambda j: (0, j))),
        out_shape=(jax.ShapeDtypeStruct(dproj.shape, dproj.dtype), jax.ShapeDtypeStruct((CONV_TAPS, CONV_W), F32)),
        input_output_aliases={6: 0},
        scratch_shapes=[piece] * 4 + [pltpu.SemaphoreType.DMA((4,))],
        compiler_params=_params(("arbitrary",)),
    )(rest, rest, rest, rest, dy, cw, dproj)


def _pool_z(u, grp, row):
    s2 = u + _shift_down(u, 1, row)
    s4 = s2 + _shift_down(s2, 2, row)
    s8 = s4 + _shift_down(s4, 4, row)
    s16 = s8 + _shift_down(s8, 8, row)
    sw = jnp.where(grp == 0, s2, jnp.where(grp == 1, s4, jnp.where(grp == 2, s8, s16)))
    return sw / _pool_count(grp, row) - u


def _pool_count(grp, row):
    return jnp.minimum(row + 1, 2 << grp).astype(F32)


def _pool_fwd(rest, pw, scale, y):
    S = rest.shape[0]

    def body(u_ref, g_ref, pw_ref, sc_ref, y_in, y_ref):
        grp = pl.program_id(0)
        row = lax.broadcasted_iota(jnp.int32, (S, LANES), 0)
        z = _pool_z(u_ref[...], grp, row)
        zp = jnp.dot(z.astype(BF16), pw_ref[...].astype(BF16), preferred_element_type=F32)
        g = g_ref[...]
        y_ref[...] = ((zp * sc_ref[...]) * (g * _sigmoid(g))).astype(BF16)

    return pl.pallas_call(
        body, name="pool_fwd", grid=(N_POOL,),
        in_specs=[_colblock(S, PUB), _colblock(S, PGB),
                  pl.BlockSpec((None, POOL_GROUP, POOL_GROUP), lambda j: (j, 0, 0)),
                  pl.BlockSpec((1, LANES), lambda j: (0, j)), ANY],
        out_specs=_colblock(S, (ATT_W + CONV_W) // LANES),
        out_shape=jax.ShapeDtypeStruct(y.shape, y.dtype),
        input_output_aliases={4: 0},
        compiler_params=_params(("parallel",)),
    )(rest, rest, pw, scale, y)


def _pool_bwd(rest, dy, pw, scale, dproj):
    S = rest.shape[0]

    def body(u_ref, g_ref, dy_ref, pw_ref, sc_ref, dproj_in, dproj_ref, dpw_ref, dsc_ref, du_ref, dg_ref, sems):
        grp = pl.program_id(0)
        row = lax.broadcasted_iota(jnp.int32, (S, LANES), 0)
        z = _pool_z(u_ref[...], grp, row).astype(BF16)
        pwb = pw_ref[...].astype(BF16)
        zp = jnp.dot(z, pwb, preferred_element_type=F32)
        g = g_ref[...]
        sg = _sigmoid(g)
        silu = g * sg
        dy = dy_ref[...]
        sc = sc_ref[...]
        dsc_ref[...] = jnp.sum(dy * silu * zp, axis=0, keepdims=True)
        dg_ref[...] = (dy * (zp * sc) * (sg * (1.0 + g * (1.0 - sg)))).astype(BF16)
        dzp = (dy * silu * sc).astype(BF16)
        dpw_ref[...] = lax.dot_general(z, dzp, (((0,), (0,)), ((), ())), preferred_element_type=F32)
        dz = lax.dot_general(dzp, pwb, (((1,), (1,)), ((), ())), preferred_element_type=F32)
        f1 = dz / _pool_count(grp, row)
        f2 = f1 + _shift_up(f1, 1, row)
        f4 = f2 + _shift_up(f2, 2, row)
        f8 = f4 + _shift_up(f4, 4, row)
        f16 = f8 + _shift_up(f8, 8, row)
        fw = jnp.where(grp == 0, f2, jnp.where(grp == 1, f4, jnp.where(grp == 2, f8, f16)))
        du_ref[...] = (fw - dz).astype(BF16)
        first = QKV_W // LANES
        _put_columns([du_ref, dg_ref], dproj_ref, [first + PUB + grp, first + PGB + grp], sems)

    piece = pltpu.VMEM((S, LANES), BF16)
    return pl.pallas_call(
        body, name="pool_bwd", grid=(N_POOL,),
        in_specs=[_colblock(S, PUB), _colblock(S, PGB), _colblock(S, (ATT_W + CONV_W) // LANES),
                  pl.BlockSpec((None, POOL_GROUP, POOL_GROUP), lambda j: (j, 0, 0)),
                  pl.BlockSpec((1, LANES), lambda j: (0, j)), ANY],
        out_specs=(ANY, pl.BlockSpec((None, POOL_GROUP, POOL_GROUP), lambda j: (j, 0, 0)),
                   pl.BlockSpec((1, LANES), lambda j: (0, j))),
        out_shape=(jax.ShapeDtypeStruct(dproj.shape, dproj.dtype),
                   jax.ShapeDtypeStruct((N_POOL, POOL_GROUP, POOL_GROUP), F32), jax.ShapeDtypeStruct((1, POOL_W), F32)),
        input_output_aliases={5: 0},
        scratch_shapes=[piece] * 2 + [pltpu.SemaphoreType.DMA((2,))],
        compiler_params=_params(("arbitrary",)),
    )(rest, rest, dy, pw, scale, dproj)


def _ln_fwd(z, g, b, *, ts):
    S, D = z.shape

    def body(z_ref, g_ref, b_ref, o_ref, ob_ref, obt_ref, xh_ref, rs_ref):
        zz = z_ref[...]
        mu = jnp.mean(zz, axis=1, keepdims=True)
        zc = zz - mu
        rstd = lax.rsqrt(jnp.mean(zc * zc, axis=1, keepdims=True) + LN_EPS)
        xh = zc * rstd
        out = xh * g_ref[...] + b_ref[...]
        o_ref[...] = out
        ob_ref[...] = out.astype(BF16)
        obt_ref[...] = out.T.astype(BF16)
        xh_ref[...] = xh
        rs_ref[...] = rstd

    rowblk = pl.BlockSpec((ts, D), lambda i: (i, 0))
    vec = pl.BlockSpec((1, D), lambda i: (0, 0))
    return pl.pallas_call(
        body, name="ln_fwd", grid=(S // ts,),
        in_specs=[rowblk, vec, vec],
        out_specs=(rowblk, rowblk, pl.BlockSpec((D, ts), lambda i: (0, i)), rowblk,
                   pl.BlockSpec((ts, 1), lambda i: (i, 0))),
        out_shape=(jax.ShapeDtypeStruct((S, D), F32), jax.ShapeDtypeStruct((S, D), BF16),
                   jax.ShapeDtypeStruct((D, S), BF16), jax.ShapeDtypeStruct((S, D), F32),
                   jax.ShapeDtypeStruct((S, 1), F32)),
        compiler_params=_params(("parallel",)),
    )(z, g, b)


def _prep(x, after, *, ts):
    S, D = x.shape

    def body(x_ref, after_ref, xb_ref, xbt_ref):
        xb_ref[...] = x_ref[...].astype(BF16)
        xbt_ref[...] = x_ref[...].T.astype(BF16)

    return pl.pallas_call(
        body, name="prep", grid=(S // ts,),
        in_specs=[pl.BlockSpec((ts, D), lambda i: (i, 0)), ANY],
        out_specs=(pl.BlockSpec((ts, D), lambda i: (i, 0)), pl.BlockSpec((D, ts), lambda i: (0, i))),
        out_shape=(jax.ShapeDtypeStruct((S, D), BF16), jax.ShapeDtypeStruct((D, S), BF16)),
        compiler_params=_params(("parallel",)),
    )(x, after)


def _ln_bwd(dout, xhat, rstd, g, *, ts):
    S, D = dout.shape

    def body(d_ref, xh_ref, rs_ref, g_ref, dz_ref, dzb_ref, dg_ref, db_ref):
        @pl.when(pl.program_id(0) == 0)
        def _():
            dg_ref[...] = jnp.zeros_like(dg_ref)
            db_ref[...] = jnp.zeros_like(db_ref)

        d = d_ref[...]
        xh = xh_ref[...]
        dxh = d * g_ref[...]
        m1 = jnp.mean(dxh, axis=1, keepdims=True)
        m2 = jnp.mean(dxh * xh, axis=1, keepdims=True)
        dz = rs_ref[...] * (dxh - m1 - xh * m2)
        dz_ref[...] = dz
        dzb_ref[...] = dz.astype(BF16)
        dg_ref[...] += jnp.sum(d * xh, axis=0, keepdims=True)
        db_ref[...] += jnp.sum(d, axis=0, keepdims=True)

    rowblk = pl.BlockSpec((ts, D), lambda i: (i, 0))
    vec = pl.BlockSpec((1, D), lambda i: (0, 0))
    return pl.pallas_call(
        body, name="ln_bwd", grid=(S // ts,),
        in_specs=[rowblk, rowblk, pl.BlockSpec((ts, 1), lambda i: (i, 0)), vec],
        out_specs=(rowblk, rowblk, vec, vec),
        out_shape=(jax.ShapeDtypeStruct((S, D), F32), jax.ShapeDtypeStruct((S, D), BF16),
                   jax.ShapeDtypeStruct((1, D), F32), jax.ShapeDtypeStruct((1, D), F32)),
        compiler_params=_params(("arbitrary",)),
    )(dout, xhat, rstd, g)


def _loss_head(y, tgt, *, ts):
    S, D = y.shape

    def body(y_ref, t_ref, l_ref, dy_ref):
        @pl.when(pl.program_id(0) == 0)
        def _():
            l_ref[...] = jnp.zeros_like(l_ref)

        e = y_ref[...] - t_ref[...]
        dy_ref[...] = e * (1.0 / D)
        rowloss = jnp.sum(e * e, axis=1, keepdims=True) * (0.5 / D)
        l_ref[...] += jnp.sum(rowloss, axis=0, keepdims=True)

    rowblk = pl.BlockSpec((ts, D), lambda i: (i, 0))
    return pl.pallas_call(
        body, name="loss_head", grid=(S // ts,),
        in_specs=[rowblk, rowblk],
        out_specs=(pl.BlockSpec((1, 1), lambda i: (0, 0)), rowblk),
        out_shape=(jax.ShapeDtypeStruct((1, 1), F32), jax.ShapeDtypeStruct((S, D), F32)),
        compiler_params=_params(("arbitrary",)),
    )(y, tgt)


def _adamw(lands, srcs, w, m, v, *, name, first=0, into=None):
    R, C = w.shape
    npart = len(lands)
    rp = lands[0].shape[1]
    assert R % rp == 0 and all(p.shape == (N_DEV, rp, C) for p in lands)
    tiled = rp % 8 == 0
    assert tiled or (first == 0 and npart * rp == R and into is None)
    tr = min(ADAMW_BLOCK_ELEMS // (LANES * pl.cdiv(C, LANES)), rp) if tiled else rp
    assert rp % tr == 0
    steps = rp // tr
    c1 = 1.0 - ADAM_B1 ** ADAM_STEP
    c2 = 1.0 - ADAM_B2 ** ADAM_STEP
    slabbed = [p.ndim == 3 for p in srcs]
    n_into = 0 if into is None else 4

    def body(*refs):
        l_refs, s_refs = refs[:npart], refs[npart:2 * npart]
        w_ref, m_ref, v_ref = refs[2 * npart:2 * npart + 3]
        g_ref, d_ref, nm_ref, nv_ref = refs[2 * npart + 3 + n_into:]
        x, y, c = _my_place()
        me = 4 * x + 2 * y + c

        def update(l, rows):
            g = None
            for j in range(N_DEV):
                own = s_refs[l][j] if slabbed[l] else s_refs[l][...]
                p = jnp.where(me == j, own, l_refs[l][j]).astype(F32)
                g = p if g is None else g + p
            nm = ADAM_B1 * m_ref[rows, :] + (1.0 - ADAM_B1) * g
            nv = ADAM_B2 * v_ref[rows, :] + (1.0 - ADAM_B2) * (g * g)
            g_ref[rows, :] = g
            nm_ref[rows, :] = nm
            nv_ref[rows, :] = nv
            d_ref[rows, :] = -ADAM_LR * ((nm / c1) / (jnp.sqrt(nv / c2) + ADAM_EPS) + ADAM_WD * w_ref[rows, :])

        for l in range(npart):
            if tiled:
                pl.when(pl.program_id(0) == l)(lambda l=l: update(l, slice(None)))
            else:
                update(l, slice(l * rp, (l + 1) * rp))

    def part_spec(l, p):
        if p.ndim == 3:
            return pl.BlockSpec((N_DEV, tr, C), lambda q, i: (0, jnp.where(q == l, i, 0), 0))
        return pl.BlockSpec((tr, C), lambda q, i: (jnp.where(q == l, i, 0), 0))

    if tiled:
        blk = pl.BlockSpec((tr, C), lambda q, i: ((first + q) * steps + i, 0))
    else:
        blk = pl.BlockSpec((R, C), lambda q, i: (0, 0))
    out = jax.ShapeDtypeStruct((R, C), F32)
    n_in = 2 * npart + 3
    return pl.pallas_call(
        body, name=name, grid=(npart, steps) if tiled else (1, 1),
        in_specs=[part_spec(l, p) for l, p in enumerate(lands)] + [part_spec(l, p) for l, p in enumerate(srcs)]
        + [blk, blk, blk] + [ANY] * n_into,
        out_specs=(blk, blk, blk, blk),
        out_shape=(out, out, out, out),
        input_output_aliases={n_in + i: i for i in range(n_into)},
        compiler_params=_params(("arbitrary", "arbitrary")),
    )(*lands, *srcs, w, m, v, *(into or ()))


def _my_place():
    return lax.axis_index("x"), lax.axis_index("y"), lax.axis_index("c")


HBM = pl.BlockSpec(memory_space=pltpu.HBM)
SEM = pl.BlockSpec(memory_space=pltpu.SEMAPHORE)
DATAFLOW = pltpu.SideEffectType.DATAFLOW_SIDE_EFFECTING
EVERYONE = (1, 2, 3, 4, 5, 6, 7)
SIBLING = 1
SAME_CORE = (4, 2, 6)


def _xchg_copies(src, land, send_sem, recv_sem, gather, peers):
    x, y, c = _my_place()
    me = 4 * x + 2 * y + c
    copies = []
    for k, bits in enumerate(peers):
        px, py, pc = x ^ ((bits >> 2) & 1), y ^ ((bits >> 1) & 1), c ^ (bits & 1)
        peer = 4 * px + 2 * py + pc
        for a in range(len(src)):
            i = a * len(peers) + k
            out = src[a] if gather[a] else src[a].at[peer]
            sent, landed = [pltpu.make_async_remote_copy(
                src_ref=out, dst_ref=land[a].at[slab], send_sem=send_sem.at[i], recv_sem=recv_sem.at[i],
                device_id=(px, py, pc), device_id_type=MESH) for slab in (me, peer)]
            copies.append((sent, landed))
    return copies


def _xchg_start(srcs, *, gather, name, peers=EVERYONE, after=None):
    n = len(srcs)
    follows = [] if after is None else [after]
    lands = [lax.empty(((N_DEV,) + v.shape) if g else v.shape, v.dtype) for v, g in zip(srcs, gather)]

    def body(*refs):
        src, land = refs[:n], refs[n:2 * n]
        send_sem, recv_sem = refs[2 * n + len(follows)], refs[2 * n + len(follows) + 1]
        token = refs[-1]
        for sent, _ in _xchg_copies(src, land, send_sem, recv_sem, gather, peers):
            sent.start()
        token[...] = jnp.zeros_like(token)

    sems = pltpu.SemaphoreType.DMA((n * len(peers),))
    outs = pl.pallas_call(
        body, name=name,
        out_shape=(sems, sems, *[pltpu.HBM(v.shape, v.dtype) for v in srcs + lands],
                   jax.ShapeDtypeStruct((8, LANES), F32)),
        in_specs=[HBM] * (2 * n) + [ANY] * len(follows),
        out_specs=(SEM, SEM, *[HBM] * (2 * n), pl.BlockSpec(memory_space=pltpu.VMEM)),
        input_output_aliases={i: 2 + i for i in range(2 * n)},
        compiler_params=pltpu.CompilerParams(has_side_effects=DATAFLOW),
    )(*[pltpu.with_memory_space_constraint(v, pltpu.HBM) for v in srcs + lands], *follows)
    return dict(send=outs[0], recv=outs[1], srcs=list(outs[2:2 + n]), lands=list(outs[2 + n:2 + 2 * n]),
                token=outs[-1], gather=gather, peers=peers)


def _xchg_wait(h, after, *, name):
    n = len(h["srcs"])
    gather, peers = h["gather"], h["peers"]
    after = list(after) if isinstance(after, (list, tuple)) else [after]

    def body(*refs):
        src, land = refs[:n], refs[n:2 * n]
        send_sem, recv_sem = refs[2 * n], refs[2 * n + 1]
        for sent, landed in _xchg_copies(src, land, send_sem, recv_sem, gather, peers):
            sent.wait_send()
            landed.wait_recv()

    thru = h["srcs"] + h["lands"]
    outs = pl.pallas_call(
        body, name=name,
        out_shape=[pltpu.HBM(v.shape, v.dtype) for v in thru],
        in_specs=[HBM] * (2 * n) + [SEM, SEM] + [ANY] * len(after),
        out_specs=[HBM] * (2 * n),
        input_output_aliases={i: i for i in range(2 * n)},
        compiler_params=pltpu.CompilerParams(has_side_effects=DATAFLOW),
    )(*thru, h["send"], h["recv"], *after)
    return list(outs[:n]), list(outs[n:])


def _relay_copies(land, send_sem, recv_sem):
    x, y, c = _my_place()
    me = 4 * x + 2 * y + c
    copies = []
    for k, bits in enumerate(SAME_CORE):
        for a in range(len(land)):
            i = a * len(SAME_CORE) + k
            sent, landed = [pltpu.make_async_remote_copy(
                src_ref=land[a].at[me ^ bits], dst_ref=land[a].at[slab], send_sem=send_sem.at[i], recv_sem=recv_sem.at[i],
                device_id=(x, y, 1 - c), device_id_type=MESH) for slab in (me ^ bits, me ^ bits ^ SIBLING)]
            copies.append((sent, landed))
    return copies


def _relay_start(lands, *, name):
    n = len(lands)

    def body(*refs):
        land = refs[:n]
        send_sem, recv_sem = refs[n], refs[n + 1]
        token = refs[-1]
        for sent, _ in _relay_copies(land, send_sem, recv_sem):
            sent.start()
        token[...] = jnp.zeros_like(token)

    sems = pltpu.SemaphoreType.DMA((n * len(SAME_CORE),))
    outs = pl.pallas_call(
        body, name=name,
        out_shape=(sems, sems, *[pltpu.HBM(v.shape, v.dtype) for v in lands], jax.ShapeDtypeStruct((8, LANES), F32)),
        in_specs=[HBM] * n,
        out_specs=(SEM, SEM, *[HBM] * n, pl.BlockSpec(memory_space=pltpu.VMEM)),
        input_output_aliases={i: 2 + i for i in range(n)},
        compiler_params=pltpu.CompilerParams(has_side_effects=DATAFLOW),
    )(*[pltpu.with_memory_space_constraint(v, pltpu.HBM) for v in lands])
    return dict(send=outs[0], recv=outs[1], lands=list(outs[2:2 + n]), token=outs[-1])


def _relay_wait(h, after, *, name):
    n = len(h["lands"])

    def body(*refs):
        land = refs[:n]
        send_sem, recv_sem = refs[n], refs[n + 1]
        for sent, landed in _relay_copies(land, send_sem, recv_sem):
            sent.wait_send()
            landed.wait_recv()

    outs = pl.pallas_call(
        body, name=name,
        out_shape=[pltpu.HBM(v.shape, v.dtype) for v in h["lands"]],
        in_specs=[HBM] * n + [SEM, SEM, ANY],
        out_specs=[HBM] * n,
        input_output_aliases={i: i for i in range(n)},
        compiler_params=pltpu.CompilerParams(has_side_effects=DATAFLOW),
    )(*h["lands"], h["send"], h["recv"], after)
    return list(outs)


def _with_own(land, own, me):
    slot = lax.broadcasted_iota(jnp.int32, land.shape, 0)
    return jnp.where(slot == me, own if own.ndim == land.ndim else own[None], land)


def _w_in_segments(j, cols):
    lo, hi = j * cols, (j + 1) * cols
    segs = []
    for a, b, where, shift in ((0, FG_AT, "main", 0), (FG_AT, FG_AT + N_HEADS, "fg", -FG_AT),
                               (FG_AT + N_HEADS, IN_W, "main", -N_HEADS)):
        s0, s1 = max(lo, a), min(hi, b)
        if s0 < s1:
            segs.append((s0 - lo, s1 - s0, where, s0 + shift))
    return segs


SLAB_W = 1024


def _window(off, width):
    base = min((off // LANES) * LANES, width - SLAB_W)
    return base, off - base


def _w_unpack(land, own, *, tr):
    _, D, cols = land.shape
    assert cols + LANES - 1 <= SLAB_W

    def body(land_ref, own_ref, main_ref, slab):
        x, y, c = _my_place()
        me = 4 * x + 2 * y + c
        lane = lax.broadcasted_iota(jnp.int32, (tr, SLAB_W), 1)
        main_ref[...] = jnp.zeros_like(main_ref)
        slab[...] = jnp.zeros_like(slab)
        for j in range(N_DEV):
            slab[:, :cols] = jnp.where(me == j, own_ref[...], land_ref[j])
            v = slab[...]
            segs = _w_in_segments(j, cols)
            for lo, n, where, dst in segs:
                part = v if len(segs) == 1 else jnp.where((lane >= lo) & (lane < lo + n), v, jnp.zeros_like(v))
                if where == "fg":
                    main_ref[:, MAIN_W:] = pltpu.roll(part, (SLAB_W - lo) % SLAB_W, 1)[:, :LANES]
                else:
                    base, r = _window(dst - lo, MAIN_W)
                    assert 0 <= r and lo + n + r <= SLAB_W
                    main_ref[:, base:base + SLAB_W] += pltpu.roll(part, r, 1)

    return pl.pallas_call(
        body, name="w_unpack", grid=(D // tr,),
        in_specs=[pl.BlockSpec((N_DEV, tr, cols), lambda i: (0, i, 0)), pl.BlockSpec((tr, cols), lambda i: (i, 0))],
        out_specs=pl.BlockSpec((tr, ALL_W), lambda i: (i, 0)),
        out_shape=jax.ShapeDtypeStruct((D, ALL_W), BF16),
        scratch_shapes=[pltpu.VMEM((tr, SLAB_W), BF16)],
        compiler_params=_params(("parallel",)),
    )(land, own)


def _g_pack(dw_main, dw_fg, cols, *, tr):
    D = dw_main.shape[0]

    def body(main_ref, fg_ref, out_ref, slab):
        lane = lax.broadcasted_iota(jnp.int32, (tr, SLAB_W), 1)
        slab[...] = jnp.zeros_like(slab)
        slab[:, :LANES] = fg_ref[...]
        fg = slab[...]
        for j in range(N_DEV):
            v = None
            for lo, n, where, src in _w_in_segments(j, cols):
                if where == "fg":
                    part = pltpu.roll(fg, lo, 1)
                else:
                    base, r = _window(src - lo, MAIN_W)
                    assert 0 <= r and lo + n + r <= SLAB_W
                    part = pltpu.roll(main_ref[:, base:base + SLAB_W], (SLAB_W - r) % SLAB_W, 1)
                v = part if v is None else jnp.where((lane >= lo) & (lane < lo + n), part, v)
            out_ref[j] = v[:, :cols]

    return pl.pallas_call(
        body, name="g_pack", grid=(D // tr,),
        in_specs=[pl.BlockSpec((tr, MAIN_W), lambda i: (i, 0)), pl.BlockSpec((tr, LANES), lambda i: (i, 0))],
        out_specs=pl.BlockSpec((N_DEV, tr, cols), lambda i: (0, i, 0)),
        out_shape=jax.ShapeDtypeStruct((N_DEV, D, cols), BF16),
        scratch_shapes=[pltpu.VMEM((tr, SLAB_W), BF16)],
        compiler_params=_params(("parallel",)),
    )(dw_main, dw_fg)


TQ = 1024
TS = 256
W_TILE_ROWS = 128


def _layer_fwd_proj(xb, w_in):
    qkv = _mm(xb, w_in, out_dtype=BF16, tm=2048, tn=512, tk=D_MODEL, name="mm_qkv", n_cols=QKV_W)
    rest = _mm(xb, w_in, out_dtype=F32, tm=2048, tn=512, tk=D_MODEL, name="mm_rest", n_off=QKV_W, n_cols=REST_W)
    fgp = _mm(xb, w_in, out_dtype=F32, tm=1024, tn=LANES, tk=D_MODEL, name="mm_fg", n_off=MAIN_W, n_cols=LANES)
    return qkv, rest, fgp


def _layer_fwd_mix(x, xbt, proj, wl):
    qkv, rest, fgp = proj
    cum = _fg_fwd(fgp, wl["b_f"])
    S = x.shape[0]
    cumr = (cum[:, :N_HEADS].T * LOG2E).reshape(N_HEADS, S // TQ, 1, TQ)
    o, y, lse = _attn_fwd(qkv, rest, cum, cumr, tq=TQ)
    y = _conv_fwd(rest, wl["conv_w"], y)
    y = _pool_fwd(rest, wl["pool_w"], wl["pool_scale"], y)
    z = _mm(y, wl["w_out"], out_dtype=F32, tm=2048, tn=512, tk=D_MODEL, name="mm_out", c=x, c_scale=ALPHA)
    out, outb, outbt, xhat, rstd = _ln_fwd(z, wl["ln_g"], wl["ln_b"], ts=TS)
    saved = dict(xbt=xbt, qkv=qkv, rest=rest, fgp=fgp, cum=cum, cumr=cumr, o=o, lse=lse, y=y, xhat=xhat, rstd=rstd)
    return out, outb, outbt, saved


def _layer_bwd_mix(dout, sv, wl):
    S = dout.shape[0]
    dz, dzb, dln_g, dln_b = _ln_bwd(dout, sv["xhat"], sv["rstd"], wl["ln_g"], ts=TS)
    dy = _mm(dzb, wl["w_out"], out_dtype=F32, tm=2048, tn=512, tk=D_MODEL, name="mm_dy", bt=True)
    dw_out = _mm(sv["y"], dzb, out_dtype=BF16, tm=512, tn=D_MODEL, tk=2048, name="mm_dw_out", at=True)
    rest = sv["rest"]
    do, dproj, delta = _attn_bwd_pre(rest, sv["o"], dy, tq=TQ)
    a2 = sv["cumr"] - sv["lse"] * LOG2E
    dproj, ds_q, ds_k = _attn_bwd(sv["qkv"], do, a2, delta, sv["cum"], dproj, tq=TQ)
    dcum = jnp.pad(ds_q.reshape(N_HEADS, S).T, ((0, 0), (0, LANES - N_HEADS))) - ds_k
    dproj, db_f = _fg_bwd(dcum, sv["fgp"], wl["b_f"], dproj)
    dproj, dconv_w = _conv_bwd(rest, dy, wl["conv_w"], dproj)
    dproj, dpool_w, dpool_scale = _pool_bwd(rest, dy, wl["pool_w"], wl["pool_scale"], dproj)
    grads = dict(b_f=db_f[:, :N_HEADS], conv_w=dconv_w, pool_w=dpool_w, pool_scale=dpool_scale, w_out=dw_out,
                 ln_g=dln_g, ln_b=dln_b)
    return grads, (dproj, dz)


def _layer_bwd_w_in(mid, sv):
    dproj, dz = mid
    xt = sv["xbt"]
    S = xt.shape[1]
    dw_main = _mm(xt, dproj, out_dtype=BF16, tm=1024, tn=512, tk=S, name="mm_dw_main", n_cols=MAIN_W)
    dw_fg = _mm(xt, dproj, out_dtype=BF16, tm=1024, tn=LANES, tk=2048, name="mm_dw_fg", n_off=MAIN_W, n_cols=LANES)
    return (dw_main, dw_fg), (dproj, dz)


def _layer_bwd_input(ctx, wl, after):
    dproj, dz = ctx
    return _mm(dproj, wl["w_in"], out_dtype=F32, tm=1024, tn=256, tk=ALL_W, name="mm_dx", c=dz, c_scale=ALPHA, bt=True,
               after=after)


def kernel(x, w_in, b_f, conv_w, pool_w, pool_scale, w_out, ln_g, ln_b, loss_target, m_w_in, m_b_f, m_conv_w, m_pool_w, m_pool_scale, m_w_out, m_ln_g, m_ln_b, v_w_in, v_b_f, v_conv_w, v_pool_w, v_pool_scale, v_w_out, v_ln_g, v_ln_b):
    L, D, cols = w_in.shape
    rows_out = w_out.shape[1]
    ccols = conv_w.shape[2]
    mx, my, mc = _my_place()
    me = 4 * mx + 2 * my + mc

    gathers = []
    for l in range(L):
        gathers.append(_xchg_start([w_in[l].astype(BF16), w_out[l].astype(BF16), conv_w[l]], gather=[True] * 3,
                                   name=f"gather_start_{l}",
                                   peers=(SIBLING,) + SAME_CORE if l == 0 else EVERYONE,
                                   after=gathers[-1]["token"] if gathers else None))
    started = gathers[-1]["token"]

    xl = x[0]
    xb, xbt = _prep(xl, started, ts=TS)
    weights, saved = [], []
    for l in range(L):
        (own_in, own_out, own_cw), (g_in, g_out, g_cw) = _xchg_wait(gathers[l], xb, name=f"gather_wait_{l}")
        if l == 0:
            g_in, g_out, g_cw = _relay_wait(_relay_start([g_in, g_out, g_cw], name="gather_relay_start"), xb,
                                            name="gather_relay_wait")
        w_all = _w_unpack(g_in, own_in, tr=W_TILE_ROWS)
        proj = _layer_fwd_proj(xb, w_all)
        w_o = _with_own(g_out, own_out, me).reshape(N_DEV * rows_out, D)
        cw = jnp.transpose(_with_own(g_cw, own_cw, me), (1, 0, 2)).reshape(CONV_TAPS, N_DEV * ccols)
        wl = dict(w_in=w_all, w_out=w_o, conv_w=cw,
                  b_f=jnp.pad(b_f[l][None, :], ((0, 0), (0, LANES - N_HEADS))), pool_w=pool_w[l],
                  pool_scale=pool_scale[l][None, :], ln_g=ln_g[l][None, :], ln_b=ln_b[l][None, :])
        xl, xb, xbt, sv = _layer_fwd_mix(xl, xbt, proj, wl)
        weights.append(wl)
        saved.append(sv)

    loss, dx = _loss_head(xl, loss_target[0], ts=TS)
    loss = lax.psum(loss[0, 0], ("x", "y", "c"))

    names = ["w_in", "w_out", "conv_w", "b_f", "pool_w", "pool_scale", "ln_g", "ln_b"]
    sharded = names[:3]
    scatters = [None] * L
    for l in reversed(range(L)):
        g, mid = _layer_bwd_mix(dx, saved[l], weights[l])
        (dw_main, dw_fg), ctx = _layer_bwd_w_in(mid, saved[l])
        g["w_in"] = _g_pack(dw_main, dw_fg, cols, tr=W_TILE_ROWS)
        g["w_out"] = g["w_out"].reshape(N_DEV, rows_out, D)
        g["conv_w"] = jnp.transpose(g["conv_w"].reshape(CONV_TAPS, N_DEV, ccols), (1, 0, 2))
        scatters[l] = _xchg_start([g[k] for k in names], gather=[k not in sharded for k in names],
                                  name=f"scatter_start_{l}")
        dx = _layer_bwd_input(ctx, weights[l], scatters[l]["token"])

    given = dict(w_in=(w_in, m_w_in, v_w_in), b_f=(b_f, m_b_f, v_b_f), conv_w=(conv_w, m_conv_w, v_conv_w),
                 pool_w=(pool_w, m_pool_w, v_pool_w), pool_scale=(pool_scale, m_pool_scale, v_pool_scale),
                 w_out=(w_out, m_w_out, v_w_out), ln_g=(ln_g, m_ln_g, v_ln_g), ln_b=(ln_b, m_ln_b, v_ln_b))
    sent, landed = [{} for _ in range(L)], [{} for _ in range(L)]

    def wait(l, after):
        s_, l_ = _xchg_wait(scatters[l], after, name=f"scatter_wait_{l}")
        sent[l].update(zip(names, s_))
        landed[l].update(zip(names, l_))

    last_start = scatters[0]["token"][0, 0]

    def update(k, layers, into=None):
        w, m, v = given[k]
        C = w.shape[-1]
        if k == "w_in":
            m, v = m + last_start, v + last_start
        w2, m2, v2 = (t.reshape(-1, C) for t in (w, m, v))
        rp = w2.shape[0] // L
        lands = [landed[l][k].reshape(N_DEV, rp, C) for l in layers]
        srcs = [sent[l][k].reshape((N_DEV, rp, C) if k in sharded else (rp, C)) for l in layers]
        return _adamw(lands, srcs, w2, m2, v2, name=f"adamw_{k}_{layers[0]}", first=layers[0], into=into)

    big = ["w_in", "w_out", "pool_w"]
    for l in reversed(range(1, L)):
        wait(l, dx)
    upper = {k: update(k, list(range(1, L))) for k in big} if L > 1 else {}
    wait(0, [upper[k][0] for k in big] if upper else dx)
    res = {k: update(k, [0], into=upper[k]) if k in upper else update(k, list(range(L))) for k in names}
    res = {k: [t.reshape(given[k][0].shape) for t in outs] for k, outs in res.items()}

    order = ["w_in", "b_f", "conv_w", "pool_w", "pool_scale", "w_out", "ln_g", "ln_b"]
    return (loss, dx[None], *[res[k][0] for k in order], *[res[k][1] for k in order],
            *[res[k][2] for k in order], *[res[k][3] for k in order])
```

```python
import jax
import jax.numpy as jnp
from jax import lax
from jax.experimental import pallas as pl
from jax.experimental.pallas import tpu as pltpu

F32 = jnp.float32
BF16 = jnp.bfloat16

N_DEV = 8
D_MODEL = 2048
N_HEADS = 8
HEAD_DIM = 128
ATT_W = N_HEADS * HEAD_DIM
CONV_W = 512
CONV_TAPS = 3
POOL_W = 512
POOL_GROUP = 128
N_POOL = POOL_W // POOL_GROUP
FG_AT = 4 * ATT_W
IN_W = FG_AT + N_HEADS + 4 * CONV_W + 2 * POOL_W
MAIN_W = IN_W - N_HEADS
LANES = 128
ALL_W = MAIN_W + LANES
LN_EPS = 1e-5
DEPTH = 4
ALPHA = (2 * DEPTH) ** 0.25
SCALE = HEAD_DIM ** -0.5
LOG2E = 1.4426950408889634
LN2 = 0.6931471805599453
NEG = -1e30

ADAM_LR, ADAM_B1, ADAM_B2, ADAM_EPS, ADAM_WD, ADAM_STEP = 0.001, 0.9, 0.999, 1e-08, 0.01, 10

QKV_W = 3 * ATT_W
REST_W = MAIN_W - QKV_W
QB, KB, VB = 0, 8, 16
GB = 0
CBB, CCB, CHB, CGB = 8, 12, 16, 20
PUB, PGB = 24, 28
FGB = MAIN_W // LANES
VMEM_LIMIT = 56 * 1024 * 1024
ADAMW_BLOCK_ELEMS = 64 * 1024

MESH = pl.DeviceIdType.MESH
ANY = pl.BlockSpec(memory_space=pl.ANY)


def _params(semantics):
    return pltpu.CompilerParams(dimension_semantics=semantics, vmem_limit_bytes=VMEM_LIMIT)


def _sigmoid(x):
    return 1.0 / (1.0 + jnp.exp(-x))


def _mm(a, b, *, out_dtype, tm, tn, tk, name, c=None, c_scale=1.0, n_off=0, n_cols=None, bt=False, at=False,
        after=None):
    M, K = a.shape[::-1] if at else a.shape
    N = b.shape[0 if bt else 1] if n_cols is None else n_cols
    assert not (bt and n_off)
    dims = (((0 if at else 1,), (1 if bt else 0,)), ((), ()))
    tm, tn, tk = min(tm, M), min(tn, N), min(tk, K)
    assert M % tm == 0 and N % tn == 0 and K % tk == 0 and n_off % tn == 0, (a.shape, b.shape, tm, tn, tk)
    nk, joff = K // tk, n_off // tn
    has_c, has_after = c is not None, after is not None

    def body(*refs):
        a_ref, b_ref = refs[0], refs[1]
        c_ref = refs[2] if has_c else None
        o_ref = refs[2 + has_c + has_after]

        def finish(r):
            if has_c:
                r = r + c_scale * c_ref[...]
            o_ref[...] = r.astype(out_dtype)

        if nk == 1:
            finish(lax.dot_general(a_ref[...], b_ref[...], dims, preferred_element_type=F32))
        else:
            acc_ref = refs[3 + has_c + has_after]
            k = pl.program_id(2)

            @pl.when(k == 0)
            def _():
                acc_ref[...] = jnp.zeros_like(acc_ref)

            acc_ref[...] += lax.dot_general(a_ref[...], b_ref[...], dims, preferred_element_type=F32)

            @pl.when(k == nk - 1)
            def _():
                finish(acc_ref[...])

    in_specs = [pl.BlockSpec((tk, tm), lambda i, j, k: (k, i)) if at else pl.BlockSpec((tm, tk), lambda i, j, k: (i, k)),
                pl.BlockSpec((tn, tk), lambda i, j, k: (j, k)) if bt
                else pl.BlockSpec((tk, tn), lambda i, j, k: (k, j + joff))]
    args = [a, b]
    if has_c:
        in_specs.append(pl.BlockSpec((tm, tn), lambda i, j, k: (i, j)))
        args.append(c)
    if has_after:
        in_specs.append(ANY)
        args.append(after)
    return pl.pallas_call(
        body, name=name, grid=(M // tm, N // tn, nk),
        in_specs=in_specs,
        out_specs=pl.BlockSpec((tm, tn), lambda i, j, k: (i, j)),
        out_shape=jax.ShapeDtypeStruct((M, N), out_dtype),
        scratch_shapes=[pltpu.VMEM((tm, tn), F32)] if nk > 1 else [],
        compiler_params=_params(("parallel", "parallel", "arbitrary")),
    )(*args)


def _shift_down(x, k, row):
    return jnp.where(row >= k, pltpu.roll(x, k, 0), 0.0)


def _shift_up(x, k, row):
    s = x.shape[0]
    return jnp.where(row < s - k, pltpu.roll(x, s - k, 0), 0.0)


def _fg_fwd(fgp, bf):
    S = fgp.shape[0]

    def body(f_ref, b_ref, cum_ref):
        a = f_ref[...] + b_ref[...]
        x = jnp.minimum(a, 0.0) - jnp.log(1.0 + jnp.exp(-jnp.abs(a)))
        row = lax.broadcasted_iota(jnp.int32, x.shape, 0)
        k = 1
        while k < S:
            x = x + _shift_down(x, k, row)
            k *= 2
        cum_ref[...] = x

    return pl.pallas_call(body, name="fg_fwd", out_shape=jax.ShapeDtypeStruct((S, LANES), F32),
                          compiler_params=_params(None))(fgp, bf)


def _fg_bwd(dcum, fgp, bf, dproj):
    S = fgp.shape[0]

    def body(d_ref, f_ref, b_ref, dproj_in, dfg_ref, dbf_ref):
        x = d_ref[...]
        row = lax.broadcasted_iota(jnp.int32, x.shape, 0)
        k = 1
        while k < S:
            x = x + _shift_up(x, k, row)
            k *= 2
        dfg = x * _sigmoid(-(f_ref[...] + b_ref[...]))
        dfg_ref[...] = dfg.astype(BF16)
        dbf_ref[...] = jnp.sum(dfg, axis=0, keepdims=True)

    whole = pl.BlockSpec((S, LANES), lambda i: (0, 0))
    vec = pl.BlockSpec((1, LANES), lambda i: (0, 0))
    return pl.pallas_call(
        body, name="fg_bwd", grid=(1,),
        in_specs=[whole, whole, vec, ANY],
        out_specs=(pl.BlockSpec((S, LANES), lambda i: (0, FGB)), vec),
        out_shape=(jax.ShapeDtypeStruct(dproj.shape, dproj.dtype), jax.ShapeDtypeStruct((1, LANES), F32)),
        input_output_aliases={3: 0},
        compiler_params=_params(("arbitrary",)),
    )(dcum, fgp, bf, dproj)


def _put_columns(pieces, dst, blocks, sems):
    copies = [pltpu.make_async_copy(p, dst.at[:, pl.ds(pl.multiple_of(b * LANES, LANES), LANES)], sems.at[i])
              for i, (p, b) in enumerate(zip(pieces, blocks))]
    for cp in copies:
        cp.start()
    for cp in copies:
        cp.wait()


def _colblock(S, off):
    return pl.BlockSpec((S, LANES), lambda h: (0, h + off))


def _head_row(nq, tq):
    return pl.BlockSpec((None, nq, 1, tq), lambda h: (h, 0, 0, 0))


def _lane_of(ref, h):
    lane = lax.broadcasted_iota(jnp.int32, ref.shape, 1)
    return jnp.sum(jnp.where(lane == h, ref[...], 0.0), axis=1, keepdims=True)


def _attn_fwd(qkv, rest, cum, cumr, *, tq):
    S = qkv.shape[0]
    nq = S // tq

    def body(q_ref, k_ref, v_ref, g_ref, cum_ref, cumr_ref, o_ref, y_ref, lse_ref, cc):
        cc[...] = _lane_of(cum_ref, pl.program_id(0)) * LOG2E
        tri = (lax.broadcasted_iota(jnp.int32, (tq, tq), 0) <= lax.broadcasted_iota(jnp.int32, (tq, tq), 1))

        def q_step(qi, _):
            rows = pl.ds(pl.multiple_of(qi * tq, tq), tq)
            q = q_ref[rows, :]
            ci = cumr_ref[qi]

            def tile(kj, carry, masked):
                m, l, acc = carry
                cols = pl.ds(pl.multiple_of(kj * tq, tq), tq)
                s = lax.dot_general(k_ref[cols, :], q, (((1,), (1,)), ((), ())), preferred_element_type=F32)
                s = s * (SCALE * LOG2E) + ci - cc[cols, :]
                if masked:
                    s = jnp.where(tri, s, NEG)
                m_new = jnp.maximum(m, jnp.max(s, axis=0, keepdims=True))
                a = jnp.exp2(m - m_new)
                p = jnp.exp2(s - m_new)
                l = a * l + jnp.sum(p, axis=0, keepdims=True)
                acc = a * acc + lax.dot_general(v_ref[cols, :], p.astype(BF16), (((0,), (0,)), ((), ())),
                                                preferred_element_type=F32)
                return m_new, l, acc

            init = (jnp.full((1, tq), NEG, F32), jnp.zeros((1, tq), F32), jnp.zeros((HEAD_DIM, tq), F32))
            carry = lax.fori_loop(0, qi, lambda kj, c: tile(kj, c, False), init)
            m, l, acc = tile(qi, carry, True)
            o = (acc / l).T
            g = g_ref[rows, :]
            o_ref[rows, :] = o
            y_ref[rows, :] = (o * (g * _sigmoid(g))).astype(BF16)
            lse_ref[qi] = m * LN2 + jnp.log(l)
            return 0

        lax.fori_loop(0, nq, q_step, 0)

    return pl.pallas_call(
        body, name="attn_fwd", grid=(N_HEADS,),
        in_specs=[_colblock(S, QB), _colblock(S, KB), _colblock(S, VB), _colblock(S, GB),
                  pl.BlockSpec((S, LANES), lambda h: (0, 0)), _head_row(nq, tq)],
        out_specs=(_colblock(S, 0), _colblock(S, 0), _head_row(nq, tq)),
        out_shape=(jax.ShapeDtypeStruct((S, ATT_W), F32), jax.ShapeDtypeStruct((S, D_MODEL), BF16),
                   jax.ShapeDtypeStruct((N_HEADS, nq, 1, tq), F32)),
        scratch_shapes=[pltpu.VMEM((S, 1), F32)],
        compiler_params=_params(("arbitrary",)),
    )(qkv, qkv, qkv, rest, cum, cumr)


def _attn_bwd_pre(rest, o, dy, *, tq):
    S = o.shape[0]
    nq = S // tq

    def body(g_ref, o_ref, dy_ref, do_ref, dg_ref, dl_ref):
        g = g_ref[...]
        sg = _sigmoid(g)
        ov = o_ref[...]
        dy = dy_ref[...]
        do = dy * (g * sg)
        do_ref[...] = do.astype(BF16)
        dg_ref[...] = (dy * ov * (sg * (1.0 + g * (1.0 - sg)))).astype(BF16)
        rows = lax.dot_general(jnp.ones((8, HEAD_DIM), F32), do * ov, (((1,), (1,)), ((), ())),
                               precision=lax.Precision.HIGHEST, preferred_element_type=F32)
        for qi in range(nq):
            dl_ref[qi] = rows[0:1, qi * tq:(qi + 1) * tq]

    att = jax.ShapeDtypeStruct((S, ATT_W), BF16)
    return pl.pallas_call(
        body, name="attn_bwd_pre", grid=(N_HEADS,),
        in_specs=[_colblock(S, GB), _colblock(S, 0), _colblock(S, 0)],
        out_specs=(_colblock(S, 0), _colblock(S, QKV_W // LANES), _head_row(nq, tq)),
        out_shape=(att, jax.ShapeDtypeStruct((S, ALL_W), BF16), jax.ShapeDtypeStruct((N_HEADS, nq, 1, tq), F32)),
        compiler_params=_params(("parallel",)),
    )(rest, o, dy)


def _attn_bwd(qkv, do, a2, delta, cum, dproj, *, tq):
    S = qkv.shape[0]
    nq = S // tq
    tdot = (((0,), (0,)), ((), ()))
    ndot = (((1,), (1,)), ((), ()))

    def body(q_ref, k_ref, v_ref, do_ref, a_ref, dl_ref, cum_ref, dproj_in, dproj_ref, dsq_ref, dsk_ref,
             dqa, dq_ref, dk_ref, dv_ref, cc, sems):
        h = pl.program_id(0)
        dqa[...] = jnp.zeros_like(dqa)
        dsq_ref[...] = jnp.zeros_like(dsq_ref)
        cc[...] = _lane_of(cum_ref, h) * LOG2E

        @pl.when(h == 0)
        def _():
            dsk_ref[...] = jnp.zeros_like(dsk_ref)

        tri = (lax.broadcasted_iota(jnp.int32, (tq, tq), 0) <= lax.broadcasted_iota(jnp.int32, (tq, tq), 1))
        mine = lax.broadcasted_iota(jnp.int32, (tq, LANES), 1) == h

        def k_step(kj, _):
            cols = pl.ds(pl.multiple_of(kj * tq, tq), tq)
            k = k_ref[cols, :]
            v = v_ref[cols, :]
            cj = cc[cols, :]

            def tile(qi, carry, masked):
                dk, dv, ksum = carry
                rows = pl.ds(pl.multiple_of(qi * tq, tq), tq)
                q = q_ref[rows, :]
                dot = do_ref[rows, :]
                s = lax.dot_general(k, q, ndot, preferred_element_type=F32)
                p = jnp.exp2(s * (SCALE * LOG2E) + a_ref[qi] - cj)
                if masked:
                    p = jnp.where(tri, p, 0.0)
                dv = dv + jnp.dot(p.astype(BF16), dot, preferred_element_type=F32)
                dp = lax.dot_general(v, dot, ndot, preferred_element_type=F32)
                ds = p * (dp - dl_ref[qi])
                dsq_ref[qi] += jnp.sum(ds, axis=0, keepdims=True)
                ksum = ksum + jnp.sum(ds, axis=1, keepdims=True)
                dsb = (ds * SCALE).astype(BF16)
                dk = dk + jnp.dot(dsb, q, preferred_element_type=F32)
                dqa[rows, :] += lax.dot_general(dsb, k, tdot, preferred_element_type=F32)
                return dk, dv, ksum

            init = (jnp.zeros((tq, HEAD_DIM), F32), jnp.zeros((tq, HEAD_DIM), F32), jnp.zeros((tq, 1), F32))
            carry = tile(kj, init, True)
            dk, dv, ksum = lax.fori_loop(kj + 1, nq, lambda qi, c: tile(qi, c, False), carry)
            dk_ref[cols, :] = dk.astype(BF16)
            dv_ref[cols, :] = dv.astype(BF16)
            dsk_ref[cols, :] += jnp.where(mine, ksum, 0.0)
            return 0

        lax.fori_loop(0, nq, k_step, 0)
        dq_ref[...] = dqa[...].astype(BF16)
        _put_columns([dq_ref, dk_ref, dv_ref], dproj_ref, [QB + h, KB + h, VB + h], sems)

    head = pltpu.VMEM((S, HEAD_DIM), BF16)
    return pl.pallas_call(
        body, name="attn_bwd", grid=(N_HEADS,),
        in_specs=[_colblock(S, QB), _colblock(S, KB), _colblock(S, VB), _colblock(S, 0),
                  _head_row(nq, tq), _head_row(nq, tq), pl.BlockSpec((S, LANES), lambda h: (0, 0)), ANY],
        out_specs=(ANY, _head_row(nq, tq), pl.BlockSpec((S, LANES), lambda h: (0, 0))),
        out_shape=(jax.ShapeDtypeStruct(dproj.shape, dproj.dtype), jax.ShapeDtypeStruct((N_HEADS, nq, 1, tq), F32),
                   jax.ShapeDtypeStruct((S, LANES), F32)),
        input_output_aliases={7: 0},
        scratch_shapes=[pltpu.VMEM((S, HEAD_DIM), F32), head, head, head, pltpu.VMEM((S, 1), F32),
                        pltpu.SemaphoreType.DMA((3,))],
        compiler_params=_params(("arbitrary",)),
    )(qkv, qkv, qkv, do, a2, delta, cum, dproj)


def _conv_taps(u, w_ref, row):
    return (w_ref[0:1, :] * _shift_down(u, 2, row) + w_ref[1:2, :] * _shift_down(u, 1, row)) + w_ref[2:3, :] * u


def _conv_fwd(rest, cw, y):
    S = rest.shape[0]

    def body(cb_ref, cc_ref, ch_ref, g_ref, w_ref, y_in, y_ref):
        row = lax.broadcasted_iota(jnp.int32, (S, LANES), 0)
        g = g_ref[...]
        y = _conv_taps(cc_ref[...] * ch_ref[...], w_ref, row)
        y_ref[...] = ((cb_ref[...] * y) * (g * _sigmoid(g))).astype(BF16)

    return pl.pallas_call(
        body, name="conv_fwd", grid=(CONV_W // LANES,),
        in_specs=[_colblock(S, CBB), _colblock(S, CCB), _colblock(S, CHB), _colblock(S, CGB),
                  pl.BlockSpec((CONV_TAPS, LANES), lambda j: (0, j)), ANY],
        out_specs=_colblock(S, ATT_W // LANES),
        out_shape=jax.ShapeDtypeStruct(y.shape, y.dtype),
        input_output_aliases={5: 0},
        compiler_params=_params(("parallel",)),
    )(rest, rest, rest, rest, cw, y)


def _conv_bwd(rest, dy, cw, dproj):
    S = rest.shape[0]

    def body(cb_ref, cc_ref, ch_ref, g_ref, dy_ref, w_ref, dproj_in, dproj_ref, dw_ref,
             dcb_ref, dcc_ref, dch_ref, dg_ref, sems):
        row = lax.broadcasted_iota(jnp.int32, (S, LANES), 0)
        g = g_ref[...]
        sg = _sigmoid(g)
        silu = g * sg
        cb, cc, ch, dy = cb_ref[...], cc_ref[...], ch_ref[...], dy_ref[...]
        u = cc * ch
        u1 = _shift_down(u, 1, row)
        u2 = _shift_down(u, 2, row)
        y = (w_ref[0:1, :] * u2 + w_ref[1:2, :] * u1) + w_ref[2:3, :] * u
        dcb_ref[...] = (dy * silu * y).astype(BF16)
        dg_ref[...] = (dy * (cb * y) * (sg * (1.0 + g * (1.0 - sg)))).astype(BF16)
        dyv = dy * silu * cb
        du = w_ref[2:3, :] * dyv + w_ref[1:2, :] * _shift_up(dyv, 1, row) + w_ref[0:1, :] * _shift_up(dyv, 2, row)
        dcc_ref[...] = (du * ch).astype(BF16)
        dch_ref[...] = (du * cc).astype(BF16)
        dw_ref[0:1, :] = jnp.sum(dyv * u2, axis=0, keepdims=True)
        dw_ref[1:2, :] = jnp.sum(dyv * u1, axis=0, keepdims=True)
        dw_ref[2:3, :] = jnp.sum(dyv * u, axis=0, keepdims=True)
        j = pl.program_id(0)
        first = QKV_W // LANES
        _put_columns([dcb_ref, dcc_ref, dch_ref, dg_ref], dproj_ref,
                     [first + CBB + j, first + CCB + j, first + CHB + j, first + CGB + j], sems)

    piece = pltpu.VMEM((S, LANES), BF16)
    return pl.pallas_call(
        body, name="conv_bwd", grid=(CONV_W // LANES,),
        in_specs=[_colblock(S, CBB), _colblock(S, CCB), _colblock(S, CHB), _colblock(S, CGB),
                  _colblock(S, ATT_W // LANES), pl.BlockSpec((CONV_TAPS, LANES), lambda j: (0, j)), ANY],
        out_specs=(ANY, pl.BlockSpec((CONV_TAPS, LANES), lambda j: (0, j))),
        out_shape=(jax.ShapeDtypeStruct(dproj.shape, dproj.dtype), jax.ShapeDtypeStruct((CONV_TAPS, CONV_W), F32)),
        input_output_aliases={6: 0},
        scratch_shapes=[piece] * 4 + [pltpu.SemaphoreType.DMA((4,))],
        compiler_params=_params(("arbitrary",)),
    )(rest, rest, rest, rest, dy, cw, dproj)


def _pool_z(u, grp, row):
    s2 = u + _shift_down(u, 1, row)
    s4 = s2 + _shift_down(s2, 2, row)
    s8 = s4 + _shift_down(s4, 4, row)
    s16 = s8 + _shift_down(s8, 8, row)
    sw = jnp.where(grp == 0, s2, jnp.where(grp == 1, s4, jnp.where(grp == 2, s8, s16)))
    return sw / _pool_count(grp, row) - u


def _pool_count(grp, row):
    return jnp.minimum(row + 1, 2 << grp).astype(F32)


def _pool_fwd(rest, pw, scale, y):
    S = rest.shape[0]

    def body(u_ref, g_ref, pw_ref, sc_ref, y_in, y_ref):
        grp = pl.program_id(0)
        row = lax.broadcasted_iota(jnp.int32, (S, LANES), 0)
        z = _pool_z(u_ref[...], grp, row)
        zp = jnp.dot(z.astype(BF16), pw_ref[...].astype(BF16), preferred_element_type=F32)
        g = g_ref[...]
        y_ref[...] = ((zp * sc_ref[...]) * (g * _sigmoid(g))).astype(BF16)

    return pl.pallas_call(
        body, name="pool_fwd", grid=(N_POOL,),
        in_specs=[_colblock(S, PUB), _colblock(S, PGB),
                  pl.BlockSpec((None, POOL_GROUP, POOL_GROUP), lambda j: (j, 0, 0)),
                  pl.BlockSpec((1, LANES), lambda j: (0, j)), ANY],
        out_specs=_colblock(S, (ATT_W + CONV_W) // LANES),
        out_shape=jax.ShapeDtypeStruct(y.shape, y.dtype),
        input_output_aliases={4: 0},
        compiler_params=_params(("parallel",)),
    )(rest, rest, pw, scale, y)


def _pool_bwd(rest, dy, pw, scale, dproj):
    S = rest.shape[0]

    def body(u_ref, g_ref, dy_ref, pw_ref, sc_ref, dproj_in, dproj_ref, dpw_ref, dsc_ref, du_ref, dg_ref, sems):
        grp = pl.program_id(0)
        row = lax.broadcasted_iota(jnp.int32, (S, LANES), 0)
        z = _pool_z(u_ref[...], grp, row).astype(BF16)
        pwb = pw_ref[...].astype(BF16)
        zp = jnp.dot(z, pwb, preferred_element_type=F32)
        g = g_ref[...]
        sg = _sigmoid(g)
        silu = g * sg
        dy = dy_ref[...]
        sc = sc_ref[...]
        dsc_ref[...] = jnp.sum(dy * silu * zp, axis=0, keepdims=True)
        dg_ref[...] = (dy * (zp * sc) * (sg * (1.0 + g * (1.0 - sg)))).astype(BF16)
        dzp = (dy * silu * sc).astype(BF16)
        dpw_ref[...] = lax.dot_general(z, dzp, (((0,), (0,)), ((), ())), preferred_element_type=F32)
        dz = lax.dot_general(dzp, pwb, (((1,), (1,)), ((), ())), preferred_element_type=F32)
        f1 = dz / _pool_count(grp, row)
        f2 = f1 + _shift_up(f1, 1, row)
        f4 = f2 + _shift_up(f2, 2, row)
        f8 = f4 + _shift_up(f4, 4, row)
        f16 = f8 + _shift_up(f8, 8, row)
        fw = jnp.where(grp == 0, f2, jnp.where(grp == 1, f4, jnp.where(grp == 2, f8, f16)))
        du_ref[...] = (fw - dz).astype(BF16)
        first = QKV_W // LANES
        _put_columns([du_ref, dg_ref], dproj_ref, [first + PUB + grp, first + PGB + grp], sems)

    piece = pltpu.VMEM((S, LANES), BF16)
    return pl.pallas_call(
        body, name="pool_bwd", grid=(N_POOL,),
        in_specs=[_colblock(S, PUB), _colblock(S, PGB), _colblock(S, (ATT_W + CONV_W) // LANES),
                  pl.BlockSpec((None, POOL_GROUP, POOL_GROUP), lambda j: (j, 0, 0)),
                  pl.BlockSpec((1, LANES), lambda j: (0, j)), ANY],
        out_specs=(ANY, pl.BlockSpec((None, POOL_GROUP, POOL_GROUP), lambda j: (j, 0, 0)),
                   pl.BlockSpec((1, LANES), lambda j: (0, j))),
        out_shape=(jax.ShapeDtypeStruct(dproj.shape, dproj.dtype),
                   jax.ShapeDtypeStruct((N_POOL, POOL_GROUP, POOL_GROUP), F32), jax.ShapeDtypeStruct((1, POOL_W), F32)),
        input_output_aliases={5: 0},
        scratch_shapes=[piece] * 2 + [pltpu.SemaphoreType.DMA((2,))],
        compiler_params=_params(("arbitrary",)),
    )(rest, rest, dy, pw, scale, dproj)


def _ln_fwd(z, g, b, *, ts):
    S, D = z.shape

    def body(z_ref, g_ref, b_ref, o_ref, ob_ref, obt_ref, xh_ref, rs_ref):
        zz = z_ref[...]
        mu = jnp.mean(zz, axis=1, keepdims=True)
        zc = zz - mu
        rstd = lax.rsqrt(jnp.mean(zc * zc, axis=1, keepdims=True) + LN_EPS)
        xh = zc * rstd
        out = xh * g_ref[...] + b_ref[...]
        o_ref[...] = out
        ob_ref[...] = out.astype(BF16)
        obt_ref[...] = out.T.astype(BF16)
        xh_ref[...] = xh
        rs_ref[...] = rstd

    rowblk = pl.BlockSpec((ts, D), lambda i: (i, 0))
    vec = pl.BlockSpec((1, D), lambda i: (0, 0))
    return pl.pallas_call(
        body, name="ln_fwd", grid=(S // ts,),
        in_specs=[rowblk, vec, vec],
        out_specs=(rowblk, rowblk, pl.BlockSpec((D, ts), lambda i: (0, i)), rowblk,
                   pl.BlockSpec((ts, 1), lambda i: (i, 0))),
        out_shape=(jax.ShapeDtypeStruct((S, D), F32), jax.ShapeDtypeStruct((S, D), BF16),
                   jax.ShapeDtypeStruct((D, S), BF16), jax.ShapeDtypeStruct((S, D), F32),
                   jax.ShapeDtypeStruct((S, 1), F32)),
        compiler_params=_params(("parallel",)),
    )(z, g, b)


def _prep(x, after, *, ts):
    S, D = x.shape

    def body(x_ref, after_ref, xb_ref, xbt_ref):
        xb_ref[...] = x_ref[...].astype(BF16)
        xbt_ref[...] = x_ref[...].T.astype(BF16)

    return pl.pallas_call(
        body, name="prep", grid=(S // ts,),
        in_specs=[pl.BlockSpec((ts, D), lambda i: (i, 0)), ANY],
        out_specs=(pl.BlockSpec((ts, D), lambda i: (i, 0)), pl.BlockSpec((D, ts), lambda i: (0, i))),
        out_shape=(jax.ShapeDtypeStruct((S, D), BF16), jax.ShapeDtypeStruct((D, S), BF16)),
        compiler_params=_params(("parallel",)),
    )(x, after)


def _ln_bwd(dout, xhat, rstd, g, *, ts):
    S, D = dout.shape

    def body(d_ref, xh_ref, rs_ref, g_ref, dz_ref, dzb_ref, dg_ref, db_ref):
        @pl.when(pl.program_id(0) == 0)
        def _():
            dg_ref[...] = jnp.zeros_like(dg_ref)
            db_ref[...] = jnp.zeros_like(db_ref)

        d = d_ref[...]
        xh = xh_ref[...]
        dxh = d * g_ref[...]
        m1 = jnp.mean(dxh, axis=1, keepdims=True)
        m2 = jnp.mean(dxh * xh, axis=1, keepdims=True)
        dz = rs_ref[...] * (dxh - m1 - xh * m2)
        dz_ref[...] = dz
        dzb_ref[...] = dz.astype(BF16)
        dg_ref[...] += jnp.sum(d * xh, axis=0, keepdims=True)
        db_ref[...] += jnp.sum(d, axis=0, keepdims=True)

    rowblk = pl.BlockSpec((ts, D), lambda i: (i, 0))
    vec = pl.BlockSpec((1, D), lambda i: (0, 0))
    return pl.pallas_call(
        body, name="ln_bwd", grid=(S // ts,),
        in_specs=[rowblk, rowblk, pl.BlockSpec((ts, 1), lambda i: (i, 0)), vec],
        out_specs=(rowblk, rowblk, vec, vec),
        out_shape=(jax.ShapeDtypeStruct((S, D), F32), jax.ShapeDtypeStruct((S, D), BF16),
                   jax.ShapeDtypeStruct((1, D), F32), jax.ShapeDtypeStruct((1, D), F32)),
        compiler_params=_params(("arbitrary",)),
    )(dout, xhat, rstd, g)


def _loss_head(y, tgt, *, ts):
    S, D = y.shape

    def body(y_ref, t_ref, l_ref, dy_ref):
        @pl.when(pl.program_id(0) == 0)
        def _():
            l_ref[...] = jnp.zeros_like(l_ref)

        e = y_ref[...] - t_ref[...]
        dy_ref[...] = e * (1.0 / D)
        rowloss = jnp.sum(e * e, axis=1, keepdims=True) * (0.5 / D)
        l_ref[...] += jnp.sum(rowloss, axis=0, keepdims=True)

    rowblk = pl.BlockSpec((ts, D), lambda i: (i, 0))
    return pl.pallas_call(
        body, name="loss_head", grid=(S // ts,),
        in_specs=[rowblk, rowblk],
        out_specs=(pl.BlockSpec((1, 1), lambda i: (0, 0)), rowblk),
        out_shape=(jax.ShapeDtypeStruct((1, 1), F32), jax.ShapeDtypeStruct((S, D), F32)),
        compiler_params=_params(("arbitrary",)),
    )(y, tgt)


def _adamw(lands, srcs, w, m, v, *, name, first=0, into=None):
    R, C = w.shape
    npart = len(lands)
    rp = lands[0].shape[1]
    assert R % rp == 0 and all(p.shape == (N_DEV, rp, C) for p in lands)
    tiled = rp % 8 == 0
    assert tiled or (first == 0 and npart * rp == R and into is None)
    tr = min(ADAMW_BLOCK_ELEMS // (LANES * pl.cdiv(C, LANES)), rp) if tiled else rp
    assert rp % tr == 0
    steps = rp // tr
    c1 = 1.0 - ADAM_B1 ** ADAM_STEP
    c2 = 1.0 - ADAM_B2 ** ADAM_STEP
    slabbed = [p.ndim == 3 for p in srcs]
    n_into = 0 if into is None else 4

    def body(*refs):
        l_refs, s_refs = refs[:npart], refs[npart:2 * npart]
        w_ref, m_ref, v_ref = refs[2 * npart:2 * npart + 3]
        g_ref, d_ref, nm_ref, nv_ref = refs[2 * npart + 3 + n_into:]
        x, y, c = _my_place()
        me = 4 * x + 2 * y + c

        def update(l, rows):
            g = None
            for j in range(N_DEV):
                own = s_refs[l][j] if slabbed[l] else s_refs[l][...]
                p = jnp.where(me == j, own, l_refs[l][j]).astype(F32)
                g = p if g is None else g + p
            nm = ADAM_B1 * m_ref[rows, :] + (1.0 - ADAM_B1) * g
            nv = ADAM_B2 * v_ref[rows, :] + (1.0 - ADAM_B2) * (g * g)
            g_ref[rows, :] = g
            nm_ref[rows, :] = nm
            nv_ref[rows, :] = nv
            d_ref[rows, :] = -ADAM_LR * ((nm / c1) / (jnp.sqrt(nv / c2) + ADAM_EPS) + ADAM_WD * w_ref[rows, :])

        for l in range(npart):
            if tiled:
                pl.when(pl.program_id(0) == l)(lambda l=l: update(l, slice(None)))
            else:
                update(l, slice(l * rp, (l + 1) * rp))

    def part_spec(l, p):
        if p.ndim == 3:
            return pl.BlockSpec((N_DEV, tr, C), lambda q, i: (0, jnp.where(q == l, i, 0), 0))
        return pl.BlockSpec((tr, C), lambda q, i: (jnp.where(q == l, i, 0), 0))

    if tiled:
        blk = pl.BlockSpec((tr, C), lambda q, i: ((first + q) * steps + i, 0))
    else:
        blk = pl.BlockSpec((R, C), lambda q, i: (0, 0))
    out = jax.ShapeDtypeStruct((R, C), F32)
    n_in = 2 * npart + 3
    return pl.pallas_call(
        body, name=name, grid=(npart, steps) if tiled else (1, 1),
        in_specs=[part_spec(l, p) for l, p in enumerate(lands)] + [part_spec(l, p) for l, p in enumerate(srcs)]
        + [blk, blk, blk] + [ANY] * n_into,
        out_specs=(blk, blk, blk, blk),
        out_shape=(out, out, out, out),
        input_output_aliases={n_in + i: i for i in range(n_into)},
        compiler_params=_params(("arbitrary", "arbitrary")),
    )(*lands, *srcs, w, m, v, *(into or ()))


def _my_place():
    return lax.axis_index("x"), lax.axis_index("y"), lax.axis_index("c")


HBM = pl.BlockSpec(memory_space=pltpu.HBM)
SEM = pl.BlockSpec(memory_space=pltpu.SEMAPHORE)
DATAFLOW = pltpu.SideEffectType.DATAFLOW_SIDE_EFFECTING
EVERYONE = (1, 2, 3, 4, 5, 6, 7)
SIBLING = 1
SAME_CORE = (4, 2, 6)


def _xchg_copies(src, land, send_sem, recv_sem, gather, peers):
    x, y, c = _my_place()
    me = 4 * x + 2 * y + c
    copies = []
    for k, bits in enumerate(peers):
        px, py, pc = x ^ ((bits >> 2) & 1), y ^ ((bits >> 1) & 1), c ^ (bits & 1)
        peer = 4 * px + 2 * py + pc
        for a in range(len(src)):
            i = a * len(peers) + k
            out = src[a] if gather[a] else src[a].at[peer]
            sent, landed = [pltpu.make_async_remote_copy(
                src_ref=out, dst_ref=land[a].at[slab], send_sem=send_sem.at[i], recv_sem=recv_sem.at[i],
                device_id=(px, py, pc), device_id_type=MESH) for slab in (me, peer)]
            copies.append((sent, landed))
    return copies


def _xchg_start(srcs, *, gather, name, peers=EVERYONE, after=None):
    n = len(srcs)
    follows = [] if after is None else [after]
    lands = [lax.empty(((N_DEV,) + v.shape) if g else v.shape, v.dtype) for v, g in zip(srcs, gather)]

    def body(*refs):
        src, land = refs[:n], refs[n:2 * n]
        send_sem, recv_sem = refs[2 * n + len(follows)], refs[2 * n + len(follows) + 1]
        token = refs[-1]
        for sent, _ in _xchg_copies(src, land, send_sem, recv_sem, gather, peers):
            sent.start()
        token[...] = jnp.zeros_like(token)

    sems = pltpu.SemaphoreType.DMA((n * len(peers),))
    outs = pl.pallas_call(
        body, name=name,
        out_shape=(sems, sems, *[pltpu.HBM(v.shape, v.dtype) for v in srcs + lands],
                   jax.ShapeDtypeStruct((8, LANES), F32)),
        in_specs=[HBM] * (2 * n) + [ANY] * len(follows),
        out_specs=(SEM, SEM, *[HBM] * (2 * n), pl.BlockSpec(memory_space=pltpu.VMEM)),
        input_output_aliases={i: 2 + i for i in range(2 * n)},
        compiler_params=pltpu.CompilerParams(has_side_effects=DATAFLOW),
    )(*[pltpu.with_memory_space_constraint(v, pltpu.HBM) for v in srcs + lands], *follows)
    return dict(send=outs[0], recv=outs[1], srcs=list(outs[2:2 + n]), lands=list(outs[2 + n:2 + 2 * n]),
                token=outs[-1], gather=gather, peers=peers)


def _xchg_wait(h, after, *, name):
    n = len(h["srcs"])
    gather, peers = h["gather"], h["peers"]
    after = list(after) if isinstance(after, (list, tuple)) else [after]

    def body(*refs):
        src, land = refs[:n], refs[n:2 * n]
        send_sem, recv_sem = refs[2 * n], refs[2 * n + 1]
        for sent, landed in _xchg_copies(src, land, send_sem, recv_sem, gather, peers):
            sent.wait_send()
            landed.wait_recv()

    thru = h["srcs"] + h["lands"]
    outs = pl.pallas_call(
        body, name=name,
        out_shape=[pltpu.HBM(v.shape, v.dtype) for v in thru],
        in_specs=[HBM] * (2 * n) + [SEM, SEM] + [ANY] * len(after),
        out_specs=[HBM] * (2 * n),
        input_output_aliases={i: i for i in range(2 * n)},
        compiler_params=pltpu.CompilerParams(has_side_effects=DATAFLOW),
    )(*thru, h["send"], h["recv"], *after)
    return list(outs[:n]), list(outs[n:])


def _relay_copies(land, send_sem, recv_sem):
    x, y, c = _my_place()
    me = 4 * x + 2 * y + c
    copies = []
    for k, bits in enumerate(SAME_CORE):
        for a in range(len(land)):
            i = a * len(SAME_CORE) + k
            sent, landed = [pltpu.make_async_remote_copy(
                src_ref=land[a].at[me ^ bits], dst_ref=land[a].at[slab], send_sem=send_sem.at[i], recv_sem=recv_sem.at[i],
                device_id=(x, y, 1 - c), device_id_type=MESH) for slab in (me ^ bits, me ^ bits ^ SIBLING)]
            copies.append((sent, landed))
    return copies


def _relay_start(lands, *, name):
    n = len(lands)

    def body(*refs):
        land = refs[:n]
        send_sem, recv_sem = refs[n], refs[n + 1]
        token = refs[-1]
        for sent, _ in _relay_copies(land, send_sem, recv_sem):
            sent.start()
        token[...] = jnp.zeros_like(token)

    sems = pltpu.SemaphoreType.DMA((n * len(SAME_CORE),))
    outs = pl.pallas_call(
        body, name=name,
        out_shape=(sems, sems, *[pltpu.HBM(v.shape, v.dtype) for v in lands], jax.ShapeDtypeStruct((8, LANES), F32)),
        in_specs=[HBM] * n,
        out_specs=(SEM, SEM, *[HBM] * n, pl.BlockSpec(memory_space=pltpu.VMEM)),
        input_output_aliases={i: 2 + i for i in range(n)},
        compiler_params=pltpu.CompilerParams(has_side_effects=DATAFLOW),
    )(*[pltpu.with_memory_space_constraint(v, pltpu.HBM) for v in lands])
    return dict(send=outs[0], recv=outs[1], lands=list(outs[2:2 + n]), token=outs[-1])


def _relay_wait(h, after, *, name):
    n = len(h["lands"])

    def body(*refs):
        land = refs[:n]
        send_sem, recv_sem = refs[n], refs[n + 1]
        for sent, landed in _relay_copies(land, send_sem, recv_sem):
            sent.wait_send()
            landed.wait_recv()

    outs = pl.pallas_call(
        body, name=name,
        out_shape=[pltpu.HBM(v.shape, v.dtype) for v in h["lands"]],
        in_specs=[HBM] * n + [SEM, SEM, ANY],
        out_specs=[HBM] * n,
        input_output_aliases={i: i for i in range(n)},
        compiler_params=pltpu.CompilerParams(has_side_effects=DATAFLOW),
    )(*h["lands"], h["send"], h["recv"], after)
    return list(outs)


def _with_own(land, own, me):
    slot = lax.broadcasted_iota(jnp.int32, land.shape, 0)
    return jnp.where(slot == me, own if own.ndim == land.ndim else own[None], land)


def _w_in_segments(j, cols):
    lo, hi = j * cols, (j + 1) * cols
    segs = []
    for a, b, where, shift in ((0, FG_AT, "main", 0), (FG_AT, FG_AT + N_HEADS, "fg", -FG_AT),
                               (FG_AT + N_HEADS, IN_W, "main", -N_HEADS)):
        s0, s1 = max(lo, a), min(hi, b)
        if s0 < s1:
            segs.append((s0 - lo, s1 - s0, where, s0 + shift))
    return segs


SLAB_W = 1024


def _window(off, width):
    base = min((off // LANES) * LANES, width - SLAB_W)
    return base, off - base


def _w_unpack(land, own, *, tr):
    _, D, cols = land.shape
    assert cols + LANES - 1 <= SLAB_W

    def body(land_ref, own_ref, main_ref, slab):
        x, y, c = _my_place()
        me = 4 * x + 2 * y + c
        lane = lax.broadcasted_iota(jnp.int32, (tr, SLAB_W), 1)
        main_ref[...] = jnp.zeros_like(main_ref)
        slab[...] = jnp.zeros_like(slab)
        for j in range(N_DEV):
            slab[:, :cols] = jnp.where(me == j, own_ref[...], land_ref[j])
            v = slab[...]
            segs = _w_in_segments(j, cols)
            for lo, n, where, dst in segs:
                part = v if len(segs) == 1 else jnp.where((lane >= lo) & (lane < lo + n), v, jnp.zeros_like(v))
                if where == "fg":
                    main_ref[:, MAIN_W:] = pltpu.roll(part, (SLAB_W - lo) % SLAB_W, 1)[:, :LANES]
                else:
                    base, r = _window(dst - lo, MAIN_W)
                    assert 0 <= r and lo + n + r <= SLAB_W
                    main_ref[:, base:base + SLAB_W] += pltpu.roll(part, r, 1)

    return pl.pallas_call(
        body, name="w_unpack", grid=(D // tr,),
        in_specs=[pl.BlockSpec((N_DEV, tr, cols), lambda i: (0, i, 0)), pl.BlockSpec((tr, cols), lambda i: (i, 0))],
        out_specs=pl.BlockSpec((tr, ALL_W), lambda i: (i, 0)),
        out_shape=jax.ShapeDtypeStruct((D, ALL_W), BF16),
        scratch_shapes=[pltpu.VMEM((tr, SLAB_W), BF16)],
        compiler_params=_params(("parallel",)),
    )(land, own)


def _g_pack(dw_main, dw_fg, cols, *, tr):
    D = dw_main.shape[0]

    def body(main_ref, fg_ref, out_ref, slab):
        lane = lax.broadcasted_iota(jnp.int32, (tr, SLAB_W), 1)
        slab[...] = jnp.zeros_like(slab)
        slab[:, :LANES] = fg_ref[...]
        fg = slab[...]
        for j in range(N_DEV):
            v = None
            for lo, n, where, src in _w_in_segments(j, cols):
                if where == "fg":
                    part = pltpu.roll(fg, lo, 1)
                else:
                    base, r = _window(src - lo, MAIN_W)
                    assert 0 <= r and lo + n + r <= SLAB_W
                    part = pltpu.roll(main_ref[:, base:base + SLAB_W], (SLAB_W - r) % SLAB_W, 1)
                v = part if v is None else jnp.where((lane >= lo) & (lane < lo + n), part, v)
            out_ref[j] = v[:, :cols]

    return pl.pallas_call(
        body, name="g_pack", grid=(D // tr,),
        in_specs=[pl.BlockSpec((tr, MAIN_W), lambda i: (i, 0)), pl.BlockSpec((tr, LANES), lambda i: (i, 0))],
        out_specs=pl.BlockSpec((N_DEV, tr, cols), lambda i: (0, i, 0)),
        out_shape=jax.ShapeDtypeStruct((N_DEV, D, cols), BF16),
        scratch_shapes=[pltpu.VMEM((tr, SLAB_W), BF16)],
        compiler_params=_params(("parallel",)),
    )(dw_main, dw_fg)


TQ = 1024
TS = 256
W_TILE_ROWS = 128


def _layer_fwd_proj(xb, w_in):
    qkv = _mm(xb, w_in, out_dtype=BF16, tm=2048, tn=512, tk=D_MODEL, name="mm_qkv", n_cols=QKV_W)
    rest = _mm(xb, w_in, out_dtype=F32, tm=2048, tn=512, tk=D_MODEL, name="mm_rest", n_off=QKV_W, n_cols=REST_W)
    fgp = _mm(xb, w_in, out_dtype=F32, tm=1024, tn=LANES, tk=D_MODEL, name="mm_fg", n_off=MAIN_W, n_cols=LANES)
    return qkv, rest, fgp


def _layer_fwd_mix(x, xbt, proj, wl):
    qkv, rest, fgp = proj
    cum = _fg_fwd(fgp, wl["b_f"])
    S = x.shape[0]
    cumr = (cum[:, :N_HEADS].T * LOG2E).reshape(N_HEADS, S // TQ, 1, TQ)
    o, y, lse = _attn_fwd(qkv, rest, cum, cumr, tq=TQ)
    y = _conv_fwd(rest, wl["conv_w"], y)
    y = _pool_fwd(rest, wl["pool_w"], wl["pool_scale"], y)
    z = _mm(y, wl["w_out"], out_dtype=F32, tm=2048, tn=512, tk=D_MODEL, name="mm_out", c=x, c_scale=ALPHA)
    out, outb, outbt, xhat, rstd = _ln_fwd(z, wl["ln_g"], wl["ln_b"], ts=TS)
    saved = dict(xbt=xbt, qkv=qkv, rest=rest, fgp=fgp, cum=cum, cumr=cumr, o=o, lse=lse, y=y, xhat=xhat, rstd=rstd)
    return out, outb, outbt, saved


def _layer_bwd_mix(dout, sv, wl):
    S = dout.shape[0]
    dz, dzb, dln_g, dln_b = _ln_bwd(dout, sv["xhat"], sv["rstd"], wl["ln_g"], ts=TS)
    dy = _mm(dzb, wl["w_out"], out_dtype=F32, tm=2048, tn=512, tk=D_MODEL, name="mm_dy", bt=True)
    dw_out = _mm(sv["y"], dzb, out_dtype=BF16, tm=512, tn=D_MODEL, tk=2048, name="mm_dw_out", at=True)
    rest = sv["rest"]
    do, dproj, delta = _attn_bwd_pre(rest, sv["o"], dy, tq=TQ)
    a2 = sv["cumr"] - sv["lse"] * LOG2E
    dproj, ds_q, ds_k = _attn_bwd(sv["qkv"], do, a2, delta, sv["cum"], dproj, tq=TQ)
    dcum = jnp.pad(ds_q.reshape(N_HEADS, S).T, ((0, 0), (0, LANES - N_HEADS))) - ds_k
    dproj, db_f = _fg_bwd(dcum, sv["fgp"], wl["b_f"], dproj)
    dproj, dconv_w = _conv_bwd(rest, dy, wl["conv_w"], dproj)
    dproj, dpool_w, dpool_scale = _pool_bwd(rest, dy, wl["pool_w"], wl["pool_scale"], dproj)
    grads = dict(b_f=db_f[:, :N_HEADS], conv_w=dconv_w, pool_w=dpool_w, pool_scale=dpool_scale, w_out=dw_out,
                 ln_g=dln_g, ln_b=dln_b)
    return grads, (dproj, dz)


def _layer_bwd_w_in(mid, sv):
    dproj, dz = mid
    xt = sv["xbt"]
    S = xt.shape[1]
    dw_main = _mm(xt, dproj, out_dtype=BF16, tm=1024, tn=512, tk=S, name="mm_dw_main", n_cols=MAIN_W)
    dw_fg = _mm(xt, dproj, out_dtype=BF16, tm=1024, tn=LANES, tk=2048, name="mm_dw_fg", n_off=MAIN_W, n_cols=LANES)
    return (dw_main, dw_fg), (dproj, dz)


def _layer_bwd_input(ctx, wl, after):
    dproj, dz = ctx
    return _mm(dproj, wl["w_in"], out_dtype=F32, tm=1024, tn=512, tk=ALL_W, name="mm_dx", c=dz, c_scale=ALPHA, bt=True,
               after=after)


def kernel(x, w_in, b_f, conv_w, pool_w, pool_scale, w_out, ln_g, ln_b, loss_target, m_w_in, m_b_f, m_conv_w, m_pool_w, m_pool_scale, m_w_out, m_ln_g, m_ln_b, v_w_in, v_b_f, v_conv_w, v_pool_w, v_pool_scale, v_w_out, v_ln_g, v_ln_b):
    L, D, cols = w_in.shape
    rows_out = w_out.shape[1]
    ccols = conv_w.shape[2]
    mx, my, mc = _my_place()
    me = 4 * mx + 2 * my + mc

    gathers = []
    for l in range(L):
        gathers.append(_xchg_start([w_in[l].astype(BF16), w_out[l].astype(BF16), conv_w[l]], gather=[True] * 3,
                                   name=f"gather_start_{l}",
                                   peers=(SIBLING,) + SAME_CORE if l == 0 else EVERYONE,
                                   after=gathers[-1]["token"] if gathers else None))
    started = gathers[-1]["token"]

    xl = x[0]
    xb, xbt = _prep(xl, started, ts=TS)
    weights, saved = [], []
    for l in range(L):
        (own_in, own_out, own_cw), (g_in, g_out, g_cw) = _xchg_wait(gathers[l], xb, name=f"gather_wait_{l}")
        if l == 0:
            g_in, g_out, g_cw = _relay_wait(_relay_start([g_in, g_out, g_cw], name="gather_relay_start"), xb,
                                            name="gather_relay_wait")
        w_all = _w_unpack(g_in, own_in, tr=W_TILE_ROWS)
        proj = _layer_fwd_proj(xb, w_all)
        w_o = _with_own(g_out, own_out, me).reshape(N_DEV * rows_out, D)
        cw = jnp.transpose(_with_own(g_cw, own_cw, me), (1, 0, 2)).reshape(CONV_TAPS, N_DEV * ccols)
        wl = dict(w_in=w_all, w_out=w_o, conv_w=cw,
                  b_f=jnp.pad(b_f[l][None, :], ((0, 0), (0, LANES - N_HEADS))), pool_w=pool_w[l],
                  pool_scale=pool_scale[l][None, :], ln_g=ln_g[l][None, :], ln_b=ln_b[l][None, :])
        xl, xb, xbt, sv = _layer_fwd_mix(xl, xbt, proj, wl)
        weights.append(wl)
        saved.append(sv)

    loss, dx = _loss_head(xl, loss_target[0], ts=TS)
    loss = lax.psum(loss[0, 0], ("x", "y", "c"))

    names = ["w_in", "w_out", "conv_w", "b_f", "pool_w", "pool_scale", "ln_g", "ln_b"]
    sharded = names[:3]
    scatters = [None] * L
    for l in reversed(range(L)):
        g, mid = _layer_bwd_mix(dx, saved[l], weights[l])
        (dw_main, dw_fg), ctx = _layer_bwd_w_in(mid, saved[l])
        g["w_in"] = _g_pack(dw_main, dw_fg, cols, tr=W_TILE_ROWS)
        g["w_out"] = g["w_out"].reshape(N_DEV, rows_out, D)
        g["conv_w"] = jnp.transpose(g["conv_w"].reshape(CONV_TAPS, N_DEV, ccols), (1, 0, 2))
        scatters[l] = _xchg_start([g[k] for k in names], gather=[k not in sharded for k in names],
                                  name=f"scatter_start_{l}")
        dx = _layer_bwd_input(ctx, weights[l], scatters[l]["token"])

    given = dict(w_in=(w_in, m_w_in, v_w_in), b_f=(b_f, m_b_f, v_b_f), conv_w=(conv_w, m_conv_w, v_conv_w),
                 pool_w=(pool_w, m_pool_w, v_pool_w), pool_scale=(pool_scale, m_pool_scale, v_pool_scale),
                 w_out=(w_out, m_w_out, v_w_out), ln_g=(ln_g, m_ln_g, v_ln_g), ln_b=(ln_b, m_ln_b, v_ln_b))
    sent, landed = [{} for _ in range(L)], [{} for _ in range(L)]

    def wait(l, after):
        s_, l_ = _xchg_wait(scatters[l], after, name=f"scatter_wait_{l}")
        sent[l].update(zip(names, s_))
        landed[l].update(zip(names, l_))

    last_start = scatters[0]["token"][0, 0]

    def update(k, layers, into=None):
        w, m, v = given[k]
        C = w.shape[-1]
        if k == "w_in":
            m, v = m + last_start, v + last_start
        w2, m2, v2 = (t.reshape(-1, C) for t in (w, m, v))
        rp = w2.shape[0] // L
        lands = [landed[l][k].reshape(N_DEV, rp, C) for l in layers]
        srcs = [sent[l][k].reshape((N_DEV, rp, C) if k in sharded else (rp, C)) for l in layers]
        return _adamw(lands, srcs, w2, m2, v2, name=f"adamw_{k}_{layers[0]}", first=layers[0], into=into)

    big = ["w_in", "w_out", "pool_w"]
    for l in reversed(range(1, L)):
        wait(l, dx)
    upper = {k: update(k, list(range(1, L))) for k in big} if L > 1 else {}
    wait(0, [upper[k][0] for k in big] if upper else dx)
    res = {k: update(k, [0], into=upper[k]) if k in upper else update(k, list(range(L))) for k in names}
    res = {k: [t.reshape(given[k][0].shape) for t in outs] for k, outs in res.items()}

    order = ["w_in", "b_f", "conv_w", "pool_w", "pool_scale", "w_out", "ln_g", "ln_b"]
    return (loss, dx[None], *[res[k][0] for k in order], *[res[k][1] for k in order],
            *[res[k][2] for k in order], *[res[k][3] for k in order])
```

```python
import jax
import jax.numpy as jnp
from jax import lax
from jax.experimental import pallas as pl
from jax.experimental.pallas import tpu as pltpu

F32 = jnp.float32
BF16 = jnp.bfloat16

N_DEV = 8
D_MODEL = 2048
N_HEADS = 8
HEAD_DIM = 128
ATT_W = N_HEADS * HEAD_DIM
CONV_W = 512
CONV_TAPS = 3
POOL_W = 512
POOL_GROUP = 128
N_POOL = POOL_W // POOL_GROUP
FG_AT = 4 * ATT_W
IN_W = FG_AT + N_HEADS + 4 * CONV_W + 2 * POOL_W
MAIN_W = IN_W - N_HEADS
LANES = 128
ALL_W = MAIN_W + LANES
LN_EPS = 1e-5
DEPTH = 4
ALPHA = (2 * DEPTH) ** 0.25
SCALE = HEAD_DIM ** -0.5
LOG2E = 1.4426950408889634
LN2 = 0.6931471805599453
NEG = -1e30

ADAM_LR, ADAM_B1, ADAM_B2, ADAM_EPS, ADAM_WD, ADAM_STEP = 0.001, 0.9, 0.999, 1e-08, 0.01, 10

QKV_W = 3 * ATT_W
REST_W = MAIN_W - QKV_W
QB, KB, VB = 0, 8, 16
GB = 0
CBB, CCB, CHB, CGB = 8, 12, 16, 20
PUB, PGB = 24, 28
FGB = MAIN_W // LANES
VMEM_LIMIT = 48 * 1024 * 1024
ADAMW_BLOCK_ELEMS = 64 * 1024

MESH = pl.DeviceIdType.MESH
ANY = pl.BlockSpec(memory_space=pl.ANY)


def _params(semantics):
    return pltpu.CompilerParams(dimension_semantics=semantics, vmem_limit_bytes=VMEM_LIMIT)


def _sigmoid(x):
    return 1.0 / (1.0 + jnp.exp(-x))


def _mm(a, b, *, out_dtype, tm, tn, tk, name, c=None, c_scale=1.0, n_off=0, n_cols=None, bt=False, at=False,
        after=None):
    M, K = a.shape[::-1] if at else a.shape
    N = b.shape[0 if bt else 1] if n_cols is None else n_cols
    assert not (bt and n_off)
    dims = (((0 if at else 1,), (1 if bt else 0,)), ((), ()))
    tm, tn, tk = min(tm, M), min(tn, N), min(tk, K)
    assert M % tm == 0 and N % tn == 0 and K % tk == 0 and n_off % tn == 0, (a.shape, b.shape, tm, tn, tk)
    nk, joff = K // tk, n_off // tn
    has_c, has_after = c is not None, after is not None

    def body(*refs):
        a_ref, b_ref = refs[0], refs[1]
        c_ref = refs[2] if has_c else None
        o_ref = refs[2 + has_c + has_after]

        def finish(r):
            if has_c:
                r = r + c_scale * c_ref[...]
            o_ref[...] = r.astype(out_dtype)

        if nk == 1:
            finish(lax.dot_general(a_ref[...], b_ref[...], dims, preferred_element_type=F32))
        else:
            acc_ref = refs[3 + has_c + has_after]
            k = pl.program_id(2)

            @pl.when(k == 0)
            def _():
                acc_ref[...] = jnp.zeros_like(acc_ref)

            acc_ref[...] += lax.dot_general(a_ref[...], b_ref[...], dims, preferred_element_type=F32)

            @pl.when(k == nk - 1)
            def _():
                finish(acc_ref[...])

    in_specs = [pl.BlockSpec((tk, tm), lambda i, j, k: (k, i)) if at else pl.BlockSpec((tm, tk), lambda i, j, k: (i, k)),
                pl.BlockSpec((tn, tk), lambda i, j, k: (j, k)) if bt
                else pl.BlockSpec((tk, tn), lambda i, j, k: (k, j + joff))]
    args = [a, b]
    if has_c:
        in_specs.append(pl.BlockSpec((tm, tn), lambda i, j, k: (i, j)))
        args.append(c)
    if has_after:
        in_specs.append(ANY)
        args.append(after)
    return pl.pallas_call(
        body, name=name, grid=(M // tm, N // tn, nk),
        in_specs=in_specs,
        out_specs=pl.BlockSpec((tm, tn), lambda i, j, k: (i, j)),
        out_shape=jax.ShapeDtypeStruct((M, N), out_dtype),
        scratch_shapes=[pltpu.VMEM((tm, tn), F32)] if nk > 1 else [],
        compiler_params=_params(("parallel", "parallel", "arbitrary")),
    )(*args)


def _shift_down(x, k, row):
    return jnp.where(row >= k, pltpu.roll(x, k, 0), 0.0)


def _shift_up(x, k, row):
    s = x.shape[0]
    return jnp.where(row < s - k, pltpu.roll(x, s - k, 0), 0.0)


def _fg_fwd(fgp, bf):
    S = fgp.shape[0]

    def body(f_ref, b_ref, cum_ref):
        a = f_ref[...] + b_ref[...]
        x = jnp.minimum(a, 0.0) - jnp.log(1.0 + jnp.exp(-jnp.abs(a)))
        row = lax.broadcasted_iota(jnp.int32, x.shape, 0)
        k = 1
        while k < S:
            x = x + _shift_down(x, k, row)
            k *= 2
        cum_ref[...] = x

    return pl.pallas_call(body, name="fg_fwd", out_shape=jax.ShapeDtypeStruct((S, LANES), F32),
                          compiler_params=_params(None))(fgp, bf)


def _fg_bwd(dcum, fgp, bf, dproj):
    S = fgp.shape[0]

    def body(d_ref, f_ref, b_ref, dproj_in, dfg_ref, dbf_ref):
        x = d_ref[...]
        row = lax.broadcasted_iota(jnp.int32, x.shape, 0)
        k = 1
        while k < S:
            x = x + _shift_up(x, k, row)
            k *= 2
        dfg = x * _sigmoid(-(f_ref[...] + b_ref[...]))
        dfg_ref[...] = dfg.astype(BF16)
        dbf_ref[...] = jnp.sum(dfg, axis=0, keepdims=True)

    whole = pl.BlockSpec((S, LANES), lambda i: (0, 0))
    vec = pl.BlockSpec((1, LANES), lambda i: (0, 0))
    return pl.pallas_call(
        body, name="fg_bwd", grid=(1,),
        in_specs=[whole, whole, vec, ANY],
        out_specs=(pl.BlockSpec((S, LANES), lambda i: (0, FGB)), vec),
        out_shape=(jax.ShapeDtypeStruct(dproj.shape, dproj.dtype), jax.ShapeDtypeStruct((1, LANES), F32)),
        input_output_aliases={3: 0},
        compiler_params=_params(("arbitrary",)),
    )(dcum, fgp, bf, dproj)


def _put_columns(pieces, dst, blocks, sems):
    copies = [pltpu.make_async_copy(p, dst.at[:, pl.ds(pl.multiple_of(b * LANES, LANES), LANES)], sems.at[i])
              for i, (p, b) in enumerate(zip(pieces, blocks))]
    for cp in copies:
        cp.start()
    for cp in copies:
        cp.wait()


def _colblock(S, off):
    return pl.BlockSpec((S, LANES), lambda h: (0, h + off))


def _head_row(nq, tq):
    return pl.BlockSpec((None, nq, 1, tq), lambda h: (h, 0, 0, 0))


def _lane_of(ref, h):
    lane = lax.broadcasted_iota(jnp.int32, ref.shape, 1)
    return jnp.sum(jnp.where(lane == h, ref[...], 0.0), axis=1, keepdims=True)


def _attn_fwd(qkv, rest, cum, cumr, *, tq):
    S = qkv.shape[0]
    nq = S // tq

    def body(q_ref, k_ref, v_ref, g_ref, cum_ref, cumr_ref, o_ref, y_ref, lse_ref, cc):
        cc[...] = _lane_of(cum_ref, pl.program_id(0)) * LOG2E
        tri = (lax.broadcasted_iota(jnp.int32, (tq, tq), 0) <= lax.broadcasted_iota(jnp.int32, (tq, tq), 1))

        def q_step(qi, _):
            rows = pl.ds(pl.multiple_of(qi * tq, tq), tq)
            q = q_ref[rows, :]
            ci = cumr_ref[qi]

            def tile(kj, carry, masked):
                m, l, acc = carry
                cols = pl.ds(pl.multiple_of(kj * tq, tq), tq)
                s = lax.dot_general(k_ref[cols, :], q, (((1,), (1,)), ((), ())), preferred_element_type=F32)
                s = s * (SCALE * LOG2E) + ci - cc[cols, :]
                if masked:
                    s = jnp.where(tri, s, NEG)
                m_new = jnp.maximum(m, jnp.max(s, axis=0, keepdims=True))
                a = jnp.exp2(m - m_new)
                p = jnp.exp2(s - m_new)
                l = a * l + jnp.sum(p, axis=0, keepdims=True)
                acc = a * acc + lax.dot_general(v_ref[cols, :], p.astype(BF16), (((0,), (0,)), ((), ())),
                                                preferred_element_type=F32)
                return m_new, l, acc

            init = (jnp.full((1, tq), NEG, F32), jnp.zeros((1, tq), F32), jnp.zeros((HEAD_DIM, tq), F32))
            carry = lax.fori_loop(0, qi, lambda kj, c: tile(kj, c, False), init)
            m, l, acc = tile(qi, carry, True)
            o = (acc / l).T
            g = g_ref[rows, :]
            o_ref[rows, :] = o
            y_ref[rows, :] = (o * (g * _sigmoid(g))).astype(BF16)
            lse_ref[qi] = m * LN2 + jnp.log(l)
            return 0

        lax.fori_loop(0, nq, q_step, 0)

    return pl.pallas_call(
        body, name="attn_fwd", grid=(N_HEADS,),
        in_specs=[_colblock(S, QB), _colblock(S, KB), _colblock(S, VB), _colblock(S, GB),
                  pl.BlockSpec((S, LANES), lambda h: (0, 0)), _head_row(nq, tq)],
        out_specs=(_colblock(S, 0), _colblock(S, 0), _head_row(nq, tq)),
        out_shape=(jax.ShapeDtypeStruct((S, ATT_W), F32), jax.ShapeDtypeStruct((S, D_MODEL), BF16),
                   jax.ShapeDtypeStruct((N_HEADS, nq, 1, tq), F32)),
        scratch_shapes=[pltpu.VMEM((S, 1), F32)],
        compiler_params=_params(("arbitrary",)),
    )(qkv, qkv, qkv, rest, cum, cumr)


def _attn_bwd_pre(rest, o, dy, *, tq):
    S = o.shape[0]
    nq = S // tq

    def body(g_ref, o_ref, dy_ref, do_ref, dg_ref, dl_ref):
        g = g_ref[...]
        sg = _sigmoid(g)
        ov = o_ref[...]
        dy = dy_ref[...]
        do = dy * (g * sg)
        do_ref[...] = do.astype(BF16)
        dg_ref[...] = (dy * ov * (sg * (1.0 + g * (1.0 - sg)))).astype(BF16)
        rows = lax.dot_general(jnp.ones((8, HEAD_DIM), F32), do * ov, (((1,), (1,)), ((), ())),
                               precision=lax.Precision.HIGHEST, preferred_element_type=F32)
        for qi in range(nq):
            dl_ref[qi] = rows[0:1, qi * tq:(qi + 1) * tq]

    att = jax.ShapeDtypeStruct((S, ATT_W), BF16)
    return pl.pallas_call(
        body, name="attn_bwd_pre", grid=(N_HEADS,),
        in_specs=[_colblock(S, GB), _colblock(S, 0), _colblock(S, 0)],
        out_specs=(_colblock(S, 0), _colblock(S, QKV_W // LANES), _head_row(nq, tq)),
        out_shape=(att, jax.ShapeDtypeStruct((S, ALL_W), BF16), jax.ShapeDtypeStruct((N_HEADS, nq, 1, tq), F32)),
        compiler_params=_params(("parallel",)),
    )(rest, o, dy)


def _attn_bwd(qkv, do, a2, delta, cum, dproj, *, tq):
    S = qkv.shape[0]
    nq = S // tq
    tdot = (((0,), (0,)), ((), ()))
    ndot = (((1,), (1,)), ((), ()))

    def body(q_ref, k_ref, v_ref, do_ref, a_ref, dl_ref, cum_ref, dproj_in, dproj_ref, dsq_ref, dsk_ref,
             dqa, dq_ref, dk_ref, dv_ref, cc, sems):
        h = pl.program_id(0)
        dqa[...] = jnp.zeros_like(dqa)
        dsq_ref[...] = jnp.zeros_like(dsq_ref)
        cc[...] = _lane_of(cum_ref, h) * LOG2E

        @pl.when(h == 0)
        def _():
            dsk_ref[...] = jnp.zeros_like(dsk_ref)

        tri = (lax.broadcasted_iota(jnp.int32, (tq, tq), 0) <= lax.broadcasted_iota(jnp.int32, (tq, tq), 1))
        mine = lax.broadcasted_iota(jnp.int32, (tq, LANES), 1) == h

        def k_step(kj, _):
            cols = pl.ds(pl.multiple_of(kj * tq, tq), tq)
            k = k_ref[cols, :]
            v = v_ref[cols, :]
            cj = cc[cols, :]

            def tile(qi, carry, masked):
                dk, dv, ksum = carry
                rows = pl.ds(pl.multiple_of(qi * tq, tq), tq)
                q = q_ref[rows, :]
                dot = do_ref[rows, :]
                s = lax.dot_general(k, q, ndot, preferred_element_type=F32)
                p = jnp.exp2(s * (SCALE * LOG2E) + a_ref[qi] - cj)
                if masked:
                    p = jnp.where(tri, p, 0.0)
                dv = dv + jnp.dot(p.astype(BF16), dot, preferred_element_type=F32)
                dp = lax.dot_general(v, dot, ndot, preferred_element_type=F32)
                ds = p * (dp - dl_ref[qi])
                dsq_ref[qi] += jnp.sum(ds, axis=0, keepdims=True)
                ksum = ksum + jnp.sum(ds, axis=1, keepdims=True)
                dsb = (ds * SCALE).astype(BF16)
                dk = dk + jnp.dot(dsb, q, preferred_element_type=F32)
                dqa[rows, :] += lax.dot_general(dsb, k, tdot, preferred_element_type=F32)
                return dk, dv, ksum

            init = (jnp.zeros((tq, HEAD_DIM), F32), jnp.zeros((tq, HEAD_DIM), F32), jnp.zeros((tq, 1), F32))
            carry = tile(kj, init, True)
            dk, dv, ksum = lax.fori_loop(kj + 1, nq, lambda qi, c: tile(qi, c, False), carry)
            dk_ref[cols, :] = dk.astype(BF16)
            dv_ref[cols, :] = dv.astype(BF16)
            dsk_ref[cols, :] += jnp.where(mine, ksum, 0.0)
            return 0

        lax.fori_loop(0, nq, k_step, 0)
        dq_ref[...] = dqa[...].astype(BF16)
        _put_columns([dq_ref, dk_ref, dv_ref], dproj_ref, [QB + h, KB + h, VB + h], sems)

    head = pltpu.VMEM((S, HEAD_DIM), BF16)
    return pl.pallas_call(
        body, name="attn_bwd", grid=(N_HEADS,),
        in_specs=[_colblock(S, QB), _colblock(S, KB), _colblock(S, VB), _colblock(S, 0),
                  _head_row(nq, tq), _head_row(nq, tq), pl.BlockSpec((S, LANES), lambda h: (0, 0)), ANY],
        out_specs=(ANY, _head_row(nq, tq), pl.BlockSpec((S, LANES), lambda h: (0, 0))),
        out_shape=(jax.ShapeDtypeStruct(dproj.shape, dproj.dtype), jax.ShapeDtypeStruct((N_HEADS, nq, 1, tq), F32),
                   jax.ShapeDtypeStruct((S, LANES), F32)),
        input_output_aliases={7: 0},
        scratch_shapes=[pltpu.VMEM((S, HEAD_DIM), F32), head, head, head, pltpu.VMEM((S, 1), F32),
                        pltpu.SemaphoreType.DMA((3,))],
        compiler_params=_params(("arbitrary",)),
    )(qkv, qkv, qkv, do, a2, delta, cum, dproj)


def _conv_taps(u, w_ref, row):
    return (w_ref[0:1, :] * _shift_down(u, 2, row) + w_ref[1:2, :] * _shift_down(u, 1, row)) + w_ref[2:3, :] * u


def _conv_fwd(rest, cw, y):
    S = rest.shape[0]

    def body(cb_ref, cc_ref, ch_ref, g_ref, w_ref, y_in, y_ref):
        row = lax.broadcasted_iota(jnp.int32, (S, LANES), 0)
        g = g_ref[...]
        y = _conv_taps(cc_ref[...] * ch_ref[...], w_ref, row)
        y_ref[...] = ((cb_ref[...] * y) * (g * _sigmoid(g))).astype(BF16)

    return pl.pallas_call(
        body, name="conv_fwd", grid=(CONV_W // LANES,),
        in_specs=[_colblock(S, CBB), _colblock(S, CCB), _colblock(S, CHB), _colblock(S, CGB),
                  pl.BlockSpec((CONV_TAPS, LANES), lambda j: (0, j)), ANY],
        out_specs=_colblock(S, ATT_W // LANES),
        out_shape=jax.ShapeDtypeStruct(y.shape, y.dtype),
        input_output_aliases={5: 0},
        compiler_params=_params(("parallel",)),
    )(rest, rest, rest, rest, cw, y)


def _conv_bwd(rest, dy, cw, dproj):
    S = rest.shape[0]

    def body(cb_ref, cc_ref, ch_ref, g_ref, dy_ref, w_ref, dproj_in, dproj_ref, dw_ref,
             dcb_ref, dcc_ref, dch_ref, dg_ref, sems):
        row = lax.broadcasted_iota(jnp.int32, (S, LANES), 0)
        g = g_ref[...]
        sg = _sigmoid(g)
        silu = g * sg
        cb, cc, ch, dy = cb_ref[...], cc_ref[...], ch_ref[...], dy_ref[...]
        u = cc * ch
        u1 = _shift_down(u, 1, row)
        u2 = _shift_down(u, 2, row)
        y = (w_ref[0:1, :] * u2 + w_ref[1:2, :] * u1) + w_ref[2:3, :] * u
        dcb_ref[...] = (dy * silu * y).astype(BF16)
        dg_ref[...] = (dy * (cb * y) * (sg * (1.0 + g * (1.0 - sg)))).astype(BF16)
        dyv = dy * silu * cb
        du = w_ref[2:3, :] * dyv + w_ref[1:2, :] * _shift_up(dyv, 1, row) + w_ref[0:1, :] * _shift_up(dyv, 2, row)
        dcc_ref[...] = (du * ch).astype(BF16)
        dch_ref[...] = (du * cc).astype(BF16)
        dw_ref[0:1, :] = jnp.sum(dyv * u2, axis=0, keepdims=True)
        dw_ref[1:2, :] = jnp.sum(dyv * u1, axis=0, keepdims=True)
        dw_ref[2:3, :] = jnp.sum(dyv * u, axis=0, keepdims=True)
        j = pl.program_id(0)
        first = QKV_W // LANES
        _put_columns([dcb_ref, dcc_ref, dch_ref, dg_ref], dproj_ref,
                     [first + CBB + j, first + CCB + j, first + CHB + j, first + CGB + j], sems)

    piece = pltpu.VMEM((S, LANES), BF16)
    return pl.pallas_call(
        body, name="conv_bwd", grid=(CONV_W // LANES,),
        in_specs=[_colblock(S, CBB), _colblock(S, CCB), _colblock(S, CHB), _colblock(S, CGB),
                  _colblock(S, ATT_W // LANES), pl.BlockSpec((CONV_TAPS, LANES), lambda j: (0, j)), ANY],
        out_specs=(ANY, pl.BlockSpec((CONV_TAPS, LANES), lambda j: (0, j))),
        out_shape=(jax.ShapeDtypeStruct(dproj.shape, dproj.dtype), jax.ShapeDtypeStruct((CONV_TAPS, CONV_W), F32)),
        input_output_aliases={6: 0},
        scratch_shapes=[piece] * 4 + [pltpu.SemaphoreType.DMA((4,))],
        compiler_params=_params(("arbitrary",)),
    )(rest, rest, rest, rest, dy, cw, dproj)


def _pool_z(u, grp, row):
    s2 = u + _shift_down(u, 1, row)
    s4 = s2 + _shift_down(s2, 2, row)
    s8 = s4 + _shift_down(s4, 4, row)
    s16 = s8 + _shift_down(s8, 8, row)
    sw = jnp.where(grp == 0, s2, jnp.where(grp == 1, s4, jnp.where(grp == 2, s8, s16)))
    return sw / _pool_count(grp, row) - u


def _pool_count(grp, row):
    return jnp.minimum(row + 1, 2 << grp).astype(F32)


def _pool_fwd(rest, pw, scale, y):
    S = rest.shape[0]

    def body(u_ref, g_ref, pw_ref, sc_ref, y_in, y_ref):
        grp = pl.program_id(0)
        row = lax.broadcasted_iota(jnp.int32, (S, LANES), 0)
        z = _pool_z(u_ref[...], grp, row)
        zp = jnp.dot(z.astype(BF16), pw_ref[...].astype(BF16), preferred_element_type=F32)
        g = g_ref[...]
        y_ref[...] = ((zp * sc_ref[...]) * (g * _sigmoid(g))).astype(BF16)

    return pl.pallas_call(
        body, name="pool_fwd", grid=(N_POOL,),
        in_specs=[_colblock(S, PUB), _colblock(S, PGB),
                  pl.BlockSpec((None, POOL_GROUP, POOL_GROUP), lambda j: (j, 0, 0)),
                  pl.BlockSpec((1, LANES), lambda j: (0, j)), ANY],
        out_specs=_colblock(S, (ATT_W + CONV_W) // LANES),
        out_shape=jax.ShapeDtypeStruct(y.shape, y.dtype),
        input_output_aliases={4: 0},
        compiler_params=_params(("parallel",)),
    )(rest, rest, pw, scale, y)


def _pool_bwd(rest, dy, pw, scale, dproj):
    S = rest.shape[0]

    def body(u_ref, g_ref, dy_ref, pw_ref, sc_ref, dproj_in, dproj_ref, dpw_ref, dsc_ref, du_ref, dg_ref, sems):
        grp = pl.program_id(0)
        row = lax.broadcasted_iota(jnp.int32, (S, LANES), 0)
        z = _pool_z(u_ref[...], grp, row).astype(BF16)
        pwb = pw_ref[...].astype(BF16)
        zp = jnp.dot(z, pwb, preferred_element_type=F32)
        g = g_ref[...]
        sg = _sigmoid(g)
        silu = g * sg
        dy = dy_ref[...]
        sc = sc_ref[...]
        dsc_ref[...] = jnp.sum(dy * silu * zp, axis=0, keepdims=True)
        dg_ref[...] = (dy * (zp * sc) * (sg * (1.0 + g * (1.0 - sg)))).astype(BF16)
        dzp = (dy * silu * sc).astype(BF16)
        dpw_ref[...] = lax.dot_general(z, dzp, (((0,), (0,)), ((), ())), preferred_element_type=F32)
        dz = lax.dot_general(dzp, pwb, (((1,), (1,)), ((), ())), preferred_element_type=F32)
        f1 = dz / _pool_count(grp, row)
        f2 = f1 + _shift_up(f1, 1, row)
        f4 = f2 + _shift_up(f2, 2, row)
        f8 = f4 + _shift_up(f4, 4, row)
        f16 = f8 + _shift_up(f8, 8, row)
        fw = jnp.where(grp == 0, f2, jnp.where(grp == 1, f4, jnp.where(grp == 2, f8, f16)))
        du_ref[...] = (fw - dz).astype(BF16)
        first = QKV_W // LANES
        _put_columns([du_ref, dg_ref], dproj_ref, [first + PUB + grp, first + PGB + grp], sems)

    piece = pltpu.VMEM((S, LANES), BF16)
    return pl.pallas_call(
        body, name="pool_bwd", grid=(N_POOL,),
        in_specs=[_colblock(S, PUB), _colblock(S, PGB), _colblock(S, (ATT_W + CONV_W) // LANES),
                  pl.BlockSpec((None, POOL_GROUP, POOL_GROUP), lambda j: (j, 0, 0)),
                  pl.BlockSpec((1, LANES), lambda j: (0, j)), ANY],
        out_specs=(ANY, pl.BlockSpec((None, POOL_GROUP, POOL_GROUP), lambda j: (j, 0, 0)),
                   pl.BlockSpec((1, LANES), lambda j: (0, j))),
        out_shape=(jax.ShapeDtypeStruct(dproj.shape, dproj.dtype),
                   jax.ShapeDtypeStruct((N_POOL, POOL_GROUP, POOL_GROUP), F32), jax.ShapeDtypeStruct((1, POOL_W), F32)),
        input_output_aliases={5: 0},
        scratch_shapes=[piece] * 2 + [pltpu.SemaphoreType.DMA((2,))],
        compiler_params=_params(("arbitrary",)),
    )(rest, rest, dy, pw, scale, dproj)


def _ln_fwd(z, g, b, *, ts):
    S, D = z.shape

    def body(z_ref, g_ref, b_ref, o_ref, ob_ref, obt_ref, xh_ref, rs_ref):
        zz = z_ref[...]
        mu = jnp.mean(zz, axis=1, keepdims=True)
        zc = zz - mu
        rstd = lax.rsqrt(jnp.mean(zc * zc, axis=1, keepdims=True) + LN_EPS)
        xh = zc * rstd
        out = xh * g_ref[...] + b_ref[...]
        o_ref[...] = out
        ob_ref[...] = out.astype(BF16)
        obt_ref[...] = out.T.astype(BF16)
        xh_ref[...] = xh
        rs_ref[...] = rstd

    rowblk = pl.BlockSpec((ts, D), lambda i: (i, 0))
    vec = pl.BlockSpec((1, D), lambda i: (0, 0))
    return pl.pallas_call(
        body, name="ln_fwd", grid=(S // ts,),
        in_specs=[rowblk, vec, vec],
        out_specs=(rowblk, rowblk, pl.BlockSpec((D, ts), lambda i: (0, i)), rowblk,
                   pl.BlockSpec((ts, 1), lambda i: (i, 0))),
        out_shape=(jax.ShapeDtypeStruct((S, D), F32), jax.ShapeDtypeStruct((S, D), BF16),
                   jax.ShapeDtypeStruct((D, S), BF16), jax.ShapeDtypeStruct((S, D), F32),
                   jax.ShapeDtypeStruct((S, 1), F32)),
        compiler_params=_params(("parallel",)),
    )(z, g, b)


def _prep(x, after, *, ts):
    S, D = x.shape

    def body(x_ref, after_ref, xb_ref, xbt_ref):
        xb_ref[...] = x_ref[...].astype(BF16)
        xbt_ref[...] = x_ref[...].T.astype(BF16)

    return pl.pallas_call(
        body, name="prep", grid=(S // ts,),
        in_specs=[pl.BlockSpec((ts, D), lambda i: (i, 0)), ANY],
        out_specs=(pl.BlockSpec((ts, D), lambda i: (i, 0)), pl.BlockSpec((D, ts), lambda i: (0, i))),
        out_shape=(jax.ShapeDtypeStruct((S, D), BF16), jax.ShapeDtypeStruct((D, S), BF16)),
        compiler_params=_params(("parallel",)),
    )(x, after)


def _ln_bwd(dout, xhat, rstd, g, *, ts):
    S, D = dout.shape

    def body(d_ref, xh_ref, rs_ref, g_ref, dz_ref, dzb_ref, dg_ref, db_ref):
        @pl.when(pl.program_id(0) == 0)
        def _():
            dg_ref[...] = jnp.zeros_like(dg_ref)
            db_ref[...] = jnp.zeros_like(db_ref)

        d = d_ref[...]
        xh = xh_ref[...]
        dxh = d * g_ref[...]
        m1 = jnp.mean(dxh, axis=1, keepdims=True)
        m2 = jnp.mean(dxh * xh, axis=1, keepdims=True)
        dz = rs_ref[...] * (dxh - m1 - xh * m2)
        dz_ref[...] = dz
        dzb_ref[...] = dz.astype(BF16)
        dg_ref[...] += jnp.sum(d * xh, axis=0, keepdims=True)
        db_ref[...] += jnp.sum(d, axis=0, keepdims=True)

    rowblk = pl.BlockSpec((ts, D), lambda i: (i, 0))
    vec = pl.BlockSpec((1, D), lambda i: (0, 0))
    return pl.pallas_call(
        body, name="ln_bwd", grid=(S // ts,),
        in_specs=[rowblk, rowblk, pl.BlockSpec((ts, 1), lambda i: (i, 0)), vec],
        out_specs=(rowblk, rowblk, vec, vec),
        out_shape=(jax.ShapeDtypeStruct((S, D), F32), jax.ShapeDtypeStruct((S, D), BF16),
                   jax.ShapeDtypeStruct((1, D), F32), jax.ShapeDtypeStruct((1, D), F32)),
        compiler_params=_params(("arbitrary",)),
    )(dout, xhat, rstd, g)


def _loss_head(y, tgt, *, ts):
    S, D = y.shape

    def body(y_ref, t_ref, l_ref, dy_ref):
        @pl.when(pl.program_id(0) == 0)
        def _():
            l_ref[...] = jnp.zeros_like(l_ref)

        e = y_ref[...] - t_ref[...]
        dy_ref[...] = e * (1.0 / D)
        rowloss = jnp.sum(e * e, axis=1, keepdims=True) * (0.5 / D)
        l_ref[...] += jnp.sum(rowloss, axis=0, keepdims=True)

    rowblk = pl.BlockSpec((ts, D), lambda i: (i, 0))
    return pl.pallas_call(
        body, name="loss_head", grid=(S // ts,),
        in_specs=[rowblk, rowblk],
        out_specs=(pl.BlockSpec((1, 1), lambda i: (0, 0)), rowblk),
        out_shape=(jax.ShapeDtypeStruct((1, 1), F32), jax.ShapeDtypeStruct((S, D), F32)),
        compiler_params=_params(("arbitrary",)),
    )(y, tgt)


def _adamw(lands, srcs, w, m, v, *, name, first=0, into=None):
    R, C = w.shape
    npart = len(lands)
    rp = lands[0].shape[1]
    assert R % rp == 0 and all(p.shape == (N_DEV, rp, C) for p in lands)
    tiled = rp % 8 == 0
    assert tiled or (first == 0 and npart * rp == R and into is None)
    tr = min(ADAMW_BLOCK_ELEMS // (LANES * pl.cdiv(C, LANES)), rp) if tiled else rp
    assert rp % tr == 0
    steps = rp // tr
    c1 = 1.0 - ADAM_B1 ** ADAM_STEP
    c2 = 1.0 - ADAM_B2 ** ADAM_STEP
    slabbed = [p.ndim == 3 for p in srcs]
    n_into = 0 if into is None else 4

    def body(*refs):
        l_refs, s_refs = refs[:npart], refs[npart:2 * npart]
        w_ref, m_ref, v_ref = refs[2 * npart:2 * npart + 3]
        g_ref, d_ref, nm_ref, nv_ref = refs[2 * npart + 3 + n_into:]
        x, y, c = _my_place()
        me = 4 * x + 2 * y + c

        def update(l, rows):
            g = None
            for j in range(N_DEV):
                own = s_refs[l][j] if slabbed[l] else s_refs[l][...]
                p = jnp.where(me == j, own, l_refs[l][j]).astype(F32)
                g = p if g is None else g + p
            nm = ADAM_B1 * m_ref[rows, :] + (1.0 - ADAM_B1) * g
            nv = ADAM_B2 * v_ref[rows, :] + (1.0 - ADAM_B2) * (g * g)
            g_ref[rows, :] = g
            nm_ref[rows, :] = nm
            nv_ref[rows, :] = nv
            d_ref[rows, :] = -ADAM_LR * ((nm / c1) / (jnp.sqrt(nv / c2) + ADAM_EPS) + ADAM_WD * w_ref[rows, :])

        for l in range(npart):
            if tiled:
                pl.when(pl.program_id(0) == l)(lambda l=l: update(l, slice(None)))
            else:
                update(l, slice(l * rp, (l + 1) * rp))

    def part_spec(l, p):
        if p.ndim == 3:
            return pl.BlockSpec((N_DEV, tr, C), lambda q, i: (0, jnp.where(q == l, i, 0), 0))
        return pl.BlockSpec((tr, C), lambda q, i: (jnp.where(q == l, i, 0), 0))

    if tiled:
        blk = pl.BlockSpec((tr, C), lambda q, i: ((first + q) * steps + i, 0))
    else:
        blk = pl.BlockSpec((R, C), lambda q, i: (0, 0))
    out = jax.ShapeDtypeStruct((R, C), F32)
    n_in = 2 * npart + 3
    return pl.pallas_call(
        body, name=name, grid=(npart, steps) if tiled else (1, 1),
        in_specs=[part_spec(l, p) for l, p in enumerate(lands)] + [part_spec(l, p) for l, p in enumerate(srcs)]
        + [blk, blk, blk] + [ANY] * n_into,
        out_specs=(blk, blk, blk, blk),
        out_shape=(out, out, out, out),
        input_output_aliases={n_in + i: i for i in range(n_into)},
        compiler_params=_params(("arbitrary", "arbitrary")),
    )(*lands, *srcs, w, m, v, *(into or ()))


def _my_place():
    return lax.axis_index("x"), lax.axis_index("y"), lax.axis_index("c")


HBM = pl.BlockSpec(memory_space=pltpu.HBM)
SEM = pl.BlockSpec(memory_space=pltpu.SEMAPHORE)
DATAFLOW = pltpu.SideEffectType.DATAFLOW_SIDE_EFFECTING
EVERYONE = (1, 2, 3, 4, 5, 6, 7)
SIBLING = 1
SAME_CORE = (4, 2, 6)


def _xchg_copies(src, land, send_sem, recv_sem, gather, peers):
    x, y, c = _my_place()
    me = 4 * x + 2 * y + c
    copies = []
    for k, bits in enumerate(peers):
        px, py, pc = x ^ ((bits >> 2) & 1), y ^ ((bits >> 1) & 1), c ^ (bits & 1)
        peer = 4 * px + 2 * py + pc
        for a in range(len(src)):
            i = a * len(peers) + k
            out = src[a] if gather[a] else src[a].at[peer]
            sent, landed = [pltpu.make_async_remote_copy(
                src_ref=out, dst_ref=land[a].at[slab], send_sem=send_sem.at[i], recv_sem=recv_sem.at[i],
                device_id=(px, py, pc), device_id_type=MESH) for slab in (me, peer)]
            copies.append((sent, landed))
    return copies


def _xchg_start(srcs, *, gather, name, peers=EVERYONE, after=None):
    n = len(srcs)
    follows = [] if after is None else [after]
    lands = [lax.empty(((N_DEV,) + v.shape) if g else v.shape, v.dtype) for v, g in zip(srcs, gather)]

    def body(*refs):
        src, land = refs[:n], refs[n:2 * n]
        send_sem, recv_sem = refs[2 * n + len(follows)], refs[2 * n + len(follows) + 1]
        token = refs[-1]
        for sent, _ in _xchg_copies(src, land, send_sem, recv_sem, gather, peers):
            sent.start()
        token[...] = jnp.zeros_like(token)

    sems = pltpu.SemaphoreType.DMA((n * len(peers),))
    outs = pl.pallas_call(
        body, name=name,
        out_shape=(sems, sems, *[pltpu.HBM(v.shape, v.dtype) for v in srcs + lands],
                   jax.ShapeDtypeStruct((8, LANES), F32)),
        in_specs=[HBM] * (2 * n) + [ANY] * len(follows),
        out_specs=(SEM, SEM, *[HBM] * (2 * n), pl.BlockSpec(memory_space=pltpu.VMEM)),
        input_output_aliases={i: 2 + i for i in range(2 * n)},
        compiler_params=pltpu.CompilerParams(has_side_effects=DATAFLOW),
    )(*[pltpu.with_memory_space_constraint(v, pltpu.HBM) for v in srcs + lands], *follows)
    return dict(send=outs[0], recv=outs[1], srcs=list(outs[2:2 + n]), lands=list(outs[2 + n:2 + 2 * n]),
                token=outs[-1], gather=gather, peers=peers)


def _xchg_wait(h, after, *, name):
    n = len(h["srcs"])
    gather, peers = h["gather"], h["peers"]
    after = list(after) if isinstance(after, (list, tuple)) else [after]

    def body(*refs):
        src, land = refs[:n], refs[n:2 * n]
        send_sem, recv_sem = refs[2 * n], refs[2 * n + 1]
        for sent, landed in _xchg_copies(src, land, send_sem, recv_sem, gather, peers):
            sent.wait_send()
            landed.wait_recv()

    thru = h["srcs"] + h["lands"]
    outs = pl.pallas_call(
        body, name=name,
        out_shape=[pltpu.HBM(v.shape, v.dtype) for v in thru],
        in_specs=[HBM] * (2 * n) + [SEM, SEM] + [ANY] * len(after),
        out_specs=[HBM] * (2 * n),
        input_output_aliases={i: i for i in range(2 * n)},
        compiler_params=pltpu.CompilerParams(has_side_effects=DATAFLOW),
    )(*thru, h["send"], h["recv"], *after)
    return list(outs[:n]), list(outs[n:])


def _relay_copies(land, send_sem, recv_sem):
    x, y, c = _my_place()
    me = 4 * x + 2 * y + c
    copies = []
    for k, bits in enumerate(SAME_CORE):
        for a in range(len(land)):
            i = a * len(SAME_CORE) + k
            sent, landed = [pltpu.make_async_remote_copy(
                src_ref=land[a].at[me ^ bits], dst_ref=land[a].at[slab], send_sem=send_sem.at[i], recv_sem=recv_sem.at[i],
                device_id=(x, y, 1 - c), device_id_type=MESH) for slab in (me ^ bits, me ^ bits ^ SIBLING)]
            copies.append((sent, landed))
    return copies


def _relay_start(lands, *, name):
    n = len(lands)

    def body(*refs):
        land = refs[:n]
        send_sem, recv_sem = refs[n], refs[n + 1]
        token = refs[-1]
        for sent, _ in _relay_copies(land, send_sem, recv_sem):
            sent.start()
        token[...] = jnp.zeros_like(token)

    sems = pltpu.SemaphoreType.DMA((n * len(SAME_CORE),))
    outs = pl.pallas_call(
        body, name=name,
        out_shape=(sems, sems, *[pltpu.HBM(v.shape, v.dtype) for v in lands], jax.ShapeDtypeStruct((8, LANES), F32)),
        in_specs=[HBM] * n,
        out_specs=(SEM, SEM, *[HBM] * n, pl.BlockSpec(memory_space=pltpu.VMEM)),
        input_output_aliases={i: 2 + i for i in range(n)},
        compiler_params=pltpu.CompilerParams(has_side_effects=DATAFLOW),
    )(*[pltpu.with_memory_space_constraint(v, pltpu.HBM) for v in lands])
    return dict(send=outs[0], recv=outs[1], lands=list(outs[2:2 + n]), token=outs[-1])


def _relay_wait(h, after, *, name):
    n = len(h["lands"])

    def body(*refs):
        land = refs[:n]
        send_sem, recv_sem = refs[n], refs[n + 1]
        for sent, landed in _relay_copies(land, send_sem, recv_sem):
            sent.wait_send()
            landed.wait_recv()

    outs = pl.pallas_call(
        body, name=name,
        out_shape=[pltpu.HBM(v.shape, v.dtype) for v in h["lands"]],
        in_specs=[HBM] * n + [SEM, SEM, ANY],
        out_specs=[HBM] * n,
        input_output_aliases={i: i for i in range(n)},
        compiler_params=pltpu.CompilerParams(has_side_effects=DATAFLOW),
    )(*h["lands"], h["send"], h["recv"], after)
    return list(outs)


def _with_own(land, own, me):
    slot = lax.broadcasted_iota(jnp.int32, land.shape, 0)
    return jnp.where(slot == me, own if own.ndim == land.ndim else own[None], land)


def _w_in_segments(j, cols):
    lo, hi = j * cols, (j + 1) * cols
    segs = []
    for a, b, where, shift in ((0, FG_AT, "main", 0), (FG_AT, FG_AT + N_HEADS, "fg", -FG_AT),
                               (FG_AT + N_HEADS, IN_W, "main", -N_HEADS)):
        s0, s1 = max(lo, a), min(hi, b)
        if s0 < s1:
            segs.append((s0 - lo, s1 - s0, where, s0 + shift))
    return segs


SLAB_W = 1024


def _window(off, width):
    base = min((off // LANES) * LANES, width - SLAB_W)
    return base, off - base


def _w_unpack(land, own, *, tr):
    _, D, cols = land.shape
    assert cols + LANES - 1 <= SLAB_W

    def body(land_ref, own_ref, main_ref, slab):
        x, y, c = _my_place()
        me = 4 * x + 2 * y + c
        lane = lax.broadcasted_iota(jnp.int32, (tr, SLAB_W), 1)
        main_ref[...] = jnp.zeros_like(main_ref)
        slab[...] = jnp.zeros_like(slab)
        for j in range(N_DEV):
            slab[:, :cols] = jnp.where(me == j, own_ref[...], land_ref[j])
            v = slab[...]
            segs = _w_in_segments(j, cols)
            for lo, n, where, dst in segs:
                part = v if len(segs) == 1 else jnp.where((lane >= lo) & (lane < lo + n), v, jnp.zeros_like(v))
                if where == "fg":
                    main_ref[:, MAIN_W:] = pltpu.roll(part, (SLAB_W - lo) % SLAB_W, 1)[:, :LANES]
                else:
                    base, r = _window(dst - lo, MAIN_W)
                    assert 0 <= r and lo + n + r <= SLAB_W
                    main_ref[:, base:base + SLAB_W] += pltpu.roll(part, r, 1)

    return pl.pallas_call(
        body, name="w_unpack", grid=(D // tr,),
        in_specs=[pl.BlockSpec((N_DEV, tr, cols), lambda i: (0, i, 0)), pl.BlockSpec((tr, cols), lambda i: (i, 0))],
        out_specs=pl.BlockSpec((tr, ALL_W), lambda i: (i, 0)),
        out_shape=jax.ShapeDtypeStruct((D, ALL_W), BF16),
        scratch_shapes=[pltpu.VMEM((tr, SLAB_W), BF16)],
        compiler_params=_params(("parallel",)),
    )(land, own)


def _g_pack(dw_main, dw_fg, cols, *, tr):
    D = dw_main.shape[0]

    def body(main_ref, fg_ref, out_ref, slab):
        lane = lax.broadcasted_iota(jnp.int32, (tr, SLAB_W), 1)
        slab[...] = jnp.zeros_like(slab)
        slab[:, :LANES] = fg_ref[...]
        fg = slab[...]
        for j in range(N_DEV):
            v = None
            for lo, n, where, src in _w_in_segments(j, cols):
                if where == "fg":
                    part = pltpu.roll(fg, lo, 1)
                else:
                    base, r = _window(src - lo, MAIN_W)
                    assert 0 <= r and lo + n + r <= SLAB_W
                    part = pltpu.roll(main_ref[:, base:base + SLAB_W], (SLAB_W - r) % SLAB_W, 1)
                v = part if v is None else jnp.where((lane >= lo) & (lane < lo + n), part, v)
            out_ref[j] = v[:, :cols]

    return pl.pallas_call(
        body, name="g_pack", grid=(D // tr,),
        in_specs=[pl.BlockSpec((tr, MAIN_W), lambda i: (i, 0)), pl.BlockSpec((tr, LANES), lambda i: (i, 0))],
        out_specs=pl.BlockSpec((N_DEV, tr, cols), lambda i: (0, i, 0)),
        out_shape=jax.ShapeDtypeStruct((N_DEV, D, cols), BF16),
        scratch_shapes=[pltpu.VMEM((tr, SLAB_W), BF16)],
        compiler_params=_params(("parallel",)),
    )(dw_main, dw_fg)


TQ = 1024
TS = 512
W_TILE_ROWS = 256


def _layer_fwd_proj(xb, w_in):
    qkv = _mm(xb, w_in, out_dtype=BF16, tm=2048, tn=512, tk=D_MODEL, name="mm_qkv", n_cols=QKV_W)
    rest = _mm(xb, w_in, out_dtype=F32, tm=2048, tn=512, tk=D_MODEL, name="mm_rest", n_off=QKV_W, n_cols=REST_W)
    fgp = _mm(xb, w_in, out_dtype=F32, tm=1024, tn=LANES, tk=D_MODEL, name="mm_fg", n_off=MAIN_W, n_cols=LANES)
    return qkv, rest, fgp


def _layer_fwd_mix(x, xbt, proj, wl):
    qkv, rest, fgp = proj
    cum = _fg_fwd(fgp, wl["b_f"])
    S = x.shape[0]
    cumr = (cum[:, :N_HEADS].T * LOG2E).reshape(N_HEADS, S // TQ, 1, TQ)
    o, y, lse = _attn_fwd(qkv, rest, cum, cumr, tq=TQ)
    y = _conv_fwd(rest, wl["conv_w"], y)
    y = _pool_fwd(rest, wl["pool_w"], wl["pool_scale"], y)
    z = _mm(y, wl["w_out"], out_dtype=F32, tm=2048, tn=512, tk=D_MODEL, name="mm_out", c=x, c_scale=ALPHA)
    out, outb, outbt, xhat, rstd = _ln_fwd(z, wl["ln_g"], wl["ln_b"], ts=TS)
    saved = dict(xbt=xbt, qkv=qkv, rest=rest, fgp=fgp, cum=cum, cumr=cumr, o=o, lse=lse, y=y, xhat=xhat, rstd=rstd)
    return out, outb, outbt, saved


def _layer_bwd_mix(dout, sv, wl):
    S = dout.shape[0]
    dz, dzb, dln_g, dln_b = _ln_bwd(dout, sv["xhat"], sv["rstd"], wl["ln_g"], ts=TS)
    dy = _mm(dzb, wl["w_out"], out_dtype=F32, tm=2048, tn=512, tk=D_MODEL, name="mm_dy", bt=True)
    dw_out = _mm(sv["y"], dzb, out_dtype=BF16, tm=512, tn=D_MODEL, tk=2048, name="mm_dw_out", at=True)
    rest = sv["rest"]
    do, dproj, delta = _attn_bwd_pre(rest, sv["o"], dy, tq=TQ)
    a2 = sv["cumr"] - sv["lse"] * LOG2E
    dproj, ds_q, ds_k = _attn_bwd(sv["qkv"], do, a2, delta, sv["cum"], dproj, tq=TQ)
    dcum = jnp.pad(ds_q.reshape(N_HEADS, S).T, ((0, 0), (0, LANES - N_HEADS))) - ds_k
    dproj, db_f = _fg_bwd(dcum, sv["fgp"], wl["b_f"], dproj)
    dproj, dconv_w = _conv_bwd(rest, dy, wl["conv_w"], dproj)
    dproj, dpool_w, dpool_scale = _pool_bwd(rest, dy, wl["pool_w"], wl["pool_scale"], dproj)
    grads = dict(b_f=db_f[:, :N_HEADS], conv_w=dconv_w, pool_w=dpool_w, pool_scale=dpool_scale, w_out=dw_out,
                 ln_g=dln_g, ln_b=dln_b)
    return grads, (dproj, dz)


def _layer_bwd_w_in(mid, sv):
    dproj, dz = mid
    xt = sv["xbt"]
    S = xt.shape[1]
    dw_main = _mm(xt, dproj, out_dtype=BF16, tm=1024, tn=512, tk=S, name="mm_dw_main", n_cols=MAIN_W)
    dw_fg = _mm(xt, dproj, out_dtype=BF16, tm=1024, tn=LANES, tk=2048, name="mm_dw_fg", n_off=MAIN_W, n_cols=LANES)
    return (dw_main, dw_fg), (dproj, dz)


def _layer_bwd_input(ctx, wl, after):
    dproj, dz = ctx
    return _mm(dproj, wl["w_in"], out_dtype=F32, tm=1024, tn=256, tk=ALL_W, name="mm_dx", c=dz, c_scale=ALPHA, bt=True,
               after=after)


def kernel(x, w_in, b_f, conv_w, pool_w, pool_scale, w_out, ln_g, ln_b, loss_target, m_w_in, m_b_f, m_conv_w, m_pool_w, m_pool_scale, m_w_out, m_ln_g, m_ln_b, v_w_in, v_b_f, v_conv_w, v_pool_w, v_pool_scale, v_w_out, v_ln_g, v_ln_b):
    L, D, cols = w_in.shape
    rows_out = w_out.shape[1]
    ccols = conv_w.shape[2]
    mx, my, mc = _my_place()
    me = 4 * mx + 2 * my + mc

    gathers = []
    for l in range(L):
        gathers.append(_xchg_start([w_in[l].astype(BF16), w_out[l].astype(BF16), conv_w[l]], gather=[True] * 3,
                                   name=f"gather_start_{l}",
                                   peers=(SIBLING,) + SAME_CORE if l == 0 else EVERYONE,
                                   after=gathers[-1]["token"] if gathers else None))
    started = gathers[-1]["token"]

    xl = x[0]
    xb, xbt = _prep(xl, started, ts=TS)
    weights, saved = [], []
    for l in range(L):
        (own_in, own_out, own_cw), (g_in, g_out, g_cw) = _xchg_wait(gathers[l], xb, name=f"gather_wait_{l}")
        if l == 0:
            g_in, g_out, g_cw = _relay_wait(_relay_start([g_in, g_out, g_cw], name="gather_relay_start"), xb,
                                            name="gather_relay_wait")
        w_all = _w_unpack(g_in, own_in, tr=W_TILE_ROWS)
        proj = _layer_fwd_proj(xb, w_all)
        w_o = _with_own(g_out, own_out, me).reshape(N_DEV * rows_out, D)
        cw = jnp.transpose(_with_own(g_cw, own_cw, me), (1, 0, 2)).reshape(CONV_TAPS, N_DEV * ccols)
        wl = dict(w_in=w_all, w_out=w_o, conv_w=cw,
                  b_f=jnp.pad(b_f[l][None, :], ((0, 0), (0, LANES - N_HEADS))), pool_w=pool_w[l],
                  pool_scale=pool_scale[l][None, :], ln_g=ln_g[l][None, :], ln_b=ln_b[l][None, :])
        xl, xb, xbt, sv = _layer_fwd_mix(xl, xbt, proj, wl)
        weights.append(wl)
        saved.append(sv)

    loss, dx = _loss_head(xl, loss_target[0], ts=TS)
    loss = lax.psum(loss[0, 0], ("x", "y", "c"))

    names = ["w_in", "w_out", "conv_w", "b_f", "pool_w", "pool_scale", "ln_g", "ln_b"]
    sharded = names[:3]
    scatters = [None] * L
    for l in reversed(range(L)):
        g, mid = _layer_bwd_mix(dx, saved[l], weights[l])
        (dw_main, dw_fg), ctx = _layer_bwd_w_in(mid, saved[l])
        g["w_in"] = _g_pack(dw_main, dw_fg, cols, tr=W_TILE_ROWS)
        g["w_out"] = g["w_out"].reshape(N_DEV, rows_out, D)
        g["conv_w"] = jnp.transpose(g["conv_w"].reshape(CONV_TAPS, N_DEV, ccols), (1, 0, 2))
        scatters[l] = _xchg_start([g[k] for k in names], gather=[k not in sharded for k in names],
                                  name=f"scatter_start_{l}")
        dx = _layer_bwd_input(ctx, weights[l], scatters[l]["token"])

    given = dict(w_in=(w_in, m_w_in, v_w_in), b_f=(b_f, m_b_f, v_b_f), conv_w=(conv_w, m_conv_w, v_conv_w),
                 pool_w=(pool_w, m_pool_w, v_pool_w), pool_scale=(pool_scale, m_pool_scale, v_pool_scale),
                 w_out=(w_out, m_w_out, v_w_out), ln_g=(ln_g, m_ln_g, v_ln_g), ln_b=(ln_b, m_ln_b, v_ln_b))
    sent, landed = [{} for _ in range(L)], [{} for _ in range(L)]

    def wait(l, after):
        s_, l_ = _xchg_wait(scatters[l], after, name=f"scatter_wait_{l}")
        sent[l].update(zip(names, s_))
        landed[l].update(zip(names, l_))

    last_start = scatters[0]["token"][0, 0]

    def update(k, layers, into=None):
        w, m, v = given[k]
        C = w.shape[-1]
        if k == "w_in":
            m, v = m + last_start, v + last_start
        w2, m2, v2 = (t.reshape(-1, C) for t in (w, m, v))
        rp = w2.shape[0] // L
        lands = [landed[l][k].reshape(N_DEV, rp, C) for l in layers]
        srcs = [sent[l][k].reshape((N_DEV, rp, C) if k in sharded else (rp, C)) for l in layers]
        return _adamw(lands, srcs, w2, m2, v2, name=f"adamw_{k}_{layers[0]}", first=layers[0], into=into)

    big = ["w_in", "w_out", "pool_w"]
    for l in reversed(range(1, L)):
        wait(l, dx)
    upper = {k: update(k, list(range(1, L))) for k in big} if L > 1 else {}
    wait(0, [upper[k][0] for k in big] if upper else dx)
    res = {k: update(k, [0], into=upper[k]) if k in upper else update(k, list(range(L))) for k in names}
    res = {k: [t.reshape(given[k][0].shape) for t in outs] for k, outs in res.items()}

    order = ["w_in", "b_f", "conv_w", "pool_w", "pool_scale", "w_out", "ln_g", "ln_b"]
    return (loss, dx[None], *[res[k][0] for k in order], *[res[k][1] for k in order],
            *[res[k][2] for k in order], *[res[k][3] for k in order])
```

```python
import jax
import jax.numpy as jnp
from jax import lax
from jax.experimental import pallas as pl
from jax.experimental.pallas import tpu as pltpu

F32 = jnp.float32
BF16 = jnp.bfloat16

N_DEV = 8
D_MODEL = 2048
N_HEADS = 8
HEAD_DIM = 128
ATT_W = N_HEADS * HEAD_DIM
CONV_W = 512
CONV_TAPS = 3
POOL_W = 512
POOL_GROUP = 128
N_POOL = POOL_W // POOL_GROUP
FG_AT = 4 * ATT_W
IN_W = FG_AT + N_HEADS + 4 * CONV_W + 2 * POOL_W
MAIN_W = IN_W - N_HEADS
LANES = 128
ALL_W = MAIN_W + LANES
LN_EPS = 1e-5
DEPTH = 4
ALPHA = (2 * DEPTH) ** 0.25
SCALE = HEAD_DIM ** -0.5
LOG2E = 1.4426950408889634
LN2 = 0.6931471805599453
NEG = -1e30

ADAM_LR, ADAM_B1, ADAM_B2, ADAM_EPS, ADAM_WD, ADAM_STEP = 0.001, 0.9, 0.999, 1e-08, 0.01, 10

QKV_W = 3 * ATT_W
REST_W = MAIN_W - QKV_W
QB, KB, VB = 0, 8, 16
GB = 0
CBB, CCB, CHB, CGB = 8, 12, 16, 20
PUB, PGB = 24, 28
FGB = MAIN_W // LANES
VMEM_LIMIT = 48 * 1024 * 1024
ADAMW_BLOCK_ELEMS = 64 * 1024

MESH = pl.DeviceIdType.MESH
ANY = pl.BlockSpec(memory_space=pl.ANY)


def _params(semantics):
    return pltpu.CompilerParams(dimension_semantics=semantics, vmem_limit_bytes=VMEM_LIMIT)


def _sigmoid(x):
    return 1.0 / (1.0 + jnp.exp(-x))


def _mm(a, b, *, out_dtype, tm, tn, tk, name, c=None, c_scale=1.0, n_off=0, n_cols=None, bt=False, at=False,
        after=None):
    M, K = a.shape[::-1] if at else a.shape
    N = b.shape[0 if bt else 1] if n_cols is None else n_cols
    assert not (bt and n_off)
    dims = (((0 if at else 1,), (1 if bt else 0,)), ((), ()))
    tm, tn, tk = min(tm, M), min(tn, N), min(tk, K)
    assert M % tm == 0 and N % tn == 0 and K % tk == 0 and n_off % tn == 0, (a.shape, b.shape, tm, tn, tk)
    nk, joff = K // tk, n_off // tn
    has_c, has_after = c is not None, after is not None

    def body(*refs):
        a_ref, b_ref = refs[0], refs[1]
        c_ref = refs[2] if has_c else None
        o_ref = refs[2 + has_c + has_after]

        def finish(r):
            if has_c:
                r = r + c_scale * c_ref[...]
            o_ref[...] = r.astype(out_dtype)

        if nk == 1:
            finish(lax.dot_general(a_ref[...], b_ref[...], dims, preferred_element_type=F32))
        else:
            acc_ref = refs[3 + has_c + has_after]
            k = pl.program_id(2)

            @pl.when(k == 0)
            def _():
                acc_ref[...] = jnp.zeros_like(acc_ref)

            acc_ref[...] += lax.dot_general(a_ref[...], b_ref[...], dims, preferred_element_type=F32)

            @pl.when(k == nk - 1)
            def _():
                finish(acc_ref[...])

    in_specs = [pl.BlockSpec((tk, tm), lambda i, j, k: (k, i)) if at else pl.BlockSpec((tm, tk), lambda i, j, k: (i, k)),
                pl.BlockSpec((tn, tk), lambda i, j, k: (j, k)) if bt
                else pl.BlockSpec((tk, tn), lambda i, j, k: (k, j + joff))]
    args = [a, b]
    if has_c:
        in_specs.append(pl.BlockSpec((tm, tn), lambda i, j, k: (i, j)))
        args.append(c)
    if has_after:
        in_specs.append(ANY)
        args.append(after)
    return pl.pallas_call(
        body, name=name, grid=(M // tm, N // tn, nk),
        in_specs=in_specs,
        out_specs=pl.BlockSpec((tm, tn), lambda i, j, k: (i, j)),
        out_shape=jax.ShapeDtypeStruct((M, N), out_dtype),
        scratch_shapes=[pltpu.VMEM((tm, tn), F32)] if nk > 1 else [],
        compiler_params=_params(("parallel", "parallel", "arbitrary")),
    )(*args)


def _shift_down(x, k, row):
    return jnp.where(row >= k, pltpu.roll(x, k, 0), 0.0)


def _shift_up(x, k, row):
    s = x.shape[0]
    return jnp.where(row < s - k, pltpu.roll(x, s - k, 0), 0.0)


def _fg_fwd(fgp, bf):
    S = fgp.shape[0]

    def body(f_ref, b_ref, cum_ref):
        a = f_ref[...] + b_ref[...]
        x = jnp.minimum(a, 0.0) - jnp.log(1.0 + jnp.exp(-jnp.abs(a)))
        row = lax.broadcasted_iota(jnp.int32, x.shape, 0)
        k = 1
        while k < S:
            x = x + _shift_down(x, k, row)
            k *= 2
        cum_ref[...] = x

    return pl.pallas_call(body, name="fg_fwd", out_shape=jax.ShapeDtypeStruct((S, LANES), F32),
                          compiler_params=_params(None))(fgp, bf)


def _fg_bwd(dcum, fgp, bf, dproj):
    S = fgp.shape[0]

    def body(d_ref, f_ref, b_ref, dproj_in, dfg_ref, dbf_ref):
        x = d_ref[...]
        row = lax.broadcasted_iota(jnp.int32, x.shape, 0)
        k = 1
        while k < S:
            x = x + _shift_up(x, k, row)
            k *= 2
        dfg = x * _sigmoid(-(f_ref[...] + b_ref[...]))
        dfg_ref[...] = dfg.astype(BF16)
        dbf_ref[...] = jnp.sum(dfg, axis=0, keepdims=True)

    whole = pl.BlockSpec((S, LANES), lambda i: (0, 0))
    vec = pl.BlockSpec((1, LANES), lambda i: (0, 0))
    return pl.pallas_call(
        body, name="fg_bwd", grid=(1,),
        in_specs=[whole, whole, vec, ANY],
        out_specs=(pl.BlockSpec((S, LANES), lambda i: (0, FGB)), vec),
        out_shape=(jax.ShapeDtypeStruct(dproj.shape, dproj.dtype), jax.ShapeDtypeStruct((1, LANES), F32)),
        input_output_aliases={3: 0},
        compiler_params=_params(("arbitrary",)),
    )(dcum, fgp, bf, dproj)


def _put_columns(pieces, dst, blocks, sems):
    copies = [pltpu.make_async_copy(p, dst.at[:, pl.ds(pl.multiple_of(b * LANES, LANES), LANES)], sems.at[i])
              for i, (p, b) in enumerate(zip(pieces, blocks))]
    for cp in copies:
        cp.start()
    for cp in copies:
        cp.wait()


def _colblock(S, off):
    return pl.BlockSpec((S, LANES), lambda h: (0, h + off))


def _head_row(nq, tq):
    return pl.BlockSpec((None, nq, 1, tq), lambda h: (h, 0, 0, 0))


def _lane_of(ref, h):
    lane = lax.broadcasted_iota(jnp.int32, ref.shape, 1)
    return jnp.sum(jnp.where(lane == h, ref[...], 0.0), axis=1, keepdims=True)


def _attn_fwd(qkv, rest, cum, cumr, *, tq):
    S = qkv.shape[0]
    nq = S // tq

    def body(q_ref, k_ref, v_ref, g_ref, cum_ref, cumr_ref, o_ref, y_ref, lse_ref, cc):
        cc[...] = _lane_of(cum_ref, pl.program_id(0)) * LOG2E
        half = tq // 2

        def q_step(qi, _):
            rows = pl.ds(pl.multiple_of(qi * tq, tq), tq)
            q = q_ref[rows, :]
            ci = cumr_ref[qi]

            def part(cols, q, ci, carry, masked):
                m, l, acc = carry
                s = lax.dot_general(k_ref[cols, :], q, (((1,), (1,)), ((), ())), preferred_element_type=F32)
                s = s * (SCALE * LOG2E) + ci - cc[cols, :]
                if masked:
                    s = jnp.where(lax.broadcasted_iota(jnp.int32, s.shape, 0) <= lax.broadcasted_iota(jnp.int32, s.shape, 1),
                                  s, NEG)
                m_new = jnp.maximum(m, jnp.max(s, axis=0, keepdims=True))
                a = jnp.exp2(m - m_new)
                p = jnp.exp2(s - m_new)
                l = a * l + jnp.sum(p, axis=0, keepdims=True)
                acc = a * acc + lax.dot_general(v_ref[cols, :], p.astype(BF16), (((0,), (0,)), ((), ())),
                                                preferred_element_type=F32)
                return m_new, l, acc

            def tile(kj, carry):
                return part(pl.ds(pl.multiple_of(kj * tq, tq), tq), q, ci, carry, False)

            init = (jnp.full((1, tq), NEG, F32), jnp.zeros((1, tq), F32), jnp.zeros((HEAD_DIM, tq), F32))
            carry = lax.fori_loop(0, qi, tile, init)
            c0 = pl.multiple_of(qi * tq, tq)
            m, l, acc = part(pl.ds(c0, half), q, ci, carry, True)
            late = part(pl.ds(c0 + half, half), q[half:], ci[:, half:], (m[:, half:], l[:, half:], acc[:, half:]), True)
            m, l, acc = (jnp.concatenate([x[:, :half], y], axis=1) for x, y in zip((m, l, acc), late))
            o = (acc / l).T
            g = g_ref[rows, :]
            o_ref[rows, :] = o
            y_ref[rows, :] = (o * (g * _sigmoid(g))).astype(BF16)
            lse_ref[qi] = m * LN2 + jnp.log(l)
            return 0

        lax.fori_loop(0, nq, q_step, 0)

    return pl.pallas_call(
        body, name="attn_fwd", grid=(N_HEADS,),
        in_specs=[_colblock(S, QB), _colblock(S, KB), _colblock(S, VB), _colblock(S, GB),
                  pl.BlockSpec((S, LANES), lambda h: (0, 0)), _head_row(nq, tq)],
        out_specs=(_colblock(S, 0), _colblock(S, 0), _head_row(nq, tq)),
        out_shape=(jax.ShapeDtypeStruct((S, ATT_W), F32), jax.ShapeDtypeStruct((S, D_MODEL), BF16),
                   jax.ShapeDtypeStruct((N_HEADS, nq, 1, tq), F32)),
        scratch_shapes=[pltpu.VMEM((S, 1), F32)],
        compiler_params=_params(("arbitrary",)),
    )(qkv, qkv, qkv, rest, cum, cumr)


def _attn_bwd_pre(rest, o, dy, *, tq):
    S = o.shape[0]
    nq = S // tq

    def body(g_ref, o_ref, dy_ref, do_ref, dg_ref, dl_ref):
        g = g_ref[...]
        sg = _sigmoid(g)
        ov = o_ref[...]
        dy = dy_ref[...]
        do = dy * (g * sg)
        do_ref[...] = do.astype(BF16)
        dg_ref[...] = (dy * ov * (sg * (1.0 + g * (1.0 - sg)))).astype(BF16)
        rows = lax.dot_general(jnp.ones((8, HEAD_DIM), F32), do * ov, (((1,), (1,)), ((), ())),
                               precision=lax.Precision.HIGHEST, preferred_element_type=F32)
        for qi in range(nq):
            dl_ref[qi] = rows[0:1, qi * tq:(qi + 1) * tq]

    att = jax.ShapeDtypeStruct((S, ATT_W), BF16)
    return pl.pallas_call(
        body, name="attn_bwd_pre", grid=(N_HEADS,),
        in_specs=[_colblock(S, GB), _colblock(S, 0), _colblock(S, 0)],
        out_specs=(_colblock(S, 0), _colblock(S, QKV_W // LANES), _head_row(nq, tq)),
        out_shape=(att, jax.ShapeDtypeStruct((S, ALL_W), BF16), jax.ShapeDtypeStruct((N_HEADS, nq, 1, tq), F32)),
        compiler_params=_params(("parallel",)),
    )(rest, o, dy)


def _attn_bwd(qkv, do, a2, delta, cum, dproj, *, tq):
    S = qkv.shape[0]
    nq = S // tq
    tdot = (((0,), (0,)), ((), ()))
    ndot = (((1,), (1,)), ((), ()))

    def body(q_ref, k_ref, v_ref, do_ref, a_ref, dl_ref, cum_ref, dproj_in, dproj_ref, dsq_ref, dsk_ref,
             dqa, dq_ref, dk_ref, dv_ref, cc, sems):
        h = pl.program_id(0)
        dqa[...] = jnp.zeros_like(dqa)
        dsq_ref[...] = jnp.zeros_like(dsq_ref)
        cc[...] = _lane_of(cum_ref, h) * LOG2E

        @pl.when(h == 0)
        def _():
            dsk_ref[...] = jnp.zeros_like(dsk_ref)

        half = tq // 2
        mine = lax.broadcasted_iota(jnp.int32, (tq, LANES), 1) == h

        def k_step(kj, _):
            c0 = pl.multiple_of(kj * tq, tq)
            cols = pl.ds(c0, tq)
            k = k_ref[cols, :]
            v = v_ref[cols, :]
            cj = cc[cols, :]

            def part(k, v, cj, qi, q0, first, carry, masked):
                dk, dv, ksum = carry
                nqs = tq - first
                rows = pl.ds(q0, nqs)
                q = q_ref[rows, :]
                dot = do_ref[rows, :]
                s = lax.dot_general(k, q, ndot, preferred_element_type=F32)
                p = jnp.exp2(s * (SCALE * LOG2E) + a_ref[qi][:, first:] - cj)
                if masked:
                    p = jnp.where(lax.broadcasted_iota(jnp.int32, p.shape, 0) <= lax.broadcasted_iota(jnp.int32, p.shape, 1),
                                  p, 0.0)
                dv = dv + jnp.dot(p.astype(BF16), dot, preferred_element_type=F32)
                dp = lax.dot_general(v, dot, ndot, preferred_element_type=F32)
                ds = p * (dp - dl_ref[qi][:, first:])
                over_keys = jnp.sum(ds, axis=0, keepdims=True)
                if first:
                    over_keys = jnp.concatenate([jnp.zeros((1, first), F32), over_keys], axis=1)
                dsq_ref[qi] += over_keys
                ksum = ksum + jnp.sum(ds, axis=1, keepdims=True)
                dsb = (ds * SCALE).astype(BF16)
                dk = dk + jnp.dot(dsb, q, preferred_element_type=F32)
                dqa[rows, :] += lax.dot_general(dsb, k, tdot, preferred_element_type=F32)
                return dk, dv, ksum

            def zeros(n):
                return (jnp.zeros((n, HEAD_DIM), F32), jnp.zeros((n, HEAD_DIM), F32), jnp.zeros((n, 1), F32))

            early = part(k[:half], v[:half], cj[:half], kj, c0, 0, zeros(half), True)
            late = part(k[half:], v[half:], cj[half:], kj, c0 + half, half, zeros(half), True)
            carry = tuple(jnp.concatenate([x, y], axis=0) for x, y in zip(early, late))
            dk, dv, ksum = lax.fori_loop(
                kj + 1, nq, lambda qi, c: part(k, v, cj, qi, pl.multiple_of(qi * tq, tq), 0, c, False), carry)
            dk_ref[cols, :] = dk.astype(BF16)
            dv_ref[cols, :] = dv.astype(BF16)
            dsk_ref[cols, :] += jnp.where(mine, ksum, 0.0)
            return 0

        lax.fori_loop(0, nq, k_step, 0)
        dq_ref[...] = dqa[...].astype(BF16)
        _put_columns([dq_ref, dk_ref, dv_ref], dproj_ref, [QB + h, KB + h, VB + h], sems)

    head = pltpu.VMEM((S, HEAD_DIM), BF16)
    return pl.pallas_call(
        body, name="attn_bwd", grid=(N_HEADS,),
        in_specs=[_colblock(S, QB), _colblock(S, KB), _colblock(S, VB), _colblock(S, 0),
                  _head_row(nq, tq), _head_row(nq, tq), pl.BlockSpec((S, LANES), lambda h: (0, 0)), ANY],
        out_specs=(ANY, _head_row(nq, tq), pl.BlockSpec((S, LANES), lambda h: (0, 0))),
        out_shape=(jax.ShapeDtypeStruct(dproj.shape, dproj.dtype), jax.ShapeDtypeStruct((N_HEADS, nq, 1, tq), F32),
                   jax.ShapeDtypeStruct((S, LANES), F32)),
        input_output_aliases={7: 0},
        scratch_shapes=[pltpu.VMEM((S, HEAD_DIM), F32), head, head, head, pltpu.VMEM((S, 1), F32),
                        pltpu.SemaphoreType.DMA((3,))],
        compiler_params=_params(("arbitrary",)),
    )(qkv, qkv, qkv, do, a2, delta, cum, dproj)


def _conv_taps(u, w_ref, row):
    return (w_ref[0:1, :] * _shift_down(u, 2, row) + w_ref[1:2, :] * _shift_down(u, 1, row)) + w_ref[2:3, :] * u


def _conv_fwd(rest, cw, y):
    S = rest.shape[0]

    def body(cb_ref, cc_ref, ch_ref, g_ref, w_ref, y_in, y_ref):
        row = lax.broadcasted_iota(jnp.int32, (S, LANES), 0)
        g = g_ref[...]
        y = _conv_taps(cc_ref[...] * ch_ref[...], w_ref, row)
        y_ref[...] = ((cb_ref[...] * y) * (g * _sigmoid(g))).astype(BF16)

    return pl.pallas_call(
        body, name="conv_fwd", grid=(CONV_W // LANES,),
        in_specs=[_colblock(S, CBB), _colblock(S, CCB), _colblock(S, CHB), _colblock(S, CGB),
                  pl.BlockSpec((CONV_TAPS, LANES), lambda j: (0, j)), ANY],
        out_specs=_colblock(S, ATT_W // LANES),
        out_shape=jax.ShapeDtypeStruct(y.shape, y.dtype),
        input_output_aliases={5: 0},
        compiler_params=_params(("parallel",)),
    )(rest, rest, rest, rest, cw, y)


def _conv_bwd(rest, dy, cw, dproj):
    S = rest.shape[0]

    def body(cb_ref, cc_ref, ch_ref, g_ref, dy_ref, w_ref, dproj_in, dproj_ref, dw_ref,
             dcb_ref, dcc_ref, dch_ref, dg_ref, sems):
        row = lax.broadcasted_iota(jnp.int32, (S, LANES), 0)
        g = g_ref[...]
        sg = _sigmoid(g)
        silu = g * sg
        cb, cc, ch, dy = cb_ref[...], cc_ref[...], ch_ref[...], dy_ref[...]
        u = cc * ch
        u1 = _shift_down(u, 1, row)
        u2 = _shift_down(u, 2, row)
        y = (w_ref[0:1, :] * u2 + w_ref[1:2, :] * u1) + w_ref[2:3, :] * u
        dcb_ref[...] = (dy * silu * y).astype(BF16)
        dg_ref[...] = (dy * (cb * y) * (sg * (1.0 + g * (1.0 - sg)))).astype(BF16)
        dyv = dy * silu * cb
        du = w_ref[2:3, :] * dyv + w_ref[1:2, :] * _shift_up(dyv, 1, row) + w_ref[0:1, :] * _shift_up(dyv, 2, row)
        dcc_ref[...] = (du * ch).astype(BF16)
        dch_ref[...] = (du * cc).astype(BF16)
        dw_ref[0:1, :] = jnp.sum(dyv * u2, axis=0, keepdims=True)
        dw_ref[1:2, :] = jnp.sum(dyv * u1, axis=0, keepdims=True)
        dw_ref[2:3, :] = jnp.sum(dyv * u, axis=0, keepdims=True)
        j = pl.program_id(0)
        first = QKV_W // LANES
        _put_columns([dcb_ref, dcc_ref, dch_ref, dg_ref], dproj_ref,
                     [first + CBB + j, first + CCB + j, first + CHB + j, first + CGB + j], sems)

    piece = pltpu.VMEM((S, LANES), BF16)
    return pl.pallas_call(
        body, name="conv_bwd", grid=(CONV_W // LANES,),
        in_specs=[_colblock(S, CBB), _colblock(S, CCB), _colblock(S, CHB), _colblock(S, CGB),
                  _colblock(S, ATT_W // LANES), pl.BlockSpec((CONV_TAPS, LANES), lambda j: (0, j)), ANY],
        out_specs=(ANY, pl.BlockSpec((CONV_TAPS, LANES), lambda j: (0, j))),
        out_shape=(jax.ShapeDtypeStruct(dproj.shape, dproj.dtype), jax.ShapeDtypeStruct((CONV_TAPS, CONV_W), F32)),
        input_output_aliases={6: 0},
        scratch_shapes=[piece] * 4 + [pltpu.SemaphoreType.DMA((4,))],
        compiler_params=_params(("arbitrary",)),
    )(rest, rest, rest, rest, dy, cw, dproj)


def _pool_z(u, grp, row):
    s2 = u + _shift_down(u, 1, row)
    s4 = s2 + _shift_down(s2, 2, row)
    s8 = s4 + _shift_down(s4, 4, row)
    s16 = s8 + _shift_down(s8, 8, row)
    sw = jnp.where(grp == 0, s2, jnp.where(grp == 1, s4, jnp.where(grp == 2, s8, s16)))
    return sw / _pool_count(grp, row) - u


def _pool_count(grp, row):
    return jnp.minimum(row + 1, 2 << grp).astype(F32)


def _pool_fwd(rest, pw, scale, y):
    S = rest.shape[0]

    def body(u_ref, g_ref, pw_ref, sc_ref, y_in, y_ref):
        grp = pl.program_id(0)
        row = lax.broadcasted_iota(jnp.int32, (S, LANES), 0)
        z = _pool_z(u_ref[...], grp, row)
        zp = jnp.dot(z.astype(BF16), pw_ref[...].astype(BF16), preferred_element_type=F32)
        g = g_ref[...]
        y_ref[...] = ((zp * sc_ref[...]) * (g * _sigmoid(g))).astype(BF16)

    return pl.pallas_call(
        body, name="pool_fwd", grid=(N_POOL,),
        in_specs=[_colblock(S, PUB), _colblock(S, PGB),
                  pl.BlockSpec((None, POOL_GROUP, POOL_GROUP), lambda j: (j, 0, 0)),
                  pl.BlockSpec((1, LANES), lambda j: (0, j)), ANY],
        out_specs=_colblock(S, (ATT_W + CONV_W) // LANES),
        out_shape=jax.ShapeDtypeStruct(y.shape, y.dtype),
        input_output_aliases={4: 0},
        compiler_params=_params(("parallel",)),
    )(rest, rest, pw, scale, y)


def _pool_bwd(rest, dy, pw, scale, dproj):
    S = rest.shape[0]

    def body(u_ref, g_ref, dy_ref, pw_ref, sc_ref, dproj_in, dproj_ref, dpw_ref, dsc_ref, du_ref, dg_ref, sems):
        grp = pl.program_id(0)
        row = lax.broadcasted_iota(jnp.int32, (S, LANES), 0)
        z = _pool_z(u_ref[...], grp, row).astype(BF16)
        pwb = pw_ref[...].astype(BF16)
        zp = jnp.dot(z, pwb, preferred_element_type=F32)
        g = g_ref[...]
        sg = _sigmoid(g)
        silu = g * sg
        dy = dy_ref[...]
        sc = sc_ref[...]
        dsc_ref[...] = jnp.sum(dy * silu * zp, axis=0, keepdims=True)
        dg_ref[...] = (dy * (zp * sc) * (sg * (1.0 + g * (1.0 - sg)))).astype(BF16)
        dzp = (dy * silu * sc).astype(BF16)
        dpw_ref[...] = lax.dot_general(z, dzp, (((0,), (0,)), ((), ())), preferred_element_type=F32)
        dz = lax.dot_general(dzp, pwb, (((1,), (1,)), ((), ())), preferred_element_type=F32)
        f1 = dz / _pool_count(grp, row)
        f2 = f1 + _shift_up(f1, 1, row)
        f4 = f2 + _shift_up(f2, 2, row)
        f8 = f4 + _shift_up(f4, 4, row)
        f16 = f8 + _shift_up(f8, 8, row)
        fw = jnp.where(grp == 0, f2, jnp.where(grp == 1, f4, jnp.where(grp == 2, f8, f16)))
        du_ref[...] = (fw - dz).astype(BF16)
        first = QKV_W // LANES
        _put_columns([du_ref, dg_ref], dproj_ref, [first + PUB + grp, first + PGB + grp], sems)

    piece = pltpu.VMEM((S, LANES), BF16)
    return pl.pallas_call(
        body, name="pool_bwd", grid=(N_POOL,),
        in_specs=[_colblock(S, PUB), _colblock(S, PGB), _colblock(S, (ATT_W + CONV_W) // LANES),
                  pl.BlockSpec((None, POOL_GROUP, POOL_GROUP), lambda j: (j, 0, 0)),
                  pl.BlockSpec((1, LANES), lambda j: (0, j)), ANY],
        out_specs=(ANY, pl.BlockSpec((None, POOL_GROUP, POOL_GROUP), lambda j: (j, 0, 0)),
                   pl.BlockSpec((1, LANES), lambda j: (0, j))),
        out_shape=(jax.ShapeDtypeStruct(dproj.shape, dproj.dtype),
                   jax.ShapeDtypeStruct((N_POOL, POOL_GROUP, POOL_GROUP), F32), jax.ShapeDtypeStruct((1, POOL_W), F32)),
        input_output_aliases={5: 0},
        scratch_shapes=[piece] * 2 + [pltpu.SemaphoreType.DMA((2,))],
        compiler_params=_params(("arbitrary",)),
    )(rest, rest, dy, pw, scale, dproj)


def _ln_fwd(z, g, b, *, ts):
    S, D = z.shape

    def body(z_ref, g_ref, b_ref, o_ref, ob_ref, obt_ref, xh_ref, rs_ref):
        zz = z_ref[...]
        mu = jnp.mean(zz, axis=1, keepdims=True)
        zc = zz - mu
        rstd = lax.rsqrt(jnp.mean(zc * zc, axis=1, keepdims=True) + LN_EPS)
        xh = zc * rstd
        out = xh * g_ref[...] + b_ref[...]
        o_ref[...] = out
        ob_ref[...] = out.astype(BF16)
        obt_ref[...] = out.T.astype(BF16)
        xh_ref[...] = xh
        rs_ref[...] = rstd

    rowblk = pl.BlockSpec((ts, D), lambda i: (i, 0))
    vec = pl.BlockSpec((1, D), lambda i: (0, 0))
    return pl.pallas_call(
        body, name="ln_fwd", grid=(S // ts,),
        in_specs=[rowblk, vec, vec],
        out_specs=(rowblk, rowblk, pl.BlockSpec((D, ts), lambda i: (0, i)), rowblk,
                   pl.BlockSpec((ts, 1), lambda i: (i, 0))),
        out_shape=(jax.ShapeDtypeStruct((S, D), F32), jax.ShapeDtypeStruct((S, D), BF16),
                   jax.ShapeDtypeStruct((D, S), BF16), jax.ShapeDtypeStruct((S, D), F32),
                   jax.ShapeDtypeStruct((S, 1), F32)),
        compiler_params=_params(("parallel",)),
    )(z, g, b)


def _prep(x, after, *, ts):
    S, D = x.shape

    def body(x_ref, after_ref, xb_ref, xbt_ref):
        xb_ref[...] = x_ref[...].astype(BF16)
        xbt_ref[...] = x_ref[...].T.astype(BF16)

    return pl.pallas_call(
        body, name="prep", grid=(S // ts,),
        in_specs=[pl.BlockSpec((ts, D), lambda i: (i, 0)), ANY],
        out_specs=(pl.BlockSpec((ts, D), lambda i: (i, 0)), pl.BlockSpec((D, ts), lambda i: (0, i))),
        out_shape=(jax.ShapeDtypeStruct((S, D), BF16), jax.ShapeDtypeStruct((D, S), BF16)),
        compiler_params=_params(("parallel",)),
    )(x, after)


def _ln_bwd(dout, xhat, rstd, g, *, ts):
    S, D = dout.shape

    def body(d_ref, xh_ref, rs_ref, g_ref, dz_ref, dzb_ref, dg_ref, db_ref):
        @pl.when(pl.program_id(0) == 0)
        def _():
            dg_ref[...] = jnp.zeros_like(dg_ref)
            db_ref[...] = jnp.zeros_like(db_ref)

        d = d_ref[...]
        xh = xh_ref[...]
        dxh = d * g_ref[...]
        m1 = jnp.mean(dxh, axis=1, keepdims=True)
        m2 = jnp.mean(dxh * xh, axis=1, keepdims=True)
        dz = rs_ref[...] * (dxh - m1 - xh * m2)
        dz_ref[...] = dz
        dzb_ref[...] = dz.astype(BF16)
        dg_ref[...] += jnp.sum(d * xh, axis=0, keepdims=True)
        db_ref[...] += jnp.sum(d, axis=0, keepdims=True)

    rowblk = pl.BlockSpec((ts, D), lambda i: (i, 0))
    vec = pl.BlockSpec((1, D), lambda i: (0, 0))
    return pl.pallas_call(
        body, name="ln_bwd", grid=(S // ts,),
        in_specs=[rowblk, rowblk, pl.BlockSpec((ts, 1), lambda i: (i, 0)), vec],
        out_specs=(rowblk, rowblk, vec, vec),
        out_shape=(jax.ShapeDtypeStruct((S, D), F32), jax.ShapeDtypeStruct((S, D), BF16),
                   jax.ShapeDtypeStruct((1, D), F32), jax.ShapeDtypeStruct((1, D), F32)),
        compiler_params=_params(("arbitrary",)),
    )(dout, xhat, rstd, g)


def _loss_head(y, tgt, *, ts):
    S, D = y.shape

    def body(y_ref, t_ref, l_ref, dy_ref):
        @pl.when(pl.program_id(0) == 0)
        def _():
            l_ref[...] = jnp.zeros_like(l_ref)

        e = y_ref[...] - t_ref[...]
        dy_ref[...] = e * (1.0 / D)
        rowloss = jnp.sum(e * e, axis=1, keepdims=True) * (0.5 / D)
        l_ref[...] += jnp.sum(rowloss, axis=0, keepdims=True)

    rowblk = pl.BlockSpec((ts, D), lambda i: (i, 0))
    return pl.pallas_call(
        body, name="loss_head", grid=(S // ts,),
        in_specs=[rowblk, rowblk],
        out_specs=(pl.BlockSpec((1, 1), lambda i: (0, 0)), rowblk),
        out_shape=(jax.ShapeDtypeStruct((1, 1), F32), jax.ShapeDtypeStruct((S, D), F32)),
        compiler_params=_params(("arbitrary",)),
    )(y, tgt)


def _adamw(lands, srcs, w, m, v, *, name, first=0, into=None):
    R, C = w.shape
    npart = len(lands)
    rp = lands[0].shape[1]
    assert R % rp == 0 and all(p.shape == (N_DEV, rp, C) for p in lands)
    tiled = rp % 8 == 0
    assert tiled or (first == 0 and npart * rp == R and into is None)
    tr = min(ADAMW_BLOCK_ELEMS // (LANES * pl.cdiv(C, LANES)), rp) if tiled else rp
    assert rp % tr == 0
    steps = rp // tr
    c1 = 1.0 - ADAM_B1 ** ADAM_STEP
    c2 = 1.0 - ADAM_B2 ** ADAM_STEP
    slabbed = [p.ndim == 3 for p in srcs]
    n_into = 0 if into is None else 4

    def body(*refs):
        l_refs, s_refs = refs[:npart], refs[npart:2 * npart]
        w_ref, m_ref, v_ref = refs[2 * npart:2 * npart + 3]
        g_ref, d_ref, nm_ref, nv_ref = refs[2 * npart + 3 + n_into:]
        x, y, c = _my_place()
        me = 4 * x + 2 * y + c

        def update(l, rows):
            g = None
            for j in range(N_DEV):
                own = s_refs[l][j] if slabbed[l] else s_refs[l][...]
                p = jnp.where(me == j, own, l_refs[l][j]).astype(F32)
                g = p if g is None else g + p
            nm = ADAM_B1 * m_ref[rows, :] + (1.0 - ADAM_B1) * g
            nv = ADAM_B2 * v_ref[rows, :] + (1.0 - ADAM_B2) * (g * g)
            g_ref[rows, :] = g
            nm_ref[rows, :] = nm
            nv_ref[rows, :] = nv
            d_ref[rows, :] = -ADAM_LR * ((nm / c1) / (jnp.sqrt(nv / c2) + ADAM_EPS) + ADAM_WD * w_ref[rows, :])

        for l in range(npart):
            if tiled:
                pl.when(pl.program_id(0) == l)(lambda l=l: update(l, slice(None)))
            else:
                update(l, slice(l * rp, (l + 1) * rp))

    def part_spec(l, p):
        if p.ndim == 3:
            return pl.BlockSpec((N_DEV, tr, C), lambda q, i: (0, jnp.where(q == l, i, 0), 0))
        return pl.BlockSpec((tr, C), lambda q, i: (jnp.where(q == l, i, 0), 0))

    if tiled:
        blk = pl.BlockSpec((tr, C), lambda q, i: ((first + q) * steps + i, 0))
    else:
        blk = pl.BlockSpec((R, C), lambda q, i: (0, 0))
    out = jax.ShapeDtypeStruct((R, C), F32)
    n_in = 2 * npart + 3
    return pl.pallas_call(
        body, name=name, grid=(npart, steps) if tiled else (1, 1),
        in_specs=[part_spec(l, p) for l, p in enumerate(lands)] + [part_spec(l, p) for l, p in enumerate(srcs)]
        + [blk, blk, blk] + [ANY] * n_into,
        out_specs=(blk, blk, blk, blk),
        out_shape=(out, out, out, out),
        input_output_aliases={n_in + i: i for i in range(n_into)},
        compiler_params=_params(("arbitrary", "arbitrary")),
    )(*lands, *srcs, w, m, v, *(into or ()))


def _my_place():
    return lax.axis_index("x"), lax.axis_index("y"), lax.axis_index("c")


HBM = pl.BlockSpec(memory_space=pltpu.HBM)
SEM = pl.BlockSpec(memory_space=pltpu.SEMAPHORE)
DATAFLOW = pltpu.SideEffectType.DATAFLOW_SIDE_EFFECTING
EVERYONE = (1, 2, 3, 4, 5, 6, 7)
SIBLING = 1
SAME_CORE = (4, 2, 6)


def _xchg_copies(src, land, send_sem, recv_sem, gather, peers):
    x, y, c = _my_place()
    me = 4 * x + 2 * y + c
    copies = []
    for k, bits in enumerate(peers):
        px, py, pc = x ^ ((bits >> 2) & 1), y ^ ((bits >> 1) & 1), c ^ (bits & 1)
        peer = 4 * px + 2 * py + pc
        for a in range(len(src)):
            i = a * len(peers) + k
            out = src[a] if gather[a] else src[a].at[peer]
            sent, landed = [pltpu.make_async_remote_copy(
                src_ref=out, dst_ref=land[a].at[slab], send_sem=send_sem.at[i], recv_sem=recv_sem.at[i],
                device_id=(px, py, pc), device_id_type=MESH) for slab in (me, peer)]
            copies.append((sent, landed))
    return copies


def _xchg_start(srcs, *, gather, name, peers=EVERYONE, after=None):
    n = len(srcs)
    follows = [] if after is None else [after]
    lands = [lax.empty(((N_DEV,) + v.shape) if g else v.shape, v.dtype) for v, g in zip(srcs, gather)]

    def body(*refs):
        src, land = refs[:n], refs[n:2 * n]
        send_sem, recv_sem = refs[2 * n + len(follows)], refs[2 * n + len(follows) + 1]
        token = refs[-1]
        for sent, _ in _xchg_copies(src, land, send_sem, recv_sem, gather, peers):
            sent.start()
        token[...] = jnp.zeros_like(token)

    sems = pltpu.SemaphoreType.DMA((n * len(peers),))
    outs = pl.pallas_call(
        body, name=name,
        out_shape=(sems, sems, *[pltpu.HBM(v.shape, v.dtype) for v in srcs + lands],
                   jax.ShapeDtypeStruct((8, LANES), F32)),
        in_specs=[HBM] * (2 * n) + [ANY] * len(follows),
        out_specs=(SEM, SEM, *[HBM] * (2 * n), pl.BlockSpec(memory_space=pltpu.VMEM)),
        input_output_aliases={i: 2 + i for i in range(2 * n)},
        compiler_params=pltpu.CompilerParams(has_side_effects=DATAFLOW),
    )(*[pltpu.with_memory_space_constraint(v, pltpu.HBM) for v in srcs + lands], *follows)
    return dict(send=outs[0], recv=outs[1], srcs=list(outs[2:2 + n]), lands=list(outs[2 + n:2 + 2 * n]),
                token=outs[-1], gather=gather, peers=peers)


def _xchg_wait(h, after, *, name):
    n = len(h["srcs"])
    gather, peers = h["gather"], h["peers"]
    after = list(after) if isinstance(after, (list, tuple)) else [after]

    def body(*refs):
        src, land = refs[:n], refs[n:2 * n]
        send_sem, recv_sem = refs[2 * n], refs[2 * n + 1]
        for sent, landed in _xchg_copies(src, land, send_sem, recv_sem, gather, peers):
            sent.wait_send()
            landed.wait_recv()

    thru = h["srcs"] + h["lands"]
    outs = pl.pallas_call(
        body, name=name,
        out_shape=[pltpu.HBM(v.shape, v.dtype) for v in thru],
        in_specs=[HBM] * (2 * n) + [SEM, SEM] + [ANY] * len(after),
        out_specs=[HBM] * (2 * n),
        input_output_aliases={i: i for i in range(2 * n)},
        compiler_params=pltpu.CompilerParams(has_side_effects=DATAFLOW),
    )(*thru, h["send"], h["recv"], *after)
    return list(outs[:n]), list(outs[n:])


def _relay_copies(land, send_sem, recv_sem):
    x, y, c = _my_place()
    me = 4 * x + 2 * y + c
    copies = []
    for k, bits in enumerate(SAME_CORE):
        for a in range(len(land)):
            i = a * len(SAME_CORE) + k
            sent, landed = [pltpu.make_async_remote_copy(
                src_ref=land[a].at[me ^ bits], dst_ref=land[a].at[slab], send_sem=send_sem.at[i], recv_sem=recv_sem.at[i],
                device_id=(x, y, 1 - c), device_id_type=MESH) for slab in (me ^ bits, me ^ bits ^ SIBLING)]
            copies.append((sent, landed))
    return copies


def _relay_start(lands, *, name):
    n = len(lands)

    def body(*refs):
        land = refs[:n]
        send_sem, recv_sem = refs[n], refs[n + 1]
        token = refs[-1]
        for sent, _ in _relay_copies(land, send_sem, recv_sem):
            sent.start()
        token[...] = jnp.zeros_like(token)

    sems = pltpu.SemaphoreType.DMA((n * len(SAME_CORE),))
    outs = pl.pallas_call(
        body, name=name,
        out_shape=(sems, sems, *[pltpu.HBM(v.shape, v.dtype) for v in lands], jax.ShapeDtypeStruct((8, LANES), F32)),
        in_specs=[HBM] * n,
        out_specs=(SEM, SEM, *[HBM] * n, pl.BlockSpec(memory_space=pltpu.VMEM)),
        input_output_aliases={i: 2 + i for i in range(n)},
        compiler_params=pltpu.CompilerParams(has_side_effects=DATAFLOW),
    )(*[pltpu.with_memory_space_constraint(v, pltpu.HBM) for v in lands])
    return dict(send=outs[0], recv=outs[1], lands=list(outs[2:2 + n]), token=outs[-1])


def _relay_wait(h, after, *, name):
    n = len(h["lands"])

    def body(*refs):
        land = refs[:n]
        send_sem, recv_sem = refs[n], refs[n + 1]
        for sent, landed in _relay_copies(land, send_sem, recv_sem):
            sent.wait_send()
            landed.wait_recv()

    outs = pl.pallas_call(
        body, name=name,
        out_shape=[pltpu.HBM(v.shape, v.dtype) for v in h["lands"]],
        in_specs=[HBM] * n + [SEM, SEM, ANY],
        out_specs=[HBM] * n,
        input_output_aliases={i: i for i in range(n)},
        compiler_params=pltpu.CompilerParams(has_side_effects=DATAFLOW),
    )(*h["lands"], h["send"], h["recv"], after)
    return list(outs)


def _with_own(land, own, me):
    slot = lax.broadcasted_iota(jnp.int32, land.shape, 0)
    return jnp.where(slot == me, own if own.ndim == land.ndim else own[None], land)


def _w_in_segments(j, cols):
    lo, hi = j * cols, (j + 1) * cols
    segs = []
    for a, b, where, shift in ((0, FG_AT, "main", 0), (FG_AT, FG_AT + N_HEADS, "fg", -FG_AT),
                               (FG_AT + N_HEADS, IN_W, "main", -N_HEADS)):
        s0, s1 = max(lo, a), min(hi, b)
        if s0 < s1:
            segs.append((s0 - lo, s1 - s0, where, s0 + shift))
    return segs


SLAB_W = 1024


def _window(off, width):
    base = min((off // LANES) * LANES, width - SLAB_W)
    return base, off - base


def _w_unpack(land, own, *, tr):
    _, D, cols = land.shape
    assert cols + LANES - 1 <= SLAB_W

    def body(land_ref, own_ref, main_ref, slab):
        x, y, c = _my_place()
        me = 4 * x + 2 * y + c
        lane = lax.broadcasted_iota(jnp.int32, (tr, SLAB_W), 1)
        main_ref[...] = jnp.zeros_like(main_ref)
        slab[...] = jnp.zeros_like(slab)
        for j in range(N_DEV):
            slab[:, :cols] = jnp.where(me == j, own_ref[...], land_ref[j])
            v = slab[...]
            segs = _w_in_segments(j, cols)
            for lo, n, where, dst in segs:
                part = v if len(segs) == 1 else jnp.where((lane >= lo) & (lane < lo + n), v, jnp.zeros_like(v))
                if where == "fg":
                    main_ref[:, MAIN_W:] = pltpu.roll(part, (SLAB_W - lo) % SLAB_W, 1)[:, :LANES]
                else:
                    base, r = _window(dst - lo, MAIN_W)
                    assert 0 <= r and lo + n + r <= SLAB_W
                    main_ref[:, base:base + SLAB_W] += pltpu.roll(part, r, 1)

    return pl.pallas_call(
        body, name="w_unpack", grid=(D // tr,),
        in_specs=[pl.BlockSpec((N_DEV, tr, cols), lambda i: (0, i, 0)), pl.BlockSpec((tr, cols), lambda i: (i, 0))],
        out_specs=pl.BlockSpec((tr, ALL_W), lambda i: (i, 0)),
        out_shape=jax.ShapeDtypeStruct((D, ALL_W), BF16),
        scratch_shapes=[pltpu.VMEM((tr, SLAB_W), BF16)],
        compiler_params=_params(("parallel",)),
    )(land, own)


def _g_pack(dw_main, dw_fg, cols, *, tr):
    D = dw_main.shape[0]

    def body(main_ref, fg_ref, out_ref, slab):
        lane = lax.broadcasted_iota(jnp.int32, (tr, SLAB_W), 1)
        slab[...] = jnp.zeros_like(slab)
        slab[:, :LANES] = fg_ref[...]
        fg = slab[...]
        for j in range(N_DEV):
            v = None
            for lo, n, where, src in _w_in_segments(j, cols):
                if where == "fg":
                    part = pltpu.roll(fg, lo, 1)
                else:
                    base, r = _window(src - lo, MAIN_W)
                    assert 0 <= r and lo + n + r <= SLAB_W
                    part = pltpu.roll(main_ref[:, base:base + SLAB_W], (SLAB_W - r) % SLAB_W, 1)
                v = part if v is None else jnp.where((lane >= lo) & (lane < lo + n), part, v)
            out_ref[j] = v[:, :cols]

    return pl.pallas_call(
        body, name="g_pack", grid=(D // tr,),
        in_specs=[pl.BlockSpec((tr, MAIN_W), lambda i: (i, 0)), pl.BlockSpec((tr, LANES), lambda i: (i, 0))],
        out_specs=pl.BlockSpec((N_DEV, tr, cols), lambda i: (0, i, 0)),
        out_shape=jax.ShapeDtypeStruct((N_DEV, D, cols), BF16),
        scratch_shapes=[pltpu.VMEM((tr, SLAB_W), BF16)],
        compiler_params=_params(("parallel",)),
    )(dw_main, dw_fg)


TQ = 1024
TS = 512
W_TILE_ROWS = 256


def _layer_fwd_proj(xb, w_in):
    qkv = _mm(xb, w_in, out_dtype=BF16, tm=2048, tn=512, tk=D_MODEL, name="mm_qkv", n_cols=QKV_W)
    rest = _mm(xb, w_in, out_dtype=F32, tm=2048, tn=512, tk=D_MODEL, name="mm_rest", n_off=QKV_W, n_cols=REST_W)
    fgp = _mm(xb, w_in, out_dtype=F32, tm=1024, tn=LANES, tk=D_MODEL, name="mm_fg", n_off=MAIN_W, n_cols=LANES)
    return qkv, rest, fgp


def _layer_fwd_mix(x, xbt, proj, wl):
    qkv, rest, fgp = proj
    cum = _fg_fwd(fgp, wl["b_f"])
    S = x.shape[0]
    cumr = (cum[:, :N_HEADS].T * LOG2E).reshape(N_HEADS, S // TQ, 1, TQ)
    o, y, lse = _attn_fwd(qkv, rest, cum, cumr, tq=TQ)
    y = _conv_fwd(rest, wl["conv_w"], y)
    y = _pool_fwd(rest, wl["pool_w"], wl["pool_scale"], y)
    z = _mm(y, wl["w_out"], out_dtype=F32, tm=2048, tn=512, tk=D_MODEL, name="mm_out", c=x, c_scale=ALPHA)
    out, outb, outbt, xhat, rstd = _ln_fwd(z, wl["ln_g"], wl["ln_b"], ts=TS)
    saved = dict(xbt=xbt, qkv=qkv, rest=rest, fgp=fgp, cum=cum, cumr=cumr, o=o, lse=lse, y=y, xhat=xhat, rstd=rstd)
    return out, outb, outbt, saved


def _layer_bwd_mix(dout, sv, wl):
    S = dout.shape[0]
    dz, dzb, dln_g, dln_b = _ln_bwd(dout, sv["xhat"], sv["rstd"], wl["ln_g"], ts=TS)
    dy = _mm(dzb, wl["w_out"], out_dtype=F32, tm=2048, tn=512, tk=D_MODEL, name="mm_dy", bt=True)
    dw_out = _mm(sv["y"], dzb, out_dtype=BF16, tm=512, tn=D_MODEL, tk=2048, name="mm_dw_out", at=True)
    rest = sv["rest"]
    do, dproj, delta = _attn_bwd_pre(rest, sv["o"], dy, tq=TQ)
    a2 = sv["cumr"] - sv["lse"] * LOG2E
    dproj, ds_q, ds_k = _attn_bwd(sv["qkv"], do, a2, delta, sv["cum"], dproj, tq=TQ)
    dcum = jnp.pad(ds_q.reshape(N_HEADS, S).T, ((0, 0), (0, LANES - N_HEADS))) - ds_k
    dproj, db_f = _fg_bwd(dcum, sv["fgp"], wl["b_f"], dproj)
    dproj, dconv_w = _conv_bwd(rest, dy, wl["conv_w"], dproj)
    dproj, dpool_w, dpool_scale = _pool_bwd(rest, dy, wl["pool_w"], wl["pool_scale"], dproj)
    grads = dict(b_f=db_f[:, :N_HEADS], conv_w=dconv_w, pool_w=dpool_w, pool_scale=dpool_scale, w_out=dw_out,
                 ln_g=dln_g, ln_b=dln_b)
    return grads, (dproj, dz)


def _layer_bwd_w_in(mid, sv):
    dproj, dz = mid
    xt = sv["xbt"]
    S = xt.shape[1]
    dw_main = _mm(xt, dproj, out_dtype=BF16, tm=1024, tn=512, tk=S, name="mm_dw_main", n_cols=MAIN_W)
    dw_fg = _mm(xt, dproj, out_dtype=BF16, tm=1024, tn=LANES, tk=2048, name="mm_dw_fg", n_off=MAIN_W, n_cols=LANES)
    return (dw_main, dw_fg), (dproj, dz)


def _layer_bwd_input(ctx, wl, after):
    dproj, dz = ctx
    return _mm(dproj, wl["w_in"], out_dtype=F32, tm=1024, tn=256, tk=ALL_W, name="mm_dx", c=dz, c_scale=ALPHA, bt=True,
               after=after)


def kernel(x, w_in, b_f, conv_w, pool_w, pool_scale, w_out, ln_g, ln_b, loss_target, m_w_in, m_b_f, m_conv_w, m_pool_w, m_pool_scale, m_w_out, m_ln_g, m_ln_b, v_w_in, v_b_f, v_conv_w, v_pool_w, v_pool_scale, v_w_out, v_ln_g, v_ln_b):
    L, D, cols = w_in.shape
    rows_out = w_out.shape[1]
    ccols = conv_w.shape[2]
    mx, my, mc = _my_place()
    me = 4 * mx + 2 * my + mc

    gathers = []
    for l in range(L):
        gathers.append(_xchg_start([w_in[l].astype(BF16), w_out[l].astype(BF16), conv_w[l]], gather=[True] * 3,
                                   name=f"gather_start_{l}",
                                   peers=(SIBLING,) + SAME_CORE if l == 0 else EVERYONE,
                                   after=gathers[-1]["token"] if gathers else None))
    started = gathers[-1]["token"]

    xl = x[0]
    xb, xbt = _prep(xl, started, ts=TS)
    weights, saved = [], []
    for l in range(L):
        (own_in, own_out, own_cw), (g_in, g_out, g_cw) = _xchg_wait(gathers[l], xb, name=f"gather_wait_{l}")
        if l == 0:
            g_in, g_out, g_cw = _relay_wait(_relay_start([g_in, g_out, g_cw], name="gather_relay_start"), xb,
                                            name="gather_relay_wait")
        w_all = _w_unpack(g_in, own_in, tr=W_TILE_ROWS)
        proj = _layer_fwd_proj(xb, w_all)
        w_o = _with_own(g_out, own_out, me).reshape(N_DEV * rows_out, D)
        cw = jnp.transpose(_with_own(g_cw, own_cw, me), (1, 0, 2)).reshape(CONV_TAPS, N_DEV * ccols)
        wl = dict(w_in=w_all, w_out=w_o, conv_w=cw,
                  b_f=jnp.pad(b_f[l][None, :], ((0, 0), (0, LANES - N_HEADS))), pool_w=pool_w[l],
                  pool_scale=pool_scale[l][None, :], ln_g=ln_g[l][None, :], ln_b=ln_b[l][None, :])
        xl, xb, xbt, sv = _layer_fwd_mix(xl, xbt, proj, wl)
        weights.append(wl)
        saved.append(sv)

    loss, dx = _loss_head(xl, loss_target[0], ts=TS)
    loss = lax.psum(loss[0, 0], ("x", "y", "c"))

    names = ["w_in", "w_out", "conv_w", "b_f", "pool_w", "pool_scale", "ln_g", "ln_b"]
    sharded = names[:3]
    scatters = [None] * L
    for l in reversed(range(L)):
        g, mid = _layer_bwd_mix(dx, saved[l], weights[l])
        (dw_main, dw_fg), ctx = _layer_bwd_w_in(mid, saved[l])
        g["w_in"] = _g_pack(dw_main, dw_fg, cols, tr=W_TILE_ROWS)
        g["w_out"] = g["w_out"].reshape(N_DEV, rows_out, D)
        g["conv_w"] = jnp.transpose(g["conv_w"].reshape(CONV_TAPS, N_DEV, ccols), (1, 0, 2))
        scatters[l] = _xchg_start([g[k] for k in names], gather=[k not in sharded for k in names],
                                  name=f"scatter_start_{l}")
        dx = _layer_bwd_input(ctx, weights[l], scatters[l]["token"])

    given = dict(w_in=(w_in, m_w_in, v_w_in), b_f=(b_f, m_b_f, v_b_f), conv_w=(conv_w, m_conv_w, v_conv_w),
                 pool_w=(pool_w, m_pool_w, v_pool_w), pool_scale=(pool_scale, m_pool_scale, v_pool_scale),
                 w_out=(w_out, m_w_out, v_w_out), ln_g=(ln_g, m_ln_g, v_ln_g), ln_b=(ln_b, m_ln_b, v_ln_b))
    sent, landed = [{} for _ in range(L)], [{} for _ in range(L)]

    def wait(l, after):
        s_, l_ = _xchg_wait(scatters[l], after, name=f"scatter_wait_{l}")
        sent[l].update(zip(names, s_))
        landed[l].update(zip(names, l_))

    last_start = scatters[0]["token"][0, 0]

    def update(k, layers, into=None):
        w, m, v = given[k]
        C = w.shape[-1]
        if k == "w_in":
            m, v = m + last_start, v + last_start
        w2, m2, v2 = (t.reshape(-1, C) for t in (w, m, v))
        rp = w2.shape[0] // L
        lands = [landed[l][k].reshape(N_DEV, rp, C) for l in layers]
        srcs = [sent[l][k].reshape((N_DEV, rp, C) if k in sharded else (rp, C)) for l in layers]
        return _adamw(lands, srcs, w2, m2, v2, name=f"adamw_{k}_{layers[0]}", first=layers[0], into=into)

    big = ["w_in", "w_out", "pool_w"]
    for l in reversed(range(1, L)):
        wait(l, dx)
    upper = {k: update(k, list(range(1, L))) for k in big} if L > 1 else {}
    wait(0, [upper[k][0] for k in big] if upper else dx)
    res = {k: update(k, [0], into=upper[k]) if k in upper else update(k, list(range(L))) for k in names}
    res = {k: [t.reshape(given[k][0].shape) for t in outs] for k, outs in res.items()}

    order = ["w_in", "b_f", "conv_w", "pool_w", "pool_scale", "w_out", "ln_g", "ln_b"]
    return (loss, dx[None], *[res[k][0] for k in order], *[res[k][1] for k in order],
            *[res[k][2] for k in order], *[res[k][3] for k in order])
```

```python
import jax
import jax.numpy as jnp
from jax import lax
from jax.experimental import pallas as pl
from jax.experimental.pallas import tpu as pltpu

F32 = jnp.float32
BF16 = jnp.bfloat16

N_DEV = 8
D_MODEL = 2048
N_HEADS = 8
HEAD_DIM = 128
ATT_W = N_HEADS * HEAD_DIM
CONV_W = 512
CONV_TAPS = 3
POOL_W = 512
POOL_GROUP = 128
N_POOL = POOL_W // POOL_GROUP
FG_AT = 4 * ATT_W
IN_W = FG_AT + N_HEADS + 4 * CONV_W + 2 * POOL_W
MAIN_W = IN_W - N_HEADS
LANES = 128
ALL_W = MAIN_W + LANES
LN_EPS = 1e-5
DEPTH = 4
ALPHA = (2 * DEPTH) ** 0.25
SCALE = HEAD_DIM ** -0.5
LOG2E = 1.4426950408889634
LN2 = 0.6931471805599453
NEG = -1e30

ADAM_LR, ADAM_B1, ADAM_B2, ADAM_EPS, ADAM_WD, ADAM_STEP = 0.001, 0.9, 0.999, 1e-08, 0.01, 10

QKV_W = 3 * ATT_W
REST_W = MAIN_W - QKV_W
QB, KB, VB = 0, 8, 16
GB = 0
CBB, CCB, CHB, CGB = 8, 12, 16, 20
PUB, PGB = 24, 28
FGB = MAIN_W // LANES
VMEM_LIMIT = 48 * 1024 * 1024
ADAMW_BLOCK_ELEMS = 64 * 1024

MESH = pl.DeviceIdType.MESH
ANY = pl.BlockSpec(memory_space=pl.ANY)


def _params(semantics):
    return pltpu.CompilerParams(dimension_semantics=semantics, vmem_limit_bytes=VMEM_LIMIT)


def _sigmoid(x):
    return 1.0 / (1.0 + jnp.exp(-x))


def _mm(a, b, *, out_dtype, tm, tn, tk, name, c=None, c_scale=1.0, n_off=0, n_cols=None, bt=False, at=False,
        after=None):
    M, K = a.shape[::-1] if at else a.shape
    N = b.shape[0 if bt else 1] if n_cols is None else n_cols
    assert not (bt and n_off)
    dims = (((0 if at else 1,), (1 if bt else 0,)), ((), ()))
    tm, tn, tk = min(tm, M), min(tn, N), min(tk, K)
    assert M % tm == 0 and N % tn == 0 and K % tk == 0 and n_off % tn == 0, (a.shape, b.shape, tm, tn, tk)
    nk, joff = K // tk, n_off // tn
    has_c, has_after = c is not None, after is not None

    def body(*refs):
        a_ref, b_ref = refs[0], refs[1]
        c_ref = refs[2] if has_c else None
        o_ref = refs[2 + has_c + has_after]

        def finish(r):
            if has_c:
                r = r + c_scale * c_ref[...]
            o_ref[...] = r.astype(out_dtype)

        if nk == 1:
            finish(lax.dot_general(a_ref[...], b_ref[...], dims, preferred_element_type=F32))
        else:
            acc_ref = refs[3 + has_c + has_after]
            k = pl.program_id(2)

            @pl.when(k == 0)
            def _():
                acc_ref[...] = jnp.zeros_like(acc_ref)

            acc_ref[...] += lax.dot_general(a_ref[...], b_ref[...], dims, preferred_element_type=F32)

            @pl.when(k == nk - 1)
            def _():
                finish(acc_ref[...])

    in_specs = [pl.BlockSpec((tk, tm), lambda i, j, k: (k, i)) if at else pl.BlockSpec((tm, tk), lambda i, j, k: (i, k)),
                pl.BlockSpec((tn, tk), lambda i, j, k: (j, k)) if bt
                else pl.BlockSpec((tk, tn), lambda i, j, k: (k, j + joff))]
    args = [a, b]
    if has_c:
        in_specs.append(pl.BlockSpec((tm, tn), lambda i, j, k: (i, j)))
        args.append(c)
    if has_after:
        in_specs.append(ANY)
        args.append(after)
    return pl.pallas_call(
        body, name=name, grid=(M // tm, N // tn, nk),
        in_specs=in_specs,
        out_specs=pl.BlockSpec((tm, tn), lambda i, j, k: (i, j)),
        out_shape=jax.ShapeDtypeStruct((M, N), out_dtype),
        scratch_shapes=[pltpu.VMEM((tm, tn), F32)] if nk > 1 else [],
        compiler_params=_params(("parallel", "parallel", "arbitrary")),
    )(*args)


def _shift_down(x, k, row):
    return jnp.where(row >= k, pltpu.roll(x, k, 0), 0.0)


def _shift_up(x, k, row):
    s = x.shape[0]
    return jnp.where(row < s - k, pltpu.roll(x, s - k, 0), 0.0)


def _fg_fwd(fgp, bf):
    S = fgp.shape[0]

    def body(f_ref, b_ref, cum_ref):
        a = f_ref[...] + b_ref[...]
        x = jnp.minimum(a, 0.0) - jnp.log(1.0 + jnp.exp(-jnp.abs(a)))
        row = lax.broadcasted_iota(jnp.int32, x.shape, 0)
        k = 1
        while k < S:
            x = x + _shift_down(x, k, row)
            k *= 2
        cum_ref[...] = x

    return pl.pallas_call(body, name="fg_fwd", out_shape=jax.ShapeDtypeStruct((S, LANES), F32),
                          compiler_params=_params(None))(fgp, bf)


def _fg_bwd(dcum, fgp, bf, dproj):
    S = fgp.shape[0]

    def body(d_ref, f_ref, b_ref, dproj_in, dfg_ref, dbf_ref):
        x = d_ref[...]
        row = lax.broadcasted_iota(jnp.int32, x.shape, 0)
        k = 1
        while k < S:
            x = x + _shift_up(x, k, row)
            k *= 2
        dfg = x * _sigmoid(-(f_ref[...] + b_ref[...]))
        dfg_ref[...] = dfg.astype(BF16)
        dbf_ref[...] = jnp.sum(dfg, axis=0, keepdims=True)

    whole = pl.BlockSpec((S, LANES), lambda i: (0, 0))
    vec = pl.BlockSpec((1, LANES), lambda i: (0, 0))
    return pl.pallas_call(
        body, name="fg_bwd", grid=(1,),
        in_specs=[whole, whole, vec, ANY],
        out_specs=(pl.BlockSpec((S, LANES), lambda i: (0, FGB)), vec),
        out_shape=(jax.ShapeDtypeStruct(dproj.shape, dproj.dtype), jax.ShapeDtypeStruct((1, LANES), F32)),
        input_output_aliases={3: 0},
        compiler_params=_params(("arbitrary",)),
    )(dcum, fgp, bf, dproj)


def _put_columns(pieces, dst, blocks, sems):
    copies = [pltpu.make_async_copy(p, dst.at[:, pl.ds(pl.multiple_of(b * LANES, LANES), LANES)], sems.at[i])
              for i, (p, b) in enumerate(zip(pieces, blocks))]
    for cp in copies:
        cp.start()
    for cp in copies:
        cp.wait()


def _colblock(S, off):
    return pl.BlockSpec((S, LANES), lambda h: (0, h + off))


def _head_row(nq, tq):
    return pl.BlockSpec((None, nq, 1, tq), lambda h: (h, 0, 0, 0))


def _lane_of(ref, h):
    lane = lax.broadcasted_iota(jnp.int32, ref.shape, 1)
    return jnp.sum(jnp.where(lane == h, ref[...], 0.0), axis=1, keepdims=True)


def _attn_fwd(qkv, rest, cum, cumr, *, tq):
    S = qkv.shape[0]
    nq = S // tq

    def body(q_ref, k_ref, v_ref, g_ref, cum_ref, cumr_ref, o_ref, y_ref, lse_ref, cc):
        cc[...] = _lane_of(cum_ref, pl.program_id(0)) * LOG2E
        half = tq // 2

        def q_step(qi, _):
            rows = pl.ds(pl.multiple_of(qi * tq, tq), tq)
            q = q_ref[rows, :]
            ci = cumr_ref[qi]

            def part(cols, q, ci, carry, masked):
                m, l, acc = carry
                s = lax.dot_general(k_ref[cols, :], q, (((1,), (1,)), ((), ())), preferred_element_type=F32)
                s = s * (SCALE * LOG2E) + ci - cc[cols, :]
                if masked:
                    s = jnp.where(lax.broadcasted_iota(jnp.int32, s.shape, 0) <= lax.broadcasted_iota(jnp.int32, s.shape, 1),
                                  s, NEG)
                m_new = jnp.maximum(m, jnp.max(s, axis=0, keepdims=True))
                a = jnp.exp2(m - m_new)
                p = jnp.exp2(s - m_new)
                l = a * l + jnp.sum(p, axis=0, keepdims=True)
                acc = a * acc + lax.dot_general(v_ref[cols, :], p.astype(BF16), (((0,), (0,)), ((), ())),
                                                preferred_element_type=F32)
                return m_new, l, acc

            def tile(kj, carry):
                return part(pl.ds(pl.multiple_of(kj * tq, tq), tq), q, ci, carry, False)

            init = (jnp.full((1, tq), NEG, F32), jnp.zeros((1, tq), F32), jnp.zeros((HEAD_DIM, tq), F32))
            carry = lax.fori_loop(0, qi, tile, init)
            c0 = pl.multiple_of(qi * tq, tq)
            m, l, acc = part(pl.ds(c0, half), q, ci, carry, True)
            late = part(pl.ds(c0 + half, half), q[half:], ci[:, half:], (m[:, half:], l[:, half:], acc[:, half:]), True)
            m, l, acc = (jnp.concatenate([x[:, :half], y], axis=1) for x, y in zip((m, l, acc), late))
            o = (acc / l).T
            g = g_ref[rows, :]
            o_ref[rows, :] = o
            y_ref[rows, :] = (o * (g * _sigmoid(g))).astype(BF16)
            lse_ref[qi] = m * LN2 + jnp.log(l)
            return 0

        lax.fori_loop(0, nq, q_step, 0)

    return pl.pallas_call(
        body, name="attn_fwd", grid=(N_HEADS,),
        in_specs=[_colblock(S, QB), _colblock(S, KB), _colblock(S, VB), _colblock(S, GB),
                  pl.BlockSpec((S, LANES), lambda h: (0, 0)), _head_row(nq, tq)],
        out_specs=(_colblock(S, 0), _colblock(S, 0), _head_row(nq, tq)),
        out_shape=(jax.ShapeDtypeStruct((S, ATT_W), F32), jax.ShapeDtypeStruct((S, D_MODEL), BF16),
                   jax.ShapeDtypeStruct((N_HEADS, nq, 1, tq), F32)),
        scratch_shapes=[pltpu.VMEM((S, 1), F32)],
        compiler_params=_params(("arbitrary",)),
    )(qkv, qkv, qkv, rest, cum, cumr)


def _attn_bwd_pre(rest, o, dy, *, tq):
    S = o.shape[0]
    nq = S // tq

    def body(g_ref, o_ref, dy_ref, do_ref, dg_ref, dl_ref):
        g = g_ref[...]
        sg = _sigmoid(g)
        ov = o_ref[...]
        dy = dy_ref[...]
        do = dy * (g * sg)
        do_ref[...] = do.astype(BF16)
        dg_ref[...] = (dy * ov * (sg * (1.0 + g * (1.0 - sg)))).astype(BF16)
        rows = lax.dot_general(jnp.ones((8, HEAD_DIM), F32), do * ov, (((1,), (1,)), ((), ())),
                               precision=lax.Precision.HIGHEST, preferred_element_type=F32)
        for qi in range(nq):
            dl_ref[qi] = rows[0:1, qi * tq:(qi + 1) * tq]

    att = jax.ShapeDtypeStruct((S, ATT_W), BF16)
    return pl.pallas_call(
        body, name="attn_bwd_pre", grid=(N_HEADS,),
        in_specs=[_colblock(S, GB), _colblock(S, 0), _colblock(S, 0)],
        out_specs=(_colblock(S, 0), _colblock(S, QKV_W // LANES), _head_row(nq, tq)),
        out_shape=(att, jax.ShapeDtypeStruct((S, ALL_W), BF16), jax.ShapeDtypeStruct((N_HEADS, nq, 1, tq), F32)),
        compiler_params=_params(("parallel",)),
    )(rest, o, dy)


def _attn_bwd(qkv, do, a2, delta, cum, dproj, *, tq):
    S = qkv.shape[0]
    nq = S // tq
    tdot = (((0,), (0,)), ((), ()))
    ndot = (((1,), (1,)), ((), ()))

    def body(q_ref, k_ref, v_ref, do_ref, a_ref, dl_ref, cum_ref, dproj_in, dproj_ref, dsq_ref, dsk_ref,
             dqa, dq_ref, dk_ref, dv_ref, cc, sems):
        h = pl.program_id(0)
        dqa[...] = jnp.zeros_like(dqa)
        dsq_ref[...] = jnp.zeros_like(dsq_ref)
        cc[...] = _lane_of(cum_ref, h) * LOG2E

        @pl.when(h == 0)
        def _():
            dsk_ref[...] = jnp.zeros_like(dsk_ref)

        half = tq // 2
        mine = lax.broadcasted_iota(jnp.int32, (tq, LANES), 1) == h

        def k_step(kj, _):
            c0 = pl.multiple_of(kj * tq, tq)
            cols = pl.ds(c0, tq)
            k = k_ref[cols, :]
            v = v_ref[cols, :]
            cj = cc[cols, :]

            def part(k, v, cj, qi, q0, first, carry, masked):
                dk, dv, ksum = carry
                nqs = tq - first
                rows = pl.ds(q0, nqs)
                q = q_ref[rows, :]
                dot = do_ref[rows, :]
                s = lax.dot_general(k, q, ndot, preferred_element_type=F32)
                p = jnp.exp2(s * (SCALE * LOG2E) + a_ref[qi][:, first:] - cj)
                if masked:
                    p = jnp.where(lax.broadcasted_iota(jnp.int32, p.shape, 0) <= lax.broadcasted_iota(jnp.int32, p.shape, 1),
                                  p, 0.0)
                dv = dv + jnp.dot(p.astype(BF16), dot, preferred_element_type=F32)
                dp = lax.dot_general(v, dot, ndot, preferred_element_type=F32)
                ds = p * (dp - dl_ref[qi][:, first:])
                over_keys = jnp.sum(ds, axis=0, keepdims=True)
                if first:
                    over_keys = jnp.concatenate([jnp.zeros((1, first), F32), over_keys], axis=1)
                dsq_ref[qi] += over_keys
                ksum = ksum + jnp.sum(ds, axis=1, keepdims=True)
                dsb = (ds * SCALE).astype(BF16)
                dk = dk + jnp.dot(dsb, q, preferred_element_type=F32)
                dqa[rows, :] += lax.dot_general(dsb, k, tdot, preferred_element_type=F32)
                return dk, dv, ksum

            def zeros(n):
                return (jnp.zeros((n, HEAD_DIM), F32), jnp.zeros((n, HEAD_DIM), F32), jnp.zeros((n, 1), F32))

            early = part(k[:half], v[:half], cj[:half], kj, c0, 0, zeros(half), True)
            late = part(k[half:], v[half:], cj[half:], kj, c0 + half, half, zeros(half), True)
            carry = tuple(jnp.concatenate([x, y], axis=0) for x, y in zip(early, late))
            dk, dv, ksum = lax.fori_loop(
                kj + 1, nq, lambda qi, c: part(k, v, cj, qi, pl.multiple_of(qi * tq, tq), 0, c, False), carry)
            dk_ref[cols, :] = dk.astype(BF16)
            dv_ref[cols, :] = dv.astype(BF16)
            dsk_ref[cols, :] += jnp.where(mine, ksum, 0.0)
            return 0

        lax.fori_loop(0, nq, k_step, 0)
        dq_ref[...] = dqa[...].astype(BF16)
        _put_columns([dq_ref, dk_ref, dv_ref], dproj_ref, [QB + h, KB + h, VB + h], sems)

    head = pltpu.VMEM((S, HEAD_DIM), BF16)
    return pl.pallas_call(
        body, name="attn_bwd", grid=(N_HEADS,),
        in_specs=[_colblock(S, QB), _colblock(S, KB), _colblock(S, VB), _colblock(S, 0),
                  _head_row(nq, tq), _head_row(nq, tq), pl.BlockSpec((S, LANES), lambda h: (0, 0)), ANY],
        out_specs=(ANY, _head_row(nq, tq), pl.BlockSpec((S, LANES), lambda h: (0, 0))),
        out_shape=(jax.ShapeDtypeStruct(dproj.shape, dproj.dtype), jax.ShapeDtypeStruct((N_HEADS, nq, 1, tq), F32),
                   jax.ShapeDtypeStruct((S, LANES), F32)),
        input_output_aliases={7: 0},
        scratch_shapes=[pltpu.VMEM((S, HEAD_DIM), F32), head, head, head, pltpu.VMEM((S, 1), F32),
                        pltpu.SemaphoreType.DMA((3,))],
        compiler_params=_params(("arbitrary",)),
    )(qkv, qkv, qkv, do, a2, delta, cum, dproj)


def _conv_taps(u, w_ref, row):
    return (w_ref[0:1, :] * _shift_down(u, 2, row) + w_ref[1:2, :] * _shift_down(u, 1, row)) + w_ref[2:3, :] * u


def _conv_fwd(rest, cw, y):
    S = rest.shape[0]

    def body(cb_ref, cc_ref, ch_ref, g_ref, w_ref, y_in, y_ref):
        row = lax.broadcasted_iota(jnp.int32, (S, LANES), 0)
        g = g_ref[...]
        y = _conv_taps(cc_ref[...] * ch_ref[...], w_ref, row)
        y_ref[...] = ((cb_ref[...] * y) * (g * _sigmoid(g))).astype(BF16)

    return pl.pallas_call(
        body, name="conv_fwd", grid=(CONV_W // LANES,),
        in_specs=[_colblock(S, CBB), _colblock(S, CCB), _colblock(S, CHB), _colblock(S, CGB),
                  pl.BlockSpec((CONV_TAPS, LANES), lambda j: (0, j)), ANY],
        out_specs=_colblock(S, ATT_W // LANES),
        out_shape=jax.ShapeDtypeStruct(y.shape, y.dtype),
        input_output_aliases={5: 0},
        compiler_params=_params(("parallel",)),
    )(rest, rest, rest, rest, cw, y)


def _conv_bwd(rest, dy, cw, dproj):
    S = rest.shape[0]

    def body(cb_ref, cc_ref, ch_ref, g_ref, dy_ref, w_ref, dproj_in, dproj_ref, dw_ref,
             dcb_ref, dcc_ref, dch_ref, dg_ref, sems):
        row = lax.broadcasted_iota(jnp.int32, (S, LANES), 0)
        g = g_ref[...]
        sg = _sigmoid(g)
        silu = g * sg
        cb, cc, ch, dy = cb_ref[...], cc_ref[...], ch_ref[...], dy_ref[...]
        u = cc * ch
        u1 = _shift_down(u, 1, row)
        u2 = _shift_down(u, 2, row)
        y = (w_ref[0:1, :] * u2 + w_ref[1:2, :] * u1) + w_ref[2:3, :] * u
        dcb_ref[...] = (dy * silu * y).astype(BF16)
        dg_ref[...] = (dy * (cb * y) * (sg * (1.0 + g * (1.0 - sg)))).astype(BF16)
        dyv = dy * silu * cb
        du = w_ref[2:3, :] * dyv + w_ref[1:2, :] * _shift_up(dyv, 1, row) + w_ref[0:1, :] * _shift_up(dyv, 2, row)
        dcc_ref[...] = (du * ch).astype(BF16)
        dch_ref[...] = (du * cc).astype(BF16)
        dw_ref[0:1, :] = jnp.sum(dyv * u2, axis=0, keepdims=True)
        dw_ref[1:2, :] = jnp.sum(dyv * u1, axis=0, keepdims=True)
        dw_ref[2:3, :] = jnp.sum(dyv * u, axis=0, keepdims=True)
        j = pl.program_id(0)
        first = QKV_W // LANES
        _put_columns([dcb_ref, dcc_ref, dch_ref, dg_ref], dproj_ref,
                     [first + CBB + j, first + CCB + j, first + CHB + j, first + CGB + j], sems)

    piece = pltpu.VMEM((S, LANES), BF16)
    return pl.pallas_call(
        body, name="conv_bwd", grid=(CONV_W // LANES,),
        in_specs=[_colblock(S, CBB), _colblock(S, CCB), _colblock(S, CHB), _colblock(S, CGB),
                  _colblock(S, ATT_W // LANES), pl.BlockSpec((CONV_TAPS, LANES), lambda j: (0, j)), ANY],
        out_specs=(ANY, pl.BlockSpec((CONV_TAPS, LANES), lambda j: (0, j))),
        out_shape=(jax.ShapeDtypeStruct(dproj.shape, dproj.dtype), jax.ShapeDtypeStruct((CONV_TAPS, CONV_W), F32)),
        input_output_aliases={6: 0},
        scratch_shapes=[piece] * 4 + [pltpu.SemaphoreType.DMA((4,))],
        compiler_params=_params(("arbitrary",)),
    )(rest, rest, rest, rest, dy, cw, dproj)


def _pool_z(u, grp, row):
    s2 = u + _shift_down(u, 1, row)
    s4 = s2 + _shift_down(s2, 2, row)
    s8 = s4 + _shift_down(s4, 4, row)
    s16 = s8 + _shift_down(s8, 8, row)
    sw = jnp.where(grp == 0, s2, jnp.where(grp == 1, s4, jnp.where(grp == 2, s8, s16)))
    return sw / _pool_count(grp, row) - u


def _pool_count(grp, row):
    return jnp.minimum(row + 1, 2 << grp).astype(F32)


def _pool_fwd(rest, pw, scale, y):
    S = rest.shape[0]

    def body(u_ref, g_ref, pw_ref, sc_ref, y_in, y_ref):
        grp = pl.program_id(0)
        row = lax.broadcasted_iota(jnp.int32, (S, LANES), 0)
        z = _pool_z(u_ref[...], grp, row)
        zp = jnp.dot(z.astype(BF16), pw_ref[...].astype(BF16), preferred_element_type=F32)
        g = g_ref[...]
        y_ref[...] = ((zp * sc_ref[...]) * (g * _sigmoid(g))).astype(BF16)

    return pl.pallas_call(
        body, name="pool_fwd", grid=(N_POOL,),
        in_specs=[_colblock(S, PUB), _colblock(S, PGB),
                  pl.BlockSpec((None, POOL_GROUP, POOL_GROUP), lambda j: (j, 0, 0)),
                  pl.BlockSpec((1, LANES), lambda j: (0, j)), ANY],
        out_specs=_colblock(S, (ATT_W + CONV_W) // LANES),
        out_shape=jax.ShapeDtypeStruct(y.shape, y.dtype),
        input_output_aliases={4: 0},
        compiler_params=_params(("parallel",)),
    )(rest, rest, pw, scale, y)


def _pool_bwd(rest, dy, pw, scale, dproj):
    S = rest.shape[0]

    def body(u_ref, g_ref, dy_ref, pw_ref, sc_ref, dproj_in, dproj_ref, dpw_ref, dsc_ref, du_ref, dg_ref, sems):
        grp = pl.program_id(0)
        row = lax.broadcasted_iota(jnp.int32, (S, LANES), 0)
        z = _pool_z(u_ref[...], grp, row).astype(BF16)
        pwb = pw_ref[...].astype(BF16)
        zp = jnp.dot(z, pwb, preferred_element_type=F32)
        g = g_ref[...]
        sg = _sigmoid(g)
        silu = g * sg
        dy = dy_ref[...]
        sc = sc_ref[...]
        dsc_ref[...] = jnp.sum(dy * silu * zp, axis=0, keepdims=True)
        dg_ref[...] = (dy * (zp * sc) * (sg * (1.0 + g * (1.0 - sg)))).astype(BF16)
        dzp = (dy * silu * sc).astype(BF16)
        dpw_ref[...] = lax.dot_general(z, dzp, (((0,), (0,)), ((), ())), preferred_element_type=F32)
        dz = lax.dot_general(dzp, pwb, (((1,), (1,)), ((), ())), preferred_element_type=F32)
        f1 = dz / _pool_count(grp, row)
        f2 = f1 + _shift_up(f1, 1, row)
        f4 = f2 + _shift_up(f2, 2, row)
        f8 = f4 + _shift_up(f4, 4, row)
        f16 = f8 + _shift_up(f8, 8, row)
        fw = jnp.where(grp == 0, f2, jnp.where(grp == 1, f4, jnp.where(grp == 2, f8, f16)))
        du_ref[...] = (fw - dz).astype(BF16)
        first = QKV_W // LANES
        _put_columns([du_ref, dg_ref], dproj_ref, [first + PUB + grp, first + PGB + grp], sems)

    piece = pltpu.VMEM((S, LANES), BF16)
    return pl.pallas_call(
        body, name="pool_bwd", grid=(N_POOL,),
        in_specs=[_colblock(S, PUB), _colblock(S, PGB), _colblock(S, (ATT_W + CONV_W) // LANES),
                  pl.BlockSpec((None, POOL_GROUP, POOL_GROUP), lambda j: (j, 0, 0)),
                  pl.BlockSpec((1, LANES), lambda j: (0, j)), ANY],
        out_specs=(ANY, pl.BlockSpec((None, POOL_GROUP, POOL_GROUP), lambda j: (j, 0, 0)),
                   pl.BlockSpec((1, LANES), lambda j: (0, j))),
        out_shape=(jax.ShapeDtypeStruct(dproj.shape, dproj.dtype),
                   jax.ShapeDtypeStruct((N_POOL, POOL_GROUP, POOL_GROUP), F32), jax.ShapeDtypeStruct((1, POOL_W), F32)),
        input_output_aliases={5: 0},
        scratch_shapes=[piece] * 2 + [pltpu.SemaphoreType.DMA((2,))],
        compiler_params=_params(("arbitrary",)),
    )(rest, rest, dy, pw, scale, dproj)


def _ln_fwd(z, g, b, *, ts):
    S, D = z.shape

    def body(z_ref, g_ref, b_ref, o_ref, ob_ref, obt_ref, xh_ref, rs_ref):
        zz = z_ref[...]
        mu = jnp.mean(zz, axis=1, keepdims=True)
        zc = zz - mu
        rstd = lax.rsqrt(jnp.mean(zc * zc, axis=1, keepdims=True) + LN_EPS)
        xh = zc * rstd
        out = xh * g_ref[...] + b_ref[...]
        o_ref[...] = out
        ob_ref[...] = out.astype(BF16)
        obt_ref[...] = out.T.astype(BF16)
        xh_ref[...] = xh
        rs_ref[...] = rstd

    rowblk = pl.BlockSpec((ts, D), lambda i: (i, 0))
    vec = pl.BlockSpec((1, D), lambda i: (0, 0))
    return pl.pallas_call(
        body, name="ln_fwd", grid=(S // ts,),
        in_specs=[rowblk, vec, vec],
        out_specs=(rowblk, rowblk, pl.BlockSpec((D, ts), lambda i: (0, i)), rowblk,
                   pl.BlockSpec((ts, 1), lambda i: (i, 0))),
        out_shape=(jax.ShapeDtypeStruct((S, D), F32), jax.ShapeDtypeStruct((S, D), BF16),
                   jax.ShapeDtypeStruct((D, S), BF16), jax.ShapeDtypeStruct((S, D), F32),
                   jax.ShapeDtypeStruct((S, 1), F32)),
        compiler_params=_params(("parallel",)),
    )(z, g, b)


def _prep(x, after, *, ts):
    S, D = x.shape

    def body(x_ref, after_ref, xb_ref, xbt_ref):
        xb_ref[...] = x_ref[...].astype(BF16)
        xbt_ref[...] = x_ref[...].T.astype(BF16)

    return pl.pallas_call(
        body, name="prep", grid=(S // ts,),
        in_specs=[pl.BlockSpec((ts, D), lambda i: (i, 0)), ANY],
        out_specs=(pl.BlockSpec((ts, D), lambda i: (i, 0)), pl.BlockSpec((D, ts), lambda i: (0, i))),
        out_shape=(jax.ShapeDtypeStruct((S, D), BF16), jax.ShapeDtypeStruct((D, S), BF16)),
        compiler_params=_params(("parallel",)),
    )(x, after)


def _ln_bwd(dout, xhat, rstd, g, *, ts):
    S, D = dout.shape

    def body(d_ref, xh_ref, rs_ref, g_ref, dz_ref, dzb_ref, dg_ref, db_ref):
        @pl.when(pl.program_id(0) == 0)
        def _():
            dg_ref[...] = jnp.zeros_like(dg_ref)
            db_ref[...] = jnp.zeros_like(db_ref)

        d = d_ref[...]
        xh = xh_ref[...]
        dxh = d * g_ref[...]
        m1 = jnp.mean(dxh, axis=1, keepdims=True)
        m2 = jnp.mean(dxh * xh, axis=1, keepdims=True)
        dz = rs_ref[...] * (dxh - m1 - xh * m2)
        dz_ref[...] = dz
        dzb_ref[...] = dz.astype(BF16)
        dg_ref[...] += jnp.sum(d * xh, axis=0, keepdims=True)
        db_ref[...] += jnp.sum(d, axis=0, keepdims=True)

    rowblk = pl.BlockSpec((ts, D), lambda i: (i, 0))
    vec = pl.BlockSpec((1, D), lambda i: (0, 0))
    return pl.pallas_call(
        body, name="ln_bwd", grid=(S // ts,),
        in_specs=[rowblk, rowblk, pl.BlockSpec((ts, 1), lambda i: (i, 0)), vec],
        out_specs=(rowblk, rowblk, vec, vec),
        out_shape=(jax.ShapeDtypeStruct((S, D), F32), jax.ShapeDtypeStruct((S, D), BF16),
                   jax.ShapeDtypeStruct((1, D), F32), jax.ShapeDtypeStruct((1, D), F32)),
        compiler_params=_params(("arbitrary",)),
    )(dout, xhat, rstd, g)


def _loss_head(y, tgt, *, ts):
    S, D = y.shape

    def body(y_ref, t_ref, l_ref, dy_ref):
        @pl.when(pl.program_id(0) == 0)
        def _():
            l_ref[...] = jnp.zeros_like(l_ref)

        e = y_ref[...] - t_ref[...]
        dy_ref[...] = e * (1.0 / D)
        rowloss = jnp.sum(e * e, axis=1, keepdims=True) * (0.5 / D)
        l_ref[...] += jnp.sum(rowloss, axis=0, keepdims=True)

    rowblk = pl.BlockSpec((ts, D), lambda i: (i, 0))
    return pl.pallas_call(
        body, name="loss_head", grid=(S // ts,),
        in_specs=[rowblk, rowblk],
        out_specs=(pl.BlockSpec((1, 1), lambda i: (0, 0)), rowblk),
        out_shape=(jax.ShapeDtypeStruct((1, 1), F32), jax.ShapeDtypeStruct((S, D), F32)),
        compiler_params=_params(("arbitrary",)),
    )(y, tgt)


def _adamw(lands, srcs, w, m, v, *, name, first=0, into=None):
    R, C = w.shape
    npart = len(lands)
    rp = lands[0].shape[1]
    assert R % rp == 0 and all(p.shape == (N_DEV, rp, C) for p in lands)
    tiled = rp % 8 == 0
    assert tiled or (first == 0 and npart * rp == R and into is None)
    tr = min(ADAMW_BLOCK_ELEMS // (LANES * pl.cdiv(C, LANES)), rp) if tiled else rp
    assert rp % tr == 0
    steps = rp // tr
    c1 = 1.0 - ADAM_B1 ** ADAM_STEP
    c2 = 1.0 - ADAM_B2 ** ADAM_STEP
    slabbed = [p.ndim == 3 for p in srcs]
    n_into = 0 if into is None else 4

    def body(*refs):
        l_refs, s_refs = refs[:npart], refs[npart:2 * npart]
        w_ref, m_ref, v_ref = refs[2 * npart:2 * npart + 3]
        g_ref, d_ref, nm_ref, nv_ref = refs[2 * npart + 3 + n_into:]
        x, y, c = _my_place()
        me = 4 * x + 2 * y + c

        def update(l, rows):
            g = None
            for j in range(N_DEV):
                own = s_refs[l][j] if slabbed[l] else s_refs[l][...]
                p = jnp.where(me == j, own, l_refs[l][j]).astype(F32)
                g = p if g is None else g + p
            nm = ADAM_B1 * m_ref[rows, :] + (1.0 - ADAM_B1) * g
            nv = ADAM_B2 * v_ref[rows, :] + (1.0 - ADAM_B2) * (g * g)
            g_ref[rows, :] = g
            nm_ref[rows, :] = nm
            nv_ref[rows, :] = nv
            d_ref[rows, :] = -ADAM_LR * ((nm / c1) / (jnp.sqrt(nv / c2) + ADAM_EPS) + ADAM_WD * w_ref[rows, :])

        for l in range(npart):
            if tiled:
                pl.when(pl.program_id(0) == l)(lambda l=l: update(l, slice(None)))
            else:
                update(l, slice(l * rp, (l + 1) * rp))

    def part_spec(l, p):
        if p.ndim == 3:
            return pl.BlockSpec((N_DEV, tr, C), lambda q, i: (0, jnp.where(q == l, i, 0), 0))
        return pl.BlockSpec((tr, C), lambda q, i: (jnp.where(q == l, i, 0), 0))

    if tiled:
        blk = pl.BlockSpec((tr, C), lambda q, i: ((first + q) * steps + i, 0))
    else:
        blk = pl.BlockSpec((R, C), lambda q, i: (0, 0))
    out = jax.ShapeDtypeStruct((R, C), F32)
    n_in = 2 * npart + 3
    return pl.pallas_call(
        body, name=name, grid=(npart, steps) if tiled else (1, 1),
        in_specs=[part_spec(l, p) for l, p in enumerate(lands)] + [part_spec(l, p) for l, p in enumerate(srcs)]
        + [blk, blk, blk] + [ANY] * n_into,
        out_specs=(blk, blk, blk, blk),
        out_shape=(out, out, out, out),
        input_output_aliases={n_in + i: i for i in range(n_into)},
        compiler_params=_params(("arbitrary", "arbitrary")),
    )(*lands, *srcs, w, m, v, *(into or ()))


def _my_place():
    return lax.axis_index("x"), lax.axis_index("y"), lax.axis_index("c")


HBM = pl.BlockSpec(memory_space=pltpu.HBM)
SEM = pl.BlockSpec(memory_space=pltpu.SEMAPHORE)
DATAFLOW = pltpu.SideEffectType.DATAFLOW_SIDE_EFFECTING
EVERYONE = (1, 2, 3, 4, 5, 6, 7)
SIBLING = 1
SAME_CORE = (4, 2, 6)


def _xchg_copies(src, land, send_sem, recv_sem, gather, peers):
    x, y, c = _my_place()
    me = 4 * x + 2 * y + c
    copies = []
    for k, bits in enumerate(peers):
        px, py, pc = x ^ ((bits >> 2) & 1), y ^ ((bits >> 1) & 1), c ^ (bits & 1)
        peer = 4 * px + 2 * py + pc
        for a in range(len(src)):
            i = a * len(peers) + k
            out = src[a] if gather[a] else src[a].at[peer]
            sent, landed = [pltpu.make_async_remote_copy(
                src_ref=out, dst_ref=land[a].at[slab], send_sem=send_sem.at[i], recv_sem=recv_sem.at[i],
                device_id=(px, py, pc), device_id_type=MESH) for slab in (me, peer)]
            copies.append((sent, landed))
    return copies


def _xchg_start(srcs, *, gather, name, peers=EVERYONE, after=None):
    n = len(srcs)
    follows = [] if after is None else [after]
    lands = [lax.empty(((N_DEV,) + v.shape) if g else v.shape, v.dtype) for v, g in zip(srcs, gather)]

    def body(*refs):
        src, land = refs[:n], refs[n:2 * n]
        send_sem, recv_sem = refs[2 * n + len(follows)], refs[2 * n + len(follows) + 1]
        token = refs[-1]
        for sent, _ in _xchg_copies(src, land, send_sem, recv_sem, gather, peers):
            sent.start()
        token[...] = jnp.zeros_like(token)

    sems = pltpu.SemaphoreType.DMA((n * len(peers),))
    outs = pl.pallas_call(
        body, name=name,
        out_shape=(sems, sems, *[pltpu.HBM(v.shape, v.dtype) for v in srcs + lands],
                   jax.ShapeDtypeStruct((8, LANES), F32)),
        in_specs=[HBM] * (2 * n) + [ANY] * len(follows),
        out_specs=(SEM, SEM, *[HBM] * (2 * n), pl.BlockSpec(memory_space=pltpu.VMEM)),
        input_output_aliases={i: 2 + i for i in range(2 * n)},
        compiler_params=pltpu.CompilerParams(has_side_effects=DATAFLOW),
    )(*[pltpu.with_memory_space_constraint(v, pltpu.HBM) for v in srcs + lands], *follows)
    return dict(send=outs[0], recv=outs[1], srcs=list(outs[2:2 + n]), lands=list(outs[2 + n:2 + 2 * n]),
                token=outs[-1], gather=gather, peers=peers)


def _xchg_wait(h, after, *, name):
    n = len(h["srcs"])
    gather, peers = h["gather"], h["peers"]
    after = list(after) if isinstance(after, (list, tuple)) else [after]

    def body(*refs):
        src, land = refs[:n], refs[n:2 * n]
        send_sem, recv_sem = refs[2 * n], refs[2 * n + 1]
        for sent, landed in _xchg_copies(src, land, send_sem, recv_sem, gather, peers):
            sent.wait_send()
            landed.wait_recv()

    thru = h["srcs"] + h["lands"]
    outs = pl.pallas_call(
        body, name=name,
        out_shape=[pltpu.HBM(v.shape, v.dtype) for v in thru],
        in_specs=[HBM] * (2 * n) + [SEM, SEM] + [ANY] * len(after),
        out_specs=[HBM] * (2 * n),
        input_output_aliases={i: i for i in range(2 * n)},
        compiler_params=pltpu.CompilerParams(has_side_effects=DATAFLOW),
    )(*thru, h["send"], h["recv"], *after)
    return list(outs[:n]), list(outs[n:])


def _relay_copies(land, send_sem, recv_sem):
    x, y, c = _my_place()
    me = 4 * x + 2 * y + c
    copies = []
    for k, bits in enumerate(SAME_CORE):
        for a in range(len(land)):
            i = a * len(SAME_CORE) + k
            sent, landed = [pltpu.make_async_remote_copy(
                src_ref=land[a].at[me ^ bits], dst_ref=land[a].at[slab], send_sem=send_sem.at[i], recv_sem=recv_sem.at[i],
                device_id=(x, y, 1 - c), device_id_type=MESH) for slab in (me ^ bits, me ^ bits ^ SIBLING)]
            copies.append((sent, landed))
    return copies


def _relay_start(lands, *, name):
    n = len(lands)

    def body(*refs):
        land = refs[:n]
        send_sem, recv_sem = refs[n], refs[n + 1]
        token = refs[-1]
        for sent, _ in _relay_copies(land, send_sem, recv_sem):
            sent.start()
        token[...] = jnp.zeros_like(token)

    sems = pltpu.SemaphoreType.DMA((n * len(SAME_CORE),))
    outs = pl.pallas_call(
        body, name=name,
        out_shape=(sems, sems, *[pltpu.HBM(v.shape, v.dtype) for v in lands], jax.ShapeDtypeStruct((8, LANES), F32)),
        in_specs=[HBM] * n,
        out_specs=(SEM, SEM, *[HBM] * n, pl.BlockSpec(memory_space=pltpu.VMEM)),
        input_output_aliases={i: 2 + i for i in range(n)},
        compiler_params=pltpu.CompilerParams(has_side_effects=DATAFLOW),
    )(*[pltpu.with_memory_space_constraint(v, pltpu.HBM) for v in lands])
    return dict(send=outs[0], recv=outs[1], lands=list(outs[2:2 + n]), token=outs[-1])


def _relay_wait(h, after, *, name):
    n = len(h["lands"])

    def body(*refs):
        land = refs[:n]
        send_sem, recv_sem = refs[n], refs[n + 1]
        for sent, landed in _relay_copies(land, send_sem, recv_sem):
            sent.wait_send()
            landed.wait_recv()

    outs = pl.pallas_call(
        body, name=name,
        out_shape=[pltpu.HBM(v.shape, v.dtype) for v in h["lands"]],
        in_specs=[HBM] * n + [SEM, SEM, ANY],
        out_specs=[HBM] * n,
        input_output_aliases={i: i for i in range(n)},
        compiler_params=pltpu.CompilerParams(has_side_effects=DATAFLOW),
    )(*h["lands"], h["send"], h["recv"], after)
    return list(outs)


def _with_own(land, own, me):
    slot = lax.broadcasted_iota(jnp.int32, land.shape, 0)
    return jnp.where(slot == me, own if own.ndim == land.ndim else own[None], land)


def _w_in_segments(j, cols):
    lo, hi = j * cols, (j + 1) * cols
    segs = []
    for a, b, where, shift in ((0, FG_AT, "main", 0), (FG_AT, FG_AT + N_HEADS, "fg", -FG_AT),
                               (FG_AT + N_HEADS, IN_W, "main", -N_HEADS)):
        s0, s1 = max(lo, a), min(hi, b)
        if s0 < s1:
            segs.append((s0 - lo, s1 - s0, where, s0 + shift))
    return segs


SLAB_W = 1024


def _window(off, width):
    base = min((off // LANES) * LANES, width - SLAB_W)
    return base, off - base


def _w_unpack(land, own, *, tr):
    _, D, cols = land.shape
    assert cols + LANES - 1 <= SLAB_W

    def body(land_ref, own_ref, main_ref, slab):
        x, y, c = _my_place()
        me = 4 * x + 2 * y + c
        lane = lax.broadcasted_iota(jnp.int32, (tr, SLAB_W), 1)
        main_ref[...] = jnp.zeros_like(main_ref)
        slab[...] = jnp.zeros_like(slab)
        for j in range(N_DEV):
            slab[:, :cols] = jnp.where(me == j, own_ref[...], land_ref[j])
            v = slab[...]
            segs = _w_in_segments(j, cols)
            for lo, n, where, dst in segs:
                part = v if len(segs) == 1 else jnp.where((lane >= lo) & (lane < lo + n), v, jnp.zeros_like(v))
                if where == "fg":
                    main_ref[:, MAIN_W:] = pltpu.roll(part, (SLAB_W - lo) % SLAB_W, 1)[:, :LANES]
                else:
                    base, r = _window(dst - lo, MAIN_W)
                    assert 0 <= r and lo + n + r <= SLAB_W
                    main_ref[:, base:base + SLAB_W] += pltpu.roll(part, r, 1)

    return pl.pallas_call(
        body, name="w_unpack", grid=(D // tr,),
        in_specs=[pl.BlockSpec((N_DEV, tr, cols), lambda i: (0, i, 0)), pl.BlockSpec((tr, cols), lambda i: (i, 0))],
        out_specs=pl.BlockSpec((tr, ALL_W), lambda i: (i, 0)),
        out_shape=jax.ShapeDtypeStruct((D, ALL_W), BF16),
        scratch_shapes=[pltpu.VMEM((tr, SLAB_W), BF16)],
        compiler_params=_params(("parallel",)),
    )(land, own)


def _g_pack(dw_main, dw_fg, cols, *, tr):
    D = dw_main.shape[0]

    def body(main_ref, fg_ref, out_ref, own_ref, slab):
        x, y, c = _my_place()
        me = 4 * x + 2 * y + c
        lane = lax.broadcasted_iota(jnp.int32, (tr, SLAB_W), 1)
        slab[...] = jnp.zeros_like(slab)
        slab[:, :LANES] = fg_ref[...]
        fg = slab[...]
        for j in range(N_DEV):
            v = None
            for lo, n, where, src in _w_in_segments(j, cols):
                if where == "fg":
                    part = pltpu.roll(fg, lo, 1)
                else:
                    base, r = _window(src - lo, MAIN_W)
                    assert 0 <= r and lo + n + r <= SLAB_W
                    part = pltpu.roll(main_ref[:, base:base + SLAB_W], (SLAB_W - r) % SLAB_W, 1)
                v = part if v is None else jnp.where((lane >= lo) & (lane < lo + n), part, v)
            out_ref[j] = v[:, :cols]

            @pl.when(me == j)
            def _():
                own_ref[...] = v[:, :cols]

    return pl.pallas_call(
        body, name="g_pack", grid=(D // tr,),
        in_specs=[pl.BlockSpec((tr, MAIN_W), lambda i: (i, 0)), pl.BlockSpec((tr, LANES), lambda i: (i, 0))],
        out_specs=(pl.BlockSpec((N_DEV, tr, cols), lambda i: (0, i, 0)), pl.BlockSpec((tr, cols), lambda i: (i, 0))),
        out_shape=(jax.ShapeDtypeStruct((N_DEV, D, cols), BF16), jax.ShapeDtypeStruct((D, cols), BF16)),
        scratch_shapes=[pltpu.VMEM((tr, SLAB_W), BF16)],
        compiler_params=_params(("parallel",)),
    )(dw_main, dw_fg)


TQ = 1024
TS = 512
W_TILE_ROWS = 256


def _layer_fwd_proj(xb, w_in):
    qkv = _mm(xb, w_in, out_dtype=BF16, tm=2048, tn=512, tk=D_MODEL, name="mm_qkv", n_cols=QKV_W)
    rest = _mm(xb, w_in, out_dtype=F32, tm=2048, tn=512, tk=D_MODEL, name="mm_rest", n_off=QKV_W, n_cols=REST_W)
    fgp = _mm(xb, w_in, out_dtype=F32, tm=1024, tn=LANES, tk=D_MODEL, name="mm_fg", n_off=MAIN_W, n_cols=LANES)
    return qkv, rest, fgp


def _layer_fwd_mix(x, xbt, proj, wl):
    qkv, rest, fgp = proj
    cum = _fg_fwd(fgp, wl["b_f"])
    S = x.shape[0]
    cumr = (cum[:, :N_HEADS].T * LOG2E).reshape(N_HEADS, S // TQ, 1, TQ)
    o, y, lse = _attn_fwd(qkv, rest, cum, cumr, tq=TQ)
    y = _conv_fwd(rest, wl["conv_w"], y)
    y = _pool_fwd(rest, wl["pool_w"], wl["pool_scale"], y)
    z = _mm(y, wl["w_out"], out_dtype=F32, tm=2048, tn=512, tk=D_MODEL, name="mm_out", c=x, c_scale=ALPHA)
    out, outb, outbt, xhat, rstd = _ln_fwd(z, wl["ln_g"], wl["ln_b"], ts=TS)
    saved = dict(xbt=xbt, qkv=qkv, rest=rest, fgp=fgp, cum=cum, cumr=cumr, o=o, lse=lse, y=y, xhat=xhat, rstd=rstd)
    return out, outb, outbt, saved


def _layer_bwd_mix(dout, sv, wl):
    S = dout.shape[0]
    dz, dzb, dln_g, dln_b = _ln_bwd(dout, sv["xhat"], sv["rstd"], wl["ln_g"], ts=TS)
    dy = _mm(dzb, wl["w_out"], out_dtype=F32, tm=2048, tn=512, tk=D_MODEL, name="mm_dy", bt=True)
    dw_out = _mm(sv["y"], dzb, out_dtype=BF16, tm=512, tn=D_MODEL, tk=2048, name="mm_dw_out", at=True)
    rest = sv["rest"]
    do, dproj, delta = _attn_bwd_pre(rest, sv["o"], dy, tq=TQ)
    a2 = sv["cumr"] - sv["lse"] * LOG2E
    dproj, ds_q, ds_k = _attn_bwd(sv["qkv"], do, a2, delta, sv["cum"], dproj, tq=TQ)
    dcum = jnp.pad(ds_q.reshape(N_HEADS, S).T, ((0, 0), (0, LANES - N_HEADS))) - ds_k
    dproj, db_f = _fg_bwd(dcum, sv["fgp"], wl["b_f"], dproj)
    dproj, dconv_w = _conv_bwd(rest, dy, wl["conv_w"], dproj)
    dproj, dpool_w, dpool_scale = _pool_bwd(rest, dy, wl["pool_w"], wl["pool_scale"], dproj)
    grads = dict(b_f=db_f[:, :N_HEADS], conv_w=dconv_w, pool_w=dpool_w, pool_scale=dpool_scale, w_out=dw_out,
                 ln_g=dln_g, ln_b=dln_b)
    return grads, (dproj, dz)


def _layer_bwd_w_in(mid, sv):
    dproj, dz = mid
    xt = sv["xbt"]
    S = xt.shape[1]
    dw_main = _mm(xt, dproj, out_dtype=BF16, tm=1024, tn=512, tk=S, name="mm_dw_main", n_cols=MAIN_W)
    dw_fg = _mm(xt, dproj, out_dtype=BF16, tm=1024, tn=LANES, tk=2048, name="mm_dw_fg", n_off=MAIN_W, n_cols=LANES)
    return (dw_main, dw_fg), (dproj, dz)


def _layer_bwd_input(ctx, wl, after):
    dproj, dz = ctx
    return _mm(dproj, wl["w_in"], out_dtype=F32, tm=1024, tn=256, tk=ALL_W, name="mm_dx", c=dz, c_scale=ALPHA, bt=True,
               after=after)


def kernel(x, w_in, b_f, conv_w, pool_w, pool_scale, w_out, ln_g, ln_b, loss_target, m_w_in, m_b_f, m_conv_w, m_pool_w, m_pool_scale, m_w_out, m_ln_g, m_ln_b, v_w_in, v_b_f, v_conv_w, v_pool_w, v_pool_scale, v_w_out, v_ln_g, v_ln_b):
    L, D, cols = w_in.shape
    rows_out = w_out.shape[1]
    ccols = conv_w.shape[2]
    mx, my, mc = _my_place()
    me = 4 * mx + 2 * my + mc

    gathers = []
    for l in range(L):
        gathers.append(_xchg_start([w_in[l].astype(BF16), w_out[l].astype(BF16), conv_w[l]], gather=[True] * 3,
                                   name=f"gather_start_{l}",
                                   peers=(SIBLING,) + SAME_CORE if l == 0 else EVERYONE,
                                   after=gathers[-1]["token"] if gathers else None))
    started = gathers[-1]["token"]

    xl = x[0]
    xb, xbt = _prep(xl, started, ts=TS)
    weights, saved = [], []
    for l in range(L):
        (own_in, own_out, own_cw), (g_in, g_out, g_cw) = _xchg_wait(gathers[l], xb, name=f"gather_wait_{l}")
        if l == 0:
            g_in, g_out, g_cw = _relay_wait(_relay_start([g_in, g_out, g_cw], name="gather_relay_start"), xb,
                                            name="gather_relay_wait")
        w_all = _w_unpack(g_in, own_in, tr=W_TILE_ROWS)
        proj = _layer_fwd_proj(xb, w_all)
        w_o = _with_own(g_out, own_out, me).reshape(N_DEV * rows_out, D)
        cw = jnp.transpose(_with_own(g_cw, own_cw, me), (1, 0, 2)).reshape(CONV_TAPS, N_DEV * ccols)
        wl = dict(w_in=w_all, w_out=w_o, conv_w=cw,
                  b_f=jnp.pad(b_f[l][None, :], ((0, 0), (0, LANES - N_HEADS))), pool_w=pool_w[l],
                  pool_scale=pool_scale[l][None, :], ln_g=ln_g[l][None, :], ln_b=ln_b[l][None, :])
        xl, xb, xbt, sv = _layer_fwd_mix(xl, xbt, proj, wl)
        weights.append(wl)
        saved.append(sv)

    loss, dx = _loss_head(xl, loss_target[0], ts=TS)
    loss = lax.psum(loss[0, 0], ("x", "y", "c"))

    names = ["w_in", "w_out", "conv_w", "b_f", "pool_w", "pool_scale", "ln_g", "ln_b"]
    sharded = names[:3]
    scatters, own_in = [None] * L, [None] * L
    for l in reversed(range(L)):
        g, mid = _layer_bwd_mix(dx, saved[l], weights[l])
        (dw_main, dw_fg), ctx = _layer_bwd_w_in(mid, saved[l])
        g["w_in"], own_in[l] = _g_pack(dw_main, dw_fg, cols, tr=W_TILE_ROWS)
        g["w_out"] = g["w_out"].reshape(N_DEV, rows_out, D)
        g["conv_w"] = jnp.transpose(g["conv_w"].reshape(CONV_TAPS, N_DEV, ccols), (1, 0, 2))
        scatters[l] = _xchg_start([g[k] for k in names], gather=[k not in sharded for k in names],
                                  name=f"scatter_start_{l}")
        dx = _layer_bwd_input(ctx, weights[l], scatters[l]["token"])

    given = dict(w_in=(w_in, m_w_in, v_w_in), b_f=(b_f, m_b_f, v_b_f), conv_w=(conv_w, m_conv_w, v_conv_w),
                 pool_w=(pool_w, m_pool_w, v_pool_w), pool_scale=(pool_scale, m_pool_scale, v_pool_scale),
                 w_out=(w_out, m_w_out, v_w_out), ln_g=(ln_g, m_ln_g, v_ln_g), ln_b=(ln_b, m_ln_b, v_ln_b))
    sent, landed = [{} for _ in range(L)], [{} for _ in range(L)]

    def wait(l, after):
        s_, l_ = _xchg_wait(scatters[l], after, name=f"scatter_wait_{l}")
        sent[l].update(zip(names, s_))
        landed[l].update(zip(names, l_))

    last_start = scatters[0]["token"][0, 0]

    def update(k, layers, into=None):
        w, m, v = given[k]
        C = w.shape[-1]
        if k == "w_in":
            m, v = m + last_start, v + last_start
        w2, m2, v2 = (t.reshape(-1, C) for t in (w, m, v))
        rp = w2.shape[0] // L
        lands = [landed[l][k].reshape(N_DEV, rp, C) for l in layers]
        srcs = [own_in[l] if k == "w_in" else sent[l][k].reshape((N_DEV, rp, C) if k in sharded else (rp, C))
                for l in layers]
        return _adamw(lands, srcs, w2, m2, v2, name=f"adamw_{k}_{layers[0]}", first=layers[0], into=into)

    big = ["w_in", "w_out", "pool_w"]
    for l in reversed(range(1, L)):
        wait(l, dx)
    upper = {k: update(k, list(range(1, L))) for k in big} if L > 1 else {}
    wait(0, [upper[k][0] for k in big] if upper else dx)
    res = {k: update(k, [0], into=upper[k]) if k in upper else update(k, list(range(L))) for k in names}
    res = {k: [t.reshape(given[k][0].shape) for t in outs] for k, outs in res.items()}

    order = ["w_in", "b_f", "conv_w", "pool_w", "pool_scale", "w_out", "ln_g", "ln_b"]
    return (loss, dx[None], *[res[k][0] for k in order], *[res[k][1] for k in order],
            *[res[k][2] for k in order], *[res[k][3] for k in order])
```

```python
import jax
import jax.numpy as jnp
from jax import lax
from jax.experimental import pallas as pl
from jax.experimental.pallas import tpu as pltpu

F32 = jnp.float32
BF16 = jnp.bfloat16

N_DEV = 8
D_MODEL = 2048
N_HEADS = 8
HEAD_DIM = 128
ATT_W = N_HEADS * HEAD_DIM
CONV_W = 512
CONV_TAPS = 3
POOL_W = 512
POOL_GROUP = 128
N_POOL = POOL_W // POOL_GROUP
FG_AT = 4 * ATT_W
IN_W = FG_AT + N_HEADS + 4 * CONV_W + 2 * POOL_W
MAIN_W = IN_W - N_HEADS
LANES = 128
ALL_W = MAIN_W + LANES
LN_EPS = 1e-5
DEPTH = 4
ALPHA = (2 * DEPTH) ** 0.25
SCALE = HEAD_DIM ** -0.5
LOG2E = 1.4426950408889634
LN2 = 0.6931471805599453
NEG = -1e30

ADAM_LR, ADAM_B1, ADAM_B2, ADAM_EPS, ADAM_WD, ADAM_STEP = 0.001, 0.9, 0.999, 1e-08, 0.01, 10

QKV_W = 3 * ATT_W
REST_W = MAIN_W - QKV_W
QB, KB, VB = 0, 8, 16
GB = 0
CBB, CCB, CHB, CGB = 8, 12, 16, 20
PUB, PGB = 24, 28
FGB = MAIN_W // LANES
VMEM_LIMIT = 48 * 1024 * 1024
ADAMW_BLOCK_ELEMS = 64 * 1024

MESH = pl.DeviceIdType.MESH
ANY = pl.BlockSpec(memory_space=pl.ANY)


def _params(semantics):
    return pltpu.CompilerParams(dimension_semantics=semantics, vmem_limit_bytes=VMEM_LIMIT)


def _sigmoid(x):
    return 1.0 / (1.0 + jnp.exp(-x))


def _mm(a, b, *, out_dtype, tm, tn, tk, name, c=None, c_scale=1.0, n_off=0, n_cols=None, bt=False, at=False,
        after=None):
    M, K = a.shape[::-1] if at else a.shape
    N = b.shape[0 if bt else 1] if n_cols is None else n_cols
    assert not (bt and n_off)
    dims = (((0 if at else 1,), (1 if bt else 0,)), ((), ()))
    tm, tn, tk = min(tm, M), min(tn, N), min(tk, K)
    assert M % tm == 0 and N % tn == 0 and K % tk == 0 and n_off % tn == 0, (a.shape, b.shape, tm, tn, tk)
    nk, joff = K // tk, n_off // tn
    has_c, has_after = c is not None, after is not None

    def body(*refs):
        a_ref, b_ref = refs[0], refs[1]
        c_ref = refs[2] if has_c else None
        o_ref = refs[2 + has_c + has_after]

        def finish(r):
            if has_c:
                r = r + c_scale * c_ref[...]
            o_ref[...] = r.astype(out_dtype)

        if nk == 1:
            finish(lax.dot_general(a_ref[...], b_ref[...], dims, preferred_element_type=F32))
        else:
            acc_ref = refs[3 + has_c + has_after]
            k = pl.program_id(2)

            @pl.when(k == 0)
            def _():
                acc_ref[...] = jnp.zeros_like(acc_ref)

            acc_ref[...] += lax.dot_general(a_ref[...], b_ref[...], dims, preferred_element_type=F32)

            @pl.when(k == nk - 1)
            def _():
                finish(acc_ref[...])

    in_specs = [pl.BlockSpec((tk, tm), lambda i, j, k: (k, i)) if at else pl.BlockSpec((tm, tk), lambda i, j, k: (i, k)),
                pl.BlockSpec((tn, tk), lambda i, j, k: (j, k)) if bt
                else pl.BlockSpec((tk, tn), lambda i, j, k: (k, j + joff))]
    args = [a, b]
    if has_c:
        in_specs.append(pl.BlockSpec((tm, tn), lambda i, j, k: (i, j)))
        args.append(c)
    if has_after:
        in_specs.append(ANY)
        args.append(after)
    return pl.pallas_call(
        body, name=name, grid=(M // tm, N // tn, nk),
        in_specs=in_specs,
        out_specs=pl.BlockSpec((tm, tn), lambda i, j, k: (i, j)),
        out_shape=jax.ShapeDtypeStruct((M, N), out_dtype),
        scratch_shapes=[pltpu.VMEM((tm, tn), F32)] if nk > 1 else [],
        compiler_params=_params(("parallel", "parallel", "arbitrary")),
    )(*args)


def _shift_down(x, k, row):
    return jnp.where(row >= k, pltpu.roll(x, k, 0), 0.0)


def _shift_up(x, k, row):
    s = x.shape[0]
    return jnp.where(row < s - k, pltpu.roll(x, s - k, 0), 0.0)


def _fg_fwd(fgp, bf):
    S = fgp.shape[0]

    def body(f_ref, b_ref, cum_ref):
        a = f_ref[...] + b_ref[...]
        x = jnp.minimum(a, 0.0) - jnp.log(1.0 + jnp.exp(-jnp.abs(a)))
        row = lax.broadcasted_iota(jnp.int32, x.shape, 0)
        k = 1
        while k < S:
            x = x + _shift_down(x, k, row)
            k *= 2
        cum_ref[...] = x

    return pl.pallas_call(body, name="fg_fwd", out_shape=jax.ShapeDtypeStruct((S, LANES), F32),
                          compiler_params=_params(None))(fgp, bf)


def _fg_bwd(dcum, fgp, bf, dproj):
    S = fgp.shape[0]

    def body(d_ref, f_ref, b_ref, dproj_in, dfg_ref, dbf_ref):
        x = d_ref[...]
        row = lax.broadcasted_iota(jnp.int32, x.shape, 0)
        k = 1
        while k < S:
            x = x + _shift_up(x, k, row)
            k *= 2
        dfg = x * _sigmoid(-(f_ref[...] + b_ref[...]))
        dfg_ref[...] = dfg.astype(BF16)
        dbf_ref[...] = jnp.sum(dfg, axis=0, keepdims=True)

    whole = pl.BlockSpec((S, LANES), lambda i: (0, 0))
    vec = pl.BlockSpec((1, LANES), lambda i: (0, 0))
    return pl.pallas_call(
        body, name="fg_bwd", grid=(1,),
        in_specs=[whole, whole, vec, ANY],
        out_specs=(pl.BlockSpec((S, LANES), lambda i: (0, FGB)), vec),
        out_shape=(jax.ShapeDtypeStruct(dproj.shape, dproj.dtype), jax.ShapeDtypeStruct((1, LANES), F32)),
        input_output_aliases={3: 0},
        compiler_params=_params(("arbitrary",)),
    )(dcum, fgp, bf, dproj)


def _put_columns(pieces, dst, blocks, sems):
    copies = [pltpu.make_async_copy(p, dst.at[:, pl.ds(pl.multiple_of(b * LANES, LANES), LANES)], sems.at[i])
              for i, (p, b) in enumerate(zip(pieces, blocks))]
    for cp in copies:
        cp.start()
    for cp in copies:
        cp.wait()


def _colblock(S, off):
    return pl.BlockSpec((S, LANES), lambda h: (0, h + off))


def _head_row(nq, tq):
    return pl.BlockSpec((None, nq, 1, tq), lambda h: (h, 0, 0, 0))


def _lane_of(ref, h):
    lane = lax.broadcasted_iota(jnp.int32, ref.shape, 1)
    return jnp.sum(jnp.where(lane == h, ref[...], 0.0), axis=1, keepdims=True)


def _attn_fwd(qkv, rest, cum, cumr, *, tq):
    S = qkv.shape[0]
    nq = S // tq

    def body(q_ref, k_ref, v_ref, g_ref, cum_ref, cumr_ref, o_ref, y_ref, lse_ref, cc):
        cc[...] = _lane_of(cum_ref, pl.program_id(0)) * LOG2E
        half = tq // 2

        def q_step(qi, _):
            rows = pl.ds(pl.multiple_of(qi * tq, tq), tq)
            q = q_ref[rows, :]
            ci = cumr_ref[qi]

            def part(cols, q, ci, carry, masked):
                m, l, acc = carry
                s = lax.dot_general(k_ref[cols, :], q, (((1,), (1,)), ((), ())), preferred_element_type=F32)
                s = s * (SCALE * LOG2E) + ci - cc[cols, :]
                if masked:
                    s = jnp.where(lax.broadcasted_iota(jnp.int32, s.shape, 0) <= lax.broadcasted_iota(jnp.int32, s.shape, 1),
                                  s, NEG)
                m_new = jnp.maximum(m, jnp.max(s, axis=0, keepdims=True))
                a = jnp.exp2(m - m_new)
                p = jnp.exp2(s - m_new)
                l = a * l + jnp.sum(p, axis=0, keepdims=True)
                acc = a * acc + lax.dot_general(v_ref[cols, :], p.astype(BF16), (((0,), (0,)), ((), ())),
                                                preferred_element_type=F32)
                return m_new, l, acc

            def tile(kj, carry):
                return part(pl.ds(pl.multiple_of(kj * tq, tq), tq), q, ci, carry, False)

            init = (jnp.full((1, tq), NEG, F32), jnp.zeros((1, tq), F32), jnp.zeros((HEAD_DIM, tq), F32))
            carry = lax.fori_loop(0, qi, tile, init)
            c0 = pl.multiple_of(qi * tq, tq)
            m, l, acc = part(pl.ds(c0, half), q, ci, carry, True)
            late = part(pl.ds(c0 + half, half), q[half:], ci[:, half:], (m[:, half:], l[:, half:], acc[:, half:]), True)
            m, l, acc = (jnp.concatenate([x[:, :half], y], axis=1) for x, y in zip((m, l, acc), late))
            o = (acc / l).T
            g = g_ref[rows, :]
            o_ref[rows, :] = o
            y_ref[rows, :] = (o * (g * _sigmoid(g))).astype(BF16)
            lse_ref[qi] = m * LN2 + jnp.log(l)
            return 0

        lax.fori_loop(0, nq, q_step, 0)

    return pl.pallas_call(
        body, name="attn_fwd", grid=(N_HEADS,),
        in_specs=[_colblock(S, QB), _colblock(S, KB), _colblock(S, VB), _colblock(S, GB),
                  pl.BlockSpec((S, LANES), lambda h: (0, 0)), _head_row(nq, tq)],
        out_specs=(_colblock(S, 0), _colblock(S, 0), _head_row(nq, tq)),
        out_shape=(jax.ShapeDtypeStruct((S, ATT_W), F32), jax.ShapeDtypeStruct((S, D_MODEL), BF16),
                   jax.ShapeDtypeStruct((N_HEADS, nq, 1, tq), F32)),
        scratch_shapes=[pltpu.VMEM((S, 1), F32)],
        compiler_params=_params(("arbitrary",)),
    )(qkv, qkv, qkv, rest, cum, cumr)


def _attn_bwd_pre(rest, o, dy, *, tq):
    S = o.shape[0]
    nq = S // tq

    def body(g_ref, o_ref, dy_ref, do_ref, dg_ref, dl_ref):
        g = g_ref[...]
        sg = _sigmoid(g)
        ov = o_ref[...]
        dy = dy_ref[...]
        do = dy * (g * sg)
        do_ref[...] = do.astype(BF16)
        dg_ref[...] = (dy * ov * (sg * (1.0 + g * (1.0 - sg)))).astype(BF16)
        rows = lax.dot_general(jnp.ones((8, HEAD_DIM), F32), do * ov, (((1,), (1,)), ((), ())),
                               precision=lax.Precision.HIGHEST, preferred_element_type=F32)
        for qi in range(nq):
            dl_ref[qi] = rows[0:1, qi * tq:(qi + 1) * tq]

    att = jax.ShapeDtypeStruct((S, ATT_W), BF16)
    return pl.pallas_call(
        body, name="attn_bwd_pre", grid=(N_HEADS,),
        in_specs=[_colblock(S, GB), _colblock(S, 0), _colblock(S, 0)],
        out_specs=(_colblock(S, 0), _colblock(S, QKV_W // LANES), _head_row(nq, tq)),
        out_shape=(att, jax.ShapeDtypeStruct((S, ALL_W), BF16), jax.ShapeDtypeStruct((N_HEADS, nq, 1, tq), F32)),
        compiler_params=_params(("parallel",)),
    )(rest, o, dy)


def _attn_bwd(qkv, do, a2, delta, cum, dproj, *, tq):
    S = qkv.shape[0]
    nq = S // tq
    tdot = (((0,), (0,)), ((), ()))
    ndot = (((1,), (1,)), ((), ()))

    def body(q_ref, k_ref, v_ref, do_ref, a_ref, dl_ref, cum_ref, dproj_in, dproj_ref, dsq_ref, dsk_ref,
             dqa, dq_ref, dk_ref, dv_ref, cc, sems):
        h = pl.program_id(0)
        dqa[...] = jnp.zeros_like(dqa)
        dsq_ref[...] = jnp.zeros_like(dsq_ref)
        cc[...] = _lane_of(cum_ref, h) * LOG2E

        @pl.when(h == 0)
        def _():
            dsk_ref[...] = jnp.zeros_like(dsk_ref)

        half = tq // 2
        mine = lax.broadcasted_iota(jnp.int32, (tq, LANES), 1) == h

        def k_step(kj, _):
            c0 = pl.multiple_of(kj * tq, tq)
            cols = pl.ds(c0, tq)
            k = k_ref[cols, :]
            v = v_ref[cols, :]
            cj = cc[cols, :]

            def part(k, v, cj, qi, q0, first, carry, masked):
                dk, dv, ksum = carry
                nqs = tq - first
                rows = pl.ds(q0, nqs)
                q = q_ref[rows, :]
                dot = do_ref[rows, :]
                s = lax.dot_general(k, q, ndot, preferred_element_type=F32)
                p = jnp.exp2(s * (SCALE * LOG2E) + a_ref[qi][:, first:] - cj)
                if masked:
                    p = jnp.where(lax.broadcasted_iota(jnp.int32, p.shape, 0) <= lax.broadcasted_iota(jnp.int32, p.shape, 1),
                                  p, 0.0)
                dv = dv + jnp.dot(p.astype(BF16), dot, preferred_element_type=F32)
                dp = lax.dot_general(v, dot, ndot, preferred_element_type=F32)
                ds = p * (dp - dl_ref[qi][:, first:])
                over_keys = jnp.sum(ds, axis=0, keepdims=True)
                if first:
                    over_keys = jnp.concatenate([jnp.zeros((1, first), F32), over_keys], axis=1)
                dsq_ref[qi] += over_keys
                ksum = ksum + jnp.sum(ds, axis=1, keepdims=True)
                dsb = (ds * SCALE).astype(BF16)
                dk = dk + jnp.dot(dsb, q, preferred_element_type=F32)
                dqa[rows, :] += lax.dot_general(dsb, k, tdot, preferred_element_type=F32)
                return dk, dv, ksum

            def zeros(n):
                return (jnp.zeros((n, HEAD_DIM), F32), jnp.zeros((n, HEAD_DIM), F32), jnp.zeros((n, 1), F32))

            early = part(k[:half], v[:half], cj[:half], kj, c0, 0, zeros(half), True)
            late = part(k[half:], v[half:], cj[half:], kj, c0 + half, half, zeros(half), True)
            carry = tuple(jnp.concatenate([x, y], axis=0) for x, y in zip(early, late))
            dk, dv, ksum = lax.fori_loop(
                kj + 1, nq, lambda qi, c: part(k, v, cj, qi, pl.multiple_of(qi * tq, tq), 0, c, False), carry)
            dk_ref[cols, :] = dk.astype(BF16)
            dv_ref[cols, :] = dv.astype(BF16)
            dsk_ref[cols, :] += jnp.where(mine, ksum, 0.0)
            return 0

        lax.fori_loop(0, nq, k_step, 0)
        dq_ref[...] = dqa[...].astype(BF16)
        _put_columns([dq_ref, dk_ref, dv_ref], dproj_ref, [QB + h, KB + h, VB + h], sems)

    head = pltpu.VMEM((S, HEAD_DIM), BF16)
    return pl.pallas_call(
        body, name="attn_bwd", grid=(N_HEADS,),
        in_specs=[_colblock(S, QB), _colblock(S, KB), _colblock(S, VB), _colblock(S, 0),
                  _head_row(nq, tq), _head_row(nq, tq), pl.BlockSpec((S, LANES), lambda h: (0, 0)), ANY],
        out_specs=(ANY, _head_row(nq, tq), pl.BlockSpec((S, LANES), lambda h: (0, 0))),
        out_shape=(jax.ShapeDtypeStruct(dproj.shape, dproj.dtype), jax.ShapeDtypeStruct((N_HEADS, nq, 1, tq), F32),
                   jax.ShapeDtypeStruct((S, LANES), F32)),
        input_output_aliases={7: 0},
        scratch_shapes=[pltpu.VMEM((S, HEAD_DIM), F32), head, head, head, pltpu.VMEM((S, 1), F32),
                        pltpu.SemaphoreType.DMA((3,))],
        compiler_params=_params(("arbitrary",)),
    )(qkv, qkv, qkv, do, a2, delta, cum, dproj)


def _conv_taps(u, w_ref, row):
    return (w_ref[0:1, :] * _shift_down(u, 2, row) + w_ref[1:2, :] * _shift_down(u, 1, row)) + w_ref[2:3, :] * u


def _conv_fwd(rest, cw, y):
    S = rest.shape[0]

    def body(cb_ref, cc_ref, ch_ref, g_ref, w_ref, y_in, y_ref):
        row = lax.broadcasted_iota(jnp.int32, (S, LANES), 0)
        g = g_ref[...]
        y = _conv_taps(cc_ref[...] * ch_ref[...], w_ref, row)
        y_ref[...] = ((cb_ref[...] * y) * (g * _sigmoid(g))).astype(BF16)

    return pl.pallas_call(
        body, name="conv_fwd", grid=(CONV_W // LANES,),
        in_specs=[_colblock(S, CBB), _colblock(S, CCB), _colblock(S, CHB), _colblock(S, CGB),
                  pl.BlockSpec((CONV_TAPS, LANES), lambda j: (0, j)), ANY],
        out_specs=_colblock(S, ATT_W // LANES),
        out_shape=jax.ShapeDtypeStruct(y.shape, y.dtype),
        input_output_aliases={5: 0},
        compiler_params=_params(("parallel",)),
    )(rest, rest, rest, rest, cw, y)


def _conv_bwd(rest, dy, cw, dproj):
    S = rest.shape[0]

    def body(cb_ref, cc_ref, ch_ref, g_ref, dy_ref, w_ref, dproj_in, dproj_ref, dw_ref,
             dcb_ref, dcc_ref, dch_ref, dg_ref, sems):
        row = lax.broadcasted_iota(jnp.int32, (S, LANES), 0)
        g = g_ref[...]
        sg = _sigmoid(g)
        silu = g * sg
        cb, cc, ch, dy = cb_ref[...], cc_ref[...], ch_ref[...], dy_ref[...]
        u = cc * ch
        u1 = _shift_down(u, 1, row)
        u2 = _shift_down(u, 2, row)
        y = (w_ref[0:1, :] * u2 + w_ref[1:2, :] * u1) + w_ref[2:3, :] * u
        dcb_ref[...] = (dy * silu * y).astype(BF16)
        dg_ref[...] = (dy * (cb * y) * (sg * (1.0 + g * (1.0 - sg)))).astype(BF16)
        dyv = dy * silu * cb
        du = w_ref[2:3, :] * dyv + w_ref[1:2, :] * _shift_up(dyv, 1, row) + w_ref[0:1, :] * _shift_up(dyv, 2, row)
        dcc_ref[...] = (du * ch).astype(BF16)
        dch_ref[...] = (du * cc).astype(BF16)
        dw_ref[0:1, :] = jnp.sum(dyv * u2, axis=0, keepdims=True)
        dw_ref[1:2, :] = jnp.sum(dyv * u1, axis=0, keepdims=True)
        dw_ref[2:3, :] = jnp.sum(dyv * u, axis=0, keepdims=True)
        j = pl.program_id(0)
        first = QKV_W // LANES
        _put_columns([dcb_ref, dcc_ref, dch_ref, dg_ref], dproj_ref,
                     [first + CBB + j, first + CCB + j, first + CHB + j, first + CGB + j], sems)

    piece = pltpu.VMEM((S, LANES), BF16)
    return pl.pallas_call(
        body, name="conv_bwd", grid=(CONV_W // LANES,),
        in_specs=[_colblock(S, CBB), _colblock(S, CCB), _colblock(S, CHB), _colblock(S, CGB),
                  _colblock(S, ATT_W // LANES), pl.BlockSpec((CONV_TAPS, LANES), lambda j: (0, j)), ANY],
        out_specs=(ANY, pl.BlockSpec((CONV_TAPS, LANES), lambda j: (0, j))),
        out_shape=(jax.ShapeDtypeStruct(dproj.shape, dproj.dtype), jax.ShapeDtypeStruct((CONV_TAPS, CONV_W), F32)),
        input_output_aliases={6: 0},
        scratch_shapes=[piece] * 4 + [pltpu.SemaphoreType.DMA((4,))],
        compiler_params=_params(("arbitrary",)),
    )(rest, rest, rest, rest, dy, cw, dproj)


def _pool_z(u, grp, row):
    s2 = u + _shift_down(u, 1, row)
    s4 = s2 + _shift_down(s2, 2, row)
    s8 = s4 + _shift_down(s4, 4, row)
    s16 = s8 + _shift_down(s8, 8, row)
    sw = jnp.where(grp == 0, s2, jnp.where(grp == 1, s4, jnp.where(grp == 2, s8, s16)))
    return sw / _pool_count(grp, row) - u


def _pool_count(grp, row):
    return jnp.minimum(row + 1, 2 << grp).astype(F32)


def _pool_fwd(rest, pw, scale, y):
    S = rest.shape[0]

    def body(u_ref, g_ref, pw_ref, sc_ref, y_in, y_ref):
        grp = pl.program_id(0)
        row = lax.broadcasted_iota(jnp.int32, (S, LANES), 0)
        z = _pool_z(u_ref[...], grp, row)
        zp = jnp.dot(z.astype(BF16), pw_ref[...].astype(BF16), preferred_element_type=F32)
        g = g_ref[...]
        y_ref[...] = ((zp * sc_ref[...]) * (g * _sigmoid(g))).astype(BF16)

    return pl.pallas_call(
        body, name="pool_fwd", grid=(N_POOL,),
        in_specs=[_colblock(S, PUB), _colblock(S, PGB),
                  pl.BlockSpec((None, POOL_GROUP, POOL_GROUP), lambda j: (j, 0, 0)),
                  pl.BlockSpec((1, LANES), lambda j: (0, j)), ANY],
        out_specs=_colblock(S, (ATT_W + CONV_W) // LANES),
        out_shape=jax.ShapeDtypeStruct(y.shape, y.dtype),
        input_output_aliases={4: 0},
        compiler_params=_params(("parallel",)),
    )(rest, rest, pw, scale, y)


def _pool_bwd(rest, dy, pw, scale, dproj):
    S = rest.shape[0]

    def body(u_ref, g_ref, dy_ref, pw_ref, sc_ref, dproj_in, dproj_ref, dpw_ref, dsc_ref, du_ref, dg_ref, sems):
        grp = pl.program_id(0)
        row = lax.broadcasted_iota(jnp.int32, (S, LANES), 0)
        z = _pool_z(u_ref[...], grp, row).astype(BF16)
        pwb = pw_ref[...].astype(BF16)
        zp = jnp.dot(z, pwb, preferred_element_type=F32)
        g = g_ref[...]
        sg = _sigmoid(g)
        silu = g * sg
        dy = dy_ref[...]
        sc = sc_ref[...]
        dsc_ref[...] = jnp.sum(dy * silu * zp, axis=0, keepdims=True)
        dg_ref[...] = (dy * (zp * sc) * (sg * (1.0 + g * (1.0 - sg)))).astype(BF16)
        dzp = (dy * silu * sc).astype(BF16)
        dpw_ref[...] = lax.dot_general(z, dzp, (((0,), (0,)), ((), ())), preferred_element_type=F32)
        dz = lax.dot_general(dzp, pwb, (((1,), (1,)), ((), ())), preferred_element_type=F32)
        f1 = dz / _pool_count(grp, row)
        f2 = f1 + _shift_up(f1, 1, row)
        f4 = f2 + _shift_up(f2, 2, row)
        f8 = f4 + _shift_up(f4, 4, row)
        f16 = f8 + _shift_up(f8, 8, row)
        fw = jnp.where(grp == 0, f2, jnp.where(grp == 1, f4, jnp.where(grp == 2, f8, f16)))
        du_ref[...] = (fw - dz).astype(BF16)
        first = QKV_W // LANES
        _put_columns([du_ref, dg_ref], dproj_ref, [first + PUB + grp, first + PGB + grp], sems)

    piece = pltpu.VMEM((S, LANES), BF16)
    return pl.pallas_call(
        body, name="pool_bwd", grid=(N_POOL,),
        in_specs=[_colblock(S, PUB), _colblock(S, PGB), _colblock(S, (ATT_W + CONV_W) // LANES),
                  pl.BlockSpec((None, POOL_GROUP, POOL_GROUP), lambda j: (j, 0, 0)),
                  pl.BlockSpec((1, LANES), lambda j: (0, j)), ANY],
        out_specs=(ANY, pl.BlockSpec((None, POOL_GROUP, POOL_GROUP), lambda j: (j, 0, 0)),
                   pl.BlockSpec((1, LANES), lambda j: (0, j))),
        out_shape=(jax.ShapeDtypeStruct(dproj.shape, dproj.dtype),
                   jax.ShapeDtypeStruct((N_POOL, POOL_GROUP, POOL_GROUP), F32), jax.ShapeDtypeStruct((1, POOL_W), F32)),
        input_output_aliases={5: 0},
        scratch_shapes=[piece] * 2 + [pltpu.SemaphoreType.DMA((2,))],
        compiler_params=_params(("arbitrary",)),
    )(rest, rest, dy, pw, scale, dproj)


def _ln_fwd(z, g, b, *, ts):
    S, D = z.shape

    def body(z_ref, g_ref, b_ref, o_ref, ob_ref, obt_ref, xh_ref, rs_ref):
        zz = z_ref[...]
        mu = jnp.mean(zz, axis=1, keepdims=True)
        zc = zz - mu
        rstd = lax.rsqrt(jnp.mean(zc * zc, axis=1, keepdims=True) + LN_EPS)
        xh = zc * rstd
        out = xh * g_ref[...] + b_ref[...]
        o_ref[...] = out
        ob_ref[...] = out.astype(BF16)
        obt_ref[...] = out.T.astype(BF16)
        xh_ref[...] = xh
        rs_ref[...] = rstd

    rowblk = pl.BlockSpec((ts, D), lambda i: (i, 0))
    vec = pl.BlockSpec((1, D), lambda i: (0, 0))
    return pl.pallas_call(
        body, name="ln_fwd", grid=(S // ts,),
        in_specs=[rowblk, vec, vec],
        out_specs=(rowblk, rowblk, pl.BlockSpec((D, ts), lambda i: (0, i)), rowblk,
                   pl.BlockSpec((ts, 1), lambda i: (i, 0))),
        out_shape=(jax.ShapeDtypeStruct((S, D), F32), jax.ShapeDtypeStruct((S, D), BF16),
                   jax.ShapeDtypeStruct((D, S), BF16), jax.ShapeDtypeStruct((S, D), F32),
                   jax.ShapeDtypeStruct((S, 1), F32)),
        compiler_params=_params(("parallel",)),
    )(z, g, b)


def _prep(x, after, *, ts):
    S, D = x.shape

    def body(x_ref, after_ref, xb_ref, xbt_ref):
        xb_ref[...] = x_ref[...].astype(BF16)
        xbt_ref[...] = x_ref[...].T.astype(BF16)

    return pl.pallas_call(
        body, name="prep", grid=(S // ts,),
        in_specs=[pl.BlockSpec((ts, D), lambda i: (i, 0)), ANY],
        out_specs=(pl.BlockSpec((ts, D), lambda i: (i, 0)), pl.BlockSpec((D, ts), lambda i: (0, i))),
        out_shape=(jax.ShapeDtypeStruct((S, D), BF16), jax.ShapeDtypeStruct((D, S), BF16)),
        compiler_params=_params(("parallel",)),
    )(x, after)


def _ln_bwd(dout, xhat, rstd, g, *, ts):
    S, D = dout.shape

    def body(d_ref, xh_ref, rs_ref, g_ref, dz_ref, dzb_ref, dg_ref, db_ref):
        @pl.when(pl.program_id(0) == 0)
        def _():
            dg_ref[...] = jnp.zeros_like(dg_ref)
            db_ref[...] = jnp.zeros_like(db_ref)

        d = d_ref[...]
        xh = xh_ref[...]
        dxh = d * g_ref[...]
        m1 = jnp.mean(dxh, axis=1, keepdims=True)
        m2 = jnp.mean(dxh * xh, axis=1, keepdims=True)
        dz = rs_ref[...] * (dxh - m1 - xh * m2)
        dz_ref[...] = dz
        dzb_ref[...] = dz.astype(BF16)
        dg_ref[...] += jnp.sum(d * xh, axis=0, keepdims=True)
        db_ref[...] += jnp.sum(d, axis=0, keepdims=True)

    rowblk = pl.BlockSpec((ts, D), lambda i: (i, 0))
    vec = pl.BlockSpec((1, D), lambda i: (0, 0))
    return pl.pallas_call(
        body, name="ln_bwd", grid=(S // ts,),
        in_specs=[rowblk, rowblk, pl.BlockSpec((ts, 1), lambda i: (i, 0)), vec],
        out_specs=(rowblk, rowblk, vec, vec),
        out_shape=(jax.ShapeDtypeStruct((S, D), F32), jax.ShapeDtypeStruct((S, D), BF16),
                   jax.ShapeDtypeStruct((1, D), F32), jax.ShapeDtypeStruct((1, D), F32)),
        compiler_params=_params(("arbitrary",)),
    )(dout, xhat, rstd, g)


def _loss_head(y, tgt, *, ts):
    S, D = y.shape

    def body(y_ref, t_ref, l_ref, dy_ref):
        @pl.when(pl.program_id(0) == 0)
        def _():
            l_ref[...] = jnp.zeros_like(l_ref)

        e = y_ref[...] - t_ref[...]
        dy_ref[...] = e * (1.0 / D)
        rowloss = jnp.sum(e * e, axis=1, keepdims=True) * (0.5 / D)
        l_ref[...] += jnp.sum(rowloss, axis=0, keepdims=True)

    rowblk = pl.BlockSpec((ts, D), lambda i: (i, 0))
    return pl.pallas_call(
        body, name="loss_head", grid=(S // ts,),
        in_specs=[rowblk, rowblk],
        out_specs=(pl.BlockSpec((1, 1), lambda i: (0, 0)), rowblk),
        out_shape=(jax.ShapeDtypeStruct((1, 1), F32), jax.ShapeDtypeStruct((S, D), F32)),
        compiler_params=_params(("arbitrary",)),
    )(y, tgt)


def _adamw(lands, srcs, w, m, v, *, name, first=0, into=None):
    R, C = w.shape
    npart = len(lands)
    rp = lands[0].shape[1]
    assert R % rp == 0 and all(p.shape == (N_DEV, rp, C) for p in lands)
    tiled = rp % 8 == 0
    assert tiled or (first == 0 and npart * rp == R and into is None)
    tr = min(ADAMW_BLOCK_ELEMS // (LANES * pl.cdiv(C, LANES)), rp) if tiled else rp
    assert rp % tr == 0
    steps = rp // tr
    c1 = 1.0 - ADAM_B1 ** ADAM_STEP
    c2 = 1.0 - ADAM_B2 ** ADAM_STEP
    slabbed = [p.ndim == 3 for p in srcs]
    n_into = 0 if into is None else 4

    def body(*refs):
        l_refs, s_refs = refs[:npart], refs[npart:2 * npart]
        w_ref, m_ref, v_ref = refs[2 * npart:2 * npart + 3]
        g_ref, d_ref, nm_ref, nv_ref = refs[2 * npart + 3 + n_into:]
        x, y, c = _my_place()
        me = 4 * x + 2 * y + c

        def update(l, rows):
            g = None
            for j in range(N_DEV):
                own = s_refs[l][j] if slabbed[l] else s_refs[l][...]
                p = jnp.where(me == j, own, l_refs[l][j]).astype(F32)
                g = p if g is None else g + p
            nm = ADAM_B1 * m_ref[rows, :] + (1.0 - ADAM_B1) * g
            nv = ADAM_B2 * v_ref[rows, :] + (1.0 - ADAM_B2) * (g * g)
            g_ref[rows, :] = g
            nm_ref[rows, :] = nm
            nv_ref[rows, :] = nv
            d_ref[rows, :] = -ADAM_LR * ((nm / c1) / (jnp.sqrt(nv / c2) + ADAM_EPS) + ADAM_WD * w_ref[rows, :])

        for l in range(npart):
            if tiled:
                pl.when(pl.program_id(0) == l)(lambda l=l: update(l, slice(None)))
            else:
                update(l, slice(l * rp, (l + 1) * rp))

    def part_spec(l, p):
        if p.ndim == 3:
            return pl.BlockSpec((N_DEV, tr, C), lambda q, i: (0, jnp.where(q == l, i, 0), 0))
        return pl.BlockSpec((tr, C), lambda q, i: (jnp.where(q == l, i, 0), 0))

    if tiled:
        blk = pl.BlockSpec((tr, C), lambda q, i: ((first + q) * steps + i, 0))
    else:
        blk = pl.BlockSpec((R, C), lambda q, i: (0, 0))
    out = jax.ShapeDtypeStruct((R, C), F32)
    n_in = 2 * npart + 3
    return pl.pallas_call(
        body, name=name, grid=(npart, steps) if tiled else (1, 1),
        in_specs=[part_spec(l, p) for l, p in enumerate(lands)] + [part_spec(l, p) for l, p in enumerate(srcs)]
        + [blk, blk, blk] + [ANY] * n_into,
        out_specs=(blk, blk, blk, blk),
        out_shape=(out, out, out, out),
        input_output_aliases={n_in + i: i for i in range(n_into)},
        compiler_params=_params(("arbitrary", "arbitrary")),
    )(*lands, *srcs, w, m, v, *(into or ()))


def _my_place():
    return lax.axis_index("x"), lax.axis_index("y"), lax.axis_index("c")


HBM = pl.BlockSpec(memory_space=pltpu.HBM)
SEM = pl.BlockSpec(memory_space=pltpu.SEMAPHORE)
DATAFLOW = pltpu.SideEffectType.DATAFLOW_SIDE_EFFECTING
EVERYONE = (1, 2, 3, 4, 5, 6, 7)
SIBLING = 1
SAME_CORE = (4, 2, 6)


def _xchg_copies(src, land, send_sem, recv_sem, gather, peers):
    x, y, c = _my_place()
    me = 4 * x + 2 * y + c
    copies = []
    for k, bits in enumerate(peers):
        px, py, pc = x ^ ((bits >> 2) & 1), y ^ ((bits >> 1) & 1), c ^ (bits & 1)
        peer = 4 * px + 2 * py + pc
        for a in range(len(src)):
            i = a * len(peers) + k
            out = src[a] if gather[a] else src[a].at[peer]
            sent, landed = [pltpu.make_async_remote_copy(
                src_ref=out, dst_ref=land[a].at[slab], send_sem=send_sem.at[i], recv_sem=recv_sem.at[i],
                device_id=(px, py, pc), device_id_type=MESH) for slab in (me, peer)]
            copies.append((sent, landed))
    return copies


def _xchg_start(srcs, *, gather, name, peers=EVERYONE, after=None):
    n = len(srcs)
    follows = [] if after is None else [after]
    lands = [lax.empty(((N_DEV,) + v.shape) if g else v.shape, v.dtype) for v, g in zip(srcs, gather)]

    def body(*refs):
        src, land = refs[:n], refs[n:2 * n]
        send_sem, recv_sem = refs[2 * n + len(follows)], refs[2 * n + len(follows) + 1]
        token = refs[-1]
        for sent, _ in _xchg_copies(src, land, send_sem, recv_sem, gather, peers):
            sent.start()
        token[...] = jnp.zeros_like(token)

    sems = pltpu.SemaphoreType.DMA((n * len(peers),))
    outs = pl.pallas_call(
        body, name=name,
        out_shape=(sems, sems, *[pltpu.HBM(v.shape, v.dtype) for v in srcs + lands],
                   jax.ShapeDtypeStruct((8, LANES), F32)),
        in_specs=[HBM] * (2 * n) + [ANY] * len(follows),
        out_specs=(SEM, SEM, *[HBM] * (2 * n), pl.BlockSpec(memory_space=pltpu.VMEM)),
        input_output_aliases={i: 2 + i for i in range(2 * n)},
        compiler_params=pltpu.CompilerParams(has_side_effects=DATAFLOW),
    )(*[pltpu.with_memory_space_constraint(v, pltpu.HBM) for v in srcs + lands], *follows)
    return dict(send=outs[0], recv=outs[1], srcs=list(outs[2:2 + n]), lands=list(outs[2 + n:2 + 2 * n]),
                token=outs[-1], gather=gather, peers=peers)


def _xchg_wait(h, after, *, name):
    n = len(h["srcs"])
    gather, peers = h["gather"], h["peers"]
    after = list(after) if isinstance(after, (list, tuple)) else [after]

    def body(*refs):
        src, land = refs[:n], refs[n:2 * n]
        send_sem, recv_sem = refs[2 * n], refs[2 * n + 1]
        for sent, landed in _xchg_copies(src, land, send_sem, recv_sem, gather, peers):
            sent.wait_send()
            landed.wait_recv()

    thru = h["srcs"] + h["lands"]
    outs = pl.pallas_call(
        body, name=name,
        out_shape=[pltpu.HBM(v.shape, v.dtype) for v in thru],
        in_specs=[HBM] * (2 * n) + [SEM, SEM] + [ANY] * len(after),
        out_specs=[HBM] * (2 * n),
        input_output_aliases={i: i for i in range(2 * n)},
        compiler_params=pltpu.CompilerParams(has_side_effects=DATAFLOW),
    )(*thru, h["send"], h["recv"], *after)
    return list(outs[:n]), list(outs[n:])


def _relay_copies(land, send_sem, recv_sem):
    x, y, c = _my_place()
    me = 4 * x + 2 * y + c
    copies = []
    for k, bits in enumerate(SAME_CORE):
        for a in range(len(land)):
            i = a * len(SAME_CORE) + k
            sent, landed = [pltpu.make_async_remote_copy(
                src_ref=land[a].at[me ^ bits], dst_ref=land[a].at[slab], send_sem=send_sem.at[i], recv_sem=recv_sem.at[i],
                device_id=(x, y, 1 - c), device_id_type=MESH) for slab in (me ^ bits, me ^ bits ^ SIBLING)]
            copies.append((sent, landed))
    return copies


def _relay_start(lands, *, name):
    n = len(lands)

    def body(*refs):
        land = refs[:n]
        send_sem, recv_sem = refs[n], refs[n + 1]
        token = refs[-1]
        for sent, _ in _relay_copies(land, send_sem, recv_sem):
            sent.start()
        token[...] = jnp.zeros_like(token)

    sems = pltpu.SemaphoreType.DMA((n * len(SAME_CORE),))
    outs = pl.pallas_call(
        body, name=name,
        out_shape=(sems, sems, *[pltpu.HBM(v.shape, v.dtype) for v in lands], jax.ShapeDtypeStruct((8, LANES), F32)),
        in_specs=[HBM] * n,
        out_specs=(SEM, SEM, *[HBM] * n, pl.BlockSpec(memory_space=pltpu.VMEM)),
        input_output_aliases={i: 2 + i for i in range(n)},
        compiler_params=pltpu.CompilerParams(has_side_effects=DATAFLOW),
    )(*[pltpu.with_memory_space_constraint(v, pltpu.HBM) for v in lands])
    return dict(send=outs[0], recv=outs[1], lands=list(outs[2:2 + n]), token=outs[-1])


def _relay_wait(h, after, *, name):
    n = len(h["lands"])

    def body(*refs):
        land = refs[:n]
        send_sem, recv_sem = refs[n], refs[n + 1]
        for sent, landed in _relay_copies(land, send_sem, recv_sem):
            sent.wait_send()
            landed.wait_recv()

    outs = pl.pallas_call(
        body, name=name,
        out_shape=[pltpu.HBM(v.shape, v.dtype) for v in h["lands"]],
        in_specs=[HBM] * n + [SEM, SEM, ANY],
        out_specs=[HBM] * n,
        input_output_aliases={i: i for i in range(n)},
        compiler_params=pltpu.CompilerParams(has_side_effects=DATAFLOW),
    )(*h["lands"], h["send"], h["recv"], after)
    return list(outs)


def _with_own(land, own, me):
    slot = lax.broadcasted_iota(jnp.int32, land.shape, 0)
    return jnp.where(slot == me, own if own.ndim == land.ndim else own[None], land)


def _w_in_segments(j, cols):
    lo, hi = j * cols, (j + 1) * cols
    segs = []
    for a, b, where, shift in ((0, FG_AT, "main", 0), (FG_AT, FG_AT + N_HEADS, "fg", -FG_AT),
                               (FG_AT + N_HEADS, IN_W, "main", -N_HEADS)):
        s0, s1 = max(lo, a), min(hi, b)
        if s0 < s1:
            segs.append((s0 - lo, s1 - s0, where, s0 + shift))
    return segs


SLAB_W = 1024


def _window(off, width):
    base = min((off // LANES) * LANES, width - SLAB_W)
    return base, off - base


def _w_unpack(land, own, *, tr):
    _, D, cols = land.shape
    assert cols + LANES - 1 <= SLAB_W

    def body(land_ref, own_ref, main_ref, slab):
        x, y, c = _my_place()
        me = 4 * x + 2 * y + c
        lane = lax.broadcasted_iota(jnp.int32, (tr, SLAB_W), 1)
        main_ref[...] = jnp.zeros_like(main_ref)
        slab[...] = jnp.zeros_like(slab)
        for j in range(N_DEV):
            slab[:, :cols] = jnp.where(me == j, own_ref[...], land_ref[j])
            v = slab[...]
            segs = _w_in_segments(j, cols)
            for lo, n, where, dst in segs:
                part = v if len(segs) == 1 else jnp.where((lane >= lo) & (lane < lo + n), v, jnp.zeros_like(v))
                if where == "fg":
                    main_ref[:, MAIN_W:] = pltpu.roll(part, (SLAB_W - lo) % SLAB_W, 1)[:, :LANES]
                else:
                    base, r = _window(dst - lo, MAIN_W)
                    assert 0 <= r and lo + n + r <= SLAB_W
                    main_ref[:, base:base + SLAB_W] += pltpu.roll(part, r, 1)

    return pl.pallas_call(
        body, name="w_unpack", grid=(D // tr,),
        in_specs=[pl.BlockSpec((N_DEV, tr, cols), lambda i: (0, i, 0)), pl.BlockSpec((tr, cols), lambda i: (i, 0))],
        out_specs=pl.BlockSpec((tr, ALL_W), lambda i: (i, 0)),
        out_shape=jax.ShapeDtypeStruct((D, ALL_W), BF16),
        scratch_shapes=[pltpu.VMEM((tr, SLAB_W), BF16)],
        compiler_params=_params(("parallel",)),
    )(land, own)


def _g_pack(dw_main, dw_fg, cols, *, tr):
    D = dw_main.shape[0]

    def body(main_ref, fg_ref, out_ref, slab):
        lane = lax.broadcasted_iota(jnp.int32, (tr, SLAB_W), 1)
        slab[...] = jnp.zeros_like(slab)
        slab[:, :LANES] = fg_ref[...]
        fg = slab[...]
        for j in range(N_DEV):
            v = None
            for lo, n, where, src in _w_in_segments(j, cols):
                if where == "fg":
                    part = pltpu.roll(fg, lo, 1)
                else:
                    base, r = _window(src - lo, MAIN_W)
                    assert 0 <= r and lo + n + r <= SLAB_W
                    part = pltpu.roll(main_ref[:, base:base + SLAB_W], (SLAB_W - r) % SLAB_W, 1)
                v = part if v is None else jnp.where((lane >= lo) & (lane < lo + n), part, v)
            out_ref[j] = v[:, :cols]

    return pl.pallas_call(
        body, name="g_pack", grid=(D // tr,),
        in_specs=[pl.BlockSpec((tr, MAIN_W), lambda i: (i, 0)), pl.BlockSpec((tr, LANES), lambda i: (i, 0))],
        out_specs=pl.BlockSpec((N_DEV, tr, cols), lambda i: (0, i, 0)),
        out_shape=jax.ShapeDtypeStruct((N_DEV, D, cols), BF16),
        scratch_shapes=[pltpu.VMEM((tr, SLAB_W), BF16)],
        compiler_params=_params(("parallel",)),
    )(dw_main, dw_fg)


TQ = 1024
TS = 512
W_TILE_ROWS = 256


def _layer_fwd_proj(xb, w_in):
    qkv = _mm(xb, w_in, out_dtype=BF16, tm=2048, tn=512, tk=D_MODEL, name="mm_qkv", n_cols=QKV_W)
    rest = _mm(xb, w_in, out_dtype=F32, tm=2048, tn=512, tk=D_MODEL, name="mm_rest", n_off=QKV_W, n_cols=REST_W)
    fgp = _mm(xb, w_in, out_dtype=F32, tm=1024, tn=LANES, tk=D_MODEL, name="mm_fg", n_off=MAIN_W, n_cols=LANES)
    return qkv, rest, fgp


def _layer_fwd_mix(x, xbt, proj, wl):
    qkv, rest, fgp = proj
    cum = _fg_fwd(fgp, wl["b_f"])
    S = x.shape[0]
    cumr = (cum[:, :N_HEADS].T * LOG2E).reshape(N_HEADS, S // TQ, 1, TQ)
    o, y, lse = _attn_fwd(qkv, rest, cum, cumr, tq=TQ)
    y = _conv_fwd(rest, wl["conv_w"], y)
    y = _pool_fwd(rest, wl["pool_w"], wl["pool_scale"], y)
    z = _mm(y, wl["w_out"], out_dtype=F32, tm=2048, tn=512, tk=D_MODEL, name="mm_out", c=x, c_scale=ALPHA)
    out, outb, outbt, xhat, rstd = _ln_fwd(z, wl["ln_g"], wl["ln_b"], ts=TS)
    saved = dict(xbt=xbt, qkv=qkv, rest=rest, fgp=fgp, cum=cum, cumr=cumr, o=o, lse=lse, y=y, xhat=xhat, rstd=rstd)
    return out, outb, outbt, saved


def _layer_bwd_mix(dout, sv, wl):
    S = dout.shape[0]
    dz, dzb, dln_g, dln_b = _ln_bwd(dout, sv["xhat"], sv["rstd"], wl["ln_g"], ts=TS)
    dy = _mm(dzb, wl["w_out"], out_dtype=F32, tm=2048, tn=512, tk=D_MODEL, name="mm_dy", bt=True)
    dw_out = _mm(sv["y"], dzb, out_dtype=BF16, tm=512, tn=D_MODEL, tk=2048, name="mm_dw_out", at=True)
    rest = sv["rest"]
    do, dproj, delta = _attn_bwd_pre(rest, sv["o"], dy, tq=TQ)
    a2 = sv["cumr"] - sv["lse"] * LOG2E
    dproj, ds_q, ds_k = _attn_bwd(sv["qkv"], do, a2, delta, sv["cum"], dproj, tq=TQ)
    dcum = jnp.pad(ds_q.reshape(N_HEADS, S).T, ((0, 0), (0, LANES - N_HEADS))) - ds_k
    dproj, db_f = _fg_bwd(dcum, sv["fgp"], wl["b_f"], dproj)
    dproj, dconv_w = _conv_bwd(rest, dy, wl["conv_w"], dproj)
    dproj, dpool_w, dpool_scale = _pool_bwd(rest, dy, wl["pool_w"], wl["pool_scale"], dproj)
    grads = dict(b_f=db_f[:, :N_HEADS], conv_w=dconv_w, pool_w=dpool_w, pool_scale=dpool_scale, w_out=dw_out,
                 ln_g=dln_g, ln_b=dln_b)
    return grads, (dproj, dz)


def _layer_bwd_w_in(mid, sv):
    dproj, dz = mid
    xt = sv["xbt"]
    S = xt.shape[1]
    dw_main = _mm(xt, dproj, out_dtype=BF16, tm=1024, tn=1024, tk=S, name="mm_dw_main", n_cols=MAIN_W)
    dw_fg = _mm(xt, dproj, out_dtype=BF16, tm=1024, tn=LANES, tk=2048, name="mm_dw_fg", n_off=MAIN_W, n_cols=LANES)
    return (dw_main, dw_fg), (dproj, dz)


def _layer_bwd_input(ctx, wl, after):
    dproj, dz = ctx
    return _mm(dproj, wl["w_in"], out_dtype=F32, tm=1024, tn=256, tk=ALL_W, name="mm_dx", c=dz, c_scale=ALPHA, bt=True,
               after=after)


def kernel(x, w_in, b_f, conv_w, pool_w, pool_scale, w_out, ln_g, ln_b, loss_target, m_w_in, m_b_f, m_conv_w, m_pool_w, m_pool_scale, m_w_out, m_ln_g, m_ln_b, v_w_in, v_b_f, v_conv_w, v_pool_w, v_pool_scale, v_w_out, v_ln_g, v_ln_b):
    L, D, cols = w_in.shape
    rows_out = w_out.shape[1]
    ccols = conv_w.shape[2]
    mx, my, mc = _my_place()
    me = 4 * mx + 2 * my + mc

    gathers = []
    for l in range(L):
        gathers.append(_xchg_start([w_in[l].astype(BF16), w_out[l].astype(BF16), conv_w[l]], gather=[True] * 3,
                                   name=f"gather_start_{l}",
                                   peers=(SIBLING,) + SAME_CORE if l == 0 else EVERYONE,
                                   after=gathers[-1]["token"] if gathers else None))
    started = gathers[-1]["token"]

    xl = x[0]
    xb, xbt = _prep(xl, started, ts=TS)
    weights, saved = [], []
    for l in range(L):
        (own_in, own_out, own_cw), (g_in, g_out, g_cw) = _xchg_wait(gathers[l], xb, name=f"gather_wait_{l}")
        if l == 0:
            g_in, g_out, g_cw = _relay_wait(_relay_start([g_in, g_out, g_cw], name="gather_relay_start"), xb,
                                            name="gather_relay_wait")
        w_all = _w_unpack(g_in, own_in, tr=W_TILE_ROWS)
        proj = _layer_fwd_proj(xb, w_all)
        w_o = _with_own(g_out, own_out, me).reshape(N_DEV * rows_out, D)
        cw = jnp.transpose(_with_own(g_cw, own_cw, me), (1, 0, 2)).reshape(CONV_TAPS, N_DEV * ccols)
        wl = dict(w_in=w_all, w_out=w_o, conv_w=cw,
                  b_f=jnp.pad(b_f[l][None, :], ((0, 0), (0, LANES - N_HEADS))), pool_w=pool_w[l],
                  pool_scale=pool_scale[l][None, :], ln_g=ln_g[l][None, :], ln_b=ln_b[l][None, :])
        xl, xb, xbt, sv = _layer_fwd_mix(xl, xbt, proj, wl)
        weights.append(wl)
        saved.append(sv)

    loss, dx = _loss_head(xl, loss_target[0], ts=TS)
    loss = lax.psum(loss[0, 0], ("x", "y", "c"))

    names = ["w_in", "w_out", "conv_w", "b_f", "pool_w", "pool_scale", "ln_g", "ln_b"]
    sharded = names[:3]
    scatters = [None] * L
    for l in reversed(range(L)):
        g, mid = _layer_bwd_mix(dx, saved[l], weights[l])
        (dw_main, dw_fg), ctx = _layer_bwd_w_in(mid, saved[l])
        g["w_in"] = _g_pack(dw_main, dw_fg, cols, tr=W_TILE_ROWS)
        g["w_out"] = g["w_out"].reshape(N_DEV, rows_out, D)
        g["conv_w"] = jnp.transpose(g["conv_w"].reshape(CONV_TAPS, N_DEV, ccols), (1, 0, 2))
        scatters[l] = _xchg_start([g[k] for k in names], gather=[k not in sharded for k in names],
                                  name=f"scatter_start_{l}")
        dx = _layer_bwd_input(ctx, weights[l], scatters[l]["token"])

    given = dict(w_in=(w_in, m_w_in, v_w_in), b_f=(b_f, m_b_f, v_b_f), conv_w=(conv_w, m_conv_w, v_conv_w),
                 pool_w=(pool_w, m_pool_w, v_pool_w), pool_scale=(pool_scale, m_pool_scale, v_pool_scale),
                 w_out=(w_out, m_w_out, v_w_out), ln_g=(ln_g, m_ln_g, v_ln_g), ln_b=(ln_b, m_ln_b, v_ln_b))
    sent, landed = [{} for _ in range(L)], [{} for _ in range(L)]

    def wait(l, after):
        s_, l_ = _xchg_wait(scatters[l], after, name=f"scatter_wait_{l}")
        sent[l].update(zip(names, s_))
        landed[l].update(zip(names, l_))

    last_start = scatters[0]["token"][0, 0]

    def update(k, layers, into=None):
        w, m, v = given[k]
        C = w.shape[-1]
        if k == "w_in":
            m, v = m + last_start, v + last_start
        w2, m2, v2 = (t.reshape(-1, C) for t in (w, m, v))
        rp = w2.shape[0] // L
        lands = [landed[l][k].reshape(N_DEV, rp, C) for l in layers]
        srcs = [sent[l][k].reshape((N_DEV, rp, C) if k in sharded else (rp, C)) for l in layers]
        return _adamw(lands, srcs, w2, m2, v2, name=f"adamw_{k}_{layers[0]}", first=layers[0], into=into)

    big = ["w_in", "w_out", "pool_w"]
    for l in reversed(range(1, L)):
        wait(l, dx)
    upper = {k: update(k, list(range(1, L))) for k in big} if L > 1 else {}
    wait(0, [upper[k][0] for k in big] if upper else dx)
    res = {k: update(k, [0], into=upper[k]) if k in upper else update(k, list(range(L))) for k in names}
    res = {k: [t.reshape(given[k][0].shape) for t in outs] for k, outs in res.items()}

    order = ["w_in", "b_f", "conv_w", "pool_w", "pool_scale", "w_out", "ln_g", "ln_b"]
    return (loss, dx[None], *[res[k][0] for k in order], *[res[k][1] for k in order],
            *[res[k][2] for k in order], *[res[k][3] for k in order])
```
